```python
import math
import jax
import jax.numpy as jnp
from jax import lax
import numpy as np

D_MODEL = 1024
BATCH = 8
SEQ = 2048
DEPTH = 1

CHUNK = 64
N_META = 16
D_MIX = 2 * D_MODEL
SB_WIDTH = D_MIX // 2
SB_HEAD_DIM = 64
SB_HEADS = SB_WIDTH // SB_HEAD_DIM
SB_QBLOCK = 128
SSD_WIDTH = D_MIX - SB_WIDTH
SSD_HEAD_DIM = 64
SSD_HEADS = SSD_WIDTH // SSD_HEAD_DIM
SSD_GROUPS = 2
SSD_HEADS_PER_GROUP = SSD_HEADS // SSD_GROUPS
SSD_STATE = 128
SSD_CONV = 4
SSD_XBC = SSD_WIDTH + 2 * SSD_GROUPS * SSD_STATE
IN_WIDTHS = (SB_WIDTH, SB_WIDTH, SB_WIDTH, SB_WIDTH, SSD_WIDTH, SSD_XBC, SSD_HEADS)
D_IN = sum(IN_WIDTHS)
EPS = 1e-5
DT_MIN = 1e-3
DT_MAX = 1e-1

kernel_name = "hybrid_stickbreak_ssd_block"


def rms_norm(x, w):
    xf = x.astype(jnp.float32)
    y = xf * lax.rsqrt(jnp.mean(xf * xf, axis=-1, keepdims=True) + EPS)
    return (y * w.astype(jnp.float32)).astype(x.dtype)


def split_cols(proj):
    outs = []
    start = 0
    for width in IN_WIDTHS:
        outs.append(proj[..., start:start + width])
        start += width
    return outs


def causal_depthwise_conv(x, w, b):
    c = x.shape[-1]
    y = lax.conv_general_dilated(
        x, w[:, None, :].astype(x.dtype), window_strides=(1,),
        padding=[(SSD_CONV - 1, 0)], dimension_numbers=("NWC", "WIO", "NWC"),
        feature_group_count=c)
    return y + b.astype(x.dtype)


def stick_breaking_attention(q, k, v):
    seqlen = q.shape[1]
    scale = 1.0 / math.sqrt(q.shape[-1])
    outs = []
    for q0 in range(0, seqlen, SB_QBLOCK):
        q1 = min(q0 + SB_QBLOCK, seqlen)
        qb = q[:, q0:q1]
        kb = k[:, :q1]
        vb = v[:, :q1]
        z = jnp.einsum("bqhd,bkhd->bhqk", qb, kb).astype(jnp.float32) * scale
        t_pos = jnp.arange(q0, q1)[:, None]
        s_pos = jnp.arange(q1)[None, :]
        strict = s_pos < t_pos
        log_keep = jnp.where(strict, jax.nn.log_sigmoid(-z), 0.0)
        after = lax.cumsum(log_keep, axis=3, reverse=True) - log_keep
        weights = jnp.where(strict, jnp.exp(jax.nn.log_sigmoid(z) + after), 0.0)
        outs.append(jnp.einsum("bhqk,bkhd->bqhd", weights.astype(v.dtype), vb))
    return jnp.concatenate(outs, axis=1)


def ssd_chunked(xs, dt, a, bmat, cmat, d_skip):
    f32 = jnp.float32
    bsz, seqlen, _ = xs.shape
    pad = (-seqlen) % CHUNK
    G, R, P, N = SSD_GROUPS, SSD_HEADS_PER_GROUP, SSD_HEAD_DIM, SSD_STATE
    x = xs.astype(f32).reshape(bsz, seqlen, SSD_HEADS, P)
    bm = bmat.astype(f32).reshape(bsz, seqlen, G, N)
    cm = cmat.astype(f32).reshape(bsz, seqlen, G, N)

    def front(t):
        return jnp.pad(t, ((0, 0), (pad, 0)) + ((0, 0),) * (t.ndim - 2))

    x, dt, bm, cm = front(x), front(dt), front(bm), front(cm)
    lp = seqlen + pad
    nc = lp // CHUNK
    x = x.reshape(bsz, nc, CHUNK, G, R, P)
    dt = dt.reshape(bsz, nc, CHUNK, G, R)
    bm = bm.reshape(bsz, nc, CHUNK, G, N)
    cm = cm.reshape(bsz, nc, CHUNK, G, N)

    xdt = x * dt[..., None]
    a_cum = jnp.cumsum(dt * a.reshape(G, R), axis=2)
    seg = a_cum[:, :, :, None] - a_cum[:, :, None]
    causal = jnp.tril(jnp.ones((CHUNK, CHUNK), bool))[:, :, None, None]
    decay_ls = jnp.exp(jnp.where(causal, seg, -jnp.inf))
    cb = jnp.einsum("bclgn,bcsgn->bclsg", cm, bm)
    y_diag = jnp.einsum("bclsg,bclsgr,bcsgrp->bclgrp", cb, decay_ls, xdt)
    decay_to_end = jnp.exp(a_cum[:, :, -1:] - a_cum)
    states = jnp.einsum("bclgn,bclgr,bclgrp->bcgrpn", bm, decay_to_end, xdt)
    chunk_decay = jnp.exp(a_cum[:, :, -1])

    def step(h, inp):
        st, dec = inp
        return h * dec[..., None, None] + st, h

    h0 = jnp.zeros((bsz, G, R, P, N), f32)
    _, h_start = lax.scan(step, h0, (jnp.moveaxis(states, 1, 0), jnp.moveaxis(chunk_decay, 1, 0)))
    h_start = jnp.moveaxis(h_start, 0, 1)
    y_off = jnp.einsum("bclgn,bcgrpn,bclgr->bclgrp", cm, h_start, jnp.exp(a_cum))
    y = y_diag + y_off + x * d_skip.astype(f32).reshape(G, R)[:, :, None]
    y = y.reshape(bsz, lp, SSD_WIDTH)[:, pad:]
    return y.astype(xs.dtype)


def hybrid_layer(h, norm_w, w_in, conv_w, conv_b, dt_bias, a_log, d_skip, sb_norm_w, ssd_norm_w, w_out):
    bsz, seqlen, _ = h.shape
    u = rms_norm(h, norm_w)
    proj = jnp.einsum("bld,de->ble", u, w_in)
    q, k, v, sb_gate, ssd_z, xbc, dt_raw = split_cols(proj)
    hs = (bsz, seqlen, SB_HEADS, SB_HEAD_DIM)
    o_sb = stick_breaking_attention(q.reshape(hs), k.reshape(hs), v.reshape(hs))
    o_sb = o_sb.reshape(bsz, seqlen, SB_WIDTH)
    y_sb = rms_norm(o_sb * jax.nn.silu(sb_gate), sb_norm_w)
    xbc = jax.nn.silu(causal_depthwise_conv(xbc, conv_w, conv_b))
    xs = xbc[..., :SSD_WIDTH]
    bmat = xbc[..., SSD_WIDTH:SSD_WIDTH + SSD_GROUPS * SSD_STATE]
    cmat = xbc[..., SSD_WIDTH + SSD_GROUPS * SSD_STATE:]
    dt = jax.nn.softplus(dt_raw.astype(jnp.float32) + dt_bias.astype(jnp.float32))
    a = -jnp.exp(a_log.astype(jnp.float32))
    o_ssd = ssd_chunked(xs, dt, a, bmat, cmat, d_skip)
    y_ssd = rms_norm(o_ssd * jax.nn.silu(ssd_z), ssd_norm_w)
    y = jnp.concatenate([y_sb, y_ssd], axis=-1)
    return h + jnp.einsum("ble,ed->bld", y, w_out)


def _fwd_setup_inputs(seed: int = 0) -> dict:
    key = jax.random.key(seed)
    ks = jax.random.split(key, 13)
    nrm = jax.random.normal
    x = nrm(ks[0], (BATCH, SEQ, D_MODEL), jnp.float32)
    meta_tokens = nrm(ks[1], (N_META, D_MODEL), jnp.float32)
    norm_w = 1.0 + 0.02 * nrm(ks[2], (DEPTH, D_MODEL), jnp.float32)
    w_in = nrm(ks[3], (DEPTH, D_MODEL, D_IN), jnp.float32) * D_MODEL ** -0.5
    conv_w = nrm(ks[4], (DEPTH, SSD_CONV, SSD_XBC), jnp.float32) * SSD_CONV ** -0.5
    conv_b = 0.02 * nrm(ks[5], (DEPTH, SSD_XBC), jnp.float32)
    dt0 = jnp.exp(jax.random.uniform(ks[6], (DEPTH, SSD_HEADS), jnp.float32,
                                     minval=math.log(DT_MIN), maxval=math.log(DT_MAX)))
    dt_bias = dt0 + jnp.log(-jnp.expm1(-dt0))
    a_log = jnp.log(jax.random.uniform(ks[7], (DEPTH, SSD_HEADS), jnp.float32, minval=1.0, maxval=16.0))
    d_skip = 1.0 + 0.1 * nrm(ks[8], (DEPTH, SSD_HEADS), jnp.float32)
    sb_norm_w = 1.0 + 0.02 * nrm(ks[9], (DEPTH, SB_WIDTH), jnp.float32)
    ssd_norm_w = 1.0 + 0.02 * nrm(ks[10], (DEPTH, SSD_WIDTH), jnp.float32)
    w_out = nrm(ks[11], (DEPTH, D_MIX, D_MODEL), jnp.float32) * D_MIX ** -0.5
    final_norm_w = 1.0 + 0.02 * nrm(ks[12], (D_MODEL,), jnp.float32)
    return {"x": x, "meta_tokens": meta_tokens, "norm_w": norm_w, "w_in": w_in,
            "conv_w": conv_w, "conv_b": conv_b, "dt_bias": dt_bias, "a_log": a_log,
            "d_skip": d_skip, "sb_norm_w": sb_norm_w, "ssd_norm_w": ssd_norm_w,
            "w_out": w_out, "final_norm_w": final_norm_w}


def _fwd_reference(x, meta_tokens, norm_w, w_in, conv_w, conv_b, dt_bias, a_log, d_skip,
              sb_norm_w, ssd_norm_w, w_out, final_norm_w):
    bsz = x.shape[0]
    meta = jnp.broadcast_to(meta_tokens.astype(x.dtype)[None], (bsz, N_META, D_MODEL))
    h = jnp.concatenate([meta, x], axis=1)
    for layer in range(DEPTH):
        h = hybrid_layer(h, norm_w[layer], w_in[layer], conv_w[layer], conv_b[layer],
                         dt_bias[layer], a_log[layer], d_skip[layer], sb_norm_w[layer],
                         ssd_norm_w[layer], w_out[layer])
    h = rms_norm(h, final_norm_w)
    return h[:, N_META:]


import jax as _jax
import jax.numpy as _jnp

TWIN_FORMAT = 'train_step'
FWD_PARAMS = ['x', 'meta_tokens', 'norm_w', 'w_in', 'conv_w', 'conv_b', 'dt_bias', 'a_log', 'd_skip', 'sb_norm_w', 'ssd_norm_w', 'w_out', 'final_norm_w']
TWIN_WEIGHTS = ['meta_tokens', 'norm_w', 'w_in', 'conv_w', 'conv_b', 'dt_bias', 'a_log', 'd_skip', 'sb_norm_w', 'ssd_norm_w', 'w_out', 'final_norm_w']
TWIN_DIFF_INPUT = 'x'
TWIN_INPUTS = ['x', 'meta_tokens', 'norm_w', 'w_in', 'conv_w', 'conv_b', 'dt_bias', 'a_log', 'd_skip', 'sb_norm_w', 'ssd_norm_w', 'w_out', 'final_norm_w', 'loss_target', 'm_meta_tokens', 'm_norm_w', 'm_w_in', 'm_conv_w', 'm_conv_b', 'm_dt_bias', 'm_a_log', 'm_d_skip', 'm_sb_norm_w', 'm_ssd_norm_w', 'm_w_out', 'm_final_norm_w', 'v_meta_tokens', 'v_norm_w', 'v_w_in', 'v_conv_w', 'v_conv_b', 'v_dt_bias', 'v_a_log', 'v_d_skip', 'v_sb_norm_w', 'v_ssd_norm_w', 'v_w_out', 'v_final_norm_w']
TWIN_OUTPUTS = ['loss', 'grad_x', 'grad_meta_tokens', 'grad_norm_w', 'grad_w_in', 'grad_conv_w', 'grad_conv_b', 'grad_dt_bias', 'grad_a_log', 'grad_d_skip', 'grad_sb_norm_w', 'grad_ssd_norm_w', 'grad_w_out', 'grad_final_norm_w', 'delta_meta_tokens', 'delta_norm_w', 'delta_w_in', 'delta_conv_w', 'delta_conv_b', 'delta_dt_bias', 'delta_a_log', 'delta_d_skip', 'delta_sb_norm_w', 'delta_ssd_norm_w', 'delta_w_out', 'delta_final_norm_w', 'new_m_meta_tokens', 'new_m_norm_w', 'new_m_w_in', 'new_m_conv_w', 'new_m_conv_b', 'new_m_dt_bias', 'new_m_a_log', 'new_m_d_skip', 'new_m_sb_norm_w', 'new_m_ssd_norm_w', 'new_m_w_out', 'new_m_final_norm_w', 'new_v_meta_tokens', 'new_v_norm_w', 'new_v_w_in', 'new_v_conv_w', 'new_v_conv_b', 'new_v_dt_bias', 'new_v_a_log', 'new_v_d_skip', 'new_v_sb_norm_w', 'new_v_ssd_norm_w', 'new_v_w_out', 'new_v_final_norm_w']
TWIN_LEAF_KINDS = {'loss': 'loss', 'grad_x': 'grad_x', 'grad_meta_tokens': 'grad_w', 'grad_norm_w': 'grad_w', 'grad_w_in': 'grad_w', 'grad_conv_w': 'grad_w', 'grad_conv_b': 'grad_w', 'grad_dt_bias': 'grad_w', 'grad_a_log': 'grad_w', 'grad_d_skip': 'grad_w', 'grad_sb_norm_w': 'grad_w', 'grad_ssd_norm_w': 'grad_w', 'grad_w_out': 'grad_w', 'grad_final_norm_w': 'grad_w', 'delta_meta_tokens': 'delta_w', 'delta_norm_w': 'delta_w', 'delta_w_in': 'delta_w', 'delta_conv_w': 'delta_w', 'delta_conv_b': 'delta_w', 'delta_dt_bias': 'delta_w', 'delta_a_log': 'delta_w', 'delta_d_skip': 'delta_w', 'delta_sb_norm_w': 'delta_w', 'delta_ssd_norm_w': 'delta_w', 'delta_w_out': 'delta_w', 'delta_final_norm_w': 'delta_w', 'new_m_meta_tokens': 'new_m', 'new_m_norm_w': 'new_m', 'new_m_w_in': 'new_m', 'new_m_conv_w': 'new_m', 'new_m_conv_b': 'new_m', 'new_m_dt_bias': 'new_m', 'new_m_a_log': 'new_m', 'new_m_d_skip': 'new_m', 'new_m_sb_norm_w': 'new_m', 'new_m_ssd_norm_w': 'new_m', 'new_m_w_out': 'new_m', 'new_m_final_norm_w': 'new_m', 'new_v_meta_tokens': 'new_v', 'new_v_norm_w': 'new_v', 'new_v_w_in': 'new_v', 'new_v_conv_w': 'new_v', 'new_v_conv_b': 'new_v', 'new_v_dt_bias': 'new_v', 'new_v_a_log': 'new_v', 'new_v_d_skip': 'new_v', 'new_v_sb_norm_w': 'new_v', 'new_v_ssd_norm_w': 'new_v', 'new_v_w_out': 'new_v', 'new_v_final_norm_w': 'new_v'}


def _forward(args):
    return _fwd_reference(*[args[k] for k in FWD_PARAMS])


def _output_shape():
    out = _jax.eval_shape(lambda: _forward(_fwd_setup_inputs(0)))
    return out.shape, out.dtype

N_MICROBATCH = 1
ADAM_LR = 0.001
ADAM_B1 = 0.9
ADAM_B2 = 0.999
ADAM_EPS = 1e-08
ADAM_WD = 0.01
ADAM_STEP = 10
PER_EXAMPLE_BATCH_AXIS = {'x': 0, 'loss_target': 0}
SHARED_INPUTS = []
_WEIGHT_DTYPES = {'meta_tokens': _jnp.float32, 'norm_w': _jnp.float32, 'w_in': _jnp.float32, 'conv_w': _jnp.float32, 'conv_b': _jnp.float32, 'dt_bias': _jnp.float32, 'a_log': _jnp.float32, 'd_skip': _jnp.float32, 'sb_norm_w': _jnp.float32, 'ssd_norm_w': _jnp.float32, 'w_out': _jnp.float32, 'final_norm_w': _jnp.float32}
MOMENT_SCALE = {'meta_tokens': 2.156721e-03, 'norm_w': 1.371148e-01, 'w_in': 5.434007e-02, 'conv_w': 5.457849e-02, 'conv_b': 7.453415e-02, 'dt_bias': 2.415887e-01, 'a_log': 1.765726e-01, 'd_skip': 3.620159e-01, 'sb_norm_w': 6.386721e-02, 'ssd_norm_w': 6.191723e-02, 'w_out': 8.900140e-02, 'final_norm_w': 1.601016e+01}


def _to_microbatches(a, axis):
    t = _jnp.moveaxis(a, axis, 0)
    t = t.reshape((N_MICROBATCH, t.shape[0] // N_MICROBATCH) + t.shape[1:])
    return _jnp.moveaxis(t, 1, axis + 1)


def setup_inputs(seed: int = 0) -> dict:
    inp = _fwd_setup_inputs(seed)
    key = _jax.random.fold_in(_jax.random.key(seed), 7919)
    shape, _ = _output_shape()
    out = dict(inp)
    out["loss_target"] = _jax.random.normal(_jax.random.fold_in(key, 0), shape, _jnp.float32)
    for i, name in enumerate(TWIN_WEIGHTS):
        w = inp[name].astype(_jnp.float32)
        if MOMENT_SCALE is None:
            s = _jnp.sqrt(_jnp.mean(_jnp.square(w)) + 1e-30)
        else:
            s = MOMENT_SCALE[name]
        km, kv = _jax.random.split(_jax.random.fold_in(key, i + 1))
        out[name] = w
        out["m_" + name] = s * _jax.random.normal(km, w.shape, _jnp.float32)
        out["v_" + name] = (s * s) * _jax.random.uniform(kv, w.shape, _jnp.float32, 0.5, 1.5)
    if N_MICROBATCH > 1:
        for name, axis in PER_EXAMPLE_BATCH_AXIS.items():
            out[name] = _to_microbatches(out[name], axis)
    return {'x': out['x'], 'meta_tokens': out['meta_tokens'], 'norm_w': out['norm_w'], 'w_in': out['w_in'], 'conv_w': out['conv_w'], 'conv_b': out['conv_b'], 'dt_bias': out['dt_bias'], 'a_log': out['a_log'], 'd_skip': out['d_skip'], 'sb_norm_w': out['sb_norm_w'], 'ssd_norm_w': out['ssd_norm_w'], 'w_out': out['w_out'], 'final_norm_w': out['final_norm_w'], 'loss_target': out['loss_target'], 'm_meta_tokens': out['m_meta_tokens'], 'm_norm_w': out['m_norm_w'], 'm_w_in': out['m_w_in'], 'm_conv_w': out['m_conv_w'], 'm_conv_b': out['m_conv_b'], 'm_dt_bias': out['m_dt_bias'], 'm_a_log': out['m_a_log'], 'm_d_skip': out['m_d_skip'], 'm_sb_norm_w': out['m_sb_norm_w'], 'm_ssd_norm_w': out['m_ssd_norm_w'], 'm_w_out': out['m_w_out'], 'm_final_norm_w': out['m_final_norm_w'], 'v_meta_tokens': out['v_meta_tokens'], 'v_norm_w': out['v_norm_w'], 'v_w_in': out['v_w_in'], 'v_conv_w': out['v_conv_w'], 'v_conv_b': out['v_conv_b'], 'v_dt_bias': out['v_dt_bias'], 'v_a_log': out['v_a_log'], 'v_d_skip': out['v_d_skip'], 'v_sb_norm_w': out['v_sb_norm_w'], 'v_ssd_norm_w': out['v_ssd_norm_w'], 'v_w_out': out['v_w_out'], 'v_final_norm_w': out['v_final_norm_w']}


def _loss(weights, diff, rest, loss_target):
    with _jax.named_scope("forward"):
        args = {**rest, TWIN_DIFF_INPUT: diff, **{k: w.astype(_WEIGHT_DTYPES[k]) for k, w in weights.items()}}
        y = _forward(args)
    with _jax.named_scope("loss_head"):
        err = _jnp.square(y.astype(_jnp.float32) - loss_target)
        return 0.5 * _jnp.sum(_jnp.mean(err, axis=-1)) if err.ndim else 0.5 * err


def _adamw(w, g, m, v):
    m = ADAM_B1 * m + (1.0 - ADAM_B1) * g
    v = ADAM_B2 * v + (1.0 - ADAM_B2) * _jnp.square(g)
    m_hat = m / (1.0 - ADAM_B1 ** ADAM_STEP)
    v_hat = v / (1.0 - ADAM_B2 ** ADAM_STEP)
    delta = -ADAM_LR * (m_hat / (_jnp.sqrt(v_hat) + ADAM_EPS) + ADAM_WD * w)
    return delta, m, v


def reference(x, meta_tokens, norm_w, w_in, conv_w, conv_b, dt_bias, a_log, d_skip, sb_norm_w, ssd_norm_w, w_out, final_norm_w, loss_target, m_meta_tokens, m_norm_w, m_w_in, m_conv_w, m_conv_b, m_dt_bias, m_a_log, m_d_skip, m_sb_norm_w, m_ssd_norm_w, m_w_out, m_final_norm_w, v_meta_tokens, v_norm_w, v_w_in, v_conv_w, v_conv_b, v_dt_bias, v_a_log, v_d_skip, v_sb_norm_w, v_ssd_norm_w, v_w_out, v_final_norm_w):
    given = dict(x=x, meta_tokens=meta_tokens, norm_w=norm_w, w_in=w_in, conv_w=conv_w, conv_b=conv_b, dt_bias=dt_bias, a_log=a_log, d_skip=d_skip, sb_norm_w=sb_norm_w, ssd_norm_w=ssd_norm_w, w_out=w_out, final_norm_w=final_norm_w, loss_target=loss_target, m_meta_tokens=m_meta_tokens, m_norm_w=m_norm_w, m_w_in=m_w_in, m_conv_w=m_conv_w, m_conv_b=m_conv_b, m_dt_bias=m_dt_bias, m_a_log=m_a_log, m_d_skip=m_d_skip, m_sb_norm_w=m_sb_norm_w, m_ssd_norm_w=m_ssd_norm_w, m_w_out=m_w_out, m_final_norm_w=m_final_norm_w, v_meta_tokens=v_meta_tokens, v_norm_w=v_norm_w, v_w_in=v_w_in, v_conv_w=v_conv_w, v_conv_b=v_conv_b, v_dt_bias=v_dt_bias, v_a_log=v_a_log, v_d_skip=v_d_skip, v_sb_norm_w=v_sb_norm_w, v_ssd_norm_w=v_ssd_norm_w, v_w_out=v_w_out, v_final_norm_w=v_final_norm_w)
    weights = {n: given[n] for n in TWIN_WEIGHTS}
    shared = {n: given[n] for n in SHARED_INPUTS}
    per_example = {n: given[n] for n in ['x']}
    grad_fn = _jax.value_and_grad(_loss, argnums=(0, 1))

    def one_microbatch(ex, loss_target):
        ex = dict(ex)
        diff = ex.pop(TWIN_DIFF_INPUT)
        return grad_fn(weights, diff, {**shared, **ex}, loss_target)

    if N_MICROBATCH == 1:
        loss, (grad_w, grad_x) = one_microbatch(per_example, given["loss_target"])
    else:
        def body(carry, xs):
            loss_sum, grad_sum = carry
            l_k, (gw_k, gx_k) = one_microbatch(xs[0], xs[1])
            with _jax.named_scope("update"):
                return (loss_sum + l_k, _jax.tree.map(_jnp.add, grad_sum, gw_k)), gx_k

        init = (_jnp.zeros((), _jnp.float32), _jax.tree.map(_jnp.zeros_like, weights))
        (loss, grad_w), grad_x = _jax.lax.scan(body, init, (per_example, given["loss_target"]))
    with _jax.named_scope("update"):
        delta_w, new_m, new_v = {}, {}, {}
        for n in TWIN_WEIGHTS:
            delta_w[n], new_m[n], new_v[n] = _adamw(weights[n], grad_w[n], given["m_" + n], given["v_" + n])
    return (loss, grad_x, *[grad_w[n] for n in TWIN_WEIGHTS], *[delta_w[n] for n in TWIN_WEIGHTS],
            *[new_m[n] for n in TWIN_WEIGHTS], *[new_v[n] for n in TWIN_WEIGHTS])
```

```python
import functools
import math

import jax
import jax.numpy as jnp
from jax import lax
from jax.experimental import pallas as pl
from jax.experimental.pallas import tpu as pltpu

F32 = jnp.float32
_MXU = jnp.bfloat16

D_MODEL = 1024
N_META = 16
PAD = 112
OFF = PAD + N_META
TM = 128
CHUNK = 64
SB_W = 1024
SSD_W = 1024
N_HEADS = 16
HEAD = 64
N_GROUPS = 2
N_STATE = 128
XBC_W = SSD_W + 2 * N_GROUPS * N_STATE
N_MAIN = 4 * SB_W + SSD_W + XBC_W
COL_GATE = 3 * SB_W
COL_Z = 4 * SB_W
COL_XBC = 5 * SB_W
D_IN = N_MAIN + N_HEADS
EPS = 1e-5
N_CHIPS = 4
W_IN_SHARD = D_IN // N_CHIPS
W_OUT_SHARD = 2 * D_MODEL // N_CHIPS

ADAM_LR = 0.001
ADAM_B1 = 0.9
ADAM_B2 = 0.999
ADAM_EPS = 1e-08
ADAM_WD = 0.01
ADAM_STEP = 10

_SDS = jax.ShapeDtypeStruct
_NT = (((1,), (1,)), ((), ()))
_TN = (((0,), (0,)), ((), ()))
_VMEM = pltpu.VMEM


def _params(sem=None, vmem_mb=None):
    kw = {}
    if sem is not None:
        kw["dimension_semantics"] = sem
    if vmem_mb is not None:
        kw["vmem_limit_bytes"] = vmem_mb * 1024 * 1024
    return pltpu.CompilerParams(**kw)


def _mm(a, b):
    return jnp.dot(a.astype(_MXU), b.astype(_MXU), preferred_element_type=F32)


def _mm_nt(a, b):
    return lax.dot_general(a.astype(_MXU), b.astype(_MXU), _NT, preferred_element_type=F32)


def _mm_tn(a, b):
    return lax.dot_general(a.astype(_MXU), b.astype(_MXU), _TN, preferred_element_type=F32)


def _split(x, parts):
    out = []
    r = x
    for _ in range(parts):
        p = r.astype(_MXU)
        out.append(p)
        r = r - p.astype(F32)
    return out


def _sel_right(x, m01, parts=3):
    acc = None
    for p in _split(x, parts):
        t = jnp.dot(p, m01, preferred_element_type=F32)
        acc = t if acc is None else acc + t
    return acc


def _sel_left(m01, x, parts=3):
    acc = None
    for p in _split(x, parts):
        t = jnp.dot(m01, p, preferred_element_type=F32)
        acc = t if acc is None else acc + t
    return acc


def _iota(shape, axis):
    return lax.broadcasted_iota(jnp.int32, shape, axis)


def _sigmoid(x):
    return 1.0 / (1.0 + jnp.exp(-x))


def _prep(x2d, meta_full, norm_w):
    seq = x2d.shape[0]
    lp = seq + OFF
    nb = lp // TM

    def body(x_ref, meta_ref, w_ref, h0_ref, u_ref, ut_ref):
        i = pl.program_id(0)

        @pl.when(i == 0)
        def _():
            h0_ref[...] = jnp.concatenate([jnp.zeros((PAD, D_MODEL), F32), meta_ref[...]], axis=0)

        @pl.when(i > 0)
        def _():
            h0_ref[...] = x_ref[...]

        h = h0_ref[...]
        rs = lax.rsqrt(jnp.mean(h * h, axis=-1, keepdims=True) + EPS)
        u = (h * rs * w_ref[...]).astype(_MXU)
        u_ref[...] = u
        ut_ref[...] = u.T

    return pl.pallas_call(
        body, name="prep", grid=(nb,),
        in_specs=[pl.BlockSpec((TM, D_MODEL), lambda i: (jnp.maximum(i - 1, 0), 0)),
                  pl.BlockSpec((N_META, D_MODEL), lambda i: (0, 0)),
                  pl.BlockSpec((1, D_MODEL), lambda i: (0, 0))],
        out_specs=[pl.BlockSpec((TM, D_MODEL), lambda i: (i, 0)),
                   pl.BlockSpec((TM, D_MODEL), lambda i: (i, 0)),
                   pl.BlockSpec((D_MODEL, TM), lambda i: (0, i))],
        out_shape=[_SDS((lp, D_MODEL), F32), _SDS((lp, D_MODEL), _MXU), _SDS((D_MODEL, lp), _MXU)],
        compiler_params=_params(("arbitrary",)),
    )(x2d, meta_full, norm_w)


def _inproj(u, w_main, w_dt):
    lp = u.shape[0]
    tn = 512

    def body(u_ref, w_ref, wdt_ref, o_ref, odt_ref):
        o_ref[...] = jnp.dot(u_ref[...], w_ref[...], preferred_element_type=F32)

        @pl.when(pl.program_id(0) == 0)
        def _():
            odt_ref[...] = jnp.dot(u_ref[...], wdt_ref[...], preferred_element_type=F32)

    return pl.pallas_call(
        body, name="inproj", grid=(N_MAIN // tn,),
        in_specs=[pl.BlockSpec((lp, D_MODEL), lambda j: (0, 0)),
                  pl.BlockSpec((D_MODEL, tn), lambda j: (0, j)),
                  pl.BlockSpec((D_MODEL, 128), lambda j: (0, 0))],
        out_specs=[pl.BlockSpec((lp, tn), lambda j: (0, j)),
                   pl.BlockSpec((lp, 128), lambda j: (0, 0))],
        out_shape=[_SDS((lp, N_MAIN), F32), _SDS((lp, 128), F32)],
        compiler_params=_params(("arbitrary",), 48),
    )(u, w_main, w_dt)


def _sb_scores(qh, kblk, mask):
    z = lax.dot_general(qh, kblk, _NT, preferred_element_type=F32)
    e = jnp.exp(-jnp.abs(z))
    l1p = jnp.log(1.0 + e)
    lk_full = -(jnp.maximum(z, 0.0) + l1p)
    ls = jnp.minimum(z, 0.0) - l1p
    lk = jnp.where(mask, lk_full, 0.0)
    return z, e, ls, lk


def _sb_fwd(proj):
    lp = proj.shape[0]
    nb = lp // TM

    def body(q_ref, k_ref, v_ref, o_ref, olo_ref):
        i = pl.program_id(1)
        lane = _iota((TM, TM), 1)
        row = _iota((TM, TM), 0)
        head0 = lane < HEAD
        upper = (row > lane).astype(_MXU)
        strict = lane < row
        q = q_ref[...] * (1.0 / math.sqrt(HEAD))
        qh = (jnp.where(head0, q, 0.0).astype(_MXU), jnp.where(head0, 0.0, q).astype(_MXU))

        def step(t, carry):
            kb = i - t
            off = pl.multiple_of(kb * TM, TM)
            kblk = k_ref[pl.ds(off, TM), :].astype(_MXU)
            vblk = v_ref[pl.ds(off, TM), :].astype(_MXU)
            mask = jnp.logical_or(strict, t > 0)
            new = []
            for hh in range(2):
                acc, acc_lo, run = carry[hh]
                _, _, ls, lk = _sb_scores(qh[hh], kblk, mask)
                aft = _sel_right(lk, upper, 2)
                w = jnp.where(mask, jnp.exp(ls + aft + run), 0.0)
                w_hi, w_lo = _split(w, 2)
                acc = acc + jnp.dot(w_hi, vblk, preferred_element_type=F32)
                acc_lo = acc_lo + jnp.dot(w_lo, vblk, preferred_element_type=F32)
                run = run + jnp.sum(lk, axis=1, keepdims=True)
                new.append((acc, acc_lo, run))
            return tuple(new)

        zt = jnp.zeros((TM, TM), F32)
        init = tuple((zt, zt, jnp.zeros((TM, 1), F32)) for _ in range(2))
        res = lax.fori_loop(0, i + 1, step, init)
        o_ref[...] = jnp.where(head0, res[0][0], res[1][0])
        olo_ref[...] = jnp.where(head0, res[0][1], res[1][1])

    npair = SB_W // TM
    blk = pl.BlockSpec((TM, TM), lambda p, i: (i, p))
    return pl.pallas_call(
        body, name="sb_fwd", grid=(npair, nb),
        in_specs=[blk,
                  pl.BlockSpec((lp, TM), lambda p, i: (0, npair + p)),
                  pl.BlockSpec((lp, TM), lambda p, i: (0, 2 * npair + p))],
        out_specs=[blk, blk],
        out_shape=[_SDS((lp, SB_W), F32), _SDS((lp, SB_W), F32)],
        compiler_params=_params(("arbitrary", "arbitrary")),
    )(proj, proj, proj)


def _sb_bwd(proj, o_sb, o_lo, d_o):
    lp = proj.shape[0]
    nb = lp // TM
    scale = 1.0 / math.sqrt(HEAD)

    def body(q_ref, k_ref, v_ref, o_ref, olo_ref, do_ref, dq_ref, dk_ref, dv_ref):
        i = pl.program_id(1)

        @pl.when(i == 0)
        def _():
            dk_ref[...] = jnp.zeros_like(dk_ref)
            dv_ref[...] = jnp.zeros_like(dv_ref)

        lane = _iota((TM, TM), 1)
        row = _iota((TM, TM), 0)
        head0 = lane < HEAD
        hmask = (head0, jnp.logical_not(head0))
        upper = (row > lane).astype(_MXU)
        lower_incl = (row >= lane).astype(_MXU)
        strict = lane < row
        q = q_ref[...] * scale
        do = do_ref[...]
        prod = do.astype(_MXU).astype(F32) * (o_ref[...] + olo_ref[...])
        qh = tuple(jnp.where(m, q, 0.0).astype(_MXU) for m in hmask)
        doh = tuple(jnp.where(m, do, 0.0).astype(_MXU) for m in hmask)
        gtot = tuple(jnp.sum(jnp.where(m, prod, 0.0), axis=1, keepdims=True) for m in hmask)

        def step(t, carry):
            kb = i - t
            off = pl.multiple_of(kb * TM, TM)
            kf = k_ref[pl.ds(off, TM), :]
            kblk = kf.astype(_MXU)
            vblk = v_ref[pl.ds(off, TM), :].astype(_MXU)
            mask = jnp.logical_or(strict, t > 0)
            dq_acc = carry[0]
            new = []
            dk_blk = jnp.zeros((TM, TM), F32)
            dv_blk = jnp.zeros((TM, TM), F32)
            for hh in range(2):
                run, gsum = carry[1 + hh]
                z, e, ls, lk = _sb_scores(qh[hh], kblk, mask)
                aft = _sel_right(lk, upper, 2)
                w = jnp.where(mask, jnp.exp(ls + aft + run), 0.0)
                r = 1.0 / (1.0 + e)
                er = e * r
                pos = z >= 0.0
                beta = jnp.where(pos, r, er)
                one_m_beta = jnp.where(pos, er, r)
                dw = lax.dot_general(doh[hh], vblk, _NT, preferred_element_type=F32)
                g = dw * w
                suffix = _sel_right(g, lower_incl, 2)
                prefix = gtot[hh] - gsum - suffix
                dz = jnp.where(mask, g * one_m_beta - beta * prefix, 0.0).astype(_MXU)
                kh = jnp.where(hmask[hh], kf, 0.0).astype(_MXU)
                dq_acc = dq_acc + jnp.dot(dz, kh, preferred_element_type=F32)
                dk_blk = dk_blk + lax.dot_general(dz, qh[hh], _TN, preferred_element_type=F32)
                dv_blk = dv_blk + lax.dot_general(w.astype(_MXU), doh[hh], _TN, preferred_element_type=F32)
                run = run + jnp.sum(lk, axis=1, keepdims=True)
                gsum = gsum + jnp.sum(g, axis=1, keepdims=True)
                new.append((run, gsum))
            dk_ref[pl.ds(off, TM), :] += dk_blk
            dv_ref[pl.ds(off, TM), :] += dv_blk
            return (dq_acc,) + tuple(new)

        z1 = jnp.zeros((TM, 1), F32)
        init = (jnp.zeros((TM, TM), F32), (z1, z1), (z1, z1))
        res = lax.fori_loop(0, i + 1, step, init)
        dq_ref[...] = res[0] * scale

    npair = SB_W // TM
    blk = pl.BlockSpec((TM, TM), lambda p, i: (i, p))
    col = pl.BlockSpec((lp, TM), lambda p, i: (0, p))
    return pl.pallas_call(
        body, name="sb_bwd", grid=(npair, nb),
        in_specs=[blk,
                  pl.BlockSpec((lp, TM), lambda p, i: (0, npair + p)),
                  pl.BlockSpec((lp, TM), lambda p, i: (0, 2 * npair + p)),
                  blk, blk, blk],
        out_specs=[blk, col, col],
        out_shape=[_SDS((lp, SB_W), F32)] * 3,
        compiler_params=_params(("arbitrary", "arbitrary")),
    )(proj, proj, proj, o_sb, o_lo, d_o)


def _conv_pre(x_ref, w_ref, b_ref, lp):
    n = lp - 8
    w = w_ref[...]
    pre = (x_ref[pl.ds(5, n), :] * w[0:1, :] + x_ref[pl.ds(6, n), :] * w[1:2, :]
           + x_ref[pl.ds(7, n), :] * w[2:3, :] + x_ref[pl.ds(8, n), :] * w[3:4, :]) + b_ref[...]
    live = (_iota((n, 128), 0) + 8) >= PAD
    return pre, live


def _conv_fwd(proj, dt_raw, conv_w, conv_b, dt_bias128):
    lp = proj.shape[0]
    nblk = XBC_W // 128
    c0 = COL_XBC // 128

    def body(x_ref, w_ref, b_ref, dtr_ref, dtb_ref, o_ref, dt_ref):
        pre, live = _conv_pre(x_ref, w_ref, b_ref, lp)
        act = pre * _sigmoid(pre)
        o_ref[pl.ds(0, 8), :] = jnp.zeros((8, 128), F32)
        o_ref[pl.ds(8, lp - 8), :] = jnp.where(live, act, 0.0)

        @pl.when(pl.program_id(0) == 0)
        def _():
            s = dtr_ref[...] + dtb_ref[...]
            sp = jnp.maximum(s, 0.0) + jnp.log(1.0 + jnp.exp(-jnp.abs(s)))
            dt_ref[...] = jnp.where(_iota((lp, 128), 0) >= PAD, sp, 0.0)

    return pl.pallas_call(
        body, name="conv_fwd", grid=(nblk,),
        in_specs=[pl.BlockSpec((lp, 128), lambda j: (0, c0 + j)),
                  pl.BlockSpec((4, 128), lambda j: (0, j)),
                  pl.BlockSpec((1, 128), lambda j: (0, j)),
                  pl.BlockSpec((lp, 128), lambda j: (0, 0)),
                  pl.BlockSpec((1, 128), lambda j: (0, 0))],
        out_specs=[pl.BlockSpec((lp, 128), lambda j: (0, j)),
                   pl.BlockSpec((lp, 128), lambda j: (0, 0))],
        out_shape=[_SDS((lp, XBC_W), F32), _SDS((lp, 128), F32)],
        compiler_params=_params(("arbitrary",)),
    )(proj, conv_w, conv_b, dt_raw, dt_bias128)


def _conv_bwd(proj, dt_raw, conv_w, conv_b, dt_bias128, d_xbc, d_dt128):
    lp = proj.shape[0]
    nblk = XBC_W // 128
    c0 = COL_XBC // 128
    n = lp - 8

    def body(x_ref, w_ref, b_ref, dtr_ref, dtb_ref, dy_ref, ddt_ref,
             dx_ref, gw_ref, gb_ref, ddtr_ref, gdtb_ref, scr):
        pre, live = _conv_pre(x_ref, w_ref, b_ref, lp)
        sg = _sigmoid(pre)
        dpre = jnp.where(live, dy_ref[pl.ds(8, n), :] * (sg * (1.0 + pre * (1.0 - sg))), 0.0)
        gb_ref[...] = jnp.sum(dpre, axis=0, keepdims=True)
        gw_ref[...] = jnp.concatenate(
            [jnp.sum(dpre * x_ref[pl.ds(5 + k, n), :], axis=0, keepdims=True) for k in range(4)], axis=0)
        scr[pl.ds(0, 8), :] = jnp.zeros((8, 128), F32)
        scr[pl.ds(8, n), :] = dpre
        scr[pl.ds(lp, 8), :] = jnp.zeros((8, 128), F32)
        w = w_ref[...]
        dx_ref[pl.ds(0, 8), :] = jnp.zeros((8, 128), F32)
        dx_ref[pl.ds(8, n), :] = (scr[pl.ds(8, n), :] * w[3:4, :] + scr[pl.ds(9, n), :] * w[2:3, :]
                                  + scr[pl.ds(10, n), :] * w[1:2, :] + scr[pl.ds(11, n), :] * w[0:1, :])

        @pl.when(pl.program_id(0) == 0)
        def _():
            s = dtr_ref[...] + dtb_ref[...]
            d = jnp.where(_iota((lp, 128), 0) >= PAD, ddt_ref[...] * _sigmoid(s), 0.0)
            ddtr_ref[...] = d
            gdtb_ref[...] = jnp.sum(d, axis=0, keepdims=True)

    colblk = pl.BlockSpec((lp, 128), lambda j: (0, j))
    full128 = pl.BlockSpec((lp, 128), lambda j: (0, 0))
    return pl.pallas_call(
        body, name="conv_bwd", grid=(nblk,),
        in_specs=[pl.BlockSpec((lp, 128), lambda j: (0, c0 + j)),
                  pl.BlockSpec((4, 128), lambda j: (0, j)),
                  pl.BlockSpec((1, 128), lambda j: (0, j)),
                  full128, pl.BlockSpec((1, 128), lambda j: (0, 0)),
                  colblk, full128],
        out_specs=[colblk, pl.BlockSpec((4, 128), lambda j: (0, j)), pl.BlockSpec((1, 128), lambda j: (0, j)),
                   full128, pl.BlockSpec((1, 128), lambda j: (0, 0))],
        out_shape=[_SDS((lp, XBC_W), F32), _SDS((4, XBC_W), F32), _SDS((1, XBC_W), F32),
                   _SDS((lp, 128), F32), _SDS((1, 128), F32)],
        scratch_shapes=[pltpu.VMEM((lp + 8, 128), F32)],
        compiler_params=_params(("arbitrary",)),
    )(proj, conv_w, conv_b, dt_raw, dt_bias128, d_xbc, d_dt128)


def _ssd_pieces(dt, dt_t, a, a_t):
    r64 = _iota((CHUNK, CHUNK), 0)
    c64 = _iota((CHUNK, CHUNK), 1)
    tril = c64 <= r64
    tril01 = tril.astype(_MXU)
    triu01 = (r64 <= c64).astype(_MXU)
    expand = (lax.shift_right_logical(_iota((N_HEADS, SSD_W), 1), 6) == _iota((N_HEADS, SSD_W), 0)).astype(_MXU)
    acum = _sel_left(tril01, dt * a)
    acum_t = _sel_right(dt_t * a_t, triu01)
    ax = _sel_right(acum, expand)
    dtx = _sel_right(dt, expand)
    return tril, expand, acum, acum_t, ax, dtx


def _seg_matrix():
    return (lax.shift_right_logical(_iota((SSD_W, N_HEADS), 0), 6) == _iota((SSD_W, N_HEADS), 1)).astype(_MXU)


def _head_decay(ax, acum_t, h, tril):
    col = ax[:, h * HEAD:(h + 1) * HEAD]
    rowv = acum_t[h:h + 1, :]
    return jnp.where(tril, jnp.exp(jnp.minimum(col - rowv, 0.0)), 0.0)


def _ssd_fwd(xbc, dt_c, dt_tc, a, a_t, dskip_x):
    lp = xbc.shape[0]
    nc = lp // CHUNK
    gw = SSD_W // N_GROUPS
    hpg = N_HEADS // N_GROUPS

    def body(x_ref, dt_ref, dtt_ref, a_ref, at_ref, d_ref, y_ref, st_ref, state):
        c = pl.program_id(0)

        @pl.when(c == 0)
        def _():
            state[...] = jnp.zeros_like(state)

        st_ref[0] = state[...]
        tril, _, _, acum_t, ax, dtx = _ssd_pieces(dt_ref[0], dtt_ref[0], a_ref[...], at_ref[...])
        x = x_ref[:, 0:SSD_W]
        xdt = x * dtx
        ea = jnp.exp(ax)
        aex = ax[CHUNK - 1:CHUNK, :]
        wd = jnp.exp(aex - ax)
        eae = jnp.exp(aex)
        xw = xdt * wd
        y_ref[...] = x * d_ref[...]
        for g in range(N_GROUPS):
            gs = slice(g * gw, (g + 1) * gw)
            rs = slice(g * N_STATE, (g + 1) * N_STATE)
            bg = x_ref[:, SSD_W + g * N_STATE:SSD_W + (g + 1) * N_STATE]
            cg = x_ref[:, SSD_W + N_GROUPS * N_STATE + g * N_STATE:SSD_W + N_GROUPS * N_STATE + (g + 1) * N_STATE]
            sg = state[rs, :]
            cb = _mm_nt(cg, bg)
            y_ref[:, gs] += _mm(cg, sg) * ea[:, gs]
            for r in range(hpg):
                h = g * hpg + r
                hs = slice(h * HEAD, (h + 1) * HEAD)
                m = cb * _head_decay(ax, acum_t, h, tril)
                y_ref[:, hs] += _mm(m, xdt[:, hs])
            state[rs, :] = sg * eae[:, gs] + _mm_tn(bg, xw[:, gs])

    return pl.pallas_call(
        body, name="ssd_fwd", grid=(nc,),
        in_specs=[pl.BlockSpec((CHUNK, XBC_W), lambda c: (c, 0)),
                  pl.BlockSpec((1, CHUNK, N_HEADS), lambda c: (c, 0, 0)),
                  pl.BlockSpec((1, N_HEADS, CHUNK), lambda c: (c, 0, 0)),
                  pl.BlockSpec((1, N_HEADS), lambda c: (0, 0)),
                  pl.BlockSpec((N_HEADS, 1), lambda c: (0, 0)),
                  pl.BlockSpec((1, SSD_W), lambda c: (0, 0))],
        out_specs=[pl.BlockSpec((CHUNK, SSD_W), lambda c: (c, 0)),
                   pl.BlockSpec((1, N_GROUPS * N_STATE, gw), lambda c: (c, 0, 0))],
        out_shape=[_SDS((lp, SSD_W), F32), _SDS((nc, N_GROUPS * N_STATE, gw), F32)],
        scratch_shapes=[pltpu.VMEM((N_GROUPS * N_STATE, gw), F32)],
        compiler_params=_params(("arbitrary",)),
    )(xbc, dt_c, dt_tc, a, a_t, dskip_x)


def _ssd_bwd(xbc, dt_c, dt_tc, a, a_t, dskip_x, states, d_y):
    lp = xbc.shape[0]
    nc = lp // CHUNK
    gw = SSD_W // N_GROUPS
    hpg = N_HEADS // N_GROUPS

    def body(x_ref, dt_ref, dtt_ref, a_ref, at_ref, d_ref, st_ref, dy_ref,
             dx_ref, ddta_ref, ddtb_ref, ga1_ref, ga2_ref, gd_ref, dstate, dxdt_scr, z_scr, yoff_scr, sds_scr):
        c = pl.program_id(0)

        @pl.when(c == 0)
        def _():
            dstate[...] = jnp.zeros_like(dstate)
            ga1_ref[...] = jnp.zeros_like(ga1_ref)
            ga2_ref[...] = jnp.zeros_like(ga2_ref)
            gd_ref[...] = jnp.zeros_like(gd_ref)

        dt = dt_ref[0]
        dt_t = dtt_ref[0]
        a = a_ref[...]
        a_t = at_ref[...]
        tril, _, acum, acum_t, ax, dtx = _ssd_pieces(dt, dt_t, a, a_t)
        seg = _seg_matrix()
        x = x_ref[:, 0:SSD_W]
        dy = dy_ref[...]
        xdt = x * dtx
        ea = jnp.exp(ax)
        aex = ax[CHUNK - 1:CHUNK, :]
        wd = jnp.exp(aex - ax)
        eae = jnp.exp(aex)
        xw = xdt * wd
        edy = ea * dy
        lane16 = _iota((CHUNK, N_HEADS), 1)
        row16 = _iota((N_HEADS, CHUNK), 0)
        da_col = jnp.zeros((CHUNK, N_HEADS), F32)
        da_row = jnp.zeros((N_HEADS, CHUNK), F32)
        for g in range(N_GROUPS):
            gs = slice(g * gw, (g + 1) * gw)
            rs = slice(g * N_STATE, (g + 1) * N_STATE)
            bcol = slice(SSD_W + g * N_STATE, SSD_W + (g + 1) * N_STATE)
            ccol = slice(SSD_W + N_GROUPS * N_STATE + g * N_STATE, SSD_W + N_GROUPS * N_STATE + (g + 1) * N_STATE)
            bg = x_ref[:, bcol]
            cg = x_ref[:, ccol]
            sg = st_ref[0, rs, :]
            dsn = dstate[rs, :]
            cb = _mm_nt(cg, bg)
            z_scr[:, gs] = _mm(bg, dsn)
            yoff_scr[:, gs] = _mm(cg, sg) * ea[:, gs]
            sds_scr[:, gs] = jnp.broadcast_to(jnp.sum(dsn * sg, axis=0, keepdims=True), (8, gw))
            dcb = jnp.zeros((CHUNK, CHUNK), F32)
            for r in range(hpg):
                h = g * hpg + r
                hs = slice(h * HEAD, (h + 1) * HEAD)
                dec = _head_decay(ax, acum_t, h, tril)
                m = cb * dec
                t1 = _mm_nt(dy[:, hs], xdt[:, hs])
                dcb = dcb + dec * t1
                tm = m * t1
                da_col = da_col + jnp.where(lane16 == h, jnp.sum(tm, axis=1, keepdims=True), 0.0)
                da_row = da_row - jnp.where(row16 == h, jnp.sum(tm, axis=0, keepdims=True), 0.0)
                dxdt_scr[:, hs] = _mm_tn(m, dy[:, hs])
            dx_ref[:, ccol] = _mm(dcb, bg) + _mm_nt(edy[:, gs], sg)
            dx_ref[:, bcol] = _mm_tn(dcb, cg) + _mm_nt(xw[:, gs], dsn)
            dstate[rs, :] = eae[:, gs] * dsn + _mm_tn(cg, edy[:, gs])
        zf = z_scr[...]
        dxdt = dxdt_scr[...] + wd * zf
        t3 = _sel_right(xw * zf, seg)
        da_col = da_col + _sel_right(dy * yoff_scr[...], seg) - t3
        aend = acum[CHUNK - 1:CHUNK, :]
        sd = _sel_right(sds_scr[...], seg)[0:1, :] * jnp.exp(aend)
        last = jnp.sum(t3, axis=0, keepdims=True) + sd
        da_col = da_col + jnp.where(_iota((CHUNK, N_HEADS), 0) == CHUNK - 1, last, 0.0)
        r64 = _iota((CHUNK, CHUNK), 0)
        c64 = _iota((CHUNK, CHUNK), 1)
        ddta1 = _sel_left((c64 >= r64).astype(_MXU), da_col)
        ddta2 = _sel_right(da_row, (r64 >= c64).astype(_MXU))
        ddta_ref[0] = a * ddta1 + _sel_right(dxdt * x, seg)
        ddtb_ref[0] = a_t * ddta2
        ga1_ref[...] += jnp.sum(dt * ddta1, axis=0, keepdims=True)
        ga2_ref[...] += jnp.sum(dt_t * ddta2, axis=1, keepdims=True)
        dx_ref[:, 0:SSD_W] = dxdt * dtx + d_ref[...] * dy
        gd_ref[...] += jnp.sum(dy * x, axis=0, keepdims=True)

    rev = lambda c: (nc - 1 - c, 0)
    rev3 = lambda c: (nc - 1 - c, 0, 0)
    return pl.pallas_call(
        body, name="ssd_bwd", grid=(nc,),
        in_specs=[pl.BlockSpec((CHUNK, XBC_W), rev),
                  pl.BlockSpec((1, CHUNK, N_HEADS), rev3),
                  pl.BlockSpec((1, N_HEADS, CHUNK), rev3),
                  pl.BlockSpec((1, N_HEADS), lambda c: (0, 0)),
                  pl.BlockSpec((N_HEADS, 1), lambda c: (0, 0)),
                  pl.BlockSpec((1, SSD_W), lambda c: (0, 0)),
                  pl.BlockSpec((1, N_GROUPS * N_STATE, gw), rev3),
                  pl.BlockSpec((CHUNK, SSD_W), rev)],
        out_specs=[pl.BlockSpec((CHUNK, XBC_W), rev),
                   pl.BlockSpec((1, CHUNK, N_HEADS), rev3),
                   pl.BlockSpec((1, N_HEADS, CHUNK), rev3),
                   pl.BlockSpec((1, N_HEADS), lambda c: (0, 0)),
                   pl.BlockSpec((N_HEADS, 1), lambda c: (0, 0)),
                   pl.BlockSpec((1, SSD_W), lambda c: (0, 0))],
        out_shape=[_SDS((lp, XBC_W), F32), _SDS((nc, CHUNK, N_HEADS), F32), _SDS((nc, N_HEADS, CHUNK), F32),
                   _SDS((1, N_HEADS), F32), _SDS((N_HEADS, 1), F32), _SDS((1, SSD_W), F32)],
        scratch_shapes=[pltpu.VMEM((N_GROUPS * N_STATE, gw), F32), pltpu.VMEM((CHUNK, SSD_W), F32),
                        pltpu.VMEM((CHUNK, SSD_W), F32), pltpu.VMEM((CHUNK, SSD_W), F32),
                        pltpu.VMEM((8, SSD_W), F32)],
        compiler_params=_params(("arbitrary",)),
    )(xbc, dt_c, dt_tc, a, a_t, dskip_x, states, d_y)


def _gated_norm(o, gate, w):
    sg = _sigmoid(gate)
    p = o * (gate * sg)
    rs = lax.rsqrt(jnp.mean(p * p, axis=-1, keepdims=True) + EPS)
    n = p * rs
    return sg, rs, n, n * w


def _tail_fwd(o_sb, o_ssd, proj, h0, target, w_out, sb_w, ssd_w, fin_w):
    lp = o_sb.shape[0]
    nb = lp // TM
    row = lambda i: (i, 0)
    one = lambda i: (0, 0)

    def body(osb_ref, gate_ref, ossd_ref, z_ref, h0_ref, tgt_ref, wo_ref, sbw_ref, ssdw_ref, fw_ref,
             dh1_ref, loss_ref, gfw_ref):
        i = pl.program_id(0)

        @pl.when(i == 0)
        def _():
            loss_ref[...] = jnp.zeros_like(loss_ref)
            gfw_ref[...] = jnp.zeros_like(gfw_ref)

        y1 = _gated_norm(osb_ref[...], gate_ref[...], sbw_ref[...])[3]
        y2 = _gated_norm(ossd_ref[...], z_ref[...], ssdw_ref[...])[3]
        h1 = (h0_ref[...] + _mm(y1, wo_ref[0:SB_W, :])) + _mm(y2, wo_ref[SB_W:SB_W + SSD_W, :])
        rs1 = lax.rsqrt(jnp.mean(h1 * h1, axis=-1, keepdims=True) + EPS)
        n1 = h1 * rs1
        fw = fw_ref[...]
        diff = jnp.where(i > 0, n1 * fw - tgt_ref[...], 0.0)
        loss_ref[...] += jnp.sum(diff * diff, axis=0, keepdims=True)
        d_out = diff * (1.0 / D_MODEL)
        gfw_ref[...] += jnp.sum(d_out * n1, axis=0, keepdims=True)
        g = d_out * fw
        dh1_ref[...] = rs1 * (g - n1 * jnp.mean(g * n1, axis=-1, keepdims=True))

    return pl.pallas_call(
        body, name="tail_fwd", grid=(nb,),
        in_specs=[pl.BlockSpec((TM, SB_W), row),
                  pl.BlockSpec((TM, SB_W), lambda i: (i, COL_GATE // SB_W)),
                  pl.BlockSpec((TM, SSD_W), row),
                  pl.BlockSpec((TM, SSD_W), lambda i: (i, COL_Z // SSD_W)),
                  pl.BlockSpec((TM, D_MODEL), row),
                  pl.BlockSpec((TM, D_MODEL), lambda i: (jnp.maximum(i - 1, 0), 0)),
                  pl.BlockSpec(memory_space=_VMEM),
                  pl.BlockSpec((1, SB_W), one), pl.BlockSpec((1, SSD_W), one), pl.BlockSpec((1, D_MODEL), one)],
        out_specs=[pl.BlockSpec((TM, D_MODEL), row), pl.BlockSpec((1, D_MODEL), one), pl.BlockSpec((1, D_MODEL), one)],
        out_shape=[_SDS((lp, D_MODEL), F32), _SDS((1, D_MODEL), F32), _SDS((1, D_MODEL), F32)],
        compiler_params=_params(("arbitrary",), 40),
    )(o_sb, proj, o_ssd, proj, h0, target, w_out, sb_w, ssd_w, fin_w)


def _gated_norm_bwd(o, gate, w, dy):
    sg, rs, n, _ = _gated_norm(o, gate, w)
    gw = jnp.sum(dy * n, axis=0, keepdims=True)
    dn = dy * w
    dp = rs * (dn - n * jnp.mean(dn * n, axis=-1, keepdims=True))
    d_o = dp * (gate * sg)
    d_gate = dp * o * (sg * (1.0 + gate * (1.0 - sg)))
    return d_o, d_gate, gw, n * w


def _tail_bwd(o_sb, o_ssd, proj, d_h1, w_out, sb_w, ssd_w):
    lp = o_sb.shape[0]
    nb = lp // TM
    row = lambda i: (i, 0)
    one = lambda i: (0, 0)

    def body(osb_ref, gate_ref, ossd_ref, z_ref, dh1_ref, wo_ref, sbw_ref, ssdw_ref,
             dosb_ref, dgate_ref, dossd_ref, dz_ref, gwo_ref, gsb_ref, gssd_ref):
        i = pl.program_id(0)

        @pl.when(i == 0)
        def _():
            gwo_ref[...] = jnp.zeros_like(gwo_ref)
            gsb_ref[...] = jnp.zeros_like(gsb_ref)
            gssd_ref[...] = jnp.zeros_like(gssd_ref)

        dh1 = dh1_ref[...].astype(_MXU)
        dy1 = lax.dot_general(dh1, wo_ref[0:SB_W, :], _NT, preferred_element_type=F32)
        dy2 = lax.dot_general(dh1, wo_ref[SB_W:SB_W + SSD_W, :], _NT, preferred_element_type=F32)
        d_o, d_g, gw, y1 = _gated_norm_bwd(osb_ref[...], gate_ref[...], sbw_ref[...], dy1)
        dosb_ref[...] = d_o
        dgate_ref[...] = d_g
        gsb_ref[...] += gw
        gwo_ref[0:SB_W, :] += lax.dot_general(y1.astype(_MXU), dh1, _TN, preferred_element_type=F32)
        d_o, d_g, gw, y2 = _gated_norm_bwd(ossd_ref[...], z_ref[...], ssdw_ref[...], dy2)
        dossd_ref[...] = d_o
        dz_ref[...] = d_g
        gssd_ref[...] += gw
        gwo_ref[SB_W:SB_W + SSD_W, :] += lax.dot_general(y2.astype(_MXU), dh1, _TN, preferred_element_type=F32)

    tile = pl.BlockSpec((TM, SB_W), row)
    return pl.pallas_call(
        body, name="tail_bwd", grid=(nb,),
        in_specs=[tile, pl.BlockSpec((TM, SB_W), lambda i: (i, COL_GATE // SB_W)),
                  tile, pl.BlockSpec((TM, SSD_W), lambda i: (i, COL_Z // SSD_W)),
                  tile, pl.BlockSpec(memory_space=_VMEM),
                  pl.BlockSpec((1, SB_W), one), pl.BlockSpec((1, SSD_W), one)],
        out_specs=[tile, tile, tile, tile,
                   pl.BlockSpec((SB_W + SSD_W, D_MODEL), one), pl.BlockSpec((1, SB_W), one), pl.BlockSpec((1, SSD_W), one)],
        out_shape=[_SDS((lp, SB_W), F32)] * 4 + [_SDS((SB_W + SSD_W, D_MODEL), F32), _SDS((1, SB_W), F32), _SDS((1, SSD_W), F32)],
        compiler_params=_params(("arbitrary",), 48),
    )(o_sb, proj, o_ssd, proj, d_h1, w_out, sb_w, ssd_w)


def _in_bwd(d_q, d_k, d_v, d_gate, d_z, d_xbc, d_dt128, w_main, w_dt, h0, d_h1, norm_w):
    lp = h0.shape[0]
    nb = lp // TM
    seq = lp - OFF
    row = lambda i: (i, 0)
    one = lambda i: (0, 0)
    secs = ((0, SB_W), (SB_W, SB_W), (2 * SB_W, SB_W), (COL_GATE, SB_W), (COL_Z, SSD_W), (COL_XBC, XBC_W))

    def body(dq_ref, dk_ref, dv_ref, dg_ref, dz_ref, dx_ref, ddt_ref, w_ref, wdt_ref, h0_ref, dh1_ref, nw_ref,
             gx_ref, gmeta_ref, gnw_ref):
        i = pl.program_id(0)

        @pl.when(i == 0)
        def _():
            gnw_ref[...] = jnp.zeros_like(gnw_ref)

        du = lax.dot_general(ddt_ref[...].astype(_MXU), wdt_ref[...], _NT, preferred_element_type=F32)
        for ref, (c0, width) in zip((dq_ref, dk_ref, dv_ref, dg_ref, dz_ref, dx_ref), secs):
            du = du + lax.dot_general(ref[...].astype(_MXU), w_ref[:, c0:c0 + width], _NT, preferred_element_type=F32)
        h = h0_ref[...]
        rs = lax.rsqrt(jnp.mean(h * h, axis=-1, keepdims=True) + EPS)
        n0 = h * rs
        gnw_ref[...] += jnp.sum(du * n0, axis=0, keepdims=True)
        g = du * nw_ref[...]
        dh0 = dh1_ref[...] + rs * (g - n0 * jnp.mean(g * n0, axis=-1, keepdims=True))

        @pl.when(i == 0)
        def _():
            gmeta_ref[...] = dh0[PAD:PAD + N_META, :]

        @pl.when(i > 0)
        def _():
            gx_ref[...] = dh0

    tile = pl.BlockSpec((TM, D_MODEL), row)
    return pl.pallas_call(
        body, name="in_bwd", grid=(nb,),
        in_specs=[tile, tile, tile, tile, tile, pl.BlockSpec((TM, XBC_W), row), pl.BlockSpec((TM, 128), row),
                  pl.BlockSpec(memory_space=_VMEM), pl.BlockSpec(memory_space=_VMEM),
                  tile, tile, pl.BlockSpec((1, D_MODEL), one)],
        out_specs=[pl.BlockSpec((TM, D_MODEL), lambda i: (jnp.maximum(i - 1, 0), 0)),
                   pl.BlockSpec((N_META, D_MODEL), one), pl.BlockSpec((1, D_MODEL), one)],
        out_shape=[_SDS((seq, D_MODEL), F32), _SDS((N_META, D_MODEL), F32), _SDS((1, D_MODEL), F32)],
        compiler_params=_params(("arbitrary",), 48),
    )(d_q, d_k, d_v, d_gate, d_z, d_xbc, d_dt128, w_main, w_dt, h0, d_h1, norm_w)


def _grad_w(u_t, d_sec, name):
    lp, n = d_sec.shape
    tn = min(512, n)

    def body(ut_ref, d_ref, o_ref):
        o_ref[...] = jnp.dot(ut_ref[...], d_ref[...].astype(_MXU), preferred_element_type=F32)

    return pl.pallas_call(
        body, name=name, grid=(n // tn,),
        in_specs=[pl.BlockSpec((D_MODEL, lp), lambda j: (0, 0)), pl.BlockSpec((lp, tn), lambda j: (0, j))],
        out_specs=pl.BlockSpec((D_MODEL, tn), lambda j: (0, j)),
        out_shape=_SDS((D_MODEL, n), F32),
        compiler_params=_params(("arbitrary",), 40),
    )(u_t, d_sec)


def _device_grads(x2d, target2d, meta_full, norm_w, w_main, w_dt, conv_w, conv_b, dt_bias, a_log, d_skip,
                  sb_w, ssd_w, w_out, fin_w):
    lp = x2d.shape[0] + OFF
    nc = lp // CHUNK
    h0, u, u_t = _prep(x2d, meta_full, norm_w)
    proj, dt_raw = _inproj(u, w_main, w_dt)
    o_sb, o_lo = _sb_fwd(proj)
    dt_bias128 = jnp.pad(dt_bias, ((0, 0), (0, 128 - N_HEADS)))
    xbc, dt128 = _conv_fwd(proj, dt_raw, conv_w, conv_b, dt_bias128)
    dt_c = dt128[:, :N_HEADS].reshape(nc, CHUNK, N_HEADS)
    dt_tc = jnp.swapaxes(dt_c, 1, 2)
    a = -jnp.exp(a_log)
    a_t = a.reshape(N_HEADS, 1)
    dskip_x = jnp.repeat(d_skip, HEAD, axis=1)
    o_ssd, states = _ssd_fwd(xbc, dt_c, dt_tc, a, a_t, dskip_x)
    d_h1, sq_err, g_fin = _tail_fwd(o_sb, o_ssd, proj, h0, target2d, w_out, sb_w, ssd_w, fin_w)

    d_osb, d_gate, d_ossd, d_z, g_wout, g_sb, g_ssd = _tail_bwd(o_sb, o_ssd, proj, d_h1, w_out, sb_w, ssd_w)
    d_q, d_k, d_v = _sb_bwd(proj, o_sb, o_lo, d_osb)
    d_xbc_act, ddt_a, ddt_b, ga1, ga2, gd = _ssd_bwd(xbc, dt_c, dt_tc, a, a_t, dskip_x, states, d_ossd)
    d_dt = (ddt_a + jnp.swapaxes(ddt_b, 1, 2)).reshape(lp, N_HEADS)
    d_dt128 = jnp.pad(d_dt, ((0, 0), (0, 128 - N_HEADS)))
    d_xbc, g_convw, g_convb, d_dtraw128, g_dtb128 = _conv_bwd(proj, dt_raw, conv_w, conv_b, dt_bias128, d_xbc_act, d_dt128)
    g_x, g_meta, g_nw = _in_bwd(d_q, d_k, d_v, d_gate, d_z, d_xbc, d_dtraw128, w_main, w_dt, h0, d_h1, norm_w)
    g_win = jnp.concatenate(
        [_grad_w(u_t, d, "gw_" + nm) for nm, d in (("q", d_q), ("k", d_k), ("v", d_v), ("gate", d_gate), ("z", d_z), ("xbc", d_xbc))]
        + [_grad_w(u_t, d_dtraw128, "gw_dt")[:, :N_HEADS]], axis=1)
    g_alog = (ga1 + ga2.reshape(1, N_HEADS)) * a
    g_dskip = gd.reshape(N_HEADS, HEAD).sum(axis=1).reshape(1, N_HEADS)
    grads = dict(meta_tokens=g_meta, norm_w=g_nw, w_in=g_win, conv_w=g_convw, conv_b=g_convb,
                 dt_bias=g_dtb128[:, :N_HEADS], a_log=g_alog, d_skip=g_dskip, sb_norm_w=g_sb, ssd_norm_w=g_ssd,
                 w_out=g_wout, final_norm_w=g_fin)
    return sq_err, g_x, grads


_MESH = pl.DeviceIdType.MESH
_ANY = pl.BlockSpec(memory_space=pl.ANY)


def _place():
    return lax.axis_index("x"), lax.axis_index("y"), lax.axis_index("c")


def _other_chips(x, y):
    return ((1 - x, y), (x, 1 - y), (1 - x, 1 - y))


def _gather_shards(arrays):
    n = len(arrays)

    def body(*refs):
        srcs, dsts = refs[:n], refs[n:2 * n]
        send_sems, recv_sems, local_sems = refs[2 * n:]
        x, y, c = _place()
        mine = 2 * x + y
        local = [pltpu.make_async_copy(srcs[a], dsts[a].at[mine], local_sems.at[a]) for a in range(n)]
        for cp in local:
            cp.start()
        remote = []
        for a in range(n):
            for k, (px, py) in enumerate(_other_chips(x, y)):
                cp = pltpu.make_async_remote_copy(
                    src_ref=srcs[a], dst_ref=dsts[a].at[mine],
                    send_sem=send_sems.at[a * 3 + k], recv_sem=recv_sems.at[a * 3 + k],
                    device_id=(px, py, c), device_id_type=_MESH)
                cp.start()
                remote.append(cp)
        for cp in remote:
            cp.wait_recv()
        for cp in remote:
            cp.wait_send()
        for cp in local:
            cp.wait()

    return pl.pallas_call(
        body, name="gather_shards",
        in_specs=[_ANY] * n, out_specs=[_ANY] * n,
        out_shape=[_SDS((N_CHIPS,) + a.shape, a.dtype) for a in arrays],
        scratch_shapes=[pltpu.SemaphoreType.DMA((3 * n,)), pltpu.SemaphoreType.DMA((3 * n,)),
                        pltpu.SemaphoreType.DMA((n,))],
    )(*arrays)


def _scatter_slabs(arrays):
    n = len(arrays)

    def body(*refs):
        srcs, dsts = refs[:n], refs[n:2 * n]
        send_sems, recv_sems, local_sems = refs[2 * n:]
        x, y, c = _place()
        mine = 2 * x + y
        local = [pltpu.make_async_copy(srcs[a].at[mine], dsts[a].at[mine], local_sems.at[a]) for a in range(n)]
        for cp in local:
            cp.start()
        remote = []
        for a in range(n):
            for k, (px, py) in enumerate(_other_chips(x, y)):
                cp = pltpu.make_async_remote_copy(
                    src_ref=srcs[a].at[2 * px + py], dst_ref=dsts[a].at[mine],
                    send_sem=send_sems.at[a * 3 + k], recv_sem=recv_sems.at[a * 3 + k],
                    device_id=(px, py, c), device_id_type=_MESH)
                cp.start()
                remote.append(cp)
        for cp in remote:
            cp.wait_recv()
        for cp in remote:
            cp.wait_send()
        for cp in local:
            cp.wait()

    return pl.pallas_call(
        body, name="scatter_slabs",
        in_specs=[_ANY] * n, out_specs=[_ANY] * n,
        out_shape=[_SDS(a.shape, a.dtype) for a in arrays],
        scratch_shapes=[pltpu.SemaphoreType.DMA((3 * n,)), pltpu.SemaphoreType.DMA((3 * n,)),
                        pltpu.SemaphoreType.DMA((n,))],
    )(*arrays)


def _swap_sibling(arrays):
    n = len(arrays)

    def body(*refs):
        srcs, dsts = refs[:n], refs[n:2 * n]
        send_sems, recv_sems = refs[2 * n:]
        x, y, c = _place()
        copies = []
        for a in range(n):
            cp = pltpu.make_async_remote_copy(
                src_ref=srcs[a], dst_ref=dsts[a], send_sem=send_sems.at[a], recv_sem=recv_sems.at[a],
                device_id=(x, y, 1 - c), device_id_type=_MESH)
            cp.start()
            copies.append(cp)
        for cp in copies:
            cp.wait_recv()
        for cp in copies:
            cp.wait_send()

    return pl.pallas_call(
        body, name="swap_sibling",
        in_specs=[_ANY] * n, out_specs=[_ANY] * n,
        out_shape=[_SDS(a.shape, a.dtype) for a in arrays],
        scratch_shapes=[pltpu.SemaphoreType.DMA((n,)), pltpu.SemaphoreType.DMA((n,))],
    )(*arrays)


N_DEV = 8
SMALL_ROWS = 32
SMALL_COLS = XBC_W


def _gather_small(packed):
    def body(src_ref, dst_ref, send_sems, recv_sems, local_sem):
        x, y, c = _place()
        me = 4 * x + 2 * y + c
        own = pltpu.make_async_copy(src_ref, dst_ref.at[me], local_sem)
        own.start()
        copies = []
        for k in range(1, N_DEV):
            bx, by, bc = (k >> 2) & 1, (k >> 1) & 1, k & 1
            peer = (x + bx - 2 * x * bx, y + by - 2 * y * by, c + bc - 2 * c * bc)
            cp = pltpu.make_async_remote_copy(
                src_ref=src_ref, dst_ref=dst_ref.at[me], send_sem=send_sems.at[k - 1], recv_sem=recv_sems.at[k - 1],
                device_id=peer, device_id_type=_MESH)
            cp.start()
            copies.append(cp)
        for cp in copies:
            cp.wait_recv()
        for cp in copies:
            cp.wait_send()
        own.wait()

    return pl.pallas_call(
        body, name="gather_small",
        in_specs=[pl.BlockSpec(memory_space=_VMEM)], out_specs=pl.BlockSpec(memory_space=_VMEM),
        out_shape=_SDS((N_DEV, SMALL_ROWS, SMALL_COLS), F32),
        scratch_shapes=[pltpu.SemaphoreType.DMA((N_DEV - 1,)), pltpu.SemaphoreType.DMA((N_DEV - 1,)),
                        pltpu.SemaphoreType.DMA],
    )(packed)


def _adamw(w, g, m, v):
    m = ADAM_B1 * m + (1.0 - ADAM_B1) * g
    v = ADAM_B2 * v + (1.0 - ADAM_B2) * (g * g)
    m_hat = m / (1.0 - ADAM_B1 ** ADAM_STEP)
    v_hat = v / (1.0 - ADAM_B2 ** ADAM_STEP)
    delta = -ADAM_LR * (m_hat / (jnp.sqrt(v_hat) + ADAM_EPS) + ADAM_WD * w)
    return delta, m, v


def _sum_slabs(slabs, name):
    _, r, c = slabs.shape
    tr = 128

    def body(s_ref, o_ref):
        o_ref[...] = ((s_ref[0].astype(F32) + s_ref[1].astype(F32)) + s_ref[2].astype(F32)) + s_ref[3].astype(F32)

    return pl.pallas_call(
        body, name=name, grid=(r // tr,),
        in_specs=[pl.BlockSpec((N_CHIPS, tr, c), lambda i: (0, i, 0))],
        out_specs=pl.BlockSpec((tr, c), lambda i: (i, 0)),
        out_shape=_SDS((r, c), F32),
        compiler_params=_params(("arbitrary",)),
    )(slabs)


def _update_big(w, m, v, s_own, s_sib, name):
    r, c = w.shape
    tr = 128

    def body(w_ref, m_ref, v_ref, a_ref, b_ref, g_ref, d_ref, mo_ref, vo_ref):
        g = a_ref[...] + b_ref[...]
        delta, m_new, v_new = _adamw(w_ref[...], g, m_ref[...], v_ref[...])
        g_ref[...] = g
        d_ref[...] = delta
        mo_ref[...] = m_new
        vo_ref[...] = v_new

    spec = pl.BlockSpec((tr, c), lambda i: (i, 0))
    return pl.pallas_call(
        body, name=name, grid=(r // tr,),
        in_specs=[spec] * 5, out_specs=[spec] * 4,
        out_shape=[_SDS((r, c), F32)] * 4,
        compiler_params=_params(("arbitrary",)),
    )(w, m, v, s_own, s_sib)


_ROW = dict(norm_w=0, sb_norm_w=1, ssd_norm_w=2, final_norm_w=3, conv_b=4, dt_bias=5, a_log=6, d_skip=7,
            conv_w=8, sq_err=12, meta_tokens=16)
_SMALL = ("meta_tokens", "norm_w", "conv_w", "conv_b", "dt_bias", "a_log", "d_skip", "sb_norm_w", "ssd_norm_w",
          "final_norm_w")


def _pack_small(sq_err, grads):
    def rowpad(a):
        return jnp.pad(a, ((0, 0), (0, SMALL_COLS - a.shape[1])))

    rows = [rowpad(grads[k]) for k in ("norm_w", "sb_norm_w", "ssd_norm_w", "final_norm_w", "conv_b", "dt_bias", "a_log", "d_skip")]
    rows.append(grads["conv_w"])
    rows.append(rowpad(sq_err))
    rows.append(jnp.zeros((3, SMALL_COLS), F32))
    rows.append(rowpad(grads["meta_tokens"]))
    return jnp.concatenate(rows, axis=0)


def _update_small(gathered, ws, ms, vs):
    names = _SMALL
    n = len(names)

    def body(*refs):
        g_ref = refs[0]
        w_refs, m_refs, v_refs = refs[1:1 + n], refs[1 + n:1 + 2 * n], refs[1 + 2 * n:1 + 3 * n]
        outs = refs[1 + 3 * n:]
        loss_ref = outs[0]
        go, do, mo, vo = outs[1:1 + n], outs[1 + n:1 + 2 * n], outs[1 + 2 * n:1 + 3 * n], outs[1 + 3 * n:1 + 4 * n]
        tot = g_ref[0]
        for d in range(1, N_DEV):
            tot = tot + g_ref[d]
        x, y, _ = _place()
        chip = 2 * x + y
        loss_ref[...] = jnp.broadcast_to(
            0.5 * jnp.sum(tot[_ROW["sq_err"]:_ROW["sq_err"] + 1, 0:D_MODEL], axis=1, keepdims=True) / D_MODEL, (1, 128))
        for idx, nm in enumerate(names):
            r0 = _ROW[nm]
            rows, cols = w_refs[idx].shape
            if nm in ("conv_w", "meta_tokens"):
                g = jnp.zeros((rows, cols), F32)
                for j in range(N_CHIPS):
                    g = g + jnp.where(chip == j, tot[r0:r0 + rows, j * cols:(j + 1) * cols], 0.0)
            else:
                g = tot[r0:r0 + rows, 0:cols]
            delta, m_new, v_new = _adamw(w_refs[idx][...], g, m_refs[idx][...], v_refs[idx][...])
            go[idx][...] = g
            do[idx][...] = delta
            mo[idx][...] = m_new
            vo[idx][...] = v_new

    shapes = [_SDS(ws[nm].shape, F32) for nm in names]
    vm = pl.BlockSpec(memory_space=_VMEM)
    res = pl.pallas_call(
        body, name="update_small",
        in_specs=[vm] * (1 + 3 * n), out_specs=[vm] * (1 + 4 * n),
        out_shape=[_SDS((1, 128), F32)] + shapes * 4,
    )(gathered, *[ws[nm] for nm in names], *[ms[nm] for nm in names], *[vs[nm] for nm in names])
    loss = res[0][0, 0]
    g = dict(zip(names, res[1:1 + n]))
    d = dict(zip(names, res[1 + n:1 + 2 * n]))
    m = dict(zip(names, res[1 + 2 * n:1 + 3 * n]))
    v = dict(zip(names, res[1 + 3 * n:1 + 4 * n]))
    return loss, g, d, m, v


_WEIGHTS = ("meta_tokens", "norm_w", "w_in", "conv_w", "conv_b", "dt_bias", "a_log", "d_skip", "sb_norm_w",
            "ssd_norm_w", "w_out", "final_norm_w")


def kernel(x, meta_tokens, norm_w, w_in, conv_w, conv_b, dt_bias, a_log, d_skip, sb_norm_w, ssd_norm_w, w_out, final_norm_w, loss_target, m_meta_tokens, m_norm_w, m_w_in, m_conv_w, m_conv_b, m_dt_bias, m_a_log, m_d_skip, m_sb_norm_w, m_ssd_norm_w, m_w_out, m_final_norm_w, v_meta_tokens, v_norm_w, v_w_in, v_conv_w, v_conv_b, v_dt_bias, v_a_log, v_d_skip, v_sb_norm_w, v_ssd_norm_w, v_w_out, v_final_norm_w):
    given = dict(meta_tokens=meta_tokens, norm_w=norm_w, w_in=w_in, conv_w=conv_w, conv_b=conv_b, dt_bias=dt_bias,
                 a_log=a_log, d_skip=d_skip, sb_norm_w=sb_norm_w, ssd_norm_w=ssd_norm_w, w_out=w_out,
                 final_norm_w=final_norm_w)
    mom = dict(meta_tokens=m_meta_tokens, norm_w=m_norm_w, w_in=m_w_in, conv_w=m_conv_w, conv_b=m_conv_b,
               dt_bias=m_dt_bias, a_log=m_a_log, d_skip=m_d_skip, sb_norm_w=m_sb_norm_w, ssd_norm_w=m_ssd_norm_w,
               w_out=m_w_out, final_norm_w=m_final_norm_w)
    var = dict(meta_tokens=v_meta_tokens, norm_w=v_norm_w, w_in=v_w_in, conv_w=v_conv_w, conv_b=v_conv_b,
               dt_bias=v_dt_bias, a_log=v_a_log, d_skip=v_d_skip, sb_norm_w=v_sb_norm_w, ssd_norm_w=v_ssd_norm_w,
               w_out=v_w_out, final_norm_w=v_final_norm_w)
    seq = x.shape[1]

    def two_d(a):
        return a.reshape((-1, a.shape[-1])) if a.ndim != 2 else a

    g_win, g_wout, g_meta, g_cw = _gather_shards(
        [w_in[0].astype(_MXU), w_out[0].astype(_MXU), meta_tokens, conv_w[0]])
    w_in_full = jnp.swapaxes(g_win, 0, 1).reshape(D_MODEL, D_IN)
    w_main = w_in_full[:, :N_MAIN]
    w_dt = jnp.pad(w_in_full[:, N_MAIN:], ((0, 0), (0, 128 - N_HEADS)))
    w_out_full = g_wout.reshape(2 * D_MODEL, D_MODEL)
    meta_full = jnp.swapaxes(g_meta, 0, 1).reshape(N_META, D_MODEL)
    conv_w_full = jnp.swapaxes(g_cw, 0, 1).reshape(4, XBC_W)

    sq_err, g_x, grads = _device_grads(
        x.reshape(seq, D_MODEL), loss_target.reshape(seq, D_MODEL), meta_full, norm_w, w_main, w_dt, conv_w_full,
        conv_b, dt_bias, a_log, d_skip, sb_norm_w, ssd_norm_w, w_out_full, final_norm_w.reshape(1, D_MODEL))

    slab_in = jnp.swapaxes(grads["w_in"].reshape(D_MODEL, N_CHIPS, W_IN_SHARD), 0, 1).astype(_MXU)
    slab_out = grads["w_out"].reshape(N_CHIPS, W_OUT_SHARD, D_MODEL).astype(_MXU)
    got_in, got_out = _scatter_slabs([slab_in, slab_out])
    s_in = _sum_slabs(got_in, "sum_w_in")
    s_out = _sum_slabs(got_out, "sum_w_out")
    sib_in, sib_out = _swap_sibling([s_in, s_out])
    big = dict(w_in=_update_big(w_in[0], m_w_in[0], v_w_in[0], s_in, sib_in, "update_w_in"),
               w_out=_update_big(w_out[0], m_w_out[0], v_w_out[0], s_out, sib_out, "update_w_out"))

    gathered = _gather_small(_pack_small(sq_err, grads))
    loss, sg, sd, sm, sv = _update_small(
        gathered, {k: two_d(given[k]) for k in _SMALL}, {k: two_d(mom[k]) for k in _SMALL},
        {k: two_d(var[k]) for k in _SMALL})

    out = {}
    for idx, group in enumerate((sg, sd, sm, sv)):
        for k in _SMALL:
            out[(idx, k)] = group[k].reshape(given[k].shape)
        for k in ("w_in", "w_out"):
            out[(idx, k)] = big[k][idx].reshape(given[k].shape)
    return (loss, g_x.reshape(x.shape), *[out[(idx, k)] for idx in range(4) for k in _WEIGHTS])
```

```python
import functools
import math

import jax
import jax.numpy as jnp
from jax import lax
from jax.experimental import pallas as pl
from jax.experimental.pallas import tpu as pltpu

F32 = jnp.float32
_MXU = jnp.bfloat16

D_MODEL = 1024
N_META = 16
PAD = 112
OFF = PAD + N_META
TM = 128
CHUNK = 64
SB_W = 1024
SSD_W = 1024
N_HEADS = 16
HEAD = 64
N_GROUPS = 2
N_STATE = 128
XBC_W = SSD_W + 2 * N_GROUPS * N_STATE
N_MAIN = 4 * SB_W + SSD_W + XBC_W
COL_GATE = 3 * SB_W
COL_Z = 4 * SB_W
COL_XBC = 5 * SB_W
D_IN = N_MAIN + N_HEADS
EPS = 1e-5
N_CHIPS = 4
W_IN_SHARD = D_IN // N_CHIPS
W_OUT_SHARD = 2 * D_MODEL // N_CHIPS

ADAM_LR = 0.001
ADAM_B1 = 0.9
ADAM_B2 = 0.999
ADAM_EPS = 1e-08
ADAM_WD = 0.01
ADAM_STEP = 10

_SDS = jax.ShapeDtypeStruct
_NT = (((1,), (1,)), ((), ()))
_TN = (((0,), (0,)), ((), ()))
_VMEM = pltpu.VMEM


def _params(sem=None, vmem_mb=None):
    kw = {}
    if sem is not None:
        kw["dimension_semantics"] = sem
    if vmem_mb is not None:
        kw["vmem_limit_bytes"] = vmem_mb * 1024 * 1024
    return pltpu.CompilerParams(**kw)


def _mm(a, b):
    return jnp.dot(a.astype(_MXU), b.astype(_MXU), preferred_element_type=F32)


def _mm_nt(a, b):
    return lax.dot_general(a.astype(_MXU), b.astype(_MXU), _NT, preferred_element_type=F32)


def _mm_tn(a, b):
    return lax.dot_general(a.astype(_MXU), b.astype(_MXU), _TN, preferred_element_type=F32)


def _split(x, parts):
    out = []
    r = x
    for _ in range(parts):
        p = r.astype(_MXU)
        out.append(p)
        r = r - p.astype(F32)
    return out


def _sel_right(x, m01, parts=3):
    acc = None
    for p in _split(x, parts):
        t = jnp.dot(p, m01, preferred_element_type=F32)
        acc = t if acc is None else acc + t
    return acc


def _sel_left(m01, x, parts=3):
    acc = None
    for p in _split(x, parts):
        t = jnp.dot(m01, p, preferred_element_type=F32)
        acc = t if acc is None else acc + t
    return acc


def _iota(shape, axis):
    return lax.broadcasted_iota(jnp.int32, shape, axis)


def _sigmoid(x):
    return 1.0 / (1.0 + jnp.exp(-x))


def _prep(x2d, meta_full, norm_w):
    seq = x2d.shape[0]
    lp = seq + OFF
    nb = lp // TM

    def body(x_ref, meta_ref, w_ref, h0_ref, u_ref, ut_ref):
        i = pl.program_id(0)

        @pl.when(i == 0)
        def _():
            h0_ref[...] = jnp.concatenate([jnp.zeros((PAD, D_MODEL), F32), meta_ref[...]], axis=0)

        @pl.when(i > 0)
        def _():
            h0_ref[...] = x_ref[...]

        h = h0_ref[...]
        rs = lax.rsqrt(jnp.mean(h * h, axis=-1, keepdims=True) + EPS)
        u = (h * rs * w_ref[...]).astype(_MXU)
        u_ref[...] = u
        ut_ref[...] = u.T

    return pl.pallas_call(
        body, name="prep", grid=(nb,),
        in_specs=[pl.BlockSpec((TM, D_MODEL), lambda i: (jnp.maximum(i - 1, 0), 0)),
                  pl.BlockSpec((N_META, D_MODEL), lambda i: (0, 0)),
                  pl.BlockSpec((1, D_MODEL), lambda i: (0, 0))],
        out_specs=[pl.BlockSpec((TM, D_MODEL), lambda i: (i, 0)),
                   pl.BlockSpec((TM, D_MODEL), lambda i: (i, 0)),
                   pl.BlockSpec((D_MODEL, TM), lambda i: (0, i))],
        out_shape=[_SDS((lp, D_MODEL), F32), _SDS((lp, D_MODEL), _MXU), _SDS((D_MODEL, lp), _MXU)],
        compiler_params=_params(("arbitrary",)),
    )(x2d, meta_full, norm_w)


def _inproj(u, w_main, w_dt):
    lp = u.shape[0]
    tn = 512

    def body(u_ref, w_ref, wdt_ref, o_ref, odt_ref):
        o_ref[...] = jnp.dot(u_ref[...], w_ref[...], preferred_element_type=F32)

        @pl.when(pl.program_id(0) == 0)
        def _():
            odt_ref[...] = jnp.dot(u_ref[...], wdt_ref[...], preferred_element_type=F32)

    return pl.pallas_call(
        body, name="inproj", grid=(N_MAIN // tn,),
        in_specs=[pl.BlockSpec((lp, D_MODEL), lambda j: (0, 0)),
                  pl.BlockSpec((D_MODEL, tn), lambda j: (0, j)),
                  pl.BlockSpec((D_MODEL, 128), lambda j: (0, 0))],
        out_specs=[pl.BlockSpec((lp, tn), lambda j: (0, j)),
                   pl.BlockSpec((lp, 128), lambda j: (0, 0))],
        out_shape=[_SDS((lp, N_MAIN), F32), _SDS((lp, 128), F32)],
        compiler_params=_params(("arbitrary",), 48),
    )(u, w_main, w_dt)


SB_WINDOW = 3
SB_DEAD = -104.0


def _sb_logs(qh, kwin):
    z = lax.dot_general(qh, kwin, _NT, preferred_element_type=F32)
    e = jnp.exp(-jnp.abs(z))
    l1p = jnp.log(1.0 + e)
    lk_full = -(jnp.maximum(z, 0.0) + l1p)
    ls = jnp.minimum(z, 0.0) - l1p
    return z, e, ls, lk_full


def _blk(a, b):
    return a[:, b * TM:(b + 1) * TM]


def _stacked_sel(blocks, m01):
    n = len(blocks)
    pieces = [_split(b, 2) for b in blocks]
    stacked = jnp.concatenate([p[0] for p in pieces] + [p[1] for p in pieces], axis=0)
    res = jnp.dot(stacked, m01, preferred_element_type=F32)
    return [res[j * TM:(j + 1) * TM] + res[(n + j) * TM:(n + j + 1) * TM] for j in range(n)]


def _sb_weights(ls, lk_full, run, last_mask, upper, n):
    lk = [_blk(lk_full, b) for b in range(n)]
    lk[n - 1] = jnp.where(last_mask, lk[n - 1], 0.0)
    aft = _stacked_sel(lk, upper)
    w = [None] * n
    for b in range(n - 1, -1, -1):
        wb = jnp.exp(_blk(ls, b) + aft[b] + run)
        w[b] = jnp.where(last_mask, wb, 0.0) if b == n - 1 else wb
        run = run + jnp.sum(lk[b], axis=1, keepdims=True)
    return w, run


def _sb_fwd(proj):
    lp = proj.shape[0]
    nb = lp // TM

    def body(q_ref, k_ref, v_ref, o_ref, olo_ref, acc, run_scr):
        i = pl.program_id(1)
        lane = _iota((TM, TM), 1)
        row = _iota((TM, TM), 0)
        head0 = lane < HEAD
        upper = (row > lane).astype(_MXU)
        strict = lane < row
        q = q_ref[...] * (1.0 / math.sqrt(HEAD))
        qh = (jnp.where(head0, q, 0.0).astype(_MXU), jnp.where(head0, 0.0, q).astype(_MXU))

        def key_set(first, n, last_mask):
            off = pl.multiple_of(first * TM, TM)
            kwin = k_ref[pl.ds(off, n * TM), :].astype(_MXU)
            vwin = v_ref[pl.ds(off, n * TM), :].astype(_MXU)
            alive = None
            for hh in range(2):
                run = run_scr[hh][:, 0:1]
                _, _, ls, lk_full = _sb_logs(qh[hh], kwin)
                w, run = _sb_weights(ls, lk_full, run, last_mask, upper, n)
                pieces = [_split(wb, 2) for wb in w]
                stacked = jnp.concatenate(
                    [jnp.concatenate([p[0] for p in pieces], axis=1), jnp.concatenate([p[1] for p in pieces], axis=1)], axis=0)
                res = jnp.dot(stacked, vwin, preferred_element_type=F32)
                acc[hh] += res[0:TM]
                acc[2 + hh] += res[TM:2 * TM]
                run_scr[hh] = jnp.broadcast_to(run, (TM, TM))
                top = jnp.max(run)
                alive = top if alive is None else jnp.maximum(alive, top)
            return (alive > SB_DEAD).astype(jnp.int32)

        acc[...] = jnp.zeros_like(acc)
        run_scr[...] = jnp.zeros_like(run_scr)

        @pl.when(i >= SB_WINDOW - 1)
        def _():
            key_set(i - (SB_WINDOW - 1), SB_WINDOW, strict)

        start = jnp.where(i >= SB_WINDOW - 1, i - SB_WINDOW, i)
        alive0 = (jnp.max(run_scr[...]) > SB_DEAD).astype(jnp.int32)

        def cond(c):
            return jnp.logical_and(c[0] >= 0, c[1] > 0)

        def step(c):
            kb = c[0]
            return kb - 1, key_set(kb, 1, jnp.logical_or(strict, kb < i))

        lax.while_loop(cond, step, (start, alive0))
        o_ref[...] = jnp.where(head0, acc[0], acc[1])
        olo_ref[...] = jnp.where(head0, acc[2], acc[3])

    npair = SB_W // TM
    blk = pl.BlockSpec((TM, TM), lambda p, i: (i, p))
    return pl.pallas_call(
        body, name="sb_fwd", grid=(npair, nb),
        in_specs=[blk,
                  pl.BlockSpec((lp, TM), lambda p, i: (0, npair + p)),
                  pl.BlockSpec((lp, TM), lambda p, i: (0, 2 * npair + p))],
        out_specs=[blk, blk],
        out_shape=[_SDS((lp, SB_W), F32), _SDS((lp, SB_W), F32)],
        scratch_shapes=[pltpu.VMEM((4, TM, TM), F32), pltpu.VMEM((2, TM, TM), F32)],
        compiler_params=_params(("arbitrary", "arbitrary")),
    )(proj, proj, proj)


def _sb_bwd(proj, o_sb, o_lo, d_o):
    lp = proj.shape[0]
    nb = lp // TM
    scale = 1.0 / math.sqrt(HEAD)

    def body(q_ref, k_ref, v_ref, o_ref, olo_ref, do_ref, dq_ref, dk_ref, dv_ref, dq_acc, run_scr, gsum_scr):
        i = pl.program_id(1)

        @pl.when(i == 0)
        def _():
            dk_ref[...] = jnp.zeros_like(dk_ref)
            dv_ref[...] = jnp.zeros_like(dv_ref)

        lane = _iota((TM, TM), 1)
        row = _iota((TM, TM), 0)
        head0 = lane < HEAD
        hmask = (head0, jnp.logical_not(head0))
        upper = (row > lane).astype(_MXU)
        lower_incl = (row >= lane).astype(_MXU)
        strict = lane < row
        q = q_ref[...] * scale
        do = do_ref[...]
        prod = do.astype(_MXU).astype(F32) * (o_ref[...] + olo_ref[...])
        qh = tuple(jnp.where(m, q, 0.0).astype(_MXU) for m in hmask)
        doh = tuple(jnp.where(m, do, 0.0).astype(_MXU) for m in hmask)
        gtot = tuple(jnp.sum(jnp.where(m, prod, 0.0), axis=1, keepdims=True) for m in hmask)

        def key_set(first, n, last_mask):
            off = pl.multiple_of(first * TM, TM)
            kf = k_ref[pl.ds(off, n * TM), :]
            kwin = kf.astype(_MXU)
            vwin = v_ref[pl.ds(off, n * TM), :].astype(_MXU)
            dk_win = None
            alive = None
            for hh in range(2):
                run = run_scr[hh][:, 0:1]
                gsum = gsum_scr[hh][:, 0:1]
                z, e, ls, lk_full = _sb_logs(qh[hh], kwin)
                w, run = _sb_weights(ls, lk_full, run, last_mask, upper, n)
                r = 1.0 / (1.0 + e)
                er = e * r
                pos = z >= 0.0
                beta = jnp.where(pos, r, er)
                one_m_beta = jnp.where(pos, er, r)
                dw = lax.dot_general(doh[hh], vwin, _NT, preferred_element_type=F32)
                g = [_blk(dw, b) * w[b] for b in range(n)]
                suffix = _stacked_sel(g, lower_incl)
                dz = [None] * n
                for b in range(n - 1, -1, -1):
                    prefix = gtot[hh] - gsum - suffix[b]
                    d = g[b] * _blk(one_m_beta, b) - _blk(beta, b) * prefix
                    dz[b] = (jnp.where(last_mask, d, 0.0) if b == n - 1 else d).astype(_MXU)
                    gsum = gsum + jnp.sum(g[b], axis=1, keepdims=True)
                dzw = jnp.concatenate(dz, axis=1)
                ww = jnp.concatenate([wb.astype(_MXU) for wb in w], axis=1)
                kh = jnp.where(hmask[hh][0:1, :], kf, 0.0).astype(_MXU)
                dq_acc[...] += jnp.dot(dzw, kh, preferred_element_type=F32)
                dk_h = lax.dot_general(dzw, qh[hh], _TN, preferred_element_type=F32)
                dv_h = lax.dot_general(ww, doh[hh], _TN, preferred_element_type=F32)
                dk_win = (dk_h, dv_h) if dk_win is None else (dk_win[0] + dk_h, dk_win[1] + dv_h)
                run_scr[hh] = jnp.broadcast_to(run, (TM, TM))
                gsum_scr[hh] = jnp.broadcast_to(gsum, (TM, TM))
                top = jnp.max(run)
                alive = top if alive is None else jnp.maximum(alive, top)
            dk_ref[pl.ds(off, n * TM), :] += dk_win[0]
            dv_ref[pl.ds(off, n * TM), :] += dk_win[1]
            return (alive > SB_DEAD).astype(jnp.int32)

        dq_acc[...] = jnp.zeros_like(dq_acc)
        run_scr[...] = jnp.zeros_like(run_scr)
        gsum_scr[...] = jnp.zeros_like(gsum_scr)

        @pl.when(i >= SB_WINDOW - 1)
        def _():
            key_set(i - (SB_WINDOW - 1), SB_WINDOW, strict)

        start = jnp.where(i >= SB_WINDOW - 1, i - SB_WINDOW, i)
        alive0 = (jnp.max(run_scr[...]) > SB_DEAD).astype(jnp.int32)

        def cond(c):
            return jnp.logical_and(c[0] >= 0, c[1] > 0)

        def step(c):
            kb = c[0]
            return kb - 1, key_set(kb, 1, jnp.logical_or(strict, kb < i))

        lax.while_loop(cond, step, (start, alive0))
        dq_ref[...] = dq_acc[...] * scale

    npair = SB_W // TM
    blk = pl.BlockSpec((TM, TM), lambda p, i: (i, p))
    col = pl.BlockSpec((lp, TM), lambda p, i: (0, p))
    return pl.pallas_call(
        body, name="sb_bwd", grid=(npair, nb),
        in_specs=[blk,
                  pl.BlockSpec((lp, TM), lambda p, i: (0, npair + p)),
                  pl.BlockSpec((lp, TM), lambda p, i: (0, 2 * npair + p)),
                  blk, blk, blk],
        out_specs=[blk, col, col],
        out_shape=[_SDS((lp, SB_W), F32)] * 3,
        scratch_shapes=[pltpu.VMEM((TM, TM), F32), pltpu.VMEM((2, TM, TM), F32), pltpu.VMEM((2, TM, TM), F32)],
        compiler_params=_params(("arbitrary", "arbitrary")),
    )(proj, proj, proj, o_sb, o_lo, d_o)


def _conv_pre(x_ref, w_ref, b_ref, lp):
    n = lp - 8
    w = w_ref[...]
    pre = (x_ref[pl.ds(5, n), :] * w[0:1, :] + x_ref[pl.ds(6, n), :] * w[1:2, :]
           + x_ref[pl.ds(7, n), :] * w[2:3, :] + x_ref[pl.ds(8, n), :] * w[3:4, :]) + b_ref[...]
    live = (_iota((n, 128), 0) + 8) >= PAD
    return pre, live


def _conv_fwd(proj, dt_raw, conv_w, conv_b, dt_bias128):
    lp = proj.shape[0]
    nblk = XBC_W // 128
    c0 = COL_XBC // 128

    def body(x_ref, w_ref, b_ref, dtr_ref, dtb_ref, o_ref, dt_ref):
        pre, live = _conv_pre(x_ref, w_ref, b_ref, lp)
        act = pre * _sigmoid(pre)
        o_ref[pl.ds(0, 8), :] = jnp.zeros((8, 128), F32)
        o_ref[pl.ds(8, lp - 8), :] = jnp.where(live, act, 0.0)

        @pl.when(pl.program_id(0) == 0)
        def _():
            s = dtr_ref[...] + dtb_ref[...]
            sp = jnp.maximum(s, 0.0) + jnp.log(1.0 + jnp.exp(-jnp.abs(s)))
            dt_ref[...] = jnp.where(_iota((lp, 128), 0) >= PAD, sp, 0.0)

    return pl.pallas_call(
        body, name="conv_fwd", grid=(nblk,),
        in_specs=[pl.BlockSpec((lp, 128), lambda j: (0, c0 + j)),
                  pl.BlockSpec((4, 128), lambda j: (0, j)),
                  pl.BlockSpec((1, 128), lambda j: (0, j)),
                  pl.BlockSpec((lp, 128), lambda j: (0, 0)),
                  pl.BlockSpec((1, 128), lambda j: (0, 0))],
        out_specs=[pl.BlockSpec((lp, 128), lambda j: (0, j)),
                   pl.BlockSpec((lp, 128), lambda j: (0, 0))],
        out_shape=[_SDS((lp, XBC_W), F32), _SDS((lp, 128), F32)],
        compiler_params=_params(("arbitrary",)),
    )(proj, conv_w, conv_b, dt_raw, dt_bias128)


def _conv_bwd(proj, dt_raw, conv_w, conv_b, dt_bias128, d_xbc, d_dt128):
    lp = proj.shape[0]
    nblk = XBC_W // 128
    c0 = COL_XBC // 128
    n = lp - 8

    def body(x_ref, w_ref, b_ref, dtr_ref, dtb_ref, dy_ref, ddt_ref,
             dx_ref, gw_ref, gb_ref, ddtr_ref, gdtb_ref, scr):
        pre, live = _conv_pre(x_ref, w_ref, b_ref, lp)
        sg = _sigmoid(pre)
        dpre = jnp.where(live, dy_ref[pl.ds(8, n), :] * (sg * (1.0 + pre * (1.0 - sg))), 0.0)
        gb_ref[...] = jnp.sum(dpre, axis=0, keepdims=True)
        gw_ref[...] = jnp.concatenate(
            [jnp.sum(dpre * x_ref[pl.ds(5 + k, n), :], axis=0, keepdims=True) for k in range(4)], axis=0)
        scr[pl.ds(0, 8), :] = jnp.zeros((8, 128), F32)
        scr[pl.ds(8, n), :] = dpre
        scr[pl.ds(lp, 8), :] = jnp.zeros((8, 128), F32)
        w = w_ref[...]
        dx_ref[pl.ds(0, 8), :] = jnp.zeros((8, 128), F32)
        dx_ref[pl.ds(8, n), :] = (scr[pl.ds(8, n), :] * w[3:4, :] + scr[pl.ds(9, n), :] * w[2:3, :]
                                  + scr[pl.ds(10, n), :] * w[1:2, :] + scr[pl.ds(11, n), :] * w[0:1, :])

        @pl.when(pl.program_id(0) == 0)
        def _():
            s = dtr_ref[...] + dtb_ref[...]
            d = jnp.where(_iota((lp, 128), 0) >= PAD, ddt_ref[...] * _sigmoid(s), 0.0)
            ddtr_ref[...] = d
            gdtb_ref[...] = jnp.sum(d, axis=0, keepdims=True)

    colblk = pl.BlockSpec((lp, 128), lambda j: (0, j))
    full128 = pl.BlockSpec((lp, 128), lambda j: (0, 0))
    return pl.pallas_call(
        body, name="conv_bwd", grid=(nblk,),
        in_specs=[pl.BlockSpec((lp, 128), lambda j: (0, c0 + j)),
                  pl.BlockSpec((4, 128), lambda j: (0, j)),
                  pl.BlockSpec((1, 128), lambda j: (0, j)),
                  full128, pl.BlockSpec((1, 128), lambda j: (0, 0)),
                  colblk, full128],
        out_specs=[colblk, pl.BlockSpec((4, 128), lambda j: (0, j)), pl.BlockSpec((1, 128), lambda j: (0, j)),
                   full128, pl.BlockSpec((1, 128), lambda j: (0, 0))],
        out_shape=[_SDS((lp, XBC_W), F32), _SDS((4, XBC_W), F32), _SDS((1, XBC_W), F32),
                   _SDS((lp, 128), F32), _SDS((1, 128), F32)],
        scratch_shapes=[pltpu.VMEM((lp + 8, 128), F32)],
        compiler_params=_params(("arbitrary",)),
    )(proj, conv_w, conv_b, dt_raw, dt_bias128, d_xbc, d_dt128)


def _ssd_pieces(dt, dt_t, a, a_t):
    r64 = _iota((CHUNK, CHUNK), 0)
    c64 = _iota((CHUNK, CHUNK), 1)
    tril = c64 <= r64
    tril01 = tril.astype(_MXU)
    triu01 = (r64 <= c64).astype(_MXU)
    expand = (lax.shift_right_logical(_iota((N_HEADS, SSD_W), 1), 6) == _iota((N_HEADS, SSD_W), 0)).astype(_MXU)
    acum = _sel_left(tril01, dt * a)
    acum_t = _sel_right(dt_t * a_t, triu01)
    ax = _sel_right(acum, expand)
    dtx = _sel_right(dt, expand)
    return tril, expand, acum, acum_t, ax, dtx


def _seg_matrix():
    return (lax.shift_right_logical(_iota((SSD_W, N_HEADS), 0), 6) == _iota((SSD_W, N_HEADS), 1)).astype(_MXU)


def _head_decay(ax, acum_t, h, tril):
    col = ax[:, h * HEAD:(h + 1) * HEAD]
    rowv = acum_t[h:h + 1, :]
    return jnp.where(tril, jnp.exp(jnp.minimum(col - rowv, 0.0)), 0.0)


def _ssd_fwd(xbc, dt_c, dt_tc, a, a_t, dskip_x):
    lp = xbc.shape[0]
    nc = lp // CHUNK
    gw = SSD_W // N_GROUPS
    hpg = N_HEADS // N_GROUPS

    def body(x_ref, dt_ref, dtt_ref, a_ref, at_ref, d_ref, y_ref, st_ref, state):
        c = pl.program_id(0)

        @pl.when(c == 0)
        def _():
            state[...] = jnp.zeros_like(state)

        st_ref[0] = state[...]
        tril, _, _, acum_t, ax, dtx = _ssd_pieces(dt_ref[0], dtt_ref[0], a_ref[...], at_ref[...])
        x = x_ref[:, 0:SSD_W]
        xdt = x * dtx
        ea = jnp.exp(ax)
        aex = ax[CHUNK - 1:CHUNK, :]
        wd = jnp.exp(aex - ax)
        eae = jnp.exp(aex)
        xw = xdt * wd
        y_ref[...] = x * d_ref[...]
        for g in range(N_GROUPS):
            gs = slice(g * gw, (g + 1) * gw)
            rs = slice(g * N_STATE, (g + 1) * N_STATE)
            bg = x_ref[:, SSD_W + g * N_STATE:SSD_W + (g + 1) * N_STATE]
            cg = x_ref[:, SSD_W + N_GROUPS * N_STATE + g * N_STATE:SSD_W + N_GROUPS * N_STATE + (g + 1) * N_STATE]
            sg = state[rs, :]
            cb = _mm_nt(cg, bg)
            y_ref[:, gs] += _mm(cg, sg) * ea[:, gs]
            for r in range(hpg):
                h = g * hpg + r
                hs = slice(h * HEAD, (h + 1) * HEAD)
                m = cb * _head_decay(ax, acum_t, h, tril)
                y_ref[:, hs] += _mm(m, xdt[:, hs])
            state[rs, :] = sg * eae[:, gs] + _mm_tn(bg, xw[:, gs])

    return pl.pallas_call(
        body, name="ssd_fwd", grid=(nc,),
        in_specs=[pl.BlockSpec((CHUNK, XBC_W), lambda c: (c, 0)),
                  pl.BlockSpec((1, CHUNK, N_HEADS), lambda c: (c, 0, 0)),
                  pl.BlockSpec((1, N_HEADS, CHUNK), lambda c: (c, 0, 0)),
                  pl.BlockSpec((1, N_HEADS), lambda c: (0, 0)),
                  pl.BlockSpec((N_HEADS, 1), lambda c: (0, 0)),
                  pl.BlockSpec((1, SSD_W), lambda c: (0, 0))],
        out_specs=[pl.BlockSpec((CHUNK, SSD_W), lambda c: (c, 0)),
                   pl.BlockSpec((1, N_GROUPS * N_STATE, gw), lambda c: (c, 0, 0))],
        out_shape=[_SDS((lp, SSD_W), F32), _SDS((nc, N_GROUPS * N_STATE, gw), F32)],
        scratch_shapes=[pltpu.VMEM((N_GROUPS * N_STATE, gw), F32)],
        compiler_params=_params(("arbitrary",)),
    )(xbc, dt_c, dt_tc, a, a_t, dskip_x)


def _ssd_bwd(xbc, dt_c, dt_tc, a, a_t, dskip_x, states, d_y):
    lp = xbc.shape[0]
    nc = lp // CHUNK
    gw = SSD_W // N_GROUPS
    hpg = N_HEADS // N_GROUPS

    def body(x_ref, dt_ref, dtt_ref, a_ref, at_ref, d_ref, st_ref, dy_ref,
             dx_ref, ddta_ref, ddtb_ref, ga1_ref, ga2_ref, gd_ref, dstate, dxdt_scr, z_scr, yoff_scr, sds_scr):
        c = pl.program_id(0)

        @pl.when(c == 0)
        def _():
            dstate[...] = jnp.zeros_like(dstate)
            ga1_ref[...] = jnp.zeros_like(ga1_ref)
            ga2_ref[...] = jnp.zeros_like(ga2_ref)
            gd_ref[...] = jnp.zeros_like(gd_ref)

        dt = dt_ref[0]
        dt_t = dtt_ref[0]
        a = a_ref[...]
        a_t = at_ref[...]
        tril, _, acum, acum_t, ax, dtx = _ssd_pieces(dt, dt_t, a, a_t)
        seg = _seg_matrix()
        x = x_ref[:, 0:SSD_W]
        dy = dy_ref[...]
        xdt = x * dtx
        ea = jnp.exp(ax)
        aex = ax[CHUNK - 1:CHUNK, :]
        wd = jnp.exp(aex - ax)
        eae = jnp.exp(aex)
        xw = xdt * wd
        edy = ea * dy
        lane16 = _iota((CHUNK, N_HEADS), 1)
        row16 = _iota((N_HEADS, CHUNK), 0)
        da_col = jnp.zeros((CHUNK, N_HEADS), F32)
        da_row = jnp.zeros((N_HEADS, CHUNK), F32)
        for g in range(N_GROUPS):
            gs = slice(g * gw, (g + 1) * gw)
            rs = slice(g * N_STATE, (g + 1) * N_STATE)
            bcol = slice(SSD_W + g * N_STATE, SSD_W + (g + 1) * N_STATE)
            ccol = slice(SSD_W + N_GROUPS * N_STATE + g * N_STATE, SSD_W + N_GROUPS * N_STATE + (g + 1) * N_STATE)
            bg = x_ref[:, bcol]
            cg = x_ref[:, ccol]
            sg = st_ref[0, rs, :]
            dsn = dstate[rs, :]
            cb = _mm_nt(cg, bg)
            z_scr[:, gs] = _mm(bg, dsn)
            yoff_scr[:, gs] = _mm(cg, sg) * ea[:, gs]
            sds_scr[:, gs] = jnp.broadcast_to(jnp.sum(dsn * sg, axis=0, keepdims=True), (8, gw))
            dcb = jnp.zeros((CHUNK, CHUNK), F32)
            for r in range(hpg):
                h = g * hpg + r
                hs = slice(h * HEAD, (h + 1) * HEAD)
                dec = _head_decay(ax, acum_t, h, tril)
                m = cb * dec
                t1 = _mm_nt(dy[:, hs], xdt[:, hs])
                dcb = dcb + dec * t1
                tm = m * t1
                da_col = da_col + jnp.where(lane16 == h, jnp.sum(tm, axis=1, keepdims=True), 0.0)
                da_row = da_row - jnp.where(row16 == h, jnp.sum(tm, axis=0, keepdims=True), 0.0)
                dxdt_scr[:, hs] = _mm_tn(m, dy[:, hs])
            dx_ref[:, ccol] = _mm(dcb, bg) + _mm_nt(edy[:, gs], sg)
            dx_ref[:, bcol] = _mm_tn(dcb, cg) + _mm_nt(xw[:, gs], dsn)
            dstate[rs, :] = eae[:, gs] * dsn + _mm_tn(cg, edy[:, gs])
        zf = z_scr[...]
        dxdt = dxdt_scr[...] + wd * zf
        t3 = _sel_right(xw * zf, seg)
        da_col = da_col + _sel_right(dy * yoff_scr[...], seg) - t3
        aend = acum[CHUNK - 1:CHUNK, :]
        sd = _sel_right(sds_scr[...], seg)[0:1, :] * jnp.exp(aend)
        last = jnp.sum(t3, axis=0, keepdims=True) + sd
        da_col = da_col + jnp.where(_iota((CHUNK, N_HEADS), 0) == CHUNK - 1, last, 0.0)
        r64 = _iota((CHUNK, CHUNK), 0)
        c64 = _iota((CHUNK, CHUNK), 1)
        ddta1 = _sel_left((c64 >= r64).astype(_MXU), da_col)
        ddta2 = _sel_right(da_row, (r64 >= c64).astype(_MXU))
        ddta_ref[0] = a * ddta1 + _sel_right(dxdt * x, seg)
        ddtb_ref[0] = a_t * ddta2
        ga1_ref[...] += jnp.sum(dt * ddta1, axis=0, keepdims=True)
        ga2_ref[...] += jnp.sum(dt_t * ddta2, axis=1, keepdims=True)
        dx_ref[:, 0:SSD_W] = dxdt * dtx + d_ref[...] * dy
        gd_ref[...] += jnp.sum(dy * x, axis=0, keepdims=True)

    rev = lambda c: (nc - 1 - c, 0)
    rev3 = lambda c: (nc - 1 - c, 0, 0)
    return pl.pallas_call(
        body, name="ssd_bwd", grid=(nc,),
        in_specs=[pl.BlockSpec((CHUNK, XBC_W), rev),
                  pl.BlockSpec((1, CHUNK, N_HEADS), rev3),
                  pl.BlockSpec((1, N_HEADS, CHUNK), rev3),
                  pl.BlockSpec((1, N_HEADS), lambda c: (0, 0)),
                  pl.BlockSpec((N_HEADS, 1), lambda c: (0, 0)),
                  pl.BlockSpec((1, SSD_W), lambda c: (0, 0)),
                  pl.BlockSpec((1, N_GROUPS * N_STATE, gw), rev3),
                  pl.BlockSpec((CHUNK, SSD_W), rev)],
        out_specs=[pl.BlockSpec((CHUNK, XBC_W), rev),
                   pl.BlockSpec((1, CHUNK, N_HEADS), rev3),
                   pl.BlockSpec((1, N_HEADS, CHUNK), rev3),
                   pl.BlockSpec((1, N_HEADS), lambda c: (0, 0)),
                   pl.BlockSpec((N_HEADS, 1), lambda c: (0, 0)),
                   pl.BlockSpec((1, SSD_W), lambda c: (0, 0))],
        out_shape=[_SDS((lp, XBC_W), F32), _SDS((nc, CHUNK, N_HEADS), F32), _SDS((nc, N_HEADS, CHUNK), F32),
                   _SDS((1, N_HEADS), F32), _SDS((N_HEADS, 1), F32), _SDS((1, SSD_W), F32)],
        scratch_shapes=[pltpu.VMEM((N_GROUPS * N_STATE, gw), F32), pltpu.VMEM((CHUNK, SSD_W), F32),
                        pltpu.VMEM((CHUNK, SSD_W), F32), pltpu.VMEM((CHUNK, SSD_W), F32),
                        pltpu.VMEM((8, SSD_W), F32)],
        compiler_params=_params(("arbitrary",)),
    )(xbc, dt_c, dt_tc, a, a_t, dskip_x, states, d_y)


def _gated_norm(o, gate, w):
    sg = _sigmoid(gate)
    p = o * (gate * sg)
    rs = lax.rsqrt(jnp.mean(p * p, axis=-1, keepdims=True) + EPS)
    n = p * rs
    return sg, rs, n, n * w


def _tail_fwd(o_sb, o_ssd, proj, h0, target, w_out, sb_w, ssd_w, fin_w):
    lp = o_sb.shape[0]
    nb = lp // TM
    row = lambda i: (i, 0)
    one = lambda i: (0, 0)

    def body(osb_ref, gate_ref, ossd_ref, z_ref, h0_ref, tgt_ref, wo_ref, sbw_ref, ssdw_ref, fw_ref,
             dh1_ref, loss_ref, gfw_ref):
        i = pl.program_id(0)

        @pl.when(i == 0)
        def _():
            loss_ref[...] = jnp.zeros_like(loss_ref)
            gfw_ref[...] = jnp.zeros_like(gfw_ref)

        y1 = _gated_norm(osb_ref[...], gate_ref[...], sbw_ref[...])[3]
        y2 = _gated_norm(ossd_ref[...], z_ref[...], ssdw_ref[...])[3]
        h1 = (h0_ref[...] + _mm(y1, wo_ref[0:SB_W, :])) + _mm(y2, wo_ref[SB_W:SB_W + SSD_W, :])
        rs1 = lax.rsqrt(jnp.mean(h1 * h1, axis=-1, keepdims=True) + EPS)
        n1 = h1 * rs1
        fw = fw_ref[...]
        diff = jnp.where(i > 0, n1 * fw - tgt_ref[...], 0.0)
        loss_ref[...] += jnp.sum(diff * diff, axis=0, keepdims=True)
        d_out = diff * (1.0 / D_MODEL)
        gfw_ref[...] += jnp.sum(d_out * n1, axis=0, keepdims=True)
        g = d_out * fw
        dh1_ref[...] = rs1 * (g - n1 * jnp.mean(g * n1, axis=-1, keepdims=True))

    return pl.pallas_call(
        body, name="tail_fwd", grid=(nb,),
        in_specs=[pl.BlockSpec((TM, SB_W), row),
                  pl.BlockSpec((TM, SB_W), lambda i: (i, COL_GATE // SB_W)),
                  pl.BlockSpec((TM, SSD_W), row),
                  pl.BlockSpec((TM, SSD_W), lambda i: (i, COL_Z // SSD_W)),
                  pl.BlockSpec((TM, D_MODEL), row),
                  pl.BlockSpec((TM, D_MODEL), lambda i: (jnp.maximum(i - 1, 0), 0)),
                  pl.BlockSpec(memory_space=_VMEM),
                  pl.BlockSpec((1, SB_W), one), pl.BlockSpec((1, SSD_W), one), pl.BlockSpec((1, D_MODEL), one)],
        out_specs=[pl.BlockSpec((TM, D_MODEL), row), pl.BlockSpec((1, D_MODEL), one), pl.BlockSpec((1, D_MODEL), one)],
        out_shape=[_SDS((lp, D_MODEL), F32), _SDS((1, D_MODEL), F32), _SDS((1, D_MODEL), F32)],
        compiler_params=_params(("arbitrary",), 40),
    )(o_sb, proj, o_ssd, proj, h0, target, w_out, sb_w, ssd_w, fin_w)


def _gated_norm_bwd(o, gate, w, dy):
    sg, rs, n, _ = _gated_norm(o, gate, w)
    gw = jnp.sum(dy * n, axis=0, keepdims=True)
    dn = dy * w
    dp = rs * (dn - n * jnp.mean(dn * n, axis=-1, keepdims=True))
    d_o = dp * (gate * sg)
    d_gate = dp * o * (sg * (1.0 + gate * (1.0 - sg)))
    return d_o, d_gate, gw, n * w


def _tail_bwd(o_sb, o_ssd, proj, d_h1, w_out, sb_w, ssd_w):
    lp = o_sb.shape[0]
    nb = lp // TM
    row = lambda i: (i, 0)
    one = lambda i: (0, 0)

    def body(osb_ref, gate_ref, ossd_ref, z_ref, dh1_ref, wo_ref, sbw_ref, ssdw_ref,
             dosb_ref, dgate_ref, dossd_ref, dz_ref, gwo_ref, gsb_ref, gssd_ref):
        i = pl.program_id(0)

        @pl.when(i == 0)
        def _():
            gwo_ref[...] = jnp.zeros_like(gwo_ref)
            gsb_ref[...] = jnp.zeros_like(gsb_ref)
            gssd_ref[...] = jnp.zeros_like(gssd_ref)

        dh1 = dh1_ref[...].astype(_MXU)
        dy1 = lax.dot_general(dh1, wo_ref[0:SB_W, :], _NT, preferred_element_type=F32)
        dy2 = lax.dot_general(dh1, wo_ref[SB_W:SB_W + SSD_W, :], _NT, preferred_element_type=F32)
        d_o, d_g, gw, y1 = _gated_norm_bwd(osb_ref[...], gate_ref[...], sbw_ref[...], dy1)
        dosb_ref[...] = d_o
        dgate_ref[...] = d_g
        gsb_ref[...] += gw
        gwo_ref[0:SB_W, :] += lax.dot_general(y1.astype(_MXU), dh1, _TN, preferred_element_type=F32)
        d_o, d_g, gw, y2 = _gated_norm_bwd(ossd_ref[...], z_ref[...], ssdw_ref[...], dy2)
        dossd_ref[...] = d_o
        dz_ref[...] = d_g
        gssd_ref[...] += gw
        gwo_ref[SB_W:SB_W + SSD_W, :] += lax.dot_general(y2.astype(_MXU), dh1, _TN, preferred_element_type=F32)

    tile = pl.BlockSpec((TM, SB_W), row)
    return pl.pallas_call(
        body, name="tail_bwd", grid=(nb,),
        in_specs=[tile, pl.BlockSpec((TM, SB_W), lambda i: (i, COL_GATE // SB_W)),
                  tile, pl.BlockSpec((TM, SSD_W), lambda i: (i, COL_Z // SSD_W)),
                  tile, pl.BlockSpec(memory_space=_VMEM),
                  pl.BlockSpec((1, SB_W), one), pl.BlockSpec((1, SSD_W), one)],
        out_specs=[tile, tile, tile, tile,
                   pl.BlockSpec((SB_W + SSD_W, D_MODEL), one), pl.BlockSpec((1, SB_W), one), pl.BlockSpec((1, SSD_W), one)],
        out_shape=[_SDS((lp, SB_W), F32)] * 4 + [_SDS((SB_W + SSD_W, D_MODEL), F32), _SDS((1, SB_W), F32), _SDS((1, SSD_W), F32)],
        compiler_params=_params(("arbitrary",), 48),
    )(o_sb, proj, o_ssd, proj, d_h1, w_out, sb_w, ssd_w)


def _in_bwd(d_q, d_k, d_v, d_gate, d_z, d_xbc, d_dt128, w_main, w_dt, h0, d_h1, norm_w):
    lp = h0.shape[0]
    nb = lp // TM
    seq = lp - OFF
    row = lambda i: (i, 0)
    one = lambda i: (0, 0)
    secs = ((0, SB_W), (SB_W, SB_W), (2 * SB_W, SB_W), (COL_GATE, SB_W), (COL_Z, SSD_W), (COL_XBC, XBC_W))

    def body(dq_ref, dk_ref, dv_ref, dg_ref, dz_ref, dx_ref, ddt_ref, w_ref, wdt_ref, h0_ref, dh1_ref, nw_ref,
             gx_ref, gmeta_ref, gnw_ref):
        i = pl.program_id(0)

        @pl.when(i == 0)
        def _():
            gnw_ref[...] = jnp.zeros_like(gnw_ref)

        du = lax.dot_general(ddt_ref[...].astype(_MXU), wdt_ref[...], _NT, preferred_element_type=F32)
        for ref, (c0, width) in zip((dq_ref, dk_ref, dv_ref, dg_ref, dz_ref, dx_ref), secs):
            du = du + lax.dot_general(ref[...].astype(_MXU), w_ref[:, c0:c0 + width], _NT, preferred_element_type=F32)
        h = h0_ref[...]
        rs = lax.rsqrt(jnp.mean(h * h, axis=-1, keepdims=True) + EPS)
        n0 = h * rs
        gnw_ref[...] += jnp.sum(du * n0, axis=0, keepdims=True)
        g = du * nw_ref[...]
        dh0 = dh1_ref[...] + rs * (g - n0 * jnp.mean(g * n0, axis=-1, keepdims=True))

        @pl.when(i == 0)
        def _():
            gmeta_ref[...] = dh0[PAD:PAD + N_META, :]

        @pl.when(i > 0)
        def _():
            gx_ref[...] = dh0

    tile = pl.BlockSpec((TM, D_MODEL), row)
    return pl.pallas_call(
        body, name="in_bwd", grid=(nb,),
        in_specs=[tile, tile, tile, tile, tile, pl.BlockSpec((TM, XBC_W), row), pl.BlockSpec((TM, 128), row),
                  pl.BlockSpec(memory_space=_VMEM), pl.BlockSpec(memory_space=_VMEM),
                  tile, tile, pl.BlockSpec((1, D_MODEL), one)],
        out_specs=[pl.BlockSpec((TM, D_MODEL), lambda i: (jnp.maximum(i - 1, 0), 0)),
                   pl.BlockSpec((N_META, D_MODEL), one), pl.BlockSpec((1, D_MODEL), one)],
        out_shape=[_SDS((seq, D_MODEL), F32), _SDS((N_META, D_MODEL), F32), _SDS((1, D_MODEL), F32)],
        compiler_params=_params(("arbitrary",), 48),
    )(d_q, d_k, d_v, d_gate, d_z, d_xbc, d_dt128, w_main, w_dt, h0, d_h1, norm_w)


def _grad_w(u_t, d_sec, name):
    lp, n = d_sec.shape
    tn = min(512, n)

    def body(ut_ref, d_ref, o_ref):
        o_ref[...] = jnp.dot(ut_ref[...], d_ref[...].astype(_MXU), preferred_element_type=F32)

    return pl.pallas_call(
        body, name=name, grid=(n // tn,),
        in_specs=[pl.BlockSpec((D_MODEL, lp), lambda j: (0, 0)), pl.BlockSpec((lp, tn), lambda j: (0, j))],
        out_specs=pl.BlockSpec((D_MODEL, tn), lambda j: (0, j)),
        out_shape=_SDS((D_MODEL, n), F32),
        compiler_params=_params(("arbitrary",), 40),
    )(u_t, d_sec)


def _device_grads(x2d, target2d, meta_full, norm_w, w_main, w_dt, conv_w, conv_b, dt_bias, a_log, d_skip,
                  sb_w, ssd_w, w_out, fin_w):
    lp = x2d.shape[0] + OFF
    nc = lp // CHUNK
    h0, u, u_t = _prep(x2d, meta_full, norm_w)
    proj, dt_raw = _inproj(u, w_main, w_dt)
    o_sb, o_lo = _sb_fwd(proj)
    dt_bias128 = jnp.pad(dt_bias, ((0, 0), (0, 128 - N_HEADS)))
    xbc, dt128 = _conv_fwd(proj, dt_raw, conv_w, conv_b, dt_bias128)
    dt_c = dt128[:, :N_HEADS].reshape(nc, CHUNK, N_HEADS)
    dt_tc = jnp.swapaxes(dt_c, 1, 2)
    a = -jnp.exp(a_log)
    a_t = a.reshape(N_HEADS, 1)
    dskip_x = jnp.repeat(d_skip, HEAD, axis=1)
    o_ssd, states = _ssd_fwd(xbc, dt_c, dt_tc, a, a_t, dskip_x)
    d_h1, sq_err, g_fin = _tail_fwd(o_sb, o_ssd, proj, h0, target2d, w_out, sb_w, ssd_w, fin_w)

    d_osb, d_gate, d_ossd, d_z, g_wout, g_sb, g_ssd = _tail_bwd(o_sb, o_ssd, proj, d_h1, w_out, sb_w, ssd_w)
    d_q, d_k, d_v = _sb_bwd(proj, o_sb, o_lo, d_osb)
    d_xbc_act, ddt_a, ddt_b, ga1, ga2, gd = _ssd_bwd(xbc, dt_c, dt_tc, a, a_t, dskip_x, states, d_ossd)
    d_dt = (ddt_a + jnp.swapaxes(ddt_b, 1, 2)).reshape(lp, N_HEADS)
    d_dt128 = jnp.pad(d_dt, ((0, 0), (0, 128 - N_HEADS)))
    d_xbc, g_convw, g_convb, d_dtraw128, g_dtb128 = _conv_bwd(proj, dt_raw, conv_w, conv_b, dt_bias128, d_xbc_act, d_dt128)
    g_x, g_meta, g_nw = _in_bwd(d_q, d_k, d_v, d_gate, d_z, d_xbc, d_dtraw128, w_main, w_dt, h0, d_h1, norm_w)
    g_win = jnp.concatenate(
        [_grad_w(u_t, d, "gw_" + nm) for nm, d in (("q", d_q), ("k", d_k), ("v", d_v), ("gate", d_gate), ("z", d_z), ("xbc", d_xbc))]
        + [_grad_w(u_t, d_dtraw128, "gw_dt")[:, :N_HEADS]], axis=1)
    g_alog = (ga1 + ga2.reshape(1, N_HEADS)) * a
    g_dskip = gd.reshape(N_HEADS, HEAD).sum(axis=1).reshape(1, N_HEADS)
    grads = dict(meta_tokens=g_meta, norm_w=g_nw, w_in=g_win, conv_w=g_convw, conv_b=g_convb,
                 dt_bias=g_dtb128[:, :N_HEADS], a_log=g_alog, d_skip=g_dskip, sb_norm_w=g_sb, ssd_norm_w=g_ssd,
                 w_out=g_wout, final_norm_w=g_fin)
    return sq_err, g_x, grads


_MESH = pl.DeviceIdType.MESH
_ANY = pl.BlockSpec(memory_space=pl.ANY)


def _place():
    return lax.axis_index("x"), lax.axis_index("y"), lax.axis_index("c")


def _other_chips(x, y):
    return ((1 - x, y), (x, 1 - y), (1 - x, 1 - y))


def _gather_shards(arrays):
    n = len(arrays)

    def body(*refs):
        srcs, dsts = refs[:n], refs[n:2 * n]
        send_sems, recv_sems, local_sems = refs[2 * n:]
        x, y, c = _place()
        mine = 2 * x + y
        local = [pltpu.make_async_copy(srcs[a], dsts[a].at[mine], local_sems.at[a]) for a in range(n)]
        for cp in local:
            cp.start()
        remote = []
        for a in range(n):
            for k, (px, py) in enumerate(_other_chips(x, y)):
                cp = pltpu.make_async_remote_copy(
                    src_ref=srcs[a], dst_ref=dsts[a].at[mine],
                    send_sem=send_sems.at[a * 3 + k], recv_sem=recv_sems.at[a * 3 + k],
                    device_id=(px, py, c), device_id_type=_MESH)
                cp.start()
                remote.append(cp)
        for cp in remote:
            cp.wait_recv()
        for cp in remote:
            cp.wait_send()
        for cp in local:
            cp.wait()

    return pl.pallas_call(
        body, name="gather_shards",
        in_specs=[_ANY] * n, out_specs=[_ANY] * n,
        out_shape=[_SDS((N_CHIPS,) + a.shape, a.dtype) for a in arrays],
        scratch_shapes=[pltpu.SemaphoreType.DMA((3 * n,)), pltpu.SemaphoreType.DMA((3 * n,)),
                        pltpu.SemaphoreType.DMA((n,))],
    )(*arrays)


def _scatter_slabs(arrays):
    n = len(arrays)

    def body(*refs):
        srcs, dsts = refs[:n], refs[n:2 * n]
        send_sems, recv_sems, local_sems = refs[2 * n:]
        x, y, c = _place()
        mine = 2 * x + y
        local = [pltpu.make_async_copy(srcs[a].at[mine], dsts[a].at[mine], local_sems.at[a]) for a in range(n)]
        for cp in local:
            cp.start()
        remote = []
        for a in range(n):
            for k, (px, py) in enumerate(_other_chips(x, y)):
                cp = pltpu.make_async_remote_copy(
                    src_ref=srcs[a].at[2 * px + py], dst_ref=dsts[a].at[mine],
                    send_sem=send_sems.at[a * 3 + k], recv_sem=recv_sems.at[a * 3 + k],
                    device_id=(px, py, c), device_id_type=_MESH)
                cp.start()
                remote.append(cp)
        for cp in remote:
            cp.wait_recv()
        for cp in remote:
            cp.wait_send()
        for cp in local:
            cp.wait()

    return pl.pallas_call(
        body, name="scatter_slabs",
        in_specs=[_ANY] * n, out_specs=[_ANY] * n,
        out_shape=[_SDS(a.shape, a.dtype) for a in arrays],
        scratch_shapes=[pltpu.SemaphoreType.DMA((3 * n,)), pltpu.SemaphoreType.DMA((3 * n,)),
                        pltpu.SemaphoreType.DMA((n,))],
    )(*arrays)


def _swap_sibling(arrays):
    n = len(arrays)

    def body(*refs):
        srcs, dsts = refs[:n], refs[n:2 * n]
        send_sems, recv_sems = refs[2 * n:]
        x, y, c = _place()
        copies = []
        for a in range(n):
            cp = pltpu.make_async_remote_copy(
                src_ref=srcs[a], dst_ref=dsts[a], send_sem=send_sems.at[a], recv_sem=recv_sems.at[a],
                device_id=(x, y, 1 - c), device_id_type=_MESH)
            cp.start()
            copies.append(cp)
        for cp in copies:
            cp.wait_recv()
        for cp in copies:
            cp.wait_send()

    return pl.pallas_call(
        body, name="swap_sibling",
        in_specs=[_ANY] * n, out_specs=[_ANY] * n,
        out_shape=[_SDS(a.shape, a.dtype) for a in arrays],
        scratch_shapes=[pltpu.SemaphoreType.DMA((n,)), pltpu.SemaphoreType.DMA((n,))],
    )(*arrays)


N_DEV = 8
SMALL_ROWS = 32
SMALL_COLS = XBC_W


def _gather_small(packed):
    def body(src_ref, dst_ref, send_sems, recv_sems, local_sem):
        x, y, c = _place()
        me = 4 * x + 2 * y + c
        own = pltpu.make_async_copy(src_ref, dst_ref.at[me], local_sem)
        own.start()
        copies = []
        for k in range(1, N_DEV):
            bx, by, bc = (k >> 2) & 1, (k >> 1) & 1, k & 1
            peer = (x + bx - 2 * x * bx, y + by - 2 * y * by, c + bc - 2 * c * bc)
            cp = pltpu.make_async_remote_copy(
                src_ref=src_ref, dst_ref=dst_ref.at[me], send_sem=send_sems.at[k - 1], recv_sem=recv_sems.at[k - 1],
                device_id=peer, device_id_type=_MESH)
            cp.start()
            copies.append(cp)
        for cp in copies:
            cp.wait_recv()
        for cp in copies:
            cp.wait_send()
        own.wait()

    return pl.pallas_call(
        body, name="gather_small",
        in_specs=[pl.BlockSpec(memory_space=_VMEM)], out_specs=pl.BlockSpec(memory_space=_VMEM),
        out_shape=_SDS((N_DEV, SMALL_ROWS, SMALL_COLS), F32),
        scratch_shapes=[pltpu.SemaphoreType.DMA((N_DEV - 1,)), pltpu.SemaphoreType.DMA((N_DEV - 1,)),
                        pltpu.SemaphoreType.DMA],
    )(packed)


def _adamw(w, g, m, v):
    m = ADAM_B1 * m + (1.0 - ADAM_B1) * g
    v = ADAM_B2 * v + (1.0 - ADAM_B2) * (g * g)
    m_hat = m / (1.0 - ADAM_B1 ** ADAM_STEP)
    v_hat = v / (1.0 - ADAM_B2 ** ADAM_STEP)
    delta = -ADAM_LR * (m_hat / (jnp.sqrt(v_hat) + ADAM_EPS) + ADAM_WD * w)
    return delta, m, v


def _sum_slabs(slabs, name):
    _, r, c = slabs.shape
    tr = 128

    def body(s_ref, o_ref):
        o_ref[...] = ((s_ref[0].astype(F32) + s_ref[1].astype(F32)) + s_ref[2].astype(F32)) + s_ref[3].astype(F32)

    return pl.pallas_call(
        body, name=name, grid=(r // tr,),
        in_specs=[pl.BlockSpec((N_CHIPS, tr, c), lambda i: (0, i, 0))],
        out_specs=pl.BlockSpec((tr, c), lambda i: (i, 0)),
        out_shape=_SDS((r, c), F32),
        compiler_params=_params(("arbitrary",)),
    )(slabs)


def _update_big(w, m, v, s_own, s_sib, name):
    r, c = w.shape
    tr = 128

    def body(w_ref, m_ref, v_ref, a_ref, b_ref, g_ref, d_ref, mo_ref, vo_ref):
        g = a_ref[...] + b_ref[...]
        delta, m_new, v_new = _adamw(w_ref[...], g, m_ref[...], v_ref[...])
        g_ref[...] = g
        d_ref[...] = delta
        mo_ref[...] = m_new
        vo_ref[...] = v_new

    spec = pl.BlockSpec((tr, c), lambda i: (i, 0))
    return pl.pallas_call(
        body, name=name, grid=(r // tr,),
        in_specs=[spec] * 5, out_specs=[spec] * 4,
        out_shape=[_SDS((r, c), F32)] * 4,
        compiler_params=_params(("arbitrary",)),
    )(w, m, v, s_own, s_sib)


_ROW = dict(norm_w=0, sb_norm_w=1, ssd_norm_w=2, final_norm_w=3, conv_b=4, dt_bias=5, a_log=6, d_skip=7,
            conv_w=8, sq_err=12, meta_tokens=16)
_SMALL = ("meta_tokens", "norm_w", "conv_w", "conv_b", "dt_bias", "a_log", "d_skip", "sb_norm_w", "ssd_norm_w",
          "final_norm_w")


def _pack_small(sq_err, grads):
    def rowpad(a):
        return jnp.pad(a, ((0, 0), (0, SMALL_COLS - a.shape[1])))

    rows = [rowpad(grads[k]) for k in ("norm_w", "sb_norm_w", "ssd_norm_w", "final_norm_w", "conv_b", "dt_bias", "a_log", "d_skip")]
    rows.append(grads["conv_w"])
    rows.append(rowpad(sq_err))
    rows.append(jnp.zeros((3, SMALL_COLS), F32))
    rows.append(rowpad(grads["meta_tokens"]))
    return jnp.concatenate(rows, axis=0)


def _update_small(gathered, ws, ms, vs):
    names = _SMALL
    n = len(names)

    def body(*refs):
        g_ref = refs[0]
        w_refs, m_refs, v_refs = refs[1:1 + n], refs[1 + n:1 + 2 * n], refs[1 + 2 * n:1 + 3 * n]
        outs = refs[1 + 3 * n:]
        loss_ref = outs[0]
        go, do, mo, vo = outs[1:1 + n], outs[1 + n:1 + 2 * n], outs[1 + 2 * n:1 + 3 * n], outs[1 + 3 * n:1 + 4 * n]
        tot = g_ref[0]
        for d in range(1, N_DEV):
            tot = tot + g_ref[d]
        x, y, _ = _place()
        chip = 2 * x + y
        loss_ref[...] = jnp.broadcast_to(
            0.5 * jnp.sum(tot[_ROW["sq_err"]:_ROW["sq_err"] + 1, 0:D_MODEL], axis=1, keepdims=True) / D_MODEL, (1, 128))
        for idx, nm in enumerate(names):
            r0 = _ROW[nm]
            rows, cols = w_refs[idx].shape
            if nm in ("conv_w", "meta_tokens"):
                g = jnp.zeros((rows, cols), F32)
                for j in range(N_CHIPS):
                    g = g + jnp.where(chip == j, tot[r0:r0 + rows, j * cols:(j + 1) * cols], 0.0)
            else:
                g = tot[r0:r0 + rows, 0:cols]
            delta, m_new, v_new = _adamw(w_refs[idx][...], g, m_refs[idx][...], v_refs[idx][...])
            go[idx][...] = g
            do[idx][...] = delta
            mo[idx][...] = m_new
            vo[idx][...] = v_new

    shapes = [_SDS(ws[nm].shape, F32) for nm in names]
    vm = pl.BlockSpec(memory_space=_VMEM)
    res = pl.pallas_call(
        body, name="update_small",
        in_specs=[vm] * (1 + 3 * n), out_specs=[vm] * (1 + 4 * n),
        out_shape=[_SDS((1, 128), F32)] + shapes * 4,
    )(gathered, *[ws[nm] for nm in names], *[ms[nm] for nm in names], *[vs[nm] for nm in names])
    loss = res[0][0, 0]
    g = dict(zip(names, res[1:1 + n]))
    d = dict(zip(names, res[1 + n:1 + 2 * n]))
    m = dict(zip(names, res[1 + 2 * n:1 + 3 * n]))
    v = dict(zip(names, res[1 + 3 * n:1 + 4 * n]))
    return loss, g, d, m, v


_WEIGHTS = ("meta_tokens", "norm_w", "w_in", "conv_w", "conv_b", "dt_bias", "a_log", "d_skip", "sb_norm_w",
            "ssd_norm_w", "w_out", "final_norm_w")


def kernel(x, meta_tokens, norm_w, w_in, conv_w, conv_b, dt_bias, a_log, d_skip, sb_norm_w, ssd_norm_w, w_out, final_norm_w, loss_target, m_meta_tokens, m_norm_w, m_w_in, m_conv_w, m_conv_b, m_dt_bias, m_a_log, m_d_skip, m_sb_norm_w, m_ssd_norm_w, m_w_out, m_final_norm_w, v_meta_tokens, v_norm_w, v_w_in, v_conv_w, v_conv_b, v_dt_bias, v_a_log, v_d_skip, v_sb_norm_w, v_ssd_norm_w, v_w_out, v_final_norm_w):
    given = dict(meta_tokens=meta_tokens, norm_w=norm_w, w_in=w_in, conv_w=conv_w, conv_b=conv_b, dt_bias=dt_bias,
                 a_log=a_log, d_skip=d_skip, sb_norm_w=sb_norm_w, ssd_norm_w=ssd_norm_w, w_out=w_out,
                 final_norm_w=final_norm_w)
    mom = dict(meta_tokens=m_meta_tokens, norm_w=m_norm_w, w_in=m_w_in, conv_w=m_conv_w, conv_b=m_conv_b,
               dt_bias=m_dt_bias, a_log=m_a_log, d_skip=m_d_skip, sb_norm_w=m_sb_norm_w, ssd_norm_w=m_ssd_norm_w,
               w_out=m_w_out, final_norm_w=m_final_norm_w)
    var = dict(meta_tokens=v_meta_tokens, norm_w=v_norm_w, w_in=v_w_in, conv_w=v_conv_w, conv_b=v_conv_b,
               dt_bias=v_dt_bias, a_log=v_a_log, d_skip=v_d_skip, sb_norm_w=v_sb_norm_w, ssd_norm_w=v_ssd_norm_w,
               w_out=v_w_out, final_norm_w=v_final_norm_w)
    seq = x.shape[1]

    def two_d(a):
        return a.reshape((-1, a.shape[-1])) if a.ndim != 2 else a

    g_win, g_wout, g_meta, g_cw = _gather_shards(
        [w_in[0].astype(_MXU), w_out[0].astype(_MXU), meta_tokens, conv_w[0]])
    w_in_full = jnp.swapaxes(g_win, 0, 1).reshape(D_MODEL, D_IN)
    w_main = w_in_full[:, :N_MAIN]
    w_dt = jnp.pad(w_in_full[:, N_MAIN:], ((0, 0), (0, 128 - N_HEADS)))
    w_out_full = g_wout.reshape(2 * D_MODEL, D_MODEL)
    meta_full = jnp.swapaxes(g_meta, 0, 1).reshape(N_META, D_MODEL)
    conv_w_full = jnp.swapaxes(g_cw, 0, 1).reshape(4, XBC_W)

    sq_err, g_x, grads = _device_grads(
        x.reshape(seq, D_MODEL), loss_target.reshape(seq, D_MODEL), meta_full, norm_w, w_main, w_dt, conv_w_full,
        conv_b, dt_bias, a_log, d_skip, sb_norm_w, ssd_norm_w, w_out_full, final_norm_w.reshape(1, D_MODEL))

    slab_in = jnp.swapaxes(grads["w_in"].reshape(D_MODEL, N_CHIPS, W_IN_SHARD), 0, 1).astype(_MXU)
    slab_out = grads["w_out"].reshape(N_CHIPS, W_OUT_SHARD, D_MODEL).astype(_MXU)
    got_in, got_out = _scatter_slabs([slab_in, slab_out])
    s_in = _sum_slabs(got_in, "sum_w_in")
    s_out = _sum_slabs(got_out, "sum_w_out")
    sib_in, sib_out = _swap_sibling([s_in, s_out])
    big = dict(w_in=_update_big(w_in[0], m_w_in[0], v_w_in[0], s_in, sib_in, "update_w_in"),
               w_out=_update_big(w_out[0], m_w_out[0], v_w_out[0], s_out, sib_out, "update_w_out"))

    gathered = _gather_small(_pack_small(sq_err, grads))
    loss, sg, sd, sm, sv = _update_small(
        gathered, {k: two_d(given[k]) for k in _SMALL}, {k: two_d(mom[k]) for k in _SMALL},
        {k: two_d(var[k]) for k in _SMALL})

    out = {}
    for idx, group in enumerate((sg, sd, sm, sv)):
        for k in _SMALL:
            out[(idx, k)] = group[k].reshape(given[k].shape)
        for k in ("w_in", "w_out"):
            out[(idx, k)] = big[k][idx].reshape(given[k].shape)
    return (loss, g_x.reshape(x.shape), *[out[(idx, k)] for idx in range(4) for k in _WEIGHTS])
```

```python
import functools
import math

import jax
import jax.numpy as jnp
from jax import lax
from jax.experimental import pallas as pl
from jax.experimental.pallas import tpu as pltpu

F32 = jnp.float32
_MXU = jnp.bfloat16

D_MODEL = 1024
N_META = 16
PAD = 112
OFF = PAD + N_META
TM = 128
CHUNK = 64
SB_W = 1024
SSD_W = 1024
N_HEADS = 16
HEAD = 64
N_GROUPS = 2
N_STATE = 128
XBC_W = SSD_W + 2 * N_GROUPS * N_STATE
N_MAIN = 4 * SB_W + SSD_W + XBC_W
COL_GATE = 3 * SB_W
COL_Z = 4 * SB_W
COL_XBC = 5 * SB_W
D_IN = N_MAIN + N_HEADS
EPS = 1e-5
N_CHIPS = 4
W_IN_SHARD = D_IN // N_CHIPS
W_OUT_SHARD = 2 * D_MODEL // N_CHIPS

ADAM_LR = 0.001
ADAM_B1 = 0.9
ADAM_B2 = 0.999
ADAM_EPS = 1e-08
ADAM_WD = 0.01
ADAM_STEP = 10

_SDS = jax.ShapeDtypeStruct
_NT = (((1,), (1,)), ((), ()))
_TN = (((0,), (0,)), ((), ()))
_VMEM = pltpu.VMEM


def _params(sem=None, vmem_mb=None):
    kw = {}
    if sem is not None:
        kw["dimension_semantics"] = sem
    if vmem_mb is not None:
        kw["vmem_limit_bytes"] = vmem_mb * 1024 * 1024
    return pltpu.CompilerParams(**kw)


def _mm(a, b):
    return jnp.dot(a.astype(_MXU), b.astype(_MXU), preferred_element_type=F32)


def _mm_nt(a, b):
    return lax.dot_general(a.astype(_MXU), b.astype(_MXU), _NT, preferred_element_type=F32)


def _mm_tn(a, b):
    return lax.dot_general(a.astype(_MXU), b.astype(_MXU), _TN, preferred_element_type=F32)


def _split(x, parts):
    out = []
    r = x
    for _ in range(parts):
        p = r.astype(_MXU)
        out.append(p)
        r = r - p.astype(F32)
    return out


def _sel_right(x, m01, parts=3):
    acc = None
    for p in _split(x, parts):
        t = jnp.dot(p, m01, preferred_element_type=F32)
        acc = t if acc is None else acc + t
    return acc


def _sel_left(m01, x, parts=3):
    acc = None
    for p in _split(x, parts):
        t = jnp.dot(m01, p, preferred_element_type=F32)
        acc = t if acc is None else acc + t
    return acc


def _iota(shape, axis):
    return lax.broadcasted_iota(jnp.int32, shape, axis)


def _sigmoid(x):
    return 1.0 / (1.0 + jnp.exp(-x))


def _prep(x2d, meta_full, norm_w):
    seq = x2d.shape[0]
    lp = seq + OFF
    nb = lp // TM

    def body(x_ref, meta_ref, w_ref, h0_ref, u_ref, ut_ref):
        i = pl.program_id(0)

        @pl.when(i == 0)
        def _():
            h0_ref[...] = jnp.concatenate([jnp.zeros((PAD, D_MODEL), F32), meta_ref[...]], axis=0)

        @pl.when(i > 0)
        def _():
            h0_ref[...] = x_ref[...]

        h = h0_ref[...]
        rs = lax.rsqrt(jnp.mean(h * h, axis=-1, keepdims=True) + EPS)
        u = (h * rs * w_ref[...]).astype(_MXU)
        u_ref[...] = u
        ut_ref[...] = u.T

    return pl.pallas_call(
        body, name="prep", grid=(nb,),
        in_specs=[pl.BlockSpec((TM, D_MODEL), lambda i: (jnp.maximum(i - 1, 0), 0)),
                  pl.BlockSpec((N_META, D_MODEL), lambda i: (0, 0)),
                  pl.BlockSpec((1, D_MODEL), lambda i: (0, 0))],
        out_specs=[pl.BlockSpec((TM, D_MODEL), lambda i: (i, 0)),
                   pl.BlockSpec((TM, D_MODEL), lambda i: (i, 0)),
                   pl.BlockSpec((D_MODEL, TM), lambda i: (0, i))],
        out_shape=[_SDS((lp, D_MODEL), F32), _SDS((lp, D_MODEL), _MXU), _SDS((D_MODEL, lp), _MXU)],
        compiler_params=_params(("arbitrary",)),
    )(x2d, meta_full, norm_w)


def _inproj(u, w_main, w_dt):
    lp = u.shape[0]
    tn = 512

    def body(u_ref, w_ref, wdt_ref, o_ref, odt_ref):
        o_ref[...] = jnp.dot(u_ref[...], w_ref[...], preferred_element_type=F32)

        @pl.when(pl.program_id(0) == 0)
        def _():
            odt_ref[...] = jnp.dot(u_ref[...], wdt_ref[...], preferred_element_type=F32)

    return pl.pallas_call(
        body, name="inproj", grid=(N_MAIN // tn,),
        in_specs=[pl.BlockSpec((lp, D_MODEL), lambda j: (0, 0)),
                  pl.BlockSpec((D_MODEL, tn), lambda j: (0, j)),
                  pl.BlockSpec((D_MODEL, 128), lambda j: (0, 0))],
        out_specs=[pl.BlockSpec((lp, tn), lambda j: (0, j)),
                   pl.BlockSpec((lp, 128), lambda j: (0, 0))],
        out_shape=[_SDS((lp, N_MAIN), F32), _SDS((lp, 128), F32)],
        compiler_params=_params(("arbitrary",), 48),
    )(u, w_main, w_dt)


SB_WINDOW = 3
SB_DEAD = -104.0


def _sb_logs(qh, kwin):
    z = lax.dot_general(qh, kwin, _NT, preferred_element_type=F32)
    e = jnp.exp(-jnp.abs(z))
    l1p = jnp.log(1.0 + e)
    lk_full = -(jnp.maximum(z, 0.0) + l1p)
    ls = jnp.minimum(z, 0.0) - l1p
    return z, e, ls, lk_full


def _blk(a, b):
    return a[:, b * TM:(b + 1) * TM]


def _stacked_sel(blocks, m01):
    n = len(blocks)
    pieces = [_split(b, 2) for b in blocks]
    stacked = jnp.concatenate([p[0] for p in pieces] + [p[1] for p in pieces], axis=0)
    res = jnp.dot(stacked, m01, preferred_element_type=F32)
    return [res[j * TM:(j + 1) * TM] + res[(n + j) * TM:(n + j + 1) * TM] for j in range(n)]


def _sb_weights(ls, lk_full, run, last_mask, upper, n):
    lk = [_blk(lk_full, b) for b in range(n)]
    lk[n - 1] = jnp.where(last_mask, lk[n - 1], 0.0)
    aft = _stacked_sel(lk, upper)
    w = [None] * n
    for b in range(n - 1, -1, -1):
        wb = jnp.exp(_blk(ls, b) + aft[b] + run)
        w[b] = jnp.where(last_mask, wb, 0.0) if b == n - 1 else wb
        run = run + jnp.sum(lk[b], axis=1, keepdims=True)
    return w, run


def _sb_fwd(proj):
    lp = proj.shape[0]
    nb = lp // TM

    def body(q_ref, k_ref, v_ref, o_ref, olo_ref, acc, run_scr):
        i = pl.program_id(1)
        lane = _iota((TM, TM), 1)
        row = _iota((TM, TM), 0)
        head0 = lane < HEAD
        upper = (row > lane).astype(_MXU)
        strict = lane < row
        q = q_ref[...] * (1.0 / math.sqrt(HEAD))
        qh = (jnp.where(head0, q, 0.0).astype(_MXU), jnp.where(head0, 0.0, q).astype(_MXU))

        def key_set(first, n, last_mask):
            off = pl.multiple_of(first * TM, TM)
            kwin = k_ref[pl.ds(off, n * TM), :].astype(_MXU)
            vwin = v_ref[pl.ds(off, n * TM), :].astype(_MXU)
            alive = None
            for hh in range(2):
                run = run_scr[hh][:, 0:1]
                _, _, ls, lk_full = _sb_logs(qh[hh], kwin)
                w, run = _sb_weights(ls, lk_full, run, last_mask, upper, n)
                pieces = [_split(wb, 2) for wb in w]
                stacked = jnp.concatenate(
                    [jnp.concatenate([p[0] for p in pieces], axis=1), jnp.concatenate([p[1] for p in pieces], axis=1)], axis=0)
                res = jnp.dot(stacked, vwin, preferred_element_type=F32)
                acc[hh] += res[0:TM]
                acc[2 + hh] += res[TM:2 * TM]
                run_scr[hh] = jnp.broadcast_to(run, (TM, TM))
                top = jnp.max(run)
                alive = top if alive is None else jnp.maximum(alive, top)
            return (alive > SB_DEAD).astype(jnp.int32)

        acc[...] = jnp.zeros_like(acc)
        run_scr[...] = jnp.zeros_like(run_scr)

        @pl.when(i >= SB_WINDOW - 1)
        def _():
            key_set(i - (SB_WINDOW - 1), SB_WINDOW, strict)

        start = jnp.where(i >= SB_WINDOW - 1, i - SB_WINDOW, i)
        alive0 = (jnp.max(run_scr[...]) > SB_DEAD).astype(jnp.int32)

        def cond(c):
            return jnp.logical_and(c[0] >= 0, c[1] > 0)

        def step(c):
            kb = c[0]
            return kb - 1, key_set(kb, 1, jnp.logical_or(strict, kb < i))

        lax.while_loop(cond, step, (start, alive0))
        o_ref[...] = jnp.where(head0, acc[0], acc[1])
        olo_ref[...] = jnp.where(head0, acc[2], acc[3])

    npair = SB_W // TM
    blk = pl.BlockSpec((TM, TM), lambda p, i: (i, p))
    return pl.pallas_call(
        body, name="sb_fwd", grid=(npair, nb),
        in_specs=[blk,
                  pl.BlockSpec((lp, TM), lambda p, i: (0, npair + p)),
                  pl.BlockSpec((lp, TM), lambda p, i: (0, 2 * npair + p))],
        out_specs=[blk, blk],
        out_shape=[_SDS((lp, SB_W), F32), _SDS((lp, SB_W), F32)],
        scratch_shapes=[pltpu.VMEM((4, TM, TM), F32), pltpu.VMEM((2, TM, TM), F32)],
        compiler_params=_params(("arbitrary", "arbitrary")),
    )(proj, proj, proj)


def _sb_bwd(proj, o_sb, o_lo, d_o):
    lp = proj.shape[0]
    nb = lp // TM
    scale = 1.0 / math.sqrt(HEAD)

    def body(q_ref, k_ref, v_ref, o_ref, olo_ref, do_ref, dq_ref, dk_ref, dv_ref, dq_acc, run_scr, gsum_scr):
        i = pl.program_id(1)

        @pl.when(i == 0)
        def _():
            dk_ref[...] = jnp.zeros_like(dk_ref)
            dv_ref[...] = jnp.zeros_like(dv_ref)

        lane = _iota((TM, TM), 1)
        row = _iota((TM, TM), 0)
        head0 = lane < HEAD
        hmask = (head0, jnp.logical_not(head0))
        upper = (row > lane).astype(_MXU)
        lower_incl = (row >= lane).astype(_MXU)
        strict = lane < row
        q = q_ref[...] * scale
        do = do_ref[...]
        prod = do.astype(_MXU).astype(F32) * (o_ref[...] + olo_ref[...])
        qh = tuple(jnp.where(m, q, 0.0).astype(_MXU) for m in hmask)
        doh = tuple(jnp.where(m, do, 0.0).astype(_MXU) for m in hmask)
        gtot = tuple(jnp.sum(jnp.where(m, prod, 0.0), axis=1, keepdims=True) for m in hmask)

        def key_set(first, n, last_mask):
            off = pl.multiple_of(first * TM, TM)
            kf = k_ref[pl.ds(off, n * TM), :]
            kwin = kf.astype(_MXU)
            vwin = v_ref[pl.ds(off, n * TM), :].astype(_MXU)
            dk_win = None
            alive = None
            for hh in range(2):
                run = run_scr[hh][:, 0:1]
                gsum = gsum_scr[hh][:, 0:1]
                z, e, ls, lk_full = _sb_logs(qh[hh], kwin)
                w, run = _sb_weights(ls, lk_full, run, last_mask, upper, n)
                r = 1.0 / (1.0 + e)
                er = e * r
                pos = z >= 0.0
                beta = jnp.where(pos, r, er)
                one_m_beta = jnp.where(pos, er, r)
                dw = lax.dot_general(doh[hh], vwin, _NT, preferred_element_type=F32)
                g = [_blk(dw, b) * w[b] for b in range(n)]
                suffix = _stacked_sel(g, lower_incl)
                dz = [None] * n
                for b in range(n - 1, -1, -1):
                    prefix = gtot[hh] - gsum - suffix[b]
                    d = g[b] * _blk(one_m_beta, b) - _blk(beta, b) * prefix
                    dz[b] = (jnp.where(last_mask, d, 0.0) if b == n - 1 else d).astype(_MXU)
                    gsum = gsum + jnp.sum(g[b], axis=1, keepdims=True)
                dzw = jnp.concatenate(dz, axis=1)
                ww = jnp.concatenate([wb.astype(_MXU) for wb in w], axis=1)
                kh = jnp.where(hmask[hh][0:1, :], kf, 0.0).astype(_MXU)
                dq_acc[...] += jnp.dot(dzw, kh, preferred_element_type=F32)
                dk_h = lax.dot_general(dzw, qh[hh], _TN, preferred_element_type=F32)
                dv_h = lax.dot_general(ww, doh[hh], _TN, preferred_element_type=F32)
                dk_win = (dk_h, dv_h) if dk_win is None else (dk_win[0] + dk_h, dk_win[1] + dv_h)
                run_scr[hh] = jnp.broadcast_to(run, (TM, TM))
                gsum_scr[hh] = jnp.broadcast_to(gsum, (TM, TM))
                top = jnp.max(run)
                alive = top if alive is None else jnp.maximum(alive, top)
            dk_ref[pl.ds(off, n * TM), :] += dk_win[0]
            dv_ref[pl.ds(off, n * TM), :] += dk_win[1]
            return (alive > SB_DEAD).astype(jnp.int32)

        dq_acc[...] = jnp.zeros_like(dq_acc)
        run_scr[...] = jnp.zeros_like(run_scr)
        gsum_scr[...] = jnp.zeros_like(gsum_scr)

        @pl.when(i >= SB_WINDOW - 1)
        def _():
            key_set(i - (SB_WINDOW - 1), SB_WINDOW, strict)

        start = jnp.where(i >= SB_WINDOW - 1, i - SB_WINDOW, i)
        alive0 = (jnp.max(run_scr[...]) > SB_DEAD).astype(jnp.int32)

        def cond(c):
            return jnp.logical_and(c[0] >= 0, c[1] > 0)

        def step(c):
            kb = c[0]
            return kb - 1, key_set(kb, 1, jnp.logical_or(strict, kb < i))

        lax.while_loop(cond, step, (start, alive0))
        dq_ref[...] = dq_acc[...] * scale

    npair = SB_W // TM
    blk = pl.BlockSpec((TM, TM), lambda p, i: (i, p))
    col = pl.BlockSpec((lp, TM), lambda p, i: (0, p))
    return pl.pallas_call(
        body, name="sb_bwd", grid=(npair, nb),
        in_specs=[blk,
                  pl.BlockSpec((lp, TM), lambda p, i: (0, npair + p)),
                  pl.BlockSpec((lp, TM), lambda p, i: (0, 2 * npair + p)),
                  blk, blk, blk],
        out_specs=[blk, col, col],
        out_shape=[_SDS((lp, SB_W), F32)] * 3,
        scratch_shapes=[pltpu.VMEM((TM, TM), F32), pltpu.VMEM((2, TM, TM), F32), pltpu.VMEM((2, TM, TM), F32)],
        compiler_params=_params(("arbitrary", "arbitrary")),
    )(proj, proj, proj, o_sb, o_lo, d_o)


def _conv_pre(x_ref, w_ref, b_ref, lp):
    n = lp - 8
    w = w_ref[...]
    pre = (x_ref[pl.ds(5, n), :] * w[0:1, :] + x_ref[pl.ds(6, n), :] * w[1:2, :]
           + x_ref[pl.ds(7, n), :] * w[2:3, :] + x_ref[pl.ds(8, n), :] * w[3:4, :]) + b_ref[...]
    live = (_iota((n, 128), 0) + 8) >= PAD
    return pre, live


def _conv_fwd(proj, dt_raw, conv_w, conv_b, dt_bias128):
    lp = proj.shape[0]
    nblk = XBC_W // 128
    c0 = COL_XBC // 128

    def body(x_ref, w_ref, b_ref, dtr_ref, dtb_ref, o_ref, dt_ref):
        pre, live = _conv_pre(x_ref, w_ref, b_ref, lp)
        act = pre * _sigmoid(pre)
        o_ref[pl.ds(0, 8), :] = jnp.zeros((8, 128), F32)
        o_ref[pl.ds(8, lp - 8), :] = jnp.where(live, act, 0.0)

        @pl.when(pl.program_id(0) == 0)
        def _():
            s = dtr_ref[...] + dtb_ref[...]
            sp = jnp.maximum(s, 0.0) + jnp.log(1.0 + jnp.exp(-jnp.abs(s)))
            dt_ref[...] = jnp.where(_iota((lp, 128), 0) >= PAD, sp, 0.0)

    return pl.pallas_call(
        body, name="conv_fwd", grid=(nblk,),
        in_specs=[pl.BlockSpec((lp, 128), lambda j: (0, c0 + j)),
                  pl.BlockSpec((4, 128), lambda j: (0, j)),
                  pl.BlockSpec((1, 128), lambda j: (0, j)),
                  pl.BlockSpec((lp, 128), lambda j: (0, 0)),
                  pl.BlockSpec((1, 128), lambda j: (0, 0))],
        out_specs=[pl.BlockSpec((lp, 128), lambda j: (0, j)),
                   pl.BlockSpec((lp, 128), lambda j: (0, 0))],
        out_shape=[_SDS((lp, XBC_W), F32), _SDS((lp, 128), F32)],
        compiler_params=_params(("arbitrary",)),
    )(proj, conv_w, conv_b, dt_raw, dt_bias128)


def _conv_bwd(proj, dt_raw, conv_w, conv_b, dt_bias128, d_xbc, d_dt128):
    lp = proj.shape[0]
    nblk = XBC_W // 128
    c0 = COL_XBC // 128
    n = lp - 8

    def body(x_ref, w_ref, b_ref, dtr_ref, dtb_ref, dy_ref, ddt_ref,
             dx_ref, gw_ref, gb_ref, ddtr_ref, gdtb_ref, scr):
        pre, live = _conv_pre(x_ref, w_ref, b_ref, lp)
        sg = _sigmoid(pre)
        dpre = jnp.where(live, dy_ref[pl.ds(8, n), :] * (sg * (1.0 + pre * (1.0 - sg))), 0.0)
        gb_ref[...] = jnp.sum(dpre, axis=0, keepdims=True)
        gw_ref[...] = jnp.concatenate(
            [jnp.sum(dpre * x_ref[pl.ds(5 + k, n), :], axis=0, keepdims=True) for k in range(4)], axis=0)
        scr[pl.ds(0, 8), :] = jnp.zeros((8, 128), F32)
        scr[pl.ds(8, n), :] = dpre
        scr[pl.ds(lp, 8), :] = jnp.zeros((8, 128), F32)
        w = w_ref[...]
        dx_ref[pl.ds(0, 8), :] = jnp.zeros((8, 128), F32)
        dx_ref[pl.ds(8, n), :] = (scr[pl.ds(8, n), :] * w[3:4, :] + scr[pl.ds(9, n), :] * w[2:3, :]
                                  + scr[pl.ds(10, n), :] * w[1:2, :] + scr[pl.ds(11, n), :] * w[0:1, :])

        @pl.when(pl.program_id(0) == 0)
        def _():
            s = dtr_ref[...] + dtb_ref[...]
            d = jnp.where(_iota((lp, 128), 0) >= PAD, ddt_ref[...] * _sigmoid(s), 0.0)
            ddtr_ref[...] = d
            gdtb_ref[...] = jnp.sum(d, axis=0, keepdims=True)

    colblk = pl.BlockSpec((lp, 128), lambda j: (0, j))
    full128 = pl.BlockSpec((lp, 128), lambda j: (0, 0))
    return pl.pallas_call(
        body, name="conv_bwd", grid=(nblk,),
        in_specs=[pl.BlockSpec((lp, 128), lambda j: (0, c0 + j)),
                  pl.BlockSpec((4, 128), lambda j: (0, j)),
                  pl.BlockSpec((1, 128), lambda j: (0, j)),
                  full128, pl.BlockSpec((1, 128), lambda j: (0, 0)),
                  colblk, full128],
        out_specs=[colblk, pl.BlockSpec((4, 128), lambda j: (0, j)), pl.BlockSpec((1, 128), lambda j: (0, j)),
                   full128, pl.BlockSpec((1, 128), lambda j: (0, 0))],
        out_shape=[_SDS((lp, XBC_W), F32), _SDS((4, XBC_W), F32), _SDS((1, XBC_W), F32),
                   _SDS((lp, 128), F32), _SDS((1, 128), F32)],
        scratch_shapes=[pltpu.VMEM((lp + 8, 128), F32)],
        compiler_params=_params(("arbitrary",)),
    )(proj, conv_w, conv_b, dt_raw, dt_bias128, d_xbc, d_dt128)


def _ssd_pieces(dt, dt_t, a, a_t):
    r64 = _iota((CHUNK, CHUNK), 0)
    c64 = _iota((CHUNK, CHUNK), 1)
    tril = c64 <= r64
    tril01 = tril.astype(_MXU)
    triu01 = (r64 <= c64).astype(_MXU)
    expand = (lax.shift_right_logical(_iota((N_HEADS, SSD_W), 1), 6) == _iota((N_HEADS, SSD_W), 0)).astype(_MXU)
    acum = _sel_left(tril01, dt * a)
    acum_t = _sel_right(dt_t * a_t, triu01)
    ax = _sel_right(acum, expand)
    dtx = _sel_right(dt, expand)
    return tril, expand, acum, acum_t, ax, dtx


def _seg_matrix():
    return (lax.shift_right_logical(_iota((SSD_W, N_HEADS), 0), 6) == _iota((SSD_W, N_HEADS), 1)).astype(_MXU)


def _head_decay(ax, acum_t, h, tril):
    col = ax[:, h * HEAD:(h + 1) * HEAD]
    rowv = acum_t[h:h + 1, :]
    return jnp.where(tril, jnp.exp(jnp.minimum(col - rowv, 0.0)), 0.0)


def _ssd_fwd(xbc, dt_c, dt_tc, a, a_t, dskip_x):
    lp = xbc.shape[0]
    nc = lp // CHUNK
    gw = SSD_W // N_GROUPS
    hpg = N_HEADS // N_GROUPS

    def body(x_ref, dt_ref, dtt_ref, a_ref, at_ref, d_ref, y_ref, st_ref, state):
        c = pl.program_id(0)

        @pl.when(c == 0)
        def _():
            state[...] = jnp.zeros_like(state)

        st_ref[0] = state[...]
        tril, _, _, acum_t, ax, dtx = _ssd_pieces(dt_ref[0], dtt_ref[0], a_ref[...], at_ref[...])
        x = x_ref[:, 0:SSD_W]
        xdt = x * dtx
        ea = jnp.exp(ax)
        aex = ax[CHUNK - 1:CHUNK, :]
        wd = jnp.exp(aex - ax)
        eae = jnp.exp(aex)
        xw = xdt * wd
        y_ref[...] = x * d_ref[...]
        for g in range(N_GROUPS):
            gs = slice(g * gw, (g + 1) * gw)
            rs = slice(g * N_STATE, (g + 1) * N_STATE)
            bg = x_ref[:, SSD_W + g * N_STATE:SSD_W + (g + 1) * N_STATE]
            cg = x_ref[:, SSD_W + N_GROUPS * N_STATE + g * N_STATE:SSD_W + N_GROUPS * N_STATE + (g + 1) * N_STATE]
            sg = state[rs, :]
            cb = _mm_nt(cg, bg)
            y_ref[:, gs] += _mm(cg, sg) * ea[:, gs]
            for r in range(hpg):
                h = g * hpg + r
                hs = slice(h * HEAD, (h + 1) * HEAD)
                m = cb * _head_decay(ax, acum_t, h, tril)
                y_ref[:, hs] += _mm(m, xdt[:, hs])
            state[rs, :] = sg * eae[:, gs] + _mm_tn(bg, xw[:, gs])

    return pl.pallas_call(
        body, name="ssd_fwd", grid=(nc,),
        in_specs=[pl.BlockSpec((CHUNK, XBC_W), lambda c: (c, 0)),
                  pl.BlockSpec((1, CHUNK, N_HEADS), lambda c: (c, 0, 0)),
                  pl.BlockSpec((1, N_HEADS, CHUNK), lambda c: (c, 0, 0)),
                  pl.BlockSpec((1, N_HEADS), lambda c: (0, 0)),
                  pl.BlockSpec((N_HEADS, 1), lambda c: (0, 0)),
                  pl.BlockSpec((1, SSD_W), lambda c: (0, 0))],
        out_specs=[pl.BlockSpec((CHUNK, SSD_W), lambda c: (c, 0)),
                   pl.BlockSpec((1, N_GROUPS * N_STATE, gw), lambda c: (c, 0, 0))],
        out_shape=[_SDS((lp, SSD_W), F32), _SDS((nc, N_GROUPS * N_STATE, gw), F32)],
        scratch_shapes=[pltpu.VMEM((N_GROUPS * N_STATE, gw), F32)],
        compiler_params=_params(("arbitrary",)),
    )(xbc, dt_c, dt_tc, a, a_t, dskip_x)


def _ssd_bwd(xbc, dt_c, dt_tc, a, a_t, dskip_x, states, d_y):
    lp = xbc.shape[0]
    nc = lp // CHUNK
    gw = SSD_W // N_GROUPS
    hpg = N_HEADS // N_GROUPS

    def body(x_ref, dt_ref, dtt_ref, a_ref, at_ref, d_ref, st_ref, dy_ref,
             dx_ref, ddta_ref, ddtb_ref, ga1_ref, ga2_ref, gd_ref, dstate, dxdt_scr, z_scr, yoff_scr, sds_scr):
        c = pl.program_id(0)

        @pl.when(c == 0)
        def _():
            dstate[...] = jnp.zeros_like(dstate)
            ga1_ref[...] = jnp.zeros_like(ga1_ref)
            ga2_ref[...] = jnp.zeros_like(ga2_ref)
            gd_ref[...] = jnp.zeros_like(gd_ref)

        dt = dt_ref[0]
        dt_t = dtt_ref[0]
        a = a_ref[...]
        a_t = at_ref[...]
        tril, _, acum, acum_t, ax, dtx = _ssd_pieces(dt, dt_t, a, a_t)
        seg = _seg_matrix()
        x = x_ref[:, 0:SSD_W]
        dy = dy_ref[...]
        xdt = x * dtx
        ea = jnp.exp(ax)
        aex = ax[CHUNK - 1:CHUNK, :]
        wd = jnp.exp(aex - ax)
        eae = jnp.exp(aex)
        xw = xdt * wd
        edy = ea * dy
        lane16 = _iota((CHUNK, N_HEADS), 1)
        row16 = _iota((N_HEADS, CHUNK), 0)
        da_col = jnp.zeros((CHUNK, N_HEADS), F32)
        da_row = jnp.zeros((N_HEADS, CHUNK), F32)
        for g in range(N_GROUPS):
            gs = slice(g * gw, (g + 1) * gw)
            rs = slice(g * N_STATE, (g + 1) * N_STATE)
            bcol = slice(SSD_W + g * N_STATE, SSD_W + (g + 1) * N_STATE)
            ccol = slice(SSD_W + N_GROUPS * N_STATE + g * N_STATE, SSD_W + N_GROUPS * N_STATE + (g + 1) * N_STATE)
            bg = x_ref[:, bcol]
            cg = x_ref[:, ccol]
            sg = st_ref[0, rs, :]
            dsn = dstate[rs, :]
            cb = _mm_nt(cg, bg)
            z_scr[:, gs] = _mm(bg, dsn)
            yoff_scr[:, gs] = _mm(cg, sg) * ea[:, gs]
            sds_scr[:, gs] = jnp.broadcast_to(jnp.sum(dsn * sg, axis=0, keepdims=True), (8, gw))
            dcb = jnp.zeros((CHUNK, CHUNK), F32)
            for r in range(hpg):
                h = g * hpg + r
                hs = slice(h * HEAD, (h + 1) * HEAD)
                dec = _head_decay(ax, acum_t, h, tril)
                m = cb * dec
                t1 = _mm_nt(dy[:, hs], xdt[:, hs])
                dcb = dcb + dec * t1
                tm = m * t1
                da_col = da_col + jnp.where(lane16 == h, jnp.sum(tm, axis=1, keepdims=True), 0.0)
                da_row = da_row - jnp.where(row16 == h, jnp.sum(tm, axis=0, keepdims=True), 0.0)
                dxdt_scr[:, hs] = _mm_tn(m, dy[:, hs])
            dx_ref[:, ccol] = _mm(dcb, bg) + _mm_nt(edy[:, gs], sg)
            dx_ref[:, bcol] = _mm_tn(dcb, cg) + _mm_nt(xw[:, gs], dsn)
            dstate[rs, :] = eae[:, gs] * dsn + _mm_tn(cg, edy[:, gs])
        zf = z_scr[...]
        dxdt = dxdt_scr[...] + wd * zf
        t3 = _sel_right(xw * zf, seg)
        da_col = da_col + _sel_right(dy * yoff_scr[...], seg) - t3
        aend = acum[CHUNK - 1:CHUNK, :]
        sd = _sel_right(sds_scr[...], seg)[0:1, :] * jnp.exp(aend)
        last = jnp.sum(t3, axis=0, keepdims=True) + sd
        da_col = da_col + jnp.where(_iota((CHUNK, N_HEADS), 0) == CHUNK - 1, last, 0.0)
        r64 = _iota((CHUNK, CHUNK), 0)
        c64 = _iota((CHUNK, CHUNK), 1)
        ddta1 = _sel_left((c64 >= r64).astype(_MXU), da_col)
        ddta2 = _sel_right(da_row, (r64 >= c64).astype(_MXU))
        ddta_ref[0] = a * ddta1 + _sel_right(dxdt * x, seg)
        ddtb_ref[0] = a_t * ddta2
        ga1_ref[...] += jnp.sum(dt * ddta1, axis=0, keepdims=True)
        ga2_ref[...] += jnp.sum(dt_t * ddta2, axis=1, keepdims=True)
        dx_ref[:, 0:SSD_W] = dxdt * dtx + d_ref[...] * dy
        gd_ref[...] += jnp.sum(dy * x, axis=0, keepdims=True)

    rev = lambda c: (nc - 1 - c, 0)
    rev3 = lambda c: (nc - 1 - c, 0, 0)
    return pl.pallas_call(
        body, name="ssd_bwd", grid=(nc,),
        in_specs=[pl.BlockSpec((CHUNK, XBC_W), rev),
                  pl.BlockSpec((1, CHUNK, N_HEADS), rev3),
                  pl.BlockSpec((1, N_HEADS, CHUNK), rev3),
                  pl.BlockSpec((1, N_HEADS), lambda c: (0, 0)),
                  pl.BlockSpec((N_HEADS, 1), lambda c: (0, 0)),
                  pl.BlockSpec((1, SSD_W), lambda c: (0, 0)),
                  pl.BlockSpec((1, N_GROUPS * N_STATE, gw), rev3),
                  pl.BlockSpec((CHUNK, SSD_W), rev)],
        out_specs=[pl.BlockSpec((CHUNK, XBC_W), rev),
                   pl.BlockSpec((1, CHUNK, N_HEADS), rev3),
                   pl.BlockSpec((1, N_HEADS, CHUNK), rev3),
                   pl.BlockSpec((1, N_HEADS), lambda c: (0, 0)),
                   pl.BlockSpec((N_HEADS, 1), lambda c: (0, 0)),
                   pl.BlockSpec((1, SSD_W), lambda c: (0, 0))],
        out_shape=[_SDS((lp, XBC_W), F32), _SDS((nc, CHUNK, N_HEADS), F32), _SDS((nc, N_HEADS, CHUNK), F32),
                   _SDS((1, N_HEADS), F32), _SDS((N_HEADS, 1), F32), _SDS((1, SSD_W), F32)],
        scratch_shapes=[pltpu.VMEM((N_GROUPS * N_STATE, gw), F32), pltpu.VMEM((CHUNK, SSD_W), F32),
                        pltpu.VMEM((CHUNK, SSD_W), F32), pltpu.VMEM((CHUNK, SSD_W), F32),
                        pltpu.VMEM((8, SSD_W), F32)],
        compiler_params=_params(("arbitrary",)),
    )(xbc, dt_c, dt_tc, a, a_t, dskip_x, states, d_y)


def _gated_norm(o, gate, w):
    sg = _sigmoid(gate)
    p = o * (gate * sg)
    rs = lax.rsqrt(jnp.mean(p * p, axis=-1, keepdims=True) + EPS)
    n = p * rs
    return sg, rs, n, n * w


def _tail_fwd(o_sb, o_ssd, proj, h0, target, w_out, sb_w, ssd_w, fin_w):
    lp = o_sb.shape[0]
    nb = lp // TM
    row = lambda i: (i, 0)
    one = lambda i: (0, 0)

    def body(osb_ref, gate_ref, ossd_ref, z_ref, h0_ref, tgt_ref, wo_ref, sbw_ref, ssdw_ref, fw_ref,
             dh1_ref, loss_ref, gfw_ref):
        i = pl.program_id(0)

        @pl.when(i == 0)
        def _():
            loss_ref[...] = jnp.zeros_like(loss_ref)
            gfw_ref[...] = jnp.zeros_like(gfw_ref)

        y1 = _gated_norm(osb_ref[...], gate_ref[...], sbw_ref[...])[3]
        y2 = _gated_norm(ossd_ref[...], z_ref[...], ssdw_ref[...])[3]
        h1 = (h0_ref[...] + _mm(y1, wo_ref[0:SB_W, :])) + _mm(y2, wo_ref[SB_W:SB_W + SSD_W, :])
        rs1 = lax.rsqrt(jnp.mean(h1 * h1, axis=-1, keepdims=True) + EPS)
        n1 = h1 * rs1
        fw = fw_ref[...]
        diff = jnp.where(i > 0, n1 * fw - tgt_ref[...], 0.0)
        loss_ref[...] += jnp.sum(diff * diff, axis=0, keepdims=True)
        d_out = diff * (1.0 / D_MODEL)
        gfw_ref[...] += jnp.sum(d_out * n1, axis=0, keepdims=True)
        g = d_out * fw
        dh1_ref[...] = rs1 * (g - n1 * jnp.mean(g * n1, axis=-1, keepdims=True))

    return pl.pallas_call(
        body, name="tail_fwd", grid=(nb,),
        in_specs=[pl.BlockSpec((TM, SB_W), row),
                  pl.BlockSpec((TM, SB_W), lambda i: (i, COL_GATE // SB_W)),
                  pl.BlockSpec((TM, SSD_W), row),
                  pl.BlockSpec((TM, SSD_W), lambda i: (i, COL_Z // SSD_W)),
                  pl.BlockSpec((TM, D_MODEL), row),
                  pl.BlockSpec((TM, D_MODEL), lambda i: (jnp.maximum(i - 1, 0), 0)),
                  pl.BlockSpec(memory_space=_VMEM),
                  pl.BlockSpec((1, SB_W), one), pl.BlockSpec((1, SSD_W), one), pl.BlockSpec((1, D_MODEL), one)],
        out_specs=[pl.BlockSpec((TM, D_MODEL), row), pl.BlockSpec((1, D_MODEL), one), pl.BlockSpec((1, D_MODEL), one)],
        out_shape=[_SDS((lp, D_MODEL), F32), _SDS((1, D_MODEL), F32), _SDS((1, D_MODEL), F32)],
        compiler_params=_params(("arbitrary",), 40),
    )(o_sb, proj, o_ssd, proj, h0, target, w_out, sb_w, ssd_w, fin_w)


def _gated_norm_bwd(o, gate, w, dy):
    sg, rs, n, _ = _gated_norm(o, gate, w)
    gw = jnp.sum(dy * n, axis=0, keepdims=True)
    dn = dy * w
    dp = rs * (dn - n * jnp.mean(dn * n, axis=-1, keepdims=True))
    d_o = dp * (gate * sg)
    d_gate = dp * o * (sg * (1.0 + gate * (1.0 - sg)))
    return d_o, d_gate, gw, n * w


def _tail_bwd(o_sb, o_ssd, proj, d_h1, w_out, sb_w, ssd_w):
    lp = o_sb.shape[0]
    nb = lp // TM
    row = lambda i: (i, 0)
    one = lambda i: (0, 0)

    def body(osb_ref, gate_ref, ossd_ref, z_ref, dh1_ref, wo_ref, sbw_ref, ssdw_ref,
             dosb_ref, dgate_ref, dossd_ref, dz_ref, gwo_ref, gsb_ref, gssd_ref):
        i = pl.program_id(0)

        @pl.when(i == 0)
        def _():
            gwo_ref[...] = jnp.zeros_like(gwo_ref)
            gsb_ref[...] = jnp.zeros_like(gsb_ref)
            gssd_ref[...] = jnp.zeros_like(gssd_ref)

        dh1 = dh1_ref[...].astype(_MXU)
        dy1 = lax.dot_general(dh1, wo_ref[0:SB_W, :], _NT, preferred_element_type=F32)
        dy2 = lax.dot_general(dh1, wo_ref[SB_W:SB_W + SSD_W, :], _NT, preferred_element_type=F32)
        d_o, d_g, gw, y1 = _gated_norm_bwd(osb_ref[...], gate_ref[...], sbw_ref[...], dy1)
        dosb_ref[...] = d_o
        dgate_ref[...] = d_g
        gsb_ref[...] += gw
        gwo_ref[0:SB_W, :] += lax.dot_general(y1.astype(_MXU), dh1, _TN, preferred_element_type=F32)
        d_o, d_g, gw, y2 = _gated_norm_bwd(ossd_ref[...], z_ref[...], ssdw_ref[...], dy2)
        dossd_ref[...] = d_o
        dz_ref[...] = d_g
        gssd_ref[...] += gw
        gwo_ref[SB_W:SB_W + SSD_W, :] += lax.dot_general(y2.astype(_MXU), dh1, _TN, preferred_element_type=F32)

    tile = pl.BlockSpec((TM, SB_W), row)
    return pl.pallas_call(
        body, name="tail_bwd", grid=(nb,),
        in_specs=[tile, pl.BlockSpec((TM, SB_W), lambda i: (i, COL_GATE // SB_W)),
                  tile, pl.BlockSpec((TM, SSD_W), lambda i: (i, COL_Z // SSD_W)),
                  tile, pl.BlockSpec(memory_space=_VMEM),
                  pl.BlockSpec((1, SB_W), one), pl.BlockSpec((1, SSD_W), one)],
        out_specs=[tile, tile, tile, tile,
                   pl.BlockSpec((SB_W + SSD_W, D_MODEL), one), pl.BlockSpec((1, SB_W), one), pl.BlockSpec((1, SSD_W), one)],
        out_shape=[_SDS((lp, SB_W), F32)] * 4 + [_SDS((SB_W + SSD_W, D_MODEL), F32), _SDS((1, SB_W), F32), _SDS((1, SSD_W), F32)],
        compiler_params=_params(("arbitrary",), 48),
    )(o_sb, proj, o_ssd, proj, d_h1, w_out, sb_w, ssd_w)


def _in_bwd(d_q, d_k, d_v, d_gate, d_z, d_xbc, d_dt128, w_main, w_dt, h0, d_h1, norm_w):
    lp = h0.shape[0]
    nb = lp // TM
    seq = lp - OFF
    row = lambda i: (i, 0)
    one = lambda i: (0, 0)
    secs = ((0, SB_W), (SB_W, SB_W), (2 * SB_W, SB_W), (COL_GATE, SB_W), (COL_Z, SSD_W), (COL_XBC, XBC_W))

    def body(dq_ref, dk_ref, dv_ref, dg_ref, dz_ref, dx_ref, ddt_ref, w_ref, wdt_ref, h0_ref, dh1_ref, nw_ref,
             gx_ref, gmeta_ref, gnw_ref):
        i = pl.program_id(0)

        @pl.when(i == 0)
        def _():
            gnw_ref[...] = jnp.zeros_like(gnw_ref)

        du = lax.dot_general(ddt_ref[...].astype(_MXU), wdt_ref[...], _NT, preferred_element_type=F32)
        for ref, (c0, width) in zip((dq_ref, dk_ref, dv_ref, dg_ref, dz_ref, dx_ref), secs):
            du = du + lax.dot_general(ref[...].astype(_MXU), w_ref[:, c0:c0 + width], _NT, preferred_element_type=F32)
        h = h0_ref[...]
        rs = lax.rsqrt(jnp.mean(h * h, axis=-1, keepdims=True) + EPS)
        n0 = h * rs
        gnw_ref[...] += jnp.sum(du * n0, axis=0, keepdims=True)
        g = du * nw_ref[...]
        dh0 = dh1_ref[...] + rs * (g - n0 * jnp.mean(g * n0, axis=-1, keepdims=True))

        @pl.when(i == 0)
        def _():
            gmeta_ref[...] = dh0[PAD:PAD + N_META, :]

        @pl.when(i > 0)
        def _():
            gx_ref[...] = dh0

    tile = pl.BlockSpec((TM, D_MODEL), row)
    return pl.pallas_call(
        body, name="in_bwd", grid=(nb,),
        in_specs=[tile, tile, tile, tile, tile, pl.BlockSpec((TM, XBC_W), row), pl.BlockSpec((TM, 128), row),
                  pl.BlockSpec(memory_space=_VMEM), pl.BlockSpec(memory_space=_VMEM),
                  tile, tile, pl.BlockSpec((1, D_MODEL), one)],
        out_specs=[pl.BlockSpec((TM, D_MODEL), lambda i: (jnp.maximum(i - 1, 0), 0)),
                   pl.BlockSpec((N_META, D_MODEL), one), pl.BlockSpec((1, D_MODEL), one)],
        out_shape=[_SDS((seq, D_MODEL), F32), _SDS((N_META, D_MODEL), F32), _SDS((1, D_MODEL), F32)],
        compiler_params=_params(("arbitrary",), 48),
    )(d_q, d_k, d_v, d_gate, d_z, d_xbc, d_dt128, w_main, w_dt, h0, d_h1, norm_w)


def _grad_w(u_t, d_sec, name):
    lp, n = d_sec.shape
    tn = min(512, n)

    def body(ut_ref, d_ref, o_ref):
        o_ref[...] = jnp.dot(ut_ref[...], d_ref[...].astype(_MXU), preferred_element_type=F32)

    return pl.pallas_call(
        body, name=name, grid=(n // tn,),
        in_specs=[pl.BlockSpec((D_MODEL, lp), lambda j: (0, 0)), pl.BlockSpec((lp, tn), lambda j: (0, j))],
        out_specs=pl.BlockSpec((D_MODEL, tn), lambda j: (0, j)),
        out_shape=_SDS((D_MODEL, n), F32),
        compiler_params=_params(("arbitrary",), 40),
    )(u_t, d_sec)


def _device_grads(x2d, target2d, meta_full, norm_w, w_main, w_dt, conv_w, conv_b, dt_bias, a_log, d_skip,
                  sb_w, ssd_w, w_out, fin_w):
    lp = x2d.shape[0] + OFF
    nc = lp // CHUNK
    h0, u, u_t = _prep(x2d, meta_full, norm_w)
    proj, dt_raw = _inproj(u, w_main, w_dt)
    o_sb, o_lo = _sb_fwd(proj)
    dt_bias128 = jnp.pad(dt_bias, ((0, 0), (0, 128 - N_HEADS)))
    xbc, dt128 = _conv_fwd(proj, dt_raw, conv_w, conv_b, dt_bias128)
    dt_c = dt128[:, :N_HEADS].reshape(nc, CHUNK, N_HEADS)
    dt_tc = jnp.swapaxes(dt_c, 1, 2)
    a = -jnp.exp(a_log)
    a_t = a.reshape(N_HEADS, 1)
    dskip_x = jnp.repeat(d_skip, HEAD, axis=1)
    o_ssd, states = _ssd_fwd(xbc, dt_c, dt_tc, a, a_t, dskip_x)
    d_h1, sq_err, g_fin = _tail_fwd(o_sb, o_ssd, proj, h0, target2d, w_out, sb_w, ssd_w, fin_w)

    d_osb, d_gate, d_ossd, d_z, g_wout, g_sb, g_ssd = _tail_bwd(o_sb, o_ssd, proj, d_h1, w_out, sb_w, ssd_w)
    d_q, d_k, d_v = _sb_bwd(proj, o_sb, o_lo, d_osb)
    d_xbc_act, ddt_a, ddt_b, ga1, ga2, gd = _ssd_bwd(xbc, dt_c, dt_tc, a, a_t, dskip_x, states, d_ossd)
    d_dt = (ddt_a + jnp.swapaxes(ddt_b, 1, 2)).reshape(lp, N_HEADS)
    d_dt128 = jnp.pad(d_dt, ((0, 0), (0, 128 - N_HEADS)))
    d_xbc, g_convw, g_convb, d_dtraw128, g_dtb128 = _conv_bwd(proj, dt_raw, conv_w, conv_b, dt_bias128, d_xbc_act, d_dt128)
    g_x, g_meta, g_nw = _in_bwd(d_q, d_k, d_v, d_gate, d_z, d_xbc, d_dtraw128, w_main, w_dt, h0, d_h1, norm_w)
    g_win = jnp.concatenate(
        [_grad_w(u_t, d, "gw_" + nm) for nm, d in (("q", d_q), ("k", d_k), ("v", d_v), ("gate", d_gate), ("z", d_z), ("xbc", d_xbc))]
        + [_grad_w(u_t, d_dtraw128, "gw_dt")[:, :N_HEADS]], axis=1)
    g_alog = (ga1 + ga2.reshape(1, N_HEADS)) * a
    g_dskip = gd.reshape(N_HEADS, HEAD).sum(axis=1).reshape(1, N_HEADS)
    grads = dict(meta_tokens=g_meta, norm_w=g_nw, w_in=g_win, conv_w=g_convw, conv_b=g_convb,
                 dt_bias=g_dtb128[:, :N_HEADS], a_log=g_alog, d_skip=g_dskip, sb_norm_w=g_sb, ssd_norm_w=g_ssd,
                 w_out=g_wout, final_norm_w=g_fin)
    return sq_err, g_x, grads


_MESH = pl.DeviceIdType.MESH
_ANY = pl.BlockSpec(memory_space=pl.ANY)


def _place():
    return lax.axis_index("x"), lax.axis_index("y"), lax.axis_index("c")


def _other_chips(x, y):
    return ((1 - x, y), (x, 1 - y), (1 - x, 1 - y))


def _gather_shards(arrays, n_big):
    n = len(arrays)

    def body(*refs):
        srcs, dsts = refs[:n], refs[n:2 * n]
        send_sems, recv_sems, fwd_send, fwd_recv, local_sems = refs[2 * n:]
        x, y, c = _place()
        mine = 2 * x + y
        chips = _other_chips(x, y)
        local = [pltpu.make_async_copy(srcs[a], dsts[a].at[mine], local_sems.at[a]) for a in range(n)]
        for cp in local:
            cp.start()

        def window(a):
            half = arrays[a].shape[0] // 2
            return pl.ds(pl.multiple_of(c * half, 16), half)

        first = []
        for a in range(n):
            for k, (px, py) in enumerate(chips):
                if a < n_big:
                    src, dst = srcs[a].at[window(a)], dsts[a].at[mine, window(a)]
                else:
                    src, dst = srcs[a], dsts[a].at[mine]
                cp = pltpu.make_async_remote_copy(
                    src_ref=src, dst_ref=dst, send_sem=send_sems.at[a * 3 + k], recv_sem=recv_sems.at[a * 3 + k],
                    device_id=(px, py, c), device_id_type=_MESH)
                cp.start()
                first.append(cp)
        passed = []
        for a in range(n):
            for k, (px, py) in enumerate(chips):
                first[a * 3 + k].wait_recv()
                if a < n_big:
                    landed = dsts[a].at[2 * px + py, window(a)]
                    cp = pltpu.make_async_remote_copy(
                        src_ref=landed, dst_ref=landed, send_sem=fwd_send.at[a * 3 + k], recv_sem=fwd_recv.at[a * 3 + k],
                        device_id=(x, y, 1 - c), device_id_type=_MESH)
                    cp.start()
                    passed.append(cp)
        for cp in passed:
            cp.wait_recv()
        for cp in first + passed:
            cp.wait_send()
        for cp in local:
            cp.wait()

    return pl.pallas_call(
        body, name="gather_shards",
        in_specs=[_ANY] * n, out_specs=[_ANY] * n,
        out_shape=[_SDS((N_CHIPS,) + a.shape, a.dtype) for a in arrays],
        scratch_shapes=[pltpu.SemaphoreType.DMA((3 * n,)), pltpu.SemaphoreType.DMA((3 * n,)),
                        pltpu.SemaphoreType.DMA((3 * n_big,)), pltpu.SemaphoreType.DMA((3 * n_big,)),
                        pltpu.SemaphoreType.DMA((n,))],
    )(*arrays)


def _scatter_slabs(arrays):
    n = len(arrays)

    def body(*refs):
        srcs, dsts = refs[:n], refs[n:2 * n]
        send_sems, recv_sems, local_sems = refs[2 * n:]
        x, y, c = _place()
        mine = 2 * x + y
        local = [pltpu.make_async_copy(srcs[a].at[mine], dsts[a].at[mine], local_sems.at[a]) for a in range(n)]
        for cp in local:
            cp.start()
        remote = []
        for a in range(n):
            for k, (px, py) in enumerate(_other_chips(x, y)):
                cp = pltpu.make_async_remote_copy(
                    src_ref=srcs[a].at[2 * px + py], dst_ref=dsts[a].at[mine],
                    send_sem=send_sems.at[a * 3 + k], recv_sem=recv_sems.at[a * 3 + k],
                    device_id=(px, py, c), device_id_type=_MESH)
                cp.start()
                remote.append(cp)
        for cp in remote:
            cp.wait_recv()
        for cp in remote:
            cp.wait_send()
        for cp in local:
            cp.wait()

    return pl.pallas_call(
        body, name="scatter_slabs",
        in_specs=[_ANY] * n, out_specs=[_ANY] * n,
        out_shape=[_SDS(a.shape, a.dtype) for a in arrays],
        scratch_shapes=[pltpu.SemaphoreType.DMA((3 * n,)), pltpu.SemaphoreType.DMA((3 * n,)),
                        pltpu.SemaphoreType.DMA((n,))],
    )(*arrays)


def _swap_halves(arrays):
    n = len(arrays)

    def body(*refs):
        srcs, dsts = refs[:n], refs[n:2 * n]
        send_sems, recv_sems = refs[2 * n:]
        x, y, c = _place()
        copies = []
        for a in range(n):
            half = arrays[a].shape[1] // 2
            cp = pltpu.make_async_remote_copy(
                src_ref=srcs[a].at[:, pl.ds(pl.multiple_of((1 - c) * half, 16), half)], dst_ref=dsts[a],
                send_sem=send_sems.at[a], recv_sem=recv_sems.at[a],
                device_id=(x, y, 1 - c), device_id_type=_MESH)
            cp.start()
            copies.append(cp)
        for cp in copies:
            cp.wait_recv()
        for cp in copies:
            cp.wait_send()

    return pl.pallas_call(
        body, name="swap_halves",
        in_specs=[_ANY] * n, out_specs=[_ANY] * n,
        out_shape=[_SDS((a.shape[0], a.shape[1] // 2, a.shape[2]), a.dtype) for a in arrays],
        scratch_shapes=[pltpu.SemaphoreType.DMA((n,)), pltpu.SemaphoreType.DMA((n,))],
    )(*arrays)


def _join_halves(arrays):
    n = len(arrays)

    def body(*refs):
        srcs, dsts = refs[:n], refs[n:2 * n]
        send_sems, recv_sems, local_sems = refs[2 * n:]
        x, y, c = _place()
        copies, local = [], []
        for a in range(n):
            half = arrays[a].shape[0]
            rows = pl.ds(pl.multiple_of(c * half, 16), half)
            lc = pltpu.make_async_copy(srcs[a], dsts[a].at[rows], local_sems.at[a])
            lc.start()
            local.append(lc)
            cp = pltpu.make_async_remote_copy(
                src_ref=srcs[a], dst_ref=dsts[a].at[rows], send_sem=send_sems.at[a], recv_sem=recv_sems.at[a],
                device_id=(x, y, 1 - c), device_id_type=_MESH)
            cp.start()
            copies.append(cp)
        for cp in copies:
            cp.wait_recv()
        for cp in copies:
            cp.wait_send()
        for lc in local:
            lc.wait()

    return pl.pallas_call(
        body, name="join_halves",
        in_specs=[_ANY] * n, out_specs=[_ANY] * n,
        out_shape=[_SDS((2 * a.shape[0], a.shape[1]), a.dtype) for a in arrays],
        scratch_shapes=[pltpu.SemaphoreType.DMA((n,)), pltpu.SemaphoreType.DMA((n,)), pltpu.SemaphoreType.DMA((n,))],
    )(*arrays)


N_DEV = 8
SMALL_ROWS = 32
SMALL_COLS = XBC_W


def _gather_small(packed):
    def body(src_ref, dst_ref, send_sems, recv_sems, local_sem):
        x, y, c = _place()
        me = 4 * x + 2 * y + c
        own = pltpu.make_async_copy(src_ref, dst_ref.at[me], local_sem)
        own.start()
        copies = []
        for k in range(1, N_DEV):
            bx, by, bc = (k >> 2) & 1, (k >> 1) & 1, k & 1
            peer = (x + bx - 2 * x * bx, y + by - 2 * y * by, c + bc - 2 * c * bc)
            cp = pltpu.make_async_remote_copy(
                src_ref=src_ref, dst_ref=dst_ref.at[me], send_sem=send_sems.at[k - 1], recv_sem=recv_sems.at[k - 1],
                device_id=peer, device_id_type=_MESH)
            cp.start()
            copies.append(cp)
        for cp in copies:
            cp.wait_recv()
        for cp in copies:
            cp.wait_send()
        own.wait()

    return pl.pallas_call(
        body, name="gather_small",
        in_specs=[pl.BlockSpec(memory_space=_VMEM)], out_specs=pl.BlockSpec(memory_space=_VMEM),
        out_shape=_SDS((N_DEV, SMALL_ROWS, SMALL_COLS), F32),
        scratch_shapes=[pltpu.SemaphoreType.DMA((N_DEV - 1,)), pltpu.SemaphoreType.DMA((N_DEV - 1,)),
                        pltpu.SemaphoreType.DMA],
    )(packed)


def _adamw(w, g, m, v):
    m = ADAM_B1 * m + (1.0 - ADAM_B1) * g
    v = ADAM_B2 * v + (1.0 - ADAM_B2) * (g * g)
    m_hat = m / (1.0 - ADAM_B1 ** ADAM_STEP)
    v_hat = v / (1.0 - ADAM_B2 ** ADAM_STEP)
    delta = -ADAM_LR * (m_hat / (jnp.sqrt(v_hat) + ADAM_EPS) + ADAM_WD * w)
    return delta, m, v


def _sum_slabs(slabs, name):
    _, r, c = slabs.shape
    tr = 128

    def body(s_ref, o_ref):
        o_ref[...] = ((s_ref[0].astype(F32) + s_ref[1].astype(F32)) + s_ref[2].astype(F32)) + s_ref[3].astype(F32)

    return pl.pallas_call(
        body, name=name, grid=(r // tr,),
        in_specs=[pl.BlockSpec((N_CHIPS, tr, c), lambda i: (0, i, 0))],
        out_specs=pl.BlockSpec((tr, c), lambda i: (i, 0)),
        out_shape=_SDS((r, c), F32),
        compiler_params=_params(("arbitrary",)),
    )(slabs)


def _add_halves(own, recv, core, name):
    _, r, c = own.shape
    half = r // 2
    tr = 128
    nblk = half // tr

    def body(core_ref, a_ref, b_ref, o_ref):
        o_ref[...] = (a_ref[...].astype(F32) + b_ref[...].astype(F32)).astype(o_ref.dtype)

    grid_spec = pltpu.PrefetchScalarGridSpec(
        num_scalar_prefetch=1, grid=(nblk,),
        in_specs=[pl.BlockSpec((N_CHIPS, tr, c), lambda i, core_ref: (0, core_ref[0] * nblk + i, 0)),
                  pl.BlockSpec((N_CHIPS, tr, c), lambda i, core_ref: (0, i, 0))],
        out_specs=pl.BlockSpec((N_CHIPS, tr, c), lambda i, core_ref: (0, i, 0)))
    return pl.pallas_call(
        body, name=name, grid_spec=grid_spec, out_shape=_SDS((N_CHIPS, half, c), own.dtype),
        compiler_params=_params(("arbitrary",)),
    )(core, own, recv)


def _update_big(w, m, v, g, name):
    r, c = w.shape
    tr = 128

    def body(w_ref, m_ref, v_ref, g_ref, d_ref, mo_ref, vo_ref):
        delta, m_new, v_new = _adamw(w_ref[...], g_ref[...], m_ref[...], v_ref[...])
        d_ref[...] = delta
        mo_ref[...] = m_new
        vo_ref[...] = v_new

    spec = pl.BlockSpec((tr, c), lambda i: (i, 0))
    return pl.pallas_call(
        body, name=name, grid=(r // tr,),
        in_specs=[spec] * 4, out_specs=[spec] * 3,
        out_shape=[_SDS((r, c), F32)] * 3,
        compiler_params=_params(("arbitrary",)),
    )(w, m, v, g)


_ROW = dict(norm_w=0, sb_norm_w=1, ssd_norm_w=2, final_norm_w=3, conv_b=4, dt_bias=5, a_log=6, d_skip=7,
            conv_w=8, sq_err=12, meta_tokens=16)
_SMALL = ("meta_tokens", "norm_w", "conv_w", "conv_b", "dt_bias", "a_log", "d_skip", "sb_norm_w", "ssd_norm_w",
          "final_norm_w")


def _pack_small(sq_err, grads):
    def rowpad(a):
        return jnp.pad(a, ((0, 0), (0, SMALL_COLS - a.shape[1])))

    rows = [rowpad(grads[k]) for k in ("norm_w", "sb_norm_w", "ssd_norm_w", "final_norm_w", "conv_b", "dt_bias", "a_log", "d_skip")]
    rows.append(grads["conv_w"])
    rows.append(rowpad(sq_err))
    rows.append(jnp.zeros((3, SMALL_COLS), F32))
    rows.append(rowpad(grads["meta_tokens"]))
    return jnp.concatenate(rows, axis=0)


def _update_small(gathered, ws, ms, vs):
    names = _SMALL
    n = len(names)

    def body(*refs):
        g_ref = refs[0]
        w_refs, m_refs, v_refs = refs[1:1 + n], refs[1 + n:1 + 2 * n], refs[1 + 2 * n:1 + 3 * n]
        outs = refs[1 + 3 * n:]
        loss_ref = outs[0]
        go, do, mo, vo = outs[1:1 + n], outs[1 + n:1 + 2 * n], outs[1 + 2 * n:1 + 3 * n], outs[1 + 3 * n:1 + 4 * n]
        tot = g_ref[0]
        for d in range(1, N_DEV):
            tot = tot + g_ref[d]
        x, y, _ = _place()
        chip = 2 * x + y
        loss_ref[...] = jnp.broadcast_to(
            0.5 * jnp.sum(tot[_ROW["sq_err"]:_ROW["sq_err"] + 1, 0:D_MODEL], axis=1, keepdims=True) / D_MODEL, (1, 128))
        for idx, nm in enumerate(names):
            r0 = _ROW[nm]
            rows, cols = w_refs[idx].shape
            if nm in ("conv_w", "meta_tokens"):
                g = jnp.zeros((rows, cols), F32)
                for j in range(N_CHIPS):
                    g = g + jnp.where(chip == j, tot[r0:r0 + rows, j * cols:(j + 1) * cols], 0.0)
            else:
                g = tot[r0:r0 + rows, 0:cols]
            delta, m_new, v_new = _adamw(w_refs[idx][...], g, m_refs[idx][...], v_refs[idx][...])
            go[idx][...] = g
            do[idx][...] = delta
            mo[idx][...] = m_new
            vo[idx][...] = v_new

    shapes = [_SDS(ws[nm].shape, F32) for nm in names]
    vm = pl.BlockSpec(memory_space=_VMEM)
    res = pl.pallas_call(
        body, name="update_small",
        in_specs=[vm] * (1 + 3 * n), out_specs=[vm] * (1 + 4 * n),
        out_shape=[_SDS((1, 128), F32)] + shapes * 4,
    )(gathered, *[ws[nm] for nm in names], *[ms[nm] for nm in names], *[vs[nm] for nm in names])
    loss = res[0][0, 0]
    g = dict(zip(names, res[1:1 + n]))
    d = dict(zip(names, res[1 + n:1 + 2 * n]))
    m = dict(zip(names, res[1 + 2 * n:1 + 3 * n]))
    v = dict(zip(names, res[1 + 3 * n:1 + 4 * n]))
    return loss, g, d, m, v


_WEIGHTS = ("meta_tokens", "norm_w", "w_in", "conv_w", "conv_b", "dt_bias", "a_log", "d_skip", "sb_norm_w",
            "ssd_norm_w", "w_out", "final_norm_w")


def kernel(x, meta_tokens, norm_w, w_in, conv_w, conv_b, dt_bias, a_log, d_skip, sb_norm_w, ssd_norm_w, w_out, final_norm_w, loss_target, m_meta_tokens, m_norm_w, m_w_in, m_conv_w, m_conv_b, m_dt_bias, m_a_log, m_d_skip, m_sb_norm_w, m_ssd_norm_w, m_w_out, m_final_norm_w, v_meta_tokens, v_norm_w, v_w_in, v_conv_w, v_conv_b, v_dt_bias, v_a_log, v_d_skip, v_sb_norm_w, v_ssd_norm_w, v_w_out, v_final_norm_w):
    given = dict(meta_tokens=meta_tokens, norm_w=norm_w, w_in=w_in, conv_w=conv_w, conv_b=conv_b, dt_bias=dt_bias,
                 a_log=a_log, d_skip=d_skip, sb_norm_w=sb_norm_w, ssd_norm_w=ssd_norm_w, w_out=w_out,
                 final_norm_w=final_norm_w)
    mom = dict(meta_tokens=m_meta_tokens, norm_w=m_norm_w, w_in=m_w_in, conv_w=m_conv_w, conv_b=m_conv_b,
               dt_bias=m_dt_bias, a_log=m_a_log, d_skip=m_d_skip, sb_norm_w=m_sb_norm_w, ssd_norm_w=m_ssd_norm_w,
               w_out=m_w_out, final_norm_w=m_final_norm_w)
    var = dict(meta_tokens=v_meta_tokens, norm_w=v_norm_w, w_in=v_w_in, conv_w=v_conv_w, conv_b=v_conv_b,
               dt_bias=v_dt_bias, a_log=v_a_log, d_skip=v_d_skip, sb_norm_w=v_sb_norm_w, ssd_norm_w=v_ssd_norm_w,
               w_out=v_w_out, final_norm_w=v_final_norm_w)
    seq = x.shape[1]

    def two_d(a):
        return a.reshape((-1, a.shape[-1])) if a.ndim != 2 else a

    g_win, g_wout, g_meta, g_cw = _gather_shards(
        [w_in[0].astype(_MXU), w_out[0].astype(_MXU), meta_tokens, conv_w[0]], 2)
    w_in_full = jnp.swapaxes(g_win, 0, 1).reshape(D_MODEL, D_IN)
    w_main = w_in_full[:, :N_MAIN]
    w_dt = jnp.pad(w_in_full[:, N_MAIN:], ((0, 0), (0, 128 - N_HEADS)))
    w_out_full = g_wout.reshape(2 * D_MODEL, D_MODEL)
    meta_full = jnp.swapaxes(g_meta, 0, 1).reshape(N_META, D_MODEL)
    conv_w_full = jnp.swapaxes(g_cw, 0, 1).reshape(4, XBC_W)

    sq_err, g_x, grads = _device_grads(
        x.reshape(seq, D_MODEL), loss_target.reshape(seq, D_MODEL), meta_full, norm_w, w_main, w_dt, conv_w_full,
        conv_b, dt_bias, a_log, d_skip, sb_norm_w, ssd_norm_w, w_out_full, final_norm_w.reshape(1, D_MODEL))

    core = lax.axis_index("c").astype(jnp.int32).reshape(1)
    slab_in = jnp.swapaxes(grads["w_in"].reshape(D_MODEL, N_CHIPS, W_IN_SHARD), 0, 1).astype(_MXU)
    slab_out = grads["w_out"].reshape(N_CHIPS, W_OUT_SHARD, D_MODEL).astype(_MXU)
    sib_in, sib_out = _swap_halves([slab_in, slab_out])
    chip_in = _add_halves(slab_in, sib_in, core, "chip_sum_w_in")
    chip_out = _add_halves(slab_out, sib_out, core, "chip_sum_w_out")
    got_in, got_out = _scatter_slabs([chip_in, chip_out])
    g_in, g_out = _join_halves([_sum_slabs(got_in, "sum_w_in"), _sum_slabs(got_out, "sum_w_out")])
    big = dict(w_in=(g_in,) + tuple(_update_big(w_in[0], m_w_in[0], v_w_in[0], g_in, "update_w_in")),
               w_out=(g_out,) + tuple(_update_big(w_out[0], m_w_out[0], v_w_out[0], g_out, "update_w_out")))

    gathered = _gather_small(_pack_small(sq_err, grads))
    loss, sg, sd, sm, sv = _update_small(
        gathered, {k: two_d(given[k]) for k in _SMALL}, {k: two_d(mom[k]) for k in _SMALL},
        {k: two_d(var[k]) for k in _SMALL})

    out = {}
    for idx, group in enumerate((sg, sd, sm, sv)):
        for k in _SMALL:
            out[(idx, k)] = group[k].reshape(given[k].shape)
        for k in ("w_in", "w_out"):
            out[(idx, k)] = big[k][idx].reshape(given[k].shape)
    return (loss, g_x.reshape(x.shape), *[out[(idx, k)] for idx in range(4) for k in _WEIGHTS])
```

```python
import functools
import math

import jax
import jax.numpy as jnp
from jax import lax
from jax.experimental import pallas as pl
from jax.experimental.pallas import tpu as pltpu

F32 = jnp.float32
_MXU = jnp.bfloat16

D_MODEL = 1024
N_META = 16
PAD = 112
OFF = PAD + N_META
TM = 128
CHUNK = 64
SB_W = 1024
SSD_W = 1024
N_HEADS = 16
HEAD = 64
N_GROUPS = 2
N_STATE = 128
XBC_W = SSD_W + 2 * N_GROUPS * N_STATE
N_MAIN = 4 * SB_W + SSD_W + XBC_W
COL_GATE = 3 * SB_W
COL_Z = 4 * SB_W
COL_XBC = 5 * SB_W
D_IN = N_MAIN + N_HEADS
EPS = 1e-5
N_CHIPS = 4
W_IN_SHARD = D_IN // N_CHIPS
W_OUT_SHARD = 2 * D_MODEL // N_CHIPS

ADAM_LR = 0.001
ADAM_B1 = 0.9
ADAM_B2 = 0.999
ADAM_EPS = 1e-08
ADAM_WD = 0.01
ADAM_STEP = 10

_SDS = jax.ShapeDtypeStruct
_NT = (((1,), (1,)), ((), ()))
_TN = (((0,), (0,)), ((), ()))
_VMEM = pltpu.VMEM


def _params(sem=None, vmem_mb=None):
    kw = {}
    if sem is not None:
        kw["dimension_semantics"] = sem
    if vmem_mb is not None:
        kw["vmem_limit_bytes"] = vmem_mb * 1024 * 1024
    return pltpu.CompilerParams(**kw)


def _mm(a, b):
    return jnp.dot(a.astype(_MXU), b.astype(_MXU), preferred_element_type=F32)


def _mm_nt(a, b):
    return lax.dot_general(a.astype(_MXU), b.astype(_MXU), _NT, preferred_element_type=F32)


def _mm_tn(a, b):
    return lax.dot_general(a.astype(_MXU), b.astype(_MXU), _TN, preferred_element_type=F32)


def _split(x, parts):
    out = []
    r = x
    for _ in range(parts):
        p = r.astype(_MXU)
        out.append(p)
        r = r - p.astype(F32)
    return out


def _sel_right(x, m01, parts=3):
    acc = None
    for p in _split(x, parts):
        t = jnp.dot(p, m01, preferred_element_type=F32)
        acc = t if acc is None else acc + t
    return acc


def _sel_left(m01, x, parts=3):
    acc = None
    for p in _split(x, parts):
        t = jnp.dot(m01, p, preferred_element_type=F32)
        acc = t if acc is None else acc + t
    return acc


def _iota(shape, axis):
    return lax.broadcasted_iota(jnp.int32, shape, axis)


def _sigmoid(x):
    return 1.0 / (1.0 + jnp.exp(-x))


def _prep(x2d, meta_full, norm_w):
    seq = x2d.shape[0]
    lp = seq + OFF
    nb = lp // TM

    def body(x_ref, meta_ref, w_ref, h0_ref, u_ref, ut_ref):
        i = pl.program_id(0)

        @pl.when(i == 0)
        def _():
            h0_ref[...] = jnp.concatenate([jnp.zeros((PAD, D_MODEL), F32), meta_ref[...]], axis=0)

        @pl.when(i > 0)
        def _():
            h0_ref[...] = x_ref[...]

        h = h0_ref[...]
        rs = lax.rsqrt(jnp.mean(h * h, axis=-1, keepdims=True) + EPS)
        u = (h * rs * w_ref[...]).astype(_MXU)
        u_ref[...] = u
        ut_ref[...] = u.T

    return pl.pallas_call(
        body, name="prep", grid=(nb,),
        in_specs=[pl.BlockSpec((TM, D_MODEL), lambda i: (jnp.maximum(i - 1, 0), 0)),
                  pl.BlockSpec((N_META, D_MODEL), lambda i: (0, 0)),
                  pl.BlockSpec((1, D_MODEL), lambda i: (0, 0))],
        out_specs=[pl.BlockSpec((TM, D_MODEL), lambda i: (i, 0)),
                   pl.BlockSpec((TM, D_MODEL), lambda i: (i, 0)),
                   pl.BlockSpec((D_MODEL, TM), lambda i: (0, i))],
        out_shape=[_SDS((lp, D_MODEL), F32), _SDS((lp, D_MODEL), _MXU), _SDS((D_MODEL, lp), _MXU)],
        compiler_params=_params(("arbitrary",)),
    )(x2d, meta_full, norm_w)


def _inproj(u, w_main, w_dt):
    lp = u.shape[0]
    tn = 512

    def body(u_ref, w_ref, wdt_ref, o_ref, odt_ref):
        o_ref[...] = jnp.dot(u_ref[...], w_ref[...], preferred_element_type=F32)

        @pl.when(pl.program_id(0) == 0)
        def _():
            odt_ref[...] = jnp.dot(u_ref[...], wdt_ref[...], preferred_element_type=F32)

    return pl.pallas_call(
        body, name="inproj", grid=(N_MAIN // tn,),
        in_specs=[pl.BlockSpec((lp, D_MODEL), lambda j: (0, 0)),
                  pl.BlockSpec((D_MODEL, tn), lambda j: (0, j)),
                  pl.BlockSpec((D_MODEL, 128), lambda j: (0, 0))],
        out_specs=[pl.BlockSpec((lp, tn), lambda j: (0, j)),
                   pl.BlockSpec((lp, 128), lambda j: (0, 0))],
        out_shape=[_SDS((lp, N_MAIN), F32), _SDS((lp, 128), F32)],
        compiler_params=_params(("arbitrary",), 48),
    )(u, w_main, w_dt)


SB_WINDOW = 3
SB_DEAD = -104.0


def _sb_logs(qh, kwin):
    z = lax.dot_general(qh, kwin, _NT, preferred_element_type=F32)
    e = jnp.exp(-jnp.abs(z))
    l1p = jnp.log(1.0 + e)
    lk_full = -(jnp.maximum(z, 0.0) + l1p)
    ls = jnp.minimum(z, 0.0) - l1p
    return z, e, ls, lk_full


def _blk(a, b):
    return a[:, b * TM:(b + 1) * TM]


def _stacked_sel(blocks, m01):
    n = len(blocks)
    pieces = [_split(b, 2) for b in blocks]
    stacked = jnp.concatenate([p[0] for p in pieces] + [p[1] for p in pieces], axis=0)
    res = jnp.dot(stacked, m01, preferred_element_type=F32)
    return [res[j * TM:(j + 1) * TM] + res[(n + j) * TM:(n + j + 1) * TM] for j in range(n)]


def _sb_weights(ls, lk_full, run, last_mask, upper, n):
    lk = [_blk(lk_full, b) for b in range(n)]
    lk[n - 1] = jnp.where(last_mask, lk[n - 1], 0.0)
    aft = _stacked_sel(lk, upper)
    w = [None] * n
    for b in range(n - 1, -1, -1):
        wb = jnp.exp(_blk(ls, b) + aft[b] + run)
        w[b] = jnp.where(last_mask, wb, 0.0) if b == n - 1 else wb
        run = run + jnp.sum(lk[b], axis=1, keepdims=True)
    return w, run


def _sb_fwd(proj):
    lp = proj.shape[0]
    nb = lp // TM

    def body(q_ref, k_ref, v_ref, o_ref, olo_ref, acc, run_scr):
        i = pl.program_id(1)
        lane = _iota((TM, TM), 1)
        row = _iota((TM, TM), 0)
        head0 = lane < HEAD
        upper = (row > lane).astype(_MXU)
        strict = lane < row
        q = q_ref[...] * (1.0 / math.sqrt(HEAD))
        qh = (jnp.where(head0, q, 0.0).astype(_MXU), jnp.where(head0, 0.0, q).astype(_MXU))

        def key_set(first, n, last_mask):
            off = pl.multiple_of(first * TM, TM)
            kwin = k_ref[pl.ds(off, n * TM), :].astype(_MXU)
            vwin = v_ref[pl.ds(off, n * TM), :].astype(_MXU)
            alive = None
            for hh in range(2):
                run = run_scr[hh][:, 0:1]
                _, _, ls, lk_full = _sb_logs(qh[hh], kwin)
                w, run = _sb_weights(ls, lk_full, run, last_mask, upper, n)
                pieces = [_split(wb, 2) for wb in w]
                stacked = jnp.concatenate(
                    [jnp.concatenate([p[0] for p in pieces], axis=1), jnp.concatenate([p[1] for p in pieces], axis=1)], axis=0)
                res = jnp.dot(stacked, vwin, preferred_element_type=F32)
                acc[hh] += res[0:TM]
                acc[2 + hh] += res[TM:2 * TM]
                run_scr[hh] = jnp.broadcast_to(run, (TM, TM))
                top = jnp.max(run)
                alive = top if alive is None else jnp.maximum(alive, top)
            return (alive > SB_DEAD).astype(jnp.int32)

        acc[...] = jnp.zeros_like(acc)
        run_scr[...] = jnp.zeros_like(run_scr)

        @pl.when(i >= SB_WINDOW - 1)
        def _():
            key_set(i - (SB_WINDOW - 1), SB_WINDOW, strict)

        start = jnp.where(i >= SB_WINDOW - 1, i - SB_WINDOW, i)
        alive0 = (jnp.max(run_scr[...]) > SB_DEAD).astype(jnp.int32)

        def cond(c):
            return jnp.logical_and(c[0] >= 0, c[1] > 0)

        def step(c):
            kb = c[0]
            return kb - 1, key_set(kb, 1, jnp.logical_or(strict, kb < i))

        lax.while_loop(cond, step, (start, alive0))
        o_ref[...] = jnp.where(head0, acc[0], acc[1])
        olo_ref[...] = jnp.where(head0, acc[2], acc[3])

    npair = SB_W // TM
    blk = pl.BlockSpec((TM, TM), lambda p, i: (i, p))
    return pl.pallas_call(
        body, name="sb_fwd", grid=(npair, nb),
        in_specs=[blk,
                  pl.BlockSpec((lp, TM), lambda p, i: (0, npair + p)),
                  pl.BlockSpec((lp, TM), lambda p, i: (0, 2 * npair + p))],
        out_specs=[blk, blk],
        out_shape=[_SDS((lp, SB_W), F32), _SDS((lp, SB_W), F32)],
        scratch_shapes=[pltpu.VMEM((4, TM, TM), F32), pltpu.VMEM((2, TM, TM), F32)],
        compiler_params=_params(("arbitrary", "arbitrary")),
    )(proj, proj, proj)


def _sb_bwd(proj, o_sb, o_lo, d_o):
    lp = proj.shape[0]
    nb = lp // TM
    scale = 1.0 / math.sqrt(HEAD)

    def body(q_ref, k_ref, v_ref, o_ref, olo_ref, do_ref, dq_ref, dk_ref, dv_ref, dq_acc, run_scr, gsum_scr):
        i = pl.program_id(1)

        @pl.when(i == 0)
        def _():
            dk_ref[...] = jnp.zeros_like(dk_ref)
            dv_ref[...] = jnp.zeros_like(dv_ref)

        lane = _iota((TM, TM), 1)
        row = _iota((TM, TM), 0)
        head0 = lane < HEAD
        hmask = (head0, jnp.logical_not(head0))
        upper = (row > lane).astype(_MXU)
        lower_incl = (row >= lane).astype(_MXU)
        strict = lane < row
        q = q_ref[...] * scale
        do = do_ref[...]
        prod = do.astype(_MXU).astype(F32) * (o_ref[...] + olo_ref[...])
        qh = tuple(jnp.where(m, q, 0.0).astype(_MXU) for m in hmask)
        doh = tuple(jnp.where(m, do, 0.0).astype(_MXU) for m in hmask)
        gtot = tuple(jnp.sum(jnp.where(m, prod, 0.0), axis=1, keepdims=True) for m in hmask)

        def key_set(first, n, last_mask):
            off = pl.multiple_of(first * TM, TM)
            kf = k_ref[pl.ds(off, n * TM), :]
            kwin = kf.astype(_MXU)
            vwin = v_ref[pl.ds(off, n * TM), :].astype(_MXU)
            dk_win = None
            alive = None
            for hh in range(2):
                run = run_scr[hh][:, 0:1]
                gsum = gsum_scr[hh][:, 0:1]
                z, e, ls, lk_full = _sb_logs(qh[hh], kwin)
                w, run = _sb_weights(ls, lk_full, run, last_mask, upper, n)
                r = 1.0 / (1.0 + e)
                er = e * r
                pos = z >= 0.0
                beta = jnp.where(pos, r, er)
                one_m_beta = jnp.where(pos, er, r)
                dw = lax.dot_general(doh[hh], vwin, _NT, preferred_element_type=F32)
                g = [_blk(dw, b) * w[b] for b in range(n)]
                suffix = _stacked_sel(g, lower_incl)
                dz = [None] * n
                for b in range(n - 1, -1, -1):
                    prefix = gtot[hh] - gsum - suffix[b]
                    d = g[b] * _blk(one_m_beta, b) - _blk(beta, b) * prefix
                    dz[b] = (jnp.where(last_mask, d, 0.0) if b == n - 1 else d).astype(_MXU)
                    gsum = gsum + jnp.sum(g[b], axis=1, keepdims=True)
                dzw = jnp.concatenate(dz, axis=1)
                ww = jnp.concatenate([wb.astype(_MXU) for wb in w], axis=1)
                kh = jnp.where(hmask[hh][0:1, :], kf, 0.0).astype(_MXU)
                dq_acc[...] += jnp.dot(dzw, kh, preferred_element_type=F32)
                dk_h = lax.dot_general(dzw, qh[hh], _TN, preferred_element_type=F32)
                dv_h = lax.dot_general(ww, doh[hh], _TN, preferred_element_type=F32)
                dk_win = (dk_h, dv_h) if dk_win is None else (dk_win[0] + dk_h, dk_win[1] + dv_h)
                run_scr[hh] = jnp.broadcast_to(run, (TM, TM))
                gsum_scr[hh] = jnp.broadcast_to(gsum, (TM, TM))
                top = jnp.max(run)
                alive = top if alive is None else jnp.maximum(alive, top)
            dk_ref[pl.ds(off, n * TM), :] += dk_win[0]
            dv_ref[pl.ds(off, n * TM), :] += dk_win[1]
            return (alive > SB_DEAD).astype(jnp.int32)

        dq_acc[...] = jnp.zeros_like(dq_acc)
        run_scr[...] = jnp.zeros_like(run_scr)
        gsum_scr[...] = jnp.zeros_like(gsum_scr)

        @pl.when(i >= SB_WINDOW - 1)
        def _():
            key_set(i - (SB_WINDOW - 1), SB_WINDOW, strict)

        start = jnp.where(i >= SB_WINDOW - 1, i - SB_WINDOW, i)
        alive0 = (jnp.max(run_scr[...]) > SB_DEAD).astype(jnp.int32)

        def cond(c):
            return jnp.logical_and(c[0] >= 0, c[1] > 0)

        def step(c):
            kb = c[0]
            return kb - 1, key_set(kb, 1, jnp.logical_or(strict, kb < i))

        lax.while_loop(cond, step, (start, alive0))
        dq_ref[...] = dq_acc[...] * scale

    npair = SB_W // TM
    blk = pl.BlockSpec((TM, TM), lambda p, i: (i, p))
    col = pl.BlockSpec((lp, TM), lambda p, i: (0, p))
    return pl.pallas_call(
        body, name="sb_bwd", grid=(npair, nb),
        in_specs=[blk,
                  pl.BlockSpec((lp, TM), lambda p, i: (0, npair + p)),
                  pl.BlockSpec((lp, TM), lambda p, i: (0, 2 * npair + p)),
                  blk, blk, blk],
        out_specs=[blk, col, col],
        out_shape=[_SDS((lp, SB_W), F32)] * 3,
        scratch_shapes=[pltpu.VMEM((TM, TM), F32), pltpu.VMEM((2, TM, TM), F32), pltpu.VMEM((2, TM, TM), F32)],
        compiler_params=_params(("arbitrary", "arbitrary")),
    )(proj, proj, proj, o_sb, o_lo, d_o)


def _conv_pre(x_ref, w_ref, b_ref, lp):
    n = lp - 8
    w = w_ref[...]
    pre = (x_ref[pl.ds(5, n), :] * w[0:1, :] + x_ref[pl.ds(6, n), :] * w[1:2, :]
           + x_ref[pl.ds(7, n), :] * w[2:3, :] + x_ref[pl.ds(8, n), :] * w[3:4, :]) + b_ref[...]
    live = (_iota((n, 128), 0) + 8) >= PAD
    return pre, live


def _conv_fwd(proj, dt_raw, conv_w, conv_b, dt_bias128):
    lp = proj.shape[0]
    nblk = XBC_W // 128
    c0 = COL_XBC // 128

    def body(x_ref, w_ref, b_ref, dtr_ref, dtb_ref, o_ref, dt_ref):
        pre, live = _conv_pre(x_ref, w_ref, b_ref, lp)
        act = pre * _sigmoid(pre)
        o_ref[pl.ds(0, 8), :] = jnp.zeros((8, 128), F32)
        o_ref[pl.ds(8, lp - 8), :] = jnp.where(live, act, 0.0)

        @pl.when(pl.program_id(0) == 0)
        def _():
            s = dtr_ref[...] + dtb_ref[...]
            sp = jnp.maximum(s, 0.0) + jnp.log(1.0 + jnp.exp(-jnp.abs(s)))
            dt_ref[...] = jnp.where(_iota((lp, 128), 0) >= PAD, sp, 0.0)

    return pl.pallas_call(
        body, name="conv_fwd", grid=(nblk,),
        in_specs=[pl.BlockSpec((lp, 128), lambda j: (0, c0 + j)),
                  pl.BlockSpec((4, 128), lambda j: (0, j)),
                  pl.BlockSpec((1, 128), lambda j: (0, j)),
                  pl.BlockSpec((lp, 128), lambda j: (0, 0)),
                  pl.BlockSpec((1, 128), lambda j: (0, 0))],
        out_specs=[pl.BlockSpec((lp, 128), lambda j: (0, j)),
                   pl.BlockSpec((lp, 128), lambda j: (0, 0))],
        out_shape=[_SDS((lp, XBC_W), F32), _SDS((lp, 128), F32)],
        compiler_params=_params(("arbitrary",)),
    )(proj, conv_w, conv_b, dt_raw, dt_bias128)


def _conv_bwd(proj, dt_raw, conv_w, conv_b, dt_bias128, d_xbc, d_dt128):
    lp = proj.shape[0]
    nblk = XBC_W // 128
    c0 = COL_XBC // 128
    n = lp - 8

    def body(x_ref, w_ref, b_ref, dtr_ref, dtb_ref, dy_ref, ddt_ref,
             dx_ref, gw_ref, gb_ref, ddtr_ref, gdtb_ref, scr):
        pre, live = _conv_pre(x_ref, w_ref, b_ref, lp)
        sg = _sigmoid(pre)
        dpre = jnp.where(live, dy_ref[pl.ds(8, n), :] * (sg * (1.0 + pre * (1.0 - sg))), 0.0)
        gb_ref[...] = jnp.sum(dpre, axis=0, keepdims=True)
        gw_ref[...] = jnp.concatenate(
            [jnp.sum(dpre * x_ref[pl.ds(5 + k, n), :], axis=0, keepdims=True) for k in range(4)], axis=0)
        scr[pl.ds(0, 8), :] = jnp.zeros((8, 128), F32)
        scr[pl.ds(8, n), :] = dpre
        scr[pl.ds(lp, 8), :] = jnp.zeros((8, 128), F32)
        w = w_ref[...]
        dx_ref[pl.ds(0, 8), :] = jnp.zeros((8, 128), F32)
        dx_ref[pl.ds(8, n), :] = (scr[pl.ds(8, n), :] * w[3:4, :] + scr[pl.ds(9, n), :] * w[2:3, :]
                                  + scr[pl.ds(10, n), :] * w[1:2, :] + scr[pl.ds(11, n), :] * w[0:1, :])

        @pl.when(pl.program_id(0) == 0)
        def _():
            s = dtr_ref[...] + dtb_ref[...]
            d = jnp.where(_iota((lp, 128), 0) >= PAD, ddt_ref[...] * _sigmoid(s), 0.0)
            ddtr_ref[...] = d
            gdtb_ref[...] = jnp.sum(d, axis=0, keepdims=True)

    colblk = pl.BlockSpec((lp, 128), lambda j: (0, j))
    full128 = pl.BlockSpec((lp, 128), lambda j: (0, 0))
    return pl.pallas_call(
        body, name="conv_bwd", grid=(nblk,),
        in_specs=[pl.BlockSpec((lp, 128), lambda j: (0, c0 + j)),
                  pl.BlockSpec((4, 128), lambda j: (0, j)),
                  pl.BlockSpec((1, 128), lambda j: (0, j)),
                  full128, pl.BlockSpec((1, 128), lambda j: (0, 0)),
                  colblk, full128],
        out_specs=[colblk, pl.BlockSpec((4, 128), lambda j: (0, j)), pl.BlockSpec((1, 128), lambda j: (0, j)),
                   full128, pl.BlockSpec((1, 128), lambda j: (0, 0))],
        out_shape=[_SDS((lp, XBC_W), F32), _SDS((4, XBC_W), F32), _SDS((1, XBC_W), F32),
                   _SDS((lp, 128), F32), _SDS((1, 128), F32)],
        scratch_shapes=[pltpu.VMEM((lp + 8, 128), F32)],
        compiler_params=_params(("arbitrary",)),
    )(proj, conv_w, conv_b, dt_raw, dt_bias128, d_xbc, d_dt128)


def _ssd_pieces(dt, dt_t, a, a_t):
    r64 = _iota((CHUNK, CHUNK), 0)
    c64 = _iota((CHUNK, CHUNK), 1)
    tril = c64 <= r64
    tril01 = tril.astype(_MXU)
    triu01 = (r64 <= c64).astype(_MXU)
    expand = (lax.shift_right_logical(_iota((N_HEADS, SSD_W), 1), 6) == _iota((N_HEADS, SSD_W), 0)).astype(_MXU)
    acum = _sel_left(tril01, dt * a)
    acum_t = _sel_right(dt_t * a_t, triu01)
    ax = _sel_right(acum, expand)
    dtx = _sel_right(dt, expand)
    return tril, expand, acum, acum_t, ax, dtx


def _seg_matrix():
    return (lax.shift_right_logical(_iota((SSD_W, N_HEADS), 0), 6) == _iota((SSD_W, N_HEADS), 1)).astype(_MXU)


def _head_decay(ax, acum_t, h, tril):
    col = ax[:, h * HEAD:(h + 1) * HEAD]
    rowv = acum_t[h:h + 1, :]
    return jnp.where(tril, jnp.exp(jnp.minimum(col - rowv, 0.0)), 0.0)


def _ssd_fwd(xbc, dt_c, dt_tc, a, a_t, dskip_x):
    lp = xbc.shape[0]
    nc = lp // CHUNK
    gw = SSD_W // N_GROUPS
    hpg = N_HEADS // N_GROUPS

    def body(x_ref, dt_ref, dtt_ref, a_ref, at_ref, d_ref, y_ref, st_ref, state):
        c = pl.program_id(0)

        @pl.when(c == 0)
        def _():
            state[...] = jnp.zeros_like(state)

        st_ref[0] = state[...]
        tril, _, _, acum_t, ax, dtx = _ssd_pieces(dt_ref[0], dtt_ref[0], a_ref[...], at_ref[...])
        x = x_ref[:, 0:SSD_W]
        xdt = x * dtx
        ea = jnp.exp(ax)
        aex = ax[CHUNK - 1:CHUNK, :]
        wd = jnp.exp(aex - ax)
        eae = jnp.exp(aex)
        xw = xdt * wd
        y_ref[...] = x * d_ref[...]
        for g in range(N_GROUPS):
            gs = slice(g * gw, (g + 1) * gw)
            rs = slice(g * N_STATE, (g + 1) * N_STATE)
            bg = x_ref[:, SSD_W + g * N_STATE:SSD_W + (g + 1) * N_STATE]
            cg = x_ref[:, SSD_W + N_GROUPS * N_STATE + g * N_STATE:SSD_W + N_GROUPS * N_STATE + (g + 1) * N_STATE]
            sg = state[rs, :]
            cb = _mm_nt(cg, bg)
            y_ref[:, gs] += _mm(cg, sg) * ea[:, gs]
            for r in range(hpg):
                h = g * hpg + r
                hs = slice(h * HEAD, (h + 1) * HEAD)
                m = cb * _head_decay(ax, acum_t, h, tril)
                y_ref[:, hs] += _mm(m, xdt[:, hs])
            state[rs, :] = sg * eae[:, gs] + _mm_tn(bg, xw[:, gs])

    return pl.pallas_call(
        body, name="ssd_fwd", grid=(nc,),
        in_specs=[pl.BlockSpec((CHUNK, XBC_W), lambda c: (c, 0)),
                  pl.BlockSpec((1, CHUNK, N_HEADS), lambda c: (c, 0, 0)),
                  pl.BlockSpec((1, N_HEADS, CHUNK), lambda c: (c, 0, 0)),
                  pl.BlockSpec((1, N_HEADS), lambda c: (0, 0)),
                  pl.BlockSpec((N_HEADS, 1), lambda c: (0, 0)),
                  pl.BlockSpec((1, SSD_W), lambda c: (0, 0))],
        out_specs=[pl.BlockSpec((CHUNK, SSD_W), lambda c: (c, 0)),
                   pl.BlockSpec((1, N_GROUPS * N_STATE, gw), lambda c: (c, 0, 0))],
        out_shape=[_SDS((lp, SSD_W), F32), _SDS((nc, N_GROUPS * N_STATE, gw), F32)],
        scratch_shapes=[pltpu.VMEM((N_GROUPS * N_STATE, gw), F32)],
        compiler_params=_params(("arbitrary",)),
    )(xbc, dt_c, dt_tc, a, a_t, dskip_x)


def _ssd_bwd(xbc, dt_c, dt_tc, a, a_t, dskip_x, states, d_y):
    lp = xbc.shape[0]
    nc = lp // CHUNK
    gw = SSD_W // N_GROUPS
    hpg = N_HEADS // N_GROUPS

    def body(x_ref, dt_ref, dtt_ref, a_ref, at_ref, d_ref, st_ref, dy_ref,
             dx_ref, ddta_ref, ddtb_ref, ga1_ref, ga2_ref, gd_ref, dstate, dxdt_scr, z_scr, yoff_scr, sds_scr):
        c = pl.program_id(0)

        @pl.when(c == 0)
        def _():
            dstate[...] = jnp.zeros_like(dstate)
            ga1_ref[...] = jnp.zeros_like(ga1_ref)
            ga2_ref[...] = jnp.zeros_like(ga2_ref)
            gd_ref[...] = jnp.zeros_like(gd_ref)

        dt = dt_ref[0]
        dt_t = dtt_ref[0]
        a = a_ref[...]
        a_t = at_ref[...]
        tril, _, acum, acum_t, ax, dtx = _ssd_pieces(dt, dt_t, a, a_t)
        seg = _seg_matrix()
        x = x_ref[:, 0:SSD_W]
        dy = dy_ref[...]
        xdt = x * dtx
        ea = jnp.exp(ax)
        aex = ax[CHUNK - 1:CHUNK, :]
        wd = jnp.exp(aex - ax)
        eae = jnp.exp(aex)
        xw = xdt * wd
        edy = ea * dy
        lane16 = _iota((CHUNK, N_HEADS), 1)
        row16 = _iota((N_HEADS, CHUNK), 0)
        da_col = jnp.zeros((CHUNK, N_HEADS), F32)
        da_row = jnp.zeros((N_HEADS, CHUNK), F32)
        for g in range(N_GROUPS):
            gs = slice(g * gw, (g + 1) * gw)
            rs = slice(g * N_STATE, (g + 1) * N_STATE)
            bcol = slice(SSD_W + g * N_STATE, SSD_W + (g + 1) * N_STATE)
            ccol = slice(SSD_W + N_GROUPS * N_STATE + g * N_STATE, SSD_W + N_GROUPS * N_STATE + (g + 1) * N_STATE)
            bg = x_ref[:, bcol]
            cg = x_ref[:, ccol]
            sg = st_ref[0, rs, :]
            dsn = dstate[rs, :]
            cb = _mm_nt(cg, bg)
            z_scr[:, gs] = _mm(bg, dsn)
            yoff_scr[:, gs] = _mm(cg, sg) * ea[:, gs]
            sds_scr[:, gs] = jnp.broadcast_to(jnp.sum(dsn * sg, axis=0, keepdims=True), (8, gw))
            dcb = jnp.zeros((CHUNK, CHUNK), F32)
            for r in range(hpg):
                h = g * hpg + r
                hs = slice(h * HEAD, (h + 1) * HEAD)
                dec = _head_decay(ax, acum_t, h, tril)
                m = cb * dec
                t1 = _mm_nt(dy[:, hs], xdt[:, hs])
                dcb = dcb + dec * t1
                tm = m * t1
                da_col = da_col + jnp.where(lane16 == h, jnp.sum(tm, axis=1, keepdims=True), 0.0)
                da_row = da_row - jnp.where(row16 == h, jnp.sum(tm, axis=0, keepdims=True), 0.0)
                dxdt_scr[:, hs] = _mm_tn(m, dy[:, hs])
            dx_ref[:, ccol] = _mm(dcb, bg) + _mm_nt(edy[:, gs], sg)
            dx_ref[:, bcol] = _mm_tn(dcb, cg) + _mm_nt(xw[:, gs], dsn)
            dstate[rs, :] = eae[:, gs] * dsn + _mm_tn(cg, edy[:, gs])
        zf = z_scr[...]
        dxdt = dxdt_scr[...] + wd * zf
        t3 = _sel_right(xw * zf, seg)
        da_col = da_col + _sel_right(dy * yoff_scr[...], seg) - t3
        aend = acum[CHUNK - 1:CHUNK, :]
        sd = _sel_right(sds_scr[...], seg)[0:1, :] * jnp.exp(aend)
        last = jnp.sum(t3, axis=0, keepdims=True) + sd
        da_col = da_col + jnp.where(_iota((CHUNK, N_HEADS), 0) == CHUNK - 1, last, 0.0)
        r64 = _iota((CHUNK, CHUNK), 0)
        c64 = _iota((CHUNK, CHUNK), 1)
        ddta1 = _sel_left((c64 >= r64).astype(_MXU), da_col)
        ddta2 = _sel_right(da_row, (r64 >= c64).astype(_MXU))
        ddta_ref[0] = a * ddta1 + _sel_right(dxdt * x, seg)
        ddtb_ref[0] = a_t * ddta2
        ga1_ref[...] += jnp.sum(dt * ddta1, axis=0, keepdims=True)
        ga2_ref[...] += jnp.sum(dt_t * ddta2, axis=1, keepdims=True)
        dx_ref[:, 0:SSD_W] = dxdt * dtx + d_ref[...] * dy
        gd_ref[...] += jnp.sum(dy * x, axis=0, keepdims=True)

    rev = lambda c: (nc - 1 - c, 0)
    rev3 = lambda c: (nc - 1 - c, 0, 0)
    return pl.pallas_call(
        body, name="ssd_bwd", grid=(nc,),
        in_specs=[pl.BlockSpec((CHUNK, XBC_W), rev),
                  pl.BlockSpec((1, CHUNK, N_HEADS), rev3),
                  pl.BlockSpec((1, N_HEADS, CHUNK), rev3),
                  pl.BlockSpec((1, N_HEADS), lambda c: (0, 0)),
                  pl.BlockSpec((N_HEADS, 1), lambda c: (0, 0)),
                  pl.BlockSpec((1, SSD_W), lambda c: (0, 0)),
                  pl.BlockSpec((1, N_GROUPS * N_STATE, gw), rev3),
                  pl.BlockSpec((CHUNK, SSD_W), rev)],
        out_specs=[pl.BlockSpec((CHUNK, XBC_W), rev),
                   pl.BlockSpec((1, CHUNK, N_HEADS), rev3),
                   pl.BlockSpec((1, N_HEADS, CHUNK), rev3),
                   pl.BlockSpec((1, N_HEADS), lambda c: (0, 0)),
                   pl.BlockSpec((N_HEADS, 1), lambda c: (0, 0)),
                   pl.BlockSpec((1, SSD_W), lambda c: (0, 0))],
        out_shape=[_SDS((lp, XBC_W), F32), _SDS((nc, CHUNK, N_HEADS), F32), _SDS((nc, N_HEADS, CHUNK), F32),
                   _SDS((1, N_HEADS), F32), _SDS((N_HEADS, 1), F32), _SDS((1, SSD_W), F32)],
        scratch_shapes=[pltpu.VMEM((N_GROUPS * N_STATE, gw), F32), pltpu.VMEM((CHUNK, SSD_W), F32),
                        pltpu.VMEM((CHUNK, SSD_W), F32), pltpu.VMEM((CHUNK, SSD_W), F32),
                        pltpu.VMEM((8, SSD_W), F32)],
        compiler_params=_params(("arbitrary",)),
    )(xbc, dt_c, dt_tc, a, a_t, dskip_x, states, d_y)


def _gated_norm(o, gate, w):
    sg = _sigmoid(gate)
    p = o * (gate * sg)
    rs = lax.rsqrt(jnp.mean(p * p, axis=-1, keepdims=True) + EPS)
    n = p * rs
    return sg, rs, n, n * w


def _tail_fwd(o_sb, o_ssd, proj, h0, target, w_out, sb_w, ssd_w, fin_w):
    lp = o_sb.shape[0]
    nb = lp // TM
    row = lambda i: (i, 0)
    one = lambda i: (0, 0)

    def body(osb_ref, gate_ref, ossd_ref, z_ref, h0_ref, tgt_ref, wo_ref, sbw_ref, ssdw_ref, fw_ref,
             dh1_ref, loss_ref, gfw_ref):
        i = pl.program_id(0)

        @pl.when(i == 0)
        def _():
            loss_ref[...] = jnp.zeros_like(loss_ref)
            gfw_ref[...] = jnp.zeros_like(gfw_ref)

        y1 = _gated_norm(osb_ref[...], gate_ref[...], sbw_ref[...])[3]
        y2 = _gated_norm(ossd_ref[...], z_ref[...], ssdw_ref[...])[3]
        h1 = (h0_ref[...] + _mm(y1, wo_ref[0:SB_W, :])) + _mm(y2, wo_ref[SB_W:SB_W + SSD_W, :])
        rs1 = lax.rsqrt(jnp.mean(h1 * h1, axis=-1, keepdims=True) + EPS)
        n1 = h1 * rs1
        fw = fw_ref[...]
        diff = jnp.where(i > 0, n1 * fw - tgt_ref[...], 0.0)
        loss_ref[...] += jnp.sum(diff * diff, axis=0, keepdims=True)
        d_out = diff * (1.0 / D_MODEL)
        gfw_ref[...] += jnp.sum(d_out * n1, axis=0, keepdims=True)
        g = d_out * fw
        dh1_ref[...] = rs1 * (g - n1 * jnp.mean(g * n1, axis=-1, keepdims=True))

    return pl.pallas_call(
        body, name="tail_fwd", grid=(nb,),
        in_specs=[pl.BlockSpec((TM, SB_W), row),
                  pl.BlockSpec((TM, SB_W), lambda i: (i, COL_GATE // SB_W)),
                  pl.BlockSpec((TM, SSD_W), row),
                  pl.BlockSpec((TM, SSD_W), lambda i: (i, COL_Z // SSD_W)),
                  pl.BlockSpec((TM, D_MODEL), row),
                  pl.BlockSpec((TM, D_MODEL), lambda i: (jnp.maximum(i - 1, 0), 0)),
                  pl.BlockSpec(memory_space=_VMEM),
                  pl.BlockSpec((1, SB_W), one), pl.BlockSpec((1, SSD_W), one), pl.BlockSpec((1, D_MODEL), one)],
        out_specs=[pl.BlockSpec((TM, D_MODEL), row), pl.BlockSpec((1, D_MODEL), one), pl.BlockSpec((1, D_MODEL), one)],
        out_shape=[_SDS((lp, D_MODEL), F32), _SDS((1, D_MODEL), F32), _SDS((1, D_MODEL), F32)],
        compiler_params=_params(("arbitrary",), 40),
    )(o_sb, proj, o_ssd, proj, h0, target, w_out, sb_w, ssd_w, fin_w)


def _gated_norm_bwd(o, gate, w, dy):
    sg, rs, n, _ = _gated_norm(o, gate, w)
    gw = jnp.sum(dy * n, axis=0, keepdims=True)
    dn = dy * w
    dp = rs * (dn - n * jnp.mean(dn * n, axis=-1, keepdims=True))
    d_o = dp * (gate * sg)
    d_gate = dp * o * (sg * (1.0 + gate * (1.0 - sg)))
    return d_o, d_gate, gw, n * w


def _tail_bwd(o_sb, o_ssd, proj, d_h1, w_out, sb_w, ssd_w):
    lp = o_sb.shape[0]
    nb = lp // TM
    row = lambda i: (i, 0)
    one = lambda i: (0, 0)

    def body(osb_ref, gate_ref, ossd_ref, z_ref, dh1_ref, wo_ref, sbw_ref, ssdw_ref,
             dosb_ref, dgate_ref, dossd_ref, dz_ref, gwo_ref, gsb_ref, gssd_ref):
        i = pl.program_id(0)

        @pl.when(i == 0)
        def _():
            gwo_ref[...] = jnp.zeros_like(gwo_ref)
            gsb_ref[...] = jnp.zeros_like(gsb_ref)
            gssd_ref[...] = jnp.zeros_like(gssd_ref)

        dh1 = dh1_ref[...].astype(_MXU)
        dy1 = lax.dot_general(dh1, wo_ref[0:SB_W, :], _NT, preferred_element_type=F32)
        dy2 = lax.dot_general(dh1, wo_ref[SB_W:SB_W + SSD_W, :], _NT, preferred_element_type=F32)
        d_o, d_g, gw, y1 = _gated_norm_bwd(osb_ref[...], gate_ref[...], sbw_ref[...], dy1)
        dosb_ref[...] = d_o
        dgate_ref[...] = d_g
        gsb_ref[...] += gw
        gwo_ref[0:SB_W, :] += lax.dot_general(y1.astype(_MXU), dh1, _TN, preferred_element_type=F32)
        d_o, d_g, gw, y2 = _gated_norm_bwd(ossd_ref[...], z_ref[...], ssdw_ref[...], dy2)
        dossd_ref[...] = d_o
        dz_ref[...] = d_g
        gssd_ref[...] += gw
        gwo_ref[SB_W:SB_W + SSD_W, :] += lax.dot_general(y2.astype(_MXU), dh1, _TN, preferred_element_type=F32)

    tile = pl.BlockSpec((TM, SB_W), row)
    return pl.pallas_call(
        body, name="tail_bwd", grid=(nb,),
        in_specs=[tile, pl.BlockSpec((TM, SB_W), lambda i: (i, COL_GATE // SB_W)),
                  tile, pl.BlockSpec((TM, SSD_W), lambda i: (i, COL_Z // SSD_W)),
                  tile, pl.BlockSpec(memory_space=_VMEM),
                  pl.BlockSpec((1, SB_W), one), pl.BlockSpec((1, SSD_W), one)],
        out_specs=[tile, tile, tile, tile,
                   pl.BlockSpec((SB_W + SSD_W, D_MODEL), one), pl.BlockSpec((1, SB_W), one), pl.BlockSpec((1, SSD_W), one)],
        out_shape=[_SDS((lp, SB_W), F32)] * 4 + [_SDS((SB_W + SSD_W, D_MODEL), F32), _SDS((1, SB_W), F32), _SDS((1, SSD_W), F32)],
        compiler_params=_params(("arbitrary",), 48),
    )(o_sb, proj, o_ssd, proj, d_h1, w_out, sb_w, ssd_w)


def _in_bwd(d_q, d_k, d_v, d_gate, d_z, d_xbc, d_dt128, w_main, w_dt, h0, d_h1, norm_w):
    lp = h0.shape[0]
    nb = lp // TM
    seq = lp - OFF
    row = lambda i: (i, 0)
    one = lambda i: (0, 0)
    secs = ((0, SB_W), (SB_W, SB_W), (2 * SB_W, SB_W), (COL_GATE, SB_W), (COL_Z, SSD_W), (COL_XBC, XBC_W))

    def body(dq_ref, dk_ref, dv_ref, dg_ref, dz_ref, dx_ref, ddt_ref, w_ref, wdt_ref, h0_ref, dh1_ref, nw_ref,
             gx_ref, gmeta_ref, gnw_ref):
        i = pl.program_id(0)

        @pl.when(i == 0)
        def _():
            gnw_ref[...] = jnp.zeros_like(gnw_ref)

        du = lax.dot_general(ddt_ref[...].astype(_MXU), wdt_ref[...], _NT, preferred_element_type=F32)
        for ref, (c0, width) in zip((dq_ref, dk_ref, dv_ref, dg_ref, dz_ref, dx_ref), secs):
            du = du + lax.dot_general(ref[...].astype(_MXU), w_ref[:, c0:c0 + width], _NT, preferred_element_type=F32)
        h = h0_ref[...]
        rs = lax.rsqrt(jnp.mean(h * h, axis=-1, keepdims=True) + EPS)
        n0 = h * rs
        gnw_ref[...] += jnp.sum(du * n0, axis=0, keepdims=True)
        g = du * nw_ref[...]
        dh0 = dh1_ref[...] + rs * (g - n0 * jnp.mean(g * n0, axis=-1, keepdims=True))

        @pl.when(i == 0)
        def _():
            gmeta_ref[...] = dh0[PAD:PAD + N_META, :]

        @pl.when(i > 0)
        def _():
            gx_ref[...] = dh0

    tile = pl.BlockSpec((TM, D_MODEL), row)
    return pl.pallas_call(
        body, name="in_bwd", grid=(nb,),
        in_specs=[tile, tile, tile, tile, tile, pl.BlockSpec((TM, XBC_W), row), pl.BlockSpec((TM, 128), row),
                  pl.BlockSpec(memory_space=_VMEM), pl.BlockSpec(memory_space=_VMEM),
                  tile, tile, pl.BlockSpec((1, D_MODEL), one)],
        out_specs=[pl.BlockSpec((TM, D_MODEL), lambda i: (jnp.maximum(i - 1, 0), 0)),
                   pl.BlockSpec((N_META, D_MODEL), one), pl.BlockSpec((1, D_MODEL), one)],
        out_shape=[_SDS((seq, D_MODEL), F32), _SDS((N_META, D_MODEL), F32), _SDS((1, D_MODEL), F32)],
        compiler_params=_params(("arbitrary",), 48),
    )(d_q, d_k, d_v, d_gate, d_z, d_xbc, d_dt128, w_main, w_dt, h0, d_h1, norm_w)


def _grad_w(u_t, d_sec, name):
    lp, n = d_sec.shape
    tn = min(512, n)

    def body(ut_ref, d_ref, o_ref):
        o_ref[...] = jnp.dot(ut_ref[...], d_ref[...].astype(_MXU), preferred_element_type=F32)

    return pl.pallas_call(
        body, name=name, grid=(n // tn,),
        in_specs=[pl.BlockSpec((D_MODEL, lp), lambda j: (0, 0)), pl.BlockSpec((lp, tn), lambda j: (0, j))],
        out_specs=pl.BlockSpec((D_MODEL, tn), lambda j: (0, j)),
        out_shape=_SDS((D_MODEL, n), F32),
        compiler_params=_params(("arbitrary",), 40),
    )(u_t, d_sec)


def _device_grads(x2d, target2d, meta_full, norm_w, w_main, w_dt, conv_w, conv_b, dt_bias, a_log, d_skip,
                  sb_w, ssd_w, w_out, fin_w):
    lp = x2d.shape[0] + OFF
    nc = lp // CHUNK
    h0, u, u_t = _prep(x2d, meta_full, norm_w)
    proj, dt_raw = _inproj(u, w_main, w_dt)
    o_sb, o_lo = _sb_fwd(proj)
    dt_bias128 = jnp.pad(dt_bias, ((0, 0), (0, 128 - N_HEADS)))
    xbc, dt128 = _conv_fwd(proj, dt_raw, conv_w, conv_b, dt_bias128)
    dt_c = dt128[:, :N_HEADS].reshape(nc, CHUNK, N_HEADS)
    dt_tc = jnp.swapaxes(dt_c, 1, 2)
    a = -jnp.exp(a_log)
    a_t = a.reshape(N_HEADS, 1)
    dskip_x = jnp.repeat(d_skip, HEAD, axis=1)
    o_ssd, states = _ssd_fwd(xbc, dt_c, dt_tc, a, a_t, dskip_x)
    d_h1, sq_err, g_fin = _tail_fwd(o_sb, o_ssd, proj, h0, target2d, w_out, sb_w, ssd_w, fin_w)

    d_osb, d_gate, d_ossd, d_z, g_wout, g_sb, g_ssd = _tail_bwd(o_sb, o_ssd, proj, d_h1, w_out, sb_w, ssd_w)
    d_q, d_k, d_v = _sb_bwd(proj, o_sb, o_lo, d_osb)
    d_xbc_act, ddt_a, ddt_b, ga1, ga2, gd = _ssd_bwd(xbc, dt_c, dt_tc, a, a_t, dskip_x, states, d_ossd)
    d_dt = (ddt_a + jnp.swapaxes(ddt_b, 1, 2)).reshape(lp, N_HEADS)
    d_dt128 = jnp.pad(d_dt, ((0, 0), (0, 128 - N_HEADS)))
    d_xbc, g_convw, g_convb, d_dtraw128, g_dtb128 = _conv_bwd(proj, dt_raw, conv_w, conv_b, dt_bias128, d_xbc_act, d_dt128)
    g_x, g_meta, g_nw = _in_bwd(d_q, d_k, d_v, d_gate, d_z, d_xbc, d_dtraw128, w_main, w_dt, h0, d_h1, norm_w)
    g_win = jnp.concatenate(
        [_grad_w(u_t, d, "gw_" + nm) for nm, d in (("q", d_q), ("k", d_k), ("v", d_v), ("gate", d_gate), ("z", d_z), ("xbc", d_xbc))]
        + [_grad_w(u_t, d_dtraw128, "gw_dt")[:, :N_HEADS]], axis=1)
    g_alog = (ga1 + ga2.reshape(1, N_HEADS)) * a
    g_dskip = gd.reshape(N_HEADS, HEAD).sum(axis=1).reshape(1, N_HEADS)
    grads = dict(meta_tokens=g_meta, norm_w=g_nw, w_in=g_win, conv_w=g_convw, conv_b=g_convb,
                 dt_bias=g_dtb128[:, :N_HEADS], a_log=g_alog, d_skip=g_dskip, sb_norm_w=g_sb, ssd_norm_w=g_ssd,
                 w_out=g_wout, final_norm_w=g_fin)
    return sq_err, g_x, grads


_MESH = pl.DeviceIdType.MESH
_ANY = pl.BlockSpec(memory_space=pl.ANY)


def _place():
    return lax.axis_index("x"), lax.axis_index("y"), lax.axis_index("c")


def _other_chips(x, y):
    return ((1 - x, y), (x, 1 - y), (1 - x, 1 - y))


def _gather_shards(arrays, n_big):
    n = len(arrays)

    def body(*refs):
        srcs, dsts = refs[:n], refs[n:2 * n]
        send_sems, recv_sems, fwd_send, fwd_recv = refs[2 * n:]
        x, y, c = _place()
        mine = 2 * x + y
        chips = _other_chips(x, y)

        def window(a):
            half = arrays[a].shape[0] // 2
            return pl.ds(pl.multiple_of(c * half, 16), half)

        first = []
        for a in range(n):
            for k, (px, py) in enumerate(chips):
                if a < n_big:
                    src, dst = srcs[a].at[window(a)], dsts[a].at[mine, window(a)]
                else:
                    src, dst = srcs[a], dsts[a].at[mine]
                cp = pltpu.make_async_remote_copy(
                    src_ref=src, dst_ref=dst, send_sem=send_sems.at[a * 3 + k], recv_sem=recv_sems.at[a * 3 + k],
                    device_id=(px, py, c), device_id_type=_MESH)
                cp.start()
                first.append(cp)
        passed = []
        for a in range(n):
            for k, (px, py) in enumerate(chips):
                first[a * 3 + k].wait_recv()
                if a < n_big:
                    landed = dsts[a].at[2 * px + py, window(a)]
                    cp = pltpu.make_async_remote_copy(
                        src_ref=landed, dst_ref=landed, send_sem=fwd_send.at[a * 3 + k], recv_sem=fwd_recv.at[a * 3 + k],
                        device_id=(x, y, 1 - c), device_id_type=_MESH)
                    cp.start()
                    passed.append(cp)
        for cp in passed:
            cp.wait_recv()
        for cp in first + passed:
            cp.wait_send()

    got = pl.pallas_call(
        body, name="gather_shards",
        in_specs=[_ANY] * n, out_specs=[_ANY] * n,
        out_shape=[_SDS((N_CHIPS,) + a.shape, a.dtype) for a in arrays],
        scratch_shapes=[pltpu.SemaphoreType.DMA((3 * n,)), pltpu.SemaphoreType.DMA((3 * n,)),
                        pltpu.SemaphoreType.DMA((3 * n_big,)), pltpu.SemaphoreType.DMA((3 * n_big,))],
    )(*arrays)
    mine = 2 * lax.axis_index("x") + lax.axis_index("y")
    return [lax.dynamic_update_slice(g, a[None], (mine,) + (0,) * a.ndim) for g, a in zip(got, arrays)]


def _scatter_slabs(arrays):
    n = len(arrays)

    def body(*refs):
        srcs, dsts = refs[:n], refs[n:2 * n]
        send_sems, recv_sems = refs[2 * n:]
        x, y, c = _place()
        mine = 2 * x + y
        remote = []
        for a in range(n):
            for k, (px, py) in enumerate(_other_chips(x, y)):
                cp = pltpu.make_async_remote_copy(
                    src_ref=srcs[a].at[2 * px + py], dst_ref=dsts[a].at[mine],
                    send_sem=send_sems.at[a * 3 + k], recv_sem=recv_sems.at[a * 3 + k],
                    device_id=(px, py, c), device_id_type=_MESH)
                cp.start()
                remote.append(cp)
        for cp in remote:
            cp.wait_recv()
        for cp in remote:
            cp.wait_send()

    return pl.pallas_call(
        body, name="scatter_slabs",
        in_specs=[_ANY] * n, out_specs=[_ANY] * n,
        out_shape=[_SDS(a.shape, a.dtype) for a in arrays],
        scratch_shapes=[pltpu.SemaphoreType.DMA((3 * n,)), pltpu.SemaphoreType.DMA((3 * n,))],
    )(*arrays)


def _swap_halves(arrays):
    n = len(arrays)

    def body(*refs):
        srcs, dsts = refs[:n], refs[n:2 * n]
        send_sems, recv_sems = refs[2 * n:]
        x, y, c = _place()
        copies = []
        for a in range(n):
            half = arrays[a].shape[1] // 2
            cp = pltpu.make_async_remote_copy(
                src_ref=srcs[a].at[:, pl.ds(pl.multiple_of((1 - c) * half, 16), half)], dst_ref=dsts[a],
                send_sem=send_sems.at[a], recv_sem=recv_sems.at[a],
                device_id=(x, y, 1 - c), device_id_type=_MESH)
            cp.start()
            copies.append(cp)
        for cp in copies:
            cp.wait_recv()
        for cp in copies:
            cp.wait_send()

    return pl.pallas_call(
        body, name="swap_halves",
        in_specs=[_ANY] * n, out_specs=[_ANY] * n,
        out_shape=[_SDS((a.shape[0], a.shape[1] // 2, a.shape[2]), a.dtype) for a in arrays],
        scratch_shapes=[pltpu.SemaphoreType.DMA((n,)), pltpu.SemaphoreType.DMA((n,))],
    )(*arrays)


def _join_halves(arrays):
    n = len(arrays)

    def body(*refs):
        dsts = refs[n:2 * n]
        send_sems, recv_sems = refs[2 * n:]
        x, y, c = _place()
        copies = []
        for a in range(n):
            half = arrays[a].shape[0] // 2
            rows = pl.ds(pl.multiple_of(c * half, 16), half)
            cp = pltpu.make_async_remote_copy(
                src_ref=dsts[a].at[rows], dst_ref=dsts[a].at[rows], send_sem=send_sems.at[a], recv_sem=recv_sems.at[a],
                device_id=(x, y, 1 - c), device_id_type=_MESH)
            cp.start()
            copies.append(cp)
        for cp in copies:
            cp.wait_recv()
        for cp in copies:
            cp.wait_send()

    return pl.pallas_call(
        body, name="join_halves",
        in_specs=[_ANY] * n, out_specs=[_ANY] * n,
        out_shape=[_SDS(a.shape, a.dtype) for a in arrays],
        input_output_aliases={a: a for a in range(n)},
        scratch_shapes=[pltpu.SemaphoreType.DMA((n,)), pltpu.SemaphoreType.DMA((n,))],
    )(*arrays)


N_DEV = 8
SMALL_ROWS = 32
SMALL_COLS = XBC_W


def _gather_small(packed):
    def body(src_ref, dst_ref, send_sems, recv_sems, local_sem):
        x, y, c = _place()
        me = 4 * x + 2 * y + c
        own = pltpu.make_async_copy(src_ref, dst_ref.at[me], local_sem)
        own.start()
        copies = []
        for k in range(1, N_DEV):
            bx, by, bc = (k >> 2) & 1, (k >> 1) & 1, k & 1
            peer = (x + bx - 2 * x * bx, y + by - 2 * y * by, c + bc - 2 * c * bc)
            cp = pltpu.make_async_remote_copy(
                src_ref=src_ref, dst_ref=dst_ref.at[me], send_sem=send_sems.at[k - 1], recv_sem=recv_sems.at[k - 1],
                device_id=peer, device_id_type=_MESH)
            cp.start()
            copies.append(cp)
        for cp in copies:
            cp.wait_recv()
        for cp in copies:
            cp.wait_send()
        own.wait()

    return pl.pallas_call(
        body, name="gather_small",
        in_specs=[pl.BlockSpec(memory_space=_VMEM)], out_specs=pl.BlockSpec(memory_space=_VMEM),
        out_shape=_SDS((N_DEV, SMALL_ROWS, SMALL_COLS), F32),
        scratch_shapes=[pltpu.SemaphoreType.DMA((N_DEV - 1,)), pltpu.SemaphoreType.DMA((N_DEV - 1,)),
                        pltpu.SemaphoreType.DMA],
    )(packed)


def _adamw(w, g, m, v):
    m = ADAM_B1 * m + (1.0 - ADAM_B1) * g
    v = ADAM_B2 * v + (1.0 - ADAM_B2) * (g * g)
    m_hat = m / (1.0 - ADAM_B1 ** ADAM_STEP)
    v_hat = v / (1.0 - ADAM_B2 ** ADAM_STEP)
    delta = -ADAM_LR * (m_hat / (jnp.sqrt(v_hat) + ADAM_EPS) + ADAM_WD * w)
    return delta, m, v


def _sum_slabs(slabs, core, name):
    _, h, c = slabs.shape
    tr = 128
    nblk = h // tr

    def body(core_ref, s_ref, o_ref):
        o_ref[...] = ((s_ref[0].astype(F32) + s_ref[1].astype(F32)) + s_ref[2].astype(F32)) + s_ref[3].astype(F32)

    grid_spec = pltpu.PrefetchScalarGridSpec(
        num_scalar_prefetch=1, grid=(nblk,),
        in_specs=[pl.BlockSpec((N_CHIPS, tr, c), lambda i, core_ref: (0, i, 0))],
        out_specs=pl.BlockSpec((tr, c), lambda i, core_ref: (core_ref[0] * nblk + i, 0)))
    return pl.pallas_call(
        body, name=name, grid_spec=grid_spec, out_shape=_SDS((2 * h, c), F32),
        compiler_params=_params(("arbitrary",)),
    )(core, slabs)


def _add_halves(own, recv, core, name):
    _, r, c = own.shape
    half = r // 2
    tr = 128
    nblk = half // tr

    def body(core_ref, a_ref, b_ref, o_ref):
        o_ref[...] = (a_ref[...].astype(F32) + b_ref[...].astype(F32)).astype(o_ref.dtype)

    grid_spec = pltpu.PrefetchScalarGridSpec(
        num_scalar_prefetch=1, grid=(nblk,),
        in_specs=[pl.BlockSpec((N_CHIPS, tr, c), lambda i, core_ref: (0, core_ref[0] * nblk + i, 0)),
                  pl.BlockSpec((N_CHIPS, tr, c), lambda i, core_ref: (0, i, 0))],
        out_specs=pl.BlockSpec((N_CHIPS, tr, c), lambda i, core_ref: (0, i, 0)))
    return pl.pallas_call(
        body, name=name, grid_spec=grid_spec, out_shape=_SDS((N_CHIPS, half, c), own.dtype),
        compiler_params=_params(("arbitrary",)),
    )(core, own, recv)


def _update_big(w, m, v, g, name):
    r, c = w.shape
    tr = 128

    def body(w_ref, m_ref, v_ref, g_ref, d_ref, mo_ref, vo_ref):
        delta, m_new, v_new = _adamw(w_ref[...], g_ref[...], m_ref[...], v_ref[...])
        d_ref[...] = delta
        mo_ref[...] = m_new
        vo_ref[...] = v_new

    spec = pl.BlockSpec((tr, c), lambda i: (i, 0))
    return pl.pallas_call(
        body, name=name, grid=(r // tr,),
        in_specs=[spec] * 4, out_specs=[spec] * 3,
        out_shape=[_SDS((r, c), F32)] * 3,
        compiler_params=_params(("arbitrary",)),
    )(w, m, v, g)


_ROW = dict(norm_w=0, sb_norm_w=1, ssd_norm_w=2, final_norm_w=3, conv_b=4, dt_bias=5, a_log=6, d_skip=7,
            conv_w=8, sq_err=12, meta_tokens=16)
_SMALL = ("meta_tokens", "norm_w", "conv_w", "conv_b", "dt_bias", "a_log", "d_skip", "sb_norm_w", "ssd_norm_w",
          "final_norm_w")


def _pack_small(sq_err, grads):
    def rowpad(a):
        return jnp.pad(a, ((0, 0), (0, SMALL_COLS - a.shape[1])))

    rows = [rowpad(grads[k]) for k in ("norm_w", "sb_norm_w", "ssd_norm_w", "final_norm_w", "conv_b", "dt_bias", "a_log", "d_skip")]
    rows.append(grads["conv_w"])
    rows.append(rowpad(sq_err))
    rows.append(jnp.zeros((3, SMALL_COLS), F32))
    rows.append(rowpad(grads["meta_tokens"]))
    return jnp.concatenate(rows, axis=0)


def _update_small(gathered, ws, ms, vs):
    names = _SMALL
    n = len(names)

    def body(*refs):
        g_ref = refs[0]
        w_refs, m_refs, v_refs = refs[1:1 + n], refs[1 + n:1 + 2 * n], refs[1 + 2 * n:1 + 3 * n]
        outs = refs[1 + 3 * n:]
        loss_ref = outs[0]
        go, do, mo, vo = outs[1:1 + n], outs[1 + n:1 + 2 * n], outs[1 + 2 * n:1 + 3 * n], outs[1 + 3 * n:1 + 4 * n]
        tot = g_ref[0]
        for d in range(1, N_DEV):
            tot = tot + g_ref[d]
        x, y, _ = _place()
        chip = 2 * x + y
        loss_ref[...] = jnp.broadcast_to(
            0.5 * jnp.sum(tot[_ROW["sq_err"]:_ROW["sq_err"] + 1, 0:D_MODEL], axis=1, keepdims=True) / D_MODEL, (1, 128))
        for idx, nm in enumerate(names):
            r0 = _ROW[nm]
            rows, cols = w_refs[idx].shape
            if nm in ("conv_w", "meta_tokens"):
                g = jnp.zeros((rows, cols), F32)
                for j in range(N_CHIPS):
                    g = g + jnp.where(chip == j, tot[r0:r0 + rows, j * cols:(j + 1) * cols], 0.0)
            else:
                g = tot[r0:r0 + rows, 0:cols]
            delta, m_new, v_new = _adamw(w_refs[idx][...], g, m_refs[idx][...], v_refs[idx][...])
            go[idx][...] = g
            do[idx][...] = delta
            mo[idx][...] = m_new
            vo[idx][...] = v_new

    shapes = [_SDS(ws[nm].shape, F32) for nm in names]
    vm = pl.BlockSpec(memory_space=_VMEM)
    res = pl.pallas_call(
        body, name="update_small",
        in_specs=[vm] * (1 + 3 * n), out_specs=[vm] * (1 + 4 * n),
        out_shape=[_SDS((1, 128), F32)] + shapes * 4,
    )(gathered, *[ws[nm] for nm in names], *[ms[nm] for nm in names], *[vs[nm] for nm in names])
    loss = res[0][0, 0]
    g = dict(zip(names, res[1:1 + n]))
    d = dict(zip(names, res[1 + n:1 + 2 * n]))
    m = dict(zip(names, res[1 + 2 * n:1 + 3 * n]))
    v = dict(zip(names, res[1 + 3 * n:1 + 4 * n]))
    return loss, g, d, m, v


_WEIGHTS = ("meta_tokens", "norm_w", "w_in", "conv_w", "conv_b", "dt_bias", "a_log", "d_skip", "sb_norm_w",
            "ssd_norm_w", "w_out", "final_norm_w")


def kernel(x, meta_tokens, norm_w, w_in, conv_w, conv_b, dt_bias, a_log, d_skip, sb_norm_w, ssd_norm_w, w_out, final_norm_w, loss_target, m_meta_tokens, m_norm_w, m_w_in, m_conv_w, m_conv_b, m_dt_bias, m_a_log, m_d_skip, m_sb_norm_w, m_ssd_norm_w, m_w_out, m_final_norm_w, v_meta_tokens, v_norm_w, v_w_in, v_conv_w, v_conv_b, v_dt_bias, v_a_log, v_d_skip, v_sb_norm_w, v_ssd_norm_w, v_w_out, v_final_norm_w):
    given = dict(meta_tokens=meta_tokens, norm_w=norm_w, w_in=w_in, conv_w=conv_w, conv_b=conv_b, dt_bias=dt_bias,
                 a_log=a_log, d_skip=d_skip, sb_norm_w=sb_norm_w, ssd_norm_w=ssd_norm_w, w_out=w_out,
                 final_norm_w=final_norm_w)
    mom = dict(meta_tokens=m_meta_tokens, norm_w=m_norm_w, w_in=m_w_in, conv_w=m_conv_w, conv_b=m_conv_b,
               dt_bias=m_dt_bias, a_log=m_a_log, d_skip=m_d_skip, sb_norm_w=m_sb_norm_w, ssd_norm_w=m_ssd_norm_w,
               w_out=m_w_out, final_norm_w=m_final_norm_w)
    var = dict(meta_tokens=v_meta_tokens, norm_w=v_norm_w, w_in=v_w_in, conv_w=v_conv_w, conv_b=v_conv_b,
               dt_bias=v_dt_bias, a_log=v_a_log, d_skip=v_d_skip, sb_norm_w=v_sb_norm_w, ssd_norm_w=v_ssd_norm_w,
               w_out=v_w_out, final_norm_w=v_final_norm_w)
    seq = x.shape[1]

    def two_d(a):
        return a.reshape((-1, a.shape[-1])) if a.ndim != 2 else a

    g_win, g_wout, g_meta, g_cw = _gather_shards(
        [w_in[0].astype(_MXU), w_out[0].astype(_MXU), meta_tokens, conv_w[0]], 2)
    w_in_full = jnp.swapaxes(g_win, 0, 1).reshape(D_MODEL, D_IN)
    w_main = w_in_full[:, :N_MAIN]
    w_dt = jnp.pad(w_in_full[:, N_MAIN:], ((0, 0), (0, 128 - N_HEADS)))
    w_out_full = g_wout.reshape(2 * D_MODEL, D_MODEL)
    meta_full = jnp.swapaxes(g_meta, 0, 1).reshape(N_META, D_MODEL)
    conv_w_full = jnp.swapaxes(g_cw, 0, 1).reshape(4, XBC_W)

    sq_err, g_x, grads = _device_grads(
        x.reshape(seq, D_MODEL), loss_target.reshape(seq, D_MODEL), meta_full, norm_w, w_main, w_dt, conv_w_full,
        conv_b, dt_bias, a_log, d_skip, sb_norm_w, ssd_norm_w, w_out_full, final_norm_w.reshape(1, D_MODEL))

    core = lax.axis_index("c").astype(jnp.int32).reshape(1)
    slab_in = jnp.swapaxes(grads["w_in"].reshape(D_MODEL, N_CHIPS, W_IN_SHARD), 0, 1).astype(_MXU)
    slab_out = grads["w_out"].reshape(N_CHIPS, W_OUT_SHARD, D_MODEL).astype(_MXU)
    sib_in, sib_out = _swap_halves([slab_in, slab_out])
    chip_in = _add_halves(slab_in, sib_in, core, "chip_sum_w_in")
    chip_out = _add_halves(slab_out, sib_out, core, "chip_sum_w_out")
    got_in, got_out = _scatter_slabs([chip_in, chip_out])
    chip = 2 * lax.axis_index("x") + lax.axis_index("y")

    def with_own(got, sent):
        own = lax.dynamic_slice(sent, (chip, 0, 0), (1,) + sent.shape[1:])
        return lax.dynamic_update_slice(got, own, (chip, 0, 0))

    g_in, g_out = _join_halves([_sum_slabs(with_own(got_in, chip_in), core, "sum_w_in"),
                                _sum_slabs(with_own(got_out, chip_out), core, "sum_w_out")])
    big = dict(w_in=(g_in,) + tuple(_update_big(w_in[0], m_w_in[0], v_w_in[0], g_in, "update_w_in")),
               w_out=(g_out,) + tuple(_update_big(w_out[0], m_w_out[0], v_w_out[0], g_out, "update_w_out")))

    gathered = _gather_small(_pack_small(sq_err, grads))
    loss, sg, sd, sm, sv = _update_small(
        gathered, {k: two_d(given[k]) for k in _SMALL}, {k: two_d(mom[k]) for k in _SMALL},
        {k: two_d(var[k]) for k in _SMALL})

    out = {}
    for idx, group in enumerate((sg, sd, sm, sv)):
        for k in _SMALL:
            out[(idx, k)] = group[k].reshape(given[k].shape)
        for k in ("w_in", "w_out"):
            out[(idx, k)] = big[k][idx].reshape(given[k].shape)
    return (loss, g_x.reshape(x.shape), *[out[(idx, k)] for idx in range(4) for k in _WEIGHTS])
```

```python
import functools
import math

import jax
import jax.numpy as jnp
from jax import lax
from jax.experimental import pallas as pl
from jax.experimental.pallas import tpu as pltpu

F32 = jnp.float32
_MXU = jnp.bfloat16

D_MODEL = 1024
N_META = 16
PAD = 112
OFF = PAD + N_META
TM = 128
CHUNK = 64
SB_W = 1024
SSD_W = 1024
N_HEADS = 16
HEAD = 64
N_GROUPS = 2
N_STATE = 128
XBC_W = SSD_W + 2 * N_GROUPS * N_STATE
N_MAIN = 4 * SB_W + SSD_W + XBC_W
COL_GATE = 3 * SB_W
COL_Z = 4 * SB_W
COL_XBC = 5 * SB_W
D_IN = N_MAIN + N_HEADS
W_ALL = N_MAIN + 128
WIN_STEP = 1664
WIN_W = 1792
EPS = 1e-5
N_CHIPS = 4
W_IN_SHARD = D_IN // N_CHIPS
W_OUT_SHARD = 2 * D_MODEL // N_CHIPS

ADAM_LR = 0.001
ADAM_B1 = 0.9
ADAM_B2 = 0.999
ADAM_EPS = 1e-08
ADAM_WD = 0.01
ADAM_STEP = 10

_SDS = jax.ShapeDtypeStruct
_NT = (((1,), (1,)), ((), ()))
_TN = (((0,), (0,)), ((), ()))
_VMEM = pltpu.VMEM


def _params(sem=None, vmem_mb=None):
    kw = {}
    if sem is not None:
        kw["dimension_semantics"] = sem
    if vmem_mb is not None:
        kw["vmem_limit_bytes"] = vmem_mb * 1024 * 1024
    return pltpu.CompilerParams(**kw)


def _mm(a, b):
    return jnp.dot(a.astype(_MXU), b.astype(_MXU), preferred_element_type=F32)


def _mm_nt(a, b):
    return lax.dot_general(a.astype(_MXU), b.astype(_MXU), _NT, preferred_element_type=F32)


def _mm_tn(a, b):
    return lax.dot_general(a.astype(_MXU), b.astype(_MXU), _TN, preferred_element_type=F32)


def _split(x, parts):
    out = []
    r = x
    for _ in range(parts):
        p = r.astype(_MXU)
        out.append(p)
        r = r - p.astype(F32)
    return out


def _sel_right(x, m01, parts=3):
    acc = None
    for p in _split(x, parts):
        t = jnp.dot(p, m01, preferred_element_type=F32)
        acc = t if acc is None else acc + t
    return acc


def _sel_left(m01, x, parts=3):
    acc = None
    for p in _split(x, parts):
        t = jnp.dot(m01, p, preferred_element_type=F32)
        acc = t if acc is None else acc + t
    return acc


def _iota(shape, axis):
    return lax.broadcasted_iota(jnp.int32, shape, axis)


def _sigmoid(x):
    return 1.0 / (1.0 + jnp.exp(-x))


def _prep(x2d, meta_full, norm_w):
    seq = x2d.shape[0]
    lp = seq + OFF
    nb = lp // TM

    def body(x_ref, meta_ref, w_ref, h0_ref, u_ref, ut_ref):
        i = pl.program_id(0)

        @pl.when(i == 0)
        def _():
            h0_ref[...] = jnp.concatenate([jnp.zeros((PAD, D_MODEL), F32), meta_ref[...]], axis=0)

        @pl.when(i > 0)
        def _():
            h0_ref[...] = x_ref[...]

        h = h0_ref[...]
        rs = lax.rsqrt(jnp.mean(h * h, axis=-1, keepdims=True) + EPS)
        u = (h * rs * w_ref[...]).astype(_MXU)
        u_ref[...] = u
        ut_ref[...] = u.T

    return pl.pallas_call(
        body, name="prep", grid=(nb,),
        in_specs=[pl.BlockSpec((TM, D_MODEL), lambda i: (jnp.maximum(i - 1, 0), 0)),
                  pl.BlockSpec((N_META, D_MODEL), lambda i: (0, 0)),
                  pl.BlockSpec((1, D_MODEL), lambda i: (0, 0))],
        out_specs=[pl.BlockSpec((TM, D_MODEL), lambda i: (i, 0)),
                   pl.BlockSpec((TM, D_MODEL), lambda i: (i, 0)),
                   pl.BlockSpec((D_MODEL, TM), lambda i: (0, i))],
        out_shape=[_SDS((lp, D_MODEL), F32), _SDS((lp, D_MODEL), _MXU), _SDS((D_MODEL, lp), _MXU)],
        compiler_params=_params(("arbitrary",)),
    )(x2d, meta_full, norm_w)


def _inproj(u, w_all):
    lp = u.shape[0]
    tn = 512

    def body(u_ref, w_ref, wdt_ref, o_ref, odt_ref):
        o_ref[...] = jnp.dot(u_ref[...], w_ref[...], preferred_element_type=F32)

        @pl.when(pl.program_id(0) == 0)
        def _():
            odt_ref[...] = jnp.dot(u_ref[...], wdt_ref[...], preferred_element_type=F32)

    return pl.pallas_call(
        body, name="inproj", grid=(N_MAIN // tn,),
        in_specs=[pl.BlockSpec((lp, D_MODEL), lambda j: (0, 0)),
                  pl.BlockSpec((D_MODEL, tn), lambda j: (0, j)),
                  pl.BlockSpec((D_MODEL, 128), lambda j: (0, N_MAIN // 128))],
        out_specs=[pl.BlockSpec((lp, tn), lambda j: (0, j)),
                   pl.BlockSpec((lp, 128), lambda j: (0, 0))],
        out_shape=[_SDS((lp, N_MAIN), F32), _SDS((lp, 128), F32)],
        compiler_params=_params(("arbitrary",), 48),
    )(u, w_all, w_all)


SB_WINDOW = 3
SB_DEAD = -104.0


def _sb_logs(qh, kwin):
    z = lax.dot_general(qh, kwin, _NT, preferred_element_type=F32)
    e = jnp.exp(-jnp.abs(z))
    l1p = jnp.log(1.0 + e)
    lk_full = -(jnp.maximum(z, 0.0) + l1p)
    ls = jnp.minimum(z, 0.0) - l1p
    return z, e, ls, lk_full


def _blk(a, b):
    return a[:, b * TM:(b + 1) * TM]


def _stacked_sel(blocks, m01):
    n = len(blocks)
    pieces = [_split(b, 2) for b in blocks]
    stacked = jnp.concatenate([p[0] for p in pieces] + [p[1] for p in pieces], axis=0)
    res = jnp.dot(stacked, m01, preferred_element_type=F32)
    return [res[j * TM:(j + 1) * TM] + res[(n + j) * TM:(n + j + 1) * TM] for j in range(n)]


def _sb_weights(ls, lk_full, run, last_mask, upper, n):
    lk = [_blk(lk_full, b) for b in range(n)]
    lk[n - 1] = jnp.where(last_mask, lk[n - 1], 0.0)
    aft = _stacked_sel(lk, upper)
    w = [None] * n
    for b in range(n - 1, -1, -1):
        wb = jnp.exp(_blk(ls, b) + aft[b] + run)
        w[b] = jnp.where(last_mask, wb, 0.0) if b == n - 1 else wb
        run = run + jnp.sum(lk[b], axis=1, keepdims=True)
    return w, run


def _sb_fwd(proj):
    lp = proj.shape[0]
    nb = lp // TM

    def body(q_ref, k_ref, v_ref, o_ref, olo_ref, acc, run_scr):
        i = pl.program_id(1)
        lane = _iota((TM, TM), 1)
        row = _iota((TM, TM), 0)
        head0 = lane < HEAD
        upper = (row > lane).astype(_MXU)
        strict = lane < row
        q = q_ref[...] * (1.0 / math.sqrt(HEAD))
        qh = (jnp.where(head0, q, 0.0).astype(_MXU), jnp.where(head0, 0.0, q).astype(_MXU))

        def key_set(first, n, last_mask):
            off = pl.multiple_of(first * TM, TM)
            kwin = k_ref[pl.ds(off, n * TM), :].astype(_MXU)
            vwin = v_ref[pl.ds(off, n * TM), :].astype(_MXU)
            alive = None
            for hh in range(2):
                run = run_scr[hh][:, 0:1]
                _, _, ls, lk_full = _sb_logs(qh[hh], kwin)
                w, run = _sb_weights(ls, lk_full, run, last_mask, upper, n)
                pieces = [_split(wb, 2) for wb in w]
                stacked = jnp.concatenate(
                    [jnp.concatenate([p[0] for p in pieces], axis=1), jnp.concatenate([p[1] for p in pieces], axis=1)], axis=0)
                res = jnp.dot(stacked, vwin, preferred_element_type=F32)
                acc[hh] += res[0:TM]
                acc[2 + hh] += res[TM:2 * TM]
                run_scr[hh] = jnp.broadcast_to(run, (TM, TM))
                top = jnp.max(run)
                alive = top if alive is None else jnp.maximum(alive, top)
            return (alive > SB_DEAD).astype(jnp.int32)

        acc[...] = jnp.zeros_like(acc)
        run_scr[...] = jnp.zeros_like(run_scr)

        @pl.when(i >= SB_WINDOW - 1)
        def _():
            key_set(i - (SB_WINDOW - 1), SB_WINDOW, strict)

        start = jnp.where(i >= SB_WINDOW - 1, i - SB_WINDOW, i)
        alive0 = (jnp.max(run_scr[...]) > SB_DEAD).astype(jnp.int32)

        def cond(c):
            return jnp.logical_and(c[0] >= 0, c[1] > 0)

        def step(c):
            kb = c[0]
            return kb - 1, key_set(kb, 1, jnp.logical_or(strict, kb < i))

        lax.while_loop(cond, step, (start, alive0))
        o_ref[...] = jnp.where(head0, acc[0], acc[1])
        olo_ref[...] = jnp.where(head0, acc[2], acc[3])

    npair = SB_W // TM
    blk = pl.BlockSpec((TM, TM), lambda p, i: (i, p))
    return pl.pallas_call(
        body, name="sb_fwd", grid=(npair, nb),
        in_specs=[blk,
                  pl.BlockSpec((lp, TM), lambda p, i: (0, npair + p)),
                  pl.BlockSpec((lp, TM), lambda p, i: (0, 2 * npair + p))],
        out_specs=[blk, blk],
        out_shape=[_SDS((lp, SB_W), F32), _SDS((lp, SB_W), F32)],
        scratch_shapes=[pltpu.VMEM((4, TM, TM), F32), pltpu.VMEM((2, TM, TM), F32)],
        compiler_params=_params(("arbitrary", "arbitrary")),
    )(proj, proj, proj)


def _sb_bwd(proj, o_sb, o_lo, d_o, d_proj):
    lp = proj.shape[0]
    nb = lp // TM
    npair = SB_W // TM
    scale = 1.0 / math.sqrt(HEAD)

    def body(q_ref, k_ref, v_ref, o_ref, olo_ref, do_ref, dproj_in, dproj_ref,
             dq_all, dk_ref, dv_ref, stage, sems, dq_acc, run_scr, gsum_scr):
        p = pl.program_id(0)
        i = pl.program_id(1)

        @pl.when(i == 0)
        def _():
            dk_ref[...] = jnp.zeros_like(dk_ref)
            dv_ref[...] = jnp.zeros_like(dv_ref)

        lane = _iota((TM, TM), 1)
        row = _iota((TM, TM), 0)
        head0 = lane < HEAD
        hmask = (head0, jnp.logical_not(head0))
        upper = (row > lane).astype(_MXU)
        lower_incl = (row >= lane).astype(_MXU)
        strict = lane < row
        q = q_ref[...] * scale
        do = do_ref[...]
        prod = do.astype(_MXU).astype(F32) * (o_ref[...] + olo_ref[...])
        qh = tuple(jnp.where(m, q, 0.0).astype(_MXU) for m in hmask)
        doh = tuple(jnp.where(m, do, 0.0).astype(_MXU) for m in hmask)
        gtot = tuple(jnp.sum(jnp.where(m, prod, 0.0), axis=1, keepdims=True) for m in hmask)

        def key_set(first, n, last_mask):
            off = pl.multiple_of(first * TM, TM)
            kf = k_ref[pl.ds(off, n * TM), :]
            kwin = kf.astype(_MXU)
            vwin = v_ref[pl.ds(off, n * TM), :].astype(_MXU)
            dk_win = None
            alive = None
            for hh in range(2):
                run = run_scr[hh][:, 0:1]
                gsum = gsum_scr[hh][:, 0:1]
                z, e, ls, lk_full = _sb_logs(qh[hh], kwin)
                w, run = _sb_weights(ls, lk_full, run, last_mask, upper, n)
                r = 1.0 / (1.0 + e)
                er = e * r
                pos = z >= 0.0
                beta = jnp.where(pos, r, er)
                one_m_beta = jnp.where(pos, er, r)
                dw = lax.dot_general(doh[hh], vwin, _NT, preferred_element_type=F32)
                g = [_blk(dw, b) * w[b] for b in range(n)]
                suffix = _stacked_sel(g, lower_incl)
                dz = [None] * n
                for b in range(n - 1, -1, -1):
                    prefix = gtot[hh] - gsum - suffix[b]
                    d = g[b] * _blk(one_m_beta, b) - _blk(beta, b) * prefix
                    dz[b] = (jnp.where(last_mask, d, 0.0) if b == n - 1 else d).astype(_MXU)
                    gsum = gsum + jnp.sum(g[b], axis=1, keepdims=True)
                dzw = jnp.concatenate(dz, axis=1)
                ww = jnp.concatenate([wb.astype(_MXU) for wb in w], axis=1)
                kh = jnp.where(hmask[hh][0:1, :], kf, 0.0).astype(_MXU)
                dq_acc[...] += jnp.dot(dzw, kh, preferred_element_type=F32)
                dk_h = lax.dot_general(dzw, qh[hh], _TN, preferred_element_type=F32)
                dv_h = lax.dot_general(ww, doh[hh], _TN, preferred_element_type=F32)
                dk_win = (dk_h, dv_h) if dk_win is None else (dk_win[0] + dk_h, dk_win[1] + dv_h)
                run_scr[hh] = jnp.broadcast_to(run, (TM, TM))
                gsum_scr[hh] = jnp.broadcast_to(gsum, (TM, TM))
                top = jnp.max(run)
                alive = top if alive is None else jnp.maximum(alive, top)
            dk_ref[pl.ds(off, n * TM), :] += dk_win[0]
            dv_ref[pl.ds(off, n * TM), :] += dk_win[1]
            return (alive > SB_DEAD).astype(jnp.int32)

        dq_acc[...] = jnp.zeros_like(dq_acc)
        run_scr[...] = jnp.zeros_like(run_scr)
        gsum_scr[...] = jnp.zeros_like(gsum_scr)

        @pl.when(i >= SB_WINDOW - 1)
        def _():
            key_set(i - (SB_WINDOW - 1), SB_WINDOW, strict)

        start = jnp.where(i >= SB_WINDOW - 1, i - SB_WINDOW, i)
        alive0 = (jnp.max(run_scr[...]) > SB_DEAD).astype(jnp.int32)

        def cond(c):
            return jnp.logical_and(c[0] >= 0, c[1] > 0)

        def step(c):
            kb = c[0]
            return kb - 1, key_set(kb, 1, jnp.logical_or(strict, kb < i))

        lax.while_loop(cond, step, (start, alive0))
        dq_all[pl.ds(pl.multiple_of(i * TM, TM), TM), :] = dq_acc[...] * scale

        @pl.when(i == nb - 1)
        def _():
            copies = []
            for s, src in enumerate((dq_all, dk_ref, dv_ref)):
                stage[s] = src[...].astype(_MXU)
                col = pl.multiple_of((s * npair + p) * TM, TM)
                copies.append(pltpu.make_async_copy(stage.at[s], dproj_ref.at[:, pl.ds(col, TM)], sems.at[s]))
                copies[-1].start()
            for cp in copies:
                cp.wait()

    blk = pl.BlockSpec((TM, TM), lambda p, i: (i, p))
    return pl.pallas_call(
        body, name="sb_bwd", grid=(npair, nb),
        in_specs=[blk,
                  pl.BlockSpec((lp, TM), lambda p, i: (0, npair + p)),
                  pl.BlockSpec((lp, TM), lambda p, i: (0, 2 * npair + p)),
                  blk, blk, blk, pl.BlockSpec(memory_space=pl.ANY)],
        out_specs=pl.BlockSpec(memory_space=pl.ANY),
        out_shape=_SDS(d_proj.shape, d_proj.dtype),
        input_output_aliases={6: 0},
        scratch_shapes=[pltpu.VMEM((lp, TM), F32), pltpu.VMEM((lp, TM), F32), pltpu.VMEM((lp, TM), F32),
                        pltpu.VMEM((3, lp, TM), _MXU), pltpu.SemaphoreType.DMA((3,)),
                        pltpu.VMEM((TM, TM), F32), pltpu.VMEM((2, TM, TM), F32), pltpu.VMEM((2, TM, TM), F32)],
        compiler_params=_params(("arbitrary", "arbitrary")),
    )(proj, proj, proj, o_sb, o_lo, d_o, d_proj)


def _conv_pre(x_ref, w_ref, b_ref, lp):
    n = lp - 8
    w = w_ref[...]
    pre = (x_ref[pl.ds(5, n), :] * w[0:1, :] + x_ref[pl.ds(6, n), :] * w[1:2, :]
           + x_ref[pl.ds(7, n), :] * w[2:3, :] + x_ref[pl.ds(8, n), :] * w[3:4, :]) + b_ref[...]
    live = (_iota((n, 128), 0) + 8) >= PAD
    return pre, live


def _conv_fwd(proj, dt_raw, conv_w, conv_b, dt_bias128):
    lp = proj.shape[0]
    nblk = XBC_W // 128
    c0 = COL_XBC // 128

    def body(x_ref, w_ref, b_ref, dtr_ref, dtb_ref, o_ref, dt_ref):
        pre, live = _conv_pre(x_ref, w_ref, b_ref, lp)
        act = pre * _sigmoid(pre)
        o_ref[pl.ds(0, 8), :] = jnp.zeros((8, 128), F32)
        o_ref[pl.ds(8, lp - 8), :] = jnp.where(live, act, 0.0)

        @pl.when(pl.program_id(0) == 0)
        def _():
            s = dtr_ref[...] + dtb_ref[...]
            sp = jnp.maximum(s, 0.0) + jnp.log(1.0 + jnp.exp(-jnp.abs(s)))
            dt_ref[...] = jnp.where(_iota((lp, 128), 0) >= PAD, sp, 0.0)

    return pl.pallas_call(
        body, name="conv_fwd", grid=(nblk,),
        in_specs=[pl.BlockSpec((lp, 128), lambda j: (0, c0 + j)),
                  pl.BlockSpec((4, 128), lambda j: (0, j)),
                  pl.BlockSpec((1, 128), lambda j: (0, j)),
                  pl.BlockSpec((lp, 128), lambda j: (0, 0)),
                  pl.BlockSpec((1, 128), lambda j: (0, 0))],
        out_specs=[pl.BlockSpec((lp, 128), lambda j: (0, j)),
                   pl.BlockSpec((lp, 128), lambda j: (0, 0))],
        out_shape=[_SDS((lp, XBC_W), F32), _SDS((lp, 128), F32)],
        compiler_params=_params(("arbitrary",)),
    )(proj, conv_w, conv_b, dt_raw, dt_bias128)


def _conv_bwd(proj, dt_raw, conv_w, conv_b, dt_bias128, d_xbc, d_dt128, d_proj):
    lp = proj.shape[0]
    nblk = XBC_W // 128
    c0 = COL_XBC // 128
    n = lp - 8
    last = nblk - 1

    def body(x_ref, w_ref, b_ref, dtr_ref, dtb_ref, dy_ref, ddt_ref, dproj_in,
             dx_ref, gw_ref, gb_ref, gdtb_ref, scr):
        j = pl.program_id(0)

        @pl.when(j < nblk)
        def _():
            pre, live = _conv_pre(x_ref, w_ref, b_ref, lp)
            sg = _sigmoid(pre)
            dpre = jnp.where(live, dy_ref[pl.ds(8, n), :] * (sg * (1.0 + pre * (1.0 - sg))), 0.0)
            gb_ref[...] = jnp.sum(dpre, axis=0, keepdims=True)
            gw_ref[...] = jnp.concatenate(
                [jnp.sum(dpre * x_ref[pl.ds(5 + k, n), :], axis=0, keepdims=True) for k in range(4)], axis=0)
            scr[pl.ds(0, 8), :] = jnp.zeros((8, 128), F32)
            scr[pl.ds(8, n), :] = dpre
            scr[pl.ds(lp, 8), :] = jnp.zeros((8, 128), F32)
            w = w_ref[...]
            dx_ref[...] = (scr[pl.ds(0, lp), :] * w[3:4, :] + scr[pl.ds(1, lp), :] * w[2:3, :]
                           + scr[pl.ds(2, lp), :] * w[1:2, :] + scr[pl.ds(3, lp), :] * w[0:1, :]).astype(dx_ref.dtype)

        @pl.when(j == nblk)
        def _():
            s = dtr_ref[...] + dtb_ref[...]
            d = jnp.where(_iota((lp, 128), 0) >= PAD, ddt_ref[...] * _sigmoid(s), 0.0)
            dx_ref[...] = d.astype(dx_ref.dtype)
            gdtb_ref[...] = jnp.sum(d, axis=0, keepdims=True)

    clamp = lambda j: (0, jnp.minimum(j, last))
    full128 = pl.BlockSpec((lp, 128), lambda j: (0, 0))
    return pl.pallas_call(
        body, name="conv_bwd", grid=(nblk + 1,),
        in_specs=[pl.BlockSpec((lp, 128), lambda j: (0, c0 + jnp.minimum(j, last))),
                  pl.BlockSpec((4, 128), clamp),
                  pl.BlockSpec((1, 128), clamp),
                  full128, pl.BlockSpec((1, 128), lambda j: (0, 0)),
                  pl.BlockSpec((lp, 128), clamp), full128, pl.BlockSpec(memory_space=pl.ANY)],
        out_specs=[pl.BlockSpec((lp, 128), lambda j: (0, c0 + j)), pl.BlockSpec((4, 128), clamp),
                   pl.BlockSpec((1, 128), clamp), pl.BlockSpec((1, 128), lambda j: (0, 0))],
        out_shape=[_SDS(d_proj.shape, d_proj.dtype), _SDS((4, XBC_W), F32), _SDS((1, XBC_W), F32), _SDS((1, 128), F32)],
        input_output_aliases={7: 0},
        scratch_shapes=[pltpu.VMEM((lp + 8, 128), F32)],
        compiler_params=_params(("arbitrary",)),
    )(proj, conv_w, conv_b, dt_raw, dt_bias128, d_xbc, d_dt128, d_proj)


def _ssd_pieces(dt, dt_t, a, a_t):
    r64 = _iota((CHUNK, CHUNK), 0)
    c64 = _iota((CHUNK, CHUNK), 1)
    tril = c64 <= r64
    tril01 = tril.astype(_MXU)
    triu01 = (r64 <= c64).astype(_MXU)
    expand = (lax.shift_right_logical(_iota((N_HEADS, SSD_W), 1), 6) == _iota((N_HEADS, SSD_W), 0)).astype(_MXU)
    acum = _sel_left(tril01, dt * a)
    acum_t = _sel_right(dt_t * a_t, triu01)
    ax = _sel_right(acum, expand)
    dtx = _sel_right(dt, expand)
    return tril, expand, acum, acum_t, ax, dtx


def _seg_matrix():
    return (lax.shift_right_logical(_iota((SSD_W, N_HEADS), 0), 6) == _iota((SSD_W, N_HEADS), 1)).astype(_MXU)


def _head_decay(ax, acum_t, h, tril):
    col = ax[:, h * HEAD:(h + 1) * HEAD]
    rowv = acum_t[h:h + 1, :]
    return jnp.where(tril, jnp.exp(jnp.minimum(col - rowv, 0.0)), 0.0)


def _ssd_fwd(xbc, dt_c, dt_tc, a, a_t, dskip_x):
    lp = xbc.shape[0]
    nc = lp // CHUNK
    gw = SSD_W // N_GROUPS
    hpg = N_HEADS // N_GROUPS

    def body(x_ref, dt_ref, dtt_ref, a_ref, at_ref, d_ref, y_ref, st_ref, state):
        c = pl.program_id(0)

        @pl.when(c == 0)
        def _():
            state[...] = jnp.zeros_like(state)

        st_ref[0] = state[...]
        tril, _, _, acum_t, ax, dtx = _ssd_pieces(dt_ref[0], dtt_ref[0], a_ref[...], at_ref[...])
        x = x_ref[:, 0:SSD_W]
        xdt = x * dtx
        ea = jnp.exp(ax)
        aex = ax[CHUNK - 1:CHUNK, :]
        wd = jnp.exp(aex - ax)
        eae = jnp.exp(aex)
        xw = xdt * wd
        y_ref[...] = x * d_ref[...]
        for g in range(N_GROUPS):
            gs = slice(g * gw, (g + 1) * gw)
            rs = slice(g * N_STATE, (g + 1) * N_STATE)
            bg = x_ref[:, SSD_W + g * N_STATE:SSD_W + (g + 1) * N_STATE]
            cg = x_ref[:, SSD_W + N_GROUPS * N_STATE + g * N_STATE:SSD_W + N_GROUPS * N_STATE + (g + 1) * N_STATE]
            sg = state[rs, :]
            cb = _mm_nt(cg, bg)
            y_ref[:, gs] += _mm(cg, sg) * ea[:, gs]
            for r in range(hpg):
                h = g * hpg + r
                hs = slice(h * HEAD, (h + 1) * HEAD)
                m = cb * _head_decay(ax, acum_t, h, tril)
                y_ref[:, hs] += _mm(m, xdt[:, hs])
            state[rs, :] = sg * eae[:, gs] + _mm_tn(bg, xw[:, gs])

    return pl.pallas_call(
        body, name="ssd_fwd", grid=(nc,),
        in_specs=[pl.BlockSpec((CHUNK, XBC_W), lambda c: (c, 0)),
                  pl.BlockSpec((1, CHUNK, N_HEADS), lambda c: (c, 0, 0)),
                  pl.BlockSpec((1, N_HEADS, CHUNK), lambda c: (c, 0, 0)),
                  pl.BlockSpec((1, N_HEADS), lambda c: (0, 0)),
                  pl.BlockSpec((N_HEADS, 1), lambda c: (0, 0)),
                  pl.BlockSpec((1, SSD_W), lambda c: (0, 0))],
        out_specs=[pl.BlockSpec((CHUNK, SSD_W), lambda c: (c, 0)),
                   pl.BlockSpec((1, N_GROUPS * N_STATE, gw), lambda c: (c, 0, 0))],
        out_shape=[_SDS((lp, SSD_W), F32), _SDS((nc, N_GROUPS * N_STATE, gw), F32)],
        scratch_shapes=[pltpu.VMEM((N_GROUPS * N_STATE, gw), F32)],
        compiler_params=_params(("arbitrary",)),
    )(xbc, dt_c, dt_tc, a, a_t, dskip_x)


def _ssd_bwd(xbc, dt_c, dt_tc, a, a_t, dskip_x, states, d_y):
    lp = xbc.shape[0]
    nc = lp // CHUNK
    gw = SSD_W // N_GROUPS
    hpg = N_HEADS // N_GROUPS

    def body(x_ref, dt_ref, dtt_ref, a_ref, at_ref, d_ref, st_ref, dy_ref,
             dx_ref, ddta_ref, ddtb_ref, ga1_ref, ga2_ref, gd_ref, dstate, dxdt_scr, z_scr, yoff_scr, sds_scr):
        c = pl.program_id(0)

        @pl.when(c == 0)
        def _():
            dstate[...] = jnp.zeros_like(dstate)
            ga1_ref[...] = jnp.zeros_like(ga1_ref)
            ga2_ref[...] = jnp.zeros_like(ga2_ref)
            gd_ref[...] = jnp.zeros_like(gd_ref)

        dt = dt_ref[0]
        dt_t = dtt_ref[0]
        a = a_ref[...]
        a_t = at_ref[...]
        tril, _, acum, acum_t, ax, dtx = _ssd_pieces(dt, dt_t, a, a_t)
        seg = _seg_matrix()
        x = x_ref[:, 0:SSD_W]
        dy = dy_ref[...]
        xdt = x * dtx
        ea = jnp.exp(ax)
        aex = ax[CHUNK - 1:CHUNK, :]
        wd = jnp.exp(aex - ax)
        eae = jnp.exp(aex)
        xw = xdt * wd
        edy = ea * dy
        lane16 = _iota((CHUNK, N_HEADS), 1)
        row16 = _iota((N_HEADS, CHUNK), 0)
        da_col = jnp.zeros((CHUNK, N_HEADS), F32)
        da_row = jnp.zeros((N_HEADS, CHUNK), F32)
        for g in range(N_GROUPS):
            gs = slice(g * gw, (g + 1) * gw)
            rs = slice(g * N_STATE, (g + 1) * N_STATE)
            bcol = slice(SSD_W + g * N_STATE, SSD_W + (g + 1) * N_STATE)
            ccol = slice(SSD_W + N_GROUPS * N_STATE + g * N_STATE, SSD_W + N_GROUPS * N_STATE + (g + 1) * N_STATE)
            bg = x_ref[:, bcol]
            cg = x_ref[:, ccol]
            sg = st_ref[0, rs, :]
            dsn = dstate[rs, :]
            cb = _mm_nt(cg, bg)
            z_scr[:, gs] = _mm(bg, dsn)
            yoff_scr[:, gs] = _mm(cg, sg) * ea[:, gs]
            sds_scr[:, gs] = jnp.broadcast_to(jnp.sum(dsn * sg, axis=0, keepdims=True), (8, gw))
            dcb = jnp.zeros((CHUNK, CHUNK), F32)
            for r in range(hpg):
                h = g * hpg + r
                hs = slice(h * HEAD, (h + 1) * HEAD)
                dec = _head_decay(ax, acum_t, h, tril)
                m = cb * dec
                t1 = _mm_nt(dy[:, hs], xdt[:, hs])
                dcb = dcb + dec * t1
                tm = m * t1
                da_col = da_col + jnp.where(lane16 == h, jnp.sum(tm, axis=1, keepdims=True), 0.0)
                da_row = da_row - jnp.where(row16 == h, jnp.sum(tm, axis=0, keepdims=True), 0.0)
                dxdt_scr[:, hs] = _mm_tn(m, dy[:, hs])
            dx_ref[:, ccol] = _mm(dcb, bg) + _mm_nt(edy[:, gs], sg)
            dx_ref[:, bcol] = _mm_tn(dcb, cg) + _mm_nt(xw[:, gs], dsn)
            dstate[rs, :] = eae[:, gs] * dsn + _mm_tn(cg, edy[:, gs])
        zf = z_scr[...]
        dxdt = dxdt_scr[...] + wd * zf
        t3 = _sel_right(xw * zf, seg)
        da_col = da_col + _sel_right(dy * yoff_scr[...], seg) - t3
        aend = acum[CHUNK - 1:CHUNK, :]
        sd = _sel_right(sds_scr[...], seg)[0:1, :] * jnp.exp(aend)
        last = jnp.sum(t3, axis=0, keepdims=True) + sd
        da_col = da_col + jnp.where(_iota((CHUNK, N_HEADS), 0) == CHUNK - 1, last, 0.0)
        r64 = _iota((CHUNK, CHUNK), 0)
        c64 = _iota((CHUNK, CHUNK), 1)
        ddta1 = _sel_left((c64 >= r64).astype(_MXU), da_col)
        ddta2 = _sel_right(da_row, (r64 >= c64).astype(_MXU))
        ddta_ref[0] = a * ddta1 + _sel_right(dxdt * x, seg)
        ddtb_ref[0] = a_t * ddta2
        ga1_ref[...] += jnp.sum(dt * ddta1, axis=0, keepdims=True)
        ga2_ref[...] += jnp.sum(dt_t * ddta2, axis=1, keepdims=True)
        dx_ref[:, 0:SSD_W] = dxdt * dtx + d_ref[...] * dy
        gd_ref[...] += jnp.sum(dy * x, axis=0, keepdims=True)

    rev = lambda c: (nc - 1 - c, 0)
    rev3 = lambda c: (nc - 1 - c, 0, 0)
    return pl.pallas_call(
        body, name="ssd_bwd", grid=(nc,),
        in_specs=[pl.BlockSpec((CHUNK, XBC_W), rev),
                  pl.BlockSpec((1, CHUNK, N_HEADS), rev3),
                  pl.BlockSpec((1, N_HEADS, CHUNK), rev3),
                  pl.BlockSpec((1, N_HEADS), lambda c: (0, 0)),
                  pl.BlockSpec((N_HEADS, 1), lambda c: (0, 0)),
                  pl.BlockSpec((1, SSD_W), lambda c: (0, 0)),
                  pl.BlockSpec((1, N_GROUPS * N_STATE, gw), rev3),
                  pl.BlockSpec((CHUNK, SSD_W), rev)],
        out_specs=[pl.BlockSpec((CHUNK, XBC_W), rev),
                   pl.BlockSpec((1, CHUNK, N_HEADS), rev3),
                   pl.BlockSpec((1, N_HEADS, CHUNK), rev3),
                   pl.BlockSpec((1, N_HEADS), lambda c: (0, 0)),
                   pl.BlockSpec((N_HEADS, 1), lambda c: (0, 0)),
                   pl.BlockSpec((1, SSD_W), lambda c: (0, 0))],
        out_shape=[_SDS((lp, XBC_W), F32), _SDS((nc, CHUNK, N_HEADS), F32), _SDS((nc, N_HEADS, CHUNK), F32),
                   _SDS((1, N_HEADS), F32), _SDS((N_HEADS, 1), F32), _SDS((1, SSD_W), F32)],
        scratch_shapes=[pltpu.VMEM((N_GROUPS * N_STATE, gw), F32), pltpu.VMEM((CHUNK, SSD_W), F32),
                        pltpu.VMEM((CHUNK, SSD_W), F32), pltpu.VMEM((CHUNK, SSD_W), F32),
                        pltpu.VMEM((8, SSD_W), F32)],
        compiler_params=_params(("arbitrary",)),
    )(xbc, dt_c, dt_tc, a, a_t, dskip_x, states, d_y)


def _gated_norm(o, gate, w):
    sg = _sigmoid(gate)
    p = o * (gate * sg)
    rs = lax.rsqrt(jnp.mean(p * p, axis=-1, keepdims=True) + EPS)
    n = p * rs
    return sg, rs, n, n * w


def _tail_fwd(o_sb, o_ssd, proj, h0, target, w_out, sb_w, ssd_w, fin_w):
    lp = o_sb.shape[0]
    nb = lp // TM
    row = lambda i: (i, 0)
    one = lambda i: (0, 0)

    def body(osb_ref, gate_ref, ossd_ref, z_ref, h0_ref, tgt_ref, wo_ref, sbw_ref, ssdw_ref, fw_ref,
             dh1_ref, loss_ref, gfw_ref):
        i = pl.program_id(0)

        @pl.when(i == 0)
        def _():
            loss_ref[...] = jnp.zeros_like(loss_ref)
            gfw_ref[...] = jnp.zeros_like(gfw_ref)

        y1 = _gated_norm(osb_ref[...], gate_ref[...], sbw_ref[...])[3]
        y2 = _gated_norm(ossd_ref[...], z_ref[...], ssdw_ref[...])[3]
        h1 = (h0_ref[...] + _mm(y1, wo_ref[0:SB_W, :])) + _mm(y2, wo_ref[SB_W:SB_W + SSD_W, :])
        rs1 = lax.rsqrt(jnp.mean(h1 * h1, axis=-1, keepdims=True) + EPS)
        n1 = h1 * rs1
        fw = fw_ref[...]
        diff = jnp.where(i > 0, n1 * fw - tgt_ref[...], 0.0)
        loss_ref[...] += jnp.sum(diff * diff, axis=0, keepdims=True)
        d_out = diff * (1.0 / D_MODEL)
        gfw_ref[...] += jnp.sum(d_out * n1, axis=0, keepdims=True)
        g = d_out * fw
        dh1_ref[...] = rs1 * (g - n1 * jnp.mean(g * n1, axis=-1, keepdims=True))

    return pl.pallas_call(
        body, name="tail_fwd", grid=(nb,),
        in_specs=[pl.BlockSpec((TM, SB_W), row),
                  pl.BlockSpec((TM, SB_W), lambda i: (i, COL_GATE // SB_W)),
                  pl.BlockSpec((TM, SSD_W), row),
                  pl.BlockSpec((TM, SSD_W), lambda i: (i, COL_Z // SSD_W)),
                  pl.BlockSpec((TM, D_MODEL), row),
                  pl.BlockSpec((TM, D_MODEL), lambda i: (jnp.maximum(i - 1, 0), 0)),
                  pl.BlockSpec(memory_space=_VMEM),
                  pl.BlockSpec((1, SB_W), one), pl.BlockSpec((1, SSD_W), one), pl.BlockSpec((1, D_MODEL), one)],
        out_specs=[pl.BlockSpec((TM, D_MODEL), row), pl.BlockSpec((1, D_MODEL), one), pl.BlockSpec((1, D_MODEL), one)],
        out_shape=[_SDS((lp, D_MODEL), F32), _SDS((1, D_MODEL), F32), _SDS((1, D_MODEL), F32)],
        compiler_params=_params(("arbitrary",), 40),
    )(o_sb, proj, o_ssd, proj, h0, target, w_out, sb_w, ssd_w, fin_w)


def _gated_norm_bwd(o, gate, w, dy):
    sg, rs, n, _ = _gated_norm(o, gate, w)
    gw = jnp.sum(dy * n, axis=0, keepdims=True)
    dn = dy * w
    dp = rs * (dn - n * jnp.mean(dn * n, axis=-1, keepdims=True))
    d_o = dp * (gate * sg)
    d_gate = dp * o * (sg * (1.0 + gate * (1.0 - sg)))
    return d_o, d_gate, gw, n * w


def _tail_bwd(o_sb, o_ssd, proj, d_h1, w_out, sb_w, ssd_w):
    lp = o_sb.shape[0]
    nb = lp // TM
    row = lambda i, t: (i, 0)
    one = lambda i, t: (0, 0)

    def body(osb_ref, gate_ref, ossd_ref, z_ref, dh1_ref, wo_ref, sbw_ref, ssdw_ref,
             dosb_ref, dossd_ref, dproj_ref, gwo_ref, gsb_ref, gssd_ref):
        i = pl.program_id(0)
        t = pl.program_id(1)

        @pl.when(jnp.logical_and(i == 0, t == 0))
        def _():
            gwo_ref[...] = jnp.zeros_like(gwo_ref)
            gsb_ref[...] = jnp.zeros_like(gsb_ref)
            gssd_ref[...] = jnp.zeros_like(gssd_ref)

        dh1 = dh1_ref[...].astype(_MXU)

        def half(o_ref, g_ref, w_ref, do_ref, gn_ref, r0):
            dy = lax.dot_general(dh1, wo_ref[r0:r0 + SB_W, :], _NT, preferred_element_type=F32)
            d_o, d_g, gw, y = _gated_norm_bwd(o_ref[...], g_ref[...], w_ref[...], dy)
            do_ref[...] = d_o
            dproj_ref[...] = d_g.astype(_MXU)
            gn_ref[...] += gw
            gwo_ref[r0:r0 + SB_W, :] += lax.dot_general(y.astype(_MXU), dh1, _TN, preferred_element_type=F32)

        @pl.when(t == 0)
        def _():
            half(osb_ref, gate_ref, sbw_ref, dosb_ref, gsb_ref, 0)

        @pl.when(t == 1)
        def _():
            half(ossd_ref, z_ref, ssdw_ref, dossd_ref, gssd_ref, SB_W)

    tile = pl.BlockSpec((TM, SB_W), row)
    return pl.pallas_call(
        body, name="tail_bwd", grid=(nb, 2),
        in_specs=[tile, pl.BlockSpec((TM, SB_W), lambda i, t: (i, COL_GATE // SB_W)),
                  tile, pl.BlockSpec((TM, SSD_W), lambda i, t: (i, COL_Z // SSD_W)),
                  tile, pl.BlockSpec(memory_space=_VMEM),
                  pl.BlockSpec((1, SB_W), one), pl.BlockSpec((1, SSD_W), one)],
        out_specs=[tile, tile, pl.BlockSpec((TM, SB_W), lambda i, t: (i, COL_GATE // SB_W + t)),
                   pl.BlockSpec((SB_W + SSD_W, D_MODEL), one), pl.BlockSpec((1, SB_W), one), pl.BlockSpec((1, SSD_W), one)],
        out_shape=[_SDS((lp, SB_W), F32), _SDS((lp, SSD_W), F32), _SDS((lp, W_ALL), _MXU),
                   _SDS((SB_W + SSD_W, D_MODEL), F32), _SDS((1, SB_W), F32), _SDS((1, SSD_W), F32)],
        compiler_params=_params(("arbitrary", "arbitrary"), 48),
    )(o_sb, proj, o_ssd, proj, d_h1, w_out, sb_w, ssd_w)


def _in_bwd(d_proj, w_all, h0, d_h1, norm_w):
    lp = h0.shape[0]
    nb = lp // TM
    seq = lp - OFF
    row = lambda i: (i, 0)
    one = lambda i: (0, 0)

    def body(dp_ref, w_ref, h0_ref, dh1_ref, nw_ref, gx_ref, gmeta_ref, gnw_ref):
        i = pl.program_id(0)

        @pl.when(i == 0)
        def _():
            gnw_ref[...] = jnp.zeros_like(gnw_ref)

        du = lax.dot_general(dp_ref[...], w_ref[...], _NT, preferred_element_type=F32)
        h = h0_ref[...]
        rs = lax.rsqrt(jnp.mean(h * h, axis=-1, keepdims=True) + EPS)
        n0 = h * rs
        gnw_ref[...] += jnp.sum(du * n0, axis=0, keepdims=True)
        g = du * nw_ref[...]
        dh0 = dh1_ref[...] + rs * (g - n0 * jnp.mean(g * n0, axis=-1, keepdims=True))

        @pl.when(i == 0)
        def _():
            gmeta_ref[...] = dh0[PAD:PAD + N_META, :]

        @pl.when(i > 0)
        def _():
            gx_ref[...] = dh0

    tile = pl.BlockSpec((TM, D_MODEL), row)
    return pl.pallas_call(
        body, name="in_bwd", grid=(nb,),
        in_specs=[pl.BlockSpec((TM, W_ALL), row), pl.BlockSpec(memory_space=_VMEM),
                  tile, tile, pl.BlockSpec((1, D_MODEL), one)],
        out_specs=[pl.BlockSpec((TM, D_MODEL), lambda i: (jnp.maximum(i - 1, 0), 0)),
                   pl.BlockSpec((N_META, D_MODEL), one), pl.BlockSpec((1, D_MODEL), one)],
        out_shape=[_SDS((seq, D_MODEL), F32), _SDS((N_META, D_MODEL), F32), _SDS((1, D_MODEL), F32)],
        compiler_params=_params(("arbitrary",), 48),
    )(d_proj, w_all, h0, d_h1, norm_w)


def _grad_w_windows(u_t, d_proj):
    lp = d_proj.shape[0]
    hw = WIN_W // 2
    steps = 2 * N_CHIPS

    def body(ut_ref, dp_hbm, o_ref, buf, sems):
        s = pl.program_id(0)
        slot = s % 2

        def fetch(step, sl):
            start = pl.multiple_of((step // 2) * WIN_STEP + (step % 2) * hw, 128)
            return pltpu.make_async_copy(dp_hbm.at[:, pl.ds(start, hw)], buf.at[sl], sems.at[sl])

        @pl.when(s == 0)
        def _():
            fetch(0, 0).start()

        @pl.when(s + 1 < steps)
        def _():
            fetch(s + 1, 1 - slot).start()

        fetch(s, slot).wait()
        o_ref[0] = jnp.dot(ut_ref[...], buf[slot], preferred_element_type=F32).astype(o_ref.dtype)

    return pl.pallas_call(
        body, name="grad_w_in", grid=(steps,),
        in_specs=[pl.BlockSpec((D_MODEL, lp), lambda s: (0, 0)), pl.BlockSpec(memory_space=pl.ANY)],
        out_specs=pl.BlockSpec((1, D_MODEL, hw), lambda s: (s // 2, 0, s % 2)),
        out_shape=_SDS((N_CHIPS, D_MODEL, WIN_W), _MXU),
        scratch_shapes=[pltpu.VMEM((2, lp, hw), _MXU), pltpu.SemaphoreType.DMA((2,))],
        compiler_params=_params(("arbitrary",), 40),
    )(u_t, d_proj)


def _device_grads(x2d, target2d, meta_full, norm_w, w_all, conv_w, conv_b, dt_bias, a_log, d_skip,
                  sb_w, ssd_w, w_out, fin_w):
    lp = x2d.shape[0] + OFF
    nc = lp // CHUNK
    h0, u, u_t = _prep(x2d, meta_full, norm_w)
    proj, dt_raw = _inproj(u, w_all)
    o_sb, o_lo = _sb_fwd(proj)
    dt_bias128 = jnp.pad(dt_bias, ((0, 0), (0, 128 - N_HEADS)))
    xbc, dt128 = _conv_fwd(proj, dt_raw, conv_w, conv_b, dt_bias128)
    dt_c = dt128[:, :N_HEADS].reshape(nc, CHUNK, N_HEADS)
    dt_tc = jnp.swapaxes(dt_c, 1, 2)
    a = -jnp.exp(a_log)
    a_t = a.reshape(N_HEADS, 1)
    dskip_x = jnp.repeat(d_skip, HEAD, axis=1)
    o_ssd, states = _ssd_fwd(xbc, dt_c, dt_tc, a, a_t, dskip_x)
    d_h1, sq_err, g_fin = _tail_fwd(o_sb, o_ssd, proj, h0, target2d, w_out, sb_w, ssd_w, fin_w)

    d_osb, d_ossd, d_proj, g_wout, g_sb, g_ssd = _tail_bwd(o_sb, o_ssd, proj, d_h1, w_out, sb_w, ssd_w)
    d_proj = _sb_bwd(proj, o_sb, o_lo, d_osb, d_proj)
    d_xbc_act, ddt_a, ddt_b, ga1, ga2, gd = _ssd_bwd(xbc, dt_c, dt_tc, a, a_t, dskip_x, states, d_ossd)
    d_dt = (ddt_a + jnp.swapaxes(ddt_b, 1, 2)).reshape(lp, N_HEADS)
    d_dt128 = jnp.pad(d_dt, ((0, 0), (0, 128 - N_HEADS)))
    d_proj, g_convw, g_convb, g_dtb128 = _conv_bwd(proj, dt_raw, conv_w, conv_b, dt_bias128, d_xbc_act, d_dt128, d_proj)
    g_win = _grad_w_windows(u_t, d_proj)
    g_x, g_meta, g_nw = _in_bwd(d_proj, w_all, h0, d_h1, norm_w)
    g_alog = (ga1 + ga2.reshape(1, N_HEADS)) * a
    g_dskip = gd.reshape(N_HEADS, HEAD).sum(axis=1).reshape(1, N_HEADS)
    grads = dict(meta_tokens=g_meta, norm_w=g_nw, w_in=g_win, conv_w=g_convw, conv_b=g_convb,
                 dt_bias=g_dtb128[:, :N_HEADS], a_log=g_alog, d_skip=g_dskip, sb_norm_w=g_sb, ssd_norm_w=g_ssd,
                 w_out=g_wout, final_norm_w=g_fin)
    return sq_err, g_x, grads


_MESH = pl.DeviceIdType.MESH
_ANY = pl.BlockSpec(memory_space=pl.ANY)


def _place():
    return lax.axis_index("x"), lax.axis_index("y"), lax.axis_index("c")


def _other_chips(x, y):
    return ((1 - x, y), (x, 1 - y), (1 - x, 1 - y))


def _gather_shards(arrays, n_big):
    n = len(arrays)

    def body(*refs):
        srcs, dsts = refs[:n], refs[n:2 * n]
        send_sems, recv_sems, fwd_send, fwd_recv = refs[2 * n:]
        x, y, c = _place()
        mine = 2 * x + y
        chips = _other_chips(x, y)

        def window(a):
            half = arrays[a].shape[0] // 2
            return pl.ds(pl.multiple_of(c * half, 16), half)

        first = []
        for a in range(n):
            for k, (px, py) in enumerate(chips):
                if a < n_big:
                    src, dst = srcs[a].at[window(a)], dsts[a].at[mine, window(a)]
                else:
                    src, dst = srcs[a], dsts[a].at[mine]
                cp = pltpu.make_async_remote_copy(
                    src_ref=src, dst_ref=dst, send_sem=send_sems.at[a * 3 + k], recv_sem=recv_sems.at[a * 3 + k],
                    device_id=(px, py, c), device_id_type=_MESH)
                cp.start()
                first.append(cp)
        passed = []
        for a in range(n):
            for k, (px, py) in enumerate(chips):
                first[a * 3 + k].wait_recv()
                if a < n_big:
                    landed = dsts[a].at[2 * px + py, window(a)]
                    cp = pltpu.make_async_remote_copy(
                        src_ref=landed, dst_ref=landed, send_sem=fwd_send.at[a * 3 + k], recv_sem=fwd_recv.at[a * 3 + k],
                        device_id=(x, y, 1 - c), device_id_type=_MESH)
                    cp.start()
                    passed.append(cp)
        for cp in passed:
            cp.wait_recv()
        for cp in first + passed:
            cp.wait_send()

    got = pl.pallas_call(
        body, name="gather_shards",
        in_specs=[_ANY] * n, out_specs=[_ANY] * n,
        out_shape=[_SDS((N_CHIPS,) + a.shape, a.dtype) for a in arrays],
        scratch_shapes=[pltpu.SemaphoreType.DMA((3 * n,)), pltpu.SemaphoreType.DMA((3 * n,)),
                        pltpu.SemaphoreType.DMA((3 * n_big,)), pltpu.SemaphoreType.DMA((3 * n_big,))],
    )(*arrays)
    mine = 2 * lax.axis_index("x") + lax.axis_index("y")
    return [lax.dynamic_update_slice(g, a[None], (mine,) + (0,) * a.ndim) for g, a in zip(got, arrays)]


def _scatter_slabs(arrays):
    n = len(arrays)

    def body(*refs):
        srcs, dsts = refs[:n], refs[n:2 * n]
        send_sems, recv_sems = refs[2 * n:]
        x, y, c = _place()
        mine = 2 * x + y
        remote = []
        for a in range(n):
            for k, (px, py) in enumerate(_other_chips(x, y)):
                cp = pltpu.make_async_remote_copy(
                    src_ref=srcs[a].at[2 * px + py], dst_ref=dsts[a].at[mine],
                    send_sem=send_sems.at[a * 3 + k], recv_sem=recv_sems.at[a * 3 + k],
                    device_id=(px, py, c), device_id_type=_MESH)
                cp.start()
                remote.append(cp)
        for cp in remote:
            cp.wait_recv()
        for cp in remote:
            cp.wait_send()

    return pl.pallas_call(
        body, name="scatter_slabs",
        in_specs=[_ANY] * n, out_specs=[_ANY] * n,
        out_shape=[_SDS(a.shape, a.dtype) for a in arrays],
        scratch_shapes=[pltpu.SemaphoreType.DMA((3 * n,)), pltpu.SemaphoreType.DMA((3 * n,))],
    )(*arrays)


def _swap_halves(arrays):
    n = len(arrays)

    def body(*refs):
        srcs, dsts = refs[:n], refs[n:2 * n]
        send_sems, recv_sems = refs[2 * n:]
        x, y, c = _place()
        copies = []
        for a in range(n):
            half = arrays[a].shape[1] // 2
            cp = pltpu.make_async_remote_copy(
                src_ref=srcs[a].at[:, pl.ds(pl.multiple_of((1 - c) * half, 16), half)], dst_ref=dsts[a],
                send_sem=send_sems.at[a], recv_sem=recv_sems.at[a],
                device_id=(x, y, 1 - c), device_id_type=_MESH)
            cp.start()
            copies.append(cp)
        for cp in copies:
            cp.wait_recv()
        for cp in copies:
            cp.wait_send()

    return pl.pallas_call(
        body, name="swap_halves",
        in_specs=[_ANY] * n, out_specs=[_ANY] * n,
        out_shape=[_SDS((a.shape[0], a.shape[1] // 2, a.shape[2]), a.dtype) for a in arrays],
        scratch_shapes=[pltpu.SemaphoreType.DMA((n,)), pltpu.SemaphoreType.DMA((n,))],
    )(*arrays)


def _join_halves(arrays):
    n = len(arrays)

    def body(*refs):
        dsts = refs[n:2 * n]
        send_sems, recv_sems = refs[2 * n:]
        x, y, c = _place()
        copies = []
        for a in range(n):
            half = arrays[a].shape[0] // 2
            rows = pl.ds(pl.multiple_of(c * half, 16), half)
            cp = pltpu.make_async_remote_copy(
                src_ref=dsts[a].at[rows], dst_ref=dsts[a].at[rows], send_sem=send_sems.at[a], recv_sem=recv_sems.at[a],
                device_id=(x, y, 1 - c), device_id_type=_MESH)
            cp.start()
            copies.append(cp)
        for cp in copies:
            cp.wait_recv()
        for cp in copies:
            cp.wait_send()

    return pl.pallas_call(
        body, name="join_halves",
        in_specs=[_ANY] * n, out_specs=[_ANY] * n,
        out_shape=[_SDS(a.shape, a.dtype) for a in arrays],
        input_output_aliases={a: a for a in range(n)},
        scratch_shapes=[pltpu.SemaphoreType.DMA((n,)), pltpu.SemaphoreType.DMA((n,))],
    )(*arrays)


N_DEV = 8
SMALL_ROWS = 32
SMALL_COLS = XBC_W


def _gather_small(packed):
    def body(src_ref, dst_ref, send_sems, recv_sems, local_sem):
        x, y, c = _place()
        me = 4 * x + 2 * y + c
        own = pltpu.make_async_copy(src_ref, dst_ref.at[me], local_sem)
        own.start()
        copies = []
        for k in range(1, N_DEV):
            bx, by, bc = (k >> 2) & 1, (k >> 1) & 1, k & 1
            peer = (x + bx - 2 * x * bx, y + by - 2 * y * by, c + bc - 2 * c * bc)
            cp = pltpu.make_async_remote_copy(
                src_ref=src_ref, dst_ref=dst_ref.at[me], send_sem=send_sems.at[k - 1], recv_sem=recv_sems.at[k - 1],
                device_id=peer, device_id_type=_MESH)
            cp.start()
            copies.append(cp)
        for cp in copies:
            cp.wait_recv()
        for cp in copies:
            cp.wait_send()
        own.wait()

    return pl.pallas_call(
        body, name="gather_small",
        in_specs=[pl.BlockSpec(memory_space=_VMEM)], out_specs=pl.BlockSpec(memory_space=_VMEM),
        out_shape=_SDS((N_DEV, SMALL_ROWS, SMALL_COLS), F32),
        scratch_shapes=[pltpu.SemaphoreType.DMA((N_DEV - 1,)), pltpu.SemaphoreType.DMA((N_DEV - 1,)),
                        pltpu.SemaphoreType.DMA],
    )(packed)


def _adamw(w, g, m, v):
    m = ADAM_B1 * m + (1.0 - ADAM_B1) * g
    v = ADAM_B2 * v + (1.0 - ADAM_B2) * (g * g)
    m_hat = m / (1.0 - ADAM_B1 ** ADAM_STEP)
    v_hat = v / (1.0 - ADAM_B2 ** ADAM_STEP)
    delta = -ADAM_LR * (m_hat / (jnp.sqrt(v_hat) + ADAM_EPS) + ADAM_WD * w)
    return delta, m, v


def _sum_slabs(slabs, core, name):
    _, h, c = slabs.shape
    tr = 128
    nblk = h // tr

    def body(core_ref, s_ref, o_ref):
        o_ref[...] = ((s_ref[0].astype(F32) + s_ref[1].astype(F32)) + s_ref[2].astype(F32)) + s_ref[3].astype(F32)

    grid_spec = pltpu.PrefetchScalarGridSpec(
        num_scalar_prefetch=1, grid=(nblk,),
        in_specs=[pl.BlockSpec((N_CHIPS, tr, c), lambda i, core_ref: (0, i, 0))],
        out_specs=pl.BlockSpec((tr, c), lambda i, core_ref: (core_ref[0] * nblk + i, 0)))
    return pl.pallas_call(
        body, name=name, grid_spec=grid_spec, out_shape=_SDS((2 * h, c), F32),
        compiler_params=_params(("arbitrary",)),
    )(core, slabs)


def _add_halves(own, recv, core, name):
    _, r, c = own.shape
    half = r // 2
    tr = 128
    nblk = half // tr

    def body(core_ref, a_ref, b_ref, o_ref):
        o_ref[...] = (a_ref[...].astype(F32) + b_ref[...].astype(F32)).astype(o_ref.dtype)

    grid_spec = pltpu.PrefetchScalarGridSpec(
        num_scalar_prefetch=1, grid=(nblk,),
        in_specs=[pl.BlockSpec((N_CHIPS, tr, c), lambda i, core_ref: (0, core_ref[0] * nblk + i, 0)),
                  pl.BlockSpec((N_CHIPS, tr, c), lambda i, core_ref: (0, i, 0))],
        out_specs=pl.BlockSpec((N_CHIPS, tr, c), lambda i, core_ref: (0, i, 0)))
    return pl.pallas_call(
        body, name=name, grid_spec=grid_spec, out_shape=_SDS((N_CHIPS, half, c), own.dtype),
        compiler_params=_params(("arbitrary",)),
    )(core, own, recv)


def _update_big(w, m, v, g, name):
    r, c = w.shape
    tr = 128

    def body(w_ref, m_ref, v_ref, g_ref, d_ref, mo_ref, vo_ref):
        delta, m_new, v_new = _adamw(w_ref[...], g_ref[...], m_ref[...], v_ref[...])
        d_ref[...] = delta
        mo_ref[...] = m_new
        vo_ref[...] = v_new

    spec = pl.BlockSpec((tr, c), lambda i: (i, 0))
    return pl.pallas_call(
        body, name=name, grid=(r // tr,),
        in_specs=[spec] * 4, out_specs=[spec] * 3,
        out_shape=[_SDS((r, c), F32)] * 3,
        compiler_params=_params(("arbitrary",)),
    )(w, m, v, g)


_ROW = dict(norm_w=0, sb_norm_w=1, ssd_norm_w=2, final_norm_w=3, conv_b=4, dt_bias=5, a_log=6, d_skip=7,
            conv_w=8, sq_err=12, meta_tokens=16)
_SMALL = ("meta_tokens", "norm_w", "conv_w", "conv_b", "dt_bias", "a_log", "d_skip", "sb_norm_w", "ssd_norm_w",
          "final_norm_w")


def _pack_small(sq_err, grads):
    def rowpad(a):
        return jnp.pad(a, ((0, 0), (0, SMALL_COLS - a.shape[1])))

    rows = [rowpad(grads[k]) for k in ("norm_w", "sb_norm_w", "ssd_norm_w", "final_norm_w", "conv_b", "dt_bias", "a_log", "d_skip")]
    rows.append(grads["conv_w"])
    rows.append(rowpad(sq_err))
    rows.append(jnp.zeros((3, SMALL_COLS), F32))
    rows.append(rowpad(grads["meta_tokens"]))
    return jnp.concatenate(rows, axis=0)


def _update_small(gathered, ws, ms, vs):
    names = _SMALL
    n = len(names)

    def body(*refs):
        g_ref = refs[0]
        w_refs, m_refs, v_refs = refs[1:1 + n], refs[1 + n:1 + 2 * n], refs[1 + 2 * n:1 + 3 * n]
        outs = refs[1 + 3 * n:]
        loss_ref = outs[0]
        go, do, mo, vo = outs[1:1 + n], outs[1 + n:1 + 2 * n], outs[1 + 2 * n:1 + 3 * n], outs[1 + 3 * n:1 + 4 * n]
        tot = g_ref[0]
        for d in range(1, N_DEV):
            tot = tot + g_ref[d]
        x, y, _ = _place()
        chip = 2 * x + y
        loss_ref[...] = jnp.broadcast_to(
            0.5 * jnp.sum(tot[_ROW["sq_err"]:_ROW["sq_err"] + 1, 0:D_MODEL], axis=1, keepdims=True) / D_MODEL, (1, 128))
        for idx, nm in enumerate(names):
            r0 = _ROW[nm]
            rows, cols = w_refs[idx].shape
            if nm in ("conv_w", "meta_tokens"):
                g = jnp.zeros((rows, cols), F32)
                for j in range(N_CHIPS):
                    g = g + jnp.where(chip == j, tot[r0:r0 + rows, j * cols:(j + 1) * cols], 0.0)
            else:
                g = tot[r0:r0 + rows, 0:cols]
            delta, m_new, v_new = _adamw(w_refs[idx][...], g, m_refs[idx][...], v_refs[idx][...])
            go[idx][...] = g
            do[idx][...] = delta
            mo[idx][...] = m_new
            vo[idx][...] = v_new

    shapes = [_SDS(ws[nm].shape, F32) for nm in names]
    vm = pl.BlockSpec(memory_space=_VMEM)
    res = pl.pallas_call(
        body, name="update_small",
        in_specs=[vm] * (1 + 3 * n), out_specs=[vm] * (1 + 4 * n),
        out_shape=[_SDS((1, 128), F32)] + shapes * 4,
    )(gathered, *[ws[nm] for nm in names], *[ms[nm] for nm in names], *[vs[nm] for nm in names])
    loss = res[0][0, 0]
    g = dict(zip(names, res[1:1 + n]))
    d = dict(zip(names, res[1 + n:1 + 2 * n]))
    m = dict(zip(names, res[1 + 2 * n:1 + 3 * n]))
    v = dict(zip(names, res[1 + 3 * n:1 + 4 * n]))
    return loss, g, d, m, v


_WEIGHTS = ("meta_tokens", "norm_w", "w_in", "conv_w", "conv_b", "dt_bias", "a_log", "d_skip", "sb_norm_w",
            "ssd_norm_w", "w_out", "final_norm_w")


def kernel(x, meta_tokens, norm_w, w_in, conv_w, conv_b, dt_bias, a_log, d_skip, sb_norm_w, ssd_norm_w, w_out, final_norm_w, loss_target, m_meta_tokens, m_norm_w, m_w_in, m_conv_w, m_conv_b, m_dt_bias, m_a_log, m_d_skip, m_sb_norm_w, m_ssd_norm_w, m_w_out, m_final_norm_w, v_meta_tokens, v_norm_w, v_w_in, v_conv_w, v_conv_b, v_dt_bias, v_a_log, v_d_skip, v_sb_norm_w, v_ssd_norm_w, v_w_out, v_final_norm_w):
    given = dict(meta_tokens=meta_tokens, norm_w=norm_w, w_in=w_in, conv_w=conv_w, conv_b=conv_b, dt_bias=dt_bias,
                 a_log=a_log, d_skip=d_skip, sb_norm_w=sb_norm_w, ssd_norm_w=ssd_norm_w, w_out=w_out,
                 final_norm_w=final_norm_w)
    mom = dict(meta_tokens=m_meta_tokens, norm_w=m_norm_w, w_in=m_w_in, conv_w=m_conv_w, conv_b=m_conv_b,
               dt_bias=m_dt_bias, a_log=m_a_log, d_skip=m_d_skip, sb_norm_w=m_sb_norm_w, ssd_norm_w=m_ssd_norm_w,
               w_out=m_w_out, final_norm_w=m_final_norm_w)
    var = dict(meta_tokens=v_meta_tokens, norm_w=v_norm_w, w_in=v_w_in, conv_w=v_conv_w, conv_b=v_conv_b,
               dt_bias=v_dt_bias, a_log=v_a_log, d_skip=v_d_skip, sb_norm_w=v_sb_norm_w, ssd_norm_w=v_ssd_norm_w,
               w_out=v_w_out, final_norm_w=v_final_norm_w)
    seq = x.shape[1]

    def two_d(a):
        return a.reshape((-1, a.shape[-1])) if a.ndim != 2 else a

    g_win, g_wout, g_meta, g_cw = _gather_shards(
        [w_in[0].astype(_MXU), w_out[0].astype(_MXU), meta_tokens, conv_w[0]], 2)
    w_all = jnp.pad(jnp.swapaxes(g_win, 0, 1).reshape(D_MODEL, D_IN), ((0, 0), (0, W_ALL - D_IN)))
    w_out_full = g_wout.reshape(2 * D_MODEL, D_MODEL)
    meta_full = jnp.swapaxes(g_meta, 0, 1).reshape(N_META, D_MODEL)
    conv_w_full = jnp.swapaxes(g_cw, 0, 1).reshape(4, XBC_W)

    sq_err, g_x, grads = _device_grads(
        x.reshape(seq, D_MODEL), loss_target.reshape(seq, D_MODEL), meta_full, norm_w, w_all, conv_w_full,
        conv_b, dt_bias, a_log, d_skip, sb_norm_w, ssd_norm_w, w_out_full, final_norm_w.reshape(1, D_MODEL))

    core = lax.axis_index("c").astype(jnp.int32).reshape(1)
    slab_in = grads["w_in"]
    slab_out = grads["w_out"].reshape(N_CHIPS, W_OUT_SHARD, D_MODEL).astype(_MXU)
    sib_in, sib_out = _swap_halves([slab_in, slab_out])
    chip_in = _add_halves(slab_in, sib_in, core, "chip_sum_w_in")
    chip_out = _add_halves(slab_out, sib_out, core, "chip_sum_w_out")
    got_in, got_out = _scatter_slabs([chip_in, chip_out])
    chip = 2 * lax.axis_index("x") + lax.axis_index("y")

    def with_own(got, sent):
        own = lax.dynamic_slice(sent, (chip, 0, 0), (1,) + sent.shape[1:])
        return lax.dynamic_update_slice(got, own, (chip, 0, 0))

    g_in, g_out = _join_halves([_sum_slabs(with_own(got_in, chip_in), core, "sum_w_in"),
                                _sum_slabs(with_own(got_out, chip_out), core, "sum_w_out")])
    g_in = lax.dynamic_slice(g_in, (0, 4 * chip), (D_MODEL, W_IN_SHARD))
    big = dict(w_in=(g_in,) + tuple(_update_big(w_in[0], m_w_in[0], v_w_in[0], g_in, "update_w_in")),
               w_out=(g_out,) + tuple(_update_big(w_out[0], m_w_out[0], v_w_out[0], g_out, "update_w_out")))

    gathered = _gather_small(_pack_small(sq_err, grads))
    loss, sg, sd, sm, sv = _update_small(
        gathered, {k: two_d(given[k]) for k in _SMALL}, {k: two_d(mom[k]) for k in _SMALL},
        {k: two_d(var[k]) for k in _SMALL})

    out = {}
    for idx, group in enumerate((sg, sd, sm, sv)):
        for k in _SMALL:
            out[(idx, k)] = group[k].reshape(given[k].shape)
        for k in ("w_in", "w_out"):
            out[(idx, k)] = big[k][idx].reshape(given[k].shape)
    return (loss, g_x.reshape(x.shape), *[out[(idx, k)] for idx in range(4) for k in _WEIGHTS])
```

```python
import functools
import math

import jax
import jax.numpy as jnp
from jax import lax
from jax.experimental import pallas as pl
from jax.experimental.pallas import tpu as pltpu

F32 = jnp.float32
_MXU = jnp.bfloat16

D_MODEL = 1024
N_META = 16
PAD = 112
OFF = PAD + N_META
TM = 128
CHUNK = 64
SB_W = 1024
SSD_W = 1024
N_HEADS = 16
HEAD = 64
N_GROUPS = 2
N_STATE = 128
XBC_W = SSD_W + 2 * N_GROUPS * N_STATE
N_MAIN = 4 * SB_W + SSD_W + XBC_W
COL_GATE = 3 * SB_W
COL_Z = 4 * SB_W
COL_XBC = 5 * SB_W
D_IN = N_MAIN + N_HEADS
W_ALL = N_MAIN + 128
WIN_STEP = 1664
WIN_W = 1792
EPS = 1e-5
N_CHIPS = 4
W_IN_SHARD = D_IN // N_CHIPS
W_OUT_SHARD = 2 * D_MODEL // N_CHIPS

ADAM_LR = 0.001
ADAM_B1 = 0.9
ADAM_B2 = 0.999
ADAM_EPS = 1e-08
ADAM_WD = 0.01
ADAM_STEP = 10

_SDS = jax.ShapeDtypeStruct
_NT = (((1,), (1,)), ((), ()))
_TN = (((0,), (0,)), ((), ()))
_VMEM = pltpu.VMEM


def _params(sem=None, vmem_mb=None):
    kw = {}
    if sem is not None:
        kw["dimension_semantics"] = sem
    if vmem_mb is not None:
        kw["vmem_limit_bytes"] = vmem_mb * 1024 * 1024
    return pltpu.CompilerParams(**kw)


def _mm(a, b):
    return jnp.dot(a.astype(_MXU), b.astype(_MXU), preferred_element_type=F32)


def _mm_nt(a, b):
    return lax.dot_general(a.astype(_MXU), b.astype(_MXU), _NT, preferred_element_type=F32)


def _mm_tn(a, b):
    return lax.dot_general(a.astype(_MXU), b.astype(_MXU), _TN, preferred_element_type=F32)


def _split(x, parts):
    out = []
    r = x
    for _ in range(parts):
        p = r.astype(_MXU)
        out.append(p)
        r = r - p.astype(F32)
    return out


def _sel_right(x, m01, parts=3):
    acc = None
    for p in _split(x, parts):
        t = jnp.dot(p, m01, preferred_element_type=F32)
        acc = t if acc is None else acc + t
    return acc


def _sel_left(m01, x, parts=3):
    acc = None
    for p in _split(x, parts):
        t = jnp.dot(m01, p, preferred_element_type=F32)
        acc = t if acc is None else acc + t
    return acc


def _iota(shape, axis):
    return lax.broadcasted_iota(jnp.int32, shape, axis)


def _sigmoid(x):
    return 1.0 / (1.0 + jnp.exp(-x))


def _prep(x2d, meta_full, norm_w):
    seq = x2d.shape[0]
    lp = seq + OFF
    nb = lp // TM

    def body(x_ref, meta_ref, w_ref, h0_ref, u_ref, ut_ref):
        i = pl.program_id(0)

        @pl.when(i == 0)
        def _():
            h0_ref[...] = jnp.concatenate([jnp.zeros((PAD, D_MODEL), F32), meta_ref[...]], axis=0)

        @pl.when(i > 0)
        def _():
            h0_ref[...] = x_ref[...]

        h = h0_ref[...]
        rs = lax.rsqrt(jnp.mean(h * h, axis=-1, keepdims=True) + EPS)
        u = (h * rs * w_ref[...]).astype(_MXU)
        u_ref[...] = u
        ut_ref[...] = u.T

    return pl.pallas_call(
        body, name="prep", grid=(nb,),
        in_specs=[pl.BlockSpec((TM, D_MODEL), lambda i: (jnp.maximum(i - 1, 0), 0)),
                  pl.BlockSpec((N_META, D_MODEL), lambda i: (0, 0)),
                  pl.BlockSpec((1, D_MODEL), lambda i: (0, 0))],
        out_specs=[pl.BlockSpec((TM, D_MODEL), lambda i: (i, 0)),
                   pl.BlockSpec((TM, D_MODEL), lambda i: (i, 0)),
                   pl.BlockSpec((D_MODEL, TM), lambda i: (0, i))],
        out_shape=[_SDS((lp, D_MODEL), F32), _SDS((lp, D_MODEL), _MXU), _SDS((D_MODEL, lp), _MXU)],
        compiler_params=_params(("arbitrary",)),
    )(x2d, meta_full, norm_w)


def _inproj(u, w_t):
    lp = u.shape[0]
    tn = 512

    def body(u_ref, w_ref, wdt_ref, o_ref, odt_ref):
        o_ref[...] = lax.dot_general(u_ref[...], w_ref[...], _NT, preferred_element_type=F32)

        @pl.when(pl.program_id(0) == 0)
        def _():
            odt_ref[...] = lax.dot_general(u_ref[...], wdt_ref[...], _NT, preferred_element_type=F32)

    return pl.pallas_call(
        body, name="inproj", grid=(N_MAIN // tn,),
        in_specs=[pl.BlockSpec((lp, D_MODEL), lambda j: (0, 0)),
                  pl.BlockSpec((tn, D_MODEL), lambda j: (j, 0)),
                  pl.BlockSpec((128, D_MODEL), lambda j: (N_MAIN // 128, 0))],
        out_specs=[pl.BlockSpec((lp, tn), lambda j: (0, j)),
                   pl.BlockSpec((lp, 128), lambda j: (0, 0))],
        out_shape=[_SDS((lp, N_MAIN), F32), _SDS((lp, 128), F32)],
        compiler_params=_params(("arbitrary",), 48),
    )(u, w_t, w_t)


SB_WINDOW = 3
SB_DEAD = -104.0


def _sb_logs(qh, kwin):
    z = lax.dot_general(qh, kwin, _NT, preferred_element_type=F32)
    e = jnp.exp(-jnp.abs(z))
    l1p = jnp.log(1.0 + e)
    lk_full = -(jnp.maximum(z, 0.0) + l1p)
    ls = jnp.minimum(z, 0.0) - l1p
    return z, e, ls, lk_full


def _blk(a, b):
    return a[:, b * TM:(b + 1) * TM]


def _stacked_sel(blocks, m01):
    n = len(blocks)
    pieces = [_split(b, 2) for b in blocks]
    stacked = jnp.concatenate([p[0] for p in pieces] + [p[1] for p in pieces], axis=0)
    res = jnp.dot(stacked, m01, preferred_element_type=F32)
    return [res[j * TM:(j + 1) * TM] + res[(n + j) * TM:(n + j + 1) * TM] for j in range(n)]


def _sb_weights(ls, lk_full, run, last_mask, upper, n):
    lk = [_blk(lk_full, b) for b in range(n)]
    lk[n - 1] = jnp.where(last_mask, lk[n - 1], 0.0)
    aft = _stacked_sel(lk, upper)
    w = [None] * n
    for b in range(n - 1, -1, -1):
        wb = jnp.exp(_blk(ls, b) + aft[b] + run)
        w[b] = jnp.where(last_mask, wb, 0.0) if b == n - 1 else wb
        run = run + jnp.sum(lk[b], axis=1, keepdims=True)
    return w, run


def _sb_fwd(proj):
    lp = proj.shape[0]
    nb = lp // TM

    def body(q_ref, k_ref, v_ref, o_ref, olo_ref, acc, run_scr):
        i = pl.program_id(1)
        lane = _iota((TM, TM), 1)
        row = _iota((TM, TM), 0)
        head0 = lane < HEAD
        upper = (row > lane).astype(_MXU)
        strict = lane < row
        q = q_ref[...] * (1.0 / math.sqrt(HEAD))
        qh = (jnp.where(head0, q, 0.0).astype(_MXU), jnp.where(head0, 0.0, q).astype(_MXU))

        def key_set(first, n, last_mask):
            off = pl.multiple_of(first * TM, TM)
            kwin = k_ref[pl.ds(off, n * TM), :].astype(_MXU)
            vwin = v_ref[pl.ds(off, n * TM), :].astype(_MXU)
            alive = None
            for hh in range(2):
                run = run_scr[hh][:, 0:1]
                _, _, ls, lk_full = _sb_logs(qh[hh], kwin)
                w, run = _sb_weights(ls, lk_full, run, last_mask, upper, n)
                pieces = [_split(wb, 2) for wb in w]
                stacked = jnp.concatenate(
                    [jnp.concatenate([p[0] for p in pieces], axis=1), jnp.concatenate([p[1] for p in pieces], axis=1)], axis=0)
                res = jnp.dot(stacked, vwin, preferred_element_type=F32)
                acc[hh] += res[0:TM]
                acc[2 + hh] += res[TM:2 * TM]
                run_scr[hh] = jnp.broadcast_to(run, (TM, TM))
                top = jnp.max(run)
                alive = top if alive is None else jnp.maximum(alive, top)
            return (alive > SB_DEAD).astype(jnp.int32)

        acc[...] = jnp.zeros_like(acc)
        run_scr[...] = jnp.zeros_like(run_scr)

        @pl.when(i >= SB_WINDOW - 1)
        def _():
            key_set(i - (SB_WINDOW - 1), SB_WINDOW, strict)

        start = jnp.where(i >= SB_WINDOW - 1, i - SB_WINDOW, i)
        alive0 = (jnp.max(run_scr[...]) > SB_DEAD).astype(jnp.int32)

        def cond(c):
            return jnp.logical_and(c[0] >= 0, c[1] > 0)

        def step(c):
            kb = c[0]
            return kb - 1, key_set(kb, 1, jnp.logical_or(strict, kb < i))

        lax.while_loop(cond, step, (start, alive0))
        o_ref[...] = jnp.where(head0, acc[0], acc[1])
        olo_ref[...] = jnp.where(head0, acc[2], acc[3])

    npair = SB_W // TM
    blk = pl.BlockSpec((TM, TM), lambda p, i: (i, p))
    return pl.pallas_call(
        body, name="sb_fwd", grid=(npair, nb),
        in_specs=[blk,
                  pl.BlockSpec((lp, TM), lambda p, i: (0, npair + p)),
                  pl.BlockSpec((lp, TM), lambda p, i: (0, 2 * npair + p))],
        out_specs=[blk, blk],
        out_shape=[_SDS((lp, SB_W), F32), _SDS((lp, SB_W), F32)],
        scratch_shapes=[pltpu.VMEM((4, TM, TM), F32), pltpu.VMEM((2, TM, TM), F32)],
        compiler_params=_params(("arbitrary", "arbitrary")),
    )(proj, proj, proj)


def _sb_bwd(proj, o_sb, o_lo, d_o, d_proj):
    lp = proj.shape[0]
    nb = lp // TM
    npair = SB_W // TM
    scale = 1.0 / math.sqrt(HEAD)

    def body(q_ref, k_ref, v_ref, o_ref, olo_ref, do_ref, dproj_in, dproj_ref,
             dq_all, dk_ref, dv_ref, stage, sems, dq_acc, run_scr, gsum_scr):
        p = pl.program_id(0)
        i = pl.program_id(1)

        @pl.when(i == 0)
        def _():
            dk_ref[...] = jnp.zeros_like(dk_ref)
            dv_ref[...] = jnp.zeros_like(dv_ref)

        lane = _iota((TM, TM), 1)
        row = _iota((TM, TM), 0)
        head0 = lane < HEAD
        hmask = (head0, jnp.logical_not(head0))
        upper = (row > lane).astype(_MXU)
        lower_incl = (row >= lane).astype(_MXU)
        strict = lane < row
        q = q_ref[...] * scale
        do = do_ref[...]
        prod = do.astype(_MXU).astype(F32) * (o_ref[...] + olo_ref[...])
        qh = tuple(jnp.where(m, q, 0.0).astype(_MXU) for m in hmask)
        doh = tuple(jnp.where(m, do, 0.0).astype(_MXU) for m in hmask)
        gtot = tuple(jnp.sum(jnp.where(m, prod, 0.0), axis=1, keepdims=True) for m in hmask)

        def key_set(first, n, last_mask):
            off = pl.multiple_of(first * TM, TM)
            kf = k_ref[pl.ds(off, n * TM), :]
            kwin = kf.astype(_MXU)
            vwin = v_ref[pl.ds(off, n * TM), :].astype(_MXU)
            dk_win = None
            alive = None
            for hh in range(2):
                run = run_scr[hh][:, 0:1]
                gsum = gsum_scr[hh][:, 0:1]
                z, e, ls, lk_full = _sb_logs(qh[hh], kwin)
                w, run = _sb_weights(ls, lk_full, run, last_mask, upper, n)
                r = 1.0 / (1.0 + e)
                er = e * r
                pos = z >= 0.0
                beta = jnp.where(pos, r, er)
                one_m_beta = jnp.where(pos, er, r)
                dw = lax.dot_general(doh[hh], vwin, _NT, preferred_element_type=F32)
                g = [_blk(dw, b) * w[b] for b in range(n)]
                suffix = _stacked_sel(g, lower_incl)
                dz = [None] * n
                for b in range(n - 1, -1, -1):
                    prefix = gtot[hh] - gsum - suffix[b]
                    d = g[b] * _blk(one_m_beta, b) - _blk(beta, b) * prefix
                    dz[b] = (jnp.where(last_mask, d, 0.0) if b == n - 1 else d).astype(_MXU)
                    gsum = gsum + jnp.sum(g[b], axis=1, keepdims=True)
                dzw = jnp.concatenate(dz, axis=1)
                ww = jnp.concatenate([wb.astype(_MXU) for wb in w], axis=1)
                kh = jnp.where(hmask[hh][0:1, :], kf, 0.0).astype(_MXU)
                dq_acc[...] += jnp.dot(dzw, kh, preferred_element_type=F32)
                dk_h = lax.dot_general(dzw, qh[hh], _TN, preferred_element_type=F32)
                dv_h = lax.dot_general(ww, doh[hh], _TN, preferred_element_type=F32)
                dk_win = (dk_h, dv_h) if dk_win is None else (dk_win[0] + dk_h, dk_win[1] + dv_h)
                run_scr[hh] = jnp.broadcast_to(run, (TM, TM))
                gsum_scr[hh] = jnp.broadcast_to(gsum, (TM, TM))
                top = jnp.max(run)
                alive = top if alive is None else jnp.maximum(alive, top)
            dk_ref[pl.ds(off, n * TM), :] += dk_win[0]
            dv_ref[pl.ds(off, n * TM), :] += dk_win[1]
            return (alive > SB_DEAD).astype(jnp.int32)

        dq_acc[...] = jnp.zeros_like(dq_acc)
        run_scr[...] = jnp.zeros_like(run_scr)
        gsum_scr[...] = jnp.zeros_like(gsum_scr)

        @pl.when(i >= SB_WINDOW - 1)
        def _():
            key_set(i - (SB_WINDOW - 1), SB_WINDOW, strict)

        start = jnp.where(i >= SB_WINDOW - 1, i - SB_WINDOW, i)
        alive0 = (jnp.max(run_scr[...]) > SB_DEAD).astype(jnp.int32)

        def cond(c):
            return jnp.logical_and(c[0] >= 0, c[1] > 0)

        def step(c):
            kb = c[0]
            return kb - 1, key_set(kb, 1, jnp.logical_or(strict, kb < i))

        lax.while_loop(cond, step, (start, alive0))
        dq_all[pl.ds(pl.multiple_of(i * TM, TM), TM), :] = dq_acc[...] * scale

        @pl.when(i == nb - 1)
        def _():
            copies = []
            for s, src in enumerate((dq_all, dk_ref, dv_ref)):
                stage[s] = src[...].astype(_MXU)
                col = pl.multiple_of((s * npair + p) * TM, TM)
                copies.append(pltpu.make_async_copy(stage.at[s], dproj_ref.at[:, pl.ds(col, TM)], sems.at[s]))
                copies[-1].start()
            for cp in copies:
                cp.wait()

    blk = pl.BlockSpec((TM, TM), lambda p, i: (i, p))
    return pl.pallas_call(
        body, name="sb_bwd", grid=(npair, nb),
        in_specs=[blk,
                  pl.BlockSpec((lp, TM), lambda p, i: (0, npair + p)),
                  pl.BlockSpec((lp, TM), lambda p, i: (0, 2 * npair + p)),
                  blk, blk, blk, pl.BlockSpec(memory_space=pl.ANY)],
        out_specs=pl.BlockSpec(memory_space=pl.ANY),
        out_shape=_SDS(d_proj.shape, d_proj.dtype),
        input_output_aliases={6: 0},
        scratch_shapes=[pltpu.VMEM((lp, TM), F32), pltpu.VMEM((lp, TM), F32), pltpu.VMEM((lp, TM), F32),
                        pltpu.VMEM((3, lp, TM), _MXU), pltpu.SemaphoreType.DMA((3,)),
                        pltpu.VMEM((TM, TM), F32), pltpu.VMEM((2, TM, TM), F32), pltpu.VMEM((2, TM, TM), F32)],
        compiler_params=_params(("arbitrary", "arbitrary")),
    )(proj, proj, proj, o_sb, o_lo, d_o, d_proj)


def _conv_pre(x_ref, w_ref, b_ref, lp):
    n = lp - 8
    w = w_ref[...]
    pre = (x_ref[pl.ds(5, n), :] * w[0:1, :] + x_ref[pl.ds(6, n), :] * w[1:2, :]
           + x_ref[pl.ds(7, n), :] * w[2:3, :] + x_ref[pl.ds(8, n), :] * w[3:4, :]) + b_ref[...]
    live = (_iota((n, 128), 0) + 8) >= PAD
    return pre, live


def _conv_fwd(proj, dt_raw, conv_w, conv_b, dt_bias128):
    lp = proj.shape[0]
    nblk = XBC_W // 128
    c0 = COL_XBC // 128

    def body(x_ref, w_ref, b_ref, dtr_ref, dtb_ref, o_ref, dt_ref):
        pre, live = _conv_pre(x_ref, w_ref, b_ref, lp)
        act = pre * _sigmoid(pre)
        o_ref[pl.ds(0, 8), :] = jnp.zeros((8, 128), F32)
        o_ref[pl.ds(8, lp - 8), :] = jnp.where(live, act, 0.0)

        @pl.when(pl.program_id(0) == 0)
        def _():
            s = dtr_ref[...] + dtb_ref[...]
            sp = jnp.maximum(s, 0.0) + jnp.log(1.0 + jnp.exp(-jnp.abs(s)))
            dt_ref[...] = jnp.where(_iota((lp, 128), 0) >= PAD, sp, 0.0)

    return pl.pallas_call(
        body, name="conv_fwd", grid=(nblk,),
        in_specs=[pl.BlockSpec((lp, 128), lambda j: (0, c0 + j)),
                  pl.BlockSpec((4, 128), lambda j: (0, j)),
                  pl.BlockSpec((1, 128), lambda j: (0, j)),
                  pl.BlockSpec((lp, 128), lambda j: (0, 0)),
                  pl.BlockSpec((1, 128), lambda j: (0, 0))],
        out_specs=[pl.BlockSpec((lp, 128), lambda j: (0, j)),
                   pl.BlockSpec((lp, 128), lambda j: (0, 0))],
        out_shape=[_SDS((lp, XBC_W), F32), _SDS((lp, 128), F32)],
        compiler_params=_params(("arbitrary",)),
    )(proj, conv_w, conv_b, dt_raw, dt_bias128)


def _conv_bwd(proj, dt_raw, conv_w, conv_b, dt_bias128, d_xbc, d_dt128, d_proj):
    lp = proj.shape[0]
    nblk = XBC_W // 128
    c0 = COL_XBC // 128
    n = lp - 8
    last = nblk - 1

    def body(x_ref, w_ref, b_ref, dtr_ref, dtb_ref, dy_ref, ddt_ref, dproj_in,
             dx_ref, gw_ref, gb_ref, gdtb_ref, scr):
        j = pl.program_id(0)

        @pl.when(j < nblk)
        def _():
            pre, live = _conv_pre(x_ref, w_ref, b_ref, lp)
            sg = _sigmoid(pre)
            dpre = jnp.where(live, dy_ref[pl.ds(8, n), :] * (sg * (1.0 + pre * (1.0 - sg))), 0.0)
            gb_ref[...] = jnp.sum(dpre, axis=0, keepdims=True)
            gw_ref[...] = jnp.concatenate(
                [jnp.sum(dpre * x_ref[pl.ds(5 + k, n), :], axis=0, keepdims=True) for k in range(4)], axis=0)
            scr[pl.ds(0, 8), :] = jnp.zeros((8, 128), F32)
            scr[pl.ds(8, n), :] = dpre
            scr[pl.ds(lp, 8), :] = jnp.zeros((8, 128), F32)
            w = w_ref[...]
            dx_ref[...] = (scr[pl.ds(0, lp), :] * w[3:4, :] + scr[pl.ds(1, lp), :] * w[2:3, :]
                           + scr[pl.ds(2, lp), :] * w[1:2, :] + scr[pl.ds(3, lp), :] * w[0:1, :]).astype(dx_ref.dtype)

        @pl.when(j == nblk)
        def _():
            s = dtr_ref[...] + dtb_ref[...]
            d = jnp.where(_iota((lp, 128), 0) >= PAD, ddt_ref[...] * _sigmoid(s), 0.0)
            dx_ref[...] = d.astype(dx_ref.dtype)
            gdtb_ref[...] = jnp.sum(d, axis=0, keepdims=True)

    clamp = lambda j: (0, jnp.minimum(j, last))
    full128 = pl.BlockSpec((lp, 128), lambda j: (0, 0))
    return pl.pallas_call(
        body, name="conv_bwd", grid=(nblk + 1,),
        in_specs=[pl.BlockSpec((lp, 128), lambda j: (0, c0 + jnp.minimum(j, last))),
                  pl.BlockSpec((4, 128), clamp),
                  pl.BlockSpec((1, 128), clamp),
                  full128, pl.BlockSpec((1, 128), lambda j: (0, 0)),
                  pl.BlockSpec((lp, 128), clamp), full128, pl.BlockSpec(memory_space=pl.ANY)],
        out_specs=[pl.BlockSpec((lp, 128), lambda j: (0, c0 + j)), pl.BlockSpec((4, 128), clamp),
                   pl.BlockSpec((1, 128), clamp), pl.BlockSpec((1, 128), lambda j: (0, 0))],
        out_shape=[_SDS(d_proj.shape, d_proj.dtype), _SDS((4, XBC_W), F32), _SDS((1, XBC_W), F32), _SDS((1, 128), F32)],
        input_output_aliases={7: 0},
        scratch_shapes=[pltpu.VMEM((lp + 8, 128), F32)],
        compiler_params=_params(("arbitrary",)),
    )(proj, conv_w, conv_b, dt_raw, dt_bias128, d_xbc, d_dt128, d_proj)


def _ssd_pieces(dt, dt_t, a, a_t):
    r64 = _iota((CHUNK, CHUNK), 0)
    c64 = _iota((CHUNK, CHUNK), 1)
    tril = c64 <= r64
    tril01 = tril.astype(_MXU)
    triu01 = (r64 <= c64).astype(_MXU)
    expand = (lax.shift_right_logical(_iota((N_HEADS, SSD_W), 1), 6) == _iota((N_HEADS, SSD_W), 0)).astype(_MXU)
    acum = _sel_left(tril01, dt * a)
    acum_t = _sel_right(dt_t * a_t, triu01)
    ax = _sel_right(acum, expand)
    dtx = _sel_right(dt, expand)
    return tril, expand, acum, acum_t, ax, dtx


def _seg_matrix():
    return (lax.shift_right_logical(_iota((SSD_W, N_HEADS), 0), 6) == _iota((SSD_W, N_HEADS), 1)).astype(_MXU)


def _head_decay(ax, acum_t, h, tril):
    col = ax[:, h * HEAD:(h + 1) * HEAD]
    rowv = acum_t[h:h + 1, :]
    return jnp.where(tril, jnp.exp(jnp.minimum(col - rowv, 0.0)), 0.0)


def _ssd_fwd(xbc, dt_c, dt_tc, a, a_t, dskip_x):
    lp = xbc.shape[0]
    nc = lp // CHUNK
    gw = SSD_W // N_GROUPS
    hpg = N_HEADS // N_GROUPS

    def body(x_ref, dt_ref, dtt_ref, a_ref, at_ref, d_ref, y_ref, st_ref, state):
        c = pl.program_id(0)

        @pl.when(c == 0)
        def _():
            state[...] = jnp.zeros_like(state)

        st_ref[0] = state[...]
        tril, _, _, acum_t, ax, dtx = _ssd_pieces(dt_ref[0], dtt_ref[0], a_ref[...], at_ref[...])
        x = x_ref[:, 0:SSD_W]
        xdt = x * dtx
        ea = jnp.exp(ax)
        aex = ax[CHUNK - 1:CHUNK, :]
        wd = jnp.exp(aex - ax)
        eae = jnp.exp(aex)
        xw = xdt * wd
        y_ref[...] = x * d_ref[...]
        for g in range(N_GROUPS):
            gs = slice(g * gw, (g + 1) * gw)
            rs = slice(g * N_STATE, (g + 1) * N_STATE)
            bg = x_ref[:, SSD_W + g * N_STATE:SSD_W + (g + 1) * N_STATE]
            cg = x_ref[:, SSD_W + N_GROUPS * N_STATE + g * N_STATE:SSD_W + N_GROUPS * N_STATE + (g + 1) * N_STATE]
            sg = state[rs, :]
            cb = _mm_nt(cg, bg)
            y_ref[:, gs] += _mm(cg, sg) * ea[:, gs]
            for r in range(hpg):
                h = g * hpg + r
                hs = slice(h * HEAD, (h + 1) * HEAD)
                m = cb * _head_decay(ax, acum_t, h, tril)
                y_ref[:, hs] += _mm(m, xdt[:, hs])
            state[rs, :] = sg * eae[:, gs] + _mm_tn(bg, xw[:, gs])

    return pl.pallas_call(
        body, name="ssd_fwd", grid=(nc,),
        in_specs=[pl.BlockSpec((CHUNK, XBC_W), lambda c: (c, 0)),
                  pl.BlockSpec((1, CHUNK, N_HEADS), lambda c: (c, 0, 0)),
                  pl.BlockSpec((1, N_HEADS, CHUNK), lambda c: (c, 0, 0)),
                  pl.BlockSpec((1, N_HEADS), lambda c: (0, 0)),
                  pl.BlockSpec((N_HEADS, 1), lambda c: (0, 0)),
                  pl.BlockSpec((1, SSD_W), lambda c: (0, 0))],
        out_specs=[pl.BlockSpec((CHUNK, SSD_W), lambda c: (c, 0)),
                   pl.BlockSpec((1, N_GROUPS * N_STATE, gw), lambda c: (c, 0, 0))],
        out_shape=[_SDS((lp, SSD_W), F32), _SDS((nc, N_GROUPS * N_STATE, gw), F32)],
        scratch_shapes=[pltpu.VMEM((N_GROUPS * N_STATE, gw), F32)],
        compiler_params=_params(("arbitrary",)),
    )(xbc, dt_c, dt_tc, a, a_t, dskip_x)


def _ssd_bwd(xbc, dt_c, dt_tc, a, a_t, dskip_x, states, d_y):
    lp = xbc.shape[0]
    nc = lp // CHUNK
    gw = SSD_W // N_GROUPS
    hpg = N_HEADS // N_GROUPS

    def body(x_ref, dt_ref, dtt_ref, a_ref, at_ref, d_ref, st_ref, dy_ref,
             dx_ref, ddta_ref, ddtb_ref, ga1_ref, ga2_ref, gd_ref, dstate, dxdt_scr, z_scr, yoff_scr, sds_scr):
        c = pl.program_id(0)

        @pl.when(c == 0)
        def _():
            dstate[...] = jnp.zeros_like(dstate)
            ga1_ref[...] = jnp.zeros_like(ga1_ref)
            ga2_ref[...] = jnp.zeros_like(ga2_ref)
            gd_ref[...] = jnp.zeros_like(gd_ref)

        dt = dt_ref[0]
        dt_t = dtt_ref[0]
        a = a_ref[...]
        a_t = at_ref[...]
        tril, _, acum, acum_t, ax, dtx = _ssd_pieces(dt, dt_t, a, a_t)
        seg = _seg_matrix()
        x = x_ref[:, 0:SSD_W]
        dy = dy_ref[...]
        xdt = x * dtx
        ea = jnp.exp(ax)
        aex = ax[CHUNK - 1:CHUNK, :]
        wd = jnp.exp(aex - ax)
        eae = jnp.exp(aex)
        xw = xdt * wd
        edy = ea * dy
        lane16 = _iota((CHUNK, N_HEADS), 1)
        row16 = _iota((N_HEADS, CHUNK), 0)
        da_col = jnp.zeros((CHUNK, N_HEADS), F32)
        da_row = jnp.zeros((N_HEADS, CHUNK), F32)
        for g in range(N_GROUPS):
            gs = slice(g * gw, (g + 1) * gw)
            rs = slice(g * N_STATE, (g + 1) * N_STATE)
            bcol = slice(SSD_W + g * N_STATE, SSD_W + (g + 1) * N_STATE)
            ccol = slice(SSD_W + N_GROUPS * N_STATE + g * N_STATE, SSD_W + N_GROUPS * N_STATE + (g + 1) * N_STATE)
            bg = x_ref[:, bcol]
            cg = x_ref[:, ccol]
            sg = st_ref[0, rs, :]
            dsn = dstate[rs, :]
            cb = _mm_nt(cg, bg)
            z_scr[:, gs] = _mm(bg, dsn)
            yoff_scr[:, gs] = _mm(cg, sg) * ea[:, gs]
            sds_scr[:, gs] = jnp.broadcast_to(jnp.sum(dsn * sg, axis=0, keepdims=True), (8, gw))
            dcb = jnp.zeros((CHUNK, CHUNK), F32)
            for r in range(hpg):
                h = g * hpg + r
                hs = slice(h * HEAD, (h + 1) * HEAD)
                dec = _head_decay(ax, acum_t, h, tril)
                m = cb * dec
                t1 = _mm_nt(dy[:, hs], xdt[:, hs])
                dcb = dcb + dec * t1
                tm = m * t1
                da_col = da_col + jnp.where(lane16 == h, jnp.sum(tm, axis=1, keepdims=True), 0.0)
                da_row = da_row - jnp.where(row16 == h, jnp.sum(tm, axis=0, keepdims=True), 0.0)
                dxdt_scr[:, hs] = _mm_tn(m, dy[:, hs])
            dx_ref[:, ccol] = _mm(dcb, bg) + _mm_nt(edy[:, gs], sg)
            dx_ref[:, bcol] = _mm_tn(dcb, cg) + _mm_nt(xw[:, gs], dsn)
            dstate[rs, :] = eae[:, gs] * dsn + _mm_tn(cg, edy[:, gs])
        zf = z_scr[...]
        dxdt = dxdt_scr[...] + wd * zf
        t3 = _sel_right(xw * zf, seg)
        da_col = da_col + _sel_right(dy * yoff_scr[...], seg) - t3
        aend = acum[CHUNK - 1:CHUNK, :]
        sd = _sel_right(sds_scr[...], seg)[0:1, :] * jnp.exp(aend)
        last = jnp.sum(t3, axis=0, keepdims=True) + sd
        da_col = da_col + jnp.where(_iota((CHUNK, N_HEADS), 0) == CHUNK - 1, last, 0.0)
        r64 = _iota((CHUNK, CHUNK), 0)
        c64 = _iota((CHUNK, CHUNK), 1)
        ddta1 = _sel_left((c64 >= r64).astype(_MXU), da_col)
        ddta2 = _sel_right(da_row, (r64 >= c64).astype(_MXU))
        ddta_ref[0] = a * ddta1 + _sel_right(dxdt * x, seg)
        ddtb_ref[0] = a_t * ddta2
        ga1_ref[...] += jnp.sum(dt * ddta1, axis=0, keepdims=True)
        ga2_ref[...] += jnp.sum(dt_t * ddta2, axis=1, keepdims=True)
        dx_ref[:, 0:SSD_W] = dxdt * dtx + d_ref[...] * dy
        gd_ref[...] += jnp.sum(dy * x, axis=0, keepdims=True)

    rev = lambda c: (nc - 1 - c, 0)
    rev3 = lambda c: (nc - 1 - c, 0, 0)
    return pl.pallas_call(
        body, name="ssd_bwd", grid=(nc,),
        in_specs=[pl.BlockSpec((CHUNK, XBC_W), rev),
                  pl.BlockSpec((1, CHUNK, N_HEADS), rev3),
                  pl.BlockSpec((1, N_HEADS, CHUNK), rev3),
                  pl.BlockSpec((1, N_HEADS), lambda c: (0, 0)),
                  pl.BlockSpec((N_HEADS, 1), lambda c: (0, 0)),
                  pl.BlockSpec((1, SSD_W), lambda c: (0, 0)),
                  pl.BlockSpec((1, N_GROUPS * N_STATE, gw), rev3),
                  pl.BlockSpec((CHUNK, SSD_W), rev)],
        out_specs=[pl.BlockSpec((CHUNK, XBC_W), rev),
                   pl.BlockSpec((1, CHUNK, N_HEADS), rev3),
                   pl.BlockSpec((1, N_HEADS, CHUNK), rev3),
                   pl.BlockSpec((1, N_HEADS), lambda c: (0, 0)),
                   pl.BlockSpec((N_HEADS, 1), lambda c: (0, 0)),
                   pl.BlockSpec((1, SSD_W), lambda c: (0, 0))],
        out_shape=[_SDS((lp, XBC_W), F32), _SDS((nc, CHUNK, N_HEADS), F32), _SDS((nc, N_HEADS, CHUNK), F32),
                   _SDS((1, N_HEADS), F32), _SDS((N_HEADS, 1), F32), _SDS((1, SSD_W), F32)],
        scratch_shapes=[pltpu.VMEM((N_GROUPS * N_STATE, gw), F32), pltpu.VMEM((CHUNK, SSD_W), F32),
                        pltpu.VMEM((CHUNK, SSD_W), F32), pltpu.VMEM((CHUNK, SSD_W), F32),
                        pltpu.VMEM((8, SSD_W), F32)],
        compiler_params=_params(("arbitrary",)),
    )(xbc, dt_c, dt_tc, a, a_t, dskip_x, states, d_y)


def _gated_norm(o, gate, w):
    sg = _sigmoid(gate)
    p = o * (gate * sg)
    rs = lax.rsqrt(jnp.mean(p * p, axis=-1, keepdims=True) + EPS)
    n = p * rs
    return sg, rs, n, n * w


def _tail_fwd(o_sb, o_ssd, proj, h0, target, w_out, sb_w, ssd_w, fin_w):
    lp = o_sb.shape[0]
    nb = lp // TM
    row = lambda i: (i, 0)
    one = lambda i: (0, 0)

    def body(osb_ref, gate_ref, ossd_ref, z_ref, h0_ref, tgt_ref, wo_ref, sbw_ref, ssdw_ref, fw_ref,
             dh1_ref, loss_ref, gfw_ref):
        i = pl.program_id(0)

        @pl.when(i == 0)
        def _():
            loss_ref[...] = jnp.zeros_like(loss_ref)
            gfw_ref[...] = jnp.zeros_like(gfw_ref)

        y1 = _gated_norm(osb_ref[...], gate_ref[...], sbw_ref[...])[3]
        y2 = _gated_norm(ossd_ref[...], z_ref[...], ssdw_ref[...])[3]
        h1 = (h0_ref[...] + _mm(y1, wo_ref[0:SB_W, :])) + _mm(y2, wo_ref[SB_W:SB_W + SSD_W, :])
        rs1 = lax.rsqrt(jnp.mean(h1 * h1, axis=-1, keepdims=True) + EPS)
        n1 = h1 * rs1
        fw = fw_ref[...]
        diff = jnp.where(i > 0, n1 * fw - tgt_ref[...], 0.0)
        loss_ref[...] += jnp.sum(diff * diff, axis=0, keepdims=True)
        d_out = diff * (1.0 / D_MODEL)
        gfw_ref[...] += jnp.sum(d_out * n1, axis=0, keepdims=True)
        g = d_out * fw
        dh1_ref[...] = rs1 * (g - n1 * jnp.mean(g * n1, axis=-1, keepdims=True))

    return pl.pallas_call(
        body, name="tail_fwd", grid=(nb,),
        in_specs=[pl.BlockSpec((TM, SB_W), row),
                  pl.BlockSpec((TM, SB_W), lambda i: (i, COL_GATE // SB_W)),
                  pl.BlockSpec((TM, SSD_W), row),
                  pl.BlockSpec((TM, SSD_W), lambda i: (i, COL_Z // SSD_W)),
                  pl.BlockSpec((TM, D_MODEL), row),
                  pl.BlockSpec((TM, D_MODEL), lambda i: (jnp.maximum(i - 1, 0), 0)),
                  pl.BlockSpec(memory_space=_VMEM),
                  pl.BlockSpec((1, SB_W), one), pl.BlockSpec((1, SSD_W), one), pl.BlockSpec((1, D_MODEL), one)],
        out_specs=[pl.BlockSpec((TM, D_MODEL), row), pl.BlockSpec((1, D_MODEL), one), pl.BlockSpec((1, D_MODEL), one)],
        out_shape=[_SDS((lp, D_MODEL), F32), _SDS((1, D_MODEL), F32), _SDS((1, D_MODEL), F32)],
        compiler_params=_params(("arbitrary",), 40),
    )(o_sb, proj, o_ssd, proj, h0, target, w_out, sb_w, ssd_w, fin_w)


def _gated_norm_bwd(o, gate, w, dy):
    sg, rs, n, _ = _gated_norm(o, gate, w)
    gw = jnp.sum(dy * n, axis=0, keepdims=True)
    dn = dy * w
    dp = rs * (dn - n * jnp.mean(dn * n, axis=-1, keepdims=True))
    d_o = dp * (gate * sg)
    d_gate = dp * o * (sg * (1.0 + gate * (1.0 - sg)))
    return d_o, d_gate, gw, n * w


def _tail_bwd(o_sb, o_ssd, proj, d_h1, w_out, sb_w, ssd_w):
    lp = o_sb.shape[0]
    nb = lp // TM
    row = lambda i, t: (i, 0)
    one = lambda i, t: (0, 0)

    def body(osb_ref, gate_ref, ossd_ref, z_ref, dh1_ref, wo_ref, sbw_ref, ssdw_ref,
             dosb_ref, dossd_ref, dproj_ref, gwo_ref, gsb_ref, gssd_ref):
        i = pl.program_id(0)
        t = pl.program_id(1)

        @pl.when(jnp.logical_and(i == 0, t == 0))
        def _():
            gwo_ref[...] = jnp.zeros_like(gwo_ref)
            gsb_ref[...] = jnp.zeros_like(gsb_ref)
            gssd_ref[...] = jnp.zeros_like(gssd_ref)

        dh1 = dh1_ref[...].astype(_MXU)

        def half(o_ref, g_ref, w_ref, do_ref, gn_ref, r0):
            dy = lax.dot_general(dh1, wo_ref[r0:r0 + SB_W, :], _NT, preferred_element_type=F32)
            d_o, d_g, gw, y = _gated_norm_bwd(o_ref[...], g_ref[...], w_ref[...], dy)
            do_ref[...] = d_o
            dproj_ref[...] = d_g.astype(_MXU)
            gn_ref[...] += gw
            gwo_ref[r0:r0 + SB_W, :] += lax.dot_general(y.astype(_MXU), dh1, _TN, preferred_element_type=F32)

        @pl.when(t == 0)
        def _():
            half(osb_ref, gate_ref, sbw_ref, dosb_ref, gsb_ref, 0)

        @pl.when(t == 1)
        def _():
            half(ossd_ref, z_ref, ssdw_ref, dossd_ref, gssd_ref, SB_W)

    tile = pl.BlockSpec((TM, SB_W), row)
    return pl.pallas_call(
        body, name="tail_bwd", grid=(nb, 2),
        in_specs=[tile, pl.BlockSpec((TM, SB_W), lambda i, t: (i, COL_GATE // SB_W)),
                  tile, pl.BlockSpec((TM, SSD_W), lambda i, t: (i, COL_Z // SSD_W)),
                  tile, pl.BlockSpec(memory_space=_VMEM),
                  pl.BlockSpec((1, SB_W), one), pl.BlockSpec((1, SSD_W), one)],
        out_specs=[tile, tile, pl.BlockSpec((TM, SB_W), lambda i, t: (i, COL_GATE // SB_W + t)),
                   pl.BlockSpec((SB_W + SSD_W, D_MODEL), one), pl.BlockSpec((1, SB_W), one), pl.BlockSpec((1, SSD_W), one)],
        out_shape=[_SDS((lp, SB_W), F32), _SDS((lp, SSD_W), F32), _SDS((lp, W_ALL), _MXU),
                   _SDS((SB_W + SSD_W, D_MODEL), F32), _SDS((1, SB_W), F32), _SDS((1, SSD_W), F32)],
        compiler_params=_params(("arbitrary", "arbitrary"), 48),
    )(o_sb, proj, o_ssd, proj, d_h1, w_out, sb_w, ssd_w)


def _d_u(d_proj, w_t):
    lp = d_proj.shape[0]
    tk = 512

    def body(dp_ref, w_ref, dpdt_ref, wdt_ref, o_ref):
        @pl.when(pl.program_id(0) == 0)
        def _():
            o_ref[...] = jnp.dot(dpdt_ref[...], wdt_ref[...], preferred_element_type=F32)

        o_ref[...] += jnp.dot(dp_ref[...], w_ref[...], preferred_element_type=F32)

    return pl.pallas_call(
        body, name="d_u", grid=(N_MAIN // tk,),
        in_specs=[pl.BlockSpec((lp, tk), lambda j: (0, j)),
                  pl.BlockSpec((tk, D_MODEL), lambda j: (j, 0)),
                  pl.BlockSpec((lp, 128), lambda j: (0, N_MAIN // 128)),
                  pl.BlockSpec((128, D_MODEL), lambda j: (N_MAIN // 128, 0))],
        out_specs=pl.BlockSpec((lp, D_MODEL), lambda j: (0, 0)),
        out_shape=_SDS((lp, D_MODEL), F32),
        compiler_params=_params(("arbitrary",), 48),
    )(d_proj, w_t, d_proj, w_t)


def _norm_bwd(du_all, h0, d_h1, norm_w):
    lp = h0.shape[0]
    nb = lp // TM
    seq = lp - OFF
    row = lambda i: (i, 0)
    one = lambda i: (0, 0)

    def body(du_ref, h0_ref, dh1_ref, nw_ref, gx_ref, gmeta_ref, gnw_ref):
        i = pl.program_id(0)

        @pl.when(i == 0)
        def _():
            gnw_ref[...] = jnp.zeros_like(gnw_ref)

        du = du_ref[...]
        h = h0_ref[...]
        rs = lax.rsqrt(jnp.mean(h * h, axis=-1, keepdims=True) + EPS)
        n0 = h * rs
        gnw_ref[...] += jnp.sum(du * n0, axis=0, keepdims=True)
        g = du * nw_ref[...]
        dh0 = dh1_ref[...] + rs * (g - n0 * jnp.mean(g * n0, axis=-1, keepdims=True))

        @pl.when(i == 0)
        def _():
            gmeta_ref[...] = dh0[PAD:PAD + N_META, :]

        @pl.when(i > 0)
        def _():
            gx_ref[...] = dh0

    tile = pl.BlockSpec((TM, D_MODEL), row)
    return pl.pallas_call(
        body, name="norm_bwd", grid=(nb,),
        in_specs=[tile, tile, tile, pl.BlockSpec((1, D_MODEL), one)],
        out_specs=[pl.BlockSpec((TM, D_MODEL), lambda i: (jnp.maximum(i - 1, 0), 0)),
                   pl.BlockSpec((N_META, D_MODEL), one), pl.BlockSpec((1, D_MODEL), one)],
        out_shape=[_SDS((seq, D_MODEL), F32), _SDS((N_META, D_MODEL), F32), _SDS((1, D_MODEL), F32)],
        compiler_params=_params(("arbitrary",)),
    )(du_all, h0, d_h1, norm_w)


def _grad_w_windows(u_t, d_proj):
    lp = d_proj.shape[0]
    hw = WIN_W // 2
    steps = 2 * N_CHIPS

    def body(ut_ref, dp_hbm, o_ref, buf, sems):
        s = pl.program_id(0)
        slot = s % 2

        def fetch(step, sl):
            start = pl.multiple_of((step // 2) * WIN_STEP + (step % 2) * hw, 128)
            return pltpu.make_async_copy(dp_hbm.at[:, pl.ds(start, hw)], buf.at[sl], sems.at[sl])

        @pl.when(s == 0)
        def _():
            fetch(0, 0).start()

        @pl.when(s + 1 < steps)
        def _():
            fetch(s + 1, 1 - slot).start()

        fetch(s, slot).wait()
        o_ref[0] = jnp.dot(ut_ref[...], buf[slot], preferred_element_type=F32).astype(o_ref.dtype)

    return pl.pallas_call(
        body, name="grad_w_in", grid=(steps,),
        in_specs=[pl.BlockSpec((D_MODEL, lp), lambda s: (0, 0)), pl.BlockSpec(memory_space=pl.ANY)],
        out_specs=pl.BlockSpec((1, D_MODEL, hw), lambda s: (s // 2, 0, s % 2)),
        out_shape=_SDS((N_CHIPS, D_MODEL, WIN_W), _MXU),
        scratch_shapes=[pltpu.VMEM((2, lp, hw), _MXU), pltpu.SemaphoreType.DMA((2,))],
        compiler_params=_params(("arbitrary",), 40),
    )(u_t, d_proj)


def _device_grads(x2d, target2d, meta_full, norm_w, w_t, conv_w, conv_b, dt_bias, a_log, d_skip,
                  sb_w, ssd_w, w_out, fin_w):
    lp = x2d.shape[0] + OFF
    nc = lp // CHUNK
    h0, u, u_t = _prep(x2d, meta_full, norm_w)
    proj, dt_raw = _inproj(u, w_t)
    o_sb, o_lo = _sb_fwd(proj)
    dt_bias128 = jnp.pad(dt_bias, ((0, 0), (0, 128 - N_HEADS)))
    xbc, dt128 = _conv_fwd(proj, dt_raw, conv_w, conv_b, dt_bias128)
    dt_c = dt128[:, :N_HEADS].reshape(nc, CHUNK, N_HEADS)
    dt_tc = jnp.swapaxes(dt_c, 1, 2)
    a = -jnp.exp(a_log)
    a_t = a.reshape(N_HEADS, 1)
    dskip_x = jnp.repeat(d_skip, HEAD, axis=1)
    o_ssd, states = _ssd_fwd(xbc, dt_c, dt_tc, a, a_t, dskip_x)
    d_h1, sq_err, g_fin = _tail_fwd(o_sb, o_ssd, proj, h0, target2d, w_out, sb_w, ssd_w, fin_w)

    d_osb, d_ossd, d_proj, g_wout, g_sb, g_ssd = _tail_bwd(o_sb, o_ssd, proj, d_h1, w_out, sb_w, ssd_w)
    d_proj = _sb_bwd(proj, o_sb, o_lo, d_osb, d_proj)
    d_xbc_act, ddt_a, ddt_b, ga1, ga2, gd = _ssd_bwd(xbc, dt_c, dt_tc, a, a_t, dskip_x, states, d_ossd)
    d_dt = (ddt_a + jnp.swapaxes(ddt_b, 1, 2)).reshape(lp, N_HEADS)
    d_dt128 = jnp.pad(d_dt, ((0, 0), (0, 128 - N_HEADS)))
    d_proj, g_convw, g_convb, g_dtb128 = _conv_bwd(proj, dt_raw, conv_w, conv_b, dt_bias128, d_xbc_act, d_dt128, d_proj)
    g_win = _grad_w_windows(u_t, d_proj)
    g_x, g_meta, g_nw = _norm_bwd(_d_u(d_proj, w_t), h0, d_h1, norm_w)
    g_alog = (ga1 + ga2.reshape(1, N_HEADS)) * a
    g_dskip = gd.reshape(N_HEADS, HEAD).sum(axis=1).reshape(1, N_HEADS)
    grads = dict(meta_tokens=g_meta, norm_w=g_nw, w_in=g_win, conv_w=g_convw, conv_b=g_convb,
                 dt_bias=g_dtb128[:, :N_HEADS], a_log=g_alog, d_skip=g_dskip, sb_norm_w=g_sb, ssd_norm_w=g_ssd,
                 w_out=g_wout, final_norm_w=g_fin)
    return sq_err, g_x, grads


_MESH = pl.DeviceIdType.MESH
_ANY = pl.BlockSpec(memory_space=pl.ANY)


def _place():
    return lax.axis_index("x"), lax.axis_index("y"), lax.axis_index("c")


def _other_chips(x, y):
    return ((1 - x, y), (x, 1 - y), (1 - x, 1 - y))


def _gather_shards(arrays, n_big):
    n = len(arrays)

    def body(*refs):
        srcs, dsts = refs[:n], refs[n:2 * n]
        send_sems, recv_sems, fwd_send, fwd_recv = refs[2 * n:]
        x, y, c = _place()
        mine = 2 * x + y
        chips = _other_chips(x, y)

        def window(a):
            half = arrays[a].shape[1] // 2
            return pl.ds(pl.multiple_of(c * half, 128), half)

        first = []
        for a in range(n):
            for k, (px, py) in enumerate(chips):
                if a < n_big:
                    src, dst = srcs[a].at[:, window(a)], dsts[a].at[mine, :, window(a)]
                else:
                    src, dst = srcs[a], dsts[a].at[mine]
                cp = pltpu.make_async_remote_copy(
                    src_ref=src, dst_ref=dst, send_sem=send_sems.at[a * 3 + k], recv_sem=recv_sems.at[a * 3 + k],
                    device_id=(px, py, c), device_id_type=_MESH)
                cp.start()
                first.append(cp)
        passed = []
        for a in range(n):
            for k, (px, py) in enumerate(chips):
                first[a * 3 + k].wait_recv()
                if a < n_big:
                    landed = dsts[a].at[2 * px + py, :, window(a)]
                    cp = pltpu.make_async_remote_copy(
                        src_ref=landed, dst_ref=landed, send_sem=fwd_send.at[a * 3 + k], recv_sem=fwd_recv.at[a * 3 + k],
                        device_id=(x, y, 1 - c), device_id_type=_MESH)
                    cp.start()
                    passed.append(cp)
        for cp in passed:
            cp.wait_recv()
        for cp in first + passed:
            cp.wait_send()

    got = pl.pallas_call(
        body, name="gather_shards",
        in_specs=[_ANY] * n, out_specs=[_ANY] * n,
        out_shape=[_SDS((N_CHIPS,) + a.shape, a.dtype) for a in arrays],
        scratch_shapes=[pltpu.SemaphoreType.DMA((3 * n,)), pltpu.SemaphoreType.DMA((3 * n,)),
                        pltpu.SemaphoreType.DMA((3 * n_big,)), pltpu.SemaphoreType.DMA((3 * n_big,))],
    )(*arrays)
    mine = 2 * lax.axis_index("x") + lax.axis_index("y")
    return [lax.dynamic_update_slice(g, a[None], (mine,) + (0,) * a.ndim) for g, a in zip(got, arrays)]


def _scatter_slabs(arrays):
    n = len(arrays)

    def body(*refs):
        srcs, dsts = refs[:n], refs[n:2 * n]
        send_sems, recv_sems = refs[2 * n:]
        x, y, c = _place()
        mine = 2 * x + y
        remote = []
        for a in range(n):
            for k, (px, py) in enumerate(_other_chips(x, y)):
                cp = pltpu.make_async_remote_copy(
                    src_ref=srcs[a].at[2 * px + py], dst_ref=dsts[a].at[mine],
                    send_sem=send_sems.at[a * 3 + k], recv_sem=recv_sems.at[a * 3 + k],
                    device_id=(px, py, c), device_id_type=_MESH)
                cp.start()
                remote.append(cp)
        for cp in remote:
            cp.wait_recv()
        for cp in remote:
            cp.wait_send()

    return pl.pallas_call(
        body, name="scatter_slabs",
        in_specs=[_ANY] * n, out_specs=[_ANY] * n,
        out_shape=[_SDS(a.shape, a.dtype) for a in arrays],
        scratch_shapes=[pltpu.SemaphoreType.DMA((3 * n,)), pltpu.SemaphoreType.DMA((3 * n,))],
    )(*arrays)


def _swap_halves(arrays):
    n = len(arrays)

    def body(*refs):
        srcs, dsts = refs[:n], refs[n:2 * n]
        send_sems, recv_sems = refs[2 * n:]
        x, y, c = _place()
        copies = []
        for a in range(n):
            half = arrays[a].shape[1] // 2
            cp = pltpu.make_async_remote_copy(
                src_ref=srcs[a].at[:, pl.ds(pl.multiple_of((1 - c) * half, 16), half)], dst_ref=dsts[a],
                send_sem=send_sems.at[a], recv_sem=recv_sems.at[a],
                device_id=(x, y, 1 - c), device_id_type=_MESH)
            cp.start()
            copies.append(cp)
        for cp in copies:
            cp.wait_recv()
        for cp in copies:
            cp.wait_send()

    return pl.pallas_call(
        body, name="swap_halves",
        in_specs=[_ANY] * n, out_specs=[_ANY] * n,
        out_shape=[_SDS((a.shape[0], a.shape[1] // 2, a.shape[2]), a.dtype) for a in arrays],
        scratch_shapes=[pltpu.SemaphoreType.DMA((n,)), pltpu.SemaphoreType.DMA((n,))],
    )(*arrays)


def _join_halves(arrays, by_cols):
    n = len(arrays)

    def body(*refs):
        dsts = refs[n:2 * n]
        send_sems, recv_sems = refs[2 * n:]
        x, y, c = _place()
        copies = []
        for a in range(n):
            if by_cols[a]:
                half = arrays[a].shape[1] // 2
                mine = dsts[a].at[:, pl.ds(pl.multiple_of(c * half, 128), half)]
            else:
                half = arrays[a].shape[0] // 2
                mine = dsts[a].at[pl.ds(pl.multiple_of(c * half, 16), half)]
            cp = pltpu.make_async_remote_copy(
                src_ref=mine, dst_ref=mine, send_sem=send_sems.at[a], recv_sem=recv_sems.at[a],
                device_id=(x, y, 1 - c), device_id_type=_MESH)
            cp.start()
            copies.append(cp)
        for cp in copies:
            cp.wait_recv()
        for cp in copies:
            cp.wait_send()

    return pl.pallas_call(
        body, name="join_halves",
        in_specs=[_ANY] * n, out_specs=[_ANY] * n,
        out_shape=[_SDS(a.shape, a.dtype) for a in arrays],
        input_output_aliases={a: a for a in range(n)},
        scratch_shapes=[pltpu.SemaphoreType.DMA((n,)), pltpu.SemaphoreType.DMA((n,))],
    )(*arrays)


N_DEV = 8
SMALL_ROWS = 32
SMALL_COLS = XBC_W


def _gather_small(packed):
    def body(src_ref, dst_ref, send_sems, recv_sems, local_sem):
        x, y, c = _place()
        me = 4 * x + 2 * y + c
        own = pltpu.make_async_copy(src_ref, dst_ref.at[me], local_sem)
        own.start()
        copies = []
        for k in range(1, N_DEV):
            bx, by, bc = (k >> 2) & 1, (k >> 1) & 1, k & 1
            peer = (x + bx - 2 * x * bx, y + by - 2 * y * by, c + bc - 2 * c * bc)
            cp = pltpu.make_async_remote_copy(
                src_ref=src_ref, dst_ref=dst_ref.at[me], send_sem=send_sems.at[k - 1], recv_sem=recv_sems.at[k - 1],
                device_id=peer, device_id_type=_MESH)
            cp.start()
            copies.append(cp)
        for cp in copies:
            cp.wait_recv()
        for cp in copies:
            cp.wait_send()
        own.wait()

    return pl.pallas_call(
        body, name="gather_small",
        in_specs=[pl.BlockSpec(memory_space=_VMEM)], out_specs=pl.BlockSpec(memory_space=_VMEM),
        out_shape=_SDS((N_DEV, SMALL_ROWS, SMALL_COLS), F32),
        scratch_shapes=[pltpu.SemaphoreType.DMA((N_DEV - 1,)), pltpu.SemaphoreType.DMA((N_DEV - 1,)),
                        pltpu.SemaphoreType.DMA],
    )(packed)


def _adamw(w, g, m, v):
    m = ADAM_B1 * m + (1.0 - ADAM_B1) * g
    v = ADAM_B2 * v + (1.0 - ADAM_B2) * (g * g)
    m_hat = m / (1.0 - ADAM_B1 ** ADAM_STEP)
    v_hat = v / (1.0 - ADAM_B2 ** ADAM_STEP)
    delta = -ADAM_LR * (m_hat / (jnp.sqrt(v_hat) + ADAM_EPS) + ADAM_WD * w)
    return delta, m, v


def _sum_slabs(slabs, core, name, transposed=False):
    _, h, c = slabs.shape
    tr = 128
    nblk = h // tr

    def body(core_ref, s_ref, o_ref):
        tot = ((s_ref[0].astype(F32) + s_ref[1].astype(F32)) + s_ref[2].astype(F32)) + s_ref[3].astype(F32)
        o_ref[...] = tot.T if transposed else tot

    if transposed:
        out_spec = pl.BlockSpec((c, tr), lambda i, core_ref: (0, core_ref[0] * nblk + i))
        out_shape = _SDS((c, 2 * h), F32)
    else:
        out_spec = pl.BlockSpec((tr, c), lambda i, core_ref: (core_ref[0] * nblk + i, 0))
        out_shape = _SDS((2 * h, c), F32)
    grid_spec = pltpu.PrefetchScalarGridSpec(
        num_scalar_prefetch=1, grid=(nblk,),
        in_specs=[pl.BlockSpec((N_CHIPS, tr, c), lambda i, core_ref: (0, i, 0))],
        out_specs=out_spec)
    return pl.pallas_call(
        body, name=name, grid_spec=grid_spec, out_shape=out_shape,
        compiler_params=_params(("arbitrary",)),
    )(core, slabs)


def _add_halves(own, recv, core, name):
    _, r, c = own.shape
    half = r // 2
    tr = 128
    nblk = half // tr

    def body(core_ref, a_ref, b_ref, o_ref):
        o_ref[...] = (a_ref[...].astype(F32) + b_ref[...].astype(F32)).astype(o_ref.dtype)

    grid_spec = pltpu.PrefetchScalarGridSpec(
        num_scalar_prefetch=1, grid=(nblk,),
        in_specs=[pl.BlockSpec((N_CHIPS, tr, c), lambda i, core_ref: (0, core_ref[0] * nblk + i, 0)),
                  pl.BlockSpec((N_CHIPS, tr, c), lambda i, core_ref: (0, i, 0))],
        out_specs=pl.BlockSpec((N_CHIPS, tr, c), lambda i, core_ref: (0, i, 0)))
    return pl.pallas_call(
        body, name=name, grid_spec=grid_spec, out_shape=_SDS((N_CHIPS, half, c), own.dtype),
        compiler_params=_params(("arbitrary",)),
    )(core, own, recv)


def _update_big(w, m, v, g, name):
    r, c = w.shape

    def body(w_ref, m_ref, v_ref, g_ref, d_ref, mo_ref, vo_ref):
        delta, m_new, v_new = _adamw(w_ref[...], g_ref[...], m_ref[...], v_ref[...])
        d_ref[...] = delta
        mo_ref[...] = m_new
        vo_ref[...] = v_new

    if r % 128 == 0:
        steps, spec = r // 128, pl.BlockSpec((128, c), lambda i: (i, 0))
    else:
        steps, spec = c // 128, pl.BlockSpec((r, 128), lambda i: (0, i))
    return pl.pallas_call(
        body, name=name, grid=(steps,),
        in_specs=[spec] * 4, out_specs=[spec] * 3,
        out_shape=[_SDS((r, c), F32)] * 3,
        compiler_params=_params(("arbitrary",)),
    )(w, m, v, g)


_ROW = dict(norm_w=0, sb_norm_w=1, ssd_norm_w=2, final_norm_w=3, conv_b=4, dt_bias=5, a_log=6, d_skip=7,
            conv_w=8, sq_err=12, meta_tokens=16)
_SMALL = ("meta_tokens", "norm_w", "conv_w", "conv_b", "dt_bias", "a_log", "d_skip", "sb_norm_w", "ssd_norm_w",
          "final_norm_w")


def _pack_small(sq_err, grads):
    def rowpad(a):
        return jnp.pad(a, ((0, 0), (0, SMALL_COLS - a.shape[1])))

    rows = [rowpad(grads[k]) for k in ("norm_w", "sb_norm_w", "ssd_norm_w", "final_norm_w", "conv_b", "dt_bias", "a_log", "d_skip")]
    rows.append(grads["conv_w"])
    rows.append(rowpad(sq_err))
    rows.append(jnp.zeros((3, SMALL_COLS), F32))
    rows.append(rowpad(grads["meta_tokens"]))
    return jnp.concatenate(rows, axis=0)


def _update_small(gathered, ws, ms, vs):
    names = _SMALL
    n = len(names)

    def body(*refs):
        g_ref = refs[0]
        w_refs, m_refs, v_refs = refs[1:1 + n], refs[1 + n:1 + 2 * n], refs[1 + 2 * n:1 + 3 * n]
        outs = refs[1 + 3 * n:]
        loss_ref = outs[0]
        go, do, mo, vo = outs[1:1 + n], outs[1 + n:1 + 2 * n], outs[1 + 2 * n:1 + 3 * n], outs[1 + 3 * n:1 + 4 * n]
        tot = g_ref[0]
        for d in range(1, N_DEV):
            tot = tot + g_ref[d]
        x, y, _ = _place()
        chip = 2 * x + y
        loss_ref[...] = jnp.broadcast_to(
            0.5 * jnp.sum(tot[_ROW["sq_err"]:_ROW["sq_err"] + 1, 0:D_MODEL], axis=1, keepdims=True) / D_MODEL, (1, 128))
        for idx, nm in enumerate(names):
            r0 = _ROW[nm]
            rows, cols = w_refs[idx].shape
            if nm in ("conv_w", "meta_tokens"):
                g = jnp.zeros((rows, cols), F32)
                for j in range(N_CHIPS):
                    g = g + jnp.where(chip == j, tot[r0:r0 + rows, j * cols:(j + 1) * cols], 0.0)
            else:
                g = tot[r0:r0 + rows, 0:cols]
            delta, m_new, v_new = _adamw(w_refs[idx][...], g, m_refs[idx][...], v_refs[idx][...])
            go[idx][...] = g
            do[idx][...] = delta
            mo[idx][...] = m_new
            vo[idx][...] = v_new

    shapes = [_SDS(ws[nm].shape, F32) for nm in names]
    vm = pl.BlockSpec(memory_space=_VMEM)
    res = pl.pallas_call(
        body, name="update_small",
        in_specs=[vm] * (1 + 3 * n), out_specs=[vm] * (1 + 4 * n),
        out_shape=[_SDS((1, 128), F32)] + shapes * 4,
    )(gathered, *[ws[nm] for nm in names], *[ms[nm] for nm in names], *[vs[nm] for nm in names])
    loss = res[0][0, 0]
    g = dict(zip(names, res[1:1 + n]))
    d = dict(zip(names, res[1 + n:1 + 2 * n]))
    m = dict(zip(names, res[1 + 2 * n:1 + 3 * n]))
    v = dict(zip(names, res[1 + 3 * n:1 + 4 * n]))
    return loss, g, d, m, v


_WEIGHTS = ("meta_tokens", "norm_w", "w_in", "conv_w", "conv_b", "dt_bias", "a_log", "d_skip", "sb_norm_w",
            "ssd_norm_w", "w_out", "final_norm_w")


def kernel(x, meta_tokens, norm_w, w_in, conv_w, conv_b, dt_bias, a_log, d_skip, sb_norm_w, ssd_norm_w, w_out, final_norm_w, loss_target, m_meta_tokens, m_norm_w, m_w_in, m_conv_w, m_conv_b, m_dt_bias, m_a_log, m_d_skip, m_sb_norm_w, m_ssd_norm_w, m_w_out, m_final_norm_w, v_meta_tokens, v_norm_w, v_w_in, v_conv_w, v_conv_b, v_dt_bias, v_a_log, v_d_skip, v_sb_norm_w, v_ssd_norm_w, v_w_out, v_final_norm_w):
    given = dict(meta_tokens=meta_tokens, norm_w=norm_w, w_in=w_in, conv_w=conv_w, conv_b=conv_b, dt_bias=dt_bias,
                 a_log=a_log, d_skip=d_skip, sb_norm_w=sb_norm_w, ssd_norm_w=ssd_norm_w, w_out=w_out,
                 final_norm_w=final_norm_w)
    mom = dict(meta_tokens=m_meta_tokens, norm_w=m_norm_w, w_in=m_w_in, conv_w=m_conv_w, conv_b=m_conv_b,
               dt_bias=m_dt_bias, a_log=m_a_log, d_skip=m_d_skip, sb_norm_w=m_sb_norm_w, ssd_norm_w=m_ssd_norm_w,
               w_out=m_w_out, final_norm_w=m_final_norm_w)
    var = dict(meta_tokens=v_meta_tokens, norm_w=v_norm_w, w_in=v_w_in, conv_w=v_conv_w, conv_b=v_conv_b,
               dt_bias=v_dt_bias, a_log=v_a_log, d_skip=v_d_skip, sb_norm_w=v_sb_norm_w, ssd_norm_w=v_ssd_norm_w,
               w_out=v_w_out, final_norm_w=v_final_norm_w)
    seq = x.shape[1]

    def two_d(a):
        return a.reshape((-1, a.shape[-1])) if a.ndim != 2 else a

    def rows_first(a):
        return jnp.transpose(a, (2, 0, 1)).reshape(W_IN_SHARD, D_MODEL)

    def rows_last(a):
        return jnp.transpose(a.reshape(W_IN_SHARD, 1, D_MODEL), (1, 2, 0))

    w_in_t, m_in_t, v_in_t = rows_first(w_in), rows_first(m_w_in), rows_first(v_w_in)

    g_win, g_wout, g_meta, g_cw = _gather_shards(
        [w_in_t.astype(_MXU), w_out[0].astype(_MXU), meta_tokens, conv_w[0]], 2)
    w_t = jnp.pad(g_win.reshape(D_IN, D_MODEL), ((0, W_ALL - D_IN), (0, 0)))
    w_out_full = g_wout.reshape(2 * D_MODEL, D_MODEL)
    meta_full = jnp.swapaxes(g_meta, 0, 1).reshape(N_META, D_MODEL)
    conv_w_full = jnp.swapaxes(g_cw, 0, 1).reshape(4, XBC_W)

    sq_err, g_x, grads = _device_grads(
        x.reshape(seq, D_MODEL), loss_target.reshape(seq, D_MODEL), meta_full, norm_w, w_t, conv_w_full,
        conv_b, dt_bias, a_log, d_skip, sb_norm_w, ssd_norm_w, w_out_full, final_norm_w.reshape(1, D_MODEL))

    core = lax.axis_index("c").astype(jnp.int32).reshape(1)
    slab_in = grads["w_in"]
    slab_out = grads["w_out"].reshape(N_CHIPS, W_OUT_SHARD, D_MODEL).astype(_MXU)
    sib_in, sib_out = _swap_halves([slab_in, slab_out])
    chip_in = _add_halves(slab_in, sib_in, core, "chip_sum_w_in")
    chip_out = _add_halves(slab_out, sib_out, core, "chip_sum_w_out")
    got_in, got_out = _scatter_slabs([chip_in, chip_out])
    chip = 2 * lax.axis_index("x") + lax.axis_index("y")

    def with_own(got, sent):
        own = lax.dynamic_slice(sent, (chip, 0, 0), (1,) + sent.shape[1:])
        return lax.dynamic_update_slice(got, own, (chip, 0, 0))

    g_in, g_out = _join_halves([_sum_slabs(with_own(got_in, chip_in), core, "sum_w_in", transposed=True),
                                _sum_slabs(with_own(got_out, chip_out), core, "sum_w_out")], (True, False))
    g_in = lax.dynamic_slice(g_in, (4 * chip, 0), (W_IN_SHARD, D_MODEL))
    big = dict(w_in=tuple(rows_last(a) for a in (g_in,) + tuple(_update_big(w_in_t, m_in_t, v_in_t, g_in, "update_w_in"))),
               w_out=(g_out,) + tuple(_update_big(w_out[0], m_w_out[0], v_w_out[0], g_out, "update_w_out")))

    gathered = _gather_small(_pack_small(sq_err, grads))
    loss, sg, sd, sm, sv = _update_small(
        gathered, {k: two_d(given[k]) for k in _SMALL}, {k: two_d(mom[k]) for k in _SMALL},
        {k: two_d(var[k]) for k in _SMALL})

    out = {}
    for idx, group in enumerate((sg, sd, sm, sv)):
        for k in _SMALL:
            out[(idx, k)] = group[k].reshape(given[k].shape)
        for k in ("w_in", "w_out"):
            out[(idx, k)] = big[k][idx].reshape(given[k].shape)
    return (loss, g_x.reshape(x.shape), *[out[(idx, k)] for idx in range(4) for k in _WEIGHTS])
```

```python
import functools
import math

import jax
import jax.numpy as jnp
from jax import lax
from jax.experimental import pallas as pl
from jax.experimental.pallas import tpu as pltpu

F32 = jnp.float32
_MXU = jnp.bfloat16

D_MODEL = 1024
N_META = 16
PAD = 112
OFF = PAD + N_META
TM = 128
CHUNK = 64
SB_W = 1024
SSD_W = 1024
N_HEADS = 16
HEAD = 64
N_GROUPS = 2
N_STATE = 128
XBC_W = SSD_W + 2 * N_GROUPS * N_STATE
N_MAIN = 4 * SB_W + SSD_W + XBC_W
QKV_W = 3 * SB_W
REST_W = N_MAIN - QKV_W
COL_GATE = 3 * SB_W
COL_Z = 4 * SB_W
COL_XBC = 5 * SB_W
D_IN = N_MAIN + N_HEADS
W_ALL = N_MAIN + 128
WIN_STEP = 1664
WIN_W = 1792
EPS = 1e-5
N_CHIPS = 4
W_IN_SHARD = D_IN // N_CHIPS
W_OUT_SHARD = 2 * D_MODEL // N_CHIPS

ADAM_LR = 0.001
ADAM_B1 = 0.9
ADAM_B2 = 0.999
ADAM_EPS = 1e-08
ADAM_WD = 0.01
ADAM_STEP = 10

_SDS = jax.ShapeDtypeStruct
_NT = (((1,), (1,)), ((), ()))
_TN = (((0,), (0,)), ((), ()))
_VMEM = pltpu.VMEM


def _params(sem=None, vmem_mb=None):
    kw = {}
    if sem is not None:
        kw["dimension_semantics"] = sem
    if vmem_mb is not None:
        kw["vmem_limit_bytes"] = vmem_mb * 1024 * 1024
    return pltpu.CompilerParams(**kw)


def _mm(a, b):
    return jnp.dot(a.astype(_MXU), b.astype(_MXU), preferred_element_type=F32)


def _mm_nt(a, b):
    return lax.dot_general(a.astype(_MXU), b.astype(_MXU), _NT, preferred_element_type=F32)


def _mm_tn(a, b):
    return lax.dot_general(a.astype(_MXU), b.astype(_MXU), _TN, preferred_element_type=F32)


def _split(x, parts):
    out = []
    r = x
    for _ in range(parts):
        p = r.astype(_MXU)
        out.append(p)
        r = r - p.astype(F32)
    return out


def _sel_right(x, m01, parts=3):
    acc = None
    for p in _split(x, parts):
        t = jnp.dot(p, m01, preferred_element_type=F32)
        acc = t if acc is None else acc + t
    return acc


def _sel_left(m01, x, parts=3):
    acc = None
    for p in _split(x, parts):
        t = jnp.dot(m01, p, preferred_element_type=F32)
        acc = t if acc is None else acc + t
    return acc


def _iota(shape, axis):
    return lax.broadcasted_iota(jnp.int32, shape, axis)


def _sigmoid(x):
    return 1.0 / (1.0 + jnp.exp(-x))


def _prep(x2d, meta_full, norm_w):
    seq = x2d.shape[0]
    lp = seq + OFF
    nb = lp // TM

    def body(x_ref, meta_ref, w_ref, h0_ref, u_ref, ut_ref):
        i = pl.program_id(0)

        @pl.when(i == 0)
        def _():
            h0_ref[...] = jnp.concatenate([jnp.zeros((PAD, D_MODEL), F32), meta_ref[...]], axis=0)

        @pl.when(i > 0)
        def _():
            h0_ref[...] = x_ref[...]

        h = h0_ref[...]
        rs = lax.rsqrt(jnp.mean(h * h, axis=-1, keepdims=True) + EPS)
        u = (h * rs * w_ref[...]).astype(_MXU)
        u_ref[...] = u
        ut_ref[...] = u.T

    return pl.pallas_call(
        body, name="prep", grid=(nb,),
        in_specs=[pl.BlockSpec((TM, D_MODEL), lambda i: (jnp.maximum(i - 1, 0), 0)),
                  pl.BlockSpec((N_META, D_MODEL), lambda i: (0, 0)),
                  pl.BlockSpec((1, D_MODEL), lambda i: (0, 0))],
        out_specs=[pl.BlockSpec((TM, D_MODEL), lambda i: (i, 0)),
                   pl.BlockSpec((TM, D_MODEL), lambda i: (i, 0)),
                   pl.BlockSpec((D_MODEL, TM), lambda i: (0, i))],
        out_shape=[_SDS((lp, D_MODEL), F32), _SDS((lp, D_MODEL), _MXU), _SDS((D_MODEL, lp), _MXU)],
        compiler_params=_params(("arbitrary",)),
    )(x2d, meta_full, norm_w)


def _inproj(u, w_t):
    lp = u.shape[0]
    tn = 512

    nq = QKV_W // tn

    def body(u_ref, w_ref, wdt_ref, qkv_ref, rest_ref, odt_ref):
        j = pl.program_id(0)
        res = lax.dot_general(u_ref[...], w_ref[...], _NT, preferred_element_type=F32)

        @pl.when(j < nq)
        def _():
            qkv_ref[...] = res.astype(qkv_ref.dtype)

        @pl.when(j >= nq)
        def _():
            rest_ref[...] = res

        @pl.when(j == 0)
        def _():
            odt_ref[...] = lax.dot_general(u_ref[...], wdt_ref[...], _NT, preferred_element_type=F32)

    return pl.pallas_call(
        body, name="inproj", grid=(N_MAIN // tn,),
        in_specs=[pl.BlockSpec((lp, D_MODEL), lambda j: (0, 0)),
                  pl.BlockSpec((tn, D_MODEL), lambda j: (j, 0)),
                  pl.BlockSpec((128, D_MODEL), lambda j: (N_MAIN // 128, 0))],
        out_specs=[pl.BlockSpec((lp, tn), lambda j: (0, jnp.minimum(j, nq - 1))),
                   pl.BlockSpec((lp, tn), lambda j: (0, jnp.maximum(j - nq, 0))),
                   pl.BlockSpec((lp, 128), lambda j: (0, 0))],
        out_shape=[_SDS((lp, QKV_W), _MXU), _SDS((lp, REST_W), F32), _SDS((lp, 128), F32)],
        compiler_params=_params(("arbitrary",), 48),
    )(u, w_t, w_t)


SB_WINDOW = 3
SB_DEAD = -104.0


def _sb_logs(qh, kwin):
    z = lax.dot_general(qh, kwin, _NT, preferred_element_type=F32)
    e = jnp.exp(-jnp.abs(z))
    l1p = jnp.log(1.0 + e)
    lk_full = -(jnp.maximum(z, 0.0) + l1p)
    ls = jnp.minimum(z, 0.0) - l1p
    return z, e, ls, lk_full


def _blk(a, b):
    return a[:, b * TM:(b + 1) * TM]


def _stacked_sel(blocks, m01):
    n = len(blocks)
    pieces = [_split(b, 2) for b in blocks]
    stacked = jnp.concatenate([p[0] for p in pieces] + [p[1] for p in pieces], axis=0)
    res = jnp.dot(stacked, m01, preferred_element_type=F32)
    return [res[j * TM:(j + 1) * TM] + res[(n + j) * TM:(n + j + 1) * TM] for j in range(n)]


def _sb_weights(ls, lk_full, run, last_mask, upper, n):
    lk = [_blk(lk_full, b) for b in range(n)]
    lk[n - 1] = jnp.where(last_mask, lk[n - 1], 0.0)
    aft = _stacked_sel(lk, upper)
    w = [None] * n
    for b in range(n - 1, -1, -1):
        wb = jnp.exp(_blk(ls, b) + aft[b] + run)
        w[b] = jnp.where(last_mask, wb, 0.0) if b == n - 1 else wb
        run = run + jnp.sum(lk[b], axis=1, keepdims=True)
    return w, run


def _sb_fwd(qkv):
    lp = qkv.shape[0]
    nb = lp // TM

    def body(q_ref, k_ref, v_ref, o_ref, olo_ref, acc, run_scr):
        i = pl.program_id(1)
        lane = _iota((TM, TM), 1)
        row = _iota((TM, TM), 0)
        head0 = lane < HEAD
        upper = (row > lane).astype(_MXU)
        strict = lane < row
        q = q_ref[...] * (1.0 / math.sqrt(HEAD))
        qh = (jnp.where(head0, q, 0.0).astype(_MXU), jnp.where(head0, 0.0, q).astype(_MXU))

        def key_set(first, n, last_mask):
            off = pl.multiple_of(first * TM, TM)
            kwin = k_ref[pl.ds(off, n * TM), :].astype(_MXU)
            vwin = v_ref[pl.ds(off, n * TM), :].astype(_MXU)
            alive = None
            for hh in range(2):
                run = run_scr[hh][:, 0:1]
                _, _, ls, lk_full = _sb_logs(qh[hh], kwin)
                w, run = _sb_weights(ls, lk_full, run, last_mask, upper, n)
                pieces = [_split(wb, 2) for wb in w]
                stacked = jnp.concatenate(
                    [jnp.concatenate([p[0] for p in pieces], axis=1), jnp.concatenate([p[1] for p in pieces], axis=1)], axis=0)
                res = jnp.dot(stacked, vwin, preferred_element_type=F32)
                acc[hh] += res[0:TM]
                acc[2 + hh] += res[TM:2 * TM]
                run_scr[hh] = jnp.broadcast_to(run, (TM, TM))
                top = jnp.max(run)
                alive = top if alive is None else jnp.maximum(alive, top)
            return (alive > SB_DEAD).astype(jnp.int32)

        acc[...] = jnp.zeros_like(acc)
        run_scr[...] = jnp.zeros_like(run_scr)

        @pl.when(i >= SB_WINDOW - 1)
        def _():
            key_set(i - (SB_WINDOW - 1), SB_WINDOW, strict)

        start = jnp.where(i >= SB_WINDOW - 1, i - SB_WINDOW, i)
        alive0 = (jnp.max(run_scr[...]) > SB_DEAD).astype(jnp.int32)

        def cond(c):
            return jnp.logical_and(c[0] >= 0, c[1] > 0)

        def step(c):
            kb = c[0]
            return kb - 1, key_set(kb, 1, jnp.logical_or(strict, kb < i))

        lax.while_loop(cond, step, (start, alive0))
        o_ref[...] = jnp.where(head0, acc[0], acc[1])
        olo_ref[...] = jnp.where(head0, acc[2], acc[3])

    npair = SB_W // TM
    blk = pl.BlockSpec((TM, TM), lambda p, i: (i, p))
    return pl.pallas_call(
        body, name="sb_fwd", grid=(npair, nb),
        in_specs=[blk,
                  pl.BlockSpec((lp, TM), lambda p, i: (0, npair + p)),
                  pl.BlockSpec((lp, TM), lambda p, i: (0, 2 * npair + p))],
        out_specs=[blk, blk],
        out_shape=[_SDS((lp, SB_W), F32), _SDS((lp, SB_W), F32)],
        scratch_shapes=[pltpu.VMEM((4, TM, TM), F32), pltpu.VMEM((2, TM, TM), F32)],
        compiler_params=_params(("arbitrary", "arbitrary")),
    )(qkv, qkv, qkv)


def _sb_bwd(qkv, o_sb, o_lo, d_o, d_proj):
    lp = qkv.shape[0]
    nb = lp // TM
    npair = SB_W // TM
    scale = 1.0 / math.sqrt(HEAD)

    def body(q_ref, k_ref, v_ref, o_ref, olo_ref, do_ref, dproj_in, dproj_ref,
             dq_all, dk_ref, dv_ref, stage, sems, dq_acc, run_scr, gsum_scr):
        p = pl.program_id(0)
        i = pl.program_id(1)

        @pl.when(i == 0)
        def _():
            dk_ref[...] = jnp.zeros_like(dk_ref)
            dv_ref[...] = jnp.zeros_like(dv_ref)

        lane = _iota((TM, TM), 1)
        row = _iota((TM, TM), 0)
        head0 = lane < HEAD
        hmask = (head0, jnp.logical_not(head0))
        upper = (row > lane).astype(_MXU)
        lower_incl = (row >= lane).astype(_MXU)
        strict = lane < row
        q = q_ref[...] * scale
        do = do_ref[...]
        prod = do.astype(_MXU).astype(F32) * (o_ref[...] + olo_ref[...])
        qh = tuple(jnp.where(m, q, 0.0).astype(_MXU) for m in hmask)
        doh = tuple(jnp.where(m, do, 0.0).astype(_MXU) for m in hmask)
        gtot = tuple(jnp.sum(jnp.where(m, prod, 0.0), axis=1, keepdims=True) for m in hmask)

        def key_set(first, n, last_mask):
            off = pl.multiple_of(first * TM, TM)
            kf = k_ref[pl.ds(off, n * TM), :]
            kwin = kf.astype(_MXU)
            vwin = v_ref[pl.ds(off, n * TM), :].astype(_MXU)
            dk_win = None
            alive = None
            for hh in range(2):
                run = run_scr[hh][:, 0:1]
                gsum = gsum_scr[hh][:, 0:1]
                z, e, ls, lk_full = _sb_logs(qh[hh], kwin)
                w, run = _sb_weights(ls, lk_full, run, last_mask, upper, n)
                r = 1.0 / (1.0 + e)
                er = e * r
                pos = z >= 0.0
                beta = jnp.where(pos, r, er)
                one_m_beta = jnp.where(pos, er, r)
                dw = lax.dot_general(doh[hh], vwin, _NT, preferred_element_type=F32)
                g = [_blk(dw, b) * w[b] for b in range(n)]
                suffix = _stacked_sel(g, lower_incl)
                dz = [None] * n
                for b in range(n - 1, -1, -1):
                    prefix = gtot[hh] - gsum - suffix[b]
                    d = g[b] * _blk(one_m_beta, b) - _blk(beta, b) * prefix
                    dz[b] = (jnp.where(last_mask, d, 0.0) if b == n - 1 else d).astype(_MXU)
                    gsum = gsum + jnp.sum(g[b], axis=1, keepdims=True)
                dzw = jnp.concatenate(dz, axis=1)
                ww = jnp.concatenate([wb.astype(_MXU) for wb in w], axis=1)
                kh = jnp.where(hmask[hh][0:1, :], kf, 0.0).astype(_MXU)
                dq_acc[...] += jnp.dot(dzw, kh, preferred_element_type=F32)
                dk_h = lax.dot_general(dzw, qh[hh], _TN, preferred_element_type=F32)
                dv_h = lax.dot_general(ww, doh[hh], _TN, preferred_element_type=F32)
                dk_win = (dk_h, dv_h) if dk_win is None else (dk_win[0] + dk_h, dk_win[1] + dv_h)
                run_scr[hh] = jnp.broadcast_to(run, (TM, TM))
                gsum_scr[hh] = jnp.broadcast_to(gsum, (TM, TM))
                top = jnp.max(run)
                alive = top if alive is None else jnp.maximum(alive, top)
            dk_ref[pl.ds(off, n * TM), :] += dk_win[0]
            dv_ref[pl.ds(off, n * TM), :] += dk_win[1]
            return (alive > SB_DEAD).astype(jnp.int32)

        dq_acc[...] = jnp.zeros_like(dq_acc)
        run_scr[...] = jnp.zeros_like(run_scr)
        gsum_scr[...] = jnp.zeros_like(gsum_scr)

        @pl.when(i >= SB_WINDOW - 1)
        def _():
            key_set(i - (SB_WINDOW - 1), SB_WINDOW, strict)

        start = jnp.where(i >= SB_WINDOW - 1, i - SB_WINDOW, i)
        alive0 = (jnp.max(run_scr[...]) > SB_DEAD).astype(jnp.int32)

        def cond(c):
            return jnp.logical_and(c[0] >= 0, c[1] > 0)

        def step(c):
            kb = c[0]
            return kb - 1, key_set(kb, 1, jnp.logical_or(strict, kb < i))

        lax.while_loop(cond, step, (start, alive0))
        dq_all[pl.ds(pl.multiple_of(i * TM, TM), TM), :] = dq_acc[...] * scale

        @pl.when(i == nb - 1)
        def _():
            copies = []
            for s, src in enumerate((dq_all, dk_ref, dv_ref)):
                stage[s] = src[...].astype(_MXU)
                col = pl.multiple_of((s * npair + p) * TM, TM)
                copies.append(pltpu.make_async_copy(stage.at[s], dproj_ref.at[:, pl.ds(col, TM)], sems.at[s]))
                copies[-1].start()
            for cp in copies:
                cp.wait()

    blk = pl.BlockSpec((TM, TM), lambda p, i: (i, p))
    return pl.pallas_call(
        body, name="sb_bwd", grid=(npair, nb),
        in_specs=[blk,
                  pl.BlockSpec((lp, TM), lambda p, i: (0, npair + p)),
                  pl.BlockSpec((lp, TM), lambda p, i: (0, 2 * npair + p)),
                  blk, blk, blk, pl.BlockSpec(memory_space=pl.ANY)],
        out_specs=pl.BlockSpec(memory_space=pl.ANY),
        out_shape=_SDS(d_proj.shape, d_proj.dtype),
        input_output_aliases={6: 0},
        scratch_shapes=[pltpu.VMEM((lp, TM), F32), pltpu.VMEM((lp, TM), F32), pltpu.VMEM((lp, TM), F32),
                        pltpu.VMEM((3, lp, TM), _MXU), pltpu.SemaphoreType.DMA((3,)),
                        pltpu.VMEM((TM, TM), F32), pltpu.VMEM((2, TM, TM), F32), pltpu.VMEM((2, TM, TM), F32)],
        compiler_params=_params(("arbitrary", "arbitrary")),
    )(qkv, qkv, qkv, o_sb, o_lo, d_o, d_proj)


def _conv_pre(x_ref, w_ref, b_ref, lp):
    n = lp - 8
    w = w_ref[...]
    pre = (x_ref[pl.ds(5, n), :] * w[0:1, :] + x_ref[pl.ds(6, n), :] * w[1:2, :]
           + x_ref[pl.ds(7, n), :] * w[2:3, :] + x_ref[pl.ds(8, n), :] * w[3:4, :]) + b_ref[...]
    live = (_iota((n, 128), 0) + 8) >= PAD
    return pre, live


def _conv_fwd(proj, dt_raw, conv_w, conv_b, dt_bias128):
    lp = proj.shape[0]
    nblk = XBC_W // 128
    c0 = (COL_XBC - QKV_W) // 128

    def body(x_ref, w_ref, b_ref, dtr_ref, dtb_ref, o_ref, dt_ref):
        pre, live = _conv_pre(x_ref, w_ref, b_ref, lp)
        act = pre * _sigmoid(pre)
        o_ref[pl.ds(0, 8), :] = jnp.zeros((8, 128), F32)
        o_ref[pl.ds(8, lp - 8), :] = jnp.where(live, act, 0.0)

        @pl.when(pl.program_id(0) == 0)
        def _():
            s = dtr_ref[...] + dtb_ref[...]
            sp = jnp.maximum(s, 0.0) + jnp.log(1.0 + jnp.exp(-jnp.abs(s)))
            dt_ref[...] = jnp.where(_iota((lp, 128), 0) >= PAD, sp, 0.0)

    return pl.pallas_call(
        body, name="conv_fwd", grid=(nblk,),
        in_specs=[pl.BlockSpec((lp, 128), lambda j: (0, c0 + j)),
                  pl.BlockSpec((4, 128), lambda j: (0, j)),
                  pl.BlockSpec((1, 128), lambda j: (0, j)),
                  pl.BlockSpec((lp, 128), lambda j: (0, 0)),
                  pl.BlockSpec((1, 128), lambda j: (0, 0))],
        out_specs=[pl.BlockSpec((lp, 128), lambda j: (0, j)),
                   pl.BlockSpec((lp, 128), lambda j: (0, 0))],
        out_shape=[_SDS((lp, XBC_W), F32), _SDS((lp, 128), F32)],
        compiler_params=_params(("arbitrary",)),
    )(proj, conv_w, conv_b, dt_raw, dt_bias128)


def _conv_bwd(proj, dt_raw, conv_w, conv_b, dt_bias128, d_xbc, d_dt128, d_proj):
    lp = proj.shape[0]
    nblk = XBC_W // 128
    c0 = COL_XBC // 128
    c0_in = (COL_XBC - QKV_W) // 128
    n = lp - 8
    last = nblk - 1

    def body(x_ref, w_ref, b_ref, dtr_ref, dtb_ref, dy_ref, ddt_ref, dproj_in,
             dx_ref, gw_ref, gb_ref, gdtb_ref, scr):
        j = pl.program_id(0)

        @pl.when(j < nblk)
        def _():
            pre, live = _conv_pre(x_ref, w_ref, b_ref, lp)
            sg = _sigmoid(pre)
            dpre = jnp.where(live, dy_ref[pl.ds(8, n), :] * (sg * (1.0 + pre * (1.0 - sg))), 0.0)
            gb_ref[...] = jnp.sum(dpre, axis=0, keepdims=True)
            gw_ref[...] = jnp.concatenate(
                [jnp.sum(dpre * x_ref[pl.ds(5 + k, n), :], axis=0, keepdims=True) for k in range(4)], axis=0)
            scr[pl.ds(0, 8), :] = jnp.zeros((8, 128), F32)
            scr[pl.ds(8, n), :] = dpre
            scr[pl.ds(lp, 8), :] = jnp.zeros((8, 128), F32)
            w = w_ref[...]
            dx_ref[...] = (scr[pl.ds(0, lp), :] * w[3:4, :] + scr[pl.ds(1, lp), :] * w[2:3, :]
                           + scr[pl.ds(2, lp), :] * w[1:2, :] + scr[pl.ds(3, lp), :] * w[0:1, :]).astype(dx_ref.dtype)

        @pl.when(j == nblk)
        def _():
            s = dtr_ref[...] + dtb_ref[...]
            d = jnp.where(_iota((lp, 128), 0) >= PAD, ddt_ref[...] * _sigmoid(s), 0.0)
            dx_ref[...] = d.astype(dx_ref.dtype)
            gdtb_ref[...] = jnp.sum(d, axis=0, keepdims=True)

    clamp = lambda j: (0, jnp.minimum(j, last))
    full128 = pl.BlockSpec((lp, 128), lambda j: (0, 0))
    return pl.pallas_call(
        body, name="conv_bwd", grid=(nblk + 1,),
        in_specs=[pl.BlockSpec((lp, 128), lambda j: (0, c0_in + jnp.minimum(j, last))),
                  pl.BlockSpec((4, 128), clamp),
                  pl.BlockSpec((1, 128), clamp),
                  full128, pl.BlockSpec((1, 128), lambda j: (0, 0)),
                  pl.BlockSpec((lp, 128), clamp), full128, pl.BlockSpec(memory_space=pl.ANY)],
        out_specs=[pl.BlockSpec((lp, 128), lambda j: (0, c0 + j)), pl.BlockSpec((4, 128), clamp),
                   pl.BlockSpec((1, 128), clamp), pl.BlockSpec((1, 128), lambda j: (0, 0))],
        out_shape=[_SDS(d_proj.shape, d_proj.dtype), _SDS((4, XBC_W), F32), _SDS((1, XBC_W), F32), _SDS((1, 128), F32)],
        input_output_aliases={7: 0},
        scratch_shapes=[pltpu.VMEM((lp + 8, 128), F32)],
        compiler_params=_params(("arbitrary",)),
    )(proj, conv_w, conv_b, dt_raw, dt_bias128, d_xbc, d_dt128, d_proj)


def _ssd_pieces(dt, dt_t, a, a_t):
    r64 = _iota((CHUNK, CHUNK), 0)
    c64 = _iota((CHUNK, CHUNK), 1)
    tril = c64 <= r64
    tril01 = tril.astype(_MXU)
    triu01 = (r64 <= c64).astype(_MXU)
    expand = (lax.shift_right_logical(_iota((N_HEADS, SSD_W), 1), 6) == _iota((N_HEADS, SSD_W), 0)).astype(_MXU)
    acum = _sel_left(tril01, dt * a)
    acum_t = _sel_right(dt_t * a_t, triu01)
    ax = _sel_right(acum, expand)
    dtx = _sel_right(dt, expand)
    return tril, expand, acum, acum_t, ax, dtx


def _seg_matrix():
    return (lax.shift_right_logical(_iota((SSD_W, N_HEADS), 0), 6) == _iota((SSD_W, N_HEADS), 1)).astype(_MXU)


def _head_decay(ax, acum_t, h, tril):
    col = ax[:, h * HEAD:(h + 1) * HEAD]
    rowv = acum_t[h:h + 1, :]
    return jnp.where(tril, jnp.exp(jnp.minimum(col - rowv, 0.0)), 0.0)


def _ssd_fwd(xbc, dt_c, dt_tc, a, a_t, dskip_x):
    lp = xbc.shape[0]
    nc = lp // CHUNK
    gw = SSD_W // N_GROUPS
    hpg = N_HEADS // N_GROUPS

    def body(x_ref, dt_ref, dtt_ref, a_ref, at_ref, d_ref, y_ref, st_ref, state):
        c = pl.program_id(0)

        @pl.when(c == 0)
        def _():
            state[...] = jnp.zeros_like(state)

        st_ref[0] = state[...]
        tril, _, _, acum_t, ax, dtx = _ssd_pieces(dt_ref[0], dtt_ref[0], a_ref[...], at_ref[...])
        x = x_ref[:, 0:SSD_W]
        xdt = x * dtx
        ea = jnp.exp(ax)
        aex = ax[CHUNK - 1:CHUNK, :]
        wd = jnp.exp(aex - ax)
        eae = jnp.exp(aex)
        xw = xdt * wd
        y_ref[...] = x * d_ref[...]
        for g in range(N_GROUPS):
            gs = slice(g * gw, (g + 1) * gw)
            rs = slice(g * N_STATE, (g + 1) * N_STATE)
            bg = x_ref[:, SSD_W + g * N_STATE:SSD_W + (g + 1) * N_STATE]
            cg = x_ref[:, SSD_W + N_GROUPS * N_STATE + g * N_STATE:SSD_W + N_GROUPS * N_STATE + (g + 1) * N_STATE]
            sg = state[rs, :]
            cb = _mm_nt(cg, bg)
            y_ref[:, gs] += _mm(cg, sg) * ea[:, gs]
            for r in range(hpg):
                h = g * hpg + r
                hs = slice(h * HEAD, (h + 1) * HEAD)
                m = cb * _head_decay(ax, acum_t, h, tril)
                y_ref[:, hs] += _mm(m, xdt[:, hs])
            state[rs, :] = sg * eae[:, gs] + _mm_tn(bg, xw[:, gs])

    return pl.pallas_call(
        body, name="ssd_fwd", grid=(nc,),
        in_specs=[pl.BlockSpec((CHUNK, XBC_W), lambda c: (c, 0)),
                  pl.BlockSpec((1, CHUNK, N_HEADS), lambda c: (c, 0, 0)),
                  pl.BlockSpec((1, N_HEADS, CHUNK), lambda c: (c, 0, 0)),
                  pl.BlockSpec((1, N_HEADS), lambda c: (0, 0)),
                  pl.BlockSpec((N_HEADS, 1), lambda c: (0, 0)),
                  pl.BlockSpec((1, SSD_W), lambda c: (0, 0))],
        out_specs=[pl.BlockSpec((CHUNK, SSD_W), lambda c: (c, 0)),
                   pl.BlockSpec((1, N_GROUPS * N_STATE, gw), lambda c: (c, 0, 0))],
        out_shape=[_SDS((lp, SSD_W), F32), _SDS((nc, N_GROUPS * N_STATE, gw), F32)],
        scratch_shapes=[pltpu.VMEM((N_GROUPS * N_STATE, gw), F32)],
        compiler_params=_params(("arbitrary",)),
    )(xbc, dt_c, dt_tc, a, a_t, dskip_x)


def _ssd_bwd(xbc, dt_c, dt_tc, a, a_t, dskip_x, states, d_y):
    lp = xbc.shape[0]
    nc = lp // CHUNK
    gw = SSD_W // N_GROUPS
    hpg = N_HEADS // N_GROUPS

    def body(x_ref, dt_ref, dtt_ref, a_ref, at_ref, d_ref, st_ref, dy_ref,
             dx_ref, ddta_ref, ddtb_ref, ga1_ref, ga2_ref, gd_ref, dstate, dxdt_scr, z_scr, yoff_scr, sds_scr):
        c = pl.program_id(0)

        @pl.when(c == 0)
        def _():
            dstate[...] = jnp.zeros_like(dstate)
            ga1_ref[...] = jnp.zeros_like(ga1_ref)
            ga2_ref[...] = jnp.zeros_like(ga2_ref)
            gd_ref[...] = jnp.zeros_like(gd_ref)

        dt = dt_ref[0]
        dt_t = dtt_ref[0]
        a = a_ref[...]
        a_t = at_ref[...]
        tril, _, acum, acum_t, ax, dtx = _ssd_pieces(dt, dt_t, a, a_t)
        seg = _seg_matrix()
        x = x_ref[:, 0:SSD_W]
        dy = dy_ref[...]
        xdt = x * dtx
        ea = jnp.exp(ax)
        aex = ax[CHUNK - 1:CHUNK, :]
        wd = jnp.exp(aex - ax)
        eae = jnp.exp(aex)
        xw = xdt * wd
        edy = ea * dy
        lane16 = _iota((CHUNK, N_HEADS), 1)
        row16 = _iota((N_HEADS, CHUNK), 0)
        da_col = jnp.zeros((CHUNK, N_HEADS), F32)
        da_row = jnp.zeros((N_HEADS, CHUNK), F32)
        for g in range(N_GROUPS):
            gs = slice(g * gw, (g + 1) * gw)
            rs = slice(g * N_STATE, (g + 1) * N_STATE)
            bcol = slice(SSD_W + g * N_STATE, SSD_W + (g + 1) * N_STATE)
            ccol = slice(SSD_W + N_GROUPS * N_STATE + g * N_STATE, SSD_W + N_GROUPS * N_STATE + (g + 1) * N_STATE)
            bg = x_ref[:, bcol]
            cg = x_ref[:, ccol]
            sg = st_ref[0, rs, :]
            dsn = dstate[rs, :]
            cb = _mm_nt(cg, bg)
            z_scr[:, gs] = _mm(bg, dsn)
            yoff_scr[:, gs] = _mm(cg, sg) * ea[:, gs]
            sds_scr[:, gs] = jnp.broadcast_to(jnp.sum(dsn * sg, axis=0, keepdims=True), (8, gw))
            dcb = jnp.zeros((CHUNK, CHUNK), F32)
            for r in range(hpg):
                h = g * hpg + r
                hs = slice(h * HEAD, (h + 1) * HEAD)
                dec = _head_decay(ax, acum_t, h, tril)
                m = cb * dec
                t1 = _mm_nt(dy[:, hs], xdt[:, hs])
                dcb = dcb + dec * t1
                tm = m * t1
                da_col = da_col + jnp.where(lane16 == h, jnp.sum(tm, axis=1, keepdims=True), 0.0)
                da_row = da_row - jnp.where(row16 == h, jnp.sum(tm, axis=0, keepdims=True), 0.0)
                dxdt_scr[:, hs] = _mm_tn(m, dy[:, hs])
            dx_ref[:, ccol] = _mm(dcb, bg) + _mm_nt(edy[:, gs], sg)
            dx_ref[:, bcol] = _mm_tn(dcb, cg) + _mm_nt(xw[:, gs], dsn)
            dstate[rs, :] = eae[:, gs] * dsn + _mm_tn(cg, edy[:, gs])
        zf = z_scr[...]
        dxdt = dxdt_scr[...] + wd * zf
        t3 = _sel_right(xw * zf, seg)
        da_col = da_col + _sel_right(dy * yoff_scr[...], seg) - t3
        aend = acum[CHUNK - 1:CHUNK, :]
        sd = _sel_right(sds_scr[...], seg)[0:1, :] * jnp.exp(aend)
        last = jnp.sum(t3, axis=0, keepdims=True) + sd
        da_col = da_col + jnp.where(_iota((CHUNK, N_HEADS), 0) == CHUNK - 1, last, 0.0)
        r64 = _iota((CHUNK, CHUNK), 0)
        c64 = _iota((CHUNK, CHUNK), 1)
        ddta1 = _sel_left((c64 >= r64).astype(_MXU), da_col)
        ddta2 = _sel_right(da_row, (r64 >= c64).astype(_MXU))
        ddta_ref[0] = a * ddta1 + _sel_right(dxdt * x, seg)
        ddtb_ref[0] = a_t * ddta2
        ga1_ref[...] += jnp.sum(dt * ddta1, axis=0, keepdims=True)
        ga2_ref[...] += jnp.sum(dt_t * ddta2, axis=1, keepdims=True)
        dx_ref[:, 0:SSD_W] = dxdt * dtx + d_ref[...] * dy
        gd_ref[...] += jnp.sum(dy * x, axis=0, keepdims=True)

    rev = lambda c: (nc - 1 - c, 0)
    rev3 = lambda c: (nc - 1 - c, 0, 0)
    return pl.pallas_call(
        body, name="ssd_bwd", grid=(nc,),
        in_specs=[pl.BlockSpec((CHUNK, XBC_W), rev),
                  pl.BlockSpec((1, CHUNK, N_HEADS), rev3),
                  pl.BlockSpec((1, N_HEADS, CHUNK), rev3),
                  pl.BlockSpec((1, N_HEADS), lambda c: (0, 0)),
                  pl.BlockSpec((N_HEADS, 1), lambda c: (0, 0)),
                  pl.BlockSpec((1, SSD_W), lambda c: (0, 0)),
                  pl.BlockSpec((1, N_GROUPS * N_STATE, gw), rev3),
                  pl.BlockSpec((CHUNK, SSD_W), rev)],
        out_specs=[pl.BlockSpec((CHUNK, XBC_W), rev),
                   pl.BlockSpec((1, CHUNK, N_HEADS), rev3),
                   pl.BlockSpec((1, N_HEADS, CHUNK), rev3),
                   pl.BlockSpec((1, N_HEADS), lambda c: (0, 0)),
                   pl.BlockSpec((N_HEADS, 1), lambda c: (0, 0)),
                   pl.BlockSpec((1, SSD_W), lambda c: (0, 0))],
        out_shape=[_SDS((lp, XBC_W), F32), _SDS((nc, CHUNK, N_HEADS), F32), _SDS((nc, N_HEADS, CHUNK), F32),
                   _SDS((1, N_HEADS), F32), _SDS((N_HEADS, 1), F32), _SDS((1, SSD_W), F32)],
        scratch_shapes=[pltpu.VMEM((N_GROUPS * N_STATE, gw), F32), pltpu.VMEM((CHUNK, SSD_W), F32),
                        pltpu.VMEM((CHUNK, SSD_W), F32), pltpu.VMEM((CHUNK, SSD_W), F32),
                        pltpu.VMEM((8, SSD_W), F32)],
        compiler_params=_params(("arbitrary",)),
    )(xbc, dt_c, dt_tc, a, a_t, dskip_x, states, d_y)


def _gated_norm(o, gate, w):
    sg = _sigmoid(gate)
    p = o * (gate * sg)
    rs = lax.rsqrt(jnp.mean(p * p, axis=-1, keepdims=True) + EPS)
    n = p * rs
    return sg, rs, n, n * w


def _tail_fwd(o_sb, o_ssd, proj, h0, target, w_out, sb_w, ssd_w, fin_w):
    lp = o_sb.shape[0]
    nb = lp // TM
    row = lambda i: (i, 0)
    one = lambda i: (0, 0)

    def body(osb_ref, gate_ref, ossd_ref, z_ref, h0_ref, tgt_ref, wo_ref, sbw_ref, ssdw_ref, fw_ref,
             dh1_ref, loss_ref, gfw_ref):
        i = pl.program_id(0)

        @pl.when(i == 0)
        def _():
            loss_ref[...] = jnp.zeros_like(loss_ref)
            gfw_ref[...] = jnp.zeros_like(gfw_ref)

        y1 = _gated_norm(osb_ref[...], gate_ref[...], sbw_ref[...])[3]
        y2 = _gated_norm(ossd_ref[...], z_ref[...], ssdw_ref[...])[3]
        h1 = (h0_ref[...] + _mm(y1, wo_ref[0:SB_W, :])) + _mm(y2, wo_ref[SB_W:SB_W + SSD_W, :])
        rs1 = lax.rsqrt(jnp.mean(h1 * h1, axis=-1, keepdims=True) + EPS)
        n1 = h1 * rs1
        fw = fw_ref[...]
        diff = jnp.where(i > 0, n1 * fw - tgt_ref[...], 0.0)
        loss_ref[...] += jnp.sum(diff * diff, axis=0, keepdims=True)
        d_out = diff * (1.0 / D_MODEL)
        gfw_ref[...] += jnp.sum(d_out * n1, axis=0, keepdims=True)
        g = d_out * fw
        dh1_ref[...] = rs1 * (g - n1 * jnp.mean(g * n1, axis=-1, keepdims=True))

    return pl.pallas_call(
        body, name="tail_fwd", grid=(nb,),
        in_specs=[pl.BlockSpec((TM, SB_W), row),
                  pl.BlockSpec((TM, SB_W), lambda i: (i, (COL_GATE - QKV_W) // SB_W)),
                  pl.BlockSpec((TM, SSD_W), row),
                  pl.BlockSpec((TM, SSD_W), lambda i: (i, (COL_Z - QKV_W) // SSD_W)),
                  pl.BlockSpec((TM, D_MODEL), row),
                  pl.BlockSpec((TM, D_MODEL), lambda i: (jnp.maximum(i - 1, 0), 0)),
                  pl.BlockSpec(memory_space=_VMEM),
                  pl.BlockSpec((1, SB_W), one), pl.BlockSpec((1, SSD_W), one), pl.BlockSpec((1, D_MODEL), one)],
        out_specs=[pl.BlockSpec((TM, D_MODEL), row), pl.BlockSpec((1, D_MODEL), one), pl.BlockSpec((1, D_MODEL), one)],
        out_shape=[_SDS((lp, D_MODEL), F32), _SDS((1, D_MODEL), F32), _SDS((1, D_MODEL), F32)],
        compiler_params=_params(("arbitrary",), 40),
    )(o_sb, proj, o_ssd, proj, h0, target, w_out, sb_w, ssd_w, fin_w)


def _gated_norm_bwd(o, gate, w, dy):
    sg, rs, n, _ = _gated_norm(o, gate, w)
    gw = jnp.sum(dy * n, axis=0, keepdims=True)
    dn = dy * w
    dp = rs * (dn - n * jnp.mean(dn * n, axis=-1, keepdims=True))
    d_o = dp * (gate * sg)
    d_gate = dp * o * (sg * (1.0 + gate * (1.0 - sg)))
    return d_o, d_gate, gw, n * w


def _tail_bwd(o_sb, o_ssd, proj, d_h1, w_out, sb_w, ssd_w):
    lp = o_sb.shape[0]
    nb = lp // TM
    row = lambda i, t: (i, 0)
    one = lambda i, t: (0, 0)

    def body(osb_ref, gate_ref, ossd_ref, z_ref, dh1_ref, wo_ref, sbw_ref, ssdw_ref,
             dosb_ref, dossd_ref, dproj_ref, gwo_ref, gsb_ref, gssd_ref):
        i = pl.program_id(0)
        t = pl.program_id(1)

        @pl.when(jnp.logical_and(i == 0, t == 0))
        def _():
            gwo_ref[...] = jnp.zeros_like(gwo_ref)
            gsb_ref[...] = jnp.zeros_like(gsb_ref)
            gssd_ref[...] = jnp.zeros_like(gssd_ref)

        dh1 = dh1_ref[...].astype(_MXU)

        def half(o_ref, g_ref, w_ref, do_ref, gn_ref, r0):
            dy = lax.dot_general(dh1, wo_ref[r0:r0 + SB_W, :], _NT, preferred_element_type=F32)
            d_o, d_g, gw, y = _gated_norm_bwd(o_ref[...], g_ref[...], w_ref[...], dy)
            do_ref[...] = d_o
            dproj_ref[...] = d_g.astype(_MXU)
            gn_ref[...] += gw
            gwo_ref[r0:r0 + SB_W, :] += lax.dot_general(y.astype(_MXU), dh1, _TN, preferred_element_type=F32)

        @pl.when(t == 0)
        def _():
            half(osb_ref, gate_ref, sbw_ref, dosb_ref, gsb_ref, 0)

        @pl.when(t == 1)
        def _():
            half(ossd_ref, z_ref, ssdw_ref, dossd_ref, gssd_ref, SB_W)

    tile = pl.BlockSpec((TM, SB_W), row)
    return pl.pallas_call(
        body, name="tail_bwd", grid=(nb, 2),
        in_specs=[tile, pl.BlockSpec((TM, SB_W), lambda i, t: (i, (COL_GATE - QKV_W) // SB_W)),
                  tile, pl.BlockSpec((TM, SSD_W), lambda i, t: (i, (COL_Z - QKV_W) // SSD_W)),
                  tile, pl.BlockSpec(memory_space=_VMEM),
                  pl.BlockSpec((1, SB_W), one), pl.BlockSpec((1, SSD_W), one)],
        out_specs=[tile, tile, pl.BlockSpec((TM, SB_W), lambda i, t: (i, COL_GATE // SB_W + t)),
                   pl.BlockSpec((SB_W + SSD_W, D_MODEL), one), pl.BlockSpec((1, SB_W), one), pl.BlockSpec((1, SSD_W), one)],
        out_shape=[_SDS((lp, SB_W), F32), _SDS((lp, SSD_W), F32), _SDS((lp, W_ALL), _MXU),
                   _SDS((SB_W + SSD_W, D_MODEL), F32), _SDS((1, SB_W), F32), _SDS((1, SSD_W), F32)],
        compiler_params=_params(("arbitrary", "arbitrary"), 48),
    )(o_sb, proj, o_ssd, proj, d_h1, w_out, sb_w, ssd_w)


def _d_u(d_proj, w_t):
    lp = d_proj.shape[0]
    tk = 512

    def body(dp_ref, w_ref, dpdt_ref, wdt_ref, o_ref):
        @pl.when(pl.program_id(0) == 0)
        def _():
            o_ref[...] = jnp.dot(dpdt_ref[...], wdt_ref[...], preferred_element_type=F32)

        o_ref[...] += jnp.dot(dp_ref[...], w_ref[...], preferred_element_type=F32)

    return pl.pallas_call(
        body, name="d_u", grid=(N_MAIN // tk,),
        in_specs=[pl.BlockSpec((lp, tk), lambda j: (0, j)),
                  pl.BlockSpec((tk, D_MODEL), lambda j: (j, 0)),
                  pl.BlockSpec((lp, 128), lambda j: (0, N_MAIN // 128)),
                  pl.BlockSpec((128, D_MODEL), lambda j: (N_MAIN // 128, 0))],
        out_specs=pl.BlockSpec((lp, D_MODEL), lambda j: (0, 0)),
        out_shape=_SDS((lp, D_MODEL), F32),
        compiler_params=_params(("arbitrary",), 48),
    )(d_proj, w_t, d_proj, w_t)


def _norm_bwd(du_all, h0, d_h1, norm_w):
    lp = h0.shape[0]
    nb = lp // TM
    seq = lp - OFF
    row = lambda i: (i, 0)
    one = lambda i: (0, 0)

    def body(du_ref, h0_ref, dh1_ref, nw_ref, gx_ref, gmeta_ref, gnw_ref):
        i = pl.program_id(0)

        @pl.when(i == 0)
        def _():
            gnw_ref[...] = jnp.zeros_like(gnw_ref)

        du = du_ref[...]
        h = h0_ref[...]
        rs = lax.rsqrt(jnp.mean(h * h, axis=-1, keepdims=True) + EPS)
        n0 = h * rs
        gnw_ref[...] += jnp.sum(du * n0, axis=0, keepdims=True)
        g = du * nw_ref[...]
        dh0 = dh1_ref[...] + rs * (g - n0 * jnp.mean(g * n0, axis=-1, keepdims=True))

        @pl.when(i == 0)
        def _():
            gmeta_ref[...] = dh0[PAD:PAD + N_META, :]

        @pl.when(i > 0)
        def _():
            gx_ref[...] = dh0

    tile = pl.BlockSpec((TM, D_MODEL), row)
    return pl.pallas_call(
        body, name="norm_bwd", grid=(nb,),
        in_specs=[tile, tile, tile, pl.BlockSpec((1, D_MODEL), one)],
        out_specs=[pl.BlockSpec((TM, D_MODEL), lambda i: (jnp.maximum(i - 1, 0), 0)),
                   pl.BlockSpec((N_META, D_MODEL), one), pl.BlockSpec((1, D_MODEL), one)],
        out_shape=[_SDS((seq, D_MODEL), F32), _SDS((N_META, D_MODEL), F32), _SDS((1, D_MODEL), F32)],
        compiler_params=_params(("arbitrary",)),
    )(du_all, h0, d_h1, norm_w)


def _grad_w_windows(u_t, d_proj):
    lp = d_proj.shape[0]
    hw = WIN_W // 2
    steps = 2 * N_CHIPS

    def body(ut_ref, dp_hbm, o_ref, buf, sems):
        s = pl.program_id(0)
        slot = s % 2

        def fetch(step, sl):
            start = pl.multiple_of((step // 2) * WIN_STEP + (step % 2) * hw, 128)
            return pltpu.make_async_copy(dp_hbm.at[:, pl.ds(start, hw)], buf.at[sl], sems.at[sl])

        @pl.when(s == 0)
        def _():
            fetch(0, 0).start()

        @pl.when(s + 1 < steps)
        def _():
            fetch(s + 1, 1 - slot).start()

        fetch(s, slot).wait()
        o_ref[0] = jnp.dot(ut_ref[...], buf[slot], preferred_element_type=F32).astype(o_ref.dtype)

    return pl.pallas_call(
        body, name="grad_w_in", grid=(steps,),
        in_specs=[pl.BlockSpec((D_MODEL, lp), lambda s: (0, 0)), pl.BlockSpec(memory_space=pl.ANY)],
        out_specs=pl.BlockSpec((1, D_MODEL, hw), lambda s: (s // 2, 0, s % 2)),
        out_shape=_SDS((N_CHIPS, D_MODEL, WIN_W), _MXU),
        scratch_shapes=[pltpu.VMEM((2, lp, hw), _MXU), pltpu.SemaphoreType.DMA((2,))],
        compiler_params=_params(("arbitrary",), 40),
    )(u_t, d_proj)


def _device_grads(x2d, target2d, meta_full, norm_w, w_t, conv_w, conv_b, dt_bias, a_log, d_skip,
                  sb_w, ssd_w, w_out, fin_w):
    lp = x2d.shape[0] + OFF
    nc = lp // CHUNK
    h0, u, u_t = _prep(x2d, meta_full, norm_w)
    qkv, proj, dt_raw = _inproj(u, w_t)
    o_sb, o_lo = _sb_fwd(qkv)
    dt_bias128 = jnp.pad(dt_bias, ((0, 0), (0, 128 - N_HEADS)))
    xbc, dt128 = _conv_fwd(proj, dt_raw, conv_w, conv_b, dt_bias128)
    dt_c = dt128[:, :N_HEADS].reshape(nc, CHUNK, N_HEADS)
    dt_tc = jnp.swapaxes(dt_c, 1, 2)
    a = -jnp.exp(a_log)
    a_t = a.reshape(N_HEADS, 1)
    dskip_x = jnp.repeat(d_skip, HEAD, axis=1)
    o_ssd, states = _ssd_fwd(xbc, dt_c, dt_tc, a, a_t, dskip_x)
    d_h1, sq_err, g_fin = _tail_fwd(o_sb, o_ssd, proj, h0, target2d, w_out, sb_w, ssd_w, fin_w)

    d_osb, d_ossd, d_proj, g_wout, g_sb, g_ssd = _tail_bwd(o_sb, o_ssd, proj, d_h1, w_out, sb_w, ssd_w)
    d_proj = _sb_bwd(qkv, o_sb, o_lo, d_osb, d_proj)
    d_xbc_act, ddt_a, ddt_b, ga1, ga2, gd = _ssd_bwd(xbc, dt_c, dt_tc, a, a_t, dskip_x, states, d_ossd)
    d_dt = (ddt_a + jnp.swapaxes(ddt_b, 1, 2)).reshape(lp, N_HEADS)
    d_dt128 = jnp.pad(d_dt, ((0, 0), (0, 128 - N_HEADS)))
    d_proj, g_convw, g_convb, g_dtb128 = _conv_bwd(proj, dt_raw, conv_w, conv_b, dt_bias128, d_xbc_act, d_dt128, d_proj)
    g_win = _grad_w_windows(u_t, d_proj)
    g_x, g_meta, g_nw = _norm_bwd(_d_u(d_proj, w_t), h0, d_h1, norm_w)
    g_alog = (ga1 + ga2.reshape(1, N_HEADS)) * a
    g_dskip = gd.reshape(N_HEADS, HEAD).sum(axis=1).reshape(1, N_HEADS)
    grads = dict(meta_tokens=g_meta, norm_w=g_nw, w_in=g_win, conv_w=g_convw, conv_b=g_convb,
                 dt_bias=g_dtb128[:, :N_HEADS], a_log=g_alog, d_skip=g_dskip, sb_norm_w=g_sb, ssd_norm_w=g_ssd,
                 w_out=g_wout, final_norm_w=g_fin)
    return sq_err, g_x, grads


_MESH = pl.DeviceIdType.MESH
_ANY = pl.BlockSpec(memory_space=pl.ANY)


def _place():
    return lax.axis_index("x"), lax.axis_index("y"), lax.axis_index("c")


def _other_chips(x, y):
    return ((1 - x, y), (x, 1 - y), (1 - x, 1 - y))


def _gather_shards(arrays, n_big):
    n = len(arrays)

    def body(*refs):
        srcs, dsts = refs[:n], refs[n:2 * n]
        send_sems, recv_sems, fwd_send, fwd_recv = refs[2 * n:]
        x, y, c = _place()
        mine = 2 * x + y
        chips = _other_chips(x, y)

        def window(a):
            half = arrays[a].shape[1] // 2
            return pl.ds(pl.multiple_of(c * half, 128), half)

        first = []
        for a in range(n):
            for k, (px, py) in enumerate(chips):
                if a < n_big:
                    src, dst = srcs[a].at[:, window(a)], dsts[a].at[mine, :, window(a)]
                else:
                    src, dst = srcs[a], dsts[a].at[mine]
                cp = pltpu.make_async_remote_copy(
                    src_ref=src, dst_ref=dst, send_sem=send_sems.at[a * 3 + k], recv_sem=recv_sems.at[a * 3 + k],
                    device_id=(px, py, c), device_id_type=_MESH)
                cp.start()
                first.append(cp)
        passed = []
        for a in range(n):
            for k, (px, py) in enumerate(chips):
                first[a * 3 + k].wait_recv()
                if a < n_big:
                    landed = dsts[a].at[2 * px + py, :, window(a)]
                    cp = pltpu.make_async_remote_copy(
                        src_ref=landed, dst_ref=landed, send_sem=fwd_send.at[a * 3 + k], recv_sem=fwd_recv.at[a * 3 + k],
                        device_id=(x, y, 1 - c), device_id_type=_MESH)
                    cp.start()
                    passed.append(cp)
        for cp in passed:
            cp.wait_recv()
        for cp in first + passed:
            cp.wait_send()

    got = pl.pallas_call(
        body, name="gather_shards",
        in_specs=[_ANY] * n, out_specs=[_ANY] * n,
        out_shape=[_SDS((N_CHIPS,) + a.shape, a.dtype) for a in arrays],
        scratch_shapes=[pltpu.SemaphoreType.DMA((3 * n,)), pltpu.SemaphoreType.DMA((3 * n,)),
                        pltpu.SemaphoreType.DMA((3 * n_big,)), pltpu.SemaphoreType.DMA((3 * n_big,))],
    )(*arrays)
    mine = 2 * lax.axis_index("x") + lax.axis_index("y")
    return [lax.dynamic_update_slice(g, a[None], (mine,) + (0,) * a.ndim) for g, a in zip(got, arrays)]


def _scatter_slabs(arrays):
    n = len(arrays)

    def body(*refs):
        srcs, dsts = refs[:n], refs[n:2 * n]
        send_sems, recv_sems = refs[2 * n:]
        x, y, c = _place()
        mine = 2 * x + y
        remote = []
        for a in range(n):
            for k, (px, py) in enumerate(_other_chips(x, y)):
                cp = pltpu.make_async_remote_copy(
                    src_ref=srcs[a].at[2 * px + py], dst_ref=dsts[a].at[mine],
                    send_sem=send_sems.at[a * 3 + k], recv_sem=recv_sems.at[a * 3 + k],
                    device_id=(px, py, c), device_id_type=_MESH)
                cp.start()
                remote.append(cp)
        for cp in remote:
            cp.wait_recv()
        for cp in remote:
            cp.wait_send()

    return pl.pallas_call(
        body, name="scatter_slabs",
        in_specs=[_ANY] * n, out_specs=[_ANY] * n,
        out_shape=[_SDS(a.shape, a.dtype) for a in arrays],
        scratch_shapes=[pltpu.SemaphoreType.DMA((3 * n,)), pltpu.SemaphoreType.DMA((3 * n,))],
    )(*arrays)


def _swap_halves(arrays):
    n = len(arrays)

    def body(*refs):
        srcs, dsts = refs[:n], refs[n:2 * n]
        send_sems, recv_sems = refs[2 * n:]
        x, y, c = _place()
        copies = []
        for a in range(n):
            half = arrays[a].shape[1] // 2
            cp = pltpu.make_async_remote_copy(
                src_ref=srcs[a].at[:, pl.ds(pl.multiple_of((1 - c) * half, 16), half)], dst_ref=dsts[a],
                send_sem=send_sems.at[a], recv_sem=recv_sems.at[a],
                device_id=(x, y, 1 - c), device_id_type=_MESH)
            cp.start()
            copies.append(cp)
        for cp in copies:
            cp.wait_recv()
        for cp in copies:
            cp.wait_send()

    return pl.pallas_call(
        body, name="swap_halves",
        in_specs=[_ANY] * n, out_specs=[_ANY] * n,
        out_shape=[_SDS((a.shape[0], a.shape[1] // 2, a.shape[2]), a.dtype) for a in arrays],
        scratch_shapes=[pltpu.SemaphoreType.DMA((n,)), pltpu.SemaphoreType.DMA((n,))],
    )(*arrays)


def _join_halves(arrays, by_cols):
    n = len(arrays)

    def body(*refs):
        dsts = refs[n:2 * n]
        send_sems, recv_sems = refs[2 * n:]
        x, y, c = _place()
        copies = []
        for a in range(n):
            if by_cols[a]:
                half = arrays[a].shape[1] // 2
                mine = dsts[a].at[:, pl.ds(pl.multiple_of(c * half, 128), half)]
            else:
                half = arrays[a].shape[0] // 2
                mine = dsts[a].at[pl.ds(pl.multiple_of(c * half, 16), half)]
            cp = pltpu.make_async_remote_copy(
                src_ref=mine, dst_ref=mine, send_sem=send_sems.at[a], recv_sem=recv_sems.at[a],
                device_id=(x, y, 1 - c), device_id_type=_MESH)
            cp.start()
            copies.append(cp)
        for cp in copies:
            cp.wait_recv()
        for cp in copies:
            cp.wait_send()

    return pl.pallas_call(
        body, name="join_halves",
        in_specs=[_ANY] * n, out_specs=[_ANY] * n,
        out_shape=[_SDS(a.shape, a.dtype) for a in arrays],
        input_output_aliases={a: a for a in range(n)},
        scratch_shapes=[pltpu.SemaphoreType.DMA((n,)), pltpu.SemaphoreType.DMA((n,))],
    )(*arrays)


N_DEV = 8
SMALL_ROWS = 32
SMALL_COLS = XBC_W


def _gather_small(packed):
    def body(src_ref, dst_ref, send_sems, recv_sems, local_sem):
        x, y, c = _place()
        me = 4 * x + 2 * y + c
        own = pltpu.make_async_copy(src_ref, dst_ref.at[me], local_sem)
        own.start()
        copies = []
        for k in range(1, N_DEV):
            bx, by, bc = (k >> 2) & 1, (k >> 1) & 1, k & 1
            peer = (x + bx - 2 * x * bx, y + by - 2 * y * by, c + bc - 2 * c * bc)
            cp = pltpu.make_async_remote_copy(
                src_ref=src_ref, dst_ref=dst_ref.at[me], send_sem=send_sems.at[k - 1], recv_sem=recv_sems.at[k - 1],
                device_id=peer, device_id_type=_MESH)
            cp.start()
            copies.append(cp)
        for cp in copies:
            cp.wait_recv()
        for cp in copies:
            cp.wait_send()
        own.wait()

    return pl.pallas_call(
        body, name="gather_small",
        in_specs=[pl.BlockSpec(memory_space=_VMEM)], out_specs=pl.BlockSpec(memory_space=_VMEM),
        out_shape=_SDS((N_DEV, SMALL_ROWS, SMALL_COLS), F32),
        scratch_shapes=[pltpu.SemaphoreType.DMA((N_DEV - 1,)), pltpu.SemaphoreType.DMA((N_DEV - 1,)),
                        pltpu.SemaphoreType.DMA],
    )(packed)


def _adamw(w, g, m, v):
    m = ADAM_B1 * m + (1.0 - ADAM_B1) * g
    v = ADAM_B2 * v + (1.0 - ADAM_B2) * (g * g)
    m_hat = m / (1.0 - ADAM_B1 ** ADAM_STEP)
    v_hat = v / (1.0 - ADAM_B2 ** ADAM_STEP)
    delta = -ADAM_LR * (m_hat / (jnp.sqrt(v_hat) + ADAM_EPS) + ADAM_WD * w)
    return delta, m, v


def _sum_slabs(slabs, core, name, transposed=False):
    _, h, c = slabs.shape
    tr = 128
    nblk = h // tr

    def body(core_ref, s_ref, o_ref):
        tot = ((s_ref[0].astype(F32) + s_ref[1].astype(F32)) + s_ref[2].astype(F32)) + s_ref[3].astype(F32)
        o_ref[...] = tot.T if transposed else tot

    if transposed:
        out_spec = pl.BlockSpec((c, tr), lambda i, core_ref: (0, core_ref[0] * nblk + i))
        out_shape = _SDS((c, 2 * h), F32)
    else:
        out_spec = pl.BlockSpec((tr, c), lambda i, core_ref: (core_ref[0] * nblk + i, 0))
        out_shape = _SDS((2 * h, c), F32)
    grid_spec = pltpu.PrefetchScalarGridSpec(
        num_scalar_prefetch=1, grid=(nblk,),
        in_specs=[pl.BlockSpec((N_CHIPS, tr, c), lambda i, core_ref: (0, i, 0))],
        out_specs=out_spec)
    return pl.pallas_call(
        body, name=name, grid_spec=grid_spec, out_shape=out_shape,
        compiler_params=_params(("arbitrary",)),
    )(core, slabs)


def _add_halves(own, recv, core, name):
    _, r, c = own.shape
    half = r // 2
    tr = 128
    nblk = half // tr

    def body(core_ref, a_ref, b_ref, o_ref):
        o_ref[...] = (a_ref[...].astype(F32) + b_ref[...].astype(F32)).astype(o_ref.dtype)

    grid_spec = pltpu.PrefetchScalarGridSpec(
        num_scalar_prefetch=1, grid=(nblk,),
        in_specs=[pl.BlockSpec((N_CHIPS, tr, c), lambda i, core_ref: (0, core_ref[0] * nblk + i, 0)),
                  pl.BlockSpec((N_CHIPS, tr, c), lambda i, core_ref: (0, i, 0))],
        out_specs=pl.BlockSpec((N_CHIPS, tr, c), lambda i, core_ref: (0, i, 0)))
    return pl.pallas_call(
        body, name=name, grid_spec=grid_spec, out_shape=_SDS((N_CHIPS, half, c), own.dtype),
        compiler_params=_params(("arbitrary",)),
    )(core, own, recv)


def _update_big(w, m, v, g, name):
    r, c = w.shape

    def body(w_ref, m_ref, v_ref, g_ref, d_ref, mo_ref, vo_ref):
        delta, m_new, v_new = _adamw(w_ref[...], g_ref[...], m_ref[...], v_ref[...])
        d_ref[...] = delta
        mo_ref[...] = m_new
        vo_ref[...] = v_new

    if r % 128 == 0:
        steps, spec = r // 128, pl.BlockSpec((128, c), lambda i: (i, 0))
    else:
        steps, spec = c // 128, pl.BlockSpec((r, 128), lambda i: (0, i))
    return pl.pallas_call(
        body, name=name, grid=(steps,),
        in_specs=[spec] * 4, out_specs=[spec] * 3,
        out_shape=[_SDS((r, c), F32)] * 3,
        compiler_params=_params(("arbitrary",)),
    )(w, m, v, g)


_ROW = dict(norm_w=0, sb_norm_w=1, ssd_norm_w=2, final_norm_w=3, conv_b=4, dt_bias=5, a_log=6, d_skip=7,
            conv_w=8, sq_err=12, meta_tokens=16)
_SMALL = ("meta_tokens", "norm_w", "conv_w", "conv_b", "dt_bias", "a_log", "d_skip", "sb_norm_w", "ssd_norm_w",
          "final_norm_w")


def _pack_small(sq_err, grads):
    def rowpad(a):
        return jnp.pad(a, ((0, 0), (0, SMALL_COLS - a.shape[1])))

    rows = [rowpad(grads[k]) for k in ("norm_w", "sb_norm_w", "ssd_norm_w", "final_norm_w", "conv_b", "dt_bias", "a_log", "d_skip")]
    rows.append(grads["conv_w"])
    rows.append(rowpad(sq_err))
    rows.append(jnp.zeros((3, SMALL_COLS), F32))
    rows.append(rowpad(grads["meta_tokens"]))
    return jnp.concatenate(rows, axis=0)


def _update_small(gathered, ws, ms, vs):
    names = _SMALL
    n = len(names)

    def body(*refs):
        g_ref = refs[0]
        w_refs, m_refs, v_refs = refs[1:1 + n], refs[1 + n:1 + 2 * n], refs[1 + 2 * n:1 + 3 * n]
        outs = refs[1 + 3 * n:]
        loss_ref = outs[0]
        go, do, mo, vo = outs[1:1 + n], outs[1 + n:1 + 2 * n], outs[1 + 2 * n:1 + 3 * n], outs[1 + 3 * n:1 + 4 * n]
        tot = g_ref[0]
        for d in range(1, N_DEV):
            tot = tot + g_ref[d]
        x, y, _ = _place()
        chip = 2 * x + y
        loss_ref[...] = jnp.broadcast_to(
            0.5 * jnp.sum(tot[_ROW["sq_err"]:_ROW["sq_err"] + 1, 0:D_MODEL], axis=1, keepdims=True) / D_MODEL, (1, 128))
        for idx, nm in enumerate(names):
            r0 = _ROW[nm]
            rows, cols = w_refs[idx].shape
            if nm in ("conv_w", "meta_tokens"):
                g = jnp.zeros((rows, cols), F32)
                for j in range(N_CHIPS):
                    g = g + jnp.where(chip == j, tot[r0:r0 + rows, j * cols:(j + 1) * cols], 0.0)
            else:
                g = tot[r0:r0 + rows, 0:cols]
            delta, m_new, v_new = _adamw(w_refs[idx][...], g, m_refs[idx][...], v_refs[idx][...])
            go[idx][...] = g
            do[idx][...] = delta
            mo[idx][...] = m_new
            vo[idx][...] = v_new

    shapes = [_SDS(ws[nm].shape, F32) for nm in names]
    vm = pl.BlockSpec(memory_space=_VMEM)
    res = pl.pallas_call(
        body, name="update_small",
        in_specs=[vm] * (1 + 3 * n), out_specs=[vm] * (1 + 4 * n),
        out_shape=[_SDS((1, 128), F32)] + shapes * 4,
    )(gathered, *[ws[nm] for nm in names], *[ms[nm] for nm in names], *[vs[nm] for nm in names])
    loss = res[0][0, 0]
    g = dict(zip(names, res[1:1 + n]))
    d = dict(zip(names, res[1 + n:1 + 2 * n]))
    m = dict(zip(names, res[1 + 2 * n:1 + 3 * n]))
    v = dict(zip(names, res[1 + 3 * n:1 + 4 * n]))
    return loss, g, d, m, v


_WEIGHTS = ("meta_tokens", "norm_w", "w_in", "conv_w", "conv_b", "dt_bias", "a_log", "d_skip", "sb_norm_w",
            "ssd_norm_w", "w_out", "final_norm_w")


def kernel(x, meta_tokens, norm_w, w_in, conv_w, conv_b, dt_bias, a_log, d_skip, sb_norm_w, ssd_norm_w, w_out, final_norm_w, loss_target, m_meta_tokens, m_norm_w, m_w_in, m_conv_w, m_conv_b, m_dt_bias, m_a_log, m_d_skip, m_sb_norm_w, m_ssd_norm_w, m_w_out, m_final_norm_w, v_meta_tokens, v_norm_w, v_w_in, v_conv_w, v_conv_b, v_dt_bias, v_a_log, v_d_skip, v_sb_norm_w, v_ssd_norm_w, v_w_out, v_final_norm_w):
    given = dict(meta_tokens=meta_tokens, norm_w=norm_w, w_in=w_in, conv_w=conv_w, conv_b=conv_b, dt_bias=dt_bias,
                 a_log=a_log, d_skip=d_skip, sb_norm_w=sb_norm_w, ssd_norm_w=ssd_norm_w, w_out=w_out,
                 final_norm_w=final_norm_w)
    mom = dict(meta_tokens=m_meta_tokens, norm_w=m_norm_w, w_in=m_w_in, conv_w=m_conv_w, conv_b=m_conv_b,
               dt_bias=m_dt_bias, a_log=m_a_log, d_skip=m_d_skip, sb_norm_w=m_sb_norm_w, ssd_norm_w=m_ssd_norm_w,
               w_out=m_w_out, final_norm_w=m_final_norm_w)
    var = dict(meta_tokens=v_meta_tokens, norm_w=v_norm_w, w_in=v_w_in, conv_w=v_conv_w, conv_b=v_conv_b,
               dt_bias=v_dt_bias, a_log=v_a_log, d_skip=v_d_skip, sb_norm_w=v_sb_norm_w, ssd_norm_w=v_ssd_norm_w,
               w_out=v_w_out, final_norm_w=v_final_norm_w)
    seq = x.shape[1]

    def two_d(a):
        return a.reshape((-1, a.shape[-1])) if a.ndim != 2 else a

    def rows_first(a):
        return jnp.transpose(a, (2, 0, 1)).reshape(W_IN_SHARD, D_MODEL)

    def rows_last(a):
        return jnp.transpose(a.reshape(W_IN_SHARD, 1, D_MODEL), (1, 2, 0))

    w_in_t, m_in_t, v_in_t = rows_first(w_in), rows_first(m_w_in), rows_first(v_w_in)

    g_win, g_wout, g_meta, g_cw = _gather_shards(
        [w_in_t.astype(_MXU), w_out[0].astype(_MXU), meta_tokens, conv_w[0]], 2)
    w_t = jnp.pad(g_win.reshape(D_IN, D_MODEL), ((0, W_ALL - D_IN), (0, 0)))
    w_out_full = g_wout.reshape(2 * D_MODEL, D_MODEL)
    meta_full = jnp.swapaxes(g_meta, 0, 1).reshape(N_META, D_MODEL)
    conv_w_full = jnp.swapaxes(g_cw, 0, 1).reshape(4, XBC_W)

    sq_err, g_x, grads = _device_grads(
        x.reshape(seq, D_MODEL), loss_target.reshape(seq, D_MODEL), meta_full, norm_w, w_t, conv_w_full,
        conv_b, dt_bias, a_log, d_skip, sb_norm_w, ssd_norm_w, w_out_full, final_norm_w.reshape(1, D_MODEL))

    core = lax.axis_index("c").astype(jnp.int32).reshape(1)
    slab_in = grads["w_in"]
    slab_out = grads["w_out"].reshape(N_CHIPS, W_OUT_SHARD, D_MODEL).astype(_MXU)
    sib_in, sib_out = _swap_halves([slab_in, slab_out])
    chip_in = _add_halves(slab_in, sib_in, core, "chip_sum_w_in")
    chip_out = _add_halves(slab_out, sib_out, core, "chip_sum_w_out")
    got_in, got_out = _scatter_slabs([chip_in, chip_out])
    chip = 2 * lax.axis_index("x") + lax.axis_index("y")

    def with_own(got, sent):
        own = lax.dynamic_slice(sent, (chip, 0, 0), (1,) + sent.shape[1:])
        return lax.dynamic_update_slice(got, own, (chip, 0, 0))

    g_in, g_out = _join_halves([_sum_slabs(with_own(got_in, chip_in), core, "sum_w_in", transposed=True),
                                _sum_slabs(with_own(got_out, chip_out), core, "sum_w_out")], (True, False))
    g_in = lax.dynamic_slice(g_in, (4 * chip, 0), (W_IN_SHARD, D_MODEL))
    big = dict(w_in=tuple(rows_last(a) for a in (g_in,) + tuple(_update_big(w_in_t, m_in_t, v_in_t, g_in, "update_w_in"))),
               w_out=(g_out,) + tuple(_update_big(w_out[0], m_w_out[0], v_w_out[0], g_out, "update_w_out")))

    gathered = _gather_small(_pack_small(sq_err, grads))
    loss, sg, sd, sm, sv = _update_small(
        gathered, {k: two_d(given[k]) for k in _SMALL}, {k: two_d(mom[k]) for k in _SMALL},
        {k: two_d(var[k]) for k in _SMALL})

    out = {}
    for idx, group in enumerate((sg, sd, sm, sv)):
        for k in _SMALL:
            out[(idx, k)] = group[k].reshape(given[k].shape)
        for k in ("w_in", "w_out"):
            out[(idx, k)] = big[k][idx].reshape(given[k].shape)
    return (loss, g_x.reshape(x.shape), *[out[(idx, k)] for idx in range(4) for k in _WEIGHTS])
```

```python
import functools
import math

import jax
import jax.numpy as jnp
from jax import lax
from jax.experimental import pallas as pl
from jax.experimental.pallas import tpu as pltpu

F32 = jnp.float32
_MXU = jnp.bfloat16

D_MODEL = 1024
N_META = 16
PAD = 112
OFF = PAD + N_META
TM = 128
CHUNK = 64
SB_W = 1024
SSD_W = 1024
N_HEADS = 16
HEAD = 64
N_GROUPS = 2
N_STATE = 128
XBC_W = SSD_W + 2 * N_GROUPS * N_STATE
N_MAIN = 4 * SB_W + SSD_W + XBC_W
QKV_W = 3 * SB_W
REST_W = N_MAIN - QKV_W
COL_GATE = 3 * SB_W
COL_Z = 4 * SB_W
COL_XBC = 5 * SB_W
D_IN = N_MAIN + N_HEADS
W_ALL = N_MAIN + 128
WIN_STEP = 1664
WIN_W = 1792
EPS = 1e-5
N_CHIPS = 4
W_IN_SHARD = D_IN // N_CHIPS
W_OUT_SHARD = 2 * D_MODEL // N_CHIPS

ADAM_LR = 0.001
ADAM_B1 = 0.9
ADAM_B2 = 0.999
ADAM_EPS = 1e-08
ADAM_WD = 0.01
ADAM_STEP = 10

_SDS = jax.ShapeDtypeStruct
_NT = (((1,), (1,)), ((), ()))
_TN = (((0,), (0,)), ((), ()))
_VMEM = pltpu.VMEM


def _params(sem=None, vmem_mb=None):
    kw = {}
    if sem is not None:
        kw["dimension_semantics"] = sem
    if vmem_mb is not None:
        kw["vmem_limit_bytes"] = vmem_mb * 1024 * 1024
    return pltpu.CompilerParams(**kw)


def _mm(a, b):
    return jnp.dot(a.astype(_MXU), b.astype(_MXU), preferred_element_type=F32)


def _mm_nt(a, b):
    return lax.dot_general(a.astype(_MXU), b.astype(_MXU), _NT, preferred_element_type=F32)


def _mm_tn(a, b):
    return lax.dot_general(a.astype(_MXU), b.astype(_MXU), _TN, preferred_element_type=F32)


def _split(x, parts):
    out = []
    r = x
    for _ in range(parts):
        p = r.astype(_MXU)
        out.append(p)
        r = r - p.astype(F32)
    return out


def _sel_right(x, m01, parts=3):
    acc = None
    for p in _split(x, parts):
        t = jnp.dot(p, m01, preferred_element_type=F32)
        acc = t if acc is None else acc + t
    return acc


def _sel_left(m01, x, parts=3):
    acc = None
    for p in _split(x, parts):
        t = jnp.dot(m01, p, preferred_element_type=F32)
        acc = t if acc is None else acc + t
    return acc


def _iota(shape, axis):
    return lax.broadcasted_iota(jnp.int32, shape, axis)


def _sigmoid(x):
    return 1.0 / (1.0 + jnp.exp(-x))


def _prep(x2d, meta_full, norm_w):
    seq = x2d.shape[0]
    lp = seq + OFF
    nb = lp // TM

    def body(x_ref, meta_ref, w_ref, h0_ref, u_ref, ut_ref):
        i = pl.program_id(0)

        @pl.when(i == 0)
        def _():
            h0_ref[...] = jnp.concatenate([jnp.zeros((PAD, D_MODEL), F32), meta_ref[...]], axis=0)

        @pl.when(i > 0)
        def _():
            h0_ref[...] = x_ref[...]

        h = h0_ref[...]
        rs = lax.rsqrt(jnp.mean(h * h, axis=-1, keepdims=True) + EPS)
        u = (h * rs * w_ref[...]).astype(_MXU)
        u_ref[...] = u
        ut_ref[...] = u.T

    return pl.pallas_call(
        body, name="prep", grid=(nb,),
        in_specs=[pl.BlockSpec((TM, D_MODEL), lambda i: (jnp.maximum(i - 1, 0), 0)),
                  pl.BlockSpec((N_META, D_MODEL), lambda i: (0, 0)),
                  pl.BlockSpec((1, D_MODEL), lambda i: (0, 0))],
        out_specs=[pl.BlockSpec((TM, D_MODEL), lambda i: (i, 0)),
                   pl.BlockSpec((TM, D_MODEL), lambda i: (i, 0)),
                   pl.BlockSpec((D_MODEL, TM), lambda i: (0, i))],
        out_shape=[_SDS((lp, D_MODEL), F32), _SDS((lp, D_MODEL), _MXU), _SDS((D_MODEL, lp), _MXU)],
        compiler_params=_params(("arbitrary",)),
    )(x2d, meta_full, norm_w)


def _inproj(u, w_t):
    lp = u.shape[0]
    tn = 512

    nq = QKV_W // tn

    def body(u_ref, w_ref, wdt_ref, qkv_ref, rest_ref, odt_ref):
        j = pl.program_id(0)
        res = lax.dot_general(u_ref[...], w_ref[...], _NT, preferred_element_type=F32)

        @pl.when(j < nq)
        def _():
            qkv_ref[...] = res.astype(qkv_ref.dtype)

        @pl.when(j >= nq)
        def _():
            rest_ref[...] = res

        @pl.when(j == 0)
        def _():
            odt_ref[...] = lax.dot_general(u_ref[...], wdt_ref[...], _NT, preferred_element_type=F32)

    return pl.pallas_call(
        body, name="inproj", grid=(N_MAIN // tn,),
        in_specs=[pl.BlockSpec((lp, D_MODEL), lambda j: (0, 0)),
                  pl.BlockSpec((tn, D_MODEL), lambda j: (j, 0)),
                  pl.BlockSpec((128, D_MODEL), lambda j: (N_MAIN // 128, 0))],
        out_specs=[pl.BlockSpec((lp, tn), lambda j: (0, jnp.minimum(j, nq - 1))),
                   pl.BlockSpec((lp, tn), lambda j: (0, jnp.maximum(j - nq, 0))),
                   pl.BlockSpec((lp, 128), lambda j: (0, 0))],
        out_shape=[_SDS((lp, QKV_W), _MXU), _SDS((lp, REST_W), F32), _SDS((lp, 128), F32)],
        compiler_params=_params(("arbitrary",), 48),
    )(u, w_t, w_t)


SB_WINDOW = 3
SB_DEAD = -104.0


def _sb_logs(qh, kwin):
    z = lax.dot_general(qh, kwin, _NT, preferred_element_type=F32)
    e = jnp.exp(-jnp.abs(z))
    l1p = jnp.log(1.0 + e)
    lk_full = -(jnp.maximum(z, 0.0) + l1p)
    ls = jnp.minimum(z, 0.0) - l1p
    return z, e, ls, lk_full


def _blk(a, b):
    return a[:, b * TM:(b + 1) * TM]


def _stacked_sel(blocks, m01):
    n = len(blocks)
    pieces = [_split(b, 2) for b in blocks]
    stacked = jnp.concatenate([p[0] for p in pieces] + [p[1] for p in pieces], axis=0)
    res = jnp.dot(stacked, m01, preferred_element_type=F32)
    return [res[j * TM:(j + 1) * TM] + res[(n + j) * TM:(n + j + 1) * TM] for j in range(n)]


def _sb_weights(ls, lk_full, run, last_mask, upper, n):
    lk = [_blk(lk_full, b) for b in range(n)]
    lk[n - 1] = jnp.where(last_mask, lk[n - 1], 0.0)
    aft = _stacked_sel(lk, upper)
    w = [None] * n
    for b in range(n - 1, -1, -1):
        wb = jnp.exp(_blk(ls, b) + aft[b] + run)
        w[b] = jnp.where(last_mask, wb, 0.0) if b == n - 1 else wb
        run = run + jnp.sum(lk[b], axis=1, keepdims=True)
    return w, run


def _sb_fwd(qkv):
    lp = qkv.shape[0]
    nb = lp // TM

    def body(q_ref, k_ref, v_ref, o_ref, olo_ref, acc, run_scr):
        i = pl.program_id(1)
        lane = _iota((TM, TM), 1)
        row = _iota((TM, TM), 0)
        head0 = lane < HEAD
        upper = (row > lane).astype(_MXU)
        strict = lane < row
        q = q_ref[...] * (1.0 / math.sqrt(HEAD))
        qh = (jnp.where(head0, q, 0.0).astype(_MXU), jnp.where(head0, 0.0, q).astype(_MXU))

        def key_set(first, n, last_mask):
            off = pl.multiple_of(first * TM, TM)
            kwin = k_ref[pl.ds(off, n * TM), :].astype(_MXU)
            vwin = v_ref[pl.ds(off, n * TM), :].astype(_MXU)
            alive = None
            for hh in range(2):
                run = run_scr[hh][:, 0:1]
                _, _, ls, lk_full = _sb_logs(qh[hh], kwin)
                w, run = _sb_weights(ls, lk_full, run, last_mask, upper, n)
                pieces = [_split(wb, 2) for wb in w]
                stacked = jnp.concatenate(
                    [jnp.concatenate([p[0] for p in pieces], axis=1), jnp.concatenate([p[1] for p in pieces], axis=1)], axis=0)
                res = jnp.dot(stacked, vwin, preferred_element_type=F32)
                acc[hh] += res[0:TM]
                acc[2 + hh] += res[TM:2 * TM]
                run_scr[hh] = jnp.broadcast_to(run, (TM, TM))
                top = jnp.max(run)
                alive = top if alive is None else jnp.maximum(alive, top)
            return (alive > SB_DEAD).astype(jnp.int32)

        acc[...] = jnp.zeros_like(acc)
        run_scr[...] = jnp.zeros_like(run_scr)

        @pl.when(i >= SB_WINDOW - 1)
        def _():
            key_set(i - (SB_WINDOW - 1), SB_WINDOW, strict)

        start = jnp.where(i >= SB_WINDOW - 1, i - SB_WINDOW, i)
        alive0 = (jnp.max(run_scr[...]) > SB_DEAD).astype(jnp.int32)

        def cond(c):
            return jnp.logical_and(c[0] >= 0, c[1] > 0)

        def step(c):
            kb = c[0]
            return kb - 1, key_set(kb, 1, jnp.logical_or(strict, kb < i))

        lax.while_loop(cond, step, (start, alive0))
        o_ref[...] = jnp.where(head0, acc[0], acc[1])
        olo_ref[...] = jnp.where(head0, acc[2], acc[3])

    npair = SB_W // TM
    blk = pl.BlockSpec((TM, TM), lambda p, i: (i, p))
    return pl.pallas_call(
        body, name="sb_fwd", grid=(npair, nb),
        in_specs=[blk,
                  pl.BlockSpec((lp, TM), lambda p, i: (0, npair + p)),
                  pl.BlockSpec((lp, TM), lambda p, i: (0, 2 * npair + p))],
        out_specs=[blk, blk],
        out_shape=[_SDS((lp, SB_W), F32), _SDS((lp, SB_W), F32)],
        scratch_shapes=[pltpu.VMEM((4, TM, TM), F32), pltpu.VMEM((2, TM, TM), F32)],
        compiler_params=_params(("arbitrary", "arbitrary")),
    )(qkv, qkv, qkv)


def _sb_bwd(qkv, o_sb, o_lo, d_o, d_proj):
    lp = qkv.shape[0]
    nb = lp // TM
    npair = SB_W // TM
    scale = 1.0 / math.sqrt(HEAD)

    def body(q_ref, k_ref, v_ref, o_ref, olo_ref, do_ref, dproj_in, dproj_ref,
             dq_all, dk_ref, dv_ref, stage, sems, dq_acc, run_scr, gsum_scr):
        p = pl.program_id(0)
        i = pl.program_id(1)

        @pl.when(i == 0)
        def _():
            dk_ref[...] = jnp.zeros_like(dk_ref)
            dv_ref[...] = jnp.zeros_like(dv_ref)

        lane = _iota((TM, TM), 1)
        row = _iota((TM, TM), 0)
        head0 = lane < HEAD
        hmask = (head0, jnp.logical_not(head0))
        upper = (row > lane).astype(_MXU)
        lower_incl = (row >= lane).astype(_MXU)
        strict = lane < row
        q = q_ref[...] * scale
        do = do_ref[...]
        prod = do.astype(_MXU).astype(F32) * (o_ref[...] + olo_ref[...])
        qh = tuple(jnp.where(m, q, 0.0).astype(_MXU) for m in hmask)
        doh = tuple(jnp.where(m, do, 0.0).astype(_MXU) for m in hmask)
        gtot = tuple(jnp.sum(jnp.where(m, prod, 0.0), axis=1, keepdims=True) for m in hmask)

        def key_set(first, n, last_mask):
            off = pl.multiple_of(first * TM, TM)
            kf = k_ref[pl.ds(off, n * TM), :]
            kwin = kf.astype(_MXU)
            vwin = v_ref[pl.ds(off, n * TM), :].astype(_MXU)
            dk_win = None
            alive = None
            for hh in range(2):
                run = run_scr[hh][:, 0:1]
                gsum = gsum_scr[hh][:, 0:1]
                z, e, ls, lk_full = _sb_logs(qh[hh], kwin)
                w, run = _sb_weights(ls, lk_full, run, last_mask, upper, n)
                r = 1.0 / (1.0 + e)
                er = e * r
                pos = z >= 0.0
                beta = jnp.where(pos, r, er)
                one_m_beta = jnp.where(pos, er, r)
                dw = lax.dot_general(doh[hh], vwin, _NT, preferred_element_type=F32)
                g = [_blk(dw, b) * w[b] for b in range(n)]
                suffix = _stacked_sel(g, lower_incl)
                dz = [None] * n
                for b in range(n - 1, -1, -1):
                    prefix = gtot[hh] - gsum - suffix[b]
                    d = g[b] * _blk(one_m_beta, b) - _blk(beta, b) * prefix
                    dz[b] = (jnp.where(last_mask, d, 0.0) if b == n - 1 else d).astype(_MXU)
                    gsum = gsum + jnp.sum(g[b], axis=1, keepdims=True)
                dzw = jnp.concatenate(dz, axis=1)
                ww = jnp.concatenate([wb.astype(_MXU) for wb in w], axis=1)
                kh = jnp.where(hmask[hh][0:1, :], kf, 0.0).astype(_MXU)
                dq_acc[...] += jnp.dot(dzw, kh, preferred_element_type=F32)
                dk_h = lax.dot_general(dzw, qh[hh], _TN, preferred_element_type=F32)
                dv_h = lax.dot_general(ww, doh[hh], _TN, preferred_element_type=F32)
                dk_win = (dk_h, dv_h) if dk_win is None else (dk_win[0] + dk_h, dk_win[1] + dv_h)
                run_scr[hh] = jnp.broadcast_to(run, (TM, TM))
                gsum_scr[hh] = jnp.broadcast_to(gsum, (TM, TM))
                top = jnp.max(run)
                alive = top if alive is None else jnp.maximum(alive, top)
            dk_ref[pl.ds(off, n * TM), :] += dk_win[0]
            dv_ref[pl.ds(off, n * TM), :] += dk_win[1]
            return (alive > SB_DEAD).astype(jnp.int32)

        dq_acc[...] = jnp.zeros_like(dq_acc)
        run_scr[...] = jnp.zeros_like(run_scr)
        gsum_scr[...] = jnp.zeros_like(gsum_scr)

        @pl.when(i >= SB_WINDOW - 1)
        def _():
            key_set(i - (SB_WINDOW - 1), SB_WINDOW, strict)

        start = jnp.where(i >= SB_WINDOW - 1, i - SB_WINDOW, i)
        alive0 = (jnp.max(run_scr[...]) > SB_DEAD).astype(jnp.int32)

        def cond(c):
            return jnp.logical_and(c[0] >= 0, c[1] > 0)

        def step(c):
            kb = c[0]
            return kb - 1, key_set(kb, 1, jnp.logical_or(strict, kb < i))

        lax.while_loop(cond, step, (start, alive0))
        dq_all[pl.ds(pl.multiple_of(i * TM, TM), TM), :] = dq_acc[...] * scale

        @pl.when(i == nb - 1)
        def _():
            copies = []
            for s, src in enumerate((dq_all, dk_ref, dv_ref)):
                stage[s] = src[...].astype(_MXU)
                col = pl.multiple_of((s * npair + p) * TM, TM)
                copies.append(pltpu.make_async_copy(stage.at[s], dproj_ref.at[:, pl.ds(col, TM)], sems.at[s]))
                copies[-1].start()
            for cp in copies:
                cp.wait()

    blk = pl.BlockSpec((TM, TM), lambda p, i: (i, p))
    return pl.pallas_call(
        body, name="sb_bwd", grid=(npair, nb),
        in_specs=[blk,
                  pl.BlockSpec((lp, TM), lambda p, i: (0, npair + p)),
                  pl.BlockSpec((lp, TM), lambda p, i: (0, 2 * npair + p)),
                  blk, blk, blk, pl.BlockSpec(memory_space=pl.ANY)],
        out_specs=pl.BlockSpec(memory_space=pl.ANY),
        out_shape=_SDS(d_proj.shape, d_proj.dtype),
        input_output_aliases={6: 0},
        scratch_shapes=[pltpu.VMEM((lp, TM), F32), pltpu.VMEM((lp, TM), F32), pltpu.VMEM((lp, TM), F32),
                        pltpu.VMEM((3, lp, TM), _MXU), pltpu.SemaphoreType.DMA((3,)),
                        pltpu.VMEM((TM, TM), F32), pltpu.VMEM((2, TM, TM), F32), pltpu.VMEM((2, TM, TM), F32)],
        compiler_params=_params(("arbitrary", "arbitrary")),
    )(qkv, qkv, qkv, o_sb, o_lo, d_o, d_proj)


def _conv_pre(x_ref, w_ref, b_ref, lp):
    n = lp - 8
    w = w_ref[...]
    pre = (x_ref[pl.ds(5, n), :] * w[0:1, :] + x_ref[pl.ds(6, n), :] * w[1:2, :]
           + x_ref[pl.ds(7, n), :] * w[2:3, :] + x_ref[pl.ds(8, n), :] * w[3:4, :]) + b_ref[...]
    live = (_iota((n, 128), 0) + 8) >= PAD
    return pre, live


def _conv_fwd(proj, dt_raw, conv_w, conv_b, dt_bias128):
    lp = proj.shape[0]
    nblk = XBC_W // 128
    c0 = (COL_XBC - QKV_W) // 128

    def body(x_ref, w_ref, b_ref, dtr_ref, dtb_ref, o_ref, dt_ref):
        pre, live = _conv_pre(x_ref, w_ref, b_ref, lp)
        act = pre * _sigmoid(pre)
        o_ref[pl.ds(0, 8), :] = jnp.zeros((8, 128), F32)
        o_ref[pl.ds(8, lp - 8), :] = jnp.where(live, act, 0.0)

        @pl.when(pl.program_id(0) == 0)
        def _():
            s = dtr_ref[...] + dtb_ref[...]
            sp = jnp.maximum(s, 0.0) + jnp.log(1.0 + jnp.exp(-jnp.abs(s)))
            dt_ref[...] = jnp.where(_iota((lp, 128), 0) >= PAD, sp, 0.0)

    return pl.pallas_call(
        body, name="conv_fwd", grid=(nblk,),
        in_specs=[pl.BlockSpec((lp, 128), lambda j: (0, c0 + j)),
                  pl.BlockSpec((4, 128), lambda j: (0, j)),
                  pl.BlockSpec((1, 128), lambda j: (0, j)),
                  pl.BlockSpec((lp, 128), lambda j: (0, 0)),
                  pl.BlockSpec((1, 128), lambda j: (0, 0))],
        out_specs=[pl.BlockSpec((lp, 128), lambda j: (0, j)),
                   pl.BlockSpec((lp, 128), lambda j: (0, 0))],
        out_shape=[_SDS((lp, XBC_W), F32), _SDS((lp, 128), F32)],
        compiler_params=_params(("arbitrary",)),
    )(proj, conv_w, conv_b, dt_raw, dt_bias128)


def _conv_bwd(proj, dt_raw, conv_w, conv_b, dt_bias128, d_xbc, d_dt128, d_proj):
    lp = proj.shape[0]
    nblk = XBC_W // 128
    c0 = COL_XBC // 128
    c0_in = (COL_XBC - QKV_W) // 128
    n = lp - 8
    last = nblk - 1

    def body(x_ref, w_ref, b_ref, dtr_ref, dtb_ref, dy_ref, ddt_ref, dproj_in,
             dx_ref, gw_ref, gb_ref, gdtb_ref, scr):
        j = pl.program_id(0)

        @pl.when(j < nblk)
        def _():
            pre, live = _conv_pre(x_ref, w_ref, b_ref, lp)
            sg = _sigmoid(pre)
            dpre = jnp.where(live, dy_ref[pl.ds(8, n), :] * (sg * (1.0 + pre * (1.0 - sg))), 0.0)
            gb_ref[...] = jnp.sum(dpre, axis=0, keepdims=True)
            gw_ref[...] = jnp.concatenate(
                [jnp.sum(dpre * x_ref[pl.ds(5 + k, n), :], axis=0, keepdims=True) for k in range(4)], axis=0)
            scr[pl.ds(0, 8), :] = jnp.zeros((8, 128), F32)
            scr[pl.ds(8, n), :] = dpre
            scr[pl.ds(lp, 8), :] = jnp.zeros((8, 128), F32)
            w = w_ref[...]
            dx_ref[...] = (scr[pl.ds(0, lp), :] * w[3:4, :] + scr[pl.ds(1, lp), :] * w[2:3, :]
                           + scr[pl.ds(2, lp), :] * w[1:2, :] + scr[pl.ds(3, lp), :] * w[0:1, :]).astype(dx_ref.dtype)

        @pl.when(j == nblk)
        def _():
            s = dtr_ref[...] + dtb_ref[...]
            d = jnp.where(_iota((lp, 128), 0) >= PAD, ddt_ref[...] * _sigmoid(s), 0.0)
            dx_ref[...] = d.astype(dx_ref.dtype)
            gdtb_ref[...] = jnp.sum(d, axis=0, keepdims=True)

    clamp = lambda j: (0, jnp.minimum(j, last))
    full128 = pl.BlockSpec((lp, 128), lambda j: (0, 0))
    return pl.pallas_call(
        body, name="conv_bwd", grid=(nblk + 1,),
        in_specs=[pl.BlockSpec((lp, 128), lambda j: (0, c0_in + jnp.minimum(j, last))),
                  pl.BlockSpec((4, 128), clamp),
                  pl.BlockSpec((1, 128), clamp),
                  full128, pl.BlockSpec((1, 128), lambda j: (0, 0)),
                  pl.BlockSpec((lp, 128), clamp), full128, pl.BlockSpec(memory_space=pl.ANY)],
        out_specs=[pl.BlockSpec((lp, 128), lambda j: (0, c0 + j)), pl.BlockSpec((4, 128), clamp),
                   pl.BlockSpec((1, 128), clamp), pl.BlockSpec((1, 128), lambda j: (0, 0))],
        out_shape=[_SDS(d_proj.shape, d_proj.dtype), _SDS((4, XBC_W), F32), _SDS((1, XBC_W), F32), _SDS((1, 128), F32)],
        input_output_aliases={7: 0},
        scratch_shapes=[pltpu.VMEM((lp + 8, 128), F32)],
        compiler_params=_params(("arbitrary",)),
    )(proj, conv_w, conv_b, dt_raw, dt_bias128, d_xbc, d_dt128, d_proj)


def _ssd_pieces(dt, dt_t, a, a_t):
    r64 = _iota((CHUNK, CHUNK), 0)
    c64 = _iota((CHUNK, CHUNK), 1)
    tril = c64 <= r64
    tril01 = tril.astype(_MXU)
    triu01 = (r64 <= c64).astype(_MXU)
    expand = (lax.shift_right_logical(_iota((N_HEADS, SSD_W), 1), 6) == _iota((N_HEADS, SSD_W), 0)).astype(_MXU)
    acum = _sel_left(tril01, dt * a)
    acum_t = _sel_right(dt_t * a_t, triu01)
    ax = _sel_right(acum, expand)
    dtx = _sel_right(dt, expand)
    return tril, expand, acum, acum_t, ax, dtx


def _seg_matrix():
    return (lax.shift_right_logical(_iota((SSD_W, N_HEADS), 0), 6) == _iota((SSD_W, N_HEADS), 1)).astype(_MXU)


def _head_decay(ax, acum_t, h, tril):
    col = ax[:, h * HEAD:(h + 1) * HEAD]
    rowv = acum_t[h:h + 1, :]
    return jnp.where(tril, jnp.exp(jnp.minimum(col - rowv, 0.0)), 0.0)


def _ssd_fwd(xbc, dt_c, dt_tc, a, a_t, dskip_x):
    lp = xbc.shape[0]
    nc = lp // CHUNK
    gw = SSD_W // N_GROUPS
    hpg = N_HEADS // N_GROUPS

    def body(x_ref, dt_ref, dtt_ref, a_ref, at_ref, d_ref, y_ref, st_ref, state):
        c = pl.program_id(0)

        @pl.when(c == 0)
        def _():
            state[...] = jnp.zeros_like(state)

        st_ref[0] = state[...]
        tril, _, _, acum_t, ax, dtx = _ssd_pieces(dt_ref[0], dtt_ref[0], a_ref[...], at_ref[...])
        x = x_ref[:, 0:SSD_W]
        xdt = x * dtx
        ea = jnp.exp(ax)
        aex = ax[CHUNK - 1:CHUNK, :]
        wd = jnp.exp(aex - ax)
        eae = jnp.exp(aex)
        xw = xdt * wd
        y_ref[...] = x * d_ref[...]
        for g in range(N_GROUPS):
            gs = slice(g * gw, (g + 1) * gw)
            rs = slice(g * N_STATE, (g + 1) * N_STATE)
            bg = x_ref[:, SSD_W + g * N_STATE:SSD_W + (g + 1) * N_STATE]
            cg = x_ref[:, SSD_W + N_GROUPS * N_STATE + g * N_STATE:SSD_W + N_GROUPS * N_STATE + (g + 1) * N_STATE]
            sg = state[rs, :]
            cb = _mm_nt(cg, bg)
            y_ref[:, gs] += _mm(cg, sg) * ea[:, gs]
            for r in range(hpg):
                h = g * hpg + r
                hs = slice(h * HEAD, (h + 1) * HEAD)
                m = cb * _head_decay(ax, acum_t, h, tril)
                y_ref[:, hs] += _mm(m, xdt[:, hs])
            state[rs, :] = sg * eae[:, gs] + _mm_tn(bg, xw[:, gs])

    return pl.pallas_call(
        body, name="ssd_fwd", grid=(nc,),
        in_specs=[pl.BlockSpec((CHUNK, XBC_W), lambda c: (c, 0)),
                  pl.BlockSpec((1, CHUNK, N_HEADS), lambda c: (c, 0, 0)),
                  pl.BlockSpec((1, N_HEADS, CHUNK), lambda c: (c, 0, 0)),
                  pl.BlockSpec((1, N_HEADS), lambda c: (0, 0)),
                  pl.BlockSpec((N_HEADS, 1), lambda c: (0, 0)),
                  pl.BlockSpec((1, SSD_W), lambda c: (0, 0))],
        out_specs=[pl.BlockSpec((CHUNK, SSD_W), lambda c: (c, 0)),
                   pl.BlockSpec((1, N_GROUPS * N_STATE, gw), lambda c: (c, 0, 0))],
        out_shape=[_SDS((lp, SSD_W), F32), _SDS((nc, N_GROUPS * N_STATE, gw), F32)],
        scratch_shapes=[pltpu.VMEM((N_GROUPS * N_STATE, gw), F32)],
        compiler_params=_params(("arbitrary",)),
    )(xbc, dt_c, dt_tc, a, a_t, dskip_x)


def _ssd_bwd(xbc, dt_c, dt_tc, a, a_t, dskip_x, states, d_y):
    lp = xbc.shape[0]
    nc = lp // CHUNK
    gw = SSD_W // N_GROUPS
    hpg = N_HEADS // N_GROUPS

    def body(x_ref, dt_ref, dtt_ref, a_ref, at_ref, d_ref, st_ref, dy_ref,
             dx_ref, ddta_ref, ddtb_ref, ga1_ref, ga2_ref, gd_ref, dstate, dxdt_scr, z_scr, yoff_scr, sds_scr):
        c = pl.program_id(0)

        @pl.when(c == 0)
        def _():
            dstate[...] = jnp.zeros_like(dstate)
            ga1_ref[...] = jnp.zeros_like(ga1_ref)
            ga2_ref[...] = jnp.zeros_like(ga2_ref)
            gd_ref[...] = jnp.zeros_like(gd_ref)

        dt = dt_ref[0]
        dt_t = dtt_ref[0]
        a = a_ref[...]
        a_t = at_ref[...]
        tril, _, acum, acum_t, ax, dtx = _ssd_pieces(dt, dt_t, a, a_t)
        seg = _seg_matrix()
        x = x_ref[:, 0:SSD_W]
        dy = dy_ref[...]
        xdt = x * dtx
        ea = jnp.exp(ax)
        aex = ax[CHUNK - 1:CHUNK, :]
        wd = jnp.exp(aex - ax)
        eae = jnp.exp(aex)
        xw = xdt * wd
        edy = ea * dy
        lane16 = _iota((CHUNK, N_HEADS), 1)
        row16 = _iota((N_HEADS, CHUNK), 0)
        da_col = jnp.zeros((CHUNK, N_HEADS), F32)
        da_row = jnp.zeros((N_HEADS, CHUNK), F32)
        for g in range(N_GROUPS):
            gs = slice(g * gw, (g + 1) * gw)
            rs = slice(g * N_STATE, (g + 1) * N_STATE)
            bcol = slice(SSD_W + g * N_STATE, SSD_W + (g + 1) * N_STATE)
            ccol = slice(SSD_W + N_GROUPS * N_STATE + g * N_STATE, SSD_W + N_GROUPS * N_STATE + (g + 1) * N_STATE)
            bg = x_ref[:, bcol]
            cg = x_ref[:, ccol]
            sg = st_ref[0, rs, :]
            dsn = dstate[rs, :]
            cb = _mm_nt(cg, bg)
            z_scr[:, gs] = _mm(bg, dsn)
            yoff_scr[:, gs] = _mm(cg, sg) * ea[:, gs]
            sds_scr[:, gs] = jnp.broadcast_to(jnp.sum(dsn * sg, axis=0, keepdims=True), (8, gw))
            dcb = jnp.zeros((CHUNK, CHUNK), F32)
            for r in range(hpg):
                h = g * hpg + r
                hs = slice(h * HEAD, (h + 1) * HEAD)
                dec = _head_decay(ax, acum_t, h, tril)
                m = cb * dec
                t1 = _mm_nt(dy[:, hs], xdt[:, hs])
                dcb = dcb + dec * t1
                tm = m * t1
                da_col = da_col + jnp.where(lane16 == h, jnp.sum(tm, axis=1, keepdims=True), 0.0)
                da_row = da_row - jnp.where(row16 == h, jnp.sum(tm, axis=0, keepdims=True), 0.0)
                dxdt_scr[:, hs] = _mm_tn(m, dy[:, hs])
            dx_ref[:, ccol] = _mm(dcb, bg) + _mm_nt(edy[:, gs], sg)
            dx_ref[:, bcol] = _mm_tn(dcb, cg) + _mm_nt(xw[:, gs], dsn)
            dstate[rs, :] = eae[:, gs] * dsn + _mm_tn(cg, edy[:, gs])
        zf = z_scr[...]
        dxdt = dxdt_scr[...] + wd * zf
        t3 = _sel_right(xw * zf, seg)
        da_col = da_col + _sel_right(dy * yoff_scr[...], seg) - t3
        aend = acum[CHUNK - 1:CHUNK, :]
        sd = _sel_right(sds_scr[...], seg)[0:1, :] * jnp.exp(aend)
        last = jnp.sum(t3, axis=0, keepdims=True) + sd
        da_col = da_col + jnp.where(_iota((CHUNK, N_HEADS), 0) == CHUNK - 1, last, 0.0)
        r64 = _iota((CHUNK, CHUNK), 0)
        c64 = _iota((CHUNK, CHUNK), 1)
        ddta1 = _sel_left((c64 >= r64).astype(_MXU), da_col)
        ddta2 = _sel_right(da_row, (r64 >= c64).astype(_MXU))
        ddta_ref[0] = a * ddta1 + _sel_right(dxdt * x, seg)
        ddtb_ref[0] = a_t * ddta2
        ga1_ref[...] += jnp.sum(dt * ddta1, axis=0, keepdims=True)
        ga2_ref[...] += jnp.sum(dt_t * ddta2, axis=1, keepdims=True)
        dx_ref[:, 0:SSD_W] = dxdt * dtx + d_ref[...] * dy
        gd_ref[...] += jnp.sum(dy * x, axis=0, keepdims=True)

    rev = lambda c: (nc - 1 - c, 0)
    rev3 = lambda c: (nc - 1 - c, 0, 0)
    return pl.pallas_call(
        body, name="ssd_bwd", grid=(nc,),
        in_specs=[pl.BlockSpec((CHUNK, XBC_W), rev),
                  pl.BlockSpec((1, CHUNK, N_HEADS), rev3),
                  pl.BlockSpec((1, N_HEADS, CHUNK), rev3),
                  pl.BlockSpec((1, N_HEADS), lambda c: (0, 0)),
                  pl.BlockSpec((N_HEADS, 1), lambda c: (0, 0)),
                  pl.BlockSpec((1, SSD_W), lambda c: (0, 0)),
                  pl.BlockSpec((1, N_GROUPS * N_STATE, gw), rev3),
                  pl.BlockSpec((CHUNK, SSD_W), rev)],
        out_specs=[pl.BlockSpec((CHUNK, XBC_W), rev),
                   pl.BlockSpec((1, CHUNK, N_HEADS), rev3),
                   pl.BlockSpec((1, N_HEADS, CHUNK), rev3),
                   pl.BlockSpec((1, N_HEADS), lambda c: (0, 0)),
                   pl.BlockSpec((N_HEADS, 1), lambda c: (0, 0)),
                   pl.BlockSpec((1, SSD_W), lambda c: (0, 0))],
        out_shape=[_SDS((lp, XBC_W), F32), _SDS((nc, CHUNK, N_HEADS), F32), _SDS((nc, N_HEADS, CHUNK), F32),
                   _SDS((1, N_HEADS), F32), _SDS((N_HEADS, 1), F32), _SDS((1, SSD_W), F32)],
        scratch_shapes=[pltpu.VMEM((N_GROUPS * N_STATE, gw), F32), pltpu.VMEM((CHUNK, SSD_W), F32),
                        pltpu.VMEM((CHUNK, SSD_W), F32), pltpu.VMEM((CHUNK, SSD_W), F32),
                        pltpu.VMEM((8, SSD_W), F32)],
        compiler_params=_params(("arbitrary",)),
    )(xbc, dt_c, dt_tc, a, a_t, dskip_x, states, d_y)


def _gated_norm(o, gate, w):
    sg = _sigmoid(gate)
    p = o * (gate * sg)
    rs = lax.rsqrt(jnp.mean(p * p, axis=-1, keepdims=True) + EPS)
    n = p * rs
    return sg, rs, n, n * w


def _tail_fwd(o_sb, o_ssd, proj, h0, target, w_out, sb_w, ssd_w, fin_w):
    lp = o_sb.shape[0]
    nb = lp // TM
    row = lambda i: (i, 0)
    one = lambda i: (0, 0)

    def body(osb_ref, gate_ref, ossd_ref, z_ref, h0_ref, tgt_ref, wo_ref, sbw_ref, ssdw_ref, fw_ref,
             dh1_ref, loss_ref, gfw_ref):
        i = pl.program_id(0)

        @pl.when(i == 0)
        def _():
            loss_ref[...] = jnp.zeros_like(loss_ref)
            gfw_ref[...] = jnp.zeros_like(gfw_ref)

        y1 = _gated_norm(osb_ref[...], gate_ref[...], sbw_ref[...])[3]
        y2 = _gated_norm(ossd_ref[...], z_ref[...], ssdw_ref[...])[3]
        h1 = (h0_ref[...] + _mm(y1, wo_ref[0:SB_W, :])) + _mm(y2, wo_ref[SB_W:SB_W + SSD_W, :])
        rs1 = lax.rsqrt(jnp.mean(h1 * h1, axis=-1, keepdims=True) + EPS)
        n1 = h1 * rs1
        fw = fw_ref[...]
        diff = jnp.where(i > 0, n1 * fw - tgt_ref[...], 0.0)
        loss_ref[...] += jnp.sum(diff * diff, axis=0, keepdims=True)
        d_out = diff * (1.0 / D_MODEL)
        gfw_ref[...] += jnp.sum(d_out * n1, axis=0, keepdims=True)
        g = d_out * fw
        dh1_ref[...] = rs1 * (g - n1 * jnp.mean(g * n1, axis=-1, keepdims=True))

    return pl.pallas_call(
        body, name="tail_fwd", grid=(nb,),
        in_specs=[pl.BlockSpec((TM, SB_W), row),
                  pl.BlockSpec((TM, SB_W), lambda i: (i, (COL_GATE - QKV_W) // SB_W)),
                  pl.BlockSpec((TM, SSD_W), row),
                  pl.BlockSpec((TM, SSD_W), lambda i: (i, (COL_Z - QKV_W) // SSD_W)),
                  pl.BlockSpec((TM, D_MODEL), row),
                  pl.BlockSpec((TM, D_MODEL), lambda i: (jnp.maximum(i - 1, 0), 0)),
                  pl.BlockSpec(memory_space=_VMEM),
                  pl.BlockSpec((1, SB_W), one), pl.BlockSpec((1, SSD_W), one), pl.BlockSpec((1, D_MODEL), one)],
        out_specs=[pl.BlockSpec((TM, D_MODEL), row), pl.BlockSpec((1, D_MODEL), one), pl.BlockSpec((1, D_MODEL), one)],
        out_shape=[_SDS((lp, D_MODEL), F32), _SDS((1, D_MODEL), F32), _SDS((1, D_MODEL), F32)],
        compiler_params=_params(("arbitrary",), 40),
    )(o_sb, proj, o_ssd, proj, h0, target, w_out, sb_w, ssd_w, fin_w)


def _gated_norm_bwd(o, gate, w, dy):
    sg, rs, n, _ = _gated_norm(o, gate, w)
    gw = jnp.sum(dy * n, axis=0, keepdims=True)
    dn = dy * w
    dp = rs * (dn - n * jnp.mean(dn * n, axis=-1, keepdims=True))
    d_o = dp * (gate * sg)
    d_gate = dp * o * (sg * (1.0 + gate * (1.0 - sg)))
    return d_o, d_gate, gw, n * w


def _tail_bwd(o_sb, o_ssd, proj, d_h1, w_out, sb_w, ssd_w):
    lp = o_sb.shape[0]
    tm = 272 if lp % 272 == 0 else TM
    nb = lp // tm
    row = lambda i, t: (i, 0)
    one = lambda i, t: (0, 0)

    def body(osb_ref, gate_ref, ossd_ref, z_ref, dh1_ref, wo_ref, sbw_ref, ssdw_ref,
             dosb_ref, dossd_ref, dproj_ref, gwo_ref, gsb_ref, gssd_ref):
        i = pl.program_id(0)
        t = pl.program_id(1)

        @pl.when(jnp.logical_and(i == 0, t == 0))
        def _():
            gwo_ref[...] = jnp.zeros_like(gwo_ref)
            gsb_ref[...] = jnp.zeros_like(gsb_ref)
            gssd_ref[...] = jnp.zeros_like(gssd_ref)

        dh1 = dh1_ref[...].astype(_MXU)

        def half(o_ref, g_ref, w_ref, do_ref, gn_ref, r0):
            dy = lax.dot_general(dh1, wo_ref[r0:r0 + SB_W, :], _NT, preferred_element_type=F32)
            d_o, d_g, gw, y = _gated_norm_bwd(o_ref[...], g_ref[...], w_ref[...], dy)
            do_ref[...] = d_o
            dproj_ref[...] = d_g.astype(_MXU)
            gn_ref[...] += gw
            gwo_ref[r0:r0 + SB_W, :] += lax.dot_general(y.astype(_MXU), dh1, _TN, preferred_element_type=F32)

        @pl.when(t == 0)
        def _():
            half(osb_ref, gate_ref, sbw_ref, dosb_ref, gsb_ref, 0)

        @pl.when(t == 1)
        def _():
            half(ossd_ref, z_ref, ssdw_ref, dossd_ref, gssd_ref, SB_W)

    tile = pl.BlockSpec((tm, SB_W), row)
    return pl.pallas_call(
        body, name="tail_bwd", grid=(nb, 2),
        in_specs=[tile, pl.BlockSpec((tm, SB_W), lambda i, t: (i, (COL_GATE - QKV_W) // SB_W)),
                  tile, pl.BlockSpec((tm, SSD_W), lambda i, t: (i, (COL_Z - QKV_W) // SSD_W)),
                  tile, pl.BlockSpec(memory_space=_VMEM),
                  pl.BlockSpec((1, SB_W), one), pl.BlockSpec((1, SSD_W), one)],
        out_specs=[tile, tile, pl.BlockSpec((tm, SB_W), lambda i, t: (i, COL_GATE // SB_W + t)),
                   pl.BlockSpec((SB_W + SSD_W, D_MODEL), one), pl.BlockSpec((1, SB_W), one), pl.BlockSpec((1, SSD_W), one)],
        out_shape=[_SDS((lp, SB_W), F32), _SDS((lp, SSD_W), F32), _SDS((lp, W_ALL), _MXU),
                   _SDS((SB_W + SSD_W, D_MODEL), F32), _SDS((1, SB_W), F32), _SDS((1, SSD_W), F32)],
        compiler_params=_params(("arbitrary", "arbitrary"), 48),
    )(o_sb, proj, o_ssd, proj, d_h1, w_out, sb_w, ssd_w)


def _d_u(d_proj, w_t, send=()):
    lp = d_proj.shape[0]
    tk = 512
    steps = N_MAIN // tk
    n = len(send)

    def body(dp_ref, w_ref, dpdt_ref, wdt_ref, *rest):
        srcs, o_ref, dsts, sems = rest[:n], rest[n], rest[n + 1:2 * n + 1], rest[2 * n + 1:]
        j = pl.program_id(0)

        @pl.when(j == 0)
        def _():
            if n:
                for cp in _slab_copies(srcs, dsts, *sems):
                    cp.start()
            o_ref[...] = jnp.dot(dpdt_ref[...], wdt_ref[...], preferred_element_type=F32)

        o_ref[...] += jnp.dot(dp_ref[...], w_ref[...], preferred_element_type=F32)

        if n:
            @pl.when(j == steps - 1)
            def _():
                copies = _slab_copies(srcs, dsts, *sems)
                for cp in copies:
                    cp.wait_recv()
                for cp in copies:
                    cp.wait_send()

    res = pl.pallas_call(
        body, name="d_u", grid=(steps,),
        in_specs=[pl.BlockSpec((lp, tk), lambda j: (0, j)),
                  pl.BlockSpec((tk, D_MODEL), lambda j: (j, 0)),
                  pl.BlockSpec((lp, 128), lambda j: (0, N_MAIN // 128)),
                  pl.BlockSpec((128, D_MODEL), lambda j: (N_MAIN // 128, 0))] + [pl.BlockSpec(memory_space=pl.ANY)] * n,
        out_specs=[pl.BlockSpec((lp, D_MODEL), lambda j: (0, 0))] + [pl.BlockSpec(memory_space=pl.ANY)] * n,
        out_shape=[_SDS((lp, D_MODEL), F32)] + [_SDS(a.shape, a.dtype) for a in send],
        scratch_shapes=[pltpu.SemaphoreType.DMA((3 * n,)), pltpu.SemaphoreType.DMA((3 * n,))] if n else [],
        compiler_params=_params(("arbitrary",), 48),
    )(d_proj, w_t, d_proj, w_t, *send)
    return res[0], list(res[1:])


def _norm_bwd(du_all, h0, d_h1, norm_w):
    lp = h0.shape[0]
    nb = lp // TM
    seq = lp - OFF
    row = lambda i: (i, 0)
    one = lambda i: (0, 0)

    def body(du_ref, h0_ref, dh1_ref, nw_ref, gx_ref, gmeta_ref, gnw_ref):
        i = pl.program_id(0)

        @pl.when(i == 0)
        def _():
            gnw_ref[...] = jnp.zeros_like(gnw_ref)

        du = du_ref[...]
        h = h0_ref[...]
        rs = lax.rsqrt(jnp.mean(h * h, axis=-1, keepdims=True) + EPS)
        n0 = h * rs
        gnw_ref[...] += jnp.sum(du * n0, axis=0, keepdims=True)
        g = du * nw_ref[...]
        dh0 = dh1_ref[...] + rs * (g - n0 * jnp.mean(g * n0, axis=-1, keepdims=True))

        @pl.when(i == 0)
        def _():
            gmeta_ref[...] = dh0[PAD:PAD + N_META, :]

        @pl.when(i > 0)
        def _():
            gx_ref[...] = dh0

    tile = pl.BlockSpec((TM, D_MODEL), row)
    return pl.pallas_call(
        body, name="norm_bwd", grid=(nb,),
        in_specs=[tile, tile, tile, pl.BlockSpec((1, D_MODEL), one)],
        out_specs=[pl.BlockSpec((TM, D_MODEL), lambda i: (jnp.maximum(i - 1, 0), 0)),
                   pl.BlockSpec((N_META, D_MODEL), one), pl.BlockSpec((1, D_MODEL), one)],
        out_shape=[_SDS((seq, D_MODEL), F32), _SDS((N_META, D_MODEL), F32), _SDS((1, D_MODEL), F32)],
        compiler_params=_params(("arbitrary",)),
    )(du_all, h0, d_h1, norm_w)


def _grad_w_windows(u_t, d_proj):
    lp = d_proj.shape[0]
    hw = WIN_W // 2
    steps = 2 * N_CHIPS

    def body(ut_ref, dp_hbm, o_ref, buf, sems):
        s = pl.program_id(0)
        slot = s % 2

        def fetch(step, sl):
            start = pl.multiple_of((step // 2) * WIN_STEP + (step % 2) * hw, 128)
            return pltpu.make_async_copy(dp_hbm.at[:, pl.ds(start, hw)], buf.at[sl], sems.at[sl])

        @pl.when(s == 0)
        def _():
            fetch(0, 0).start()

        @pl.when(s + 1 < steps)
        def _():
            fetch(s + 1, 1 - slot).start()

        fetch(s, slot).wait()
        o_ref[0] = jnp.dot(ut_ref[...], buf[slot], preferred_element_type=F32).astype(o_ref.dtype)

    return pl.pallas_call(
        body, name="grad_w_in", grid=(steps,),
        in_specs=[pl.BlockSpec((D_MODEL, lp), lambda s: (0, 0)), pl.BlockSpec(memory_space=pl.ANY)],
        out_specs=pl.BlockSpec((1, D_MODEL, hw), lambda s: (s // 2, 0, s % 2)),
        out_shape=_SDS((N_CHIPS, D_MODEL, WIN_W), _MXU),
        scratch_shapes=[pltpu.VMEM((2, lp, hw), _MXU), pltpu.SemaphoreType.DMA((2,))],
        compiler_params=_params(("arbitrary",), 40),
    )(u_t, d_proj)


def _device_grads(x2d, target2d, meta_full, norm_w, w_t, conv_w, conv_b, dt_bias, a_log, d_skip,
                  sb_w, ssd_w, w_out, fin_w, prepare_send=None):
    lp = x2d.shape[0] + OFF
    nc = lp // CHUNK
    h0, u, u_t = _prep(x2d, meta_full, norm_w)
    qkv, proj, dt_raw = _inproj(u, w_t)
    o_sb, o_lo = _sb_fwd(qkv)
    dt_bias128 = jnp.pad(dt_bias, ((0, 0), (0, 128 - N_HEADS)))
    xbc, dt128 = _conv_fwd(proj, dt_raw, conv_w, conv_b, dt_bias128)
    dt_c = dt128[:, :N_HEADS].reshape(nc, CHUNK, N_HEADS)
    dt_tc = jnp.swapaxes(dt_c, 1, 2)
    a = -jnp.exp(a_log)
    a_t = a.reshape(N_HEADS, 1)
    dskip_x = jnp.repeat(d_skip, HEAD, axis=1)
    o_ssd, states = _ssd_fwd(xbc, dt_c, dt_tc, a, a_t, dskip_x)
    d_h1, sq_err, g_fin = _tail_fwd(o_sb, o_ssd, proj, h0, target2d, w_out, sb_w, ssd_w, fin_w)

    d_osb, d_ossd, d_proj, g_wout, g_sb, g_ssd = _tail_bwd(o_sb, o_ssd, proj, d_h1, w_out, sb_w, ssd_w)
    d_proj = _sb_bwd(qkv, o_sb, o_lo, d_osb, d_proj)
    d_xbc_act, ddt_a, ddt_b, ga1, ga2, gd = _ssd_bwd(xbc, dt_c, dt_tc, a, a_t, dskip_x, states, d_ossd)
    d_dt = (ddt_a + jnp.swapaxes(ddt_b, 1, 2)).reshape(lp, N_HEADS)
    d_dt128 = jnp.pad(d_dt, ((0, 0), (0, 128 - N_HEADS)))
    d_proj, g_convw, g_convb, g_dtb128 = _conv_bwd(proj, dt_raw, conv_w, conv_b, dt_bias128, d_xbc_act, d_dt128, d_proj)
    g_win = _grad_w_windows(u_t, d_proj)
    send = () if prepare_send is None else tuple(prepare_send(g_win, g_wout))
    d_u, arrived = _d_u(d_proj, w_t, send)
    g_x, g_meta, g_nw = _norm_bwd(d_u, h0, d_h1, norm_w)
    g_alog = (ga1 + ga2.reshape(1, N_HEADS)) * a
    g_dskip = gd.reshape(N_HEADS, HEAD).sum(axis=1).reshape(1, N_HEADS)
    grads = dict(meta_tokens=g_meta, norm_w=g_nw, w_in=g_win, conv_w=g_convw, conv_b=g_convb,
                 dt_bias=g_dtb128[:, :N_HEADS], a_log=g_alog, d_skip=g_dskip, sb_norm_w=g_sb, ssd_norm_w=g_ssd,
                 w_out=g_wout, final_norm_w=g_fin, sent=send, arrived=arrived)
    return sq_err, g_x, grads


_MESH = pl.DeviceIdType.MESH
_ANY = pl.BlockSpec(memory_space=pl.ANY)


def _place():
    return lax.axis_index("x"), lax.axis_index("y"), lax.axis_index("c")


def _other_chips(x, y):
    return ((1 - x, y), (x, 1 - y), (1 - x, 1 - y))


def _gather_shards(arrays, n_big):
    n = len(arrays)

    def body(*refs):
        srcs, dsts = refs[:n], refs[n:2 * n]
        send_sems, recv_sems, fwd_send, fwd_recv = refs[2 * n:]
        x, y, c = _place()
        mine = 2 * x + y
        chips = _other_chips(x, y)

        def window(a):
            half = arrays[a].shape[1] // 2
            return pl.ds(pl.multiple_of(c * half, 128), half)

        first = []
        for a in range(n):
            for k, (px, py) in enumerate(chips):
                if a < n_big:
                    src, dst = srcs[a].at[:, window(a)], dsts[a].at[mine, :, window(a)]
                else:
                    src, dst = srcs[a], dsts[a].at[mine]
                cp = pltpu.make_async_remote_copy(
                    src_ref=src, dst_ref=dst, send_sem=send_sems.at[a * 3 + k], recv_sem=recv_sems.at[a * 3 + k],
                    device_id=(px, py, c), device_id_type=_MESH)
                cp.start()
                first.append(cp)
        passed = []
        for a in range(n):
            for k, (px, py) in enumerate(chips):
                first[a * 3 + k].wait_recv()
                if a < n_big:
                    landed = dsts[a].at[2 * px + py, :, window(a)]
                    cp = pltpu.make_async_remote_copy(
                        src_ref=landed, dst_ref=landed, send_sem=fwd_send.at[a * 3 + k], recv_sem=fwd_recv.at[a * 3 + k],
                        device_id=(x, y, 1 - c), device_id_type=_MESH)
                    cp.start()
                    passed.append(cp)
        for cp in passed:
            cp.wait_recv()
        for cp in first + passed:
            cp.wait_send()

    got = pl.pallas_call(
        body, name="gather_shards",
        in_specs=[_ANY] * n, out_specs=[_ANY] * n,
        out_shape=[_SDS((N_CHIPS,) + a.shape, a.dtype) for a in arrays],
        scratch_shapes=[pltpu.SemaphoreType.DMA((3 * n,)), pltpu.SemaphoreType.DMA((3 * n,)),
                        pltpu.SemaphoreType.DMA((3 * n_big,)), pltpu.SemaphoreType.DMA((3 * n_big,))],
    )(*arrays)
    mine = 2 * lax.axis_index("x") + lax.axis_index("y")
    return [lax.dynamic_update_slice(g, a[None], (mine,) + (0,) * a.ndim) for g, a in zip(got, arrays)]


def _slab_copies(srcs, dsts, send_sems, recv_sems):
    x, y, c = _place()
    mine = 2 * x + y
    copies = []
    for a in range(len(srcs)):
        for k, (px, py) in enumerate(_other_chips(x, y)):
            copies.append(pltpu.make_async_remote_copy(
                src_ref=srcs[a].at[2 * px + py], dst_ref=dsts[a].at[mine],
                send_sem=send_sems.at[a * 3 + k], recv_sem=recv_sems.at[a * 3 + k],
                device_id=(px, py, c), device_id_type=_MESH))
    return copies


def _swap_halves(arrays):
    n = len(arrays)

    def body(*refs):
        srcs, dsts = refs[:n], refs[n:2 * n]
        send_sems, recv_sems = refs[2 * n:]
        x, y, c = _place()
        copies = []
        for a in range(n):
            half = arrays[a].shape[1] // 2
            cp = pltpu.make_async_remote_copy(
                src_ref=srcs[a].at[:, pl.ds(pl.multiple_of((1 - c) * half, 16), half)], dst_ref=dsts[a],
                send_sem=send_sems.at[a], recv_sem=recv_sems.at[a],
                device_id=(x, y, 1 - c), device_id_type=_MESH)
            cp.start()
            copies.append(cp)
        for cp in copies:
            cp.wait_recv()
        for cp in copies:
            cp.wait_send()

    return pl.pallas_call(
        body, name="swap_halves",
        in_specs=[_ANY] * n, out_specs=[_ANY] * n,
        out_shape=[_SDS((a.shape[0], a.shape[1] // 2, a.shape[2]), a.dtype) for a in arrays],
        scratch_shapes=[pltpu.SemaphoreType.DMA((n,)), pltpu.SemaphoreType.DMA((n,))],
    )(*arrays)


def _join_halves(arrays, by_cols):
    n = len(arrays)

    def body(*refs):
        dsts = refs[n:2 * n]
        send_sems, recv_sems = refs[2 * n:]
        x, y, c = _place()
        copies = []
        for a in range(n):
            if by_cols[a]:
                half = arrays[a].shape[1] // 2
                mine = dsts[a].at[:, pl.ds(pl.multiple_of(c * half, 128), half)]
            else:
                half = arrays[a].shape[0] // 2
                mine = dsts[a].at[pl.ds(pl.multiple_of(c * half, 16), half)]
            cp = pltpu.make_async_remote_copy(
                src_ref=mine, dst_ref=mine, send_sem=send_sems.at[a], recv_sem=recv_sems.at[a],
                device_id=(x, y, 1 - c), device_id_type=_MESH)
            cp.start()
            copies.append(cp)
        for cp in copies:
            cp.wait_recv()
        for cp in copies:
            cp.wait_send()

    return pl.pallas_call(
        body, name="join_halves",
        in_specs=[_ANY] * n, out_specs=[_ANY] * n,
        out_shape=[_SDS(a.shape, a.dtype) for a in arrays],
        input_output_aliases={a: a for a in range(n)},
        scratch_shapes=[pltpu.SemaphoreType.DMA((n,)), pltpu.SemaphoreType.DMA((n,))],
    )(*arrays)


N_DEV = 8
SMALL_ROWS = 32
SMALL_COLS = XBC_W


def _gather_small(packed):
    def body(src_ref, dst_ref, send_sems, recv_sems, local_sem):
        x, y, c = _place()
        me = 4 * x + 2 * y + c
        own = pltpu.make_async_copy(src_ref, dst_ref.at[me], local_sem)
        own.start()
        copies = []
        for k in range(1, N_DEV):
            bx, by, bc = (k >> 2) & 1, (k >> 1) & 1, k & 1
            peer = (x + bx - 2 * x * bx, y + by - 2 * y * by, c + bc - 2 * c * bc)
            cp = pltpu.make_async_remote_copy(
                src_ref=src_ref, dst_ref=dst_ref.at[me], send_sem=send_sems.at[k - 1], recv_sem=recv_sems.at[k - 1],
                device_id=peer, device_id_type=_MESH)
            cp.start()
            copies.append(cp)
        for cp in copies:
            cp.wait_recv()
        for cp in copies:
            cp.wait_send()
        own.wait()

    return pl.pallas_call(
        body, name="gather_small",
        in_specs=[pl.BlockSpec(memory_space=_VMEM)], out_specs=pl.BlockSpec(memory_space=_VMEM),
        out_shape=_SDS((N_DEV, SMALL_ROWS, SMALL_COLS), F32),
        scratch_shapes=[pltpu.SemaphoreType.DMA((N_DEV - 1,)), pltpu.SemaphoreType.DMA((N_DEV - 1,)),
                        pltpu.SemaphoreType.DMA],
    )(packed)


def _adamw(w, g, m, v):
    m = ADAM_B1 * m + (1.0 - ADAM_B1) * g
    v = ADAM_B2 * v + (1.0 - ADAM_B2) * (g * g)
    m_hat = m / (1.0 - ADAM_B1 ** ADAM_STEP)
    v_hat = v / (1.0 - ADAM_B2 ** ADAM_STEP)
    delta = -ADAM_LR * (m_hat / (jnp.sqrt(v_hat) + ADAM_EPS) + ADAM_WD * w)
    return delta, m, v


def _sum_slabs(slabs, core, name, transposed=False):
    _, h, c = slabs.shape
    tr = 128
    nblk = h // tr

    def body(core_ref, s_ref, o_ref):
        tot = ((s_ref[0].astype(F32) + s_ref[1].astype(F32)) + s_ref[2].astype(F32)) + s_ref[3].astype(F32)
        o_ref[...] = tot.T if transposed else tot

    if transposed:
        out_spec = pl.BlockSpec((c, tr), lambda i, core_ref: (0, core_ref[0] * nblk + i))
        out_shape = _SDS((c, 2 * h), F32)
    else:
        out_spec = pl.BlockSpec((tr, c), lambda i, core_ref: (core_ref[0] * nblk + i, 0))
        out_shape = _SDS((2 * h, c), F32)
    grid_spec = pltpu.PrefetchScalarGridSpec(
        num_scalar_prefetch=1, grid=(nblk,),
        in_specs=[pl.BlockSpec((N_CHIPS, tr, c), lambda i, core_ref: (0, i, 0))],
        out_specs=out_spec)
    return pl.pallas_call(
        body, name=name, grid_spec=grid_spec, out_shape=out_shape,
        compiler_params=_params(("arbitrary",)),
    )(core, slabs)


def _add_halves(own, recv, core, name):
    _, r, c = own.shape
    half = r // 2
    tr = 128
    nblk = half // tr

    def body(core_ref, a_ref, b_ref, o_ref):
        o_ref[...] = (a_ref[...].astype(F32) + b_ref[...].astype(F32)).astype(o_ref.dtype)

    grid_spec = pltpu.PrefetchScalarGridSpec(
        num_scalar_prefetch=1, grid=(nblk,),
        in_specs=[pl.BlockSpec((N_CHIPS, tr, c), lambda i, core_ref: (0, core_ref[0] * nblk + i, 0)),
                  pl.BlockSpec((N_CHIPS, tr, c), lambda i, core_ref: (0, i, 0))],
        out_specs=pl.BlockSpec((N_CHIPS, tr, c), lambda i, core_ref: (0, i, 0)))
    return pl.pallas_call(
        body, name=name, grid_spec=grid_spec, out_shape=_SDS((N_CHIPS, half, c), own.dtype),
        compiler_params=_params(("arbitrary",)),
    )(core, own, recv)


def _update_big(w, m, v, g, name):
    r, c = w.shape

    def body(w_ref, m_ref, v_ref, g_ref, d_ref, mo_ref, vo_ref):
        delta, m_new, v_new = _adamw(w_ref[...], g_ref[...], m_ref[...], v_ref[...])
        d_ref[...] = delta
        mo_ref[...] = m_new
        vo_ref[...] = v_new

    if r % 128 == 0:
        steps, spec = r // 128, pl.BlockSpec((128, c), lambda i: (i, 0))
    else:
        steps, spec = c // 128, pl.BlockSpec((r, 128), lambda i: (0, i))
    return pl.pallas_call(
        body, name=name, grid=(steps,),
        in_specs=[spec] * 4, out_specs=[spec] * 3,
        out_shape=[_SDS((r, c), F32)] * 3,
        compiler_params=_params(("arbitrary",)),
    )(w, m, v, g)


_ROW = dict(norm_w=0, sb_norm_w=1, ssd_norm_w=2, final_norm_w=3, conv_b=4, dt_bias=5, a_log=6, d_skip=7,
            conv_w=8, sq_err=12, meta_tokens=16)
_SMALL = ("meta_tokens", "norm_w", "conv_w", "conv_b", "dt_bias", "a_log", "d_skip", "sb_norm_w", "ssd_norm_w",
          "final_norm_w")


def _pack_small(sq_err, grads):
    def rowpad(a):
        return jnp.pad(a, ((0, 0), (0, SMALL_COLS - a.shape[1])))

    rows = [rowpad(grads[k]) for k in ("norm_w", "sb_norm_w", "ssd_norm_w", "final_norm_w", "conv_b", "dt_bias", "a_log", "d_skip")]
    rows.append(grads["conv_w"])
    rows.append(rowpad(sq_err))
    rows.append(jnp.zeros((3, SMALL_COLS), F32))
    rows.append(rowpad(grads["meta_tokens"]))
    return jnp.concatenate(rows, axis=0)


def _update_small(gathered, ws, ms, vs):
    names = _SMALL
    n = len(names)

    def body(*refs):
        g_ref = refs[0]
        w_refs, m_refs, v_refs = refs[1:1 + n], refs[1 + n:1 + 2 * n], refs[1 + 2 * n:1 + 3 * n]
        outs = refs[1 + 3 * n:]
        loss_ref = outs[0]
        go, do, mo, vo = outs[1:1 + n], outs[1 + n:1 + 2 * n], outs[1 + 2 * n:1 + 3 * n], outs[1 + 3 * n:1 + 4 * n]
        tot = g_ref[0]
        for d in range(1, N_DEV):
            tot = tot + g_ref[d]
        x, y, _ = _place()
        chip = 2 * x + y
        loss_ref[...] = jnp.broadcast_to(
            0.5 * jnp.sum(tot[_ROW["sq_err"]:_ROW["sq_err"] + 1, 0:D_MODEL], axis=1, keepdims=True) / D_MODEL, (1, 128))
        for idx, nm in enumerate(names):
            r0 = _ROW[nm]
            rows, cols = w_refs[idx].shape
            if nm in ("conv_w", "meta_tokens"):
                g = jnp.zeros((rows, cols), F32)
                for j in range(N_CHIPS):
                    g = g + jnp.where(chip == j, tot[r0:r0 + rows, j * cols:(j + 1) * cols], 0.0)
            else:
                g = tot[r0:r0 + rows, 0:cols]
            delta, m_new, v_new = _adamw(w_refs[idx][...], g, m_refs[idx][...], v_refs[idx][...])
            go[idx][...] = g
            do[idx][...] = delta
            mo[idx][...] = m_new
            vo[idx][...] = v_new

    shapes = [_SDS(ws[nm].shape, F32) for nm in names]
    vm = pl.BlockSpec(memory_space=_VMEM)
    res = pl.pallas_call(
        body, name="update_small",
        in_specs=[vm] * (1 + 3 * n), out_specs=[vm] * (1 + 4 * n),
        out_shape=[_SDS((1, 128), F32)] + shapes * 4,
    )(gathered, *[ws[nm] for nm in names], *[ms[nm] for nm in names], *[vs[nm] for nm in names])
    loss = res[0][0, 0]
    g = dict(zip(names, res[1:1 + n]))
    d = dict(zip(names, res[1 + n:1 + 2 * n]))
    m = dict(zip(names, res[1 + 2 * n:1 + 3 * n]))
    v = dict(zip(names, res[1 + 3 * n:1 + 4 * n]))
    return loss, g, d, m, v


_WEIGHTS = ("meta_tokens", "norm_w", "w_in", "conv_w", "conv_b", "dt_bias", "a_log", "d_skip", "sb_norm_w",
            "ssd_norm_w", "w_out", "final_norm_w")


def kernel(x, meta_tokens, norm_w, w_in, conv_w, conv_b, dt_bias, a_log, d_skip, sb_norm_w, ssd_norm_w, w_out, final_norm_w, loss_target, m_meta_tokens, m_norm_w, m_w_in, m_conv_w, m_conv_b, m_dt_bias, m_a_log, m_d_skip, m_sb_norm_w, m_ssd_norm_w, m_w_out, m_final_norm_w, v_meta_tokens, v_norm_w, v_w_in, v_conv_w, v_conv_b, v_dt_bias, v_a_log, v_d_skip, v_sb_norm_w, v_ssd_norm_w, v_w_out, v_final_norm_w):
    given = dict(meta_tokens=meta_tokens, norm_w=norm_w, w_in=w_in, conv_w=conv_w, conv_b=conv_b, dt_bias=dt_bias,
                 a_log=a_log, d_skip=d_skip, sb_norm_w=sb_norm_w, ssd_norm_w=ssd_norm_w, w_out=w_out,
                 final_norm_w=final_norm_w)
    mom = dict(meta_tokens=m_meta_tokens, norm_w=m_norm_w, w_in=m_w_in, conv_w=m_conv_w, conv_b=m_conv_b,
               dt_bias=m_dt_bias, a_log=m_a_log, d_skip=m_d_skip, sb_norm_w=m_sb_norm_w, ssd_norm_w=m_ssd_norm_w,
               w_out=m_w_out, final_norm_w=m_final_norm_w)
    var = dict(meta_tokens=v_meta_tokens, norm_w=v_norm_w, w_in=v_w_in, conv_w=v_conv_w, conv_b=v_conv_b,
               dt_bias=v_dt_bias, a_log=v_a_log, d_skip=v_d_skip, sb_norm_w=v_sb_norm_w, ssd_norm_w=v_ssd_norm_w,
               w_out=v_w_out, final_norm_w=v_final_norm_w)
    seq = x.shape[1]

    def two_d(a):
        return a.reshape((-1, a.shape[-1])) if a.ndim != 2 else a

    def rows_first(a):
        return jnp.transpose(a, (2, 0, 1)).reshape(W_IN_SHARD, D_MODEL)

    def rows_last(a):
        return jnp.transpose(a.reshape(W_IN_SHARD, 1, D_MODEL), (1, 2, 0))

    w_in_t, m_in_t, v_in_t = rows_first(w_in), rows_first(m_w_in), rows_first(v_w_in)

    g_win, g_wout, g_meta, g_cw = _gather_shards(
        [w_in_t.astype(_MXU), w_out[0].astype(_MXU), meta_tokens, conv_w[0]], 2)
    w_t = jnp.pad(g_win.reshape(D_IN, D_MODEL), ((0, W_ALL - D_IN), (0, 0)))
    w_out_full = g_wout.reshape(2 * D_MODEL, D_MODEL)
    meta_full = jnp.swapaxes(g_meta, 0, 1).reshape(N_META, D_MODEL)
    conv_w_full = jnp.swapaxes(g_cw, 0, 1).reshape(4, XBC_W)

    core = lax.axis_index("c").astype(jnp.int32).reshape(1)

    def chip_sums(g_win_slabs, g_wout):
        slab_out = g_wout.reshape(N_CHIPS, W_OUT_SHARD, D_MODEL).astype(_MXU)
        sib_in, sib_out = _swap_halves([g_win_slabs, slab_out])
        return (_add_halves(g_win_slabs, sib_in, core, "chip_sum_w_in"),
                _add_halves(slab_out, sib_out, core, "chip_sum_w_out"))

    sq_err, g_x, grads = _device_grads(
        x.reshape(seq, D_MODEL), loss_target.reshape(seq, D_MODEL), meta_full, norm_w, w_t, conv_w_full,
        conv_b, dt_bias, a_log, d_skip, sb_norm_w, ssd_norm_w, w_out_full, final_norm_w.reshape(1, D_MODEL),
        prepare_send=chip_sums)
    chip_in, chip_out = grads["sent"]
    got_in, got_out = grads["arrived"]
    chip = 2 * lax.axis_index("x") + lax.axis_index("y")

    def with_own(got, sent):
        own = lax.dynamic_slice(sent, (chip, 0, 0), (1,) + sent.shape[1:])
        return lax.dynamic_update_slice(got, own, (chip, 0, 0))

    g_in, g_out = _join_halves([_sum_slabs(with_own(got_in, chip_in), core, "sum_w_in", transposed=True),
                                _sum_slabs(with_own(got_out, chip_out), core, "sum_w_out")], (True, False))
    g_in = lax.dynamic_slice(g_in, (4 * chip, 0), (W_IN_SHARD, D_MODEL))
    big = dict(w_in=tuple(rows_last(a) for a in (g_in,) + tuple(_update_big(w_in_t, m_in_t, v_in_t, g_in, "update_w_in"))),
               w_out=(g_out,) + tuple(_update_big(w_out[0], m_w_out[0], v_w_out[0], g_out, "update_w_out")))

    gathered = _gather_small(_pack_small(sq_err, grads))
    loss, sg, sd, sm, sv = _update_small(
        gathered, {k: two_d(given[k]) for k in _SMALL}, {k: two_d(mom[k]) for k in _SMALL},
        {k: two_d(var[k]) for k in _SMALL})

    out = {}
    for idx, group in enumerate((sg, sd, sm, sv)):
        for k in _SMALL:
            out[(idx, k)] = group[k].reshape(given[k].shape)
        for k in ("w_in", "w_out"):
            out[(idx, k)] = big[k][idx].reshape(given[k].shape)
    return (loss, g_x.reshape(x.shape), *[out[(idx, k)] for idx in range(4) for k in _WEIGHTS])
```

```python
import functools
import math

import jax
import jax.numpy as jnp
from jax import lax
from jax.experimental import pallas as pl
from jax.experimental.pallas import tpu as pltpu

F32 = jnp.float32
_MXU = jnp.bfloat16

D_MODEL = 1024
N_META = 16
PAD = 112
OFF = PAD + N_META
TM = 128
CHUNK = 64
SB_W = 1024
SSD_W = 1024
N_HEADS = 16
HEAD = 64
N_GROUPS = 2
N_STATE = 128
XBC_W = SSD_W + 2 * N_GROUPS * N_STATE
N_MAIN = 4 * SB_W + SSD_W + XBC_W
QKV_W = 3 * SB_W
REST_W = N_MAIN - QKV_W
COL_GATE = 3 * SB_W
COL_Z = 4 * SB_W
COL_XBC = 5 * SB_W
D_IN = N_MAIN + N_HEADS
W_ALL = N_MAIN + 128
WIN_STEP = 1664
WIN_W = 1792
EPS = 1e-5
N_CHIPS = 4
W_IN_SHARD = D_IN // N_CHIPS
W_OUT_SHARD = 2 * D_MODEL // N_CHIPS

ADAM_LR = 0.001
ADAM_B1 = 0.9
ADAM_B2 = 0.999
ADAM_EPS = 1e-08
ADAM_WD = 0.01
ADAM_STEP = 10

_SDS = jax.ShapeDtypeStruct
_NT = (((1,), (1,)), ((), ()))
_TN = (((0,), (0,)), ((), ()))
_VMEM = pltpu.VMEM


def _params(sem=None, vmem_mb=None):
    kw = {}
    if sem is not None:
        kw["dimension_semantics"] = sem
    if vmem_mb is not None:
        kw["vmem_limit_bytes"] = vmem_mb * 1024 * 1024
    return pltpu.CompilerParams(**kw)


def _mm(a, b):
    return jnp.dot(a.astype(_MXU), b.astype(_MXU), preferred_element_type=F32)


def _mm_nt(a, b):
    return lax.dot_general(a.astype(_MXU), b.astype(_MXU), _NT, preferred_element_type=F32)


def _mm_tn(a, b):
    return lax.dot_general(a.astype(_MXU), b.astype(_MXU), _TN, preferred_element_type=F32)


def _split(x, parts):
    out = []
    r = x
    for _ in range(parts):
        p = r.astype(_MXU)
        out.append(p)
        r = r - p.astype(F32)
    return out


def _sel_right(x, m01, parts=3):
    acc = None
    for p in _split(x, parts):
        t = jnp.dot(p, m01, preferred_element_type=F32)
        acc = t if acc is None else acc + t
    return acc


def _sel_left(m01, x, parts=3):
    acc = None
    for p in _split(x, parts):
        t = jnp.dot(m01, p, preferred_element_type=F32)
        acc = t if acc is None else acc + t
    return acc


def _iota(shape, axis):
    return lax.broadcasted_iota(jnp.int32, shape, axis)


def _sigmoid(x):
    return 1.0 / (1.0 + jnp.exp(-x))


def _prep(x2d, meta_full, norm_w):
    seq = x2d.shape[0]
    lp = seq + OFF
    nb = lp // TM

    def body(x_ref, meta_ref, w_ref, h0_ref, u_ref, ut_ref):
        i = pl.program_id(0)

        @pl.when(i == 0)
        def _():
            h0_ref[...] = jnp.concatenate([jnp.zeros((PAD, D_MODEL), F32), meta_ref[...]], axis=0)

        @pl.when(i > 0)
        def _():
            h0_ref[...] = x_ref[...]

        h = h0_ref[...]
        rs = lax.rsqrt(jnp.mean(h * h, axis=-1, keepdims=True) + EPS)
        u = (h * rs * w_ref[...]).astype(_MXU)
        u_ref[...] = u
        ut_ref[...] = u.T

    return pl.pallas_call(
        body, name="prep", grid=(nb,),
        in_specs=[pl.BlockSpec((TM, D_MODEL), lambda i: (jnp.maximum(i - 1, 0), 0)),
                  pl.BlockSpec((N_META, D_MODEL), lambda i: (0, 0)),
                  pl.BlockSpec((1, D_MODEL), lambda i: (0, 0))],
        out_specs=[pl.BlockSpec((TM, D_MODEL), lambda i: (i, 0)),
                   pl.BlockSpec((TM, D_MODEL), lambda i: (i, 0)),
                   pl.BlockSpec((D_MODEL, TM), lambda i: (0, i))],
        out_shape=[_SDS((lp, D_MODEL), F32), _SDS((lp, D_MODEL), _MXU), _SDS((D_MODEL, lp), _MXU)],
        compiler_params=_params(("arbitrary",)),
    )(x2d, meta_full, norm_w)


def _inproj(u, w_t):
    lp = u.shape[0]
    tn = 512

    nq = QKV_W // tn

    def body(u_ref, w_ref, wdt_ref, qkv_ref, rest_ref, odt_ref):
        j = pl.program_id(0)
        res = lax.dot_general(u_ref[...], w_ref[...], _NT, preferred_element_type=F32)

        @pl.when(j < nq)
        def _():
            qkv_ref[...] = res.astype(qkv_ref.dtype)

        @pl.when(j >= nq)
        def _():
            rest_ref[...] = res

        @pl.when(j == 0)
        def _():
            odt_ref[...] = lax.dot_general(u_ref[...], wdt_ref[...], _NT, preferred_element_type=F32)

    return pl.pallas_call(
        body, name="inproj", grid=(N_MAIN // tn,),
        in_specs=[pl.BlockSpec((lp, D_MODEL), lambda j: (0, 0)),
                  pl.BlockSpec((tn, D_MODEL), lambda j: (j, 0)),
                  pl.BlockSpec((128, D_MODEL), lambda j: (N_MAIN // 128, 0))],
        out_specs=[pl.BlockSpec((lp, tn), lambda j: (0, jnp.minimum(j, nq - 1))),
                   pl.BlockSpec((lp, tn), lambda j: (0, jnp.maximum(j - nq, 0))),
                   pl.BlockSpec((lp, 128), lambda j: (0, 0))],
        out_shape=[_SDS((lp, QKV_W), _MXU), _SDS((lp, REST_W), F32), _SDS((lp, 128), F32)],
        compiler_params=_params(("arbitrary",), 48),
    )(u, w_t, w_t)


SB_WINDOW = 3
SB_DEAD = -104.0


def _sb_logs(qh, kwin):
    z = lax.dot_general(qh, kwin, _NT, preferred_element_type=F32)
    e = jnp.exp(-jnp.abs(z))
    l1p = jnp.log(1.0 + e)
    lk_full = -(jnp.maximum(z, 0.0) + l1p)
    ls = jnp.minimum(z, 0.0) - l1p
    return z, e, ls, lk_full


def _blk(a, b):
    return a[:, b * TM:(b + 1) * TM]


def _stacked_sel(blocks, m01):
    n = len(blocks)
    pieces = [_split(b, 2) for b in blocks]
    stacked = jnp.concatenate([p[0] for p in pieces] + [p[1] for p in pieces], axis=0)
    res = jnp.dot(stacked, m01, preferred_element_type=F32)
    return [res[j * TM:(j + 1) * TM] + res[(n + j) * TM:(n + j + 1) * TM] for j in range(n)]


def _sb_weights(ls, lk_full, run, last_mask, upper, n):
    lk = [_blk(lk_full, b) for b in range(n)]
    lk[n - 1] = jnp.where(last_mask, lk[n - 1], 0.0)
    aft = _stacked_sel(lk, upper)
    w = [None] * n
    for b in range(n - 1, -1, -1):
        wb = jnp.exp(_blk(ls, b) + aft[b] + run)
        w[b] = jnp.where(last_mask, wb, 0.0) if b == n - 1 else wb
        run = run + jnp.sum(lk[b], axis=1, keepdims=True)
    return w, run


def _sb_fwd(qkv):
    lp = qkv.shape[0]
    nb = lp // TM

    def body(q_ref, k_ref, v_ref, o_ref, olo_ref, acc, run_scr):
        i = pl.program_id(1)
        lane = _iota((TM, TM), 1)
        row = _iota((TM, TM), 0)
        head0 = lane < HEAD
        upper = (row > lane).astype(_MXU)
        strict = lane < row
        q = q_ref[...] * (1.0 / math.sqrt(HEAD))
        qh = (jnp.where(head0, q, 0.0).astype(_MXU), jnp.where(head0, 0.0, q).astype(_MXU))

        def key_set(first, n, last_mask):
            off = pl.multiple_of(first * TM, TM)
            kwin = k_ref[pl.ds(off, n * TM), :].astype(_MXU)
            vwin = v_ref[pl.ds(off, n * TM), :].astype(_MXU)
            alive = None
            for hh in range(2):
                run = run_scr[hh][:, 0:1]
                _, _, ls, lk_full = _sb_logs(qh[hh], kwin)
                w, run = _sb_weights(ls, lk_full, run, last_mask, upper, n)
                pieces = [_split(wb, 2) for wb in w]
                stacked = jnp.concatenate(
                    [jnp.concatenate([p[0] for p in pieces], axis=1), jnp.concatenate([p[1] for p in pieces], axis=1)], axis=0)
                res = jnp.dot(stacked, vwin, preferred_element_type=F32)
                acc[hh] += res[0:TM]
                acc[2 + hh] += res[TM:2 * TM]
                run_scr[hh] = jnp.broadcast_to(run, (TM, TM))
                top = jnp.max(run)
                alive = top if alive is None else jnp.maximum(alive, top)
            return (alive > SB_DEAD).astype(jnp.int32)

        acc[...] = jnp.zeros_like(acc)
        run_scr[...] = jnp.zeros_like(run_scr)

        @pl.when(i >= SB_WINDOW - 1)
        def _():
            key_set(i - (SB_WINDOW - 1), SB_WINDOW, strict)

        start = jnp.where(i >= SB_WINDOW - 1, i - SB_WINDOW, i)
        alive0 = (jnp.max(run_scr[...]) > SB_DEAD).astype(jnp.int32)

        def cond(c):
            return jnp.logical_and(c[0] >= 0, c[1] > 0)

        def step(c):
            kb = c[0]
            return kb - 1, key_set(kb, 1, jnp.logical_or(strict, kb < i))

        lax.while_loop(cond, step, (start, alive0))
        o_ref[...] = jnp.where(head0, acc[0], acc[1])
        olo_ref[...] = jnp.where(head0, acc[2], acc[3])

    npair = SB_W // TM
    blk = pl.BlockSpec((TM, TM), lambda p, i: (i, p))
    return pl.pallas_call(
        body, name="sb_fwd", grid=(npair, nb),
        in_specs=[blk,
                  pl.BlockSpec((lp, TM), lambda p, i: (0, npair + p)),
                  pl.BlockSpec((lp, TM), lambda p, i: (0, 2 * npair + p))],
        out_specs=[blk, blk],
        out_shape=[_SDS((lp, SB_W), F32), _SDS((lp, SB_W), F32)],
        scratch_shapes=[pltpu.VMEM((4, TM, TM), F32), pltpu.VMEM((2, TM, TM), F32)],
        compiler_params=_params(("arbitrary", "arbitrary")),
    )(qkv, qkv, qkv)


def _sb_bwd(qkv, o_sb, o_lo, d_o, d_proj, send=(), dests=()):
    lp = qkv.shape[0]
    nb = lp // TM
    npair = SB_W // TM
    scale = 1.0 / math.sqrt(HEAD)
    n = len(send)
    host_in, host_out, host_shapes, host_sems = _host_specs(send)

    def body(q_ref, k_ref, v_ref, o_ref, olo_ref, do_ref, dproj_in, *rest):
        srcs, dproj_ref, dsts = rest[:n], rest[n], rest[n + 1:2 * n + 1]
        dq_all, dk_ref, dv_ref, stage, sems, dq_acc, run_scr, gsum_scr = rest[2 * n + 1:2 * n + 9]
        host_sem_refs = rest[2 * n + 9:]
        p = pl.program_id(0)
        i = pl.program_id(1)

        if n:
            @pl.when(jnp.logical_and(p == 0, i == 0))
            def _():
                _start_copies(_slab_copies(srcs, dsts, dests, *host_sem_refs))

        @pl.when(i == 0)
        def _():
            dk_ref[...] = jnp.zeros_like(dk_ref)
            dv_ref[...] = jnp.zeros_like(dv_ref)

        lane = _iota((TM, TM), 1)
        row = _iota((TM, TM), 0)
        head0 = lane < HEAD
        hmask = (head0, jnp.logical_not(head0))
        upper = (row > lane).astype(_MXU)
        lower_incl = (row >= lane).astype(_MXU)
        strict = lane < row
        q = q_ref[...] * scale
        do = do_ref[...]
        prod = do.astype(_MXU).astype(F32) * (o_ref[...] + olo_ref[...])
        qh = tuple(jnp.where(m, q, 0.0).astype(_MXU) for m in hmask)
        doh = tuple(jnp.where(m, do, 0.0).astype(_MXU) for m in hmask)
        gtot = tuple(jnp.sum(jnp.where(m, prod, 0.0), axis=1, keepdims=True) for m in hmask)

        def key_set(first, n, last_mask):
            off = pl.multiple_of(first * TM, TM)
            kf = k_ref[pl.ds(off, n * TM), :]
            kwin = kf.astype(_MXU)
            vwin = v_ref[pl.ds(off, n * TM), :].astype(_MXU)
            dk_win = None
            alive = None
            for hh in range(2):
                run = run_scr[hh][:, 0:1]
                gsum = gsum_scr[hh][:, 0:1]
                z, e, ls, lk_full = _sb_logs(qh[hh], kwin)
                w, run = _sb_weights(ls, lk_full, run, last_mask, upper, n)
                r = 1.0 / (1.0 + e)
                er = e * r
                pos = z >= 0.0
                beta = jnp.where(pos, r, er)
                one_m_beta = jnp.where(pos, er, r)
                dw = lax.dot_general(doh[hh], vwin, _NT, preferred_element_type=F32)
                g = [_blk(dw, b) * w[b] for b in range(n)]
                suffix = _stacked_sel(g, lower_incl)
                dz = [None] * n
                for b in range(n - 1, -1, -1):
                    prefix = gtot[hh] - gsum - suffix[b]
                    d = g[b] * _blk(one_m_beta, b) - _blk(beta, b) * prefix
                    dz[b] = (jnp.where(last_mask, d, 0.0) if b == n - 1 else d).astype(_MXU)
                    gsum = gsum + jnp.sum(g[b], axis=1, keepdims=True)
                dzw = jnp.concatenate(dz, axis=1)
                ww = jnp.concatenate([wb.astype(_MXU) for wb in w], axis=1)
                kh = jnp.where(hmask[hh][0:1, :], kf, 0.0).astype(_MXU)
                dq_acc[...] += jnp.dot(dzw, kh, preferred_element_type=F32)
                dk_h = lax.dot_general(dzw, qh[hh], _TN, preferred_element_type=F32)
                dv_h = lax.dot_general(ww, doh[hh], _TN, preferred_element_type=F32)
                dk_win = (dk_h, dv_h) if dk_win is None else (dk_win[0] + dk_h, dk_win[1] + dv_h)
                run_scr[hh] = jnp.broadcast_to(run, (TM, TM))
                gsum_scr[hh] = jnp.broadcast_to(gsum, (TM, TM))
                top = jnp.max(run)
                alive = top if alive is None else jnp.maximum(alive, top)
            dk_ref[pl.ds(off, n * TM), :] += dk_win[0]
            dv_ref[pl.ds(off, n * TM), :] += dk_win[1]
            return (alive > SB_DEAD).astype(jnp.int32)

        dq_acc[...] = jnp.zeros_like(dq_acc)
        run_scr[...] = jnp.zeros_like(run_scr)
        gsum_scr[...] = jnp.zeros_like(gsum_scr)

        @pl.when(i >= SB_WINDOW - 1)
        def _():
            key_set(i - (SB_WINDOW - 1), SB_WINDOW, strict)

        start = jnp.where(i >= SB_WINDOW - 1, i - SB_WINDOW, i)
        alive0 = (jnp.max(run_scr[...]) > SB_DEAD).astype(jnp.int32)

        def cond(c):
            return jnp.logical_and(c[0] >= 0, c[1] > 0)

        def step(c):
            kb = c[0]
            return kb - 1, key_set(kb, 1, jnp.logical_or(strict, kb < i))

        lax.while_loop(cond, step, (start, alive0))
        dq_all[pl.ds(pl.multiple_of(i * TM, TM), TM), :] = dq_acc[...] * scale

        @pl.when(i == nb - 1)
        def _():
            copies = []
            for s, src in enumerate((dq_all, dk_ref, dv_ref)):
                stage[s] = src[...].astype(_MXU)
                col = pl.multiple_of((s * npair + p) * TM, TM)
                copies.append(pltpu.make_async_copy(stage.at[s], dproj_ref.at[:, pl.ds(col, TM)], sems.at[s]))
                copies[-1].start()
            for cp in copies:
                cp.wait()

        if n:
            @pl.when(jnp.logical_and(p == npair - 1, i == nb - 1))
            def _():
                _finish_copies(_slab_copies(srcs, dsts, dests, *host_sem_refs))

    blk = pl.BlockSpec((TM, TM), lambda p, i: (i, p))
    res = pl.pallas_call(
        body, name="sb_bwd", grid=(npair, nb),
        in_specs=[blk,
                  pl.BlockSpec((lp, TM), lambda p, i: (0, npair + p)),
                  pl.BlockSpec((lp, TM), lambda p, i: (0, 2 * npair + p)),
                  blk, blk, blk, pl.BlockSpec(memory_space=pl.ANY)] + host_in,
        out_specs=[pl.BlockSpec(memory_space=pl.ANY)] + host_out,
        out_shape=[_SDS(d_proj.shape, d_proj.dtype)] + host_shapes,
        input_output_aliases={6: 0},
        scratch_shapes=[pltpu.VMEM((lp, TM), F32), pltpu.VMEM((lp, TM), F32), pltpu.VMEM((lp, TM), F32),
                        pltpu.VMEM((3, lp, TM), _MXU), pltpu.SemaphoreType.DMA((3,)),
                        pltpu.VMEM((TM, TM), F32), pltpu.VMEM((2, TM, TM), F32), pltpu.VMEM((2, TM, TM), F32)] + host_sems,
        compiler_params=_params(("arbitrary", "arbitrary")),
    )(qkv, qkv, qkv, o_sb, o_lo, d_o, d_proj, *send)
    return res[0], list(res[1:])


def _conv_pre(x_ref, w_ref, b_ref, lp):
    n = lp - 8
    w = w_ref[...]
    pre = (x_ref[pl.ds(5, n), :] * w[0:1, :] + x_ref[pl.ds(6, n), :] * w[1:2, :]
           + x_ref[pl.ds(7, n), :] * w[2:3, :] + x_ref[pl.ds(8, n), :] * w[3:4, :]) + b_ref[...]
    live = (_iota((n, 128), 0) + 8) >= PAD
    return pre, live


def _conv_fwd(proj, dt_raw, conv_w, conv_b, dt_bias128):
    lp = proj.shape[0]
    nblk = XBC_W // 128
    c0 = (COL_XBC - QKV_W) // 128

    def body(x_ref, w_ref, b_ref, dtr_ref, dtb_ref, o_ref, dt_ref):
        pre, live = _conv_pre(x_ref, w_ref, b_ref, lp)
        act = pre * _sigmoid(pre)
        o_ref[pl.ds(0, 8), :] = jnp.zeros((8, 128), F32)
        o_ref[pl.ds(8, lp - 8), :] = jnp.where(live, act, 0.0)

        @pl.when(pl.program_id(0) == 0)
        def _():
            s = dtr_ref[...] + dtb_ref[...]
            sp = jnp.maximum(s, 0.0) + jnp.log(1.0 + jnp.exp(-jnp.abs(s)))
            dt_ref[...] = jnp.where(_iota((lp, 128), 0) >= PAD, sp, 0.0)

    return pl.pallas_call(
        body, name="conv_fwd", grid=(nblk,),
        in_specs=[pl.BlockSpec((lp, 128), lambda j: (0, c0 + j)),
                  pl.BlockSpec((4, 128), lambda j: (0, j)),
                  pl.BlockSpec((1, 128), lambda j: (0, j)),
                  pl.BlockSpec((lp, 128), lambda j: (0, 0)),
                  pl.BlockSpec((1, 128), lambda j: (0, 0))],
        out_specs=[pl.BlockSpec((lp, 128), lambda j: (0, j)),
                   pl.BlockSpec((lp, 128), lambda j: (0, 0))],
        out_shape=[_SDS((lp, XBC_W), F32), _SDS((lp, 128), F32)],
        compiler_params=_params(("arbitrary",)),
    )(proj, conv_w, conv_b, dt_raw, dt_bias128)


def _conv_bwd(proj, dt_raw, conv_w, conv_b, dt_bias128, d_xbc, d_dt128, d_proj):
    lp = proj.shape[0]
    nblk = XBC_W // 128
    c0 = COL_XBC // 128
    c0_in = (COL_XBC - QKV_W) // 128
    n = lp - 8
    last = nblk - 1

    def body(x_ref, w_ref, b_ref, dtr_ref, dtb_ref, dy_ref, ddt_ref, dproj_in,
             dx_ref, gw_ref, gb_ref, gdtb_ref, scr):
        j = pl.program_id(0)

        @pl.when(j < nblk)
        def _():
            pre, live = _conv_pre(x_ref, w_ref, b_ref, lp)
            sg = _sigmoid(pre)
            dpre = jnp.where(live, dy_ref[pl.ds(8, n), :] * (sg * (1.0 + pre * (1.0 - sg))), 0.0)
            gb_ref[...] = jnp.sum(dpre, axis=0, keepdims=True)
            gw_ref[...] = jnp.concatenate(
                [jnp.sum(dpre * x_ref[pl.ds(5 + k, n), :], axis=0, keepdims=True) for k in range(4)], axis=0)
            scr[pl.ds(0, 8), :] = jnp.zeros((8, 128), F32)
            scr[pl.ds(8, n), :] = dpre
            scr[pl.ds(lp, 8), :] = jnp.zeros((8, 128), F32)
            w = w_ref[...]
            dx_ref[...] = (scr[pl.ds(0, lp), :] * w[3:4, :] + scr[pl.ds(1, lp), :] * w[2:3, :]
                           + scr[pl.ds(2, lp), :] * w[1:2, :] + scr[pl.ds(3, lp), :] * w[0:1, :]).astype(dx_ref.dtype)

        @pl.when(j == nblk)
        def _():
            s = dtr_ref[...] + dtb_ref[...]
            d = jnp.where(_iota((lp, 128), 0) >= PAD, ddt_ref[...] * _sigmoid(s), 0.0)
            dx_ref[...] = d.astype(dx_ref.dtype)
            gdtb_ref[...] = jnp.sum(d, axis=0, keepdims=True)

    clamp = lambda j: (0, jnp.minimum(j, last))
    full128 = pl.BlockSpec((lp, 128), lambda j: (0, 0))
    return pl.pallas_call(
        body, name="conv_bwd", grid=(nblk + 1,),
        in_specs=[pl.BlockSpec((lp, 128), lambda j: (0, c0_in + jnp.minimum(j, last))),
                  pl.BlockSpec((4, 128), clamp),
                  pl.BlockSpec((1, 128), clamp),
                  full128, pl.BlockSpec((1, 128), lambda j: (0, 0)),
                  pl.BlockSpec((lp, 128), clamp), full128, pl.BlockSpec(memory_space=pl.ANY)],
        out_specs=[pl.BlockSpec((lp, 128), lambda j: (0, c0 + j)), pl.BlockSpec((4, 128), clamp),
                   pl.BlockSpec((1, 128), clamp), pl.BlockSpec((1, 128), lambda j: (0, 0))],
        out_shape=[_SDS(d_proj.shape, d_proj.dtype), _SDS((4, XBC_W), F32), _SDS((1, XBC_W), F32), _SDS((1, 128), F32)],
        input_output_aliases={7: 0},
        scratch_shapes=[pltpu.VMEM((lp + 8, 128), F32)],
        compiler_params=_params(("arbitrary",)),
    )(proj, conv_w, conv_b, dt_raw, dt_bias128, d_xbc, d_dt128, d_proj)


def _ssd_pieces(dt, dt_t, a, a_t):
    r64 = _iota((CHUNK, CHUNK), 0)
    c64 = _iota((CHUNK, CHUNK), 1)
    tril = c64 <= r64
    tril01 = tril.astype(_MXU)
    triu01 = (r64 <= c64).astype(_MXU)
    expand = (lax.shift_right_logical(_iota((N_HEADS, SSD_W), 1), 6) == _iota((N_HEADS, SSD_W), 0)).astype(_MXU)
    acum = _sel_left(tril01, dt * a)
    acum_t = _sel_right(dt_t * a_t, triu01)
    ax = _sel_right(acum, expand)
    dtx = _sel_right(dt, expand)
    return tril, expand, acum, acum_t, ax, dtx


def _seg_matrix():
    return (lax.shift_right_logical(_iota((SSD_W, N_HEADS), 0), 6) == _iota((SSD_W, N_HEADS), 1)).astype(_MXU)


def _head_decay(ax, acum_t, h, tril):
    col = ax[:, h * HEAD:(h + 1) * HEAD]
    rowv = acum_t[h:h + 1, :]
    return jnp.where(tril, jnp.exp(jnp.minimum(col - rowv, 0.0)), 0.0)


def _ssd_fwd(xbc, dt_c, dt_tc, a, a_t, dskip_x):
    lp = xbc.shape[0]
    nc = lp // CHUNK
    gw = SSD_W // N_GROUPS
    hpg = N_HEADS // N_GROUPS

    def body(x_ref, dt_ref, dtt_ref, a_ref, at_ref, d_ref, y_ref, st_ref, state):
        c = pl.program_id(0)

        @pl.when(c == 0)
        def _():
            state[...] = jnp.zeros_like(state)

        st_ref[0] = state[...]
        tril, _, _, acum_t, ax, dtx = _ssd_pieces(dt_ref[0], dtt_ref[0], a_ref[...], at_ref[...])
        x = x_ref[:, 0:SSD_W]
        xdt = x * dtx
        ea = jnp.exp(ax)
        aex = ax[CHUNK - 1:CHUNK, :]
        wd = jnp.exp(aex - ax)
        eae = jnp.exp(aex)
        xw = xdt * wd
        y_ref[...] = x * d_ref[...]
        for g in range(N_GROUPS):
            gs = slice(g * gw, (g + 1) * gw)
            rs = slice(g * N_STATE, (g + 1) * N_STATE)
            bg = x_ref[:, SSD_W + g * N_STATE:SSD_W + (g + 1) * N_STATE]
            cg = x_ref[:, SSD_W + N_GROUPS * N_STATE + g * N_STATE:SSD_W + N_GROUPS * N_STATE + (g + 1) * N_STATE]
            sg = state[rs, :]
            cb = _mm_nt(cg, bg)
            y_ref[:, gs] += _mm(cg, sg) * ea[:, gs]
            for r in range(hpg):
                h = g * hpg + r
                hs = slice(h * HEAD, (h + 1) * HEAD)
                m = cb * _head_decay(ax, acum_t, h, tril)
                y_ref[:, hs] += _mm(m, xdt[:, hs])
            state[rs, :] = sg * eae[:, gs] + _mm_tn(bg, xw[:, gs])

    return pl.pallas_call(
        body, name="ssd_fwd", grid=(nc,),
        in_specs=[pl.BlockSpec((CHUNK, XBC_W), lambda c: (c, 0)),
                  pl.BlockSpec((1, CHUNK, N_HEADS), lambda c: (c, 0, 0)),
                  pl.BlockSpec((1, N_HEADS, CHUNK), lambda c: (c, 0, 0)),
                  pl.BlockSpec((1, N_HEADS), lambda c: (0, 0)),
                  pl.BlockSpec((N_HEADS, 1), lambda c: (0, 0)),
                  pl.BlockSpec((1, SSD_W), lambda c: (0, 0))],
        out_specs=[pl.BlockSpec((CHUNK, SSD_W), lambda c: (c, 0)),
                   pl.BlockSpec((1, N_GROUPS * N_STATE, gw), lambda c: (c, 0, 0))],
        out_shape=[_SDS((lp, SSD_W), F32), _SDS((nc, N_GROUPS * N_STATE, gw), F32)],
        scratch_shapes=[pltpu.VMEM((N_GROUPS * N_STATE, gw), F32)],
        compiler_params=_params(("arbitrary",)),
    )(xbc, dt_c, dt_tc, a, a_t, dskip_x)


def _ssd_bwd(xbc, dt_c, dt_tc, a, a_t, dskip_x, states, d_y):
    lp = xbc.shape[0]
    nc = lp // CHUNK
    gw = SSD_W // N_GROUPS
    hpg = N_HEADS // N_GROUPS

    def body(x_ref, dt_ref, dtt_ref, a_ref, at_ref, d_ref, st_ref, dy_ref,
             dx_ref, ddta_ref, ddtb_ref, ga1_ref, ga2_ref, gd_ref, dstate, dxdt_scr, z_scr, yoff_scr, sds_scr):
        c = pl.program_id(0)

        @pl.when(c == 0)
        def _():
            dstate[...] = jnp.zeros_like(dstate)
            ga1_ref[...] = jnp.zeros_like(ga1_ref)
            ga2_ref[...] = jnp.zeros_like(ga2_ref)
            gd_ref[...] = jnp.zeros_like(gd_ref)

        dt = dt_ref[0]
        dt_t = dtt_ref[0]
        a = a_ref[...]
        a_t = at_ref[...]
        tril, _, acum, acum_t, ax, dtx = _ssd_pieces(dt, dt_t, a, a_t)
        seg = _seg_matrix()
        x = x_ref[:, 0:SSD_W]
        dy = dy_ref[...]
        xdt = x * dtx
        ea = jnp.exp(ax)
        aex = ax[CHUNK - 1:CHUNK, :]
        wd = jnp.exp(aex - ax)
        eae = jnp.exp(aex)
        xw = xdt * wd
        edy = ea * dy
        lane16 = _iota((CHUNK, N_HEADS), 1)
        row16 = _iota((N_HEADS, CHUNK), 0)
        da_col = jnp.zeros((CHUNK, N_HEADS), F32)
        da_row = jnp.zeros((N_HEADS, CHUNK), F32)
        for g in range(N_GROUPS):
            gs = slice(g * gw, (g + 1) * gw)
            rs = slice(g * N_STATE, (g + 1) * N_STATE)
            bcol = slice(SSD_W + g * N_STATE, SSD_W + (g + 1) * N_STATE)
            ccol = slice(SSD_W + N_GROUPS * N_STATE + g * N_STATE, SSD_W + N_GROUPS * N_STATE + (g + 1) * N_STATE)
            bg = x_ref[:, bcol]
            cg = x_ref[:, ccol]
            sg = st_ref[0, rs, :]
            dsn = dstate[rs, :]
            cb = _mm_nt(cg, bg)
            z_scr[:, gs] = _mm(bg, dsn)
            yoff_scr[:, gs] = _mm(cg, sg) * ea[:, gs]
            sds_scr[:, gs] = jnp.broadcast_to(jnp.sum(dsn * sg, axis=0, keepdims=True), (8, gw))
            dcb = jnp.zeros((CHUNK, CHUNK), F32)
            for r in range(hpg):
                h = g * hpg + r
                hs = slice(h * HEAD, (h + 1) * HEAD)
                dec = _head_decay(ax, acum_t, h, tril)
                m = cb * dec
                t1 = _mm_nt(dy[:, hs], xdt[:, hs])
                dcb = dcb + dec * t1
                tm = m * t1
                da_col = da_col + jnp.where(lane16 == h, jnp.sum(tm, axis=1, keepdims=True), 0.0)
                da_row = da_row - jnp.where(row16 == h, jnp.sum(tm, axis=0, keepdims=True), 0.0)
                dxdt_scr[:, hs] = _mm_tn(m, dy[:, hs])
            dx_ref[:, ccol] = _mm(dcb, bg) + _mm_nt(edy[:, gs], sg)
            dx_ref[:, bcol] = _mm_tn(dcb, cg) + _mm_nt(xw[:, gs], dsn)
            dstate[rs, :] = eae[:, gs] * dsn + _mm_tn(cg, edy[:, gs])
        zf = z_scr[...]
        dxdt = dxdt_scr[...] + wd * zf
        t3 = _sel_right(xw * zf, seg)
        da_col = da_col + _sel_right(dy * yoff_scr[...], seg) - t3
        aend = acum[CHUNK - 1:CHUNK, :]
        sd = _sel_right(sds_scr[...], seg)[0:1, :] * jnp.exp(aend)
        last = jnp.sum(t3, axis=0, keepdims=True) + sd
        da_col = da_col + jnp.where(_iota((CHUNK, N_HEADS), 0) == CHUNK - 1, last, 0.0)
        r64 = _iota((CHUNK, CHUNK), 0)
        c64 = _iota((CHUNK, CHUNK), 1)
        ddta1 = _sel_left((c64 >= r64).astype(_MXU), da_col)
        ddta2 = _sel_right(da_row, (r64 >= c64).astype(_MXU))
        ddta_ref[0] = a * ddta1 + _sel_right(dxdt * x, seg)
        ddtb_ref[0] = a_t * ddta2
        ga1_ref[...] += jnp.sum(dt * ddta1, axis=0, keepdims=True)
        ga2_ref[...] += jnp.sum(dt_t * ddta2, axis=1, keepdims=True)
        dx_ref[:, 0:SSD_W] = dxdt * dtx + d_ref[...] * dy
        gd_ref[...] += jnp.sum(dy * x, axis=0, keepdims=True)

    rev = lambda c: (nc - 1 - c, 0)
    rev3 = lambda c: (nc - 1 - c, 0, 0)
    return pl.pallas_call(
        body, name="ssd_bwd", grid=(nc,),
        in_specs=[pl.BlockSpec((CHUNK, XBC_W), rev),
                  pl.BlockSpec((1, CHUNK, N_HEADS), rev3),
                  pl.BlockSpec((1, N_HEADS, CHUNK), rev3),
                  pl.BlockSpec((1, N_HEADS), lambda c: (0, 0)),
                  pl.BlockSpec((N_HEADS, 1), lambda c: (0, 0)),
                  pl.BlockSpec((1, SSD_W), lambda c: (0, 0)),
                  pl.BlockSpec((1, N_GROUPS * N_STATE, gw), rev3),
                  pl.BlockSpec((CHUNK, SSD_W), rev)],
        out_specs=[pl.BlockSpec((CHUNK, XBC_W), rev),
                   pl.BlockSpec((1, CHUNK, N_HEADS), rev3),
                   pl.BlockSpec((1, N_HEADS, CHUNK), rev3),
                   pl.BlockSpec((1, N_HEADS), lambda c: (0, 0)),
                   pl.BlockSpec((N_HEADS, 1), lambda c: (0, 0)),
                   pl.BlockSpec((1, SSD_W), lambda c: (0, 0))],
        out_shape=[_SDS((lp, XBC_W), F32), _SDS((nc, CHUNK, N_HEADS), F32), _SDS((nc, N_HEADS, CHUNK), F32),
                   _SDS((1, N_HEADS), F32), _SDS((N_HEADS, 1), F32), _SDS((1, SSD_W), F32)],
        scratch_shapes=[pltpu.VMEM((N_GROUPS * N_STATE, gw), F32), pltpu.VMEM((CHUNK, SSD_W), F32),
                        pltpu.VMEM((CHUNK, SSD_W), F32), pltpu.VMEM((CHUNK, SSD_W), F32),
                        pltpu.VMEM((8, SSD_W), F32)],
        compiler_params=_params(("arbitrary",)),
    )(xbc, dt_c, dt_tc, a, a_t, dskip_x, states, d_y)


def _gated_norm(o, gate, w):
    sg = _sigmoid(gate)
    p = o * (gate * sg)
    rs = lax.rsqrt(jnp.mean(p * p, axis=-1, keepdims=True) + EPS)
    n = p * rs
    return sg, rs, n, n * w


def _tail_fwd(o_sb, o_ssd, proj, h0, target, w_out, sb_w, ssd_w, fin_w):
    lp = o_sb.shape[0]
    nb = lp // TM
    row = lambda i: (i, 0)
    one = lambda i: (0, 0)

    def body(osb_ref, gate_ref, ossd_ref, z_ref, h0_ref, tgt_ref, wo_ref, sbw_ref, ssdw_ref, fw_ref,
             dh1_ref, loss_ref, gfw_ref):
        i = pl.program_id(0)

        @pl.when(i == 0)
        def _():
            loss_ref[...] = jnp.zeros_like(loss_ref)
            gfw_ref[...] = jnp.zeros_like(gfw_ref)

        y1 = _gated_norm(osb_ref[...], gate_ref[...], sbw_ref[...])[3]
        y2 = _gated_norm(ossd_ref[...], z_ref[...], ssdw_ref[...])[3]
        h1 = (h0_ref[...] + _mm(y1, wo_ref[0:SB_W, :])) + _mm(y2, wo_ref[SB_W:SB_W + SSD_W, :])
        rs1 = lax.rsqrt(jnp.mean(h1 * h1, axis=-1, keepdims=True) + EPS)
        n1 = h1 * rs1
        fw = fw_ref[...]
        diff = jnp.where(i > 0, n1 * fw - tgt_ref[...], 0.0)
        loss_ref[...] += jnp.sum(diff * diff, axis=0, keepdims=True)
        d_out = diff * (1.0 / D_MODEL)
        gfw_ref[...] += jnp.sum(d_out * n1, axis=0, keepdims=True)
        g = d_out * fw
        dh1_ref[...] = rs1 * (g - n1 * jnp.mean(g * n1, axis=-1, keepdims=True))

    return pl.pallas_call(
        body, name="tail_fwd", grid=(nb,),
        in_specs=[pl.BlockSpec((TM, SB_W), row),
                  pl.BlockSpec((TM, SB_W), lambda i: (i, (COL_GATE - QKV_W) // SB_W)),
                  pl.BlockSpec((TM, SSD_W), row),
                  pl.BlockSpec((TM, SSD_W), lambda i: (i, (COL_Z - QKV_W) // SSD_W)),
                  pl.BlockSpec((TM, D_MODEL), row),
                  pl.BlockSpec((TM, D_MODEL), lambda i: (jnp.maximum(i - 1, 0), 0)),
                  pl.BlockSpec(memory_space=_VMEM),
                  pl.BlockSpec((1, SB_W), one), pl.BlockSpec((1, SSD_W), one), pl.BlockSpec((1, D_MODEL), one)],
        out_specs=[pl.BlockSpec((TM, D_MODEL), row), pl.BlockSpec((1, D_MODEL), one), pl.BlockSpec((1, D_MODEL), one)],
        out_shape=[_SDS((lp, D_MODEL), F32), _SDS((1, D_MODEL), F32), _SDS((1, D_MODEL), F32)],
        compiler_params=_params(("arbitrary",), 40),
    )(o_sb, proj, o_ssd, proj, h0, target, w_out, sb_w, ssd_w, fin_w)


def _gated_norm_bwd(o, gate, w, dy):
    sg, rs, n, _ = _gated_norm(o, gate, w)
    gw = jnp.sum(dy * n, axis=0, keepdims=True)
    dn = dy * w
    dp = rs * (dn - n * jnp.mean(dn * n, axis=-1, keepdims=True))
    d_o = dp * (gate * sg)
    d_gate = dp * o * (sg * (1.0 + gate * (1.0 - sg)))
    return d_o, d_gate, gw, n * w


def _tail_bwd(o_sb, o_ssd, proj, d_h1, w_out, sb_w, ssd_w):
    lp = o_sb.shape[0]
    tm = 272 if lp % 272 == 0 else TM
    nb = lp // tm
    row = lambda i, t: (i, 0)
    one = lambda i, t: (0, 0)

    def body(osb_ref, gate_ref, ossd_ref, z_ref, dh1_ref, wo_ref, sbw_ref, ssdw_ref,
             dosb_ref, dossd_ref, dproj_ref, gwo_ref, gsb_ref, gssd_ref):
        i = pl.program_id(0)
        t = pl.program_id(1)

        @pl.when(jnp.logical_and(i == 0, t == 0))
        def _():
            gwo_ref[...] = jnp.zeros_like(gwo_ref)
            gsb_ref[...] = jnp.zeros_like(gsb_ref)
            gssd_ref[...] = jnp.zeros_like(gssd_ref)

        dh1 = dh1_ref[...].astype(_MXU)

        def half(o_ref, g_ref, w_ref, do_ref, gn_ref, r0):
            dy = lax.dot_general(dh1, wo_ref[r0:r0 + SB_W, :], _NT, preferred_element_type=F32)
            d_o, d_g, gw, y = _gated_norm_bwd(o_ref[...], g_ref[...], w_ref[...], dy)
            do_ref[...] = d_o
            dproj_ref[...] = d_g.astype(_MXU)
            gn_ref[...] += gw
            gwo_ref[r0:r0 + SB_W, :] += lax.dot_general(y.astype(_MXU), dh1, _TN, preferred_element_type=F32)

        @pl.when(t == 0)
        def _():
            half(osb_ref, gate_ref, sbw_ref, dosb_ref, gsb_ref, 0)

        @pl.when(t == 1)
        def _():
            half(ossd_ref, z_ref, ssdw_ref, dossd_ref, gssd_ref, SB_W)

    tile = pl.BlockSpec((tm, SB_W), row)
    return pl.pallas_call(
        body, name="tail_bwd", grid=(nb, 2),
        in_specs=[tile, pl.BlockSpec((tm, SB_W), lambda i, t: (i, (COL_GATE - QKV_W) // SB_W)),
                  tile, pl.BlockSpec((tm, SSD_W), lambda i, t: (i, (COL_Z - QKV_W) // SSD_W)),
                  tile, pl.BlockSpec(memory_space=_VMEM),
                  pl.BlockSpec((1, SB_W), one), pl.BlockSpec((1, SSD_W), one)],
        out_specs=[tile, tile, pl.BlockSpec((tm, SB_W), lambda i, t: (i, COL_GATE // SB_W + t)),
                   pl.BlockSpec((SB_W + SSD_W, D_MODEL), one), pl.BlockSpec((1, SB_W), one), pl.BlockSpec((1, SSD_W), one)],
        out_shape=[_SDS((lp, SB_W), F32), _SDS((lp, SSD_W), F32), _SDS((lp, W_ALL), _MXU),
                   _SDS((SB_W + SSD_W, D_MODEL), F32), _SDS((1, SB_W), F32), _SDS((1, SSD_W), F32)],
        compiler_params=_params(("arbitrary", "arbitrary"), 48),
    )(o_sb, proj, o_ssd, proj, d_h1, w_out, sb_w, ssd_w)


def _d_u(d_proj, w_t, send=(), dests=()):
    lp = d_proj.shape[0]
    tk = 512
    steps = N_MAIN // tk
    n = len(send)
    host_in, host_out, host_shapes, host_sems = _host_specs(send)

    def body(dp_ref, w_ref, dpdt_ref, wdt_ref, *rest):
        srcs, o_ref, dsts, sems = rest[:n], rest[n], rest[n + 1:2 * n + 1], rest[2 * n + 1:]
        j = pl.program_id(0)

        @pl.when(j == 0)
        def _():
            if n:
                _start_copies(_slab_copies(srcs, dsts, dests, *sems))
            o_ref[...] = jnp.dot(dpdt_ref[...], wdt_ref[...], preferred_element_type=F32)

        o_ref[...] += jnp.dot(dp_ref[...], w_ref[...], preferred_element_type=F32)

        if n:
            @pl.when(j == steps - 1)
            def _():
                _finish_copies(_slab_copies(srcs, dsts, dests, *sems))

    res = pl.pallas_call(
        body, name="d_u", grid=(steps,),
        in_specs=[pl.BlockSpec((lp, tk), lambda j: (0, j)),
                  pl.BlockSpec((tk, D_MODEL), lambda j: (j, 0)),
                  pl.BlockSpec((lp, 128), lambda j: (0, N_MAIN // 128)),
                  pl.BlockSpec((128, D_MODEL), lambda j: (N_MAIN // 128, 0))] + host_in,
        out_specs=[pl.BlockSpec((lp, D_MODEL), lambda j: (0, 0))] + host_out,
        out_shape=[_SDS((lp, D_MODEL), F32)] + host_shapes,
        scratch_shapes=host_sems,
        compiler_params=_params(("arbitrary",), 48),
    )(d_proj, w_t, d_proj, w_t, *send)
    return res[0], list(res[1:])


def _norm_bwd(du_all, h0, d_h1, norm_w):
    lp = h0.shape[0]
    nb = lp // TM
    seq = lp - OFF
    row = lambda i: (i, 0)
    one = lambda i: (0, 0)

    def body(du_ref, h0_ref, dh1_ref, nw_ref, gx_ref, gmeta_ref, gnw_ref):
        i = pl.program_id(0)

        @pl.when(i == 0)
        def _():
            gnw_ref[...] = jnp.zeros_like(gnw_ref)

        du = du_ref[...]
        h = h0_ref[...]
        rs = lax.rsqrt(jnp.mean(h * h, axis=-1, keepdims=True) + EPS)
        n0 = h * rs
        gnw_ref[...] += jnp.sum(du * n0, axis=0, keepdims=True)
        g = du * nw_ref[...]
        dh0 = dh1_ref[...] + rs * (g - n0 * jnp.mean(g * n0, axis=-1, keepdims=True))

        @pl.when(i == 0)
        def _():
            gmeta_ref[...] = dh0[PAD:PAD + N_META, :]

        @pl.when(i > 0)
        def _():
            gx_ref[...] = dh0

    tile = pl.BlockSpec((TM, D_MODEL), row)
    return pl.pallas_call(
        body, name="norm_bwd", grid=(nb,),
        in_specs=[tile, tile, tile, pl.BlockSpec((1, D_MODEL), one)],
        out_specs=[pl.BlockSpec((TM, D_MODEL), lambda i: (jnp.maximum(i - 1, 0), 0)),
                   pl.BlockSpec((N_META, D_MODEL), one), pl.BlockSpec((1, D_MODEL), one)],
        out_shape=[_SDS((seq, D_MODEL), F32), _SDS((N_META, D_MODEL), F32), _SDS((1, D_MODEL), F32)],
        compiler_params=_params(("arbitrary",)),
    )(du_all, h0, d_h1, norm_w)


def _grad_w_windows(u_t, d_proj, first, count, name):
    lp = d_proj.shape[0]
    hw = WIN_W // 2
    steps = 2 * count

    def body(ut_ref, dp_hbm, o_ref, buf, sems):
        s = pl.program_id(0)
        slot = s % 2

        def fetch(step, sl):
            start = pl.multiple_of((first + step // 2) * WIN_STEP + (step % 2) * hw, 128)
            return pltpu.make_async_copy(dp_hbm.at[:, pl.ds(start, hw)], buf.at[sl], sems.at[sl])

        @pl.when(s == 0)
        def _():
            fetch(0, 0).start()

        @pl.when(s + 1 < steps)
        def _():
            fetch(s + 1, 1 - slot).start()

        fetch(s, slot).wait()
        o_ref[0] = jnp.dot(ut_ref[...], buf[slot], preferred_element_type=F32).astype(o_ref.dtype)

    return pl.pallas_call(
        body, name=name, grid=(steps,),
        in_specs=[pl.BlockSpec((D_MODEL, lp), lambda s: (0, 0)), pl.BlockSpec(memory_space=pl.ANY)],
        out_specs=pl.BlockSpec((1, D_MODEL, hw), lambda s: (s // 2, 0, s % 2)),
        out_shape=_SDS((count, D_MODEL, WIN_W), _MXU),
        scratch_shapes=[pltpu.VMEM((2, lp, hw), _MXU), pltpu.SemaphoreType.DMA((2,))],
        compiler_params=_params(("arbitrary",), 40),
    )(u_t, d_proj)


def _device_grads(x2d, target2d, meta_full, norm_w, w_t, conv_w, conv_b, dt_bias, a_log, d_skip,
                  sb_w, ssd_w, w_out, fin_w, exchange=None):
    lp = x2d.shape[0] + OFF
    nc = lp // CHUNK
    h0, u, u_t = _prep(x2d, meta_full, norm_w)
    qkv, proj, dt_raw = _inproj(u, w_t)
    o_sb, o_lo = _sb_fwd(qkv)
    dt_bias128 = jnp.pad(dt_bias, ((0, 0), (0, 128 - N_HEADS)))
    xbc, dt128 = _conv_fwd(proj, dt_raw, conv_w, conv_b, dt_bias128)
    dt_c = dt128[:, :N_HEADS].reshape(nc, CHUNK, N_HEADS)
    dt_tc = jnp.swapaxes(dt_c, 1, 2)
    a = -jnp.exp(a_log)
    a_t = a.reshape(N_HEADS, 1)
    dskip_x = jnp.repeat(d_skip, HEAD, axis=1)
    o_ssd, states = _ssd_fwd(xbc, dt_c, dt_tc, a, a_t, dskip_x)
    d_h1, sq_err, g_fin = _tail_fwd(o_sb, o_ssd, proj, h0, target2d, w_out, sb_w, ssd_w, fin_w)

    d_osb, d_ossd, d_proj, g_wout, g_sb, g_ssd = _tail_bwd(o_sb, o_ssd, proj, d_h1, w_out, sb_w, ssd_w)
    d_xbc_act, ddt_a, ddt_b, ga1, ga2, gd = _ssd_bwd(xbc, dt_c, dt_tc, a, a_t, dskip_x, states, d_ossd)
    d_dt = (ddt_a + jnp.swapaxes(ddt_b, 1, 2)).reshape(lp, N_HEADS)
    d_dt128 = jnp.pad(d_dt, ((0, 0), (0, 128 - N_HEADS)))
    d_proj, g_convw, g_convb, g_dtb128 = _conv_bwd(proj, dt_raw, conv_w, conv_b, dt_bias128, d_xbc_act, d_dt128, d_proj)
    g_win_hi = _grad_w_windows(u_t, d_proj, 2, 2, "grad_w_in_hi")
    send_e, dests_e = ((), ()) if exchange is None else exchange["early"](g_win_hi, g_wout)
    d_proj, arrived_e = _sb_bwd(qkv, o_sb, o_lo, d_osb, d_proj, send_e, dests_e)
    g_win_lo = _grad_w_windows(u_t, d_proj, 0, 2, "grad_w_in_lo")
    send_l, dests_l = ((), ()) if exchange is None else exchange["late"](g_win_lo)
    d_u, arrived_l = _d_u(d_proj, w_t, send_l, dests_l)
    g_win = jnp.concatenate([g_win_lo, g_win_hi], axis=0)
    send, arrived = tuple(send_e) + tuple(send_l), tuple(arrived_e) + tuple(arrived_l)
    g_x, g_meta, g_nw = _norm_bwd(d_u, h0, d_h1, norm_w)
    g_alog = (ga1 + ga2.reshape(1, N_HEADS)) * a
    g_dskip = gd.reshape(N_HEADS, HEAD).sum(axis=1).reshape(1, N_HEADS)
    grads = dict(meta_tokens=g_meta, norm_w=g_nw, w_in=g_win, conv_w=g_convw, conv_b=g_convb,
                 dt_bias=g_dtb128[:, :N_HEADS], a_log=g_alog, d_skip=g_dskip, sb_norm_w=g_sb, ssd_norm_w=g_ssd,
                 w_out=g_wout, final_norm_w=g_fin, sent=send, arrived=arrived)
    return sq_err, g_x, grads


_MESH = pl.DeviceIdType.MESH
_ANY = pl.BlockSpec(memory_space=pl.ANY)


def _place():
    return lax.axis_index("x"), lax.axis_index("y"), lax.axis_index("c")


def _other_chips(x, y):
    return ((1 - x, y), (x, 1 - y), (1 - x, 1 - y))


def _gather_shards(arrays, n_big):
    n = len(arrays)

    def body(*refs):
        srcs, dsts = refs[:n], refs[n:2 * n]
        send_sems, recv_sems, fwd_send, fwd_recv = refs[2 * n:]
        x, y, c = _place()
        mine = 2 * x + y
        chips = _other_chips(x, y)

        def window(a):
            half = arrays[a].shape[1] // 2
            return pl.ds(pl.multiple_of(c * half, 128), half)

        first = []
        for a in range(n):
            for k, (px, py) in enumerate(chips):
                if a < n_big:
                    src, dst = srcs[a].at[:, window(a)], dsts[a].at[mine, :, window(a)]
                else:
                    src, dst = srcs[a], dsts[a].at[mine]
                cp = pltpu.make_async_remote_copy(
                    src_ref=src, dst_ref=dst, send_sem=send_sems.at[a * 3 + k], recv_sem=recv_sems.at[a * 3 + k],
                    device_id=(px, py, c), device_id_type=_MESH)
                cp.start()
                first.append(cp)
        passed = []
        for a in range(n):
            for k, (px, py) in enumerate(chips):
                first[a * 3 + k].wait_recv()
                if a < n_big:
                    landed = dsts[a].at[2 * px + py, :, window(a)]
                    cp = pltpu.make_async_remote_copy(
                        src_ref=landed, dst_ref=landed, send_sem=fwd_send.at[a * 3 + k], recv_sem=fwd_recv.at[a * 3 + k],
                        device_id=(x, y, 1 - c), device_id_type=_MESH)
                    cp.start()
                    passed.append(cp)
        for cp in passed:
            cp.wait_recv()
        for cp in first + passed:
            cp.wait_send()

    got = pl.pallas_call(
        body, name="gather_shards",
        in_specs=[_ANY] * n, out_specs=[_ANY] * n,
        out_shape=[_SDS((N_CHIPS,) + a.shape, a.dtype) for a in arrays],
        scratch_shapes=[pltpu.SemaphoreType.DMA((3 * n,)), pltpu.SemaphoreType.DMA((3 * n,)),
                        pltpu.SemaphoreType.DMA((3 * n_big,)), pltpu.SemaphoreType.DMA((3 * n_big,))],
    )(*arrays)
    mine = 2 * lax.axis_index("x") + lax.axis_index("y")
    return [lax.dynamic_update_slice(g, a[None], (mine,) + (0,) * a.ndim) for g, a in zip(got, arrays)]


def _slab_copies(srcs, dsts, dests, send_sems, recv_sems):
    x, y, c = _place()
    mine = 2 * x + y
    copies = []
    for a in range(len(srcs)):
        lo, hi = dests[a]
        receives = jnp.logical_and(mine >= lo, mine < hi)
        for k, (px, py) in enumerate(_other_chips(x, y)):
            target = 2 * px + py
            cp = pltpu.make_async_remote_copy(
                src_ref=srcs[a].at[jnp.clip(target - lo, 0, hi - lo - 1)], dst_ref=dsts[a].at[mine],
                send_sem=send_sems.at[a * 3 + k], recv_sem=recv_sems.at[a * 3 + k],
                device_id=(px, py, c), device_id_type=_MESH)
            copies.append((cp, jnp.logical_and(target >= lo, target < hi), receives))
    return copies


def _start_copies(copies):
    for cp, sends, _ in copies:
        pl.when(sends)(cp.start)


def _finish_copies(copies):
    for cp, _, receives in copies:
        pl.when(receives)(cp.wait_recv)
    for cp, sends, _ in copies:
        pl.when(sends)(cp.wait_send)


def _host_specs(send):
    n = len(send)
    hbm = [pl.BlockSpec(memory_space=pl.ANY)] * n
    shapes = [_SDS((N_CHIPS,) + a.shape[1:], a.dtype) for a in send]
    sems = [pltpu.SemaphoreType.DMA((3 * n,)), pltpu.SemaphoreType.DMA((3 * n,))] if n else []
    return hbm, hbm, shapes, sems


def _swap_halves(arrays, name):
    n = len(arrays)

    def body(*refs):
        srcs, dsts = refs[:n], refs[n:2 * n]
        send_sems, recv_sems = refs[2 * n:]
        x, y, c = _place()
        copies = []
        for a in range(n):
            half = arrays[a].shape[1] // 2
            cp = pltpu.make_async_remote_copy(
                src_ref=srcs[a].at[:, pl.ds(pl.multiple_of((1 - c) * half, 16), half)], dst_ref=dsts[a],
                send_sem=send_sems.at[a], recv_sem=recv_sems.at[a],
                device_id=(x, y, 1 - c), device_id_type=_MESH)
            cp.start()
            copies.append(cp)
        for cp in copies:
            cp.wait_recv()
        for cp in copies:
            cp.wait_send()

    return pl.pallas_call(
        body, name=name,
        in_specs=[_ANY] * n, out_specs=[_ANY] * n,
        out_shape=[_SDS((a.shape[0], a.shape[1] // 2, a.shape[2]), a.dtype) for a in arrays],
        scratch_shapes=[pltpu.SemaphoreType.DMA((n,)), pltpu.SemaphoreType.DMA((n,))],
    )(*arrays)


def _join_halves(arrays, by_cols):
    n = len(arrays)

    def body(*refs):
        dsts = refs[n:2 * n]
        send_sems, recv_sems = refs[2 * n:]
        x, y, c = _place()
        copies = []
        for a in range(n):
            if by_cols[a]:
                half = arrays[a].shape[1] // 2
                mine = dsts[a].at[:, pl.ds(pl.multiple_of(c * half, 128), half)]
            else:
                half = arrays[a].shape[0] // 2
                mine = dsts[a].at[pl.ds(pl.multiple_of(c * half, 16), half)]
            cp = pltpu.make_async_remote_copy(
                src_ref=mine, dst_ref=mine, send_sem=send_sems.at[a], recv_sem=recv_sems.at[a],
                device_id=(x, y, 1 - c), device_id_type=_MESH)
            cp.start()
            copies.append(cp)
        for cp in copies:
            cp.wait_recv()
        for cp in copies:
            cp.wait_send()

    return pl.pallas_call(
        body, name="join_halves",
        in_specs=[_ANY] * n, out_specs=[_ANY] * n,
        out_shape=[_SDS(a.shape, a.dtype) for a in arrays],
        input_output_aliases={a: a for a in range(n)},
        scratch_shapes=[pltpu.SemaphoreType.DMA((n,)), pltpu.SemaphoreType.DMA((n,))],
    )(*arrays)


N_DEV = 8
SMALL_ROWS = 32
SMALL_COLS = XBC_W


def _gather_small(packed):
    def body(src_ref, dst_ref, send_sems, recv_sems, local_sem):
        x, y, c = _place()
        me = 4 * x + 2 * y + c
        own = pltpu.make_async_copy(src_ref, dst_ref.at[me], local_sem)
        own.start()
        copies = []
        for k in range(1, N_DEV):
            bx, by, bc = (k >> 2) & 1, (k >> 1) & 1, k & 1
            peer = (x + bx - 2 * x * bx, y + by - 2 * y * by, c + bc - 2 * c * bc)
            cp = pltpu.make_async_remote_copy(
                src_ref=src_ref, dst_ref=dst_ref.at[me], send_sem=send_sems.at[k - 1], recv_sem=recv_sems.at[k - 1],
                device_id=peer, device_id_type=_MESH)
            cp.start()
            copies.append(cp)
        for cp in copies:
            cp.wait_recv()
        for cp in copies:
            cp.wait_send()
        own.wait()

    return pl.pallas_call(
        body, name="gather_small",
        in_specs=[pl.BlockSpec(memory_space=_VMEM)], out_specs=pl.BlockSpec(memory_space=_VMEM),
        out_shape=_SDS((N_DEV, SMALL_ROWS, SMALL_COLS), F32),
        scratch_shapes=[pltpu.SemaphoreType.DMA((N_DEV - 1,)), pltpu.SemaphoreType.DMA((N_DEV - 1,)),
                        pltpu.SemaphoreType.DMA],
    )(packed)


def _adamw(w, g, m, v):
    m = ADAM_B1 * m + (1.0 - ADAM_B1) * g
    v = ADAM_B2 * v + (1.0 - ADAM_B2) * (g * g)
    m_hat = m / (1.0 - ADAM_B1 ** ADAM_STEP)
    v_hat = v / (1.0 - ADAM_B2 ** ADAM_STEP)
    delta = -ADAM_LR * (m_hat / (jnp.sqrt(v_hat) + ADAM_EPS) + ADAM_WD * w)
    return delta, m, v


def _sum_slabs(slabs, core, name, transposed=False):
    _, h, c = slabs.shape
    tr = 128
    nblk = h // tr

    def body(core_ref, s_ref, o_ref):
        tot = ((s_ref[0].astype(F32) + s_ref[1].astype(F32)) + s_ref[2].astype(F32)) + s_ref[3].astype(F32)
        o_ref[...] = tot.T if transposed else tot

    if transposed:
        out_spec = pl.BlockSpec((c, tr), lambda i, core_ref: (0, core_ref[0] * nblk + i))
        out_shape = _SDS((c, 2 * h), F32)
    else:
        out_spec = pl.BlockSpec((tr, c), lambda i, core_ref: (core_ref[0] * nblk + i, 0))
        out_shape = _SDS((2 * h, c), F32)
    grid_spec = pltpu.PrefetchScalarGridSpec(
        num_scalar_prefetch=1, grid=(nblk,),
        in_specs=[pl.BlockSpec((N_CHIPS, tr, c), lambda i, core_ref: (0, i, 0))],
        out_specs=out_spec)
    return pl.pallas_call(
        body, name=name, grid_spec=grid_spec, out_shape=out_shape,
        compiler_params=_params(("arbitrary",)),
    )(core, slabs)


def _add_halves(own, recv, core, name):
    ns, r, c = own.shape
    half = r // 2
    tr = 128
    nblk = half // tr

    def body(core_ref, a_ref, b_ref, o_ref):
        o_ref[...] = (a_ref[...].astype(F32) + b_ref[...].astype(F32)).astype(o_ref.dtype)

    grid_spec = pltpu.PrefetchScalarGridSpec(
        num_scalar_prefetch=1, grid=(nblk,),
        in_specs=[pl.BlockSpec((ns, tr, c), lambda i, core_ref: (0, core_ref[0] * nblk + i, 0)),
                  pl.BlockSpec((ns, tr, c), lambda i, core_ref: (0, i, 0))],
        out_specs=pl.BlockSpec((ns, tr, c), lambda i, core_ref: (0, i, 0)))
    return pl.pallas_call(
        body, name=name, grid_spec=grid_spec, out_shape=_SDS((ns, half, c), own.dtype),
        compiler_params=_params(("arbitrary",)),
    )(core, own, recv)


def _update_big(w, m, v, g, name):
    r, c = w.shape

    def body(w_ref, m_ref, v_ref, g_ref, d_ref, mo_ref, vo_ref):
        delta, m_new, v_new = _adamw(w_ref[...], g_ref[...], m_ref[...], v_ref[...])
        d_ref[...] = delta
        mo_ref[...] = m_new
        vo_ref[...] = v_new

    if r % 128 == 0:
        steps, spec = r // 128, pl.BlockSpec((128, c), lambda i: (i, 0))
    else:
        steps, spec = c // 128, pl.BlockSpec((r, 128), lambda i: (0, i))
    return pl.pallas_call(
        body, name=name, grid=(steps,),
        in_specs=[spec] * 4, out_specs=[spec] * 3,
        out_shape=[_SDS((r, c), F32)] * 3,
        compiler_params=_params(("arbitrary",)),
    )(w, m, v, g)


_ROW = dict(norm_w=0, sb_norm_w=1, ssd_norm_w=2, final_norm_w=3, conv_b=4, dt_bias=5, a_log=6, d_skip=7,
            conv_w=8, sq_err=12, meta_tokens=16)
_SMALL = ("meta_tokens", "norm_w", "conv_w", "conv_b", "dt_bias", "a_log", "d_skip", "sb_norm_w", "ssd_norm_w",
          "final_norm_w")


def _pack_small(sq_err, grads):
    def rowpad(a):
        return jnp.pad(a, ((0, 0), (0, SMALL_COLS - a.shape[1])))

    rows = [rowpad(grads[k]) for k in ("norm_w", "sb_norm_w", "ssd_norm_w", "final_norm_w", "conv_b", "dt_bias", "a_log", "d_skip")]
    rows.append(grads["conv_w"])
    rows.append(rowpad(sq_err))
    rows.append(jnp.zeros((3, SMALL_COLS), F32))
    rows.append(rowpad(grads["meta_tokens"]))
    return jnp.concatenate(rows, axis=0)


def _update_small(gathered, ws, ms, vs):
    names = _SMALL
    n = len(names)

    def body(*refs):
        g_ref = refs[0]
        w_refs, m_refs, v_refs = refs[1:1 + n], refs[1 + n:1 + 2 * n], refs[1 + 2 * n:1 + 3 * n]
        outs = refs[1 + 3 * n:]
        loss_ref = outs[0]
        go, do, mo, vo = outs[1:1 + n], outs[1 + n:1 + 2 * n], outs[1 + 2 * n:1 + 3 * n], outs[1 + 3 * n:1 + 4 * n]
        tot = g_ref[0]
        for d in range(1, N_DEV):
            tot = tot + g_ref[d]
        x, y, _ = _place()
        chip = 2 * x + y
        loss_ref[...] = jnp.broadcast_to(
            0.5 * jnp.sum(tot[_ROW["sq_err"]:_ROW["sq_err"] + 1, 0:D_MODEL], axis=1, keepdims=True) / D_MODEL, (1, 128))
        for idx, nm in enumerate(names):
            r0 = _ROW[nm]
            rows, cols = w_refs[idx].shape
            if nm in ("conv_w", "meta_tokens"):
                g = jnp.zeros((rows, cols), F32)
                for j in range(N_CHIPS):
                    g = g + jnp.where(chip == j, tot[r0:r0 + rows, j * cols:(j + 1) * cols], 0.0)
            else:
                g = tot[r0:r0 + rows, 0:cols]
            delta, m_new, v_new = _adamw(w_refs[idx][...], g, m_refs[idx][...], v_refs[idx][...])
            go[idx][...] = g
            do[idx][...] = delta
            mo[idx][...] = m_new
            vo[idx][...] = v_new

    shapes = [_SDS(ws[nm].shape, F32) for nm in names]
    vm = pl.BlockSpec(memory_space=_VMEM)
    res = pl.pallas_call(
        body, name="update_small",
        in_specs=[vm] * (1 + 3 * n), out_specs=[vm] * (1 + 4 * n),
        out_shape=[_SDS((1, 128), F32)] + shapes * 4,
    )(gathered, *[ws[nm] for nm in names], *[ms[nm] for nm in names], *[vs[nm] for nm in names])
    loss = res[0][0, 0]
    g = dict(zip(names, res[1:1 + n]))
    d = dict(zip(names, res[1 + n:1 + 2 * n]))
    m = dict(zip(names, res[1 + 2 * n:1 + 3 * n]))
    v = dict(zip(names, res[1 + 3 * n:1 + 4 * n]))
    return loss, g, d, m, v


_WEIGHTS = ("meta_tokens", "norm_w", "w_in", "conv_w", "conv_b", "dt_bias", "a_log", "d_skip", "sb_norm_w",
            "ssd_norm_w", "w_out", "final_norm_w")


def kernel(x, meta_tokens, norm_w, w_in, conv_w, conv_b, dt_bias, a_log, d_skip, sb_norm_w, ssd_norm_w, w_out, final_norm_w, loss_target, m_meta_tokens, m_norm_w, m_w_in, m_conv_w, m_conv_b, m_dt_bias, m_a_log, m_d_skip, m_sb_norm_w, m_ssd_norm_w, m_w_out, m_final_norm_w, v_meta_tokens, v_norm_w, v_w_in, v_conv_w, v_conv_b, v_dt_bias, v_a_log, v_d_skip, v_sb_norm_w, v_ssd_norm_w, v_w_out, v_final_norm_w):
    given = dict(meta_tokens=meta_tokens, norm_w=norm_w, w_in=w_in, conv_w=conv_w, conv_b=conv_b, dt_bias=dt_bias,
                 a_log=a_log, d_skip=d_skip, sb_norm_w=sb_norm_w, ssd_norm_w=ssd_norm_w, w_out=w_out,
                 final_norm_w=final_norm_w)
    mom = dict(meta_tokens=m_meta_tokens, norm_w=m_norm_w, w_in=m_w_in, conv_w=m_conv_w, conv_b=m_conv_b,
               dt_bias=m_dt_bias, a_log=m_a_log, d_skip=m_d_skip, sb_norm_w=m_sb_norm_w, ssd_norm_w=m_ssd_norm_w,
               w_out=m_w_out, final_norm_w=m_final_norm_w)
    var = dict(meta_tokens=v_meta_tokens, norm_w=v_norm_w, w_in=v_w_in, conv_w=v_conv_w, conv_b=v_conv_b,
               dt_bias=v_dt_bias, a_log=v_a_log, d_skip=v_d_skip, sb_norm_w=v_sb_norm_w, ssd_norm_w=v_ssd_norm_w,
               w_out=v_w_out, final_norm_w=v_final_norm_w)
    seq = x.shape[1]

    def two_d(a):
        return a.reshape((-1, a.shape[-1])) if a.ndim != 2 else a

    def rows_first(a):
        return jnp.transpose(a, (2, 0, 1)).reshape(W_IN_SHARD, D_MODEL)

    def rows_last(a):
        return jnp.transpose(a.reshape(W_IN_SHARD, 1, D_MODEL), (1, 2, 0))

    w_in_t, m_in_t, v_in_t = rows_first(w_in), rows_first(m_w_in), rows_first(v_w_in)

    g_win, g_wout, g_meta, g_cw = _gather_shards(
        [w_in_t.astype(_MXU), w_out[0].astype(_MXU), meta_tokens, conv_w[0]], 2)
    w_t = jnp.pad(g_win.reshape(D_IN, D_MODEL), ((0, W_ALL - D_IN), (0, 0)))
    w_out_full = g_wout.reshape(2 * D_MODEL, D_MODEL)
    meta_full = jnp.swapaxes(g_meta, 0, 1).reshape(N_META, D_MODEL)
    conv_w_full = jnp.swapaxes(g_cw, 0, 1).reshape(4, XBC_W)

    core = lax.axis_index("c").astype(jnp.int32).reshape(1)

    def early(g_win_hi, g_wout):
        slab_out = g_wout.reshape(N_CHIPS, W_OUT_SHARD, D_MODEL).astype(_MXU)
        sib_hi, sib_out = _swap_halves([g_win_hi, slab_out], "swap_halves_early")
        return ((_add_halves(g_win_hi, sib_hi, core, "chip_sum_w_in_hi"),
                 _add_halves(slab_out, sib_out, core, "chip_sum_w_out")), ((2, N_CHIPS), (0, N_CHIPS)))

    def late(g_win_lo):
        (sib_lo,) = _swap_halves([g_win_lo], "swap_halves_late")
        return (_add_halves(g_win_lo, sib_lo, core, "chip_sum_w_in_lo"),), ((0, 2),)

    sq_err, g_x, grads = _device_grads(
        x.reshape(seq, D_MODEL), loss_target.reshape(seq, D_MODEL), meta_full, norm_w, w_t, conv_w_full,
        conv_b, dt_bias, a_log, d_skip, sb_norm_w, ssd_norm_w, w_out_full, final_norm_w.reshape(1, D_MODEL),
        exchange=dict(early=early, late=late))
    chip_in_hi, chip_out, chip_in_lo = grads["sent"]
    got_hi, got_out, got_lo = grads["arrived"]
    chip = 2 * lax.axis_index("x") + lax.axis_index("y")
    chip_in = jnp.concatenate([chip_in_lo, chip_in_hi], axis=0)
    got_in = jnp.where(chip >= 2, got_hi, got_lo)

    def with_own(got, sent):
        own = lax.dynamic_slice(sent, (chip, 0, 0), (1,) + sent.shape[1:])
        return lax.dynamic_update_slice(got, own, (chip, 0, 0))

    g_in, g_out = _join_halves([_sum_slabs(with_own(got_in, chip_in), core, "sum_w_in", transposed=True),
                                _sum_slabs(with_own(got_out, chip_out), core, "sum_w_out")], (True, False))
    g_in = lax.dynamic_slice(g_in, (4 * chip, 0), (W_IN_SHARD, D_MODEL))
    big = dict(w_in=tuple(rows_last(a) for a in (g_in,) + tuple(_update_big(w_in_t, m_in_t, v_in_t, g_in, "update_w_in"))),
               w_out=(g_out,) + tuple(_update_big(w_out[0], m_w_out[0], v_w_out[0], g_out, "update_w_out")))

    gathered = _gather_small(_pack_small(sq_err, grads))
    loss, sg, sd, sm, sv = _update_small(
        gathered, {k: two_d(given[k]) for k in _SMALL}, {k: two_d(mom[k]) for k in _SMALL},
        {k: two_d(var[k]) for k in _SMALL})

    out = {}
    for idx, group in enumerate((sg, sd, sm, sv)):
        for k in _SMALL:
            out[(idx, k)] = group[k].reshape(given[k].shape)
        for k in ("w_in", "w_out"):
            out[(idx, k)] = big[k][idx].reshape(given[k].shape)
    return (loss, g_x.reshape(x.shape), *[out[(idx, k)] for idx in range(4) for k in _WEIGHTS])
```

```python
import functools
import math

import jax
import jax.numpy as jnp
from jax import lax
from jax.experimental import pallas as pl
from jax.experimental.pallas import tpu as pltpu

F32 = jnp.float32
_MXU = jnp.bfloat16

D_MODEL = 1024
N_META = 16
PAD = 112
OFF = PAD + N_META
TM = 128
CHUNK = 64
SB_W = 1024
SSD_W = 1024
N_HEADS = 16
HEAD = 64
N_GROUPS = 2
N_STATE = 128
XBC_W = SSD_W + 2 * N_GROUPS * N_STATE
N_MAIN = 4 * SB_W + SSD_W + XBC_W
QKV_W = 3 * SB_W
REST_W = N_MAIN - QKV_W
COL_GATE = 3 * SB_W
COL_Z = 4 * SB_W
COL_XBC = 5 * SB_W
D_IN = N_MAIN + N_HEADS
W_ALL = N_MAIN + 128
WIN_STEP = 1664
WIN_W = 1792
EPS = 1e-5
N_CHIPS = 4
W_IN_SHARD = D_IN // N_CHIPS
W_OUT_SHARD = 2 * D_MODEL // N_CHIPS

ADAM_LR = 0.001
ADAM_B1 = 0.9
ADAM_B2 = 0.999
ADAM_EPS = 1e-08
ADAM_WD = 0.01
ADAM_STEP = 10

_SDS = jax.ShapeDtypeStruct
_NT = (((1,), (1,)), ((), ()))
_TN = (((0,), (0,)), ((), ()))
_VMEM = pltpu.VMEM


def _params(sem=None, vmem_mb=None):
    kw = {}
    if sem is not None:
        kw["dimension_semantics"] = sem
    if vmem_mb is not None:
        kw["vmem_limit_bytes"] = vmem_mb * 1024 * 1024
    return pltpu.CompilerParams(**kw)


def _mm(a, b):
    return jnp.dot(a.astype(_MXU), b.astype(_MXU), preferred_element_type=F32)


def _mm_nt(a, b):
    return lax.dot_general(a.astype(_MXU), b.astype(_MXU), _NT, preferred_element_type=F32)


def _mm_tn(a, b):
    return lax.dot_general(a.astype(_MXU), b.astype(_MXU), _TN, preferred_element_type=F32)


def _split(x, parts):
    out = []
    r = x
    for _ in range(parts):
        p = r.astype(_MXU)
        out.append(p)
        r = r - p.astype(F32)
    return out


def _sel_right(x, m01, parts=3):
    acc = None
    for p in _split(x, parts):
        t = jnp.dot(p, m01, preferred_element_type=F32)
        acc = t if acc is None else acc + t
    return acc


def _sel_left(m01, x, parts=3):
    acc = None
    for p in _split(x, parts):
        t = jnp.dot(m01, p, preferred_element_type=F32)
        acc = t if acc is None else acc + t
    return acc


def _iota(shape, axis):
    return lax.broadcasted_iota(jnp.int32, shape, axis)


def _sigmoid(x):
    return 1.0 / (1.0 + jnp.exp(-x))


def _prep(x2d, meta_full, norm_w):
    seq = x2d.shape[0]
    lp = seq + OFF
    nb = lp // TM

    def body(x_ref, meta_ref, w_ref, h0_ref, u_ref, ut_ref):
        i = pl.program_id(0)

        @pl.when(i == 0)
        def _():
            h0_ref[...] = jnp.concatenate([jnp.zeros((PAD, D_MODEL), F32), meta_ref[...]], axis=0)

        @pl.when(i > 0)
        def _():
            h0_ref[...] = x_ref[...]

        h = h0_ref[...]
        rs = lax.rsqrt(jnp.mean(h * h, axis=-1, keepdims=True) + EPS)
        u = (h * rs * w_ref[...]).astype(_MXU)
        u_ref[...] = u
        ut_ref[...] = u.T

    return pl.pallas_call(
        body, name="prep", grid=(nb,),
        in_specs=[pl.BlockSpec((TM, D_MODEL), lambda i: (jnp.maximum(i - 1, 0), 0)),
                  pl.BlockSpec((N_META, D_MODEL), lambda i: (0, 0)),
                  pl.BlockSpec((1, D_MODEL), lambda i: (0, 0))],
        out_specs=[pl.BlockSpec((TM, D_MODEL), lambda i: (i, 0)),
                   pl.BlockSpec((TM, D_MODEL), lambda i: (i, 0)),
                   pl.BlockSpec((D_MODEL, TM), lambda i: (0, i))],
        out_shape=[_SDS((lp, D_MODEL), F32), _SDS((lp, D_MODEL), _MXU), _SDS((D_MODEL, lp), _MXU)],
        compiler_params=_params(("arbitrary",)),
    )(x2d, meta_full, norm_w)


def _inproj(u, w_t):
    lp = u.shape[0]
    tn = 512

    nq = QKV_W // tn

    def body(u_ref, w_ref, wdt_ref, qkv_ref, rest_ref, odt_ref):
        j = pl.program_id(0)
        res = lax.dot_general(u_ref[...], w_ref[...], _NT, preferred_element_type=F32)

        @pl.when(j < nq)
        def _():
            qkv_ref[...] = res.astype(qkv_ref.dtype)

        @pl.when(j >= nq)
        def _():
            rest_ref[...] = res

        @pl.when(j == 0)
        def _():
            odt_ref[...] = lax.dot_general(u_ref[...], wdt_ref[...], _NT, preferred_element_type=F32)

    return pl.pallas_call(
        body, name="inproj", grid=(N_MAIN // tn,),
        in_specs=[pl.BlockSpec((lp, D_MODEL), lambda j: (0, 0)),
                  pl.BlockSpec((tn, D_MODEL), lambda j: (j, 0)),
                  pl.BlockSpec((128, D_MODEL), lambda j: (N_MAIN // 128, 0))],
        out_specs=[pl.BlockSpec((lp, tn), lambda j: (0, jnp.minimum(j, nq - 1))),
                   pl.BlockSpec((lp, tn), lambda j: (0, jnp.maximum(j - nq, 0))),
                   pl.BlockSpec((lp, 128), lambda j: (0, 0))],
        out_shape=[_SDS((lp, QKV_W), _MXU), _SDS((lp, REST_W), F32), _SDS((lp, 128), F32)],
        compiler_params=_params(("arbitrary",), 48),
    )(u, w_t, w_t)


SB_WINDOW = 2
SB_TOP = 16
SB_DEAD = -104.0


def _sb_logs(qh, kwin):
    z = lax.dot_general(qh, kwin, _NT, preferred_element_type=F32)
    e = jnp.exp(-jnp.abs(z))
    l1p = jnp.log(1.0 + e)
    lk_full = -(jnp.maximum(z, 0.0) + l1p)
    ls = jnp.minimum(z, 0.0) - l1p
    return z, e, ls, lk_full


def _blk(a, b):
    return a[:, b * TM:(b + 1) * TM]


def _stacked_sel(blocks, m01):
    n = len(blocks)
    rows = blocks[0].shape[0]
    pieces = [_split(b, 2) for b in blocks]
    stacked = jnp.concatenate([p[0] for p in pieces] + [p[1] for p in pieces], axis=0)
    res = jnp.dot(stacked, m01, preferred_element_type=F32)
    return [res[j * rows:(j + 1) * rows] + res[(n + j) * rows:(n + j + 1) * rows] for j in range(n)]


def _sb_weights(ls, lk_full, run, last_mask, upper, n):
    lk = [_blk(lk_full, b) for b in range(n)]
    lk[n - 1] = jnp.where(last_mask, lk[n - 1], 0.0)
    aft = _stacked_sel(lk, upper)
    w = [None] * n
    for b in range(n - 1, -1, -1):
        wb = jnp.exp(_blk(ls, b) + aft[b] + run)
        w[b] = jnp.where(last_mask, wb, 0.0) if b == n - 1 else wb
        run = run + jnp.sum(lk[b], axis=1, keepdims=True)
    return w, run


def _sb_alive(run_scr):
    top = jnp.max(run_scr[:, 0:SB_TOP, :]) > SB_DEAD
    rest = jnp.max(run_scr[:, SB_TOP:, :]) > SB_DEAD
    return top.astype(jnp.int32), rest.astype(jnp.int32)


def _sb_walk(i, key_set, strict, run_scr):
    @pl.when(i >= SB_WINDOW - 1)
    def _():
        key_set(i - (SB_WINDOW - 1), SB_WINDOW, strict, TM)

    start = jnp.where(i >= SB_WINDOW - 1, i - SB_WINDOW, i)

    def cond(c):
        return jnp.logical_and(c[0] >= 0, c[1] + c[2] > 0)

    def step(c):
        kb, _, rest = c
        mask = jnp.logical_or(strict, kb < i)

        @pl.when(rest > 0)
        def _():
            key_set(kb, 1, mask, TM)

        @pl.when(rest == 0)
        def _():
            key_set(kb, 1, mask, SB_TOP)

        return (kb - 1,) + _sb_alive(run_scr)

    lax.while_loop(cond, step, (start,) + _sb_alive(run_scr))


def _sb_fwd(qkv):
    lp = qkv.shape[0]
    nb = lp // TM

    def body(q_ref, k_ref, v_ref, o_ref, olo_ref, acc, run_scr):
        i = pl.program_id(1)
        lane = _iota((TM, TM), 1)
        row = _iota((TM, TM), 0)
        head0 = lane < HEAD
        upper = (row > lane).astype(_MXU)
        strict = lane < row
        q = q_ref[...] * (1.0 / math.sqrt(HEAD))
        qh = (jnp.where(head0, q, 0.0).astype(_MXU), jnp.where(head0, 0.0, q).astype(_MXU))

        def key_set(first, n, last_mask, nrows):
            off = pl.multiple_of(first * TM, TM)
            kwin = k_ref[pl.ds(off, n * TM), :].astype(_MXU)
            vwin = v_ref[pl.ds(off, n * TM), :].astype(_MXU)
            for hh in range(2):
                run = run_scr[hh, 0:nrows, 0:1]
                _, _, ls, lk_full = _sb_logs(qh[hh][0:nrows], kwin)
                w, run = _sb_weights(ls, lk_full, run, last_mask[0:nrows], upper, n)
                pieces = [_split(wb, 2) for wb in w]
                stacked = jnp.concatenate(
                    [jnp.concatenate([p[0] for p in pieces], axis=1), jnp.concatenate([p[1] for p in pieces], axis=1)], axis=0)
                res = jnp.dot(stacked, vwin, preferred_element_type=F32)
                acc[hh, 0:nrows] += res[0:nrows]
                acc[2 + hh, 0:nrows] += res[nrows:2 * nrows]
                run_scr[hh, 0:nrows] = jnp.broadcast_to(run, (nrows, TM))

        acc[...] = jnp.zeros_like(acc)
        run_scr[...] = jnp.zeros_like(run_scr)
        _sb_walk(i, key_set, strict, run_scr)
        o_ref[...] = jnp.where(head0, acc[0], acc[1])
        olo_ref[...] = jnp.where(head0, acc[2], acc[3])

    npair = SB_W // TM
    blk = pl.BlockSpec((TM, TM), lambda p, i: (i, p))
    return pl.pallas_call(
        body, name="sb_fwd", grid=(npair, nb),
        in_specs=[blk,
                  pl.BlockSpec((lp, TM), lambda p, i: (0, npair + p)),
                  pl.BlockSpec((lp, TM), lambda p, i: (0, 2 * npair + p))],
        out_specs=[blk, blk],
        out_shape=[_SDS((lp, SB_W), F32), _SDS((lp, SB_W), F32)],
        scratch_shapes=[pltpu.VMEM((4, TM, TM), F32), pltpu.VMEM((2, TM, TM), F32)],
        compiler_params=_params(("arbitrary", "arbitrary")),
    )(qkv, qkv, qkv)


def _sb_bwd(qkv, o_sb, o_lo, d_o, d_proj, send=(), dests=()):
    lp = qkv.shape[0]
    nb = lp // TM
    npair = SB_W // TM
    scale = 1.0 / math.sqrt(HEAD)
    n = len(send)
    host_in, host_out, host_shapes, host_sems = _host_specs(send)

    def body(q_ref, k_ref, v_ref, o_ref, olo_ref, do_ref, dproj_in, *rest):
        srcs, dproj_ref, dsts = rest[:n], rest[n], rest[n + 1:2 * n + 1]
        dq_all, dk_ref, dv_ref, stage, sems, dq_acc, run_scr, gsum_scr = rest[2 * n + 1:2 * n + 9]
        host_sem_refs = rest[2 * n + 9:]
        p = pl.program_id(0)
        i = pl.program_id(1)

        if n:
            @pl.when(jnp.logical_and(p == 0, i == 0))
            def _():
                _start_copies(_slab_copies(srcs, dsts, dests, *host_sem_refs))

        @pl.when(i == 0)
        def _():
            dk_ref[...] = jnp.zeros_like(dk_ref)
            dv_ref[...] = jnp.zeros_like(dv_ref)

        lane = _iota((TM, TM), 1)
        row = _iota((TM, TM), 0)
        head0 = lane < HEAD
        hmask = (head0, jnp.logical_not(head0))
        upper = (row > lane).astype(_MXU)
        lower_incl = (row >= lane).astype(_MXU)
        strict = lane < row
        q = q_ref[...] * scale
        do = do_ref[...]
        prod = do.astype(_MXU).astype(F32) * (o_ref[...] + olo_ref[...])
        qh = tuple(jnp.where(m, q, 0.0).astype(_MXU) for m in hmask)
        doh = tuple(jnp.where(m, do, 0.0).astype(_MXU) for m in hmask)
        gtot = tuple(jnp.sum(jnp.where(m, prod, 0.0), axis=1, keepdims=True) for m in hmask)

        def key_set(first, n, last_mask, nrows):
            off = pl.multiple_of(first * TM, TM)
            kf = k_ref[pl.ds(off, n * TM), :]
            kwin = kf.astype(_MXU)
            vwin = v_ref[pl.ds(off, n * TM), :].astype(_MXU)
            last_mask = last_mask[0:nrows]
            dk_win = None
            for hh in range(2):
                run = run_scr[hh, 0:nrows, 0:1]
                gsum = gsum_scr[hh, 0:nrows, 0:1]
                z, e, ls, lk_full = _sb_logs(qh[hh][0:nrows], kwin)
                w, run = _sb_weights(ls, lk_full, run, last_mask, upper, n)
                r = 1.0 / (1.0 + e)
                er = e * r
                pos = z >= 0.0
                beta = jnp.where(pos, r, er)
                one_m_beta = jnp.where(pos, er, r)
                dw = lax.dot_general(doh[hh][0:nrows], vwin, _NT, preferred_element_type=F32)
                g = [_blk(dw, b) * w[b] for b in range(n)]
                suffix = _stacked_sel(g, lower_incl)
                dz = [None] * n
                for b in range(n - 1, -1, -1):
                    prefix = gtot[hh][0:nrows] - gsum - suffix[b]
                    d = g[b] * _blk(one_m_beta, b) - _blk(beta, b) * prefix
                    dz[b] = (jnp.where(last_mask, d, 0.0) if b == n - 1 else d).astype(_MXU)
                    gsum = gsum + jnp.sum(g[b], axis=1, keepdims=True)
                dzw = jnp.concatenate(dz, axis=1)
                ww = jnp.concatenate([wb.astype(_MXU) for wb in w], axis=1)
                kh = jnp.where(hmask[hh][0:1, :], kf, 0.0).astype(_MXU)
                dq_acc[0:nrows] += jnp.dot(dzw, kh, preferred_element_type=F32)
                dk_h = lax.dot_general(dzw, qh[hh][0:nrows], _TN, preferred_element_type=F32)
                dv_h = lax.dot_general(ww, doh[hh][0:nrows], _TN, preferred_element_type=F32)
                dk_win = (dk_h, dv_h) if dk_win is None else (dk_win[0] + dk_h, dk_win[1] + dv_h)
                run_scr[hh, 0:nrows] = jnp.broadcast_to(run, (nrows, TM))
                gsum_scr[hh, 0:nrows] = jnp.broadcast_to(gsum, (nrows, TM))
            dk_ref[pl.ds(off, n * TM), :] += dk_win[0]
            dv_ref[pl.ds(off, n * TM), :] += dk_win[1]

        dq_acc[...] = jnp.zeros_like(dq_acc)
        run_scr[...] = jnp.zeros_like(run_scr)
        gsum_scr[...] = jnp.zeros_like(gsum_scr)

        _sb_walk(i, key_set, strict, run_scr)
        dq_all[pl.ds(pl.multiple_of(i * TM, TM), TM), :] = dq_acc[...] * scale

        @pl.when(i == nb - 1)
        def _():
            copies = []
            for s, src in enumerate((dq_all, dk_ref, dv_ref)):
                stage[s] = src[...].astype(_MXU)
                col = pl.multiple_of((s * npair + p) * TM, TM)
                copies.append(pltpu.make_async_copy(stage.at[s], dproj_ref.at[:, pl.ds(col, TM)], sems.at[s]))
                copies[-1].start()
            for cp in copies:
                cp.wait()

        if n:
            @pl.when(jnp.logical_and(p == npair - 1, i == nb - 1))
            def _():
                _finish_copies(_slab_copies(srcs, dsts, dests, *host_sem_refs))

    blk = pl.BlockSpec((TM, TM), lambda p, i: (i, p))
    res = pl.pallas_call(
        body, name="sb_bwd", grid=(npair, nb),
        in_specs=[blk,
                  pl.BlockSpec((lp, TM), lambda p, i: (0, npair + p)),
                  pl.BlockSpec((lp, TM), lambda p, i: (0, 2 * npair + p)),
                  blk, blk, blk, pl.BlockSpec(memory_space=pl.ANY)] + host_in,
        out_specs=[pl.BlockSpec(memory_space=pl.ANY)] + host_out,
        out_shape=[_SDS(d_proj.shape, d_proj.dtype)] + host_shapes,
        input_output_aliases={6: 0},
        scratch_shapes=[pltpu.VMEM((lp, TM), F32), pltpu.VMEM((lp, TM), F32), pltpu.VMEM((lp, TM), F32),
                        pltpu.VMEM((3, lp, TM), _MXU), pltpu.SemaphoreType.DMA((3,)),
                        pltpu.VMEM((TM, TM), F32), pltpu.VMEM((2, TM, TM), F32), pltpu.VMEM((2, TM, TM), F32)] + host_sems,
        compiler_params=_params(("arbitrary", "arbitrary")),
    )(qkv, qkv, qkv, o_sb, o_lo, d_o, d_proj, *send)
    return res[0], list(res[1:])


def _conv_pre(x_ref, w_ref, b_ref, lp):
    n = lp - 8
    w = w_ref[...]
    pre = (x_ref[pl.ds(5, n), :] * w[0:1, :] + x_ref[pl.ds(6, n), :] * w[1:2, :]
           + x_ref[pl.ds(7, n), :] * w[2:3, :] + x_ref[pl.ds(8, n), :] * w[3:4, :]) + b_ref[...]
    live = (_iota((n, 128), 0) + 8) >= PAD
    return pre, live


def _conv_fwd(proj, dt_raw, conv_w, conv_b, dt_bias128):
    lp = proj.shape[0]
    nblk = XBC_W // 128
    c0 = (COL_XBC - QKV_W) // 128

    def body(x_ref, w_ref, b_ref, dtr_ref, dtb_ref, o_ref, dt_ref):
        pre, live = _conv_pre(x_ref, w_ref, b_ref, lp)
        act = pre * _sigmoid(pre)
        o_ref[pl.ds(0, 8), :] = jnp.zeros((8, 128), F32)
        o_ref[pl.ds(8, lp - 8), :] = jnp.where(live, act, 0.0)

        @pl.when(pl.program_id(0) == 0)
        def _():
            s = dtr_ref[...] + dtb_ref[...]
            sp = jnp.maximum(s, 0.0) + jnp.log(1.0 + jnp.exp(-jnp.abs(s)))
            dt_ref[...] = jnp.where(_iota((lp, 128), 0) >= PAD, sp, 0.0)

    return pl.pallas_call(
        body, name="conv_fwd", grid=(nblk,),
        in_specs=[pl.BlockSpec((lp, 128), lambda j: (0, c0 + j)),
                  pl.BlockSpec((4, 128), lambda j: (0, j)),
                  pl.BlockSpec((1, 128), lambda j: (0, j)),
                  pl.BlockSpec((lp, 128), lambda j: (0, 0)),
                  pl.BlockSpec((1, 128), lambda j: (0, 0))],
        out_specs=[pl.BlockSpec((lp, 128), lambda j: (0, j)),
                   pl.BlockSpec((lp, 128), lambda j: (0, 0))],
        out_shape=[_SDS((lp, XBC_W), F32), _SDS((lp, 128), F32)],
        compiler_params=_params(("arbitrary",)),
    )(proj, conv_w, conv_b, dt_raw, dt_bias128)


def _conv_bwd(proj, dt_raw, conv_w, conv_b, dt_bias128, d_xbc, d_dt128, d_proj):
    lp = proj.shape[0]
    nblk = XBC_W // 128
    c0 = COL_XBC // 128
    c0_in = (COL_XBC - QKV_W) // 128
    n = lp - 8
    last = nblk - 1

    def body(x_ref, w_ref, b_ref, dtr_ref, dtb_ref, dy_ref, ddt_ref, dproj_in,
             dx_ref, gw_ref, gb_ref, gdtb_ref, scr):
        j = pl.program_id(0)

        @pl.when(j < nblk)
        def _():
            pre, live = _conv_pre(x_ref, w_ref, b_ref, lp)
            sg = _sigmoid(pre)
            dpre = jnp.where(live, dy_ref[pl.ds(8, n), :] * (sg * (1.0 + pre * (1.0 - sg))), 0.0)
            gb_ref[...] = jnp.sum(dpre, axis=0, keepdims=True)
            gw_ref[...] = jnp.concatenate(
                [jnp.sum(dpre * x_ref[pl.ds(5 + k, n), :], axis=0, keepdims=True) for k in range(4)], axis=0)
            scr[pl.ds(0, 8), :] = jnp.zeros((8, 128), F32)
            scr[pl.ds(8, n), :] = dpre
            scr[pl.ds(lp, 8), :] = jnp.zeros((8, 128), F32)
            w = w_ref[...]
            dx_ref[...] = (scr[pl.ds(0, lp), :] * w[3:4, :] + scr[pl.ds(1, lp), :] * w[2:3, :]
                           + scr[pl.ds(2, lp), :] * w[1:2, :] + scr[pl.ds(3, lp), :] * w[0:1, :]).astype(dx_ref.dtype)

        @pl.when(j == nblk)
        def _():
            s = dtr_ref[...] + dtb_ref[...]
            d = jnp.where(_iota((lp, 128), 0) >= PAD, ddt_ref[...] * _sigmoid(s), 0.0)
            dx_ref[...] = d.astype(dx_ref.dtype)
            gdtb_ref[...] = jnp.sum(d, axis=0, keepdims=True)

    clamp = lambda j: (0, jnp.minimum(j, last))
    full128 = pl.BlockSpec((lp, 128), lambda j: (0, 0))
    return pl.pallas_call(
        body, name="conv_bwd", grid=(nblk + 1,),
        in_specs=[pl.BlockSpec((lp, 128), lambda j: (0, c0_in + jnp.minimum(j, last))),
                  pl.BlockSpec((4, 128), clamp),
                  pl.BlockSpec((1, 128), clamp),
                  full128, pl.BlockSpec((1, 128), lambda j: (0, 0)),
                  pl.BlockSpec((lp, 128), clamp), full128, pl.BlockSpec(memory_space=pl.ANY)],
        out_specs=[pl.BlockSpec((lp, 128), lambda j: (0, c0 + j)), pl.BlockSpec((4, 128), clamp),
                   pl.BlockSpec((1, 128), clamp), pl.BlockSpec((1, 128), lambda j: (0, 0))],
        out_shape=[_SDS(d_proj.shape, d_proj.dtype), _SDS((4, XBC_W), F32), _SDS((1, XBC_W), F32), _SDS((1, 128), F32)],
        input_output_aliases={7: 0},
        scratch_shapes=[pltpu.VMEM((lp + 8, 128), F32)],
        compiler_params=_params(("arbitrary",)),
    )(proj, conv_w, conv_b, dt_raw, dt_bias128, d_xbc, d_dt128, d_proj)


def _ssd_pieces(dt, dt_t, a, a_t):
    r64 = _iota((CHUNK, CHUNK), 0)
    c64 = _iota((CHUNK, CHUNK), 1)
    tril = c64 <= r64
    tril01 = tril.astype(_MXU)
    triu01 = (r64 <= c64).astype(_MXU)
    expand = (lax.shift_right_logical(_iota((N_HEADS, SSD_W), 1), 6) == _iota((N_HEADS, SSD_W), 0)).astype(_MXU)
    acum = _sel_left(tril01, dt * a)
    acum_t = _sel_right(dt_t * a_t, triu01)
    ax = _sel_right(acum, expand)
    dtx = _sel_right(dt, expand)
    return tril, expand, acum, acum_t, ax, dtx


def _seg_matrix():
    return (lax.shift_right_logical(_iota((SSD_W, N_HEADS), 0), 6) == _iota((SSD_W, N_HEADS), 1)).astype(_MXU)


def _head_decay(ax, acum_t, h, tril):
    col = ax[:, h * HEAD:(h + 1) * HEAD]
    rowv = acum_t[h:h + 1, :]
    return jnp.where(tril, jnp.exp(jnp.minimum(col - rowv, 0.0)), 0.0)


def _ssd_fwd(xbc, dt_c, dt_tc, a, a_t, dskip_x):
    lp = xbc.shape[0]
    nc = lp // CHUNK
    gw = SSD_W // N_GROUPS
    hpg = N_HEADS // N_GROUPS

    def body(x_ref, dt_ref, dtt_ref, a_ref, at_ref, d_ref, y_ref, st_ref, state):
        c = pl.program_id(0)

        @pl.when(c == 0)
        def _():
            state[...] = jnp.zeros_like(state)

        st_ref[0] = state[...]
        tril, _, _, acum_t, ax, dtx = _ssd_pieces(dt_ref[0], dtt_ref[0], a_ref[...], at_ref[...])
        x = x_ref[:, 0:SSD_W]
        xdt = x * dtx
        ea = jnp.exp(ax)
        aex = ax[CHUNK - 1:CHUNK, :]
        wd = jnp.exp(aex - ax)
        eae = jnp.exp(aex)
        xw = xdt * wd
        y_ref[...] = x * d_ref[...]
        for g in range(N_GROUPS):
            gs = slice(g * gw, (g + 1) * gw)
            rs = slice(g * N_STATE, (g + 1) * N_STATE)
            bg = x_ref[:, SSD_W + g * N_STATE:SSD_W + (g + 1) * N_STATE]
            cg = x_ref[:, SSD_W + N_GROUPS * N_STATE + g * N_STATE:SSD_W + N_GROUPS * N_STATE + (g + 1) * N_STATE]
            sg = state[rs, :]
            cb = _mm_nt(cg, bg)
            y_ref[:, gs] += _mm(cg, sg) * ea[:, gs]
            for r in range(hpg):
                h = g * hpg + r
                hs = slice(h * HEAD, (h + 1) * HEAD)
                m = cb * _head_decay(ax, acum_t, h, tril)
                y_ref[:, hs] += _mm(m, xdt[:, hs])
            state[rs, :] = sg * eae[:, gs] + _mm_tn(bg, xw[:, gs])

    return pl.pallas_call(
        body, name="ssd_fwd", grid=(nc,),
        in_specs=[pl.BlockSpec((CHUNK, XBC_W), lambda c: (c, 0)),
                  pl.BlockSpec((1, CHUNK, N_HEADS), lambda c: (c, 0, 0)),
                  pl.BlockSpec((1, N_HEADS, CHUNK), lambda c: (c, 0, 0)),
                  pl.BlockSpec((1, N_HEADS), lambda c: (0, 0)),
                  pl.BlockSpec((N_HEADS, 1), lambda c: (0, 0)),
                  pl.BlockSpec((1, SSD_W), lambda c: (0, 0))],
        out_specs=[pl.BlockSpec((CHUNK, SSD_W), lambda c: (c, 0)),
                   pl.BlockSpec((1, N_GROUPS * N_STATE, gw), lambda c: (c, 0, 0))],
        out_shape=[_SDS((lp, SSD_W), F32), _SDS((nc, N_GROUPS * N_STATE, gw), F32)],
        scratch_shapes=[pltpu.VMEM((N_GROUPS * N_STATE, gw), F32)],
        compiler_params=_params(("arbitrary",)),
    )(xbc, dt_c, dt_tc, a, a_t, dskip_x)


def _ssd_bwd(xbc, dt_c, dt_tc, a, a_t, dskip_x, states, d_y):
    lp = xbc.shape[0]
    nc = lp // CHUNK
    gw = SSD_W // N_GROUPS
    hpg = N_HEADS // N_GROUPS

    def body(x_ref, dt_ref, dtt_ref, a_ref, at_ref, d_ref, st_ref, dy_ref,
             dx_ref, ddta_ref, ddtb_ref, ga1_ref, ga2_ref, gd_ref, dstate, dxdt_scr, z_scr, yoff_scr, sds_scr):
        c = pl.program_id(0)

        @pl.when(c == 0)
        def _():
            dstate[...] = jnp.zeros_like(dstate)
            ga1_ref[...] = jnp.zeros_like(ga1_ref)
            ga2_ref[...] = jnp.zeros_like(ga2_ref)
            gd_ref[...] = jnp.zeros_like(gd_ref)

        dt = dt_ref[0]
        dt_t = dtt_ref[0]
        a = a_ref[...]
        a_t = at_ref[...]
        tril, _, acum, acum_t, ax, dtx = _ssd_pieces(dt, dt_t, a, a_t)
        seg = _seg_matrix()
        x = x_ref[:, 0:SSD_W]
        dy = dy_ref[...]
        xdt = x * dtx
        ea = jnp.exp(ax)
        aex = ax[CHUNK - 1:CHUNK, :]
        wd = jnp.exp(aex - ax)
        eae = jnp.exp(aex)
        xw = xdt * wd
        edy = ea * dy
        lane16 = _iota((CHUNK, N_HEADS), 1)
        row16 = _iota((N_HEADS, CHUNK), 0)
        da_col = jnp.zeros((CHUNK, N_HEADS), F32)
        da_row = jnp.zeros((N_HEADS, CHUNK), F32)
        for g in range(N_GROUPS):
            gs = slice(g * gw, (g + 1) * gw)
            rs = slice(g * N_STATE, (g + 1) * N_STATE)
            bcol = slice(SSD_W + g * N_STATE, SSD_W + (g + 1) * N_STATE)
            ccol = slice(SSD_W + N_GROUPS * N_STATE + g * N_STATE, SSD_W + N_GROUPS * N_STATE + (g + 1) * N_STATE)
            bg = x_ref[:, bcol]
            cg = x_ref[:, ccol]
            sg = st_ref[0, rs, :]
            dsn = dstate[rs, :]
            cb = _mm_nt(cg, bg)
            z_scr[:, gs] = _mm(bg, dsn)
            yoff_scr[:, gs] = _mm(cg, sg) * ea[:, gs]
            sds_scr[:, gs] = jnp.broadcast_to(jnp.sum(dsn * sg, axis=0, keepdims=True), (8, gw))
            dcb = jnp.zeros((CHUNK, CHUNK), F32)
            for r in range(hpg):
                h = g * hpg + r
                hs = slice(h * HEAD, (h + 1) * HEAD)
                dec = _head_decay(ax, acum_t, h, tril)
                m = cb * dec
                t1 = _mm_nt(dy[:, hs], xdt[:, hs])
                dcb = dcb + dec * t1
                tm = m * t1
                da_col = da_col + jnp.where(lane16 == h, jnp.sum(tm, axis=1, keepdims=True), 0.0)
                da_row = da_row - jnp.where(row16 == h, jnp.sum(tm, axis=0, keepdims=True), 0.0)
                dxdt_scr[:, hs] = _mm_tn(m, dy[:, hs])
            dx_ref[:, ccol] = _mm(dcb, bg) + _mm_nt(edy[:, gs], sg)
            dx_ref[:, bcol] = _mm_tn(dcb, cg) + _mm_nt(xw[:, gs], dsn)
            dstate[rs, :] = eae[:, gs] * dsn + _mm_tn(cg, edy[:, gs])
        zf = z_scr[...]
        dxdt = dxdt_scr[...] + wd * zf
        t3 = _sel_right(xw * zf, seg)
        da_col = da_col + _sel_right(dy * yoff_scr[...], seg) - t3
        aend = acum[CHUNK - 1:CHUNK, :]
        sd = _sel_right(sds_scr[...], seg)[0:1, :] * jnp.exp(aend)
        last = jnp.sum(t3, axis=0, keepdims=True) + sd
        da_col = da_col + jnp.where(_iota((CHUNK, N_HEADS), 0) == CHUNK - 1, last, 0.0)
        r64 = _iota((CHUNK, CHUNK), 0)
        c64 = _iota((CHUNK, CHUNK), 1)
        ddta1 = _sel_left((c64 >= r64).astype(_MXU), da_col)
        ddta2 = _sel_right(da_row, (r64 >= c64).astype(_MXU))
        ddta_ref[0] = a * ddta1 + _sel_right(dxdt * x, seg)
        ddtb_ref[0] = a_t * ddta2
        ga1_ref[...] += jnp.sum(dt * ddta1, axis=0, keepdims=True)
        ga2_ref[...] += jnp.sum(dt_t * ddta2, axis=1, keepdims=True)
        dx_ref[:, 0:SSD_W] = dxdt * dtx + d_ref[...] * dy
        gd_ref[...] += jnp.sum(dy * x, axis=0, keepdims=True)

    rev = lambda c: (nc - 1 - c, 0)
    rev3 = lambda c: (nc - 1 - c, 0, 0)
    return pl.pallas_call(
        body, name="ssd_bwd", grid=(nc,),
        in_specs=[pl.BlockSpec((CHUNK, XBC_W), rev),
                  pl.BlockSpec((1, CHUNK, N_HEADS), rev3),
                  pl.BlockSpec((1, N_HEADS, CHUNK), rev3),
                  pl.BlockSpec((1, N_HEADS), lambda c: (0, 0)),
                  pl.BlockSpec((N_HEADS, 1), lambda c: (0, 0)),
                  pl.BlockSpec((1, SSD_W), lambda c: (0, 0)),
                  pl.BlockSpec((1, N_GROUPS * N_STATE, gw), rev3),
                  pl.BlockSpec((CHUNK, SSD_W), rev)],
        out_specs=[pl.BlockSpec((CHUNK, XBC_W), rev),
                   pl.BlockSpec((1, CHUNK, N_HEADS), rev3),
                   pl.BlockSpec((1, N_HEADS, CHUNK), rev3),
                   pl.BlockSpec((1, N_HEADS), lambda c: (0, 0)),
                   pl.BlockSpec((N_HEADS, 1), lambda c: (0, 0)),
                   pl.BlockSpec((1, SSD_W), lambda c: (0, 0))],
        out_shape=[_SDS((lp, XBC_W), F32), _SDS((nc, CHUNK, N_HEADS), F32), _SDS((nc, N_HEADS, CHUNK), F32),
                   _SDS((1, N_HEADS), F32), _SDS((N_HEADS, 1), F32), _SDS((1, SSD_W), F32)],
        scratch_shapes=[pltpu.VMEM((N_GROUPS * N_STATE, gw), F32), pltpu.VMEM((CHUNK, SSD_W), F32),
                        pltpu.VMEM((CHUNK, SSD_W), F32), pltpu.VMEM((CHUNK, SSD_W), F32),
                        pltpu.VMEM((8, SSD_W), F32)],
        compiler_params=_params(("arbitrary",)),
    )(xbc, dt_c, dt_tc, a, a_t, dskip_x, states, d_y)


def _gated_norm(o, gate, w):
    sg = _sigmoid(gate)
    p = o * (gate * sg)
    rs = lax.rsqrt(jnp.mean(p * p, axis=-1, keepdims=True) + EPS)
    n = p * rs
    return sg, rs, n, n * w


def _tail_fwd(o_sb, o_ssd, proj, h0, target, w_out, sb_w, ssd_w, fin_w):
    lp = o_sb.shape[0]
    nb = lp // TM
    row = lambda i: (i, 0)
    one = lambda i: (0, 0)

    def body(osb_ref, gate_ref, ossd_ref, z_ref, h0_ref, tgt_ref, wo_ref, sbw_ref, ssdw_ref, fw_ref,
             dh1_ref, loss_ref, gfw_ref):
        i = pl.program_id(0)

        @pl.when(i == 0)
        def _():
            loss_ref[...] = jnp.zeros_like(loss_ref)
            gfw_ref[...] = jnp.zeros_like(gfw_ref)

        y1 = _gated_norm(osb_ref[...], gate_ref[...], sbw_ref[...])[3]
        y2 = _gated_norm(ossd_ref[...], z_ref[...], ssdw_ref[...])[3]
        h1 = (h0_ref[...] + _mm(y1, wo_ref[0:SB_W, :])) + _mm(y2, wo_ref[SB_W:SB_W + SSD_W, :])
        rs1 = lax.rsqrt(jnp.mean(h1 * h1, axis=-1, keepdims=True) + EPS)
        n1 = h1 * rs1
        fw = fw_ref[...]
        diff = jnp.where(i > 0, n1 * fw - tgt_ref[...], 0.0)
        loss_ref[...] += jnp.sum(diff * diff, axis=0, keepdims=True)
        d_out = diff * (1.0 / D_MODEL)
        gfw_ref[...] += jnp.sum(d_out * n1, axis=0, keepdims=True)
        g = d_out * fw
        dh1_ref[...] = rs1 * (g - n1 * jnp.mean(g * n1, axis=-1, keepdims=True))

    return pl.pallas_call(
        body, name="tail_fwd", grid=(nb,),
        in_specs=[pl.BlockSpec((TM, SB_W), row),
                  pl.BlockSpec((TM, SB_W), lambda i: (i, (COL_GATE - QKV_W) // SB_W)),
                  pl.BlockSpec((TM, SSD_W), row),
                  pl.BlockSpec((TM, SSD_W), lambda i: (i, (COL_Z - QKV_W) // SSD_W)),
                  pl.BlockSpec((TM, D_MODEL), row),
                  pl.BlockSpec((TM, D_MODEL), lambda i: (jnp.maximum(i - 1, 0), 0)),
                  pl.BlockSpec(memory_space=_VMEM),
                  pl.BlockSpec((1, SB_W), one), pl.BlockSpec((1, SSD_W), one), pl.BlockSpec((1, D_MODEL), one)],
        out_specs=[pl.BlockSpec((TM, D_MODEL), row), pl.BlockSpec((1, D_MODEL), one), pl.BlockSpec((1, D_MODEL), one)],
        out_shape=[_SDS((lp, D_MODEL), F32), _SDS((1, D_MODEL), F32), _SDS((1, D_MODEL), F32)],
        compiler_params=_params(("arbitrary",), 40),
    )(o_sb, proj, o_ssd, proj, h0, target, w_out, sb_w, ssd_w, fin_w)


def _gated_norm_bwd(o, gate, w, dy):
    sg, rs, n, _ = _gated_norm(o, gate, w)
    gw = jnp.sum(dy * n, axis=0, keepdims=True)
    dn = dy * w
    dp = rs * (dn - n * jnp.mean(dn * n, axis=-1, keepdims=True))
    d_o = dp * (gate * sg)
    d_gate = dp * o * (sg * (1.0 + gate * (1.0 - sg)))
    return d_o, d_gate, gw, n * w


def _tail_bwd(o_sb, o_ssd, proj, d_h1, w_out, sb_w, ssd_w):
    lp = o_sb.shape[0]
    tm = 272 if lp % 272 == 0 else TM
    nb = lp // tm
    row = lambda i, t: (i, 0)
    one = lambda i, t: (0, 0)

    def body(osb_ref, gate_ref, ossd_ref, z_ref, dh1_ref, wo_ref, sbw_ref, ssdw_ref,
             dosb_ref, dossd_ref, dproj_ref, gwo_ref, gsb_ref, gssd_ref):
        i = pl.program_id(0)
        t = pl.program_id(1)

        @pl.when(jnp.logical_and(i == 0, t == 0))
        def _():
            gwo_ref[...] = jnp.zeros_like(gwo_ref)
            gsb_ref[...] = jnp.zeros_like(gsb_ref)
            gssd_ref[...] = jnp.zeros_like(gssd_ref)

        dh1 = dh1_ref[...].astype(_MXU)

        def half(o_ref, g_ref, w_ref, do_ref, gn_ref, r0):
            dy = lax.dot_general(dh1, wo_ref[r0:r0 + SB_W, :], _NT, preferred_element_type=F32)
            d_o, d_g, gw, y = _gated_norm_bwd(o_ref[...], g_ref[...], w_ref[...], dy)
            do_ref[...] = d_o
            dproj_ref[...] = d_g.astype(_MXU)
            gn_ref[...] += gw
            gwo_ref[r0:r0 + SB_W, :] += lax.dot_general(y.astype(_MXU), dh1, _TN, preferred_element_type=F32)

        @pl.when(t == 0)
        def _():
            half(osb_ref, gate_ref, sbw_ref, dosb_ref, gsb_ref, 0)

        @pl.when(t == 1)
        def _():
            half(ossd_ref, z_ref, ssdw_ref, dossd_ref, gssd_ref, SB_W)

    tile = pl.BlockSpec((tm, SB_W), row)
    return pl.pallas_call(
        body, name="tail_bwd", grid=(nb, 2),
        in_specs=[tile, pl.BlockSpec((tm, SB_W), lambda i, t: (i, (COL_GATE - QKV_W) // SB_W)),
                  tile, pl.BlockSpec((tm, SSD_W), lambda i, t: (i, (COL_Z - QKV_W) // SSD_W)),
                  tile, pl.BlockSpec(memory_space=_VMEM),
                  pl.BlockSpec((1, SB_W), one), pl.BlockSpec((1, SSD_W), one)],
        out_specs=[tile, tile, pl.BlockSpec((tm, SB_W), lambda i, t: (i, COL_GATE // SB_W + t)),
                   pl.BlockSpec((SB_W + SSD_W, D_MODEL), one), pl.BlockSpec((1, SB_W), one), pl.BlockSpec((1, SSD_W), one)],
        out_shape=[_SDS((lp, SB_W), F32), _SDS((lp, SSD_W), F32), _SDS((lp, W_ALL), _MXU),
                   _SDS((SB_W + SSD_W, D_MODEL), F32), _SDS((1, SB_W), F32), _SDS((1, SSD_W), F32)],
        compiler_params=_params(("arbitrary", "arbitrary"), 48),
    )(o_sb, proj, o_ssd, proj, d_h1, w_out, sb_w, ssd_w)


def _d_u(d_proj, w_t, send=(), dests=()):
    lp = d_proj.shape[0]
    tk = 512
    steps = N_MAIN // tk
    n = len(send)
    host_in, host_out, host_shapes, host_sems = _host_specs(send)

    def body(dp_ref, w_ref, dpdt_ref, wdt_ref, *rest):
        srcs, o_ref, dsts, sems = rest[:n], rest[n], rest[n + 1:2 * n + 1], rest[2 * n + 1:]
        j = pl.program_id(0)

        @pl.when(j == 0)
        def _():
            if n:
                _start_copies(_slab_copies(srcs, dsts, dests, *sems))
            o_ref[...] = jnp.dot(dpdt_ref[...], wdt_ref[...], preferred_element_type=F32)

        o_ref[...] += jnp.dot(dp_ref[...], w_ref[...], preferred_element_type=F32)

        if n:
            @pl.when(j == steps - 1)
            def _():
                _finish_copies(_slab_copies(srcs, dsts, dests, *sems))

    res = pl.pallas_call(
        body, name="d_u", grid=(steps,),
        in_specs=[pl.BlockSpec((lp, tk), lambda j: (0, j)),
                  pl.BlockSpec((tk, D_MODEL), lambda j: (j, 0)),
                  pl.BlockSpec((lp, 128), lambda j: (0, N_MAIN // 128)),
                  pl.BlockSpec((128, D_MODEL), lambda j: (N_MAIN // 128, 0))] + host_in,
        out_specs=[pl.BlockSpec((lp, D_MODEL), lambda j: (0, 0))] + host_out,
        out_shape=[_SDS((lp, D_MODEL), F32)] + host_shapes,
        scratch_shapes=host_sems,
        compiler_params=_params(("arbitrary",), 48),
    )(d_proj, w_t, d_proj, w_t, *send)
    return res[0], list(res[1:])


def _norm_bwd(du_all, h0, d_h1, norm_w):
    lp = h0.shape[0]
    nb = lp // TM
    seq = lp - OFF
    row = lambda i: (i, 0)
    one = lambda i: (0, 0)

    def body(du_ref, h0_ref, dh1_ref, nw_ref, gx_ref, gmeta_ref, gnw_ref):
        i = pl.program_id(0)

        @pl.when(i == 0)
        def _():
            gnw_ref[...] = jnp.zeros_like(gnw_ref)

        du = du_ref[...]
        h = h0_ref[...]
        rs = lax.rsqrt(jnp.mean(h * h, axis=-1, keepdims=True) + EPS)
        n0 = h * rs
        gnw_ref[...] += jnp.sum(du * n0, axis=0, keepdims=True)
        g = du * nw_ref[...]
        dh0 = dh1_ref[...] + rs * (g - n0 * jnp.mean(g * n0, axis=-1, keepdims=True))

        @pl.when(i == 0)
        def _():
            gmeta_ref[...] = dh0[PAD:PAD + N_META, :]

        @pl.when(i > 0)
        def _():
            gx_ref[...] = dh0

    tile = pl.BlockSpec((TM, D_MODEL), row)
    return pl.pallas_call(
        body, name="norm_bwd", grid=(nb,),
        in_specs=[tile, tile, tile, pl.BlockSpec((1, D_MODEL), one)],
        out_specs=[pl.BlockSpec((TM, D_MODEL), lambda i: (jnp.maximum(i - 1, 0), 0)),
                   pl.BlockSpec((N_META, D_MODEL), one), pl.BlockSpec((1, D_MODEL), one)],
        out_shape=[_SDS((seq, D_MODEL), F32), _SDS((N_META, D_MODEL), F32), _SDS((1, D_MODEL), F32)],
        compiler_params=_params(("arbitrary",)),
    )(du_all, h0, d_h1, norm_w)


def _grad_w_windows(u_t, d_proj, first, count, name):
    lp = d_proj.shape[0]
    hw = WIN_W // 2
    steps = 2 * count

    def body(ut_ref, dp_hbm, o_ref, buf, sems):
        s = pl.program_id(0)
        slot = s % 2

        def fetch(step, sl):
            start = pl.multiple_of((first + step // 2) * WIN_STEP + (step % 2) * hw, 128)
            return pltpu.make_async_copy(dp_hbm.at[:, pl.ds(start, hw)], buf.at[sl], sems.at[sl])

        @pl.when(s == 0)
        def _():
            fetch(0, 0).start()

        @pl.when(s + 1 < steps)
        def _():
            fetch(s + 1, 1 - slot).start()

        fetch(s, slot).wait()
        o_ref[0] = jnp.dot(ut_ref[...], buf[slot], preferred_element_type=F32).astype(o_ref.dtype)

    return pl.pallas_call(
        body, name=name, grid=(steps,),
        in_specs=[pl.BlockSpec((D_MODEL, lp), lambda s: (0, 0)), pl.BlockSpec(memory_space=pl.ANY)],
        out_specs=pl.BlockSpec((1, D_MODEL, hw), lambda s: (s // 2, 0, s % 2)),
        out_shape=_SDS((count, D_MODEL, WIN_W), _MXU),
        scratch_shapes=[pltpu.VMEM((2, lp, hw), _MXU), pltpu.SemaphoreType.DMA((2,))],
        compiler_params=_params(("arbitrary",), 40),
    )(u_t, d_proj)


def _device_grads(x2d, target2d, meta_full, norm_w, w_t, conv_w, conv_b, dt_bias, a_log, d_skip,
                  sb_w, ssd_w, w_out, fin_w, exchange=None):
    lp = x2d.shape[0] + OFF
    nc = lp // CHUNK
    h0, u, u_t = _prep(x2d, meta_full, norm_w)
    qkv, proj, dt_raw = _inproj(u, w_t)
    o_sb, o_lo = _sb_fwd(qkv)
    dt_bias128 = jnp.pad(dt_bias, ((0, 0), (0, 128 - N_HEADS)))
    xbc, dt128 = _conv_fwd(proj, dt_raw, conv_w, conv_b, dt_bias128)
    dt_c = dt128[:, :N_HEADS].reshape(nc, CHUNK, N_HEADS)
    dt_tc = jnp.swapaxes(dt_c, 1, 2)
    a = -jnp.exp(a_log)
    a_t = a.reshape(N_HEADS, 1)
    dskip_x = jnp.repeat(d_skip, HEAD, axis=1)
    o_ssd, states = _ssd_fwd(xbc, dt_c, dt_tc, a, a_t, dskip_x)
    d_h1, sq_err, g_fin = _tail_fwd(o_sb, o_ssd, proj, h0, target2d, w_out, sb_w, ssd_w, fin_w)

    d_osb, d_ossd, d_proj, g_wout, g_sb, g_ssd = _tail_bwd(o_sb, o_ssd, proj, d_h1, w_out, sb_w, ssd_w)
    d_xbc_act, ddt_a, ddt_b, ga1, ga2, gd = _ssd_bwd(xbc, dt_c, dt_tc, a, a_t, dskip_x, states, d_ossd)
    d_dt = (ddt_a + jnp.swapaxes(ddt_b, 1, 2)).reshape(lp, N_HEADS)
    d_dt128 = jnp.pad(d_dt, ((0, 0), (0, 128 - N_HEADS)))
    d_proj, g_convw, g_convb, g_dtb128 = _conv_bwd(proj, dt_raw, conv_w, conv_b, dt_bias128, d_xbc_act, d_dt128, d_proj)
    send_e, dests_e = ((), ()) if exchange is None else exchange["early"](g_wout)
    d_proj, arrived_e = _sb_bwd(qkv, o_sb, o_lo, d_osb, d_proj, send_e, dests_e)
    g_win = _grad_w_windows(u_t, d_proj, 0, N_CHIPS, "grad_w_in")
    send_l, dests_l = ((), ()) if exchange is None else exchange["late"](g_win)
    d_u, arrived_l = _d_u(d_proj, w_t, send_l, dests_l)
    send, arrived = tuple(send_e) + tuple(send_l), tuple(arrived_e) + tuple(arrived_l)
    g_x, g_meta, g_nw = _norm_bwd(d_u, h0, d_h1, norm_w)
    g_alog = (ga1 + ga2.reshape(1, N_HEADS)) * a
    g_dskip = gd.reshape(N_HEADS, HEAD).sum(axis=1).reshape(1, N_HEADS)
    grads = dict(meta_tokens=g_meta, norm_w=g_nw, w_in=g_win, conv_w=g_convw, conv_b=g_convb,
                 dt_bias=g_dtb128[:, :N_HEADS], a_log=g_alog, d_skip=g_dskip, sb_norm_w=g_sb, ssd_norm_w=g_ssd,
                 w_out=g_wout, final_norm_w=g_fin, sent=send, arrived=arrived)
    return sq_err, g_x, grads


_MESH = pl.DeviceIdType.MESH
_ANY = pl.BlockSpec(memory_space=pl.ANY)


def _place():
    return lax.axis_index("x"), lax.axis_index("y"), lax.axis_index("c")


def _other_chips(x, y):
    return ((1 - x, y), (x, 1 - y), (1 - x, 1 - y))


def _gather_shards(arrays, n_big):
    n = len(arrays)

    def body(*refs):
        srcs, dsts = refs[:n], refs[n:2 * n]
        send_sems, recv_sems, fwd_send, fwd_recv = refs[2 * n:]
        x, y, c = _place()
        mine = 2 * x + y
        chips = _other_chips(x, y)

        def window(a):
            half = arrays[a].shape[1] // 2
            return pl.ds(pl.multiple_of(c * half, 128), half)

        first = []
        for a in range(n):
            for k, (px, py) in enumerate(chips):
                if a < n_big:
                    src, dst = srcs[a].at[:, window(a)], dsts[a].at[mine, :, window(a)]
                else:
                    src, dst = srcs[a], dsts[a].at[mine]
                cp = pltpu.make_async_remote_copy(
                    src_ref=src, dst_ref=dst, send_sem=send_sems.at[a * 3 + k], recv_sem=recv_sems.at[a * 3 + k],
                    device_id=(px, py, c), device_id_type=_MESH)
                cp.start()
                first.append(cp)
        passed = []
        for a in range(n):
            for k, (px, py) in enumerate(chips):
                first[a * 3 + k].wait_recv()
                if a < n_big:
                    landed = dsts[a].at[2 * px + py, :, window(a)]
                    cp = pltpu.make_async_remote_copy(
                        src_ref=landed, dst_ref=landed, send_sem=fwd_send.at[a * 3 + k], recv_sem=fwd_recv.at[a * 3 + k],
                        device_id=(x, y, 1 - c), device_id_type=_MESH)
                    cp.start()
                    passed.append(cp)
        for cp in passed:
            cp.wait_recv()
        for cp in first + passed:
            cp.wait_send()

    got = pl.pallas_call(
        body, name="gather_shards",
        in_specs=[_ANY] * n, out_specs=[_ANY] * n,
        out_shape=[_SDS((N_CHIPS,) + a.shape, a.dtype) for a in arrays],
        scratch_shapes=[pltpu.SemaphoreType.DMA((3 * n,)), pltpu.SemaphoreType.DMA((3 * n,)),
                        pltpu.SemaphoreType.DMA((3 * n_big,)), pltpu.SemaphoreType.DMA((3 * n_big,))],
    )(*arrays)
    mine = 2 * lax.axis_index("x") + lax.axis_index("y")
    return [lax.dynamic_update_slice(g, a[None], (mine,) + (0,) * a.ndim) for g, a in zip(got, arrays)]


def _slab_copies(srcs, dsts, dests, send_sems, recv_sems):
    x, y, c = _place()
    mine = 2 * x + y
    copies = []
    for a in range(len(srcs)):
        lo, hi = dests[a]
        receives = jnp.logical_and(mine >= lo, mine < hi)
        for k, (px, py) in enumerate(_other_chips(x, y)):
            target = 2 * px + py
            cp = pltpu.make_async_remote_copy(
                src_ref=srcs[a].at[jnp.clip(target - lo, 0, hi - lo - 1)], dst_ref=dsts[a].at[mine],
                send_sem=send_sems.at[a * 3 + k], recv_sem=recv_sems.at[a * 3 + k],
                device_id=(px, py, c), device_id_type=_MESH)
            copies.append((cp, jnp.logical_and(target >= lo, target < hi), receives))
    return copies


def _start_copies(copies):
    for cp, sends, _ in copies:
        pl.when(sends)(cp.start)


def _finish_copies(copies):
    for cp, _, receives in copies:
        pl.when(receives)(cp.wait_recv)
    for cp, sends, _ in copies:
        pl.when(sends)(cp.wait_send)


def _host_specs(send):
    n = len(send)
    hbm = [pl.BlockSpec(memory_space=pl.ANY)] * n
    shapes = [_SDS((N_CHIPS,) + a.shape[1:], a.dtype) for a in send]
    sems = [pltpu.SemaphoreType.DMA((3 * n,)), pltpu.SemaphoreType.DMA((3 * n,))] if n else []
    return hbm, hbm, shapes, sems


def _swap_halves(arrays, name):
    n = len(arrays)

    def body(*refs):
        srcs, dsts = refs[:n], refs[n:2 * n]
        send_sems, recv_sems = refs[2 * n:]
        x, y, c = _place()
        copies = []
        for a in range(n):
            half = arrays[a].shape[1] // 2
            cp = pltpu.make_async_remote_copy(
                src_ref=srcs[a].at[:, pl.ds(pl.multiple_of((1 - c) * half, 16), half)], dst_ref=dsts[a],
                send_sem=send_sems.at[a], recv_sem=recv_sems.at[a],
                device_id=(x, y, 1 - c), device_id_type=_MESH)
            cp.start()
            copies.append(cp)
        for cp in copies:
            cp.wait_recv()
        for cp in copies:
            cp.wait_send()

    return pl.pallas_call(
        body, name=name,
        in_specs=[_ANY] * n, out_specs=[_ANY] * n,
        out_shape=[_SDS((a.shape[0], a.shape[1] // 2, a.shape[2]), a.dtype) for a in arrays],
        scratch_shapes=[pltpu.SemaphoreType.DMA((n,)), pltpu.SemaphoreType.DMA((n,))],
    )(*arrays)


def _join_halves(arrays, by_cols):
    n = len(arrays)

    def body(*refs):
        dsts = refs[n:2 * n]
        send_sems, recv_sems = refs[2 * n:]
        x, y, c = _place()
        copies = []
        for a in range(n):
            if by_cols[a]:
                half = arrays[a].shape[1] // 2
                mine = dsts[a].at[:, pl.ds(pl.multiple_of(c * half, 128), half)]
            else:
                half = arrays[a].shape[0] // 2
                mine = dsts[a].at[pl.ds(pl.multiple_of(c * half, 16), half)]
            cp = pltpu.make_async_remote_copy(
                src_ref=mine, dst_ref=mine, send_sem=send_sems.at[a], recv_sem=recv_sems.at[a],
                device_id=(x, y, 1 - c), device_id_type=_MESH)
            cp.start()
            copies.append(cp)
        for cp in copies:
            cp.wait_recv()
        for cp in copies:
            cp.wait_send()

    return pl.pallas_call(
        body, name="join_halves",
        in_specs=[_ANY] * n, out_specs=[_ANY] * n,
        out_shape=[_SDS(a.shape, a.dtype) for a in arrays],
        input_output_aliases={a: a for a in range(n)},
        scratch_shapes=[pltpu.SemaphoreType.DMA((n,)), pltpu.SemaphoreType.DMA((n,))],
    )(*arrays)


N_DEV = 8
SMALL_ROWS = 32
SMALL_COLS = XBC_W


def _gather_small(packed):
    def body(src_ref, dst_ref, send_sems, recv_sems, local_sem):
        x, y, c = _place()
        me = 4 * x + 2 * y + c
        own = pltpu.make_async_copy(src_ref, dst_ref.at[me], local_sem)
        own.start()
        copies = []
        for k in range(1, N_DEV):
            bx, by, bc = (k >> 2) & 1, (k >> 1) & 1, k & 1
            peer = (x + bx - 2 * x * bx, y + by - 2 * y * by, c + bc - 2 * c * bc)
            cp = pltpu.make_async_remote_copy(
                src_ref=src_ref, dst_ref=dst_ref.at[me], send_sem=send_sems.at[k - 1], recv_sem=recv_sems.at[k - 1],
                device_id=peer, device_id_type=_MESH)
            cp.start()
            copies.append(cp)
        for cp in copies:
            cp.wait_recv()
        for cp in copies:
            cp.wait_send()
        own.wait()

    return pl.pallas_call(
        body, name="gather_small",
        in_specs=[pl.BlockSpec(memory_space=_VMEM)], out_specs=pl.BlockSpec(memory_space=_VMEM),
        out_shape=_SDS((N_DEV, SMALL_ROWS, SMALL_COLS), F32),
        scratch_shapes=[pltpu.SemaphoreType.DMA((N_DEV - 1,)), pltpu.SemaphoreType.DMA((N_DEV - 1,)),
                        pltpu.SemaphoreType.DMA],
    )(packed)


def _adamw(w, g, m, v):
    m = ADAM_B1 * m + (1.0 - ADAM_B1) * g
    v = ADAM_B2 * v + (1.0 - ADAM_B2) * (g * g)
    m_hat = m / (1.0 - ADAM_B1 ** ADAM_STEP)
    v_hat = v / (1.0 - ADAM_B2 ** ADAM_STEP)
    delta = -ADAM_LR * (m_hat / (jnp.sqrt(v_hat) + ADAM_EPS) + ADAM_WD * w)
    return delta, m, v


def _sum_slabs(slabs, core, name, transposed=False):
    _, h, c = slabs.shape
    tr = 128
    nblk = h // tr

    def body(core_ref, s_ref, o_ref):
        tot = ((s_ref[0].astype(F32) + s_ref[1].astype(F32)) + s_ref[2].astype(F32)) + s_ref[3].astype(F32)
        o_ref[...] = tot.T if transposed else tot

    if transposed:
        out_spec = pl.BlockSpec((c, tr), lambda i, core_ref: (0, core_ref[0] * nblk + i))
        out_shape = _SDS((c, 2 * h), F32)
    else:
        out_spec = pl.BlockSpec((tr, c), lambda i, core_ref: (core_ref[0] * nblk + i, 0))
        out_shape = _SDS((2 * h, c), F32)
    grid_spec = pltpu.PrefetchScalarGridSpec(
        num_scalar_prefetch=1, grid=(nblk,),
        in_specs=[pl.BlockSpec((N_CHIPS, tr, c), lambda i, core_ref: (0, i, 0))],
        out_specs=out_spec)
    return pl.pallas_call(
        body, name=name, grid_spec=grid_spec, out_shape=out_shape,
        compiler_params=_params(("arbitrary",)),
    )(core, slabs)


def _add_halves(own, recv, core, name):
    ns, r, c = own.shape
    half = r // 2
    tr = 128
    nblk = half // tr

    def body(core_ref, a_ref, b_ref, o_ref):
        o_ref[...] = (a_ref[...].astype(F32) + b_ref[...].astype(F32)).astype(o_ref.dtype)

    grid_spec = pltpu.PrefetchScalarGridSpec(
        num_scalar_prefetch=1, grid=(nblk,),
        in_specs=[pl.BlockSpec((ns, tr, c), lambda i, core_ref: (0, core_ref[0] * nblk + i, 0)),
                  pl.BlockSpec((ns, tr, c), lambda i, core_ref: (0, i, 0))],
        out_specs=pl.BlockSpec((ns, tr, c), lambda i, core_ref: (0, i, 0)))
    return pl.pallas_call(
        body, name=name, grid_spec=grid_spec, out_shape=_SDS((ns, half, c), own.dtype),
        compiler_params=_params(("arbitrary",)),
    )(core, own, recv)


def _update_big(w, m, v, g, name):
    r, c = w.shape

    def body(w_ref, m_ref, v_ref, g_ref, d_ref, mo_ref, vo_ref):
        delta, m_new, v_new = _adamw(w_ref[...], g_ref[...], m_ref[...], v_ref[...])
        d_ref[...] = delta
        mo_ref[...] = m_new
        vo_ref[...] = v_new

    if r % 128 == 0:
        steps, spec = r // 128, pl.BlockSpec((128, c), lambda i: (i, 0))
    else:
        steps, spec = c // 128, pl.BlockSpec((r, 128), lambda i: (0, i))
    return pl.pallas_call(
        body, name=name, grid=(steps,),
        in_specs=[spec] * 4, out_specs=[spec] * 3,
        out_shape=[_SDS((r, c), F32)] * 3,
        compiler_params=_params(("arbitrary",)),
    )(w, m, v, g)


_ROW = dict(norm_w=0, sb_norm_w=1, ssd_norm_w=2, final_norm_w=3, conv_b=4, dt_bias=5, a_log=6, d_skip=7,
            conv_w=8, sq_err=12, meta_tokens=16)
_SMALL = ("meta_tokens", "norm_w", "conv_w", "conv_b", "dt_bias", "a_log", "d_skip", "sb_norm_w", "ssd_norm_w",
          "final_norm_w")


def _pack_small(sq_err, grads):
    def rowpad(a):
        return jnp.pad(a, ((0, 0), (0, SMALL_COLS - a.shape[1])))

    rows = [rowpad(grads[k]) for k in ("norm_w", "sb_norm_w", "ssd_norm_w", "final_norm_w", "conv_b", "dt_bias", "a_log", "d_skip")]
    rows.append(grads["conv_w"])
    rows.append(rowpad(sq_err))
    rows.append(jnp.zeros((3, SMALL_COLS), F32))
    rows.append(rowpad(grads["meta_tokens"]))
    return jnp.concatenate(rows, axis=0)


def _update_small(gathered, ws, ms, vs):
    names = _SMALL
    n = len(names)

    def body(*refs):
        g_ref = refs[0]
        w_refs, m_refs, v_refs = refs[1:1 + n], refs[1 + n:1 + 2 * n], refs[1 + 2 * n:1 + 3 * n]
        outs = refs[1 + 3 * n:]
        loss_ref = outs[0]
        go, do, mo, vo = outs[1:1 + n], outs[1 + n:1 + 2 * n], outs[1 + 2 * n:1 + 3 * n], outs[1 + 3 * n:1 + 4 * n]
        tot = g_ref[0]
        for d in range(1, N_DEV):
            tot = tot + g_ref[d]
        x, y, _ = _place()
        chip = 2 * x + y
        loss_ref[...] = jnp.broadcast_to(
            0.5 * jnp.sum(tot[_ROW["sq_err"]:_ROW["sq_err"] + 1, 0:D_MODEL], axis=1, keepdims=True) / D_MODEL, (1, 128))
        for idx, nm in enumerate(names):
            r0 = _ROW[nm]
            rows, cols = w_refs[idx].shape
            if nm in ("conv_w", "meta_tokens"):
                g = jnp.zeros((rows, cols), F32)
                for j in range(N_CHIPS):
                    g = g + jnp.where(chip == j, tot[r0:r0 + rows, j * cols:(j + 1) * cols], 0.0)
            else:
                g = tot[r0:r0 + rows, 0:cols]
            delta, m_new, v_new = _adamw(w_refs[idx][...], g, m_refs[idx][...], v_refs[idx][...])
            go[idx][...] = g
            do[idx][...] = delta
            mo[idx][...] = m_new
            vo[idx][...] = v_new

    shapes = [_SDS(ws[nm].shape, F32) for nm in names]
    vm = pl.BlockSpec(memory_space=_VMEM)
    res = pl.pallas_call(
        body, name="update_small",
        in_specs=[vm] * (1 + 3 * n), out_specs=[vm] * (1 + 4 * n),
        out_shape=[_SDS((1, 128), F32)] + shapes * 4,
    )(gathered, *[ws[nm] for nm in names], *[ms[nm] for nm in names], *[vs[nm] for nm in names])
    loss = res[0][0, 0]
    g = dict(zip(names, res[1:1 + n]))
    d = dict(zip(names, res[1 + n:1 + 2 * n]))
    m = dict(zip(names, res[1 + 2 * n:1 + 3 * n]))
    v = dict(zip(names, res[1 + 3 * n:1 + 4 * n]))
    return loss, g, d, m, v


_WEIGHTS = ("meta_tokens", "norm_w", "w_in", "conv_w", "conv_b", "dt_bias", "a_log", "d_skip", "sb_norm_w",
            "ssd_norm_w", "w_out", "final_norm_w")


def kernel(x, meta_tokens, norm_w, w_in, conv_w, conv_b, dt_bias, a_log, d_skip, sb_norm_w, ssd_norm_w, w_out, final_norm_w, loss_target, m_meta_tokens, m_norm_w, m_w_in, m_conv_w, m_conv_b, m_dt_bias, m_a_log, m_d_skip, m_sb_norm_w, m_ssd_norm_w, m_w_out, m_final_norm_w, v_meta_tokens, v_norm_w, v_w_in, v_conv_w, v_conv_b, v_dt_bias, v_a_log, v_d_skip, v_sb_norm_w, v_ssd_norm_w, v_w_out, v_final_norm_w):
    given = dict(meta_tokens=meta_tokens, norm_w=norm_w, w_in=w_in, conv_w=conv_w, conv_b=conv_b, dt_bias=dt_bias,
                 a_log=a_log, d_skip=d_skip, sb_norm_w=sb_norm_w, ssd_norm_w=ssd_norm_w, w_out=w_out,
                 final_norm_w=final_norm_w)
    mom = dict(meta_tokens=m_meta_tokens, norm_w=m_norm_w, w_in=m_w_in, conv_w=m_conv_w, conv_b=m_conv_b,
               dt_bias=m_dt_bias, a_log=m_a_log, d_skip=m_d_skip, sb_norm_w=m_sb_norm_w, ssd_norm_w=m_ssd_norm_w,
               w_out=m_w_out, final_norm_w=m_final_norm_w)
    var = dict(meta_tokens=v_meta_tokens, norm_w=v_norm_w, w_in=v_w_in, conv_w=v_conv_w, conv_b=v_conv_b,
               dt_bias=v_dt_bias, a_log=v_a_log, d_skip=v_d_skip, sb_norm_w=v_sb_norm_w, ssd_norm_w=v_ssd_norm_w,
               w_out=v_w_out, final_norm_w=v_final_norm_w)
    seq = x.shape[1]

    def two_d(a):
        return a.reshape((-1, a.shape[-1])) if a.ndim != 2 else a

    def rows_first(a):
        return jnp.transpose(a, (2, 0, 1)).reshape(W_IN_SHARD, D_MODEL)

    def rows_last(a):
        return jnp.transpose(a.reshape(W_IN_SHARD, 1, D_MODEL), (1, 2, 0))

    w_in_t, m_in_t, v_in_t = rows_first(w_in), rows_first(m_w_in), rows_first(v_w_in)

    g_win, g_wout, g_meta, g_cw = _gather_shards(
        [w_in_t.astype(_MXU), w_out[0].astype(_MXU), meta_tokens, conv_w[0]], 2)
    w_t = jnp.pad(g_win.reshape(D_IN, D_MODEL), ((0, W_ALL - D_IN), (0, 0)))
    w_out_full = g_wout.reshape(2 * D_MODEL, D_MODEL)
    meta_full = jnp.swapaxes(g_meta, 0, 1).reshape(N_META, D_MODEL)
    conv_w_full = jnp.swapaxes(g_cw, 0, 1).reshape(4, XBC_W)

    core = lax.axis_index("c").astype(jnp.int32).reshape(1)

    def early(g_wout):
        slab_out = g_wout.reshape(N_CHIPS, W_OUT_SHARD, D_MODEL).astype(_MXU)
        (sib_out,) = _swap_halves([slab_out], "swap_halves_w_out")
        return (_add_halves(slab_out, sib_out, core, "chip_sum_w_out"),), ((0, N_CHIPS),)

    def late(g_win):
        (sib_in,) = _swap_halves([g_win], "swap_halves_w_in")
        return (_add_halves(g_win, sib_in, core, "chip_sum_w_in"),), ((0, N_CHIPS),)

    sq_err, g_x, grads = _device_grads(
        x.reshape(seq, D_MODEL), loss_target.reshape(seq, D_MODEL), meta_full, norm_w, w_t, conv_w_full,
        conv_b, dt_bias, a_log, d_skip, sb_norm_w, ssd_norm_w, w_out_full, final_norm_w.reshape(1, D_MODEL),
        exchange=dict(early=early, late=late))
    chip_out, chip_in = grads["sent"]
    got_out, got_in = grads["arrived"]
    chip = 2 * lax.axis_index("x") + lax.axis_index("y")

    def with_own(got, sent):
        own = lax.dynamic_slice(sent, (chip, 0, 0), (1,) + sent.shape[1:])
        return lax.dynamic_update_slice(got, own, (chip, 0, 0))

    g_in, g_out = _join_halves([_sum_slabs(with_own(got_in, chip_in), core, "sum_w_in", transposed=True),
                                _sum_slabs(with_own(got_out, chip_out), core, "sum_w_out")], (True, False))
    g_in = lax.dynamic_slice(g_in, (4 * chip, 0), (W_IN_SHARD, D_MODEL))
    big = dict(w_in=tuple(rows_last(a) for a in (g_in,) + tuple(_update_big(w_in_t, m_in_t, v_in_t, g_in, "update_w_in"))),
               w_out=(g_out,) + tuple(_update_big(w_out[0], m_w_out[0], v_w_out[0], g_out, "update_w_out")))

    gathered = _gather_small(_pack_small(sq_err, grads))
    loss, sg, sd, sm, sv = _update_small(
        gathered, {k: two_d(given[k]) for k in _SMALL}, {k: two_d(mom[k]) for k in _SMALL},
        {k: two_d(var[k]) for k in _SMALL})

    out = {}
    for idx, group in enumerate((sg, sd, sm, sv)):
        for k in _SMALL:
            out[(idx, k)] = group[k].reshape(given[k].shape)
        for k in ("w_in", "w_out"):
            out[(idx, k)] = big[k][idx].reshape(given[k].shape)
    return (loss, g_x.reshape(x.shape), *[out[(idx, k)] for idx in range(4) for k in _WEIGHTS])
```

```python
import functools
import math

import jax
import jax.numpy as jnp
from jax import lax
from jax.experimental import pallas as pl
from jax.experimental.pallas import tpu as pltpu

F32 = jnp.float32
_MXU = jnp.bfloat16

D_MODEL = 1024
N_META = 16
PAD = 112
OFF = PAD + N_META
TM = 128
CHUNK = 64
SB_W = 1024
SSD_W = 1024
N_HEADS = 16
HEAD = 64
N_GROUPS = 2
N_STATE = 128
XBC_W = SSD_W + 2 * N_GROUPS * N_STATE
N_MAIN = 4 * SB_W + SSD_W + XBC_W
QKV_W = 3 * SB_W
REST_W = N_MAIN - QKV_W
COL_GATE = 3 * SB_W
COL_Z = 4 * SB_W
COL_XBC = 5 * SB_W
D_IN = N_MAIN + N_HEADS
W_ALL = N_MAIN + 128
WIN_STEP = 1664
WIN_W = 1792
EPS = 1e-5
N_CHIPS = 4
W_IN_SHARD = D_IN // N_CHIPS
W_OUT_SHARD = 2 * D_MODEL // N_CHIPS

ADAM_LR = 0.001
ADAM_B1 = 0.9
ADAM_B2 = 0.999
ADAM_EPS = 1e-08
ADAM_WD = 0.01
ADAM_STEP = 10

_SDS = jax.ShapeDtypeStruct
_NT = (((1,), (1,)), ((), ()))
_TN = (((0,), (0,)), ((), ()))
_VMEM = pltpu.VMEM


def _params(sem=None, vmem_mb=None):
    kw = {}
    if sem is not None:
        kw["dimension_semantics"] = sem
    if vmem_mb is not None:
        kw["vmem_limit_bytes"] = vmem_mb * 1024 * 1024
    return pltpu.CompilerParams(**kw)


def _mm(a, b):
    return jnp.dot(a.astype(_MXU), b.astype(_MXU), preferred_element_type=F32)


def _mm_nt(a, b):
    return lax.dot_general(a.astype(_MXU), b.astype(_MXU), _NT, preferred_element_type=F32)


def _mm_tn(a, b):
    return lax.dot_general(a.astype(_MXU), b.astype(_MXU), _TN, preferred_element_type=F32)


def _split(x, parts):
    out = []
    r = x
    for _ in range(parts):
        p = r.astype(_MXU)
        out.append(p)
        r = r - p.astype(F32)
    return out


def _sel_right(x, m01, parts=3):
    acc = None
    for p in _split(x, parts):
        t = jnp.dot(p, m01, preferred_element_type=F32)
        acc = t if acc is None else acc + t
    return acc


def _sel_left(m01, x, parts=3):
    acc = None
    for p in _split(x, parts):
        t = jnp.dot(m01, p, preferred_element_type=F32)
        acc = t if acc is None else acc + t
    return acc


def _iota(shape, axis):
    return lax.broadcasted_iota(jnp.int32, shape, axis)


def _sigmoid(x):
    return 1.0 / (1.0 + jnp.exp(-x))


def _prep(x2d, meta_full, norm_w):
    seq = x2d.shape[0]
    lp = seq + OFF
    nb = lp // TM

    def body(x_ref, meta_ref, w_ref, h0_ref, u_ref, ut_ref):
        i = pl.program_id(0)

        @pl.when(i == 0)
        def _():
            h0_ref[...] = jnp.concatenate([jnp.zeros((PAD, D_MODEL), F32), meta_ref[...]], axis=0)

        @pl.when(i > 0)
        def _():
            h0_ref[...] = x_ref[...]

        h = h0_ref[...]
        rs = lax.rsqrt(jnp.mean(h * h, axis=-1, keepdims=True) + EPS)
        u = (h * rs * w_ref[...]).astype(_MXU)
        u_ref[...] = u
        ut_ref[...] = u.T

    return pl.pallas_call(
        body, name="prep", grid=(nb,),
        in_specs=[pl.BlockSpec((TM, D_MODEL), lambda i: (jnp.maximum(i - 1, 0), 0)),
                  pl.BlockSpec((N_META, D_MODEL), lambda i: (0, 0)),
                  pl.BlockSpec((1, D_MODEL), lambda i: (0, 0))],
        out_specs=[pl.BlockSpec((TM, D_MODEL), lambda i: (i, 0)),
                   pl.BlockSpec((TM, D_MODEL), lambda i: (i, 0)),
                   pl.BlockSpec((D_MODEL, TM), lambda i: (0, i))],
        out_shape=[_SDS((lp, D_MODEL), F32), _SDS((lp, D_MODEL), _MXU), _SDS((D_MODEL, lp), _MXU)],
        compiler_params=_params(("arbitrary",)),
    )(x2d, meta_full, norm_w)


def _inproj(u, w_t):
    lp = u.shape[0]
    tn = 512

    nq = QKV_W // tn

    def body(u_ref, w_ref, wdt_ref, qkv_ref, rest_ref, odt_ref):
        j = pl.program_id(0)
        res = lax.dot_general(u_ref[...], w_ref[...], _NT, preferred_element_type=F32)

        @pl.when(j < nq)
        def _():
            qkv_ref[...] = res.astype(qkv_ref.dtype)

        @pl.when(j >= nq)
        def _():
            rest_ref[...] = res

        @pl.when(j == 0)
        def _():
            odt_ref[...] = lax.dot_general(u_ref[...], wdt_ref[...], _NT, preferred_element_type=F32)

    return pl.pallas_call(
        body, name="inproj", grid=(N_MAIN // tn,),
        in_specs=[pl.BlockSpec((lp, D_MODEL), lambda j: (0, 0)),
                  pl.BlockSpec((tn, D_MODEL), lambda j: (j, 0)),
                  pl.BlockSpec((128, D_MODEL), lambda j: (N_MAIN // 128, 0))],
        out_specs=[pl.BlockSpec((lp, tn), lambda j: (0, jnp.minimum(j, nq - 1))),
                   pl.BlockSpec((lp, tn), lambda j: (0, jnp.maximum(j - nq, 0))),
                   pl.BlockSpec((lp, 128), lambda j: (0, 0))],
        out_shape=[_SDS((lp, QKV_W), _MXU), _SDS((lp, REST_W), F32), _SDS((lp, 128), F32)],
        compiler_params=_params(("arbitrary",), 48),
    )(u, w_t, w_t)


SB_WINDOW = 3
SB_TOP = 16
SB_DEAD = -104.0


def _sb_logs(qh, kwin):
    z = lax.dot_general(qh, kwin, _NT, preferred_element_type=F32)
    e = jnp.exp(-jnp.abs(z))
    l1p = jnp.log(1.0 + e)
    lk_full = -(jnp.maximum(z, 0.0) + l1p)
    ls = jnp.minimum(z, 0.0) - l1p
    return z, e, ls, lk_full


def _blk(a, b):
    return a[:, b * TM:(b + 1) * TM]


def _stacked_sel(blocks, m01):
    n = len(blocks)
    rows = blocks[0].shape[0]
    pieces = [_split(b, 2) for b in blocks]
    stacked = jnp.concatenate([p[0] for p in pieces] + [p[1] for p in pieces], axis=0)
    res = jnp.dot(stacked, m01, preferred_element_type=F32)
    return [res[j * rows:(j + 1) * rows] + res[(n + j) * rows:(n + j + 1) * rows] for j in range(n)]


def _sb_weights(ls, lk_full, run, last_mask, upper, n):
    lk = [_blk(lk_full, b) for b in range(n)]
    lk[n - 1] = jnp.where(last_mask, lk[n - 1], 0.0)
    aft = _stacked_sel(lk, upper)
    w = [None] * n
    for b in range(n - 1, -1, -1):
        wb = jnp.exp(_blk(ls, b) + aft[b] + run)
        w[b] = jnp.where(last_mask, wb, 0.0) if b == n - 1 else wb
        run = run + jnp.sum(lk[b], axis=1, keepdims=True)
    return w, run


def _sb_alive(run_scr):
    top = jnp.max(run_scr[:, 0:SB_TOP, :]) > SB_DEAD
    rest = jnp.max(run_scr[:, SB_TOP:, :]) > SB_DEAD
    return top.astype(jnp.int32), rest.astype(jnp.int32)


def _sb_walk(i, key_set, strict, run_scr):
    @pl.when(i >= SB_WINDOW - 1)
    def _():
        key_set(i - (SB_WINDOW - 1), SB_WINDOW, strict, TM)

    start = jnp.where(i >= SB_WINDOW - 1, i - SB_WINDOW, i)

    def cond(c):
        return jnp.logical_and(c[0] >= 0, c[1] + c[2] > 0)

    def step(c):
        kb, _, rest = c
        mask = jnp.logical_or(strict, kb < i)

        @pl.when(rest > 0)
        def _():
            key_set(kb, 1, mask, TM)

        @pl.when(rest == 0)
        def _():
            key_set(kb, 1, mask, SB_TOP)

        return (kb - 1,) + _sb_alive(run_scr)

    lax.while_loop(cond, step, (start,) + _sb_alive(run_scr))


def _sb_fwd(qkv):
    lp = qkv.shape[0]
    nb = lp // TM

    def body(q_ref, k_ref, v_ref, o_ref, olo_ref, acc, run_scr):
        i = pl.program_id(1)
        lane = _iota((TM, TM), 1)
        row = _iota((TM, TM), 0)
        head0 = lane < HEAD
        upper = (row > lane).astype(_MXU)
        strict = lane < row
        q = q_ref[...] * (1.0 / math.sqrt(HEAD))
        qh = (jnp.where(head0, q, 0.0).astype(_MXU), jnp.where(head0, 0.0, q).astype(_MXU))

        def key_set(first, n, last_mask, nrows):
            off = pl.multiple_of(first * TM, TM)
            kwin = k_ref[pl.ds(off, n * TM), :].astype(_MXU)
            vwin = v_ref[pl.ds(off, n * TM), :].astype(_MXU)
            for hh in range(2):
                run = run_scr[hh, 0:nrows, 0:1]
                _, _, ls, lk_full = _sb_logs(qh[hh][0:nrows], kwin)
                w, run = _sb_weights(ls, lk_full, run, last_mask[0:nrows], upper, n)
                pieces = [_split(wb, 2) for wb in w]
                stacked = jnp.concatenate(
                    [jnp.concatenate([p[0] for p in pieces], axis=1), jnp.concatenate([p[1] for p in pieces], axis=1)], axis=0)
                res = jnp.dot(stacked, vwin, preferred_element_type=F32)
                acc[hh, 0:nrows] += res[0:nrows]
                acc[2 + hh, 0:nrows] += res[nrows:2 * nrows]
                run_scr[hh, 0:nrows] = jnp.broadcast_to(run, (nrows, TM))

        acc[...] = jnp.zeros_like(acc)
        run_scr[...] = jnp.zeros_like(run_scr)
        _sb_walk(i, key_set, strict, run_scr)
        o_ref[...] = jnp.where(head0, acc[0], acc[1])
        olo_ref[...] = jnp.where(head0, acc[2], acc[3])

    npair = SB_W // TM
    blk = pl.BlockSpec((TM, TM), lambda p, i: (i, p))
    return pl.pallas_call(
        body, name="sb_fwd", grid=(npair, nb),
        in_specs=[blk,
                  pl.BlockSpec((lp, TM), lambda p, i: (0, npair + p)),
                  pl.BlockSpec((lp, TM), lambda p, i: (0, 2 * npair + p))],
        out_specs=[blk, blk],
        out_shape=[_SDS((lp, SB_W), F32), _SDS((lp, SB_W), F32)],
        scratch_shapes=[pltpu.VMEM((4, TM, TM), F32), pltpu.VMEM((2, TM, TM), F32)],
        compiler_params=_params(("arbitrary", "arbitrary")),
    )(qkv, qkv, qkv)


def _sb_bwd(qkv, o_sb, o_lo, d_o, d_proj, send=(), dests=()):
    lp = qkv.shape[0]
    nb = lp // TM
    npair = SB_W // TM
    scale = 1.0 / math.sqrt(HEAD)
    n = len(send)
    host_in, host_out, host_shapes, host_sems = _host_specs(send)

    def body(q_ref, k_ref, v_ref, o_ref, olo_ref, do_ref, dproj_in, *rest):
        srcs, dproj_ref, dsts = rest[:n], rest[n], rest[n + 1:2 * n + 1]
        dq_all, dk_ref, dv_ref, stage, sems, dq_acc, run_scr, gsum_scr = rest[2 * n + 1:2 * n + 9]
        host_sem_refs = rest[2 * n + 9:]
        p = pl.program_id(0)
        i = pl.program_id(1)

        if n:
            @pl.when(jnp.logical_and(p == 0, i == 0))
            def _():
                _start_copies(_slab_copies(srcs, dsts, dests, *host_sem_refs))

        @pl.when(i == 0)
        def _():
            dk_ref[...] = jnp.zeros_like(dk_ref)
            dv_ref[...] = jnp.zeros_like(dv_ref)

        lane = _iota((TM, TM), 1)
        row = _iota((TM, TM), 0)
        head0 = lane < HEAD
        hmask = (head0, jnp.logical_not(head0))
        upper = (row > lane).astype(_MXU)
        lower_incl = (row >= lane).astype(_MXU)
        strict = lane < row
        q = q_ref[...] * scale
        do = do_ref[...]
        prod = do.astype(_MXU).astype(F32) * (o_ref[...] + olo_ref[...])
        qh = tuple(jnp.where(m, q, 0.0).astype(_MXU) for m in hmask)
        doh = tuple(jnp.where(m, do, 0.0).astype(_MXU) for m in hmask)
        gtot = tuple(jnp.sum(jnp.where(m, prod, 0.0), axis=1, keepdims=True) for m in hmask)

        def key_set(first, n, last_mask, nrows):
            off = pl.multiple_of(first * TM, TM)
            kf = k_ref[pl.ds(off, n * TM), :]
            kwin = kf.astype(_MXU)
            vwin = v_ref[pl.ds(off, n * TM), :].astype(_MXU)
            last_mask = last_mask[0:nrows]
            dk_win = None
            for hh in range(2):
                run = run_scr[hh, 0:nrows, 0:1]
                gsum = gsum_scr[hh, 0:nrows, 0:1]
                z, e, ls, lk_full = _sb_logs(qh[hh][0:nrows], kwin)
                w, run = _sb_weights(ls, lk_full, run, last_mask, upper, n)
                r = 1.0 / (1.0 + e)
                er = e * r
                pos = z >= 0.0
                beta = jnp.where(pos, r, er)
                one_m_beta = jnp.where(pos, er, r)
                dw = lax.dot_general(doh[hh][0:nrows], vwin, _NT, preferred_element_type=F32)
                g = [_blk(dw, b) * w[b] for b in range(n)]
                suffix = _stacked_sel(g, lower_incl)
                dz = [None] * n
                for b in range(n - 1, -1, -1):
                    prefix = gtot[hh][0:nrows] - gsum - suffix[b]
                    d = g[b] * _blk(one_m_beta, b) - _blk(beta, b) * prefix
                    dz[b] = (jnp.where(last_mask, d, 0.0) if b == n - 1 else d).astype(_MXU)
                    gsum = gsum + jnp.sum(g[b], axis=1, keepdims=True)
                dzw = jnp.concatenate(dz, axis=1)
                ww = jnp.concatenate([wb.astype(_MXU) for wb in w], axis=1)
                kh = jnp.where(hmask[hh][0:1, :], kf, 0.0).astype(_MXU)
                dq_acc[0:nrows] += jnp.dot(dzw, kh, preferred_element_type=F32)
                dk_h = lax.dot_general(dzw, qh[hh][0:nrows], _TN, preferred_element_type=F32)
                dv_h = lax.dot_general(ww, doh[hh][0:nrows], _TN, preferred_element_type=F32)
                dk_win = (dk_h, dv_h) if dk_win is None else (dk_win[0] + dk_h, dk_win[1] + dv_h)
                run_scr[hh, 0:nrows] = jnp.broadcast_to(run, (nrows, TM))
                gsum_scr[hh, 0:nrows] = jnp.broadcast_to(gsum, (nrows, TM))
            dk_ref[pl.ds(off, n * TM), :] += dk_win[0]
            dv_ref[pl.ds(off, n * TM), :] += dk_win[1]

        dq_acc[...] = jnp.zeros_like(dq_acc)
        run_scr[...] = jnp.zeros_like(run_scr)
        gsum_scr[...] = jnp.zeros_like(gsum_scr)

        _sb_walk(i, key_set, strict, run_scr)
        dq_all[pl.ds(pl.multiple_of(i * TM, TM), TM), :] = dq_acc[...] * scale

        @pl.when(i == nb - 1)
        def _():
            copies = []
            for s, src in enumerate((dq_all, dk_ref, dv_ref)):
                stage[s] = src[...].astype(_MXU)
                col = pl.multiple_of((s * npair + p) * TM, TM)
                copies.append(pltpu.make_async_copy(stage.at[s], dproj_ref.at[:, pl.ds(col, TM)], sems.at[s]))
                copies[-1].start()
            for cp in copies:
                cp.wait()

        if n:
            @pl.when(jnp.logical_and(p == npair - 1, i == nb - 1))
            def _():
                _finish_copies(_slab_copies(srcs, dsts, dests, *host_sem_refs))

    blk = pl.BlockSpec((TM, TM), lambda p, i: (i, p))
    res = pl.pallas_call(
        body, name="sb_bwd", grid=(npair, nb),
        in_specs=[blk,
                  pl.BlockSpec((lp, TM), lambda p, i: (0, npair + p)),
                  pl.BlockSpec((lp, TM), lambda p, i: (0, 2 * npair + p)),
                  blk, blk, blk, pl.BlockSpec(memory_space=pl.ANY)] + host_in,
        out_specs=[pl.BlockSpec(memory_space=pl.ANY)] + host_out,
        out_shape=[_SDS(d_proj.shape, d_proj.dtype)] + host_shapes,
        input_output_aliases={6: 0},
        scratch_shapes=[pltpu.VMEM((lp, TM), F32), pltpu.VMEM((lp, TM), F32), pltpu.VMEM((lp, TM), F32),
                        pltpu.VMEM((3, lp, TM), _MXU), pltpu.SemaphoreType.DMA((3,)),
                        pltpu.VMEM((TM, TM), F32), pltpu.VMEM((2, TM, TM), F32), pltpu.VMEM((2, TM, TM), F32)] + host_sems,
        compiler_params=_params(("arbitrary", "arbitrary")),
    )(qkv, qkv, qkv, o_sb, o_lo, d_o, d_proj, *send)
    return res[0], list(res[1:])


def _conv_pre(x_ref, w_ref, b_ref, lp):
    n = lp - 8
    w = w_ref[...]
    pre = (x_ref[pl.ds(5, n), :] * w[0:1, :] + x_ref[pl.ds(6, n), :] * w[1:2, :]
           + x_ref[pl.ds(7, n), :] * w[2:3, :] + x_ref[pl.ds(8, n), :] * w[3:4, :]) + b_ref[...]
    live = (_iota((n, 128), 0) + 8) >= PAD
    return pre, live


def _conv_fwd(proj, dt_raw, conv_w, conv_b, dt_bias128):
    lp = proj.shape[0]
    nblk = XBC_W // 128
    c0 = (COL_XBC - QKV_W) // 128

    def body(x_ref, w_ref, b_ref, dtr_ref, dtb_ref, o_ref, dt_ref):
        pre, live = _conv_pre(x_ref, w_ref, b_ref, lp)
        act = pre * _sigmoid(pre)
        o_ref[pl.ds(0, 8), :] = jnp.zeros((8, 128), F32)
        o_ref[pl.ds(8, lp - 8), :] = jnp.where(live, act, 0.0)

        @pl.when(pl.program_id(0) == 0)
        def _():
            s = dtr_ref[...] + dtb_ref[...]
            sp = jnp.maximum(s, 0.0) + jnp.log(1.0 + jnp.exp(-jnp.abs(s)))
            dt_ref[...] = jnp.where(_iota((lp, 128), 0) >= PAD, sp, 0.0)

    return pl.pallas_call(
        body, name="conv_fwd", grid=(nblk,),
        in_specs=[pl.BlockSpec((lp, 128), lambda j: (0, c0 + j)),
                  pl.BlockSpec((4, 128), lambda j: (0, j)),
                  pl.BlockSpec((1, 128), lambda j: (0, j)),
                  pl.BlockSpec((lp, 128), lambda j: (0, 0)),
                  pl.BlockSpec((1, 128), lambda j: (0, 0))],
        out_specs=[pl.BlockSpec((lp, 128), lambda j: (0, j)),
                   pl.BlockSpec((lp, 128), lambda j: (0, 0))],
        out_shape=[_SDS((lp, XBC_W), F32), _SDS((lp, 128), F32)],
        compiler_params=_params(("arbitrary",)),
    )(proj, conv_w, conv_b, dt_raw, dt_bias128)


def _conv_bwd(proj, dt_raw, conv_w, conv_b, dt_bias128, d_xbc, d_dt128, d_proj):
    lp = proj.shape[0]
    nblk = XBC_W // 128
    c0 = COL_XBC // 128
    c0_in = (COL_XBC - QKV_W) // 128
    n = lp - 8
    last = nblk - 1

    def body(x_ref, w_ref, b_ref, dtr_ref, dtb_ref, dy_ref, ddt_ref, dproj_in,
             dx_ref, gw_ref, gb_ref, gdtb_ref, scr):
        j = pl.program_id(0)

        @pl.when(j < nblk)
        def _():
            pre, live = _conv_pre(x_ref, w_ref, b_ref, lp)
            sg = _sigmoid(pre)
            dpre = jnp.where(live, dy_ref[pl.ds(8, n), :] * (sg * (1.0 + pre * (1.0 - sg))), 0.0)
            gb_ref[...] = jnp.sum(dpre, axis=0, keepdims=True)
            gw_ref[...] = jnp.concatenate(
                [jnp.sum(dpre * x_ref[pl.ds(5 + k, n), :], axis=0, keepdims=True) for k in range(4)], axis=0)
            scr[pl.ds(0, 8), :] = jnp.zeros((8, 128), F32)
            scr[pl.ds(8, n), :] = dpre
            scr[pl.ds(lp, 8), :] = jnp.zeros((8, 128), F32)
            w = w_ref[...]
            dx_ref[...] = (scr[pl.ds(0, lp), :] * w[3:4, :] + scr[pl.ds(1, lp), :] * w[2:3, :]
                           + scr[pl.ds(2, lp), :] * w[1:2, :] + scr[pl.ds(3, lp), :] * w[0:1, :]).astype(dx_ref.dtype)

        @pl.when(j == nblk)
        def _():
            s = dtr_ref[...] + dtb_ref[...]
            d = jnp.where(_iota((lp, 128), 0) >= PAD, ddt_ref[...] * _sigmoid(s), 0.0)
            dx_ref[...] = d.astype(dx_ref.dtype)
            gdtb_ref[...] = jnp.sum(d, axis=0, keepdims=True)

    clamp = lambda j: (0, jnp.minimum(j, last))
    full128 = pl.BlockSpec((lp, 128), lambda j: (0, 0))
    return pl.pallas_call(
        body, name="conv_bwd", grid=(nblk + 1,),
        in_specs=[pl.BlockSpec((lp, 128), lambda j: (0, c0_in + jnp.minimum(j, last))),
                  pl.BlockSpec((4, 128), clamp),
                  pl.BlockSpec((1, 128), clamp),
                  full128, pl.BlockSpec((1, 128), lambda j: (0, 0)),
                  pl.BlockSpec((lp, 128), clamp), full128, pl.BlockSpec(memory_space=pl.ANY)],
        out_specs=[pl.BlockSpec((lp, 128), lambda j: (0, c0 + j)), pl.BlockSpec((4, 128), clamp),
                   pl.BlockSpec((1, 128), clamp), pl.BlockSpec((1, 128), lambda j: (0, 0))],
        out_shape=[_SDS(d_proj.shape, d_proj.dtype), _SDS((4, XBC_W), F32), _SDS((1, XBC_W), F32), _SDS((1, 128), F32)],
        input_output_aliases={7: 0},
        scratch_shapes=[pltpu.VMEM((lp + 8, 128), F32)],
        compiler_params=_params(("arbitrary",)),
    )(proj, conv_w, conv_b, dt_raw, dt_bias128, d_xbc, d_dt128, d_proj)


def _ssd_pieces(dt, dt_t, a, a_t):
    r64 = _iota((CHUNK, CHUNK), 0)
    c64 = _iota((CHUNK, CHUNK), 1)
    tril = c64 <= r64
    tril01 = tril.astype(_MXU)
    triu01 = (r64 <= c64).astype(_MXU)
    expand = (lax.shift_right_logical(_iota((N_HEADS, SSD_W), 1), 6) == _iota((N_HEADS, SSD_W), 0)).astype(_MXU)
    acum = _sel_left(tril01, dt * a)
    acum_t = _sel_right(dt_t * a_t, triu01)
    ax = _sel_right(acum, expand)
    dtx = _sel_right(dt, expand)
    return tril, expand, acum, acum_t, ax, dtx


def _seg_matrix():
    return (lax.shift_right_logical(_iota((SSD_W, N_HEADS), 0), 6) == _iota((SSD_W, N_HEADS), 1)).astype(_MXU)


def _head_decay(ax, acum_t, h, tril):
    col = ax[:, h * HEAD:(h + 1) * HEAD]
    rowv = acum_t[h:h + 1, :]
    return jnp.where(tril, jnp.exp(jnp.minimum(col - rowv, 0.0)), 0.0)


def _ssd_fwd(xbc, dt_c, dt_tc, a, a_t, dskip_x):
    lp = xbc.shape[0]
    nc = lp // CHUNK
    gw = SSD_W // N_GROUPS
    hpg = N_HEADS // N_GROUPS

    def body(x_ref, dt_ref, dtt_ref, a_ref, at_ref, d_ref, y_ref, st_ref, state):
        c = pl.program_id(0)

        @pl.when(c == 0)
        def _():
            state[...] = jnp.zeros_like(state)

        st_ref[0] = state[...]
        tril, _, _, acum_t, ax, dtx = _ssd_pieces(dt_ref[0], dtt_ref[0], a_ref[...], at_ref[...])
        x = x_ref[:, 0:SSD_W]
        xdt = x * dtx
        ea = jnp.exp(ax)
        aex = ax[CHUNK - 1:CHUNK, :]
        wd = jnp.exp(aex - ax)
        eae = jnp.exp(aex)
        xw = xdt * wd
        y_ref[...] = x * d_ref[...]
        for g in range(N_GROUPS):
            gs = slice(g * gw, (g + 1) * gw)
            rs = slice(g * N_STATE, (g + 1) * N_STATE)
            bg = x_ref[:, SSD_W + g * N_STATE:SSD_W + (g + 1) * N_STATE]
            cg = x_ref[:, SSD_W + N_GROUPS * N_STATE + g * N_STATE:SSD_W + N_GROUPS * N_STATE + (g + 1) * N_STATE]
            sg = state[rs, :]
            cb = _mm_nt(cg, bg)
            y_ref[:, gs] += _mm(cg, sg) * ea[:, gs]
            for r in range(hpg):
                h = g * hpg + r
                hs = slice(h * HEAD, (h + 1) * HEAD)
                m = cb * _head_decay(ax, acum_t, h, tril)
                y_ref[:, hs] += _mm(m, xdt[:, hs])
            state[rs, :] = sg * eae[:, gs] + _mm_tn(bg, xw[:, gs])

    return pl.pallas_call(
        body, name="ssd_fwd", grid=(nc,),
        in_specs=[pl.BlockSpec((CHUNK, XBC_W), lambda c: (c, 0)),
                  pl.BlockSpec((1, CHUNK, N_HEADS), lambda c: (c, 0, 0)),
                  pl.BlockSpec((1, N_HEADS, CHUNK), lambda c: (c, 0, 0)),
                  pl.BlockSpec((1, N_HEADS), lambda c: (0, 0)),
                  pl.BlockSpec((N_HEADS, 1), lambda c: (0, 0)),
                  pl.BlockSpec((1, SSD_W), lambda c: (0, 0))],
        out_specs=[pl.BlockSpec((CHUNK, SSD_W), lambda c: (c, 0)),
                   pl.BlockSpec((1, N_GROUPS * N_STATE, gw), lambda c: (c, 0, 0))],
        out_shape=[_SDS((lp, SSD_W), F32), _SDS((nc, N_GROUPS * N_STATE, gw), F32)],
        scratch_shapes=[pltpu.VMEM((N_GROUPS * N_STATE, gw), F32)],
        compiler_params=_params(("arbitrary",)),
    )(xbc, dt_c, dt_tc, a, a_t, dskip_x)


def _ssd_bwd(xbc, dt_c, dt_tc, a, a_t, dskip_x, states, d_y):
    lp = xbc.shape[0]
    nc = lp // CHUNK
    gw = SSD_W // N_GROUPS
    hpg = N_HEADS // N_GROUPS

    def body(x_ref, dt_ref, dtt_ref, a_ref, at_ref, d_ref, st_ref, dy_ref,
             dx_ref, ddta_ref, ddtb_ref, ga1_ref, ga2_ref, gd_ref, dstate, dxdt_scr, z_scr, yoff_scr, sds_scr):
        c = pl.program_id(0)

        @pl.when(c == 0)
        def _():
            dstate[...] = jnp.zeros_like(dstate)
            ga1_ref[...] = jnp.zeros_like(ga1_ref)
            ga2_ref[...] = jnp.zeros_like(ga2_ref)
            gd_ref[...] = jnp.zeros_like(gd_ref)

        dt = dt_ref[0]
        dt_t = dtt_ref[0]
        a = a_ref[...]
        a_t = at_ref[...]
        tril, _, acum, acum_t, ax, dtx = _ssd_pieces(dt, dt_t, a, a_t)
        seg = _seg_matrix()
        x = x_ref[:, 0:SSD_W]
        dy = dy_ref[...]
        xdt = x * dtx
        ea = jnp.exp(ax)
        aex = ax[CHUNK - 1:CHUNK, :]
        wd = jnp.exp(aex - ax)
        eae = jnp.exp(aex)
        xw = xdt * wd
        edy = ea * dy
        lane16 = _iota((CHUNK, N_HEADS), 1)
        row16 = _iota((N_HEADS, CHUNK), 0)
        da_col = jnp.zeros((CHUNK, N_HEADS), F32)
        da_row = jnp.zeros((N_HEADS, CHUNK), F32)
        for g in range(N_GROUPS):
            gs = slice(g * gw, (g + 1) * gw)
            rs = slice(g * N_STATE, (g + 1) * N_STATE)
            bcol = slice(SSD_W + g * N_STATE, SSD_W + (g + 1) * N_STATE)
            ccol = slice(SSD_W + N_GROUPS * N_STATE + g * N_STATE, SSD_W + N_GROUPS * N_STATE + (g + 1) * N_STATE)
            bg = x_ref[:, bcol]
            cg = x_ref[:, ccol]
            sg = st_ref[0, rs, :]
            dsn = dstate[rs, :]
            cb = _mm_nt(cg, bg)
            z_scr[:, gs] = _mm(bg, dsn)
            yoff_scr[:, gs] = _mm(cg, sg) * ea[:, gs]
            sds_scr[:, gs] = jnp.broadcast_to(jnp.sum(dsn * sg, axis=0, keepdims=True), (8, gw))
            dcb = jnp.zeros((CHUNK, CHUNK), F32)
            for r in range(hpg):
                h = g * hpg + r
                hs = slice(h * HEAD, (h + 1) * HEAD)
                dec = _head_decay(ax, acum_t, h, tril)
                m = cb * dec
                t1 = _mm_nt(dy[:, hs], xdt[:, hs])
                dcb = dcb + dec * t1
                tm = m * t1
                da_col = da_col + jnp.where(lane16 == h, jnp.sum(tm, axis=1, keepdims=True), 0.0)
                da_row = da_row - jnp.where(row16 == h, jnp.sum(tm, axis=0, keepdims=True), 0.0)
                dxdt_scr[:, hs] = _mm_tn(m, dy[:, hs])
            dx_ref[:, ccol] = _mm(dcb, bg) + _mm_nt(edy[:, gs], sg)
            dx_ref[:, bcol] = _mm_tn(dcb, cg) + _mm_nt(xw[:, gs], dsn)
            dstate[rs, :] = eae[:, gs] * dsn + _mm_tn(cg, edy[:, gs])
        zf = z_scr[...]
        dxdt = dxdt_scr[...] + wd * zf
        t3 = _sel_right(xw * zf, seg)
        da_col = da_col + _sel_right(dy * yoff_scr[...], seg) - t3
        aend = acum[CHUNK - 1:CHUNK, :]
        sd = _sel_right(sds_scr[...], seg)[0:1, :] * jnp.exp(aend)
        last = jnp.sum(t3, axis=0, keepdims=True) + sd
        da_col = da_col + jnp.where(_iota((CHUNK, N_HEADS), 0) == CHUNK - 1, last, 0.0)
        r64 = _iota((CHUNK, CHUNK), 0)
        c64 = _iota((CHUNK, CHUNK), 1)
        ddta1 = _sel_left((c64 >= r64).astype(_MXU), da_col)
        ddta2 = _sel_right(da_row, (r64 >= c64).astype(_MXU))
        ddta_ref[0] = a * ddta1 + _sel_right(dxdt * x, seg)
        ddtb_ref[0] = a_t * ddta2
        ga1_ref[...] += jnp.sum(dt * ddta1, axis=0, keepdims=True)
        ga2_ref[...] += jnp.sum(dt_t * ddta2, axis=1, keepdims=True)
        dx_ref[:, 0:SSD_W] = dxdt * dtx + d_ref[...] * dy
        gd_ref[...] += jnp.sum(dy * x, axis=0, keepdims=True)

    rev = lambda c: (nc - 1 - c, 0)
    rev3 = lambda c: (nc - 1 - c, 0, 0)
    return pl.pallas_call(
        body, name="ssd_bwd", grid=(nc,),
        in_specs=[pl.BlockSpec((CHUNK, XBC_W), rev),
                  pl.BlockSpec((1, CHUNK, N_HEADS), rev3),
                  pl.BlockSpec((1, N_HEADS, CHUNK), rev3),
                  pl.BlockSpec((1, N_HEADS), lambda c: (0, 0)),
                  pl.BlockSpec((N_HEADS, 1), lambda c: (0, 0)),
                  pl.BlockSpec((1, SSD_W), lambda c: (0, 0)),
                  pl.BlockSpec((1, N_GROUPS * N_STATE, gw), rev3),
                  pl.BlockSpec((CHUNK, SSD_W), rev)],
        out_specs=[pl.BlockSpec((CHUNK, XBC_W), rev),
                   pl.BlockSpec((1, CHUNK, N_HEADS), rev3),
                   pl.BlockSpec((1, N_HEADS, CHUNK), rev3),
                   pl.BlockSpec((1, N_HEADS), lambda c: (0, 0)),
                   pl.BlockSpec((N_HEADS, 1), lambda c: (0, 0)),
                   pl.BlockSpec((1, SSD_W), lambda c: (0, 0))],
        out_shape=[_SDS((lp, XBC_W), F32), _SDS((nc, CHUNK, N_HEADS), F32), _SDS((nc, N_HEADS, CHUNK), F32),
                   _SDS((1, N_HEADS), F32), _SDS((N_HEADS, 1), F32), _SDS((1, SSD_W), F32)],
        scratch_shapes=[pltpu.VMEM((N_GROUPS * N_STATE, gw), F32), pltpu.VMEM((CHUNK, SSD_W), F32),
                        pltpu.VMEM((CHUNK, SSD_W), F32), pltpu.VMEM((CHUNK, SSD_W), F32),
                        pltpu.VMEM((8, SSD_W), F32)],
        compiler_params=_params(("arbitrary",)),
    )(xbc, dt_c, dt_tc, a, a_t, dskip_x, states, d_y)


def _gated_norm(o, gate, w):
    sg = _sigmoid(gate)
    p = o * (gate * sg)
    rs = lax.rsqrt(jnp.mean(p * p, axis=-1, keepdims=True) + EPS)
    n = p * rs
    return sg, rs, n, n * w


def _tail_fwd(o_sb, o_ssd, proj, h0, target, w_out, sb_w, ssd_w, fin_w):
    lp = o_sb.shape[0]
    nb = lp // TM
    row = lambda i: (i, 0)
    one = lambda i: (0, 0)

    def body(osb_ref, gate_ref, ossd_ref, z_ref, h0_ref, tgt_ref, wo_ref, sbw_ref, ssdw_ref, fw_ref,
             dh1_ref, loss_ref, gfw_ref):
        i = pl.program_id(0)

        @pl.when(i == 0)
        def _():
            loss_ref[...] = jnp.zeros_like(loss_ref)
            gfw_ref[...] = jnp.zeros_like(gfw_ref)

        y1 = _gated_norm(osb_ref[...], gate_ref[...], sbw_ref[...])[3]
        y2 = _gated_norm(ossd_ref[...], z_ref[...], ssdw_ref[...])[3]
        h1 = (h0_ref[...] + _mm(y1, wo_ref[0:SB_W, :])) + _mm(y2, wo_ref[SB_W:SB_W + SSD_W, :])
        rs1 = lax.rsqrt(jnp.mean(h1 * h1, axis=-1, keepdims=True) + EPS)
        n1 = h1 * rs1
        fw = fw_ref[...]
        diff = jnp.where(i > 0, n1 * fw - tgt_ref[...], 0.0)
        loss_ref[...] += jnp.sum(diff * diff, axis=0, keepdims=True)
        d_out = diff * (1.0 / D_MODEL)
        gfw_ref[...] += jnp.sum(d_out * n1, axis=0, keepdims=True)
        g = d_out * fw
        dh1_ref[...] = rs1 * (g - n1 * jnp.mean(g * n1, axis=-1, keepdims=True))

    return pl.pallas_call(
        body, name="tail_fwd", grid=(nb,),
        in_specs=[pl.BlockSpec((TM, SB_W), row),
                  pl.BlockSpec((TM, SB_W), lambda i: (i, (COL_GATE - QKV_W) // SB_W)),
                  pl.BlockSpec((TM, SSD_W), row),
                  pl.BlockSpec((TM, SSD_W), lambda i: (i, (COL_Z - QKV_W) // SSD_W)),
                  pl.BlockSpec((TM, D_MODEL), row),
                  pl.BlockSpec((TM, D_MODEL), lambda i: (jnp.maximum(i - 1, 0), 0)),
                  pl.BlockSpec(memory_space=_VMEM),
                  pl.BlockSpec((1, SB_W), one), pl.BlockSpec((1, SSD_W), one), pl.BlockSpec((1, D_MODEL), one)],
        out_specs=[pl.BlockSpec((TM, D_MODEL), row), pl.BlockSpec((1, D_MODEL), one), pl.BlockSpec((1, D_MODEL), one)],
        out_shape=[_SDS((lp, D_MODEL), F32), _SDS((1, D_MODEL), F32), _SDS((1, D_MODEL), F32)],
        compiler_params=_params(("arbitrary",), 40),
    )(o_sb, proj, o_ssd, proj, h0, target, w_out, sb_w, ssd_w, fin_w)


def _gated_norm_bwd(o, gate, w, dy):
    sg, rs, n, _ = _gated_norm(o, gate, w)
    gw = jnp.sum(dy * n, axis=0, keepdims=True)
    dn = dy * w
    dp = rs * (dn - n * jnp.mean(dn * n, axis=-1, keepdims=True))
    d_o = dp * (gate * sg)
    d_gate = dp * o * (sg * (1.0 + gate * (1.0 - sg)))
    return d_o, d_gate, gw, n * w


def _tail_bwd(o_sb, o_ssd, proj, d_h1, w_out, sb_w, ssd_w):
    lp = o_sb.shape[0]
    tm = 272 if lp % 272 == 0 else TM
    nb = lp // tm
    row = lambda i, t: (i, 0)
    one = lambda i, t: (0, 0)

    def body(osb_ref, gate_ref, ossd_ref, z_ref, dh1_ref, wo_ref, sbw_ref, ssdw_ref,
             dosb_ref, dossd_ref, dproj_ref, gwo_ref, gsb_ref, gssd_ref):
        i = pl.program_id(0)
        t = pl.program_id(1)

        @pl.when(jnp.logical_and(i == 0, t == 0))
        def _():
            gwo_ref[...] = jnp.zeros_like(gwo_ref)
            gsb_ref[...] = jnp.zeros_like(gsb_ref)
            gssd_ref[...] = jnp.zeros_like(gssd_ref)

        dh1 = dh1_ref[...].astype(_MXU)

        def half(o_ref, g_ref, w_ref, do_ref, gn_ref, r0):
            dy = lax.dot_general(dh1, wo_ref[r0:r0 + SB_W, :], _NT, preferred_element_type=F32)
            d_o, d_g, gw, y = _gated_norm_bwd(o_ref[...], g_ref[...], w_ref[...], dy)
            do_ref[...] = d_o
            dproj_ref[...] = d_g.astype(_MXU)
            gn_ref[...] += gw
            gwo_ref[r0:r0 + SB_W, :] += lax.dot_general(y.astype(_MXU), dh1, _TN, preferred_element_type=F32)

        @pl.when(t == 0)
        def _():
            half(osb_ref, gate_ref, sbw_ref, dosb_ref, gsb_ref, 0)

        @pl.when(t == 1)
        def _():
            half(ossd_ref, z_ref, ssdw_ref, dossd_ref, gssd_ref, SB_W)

    tile = pl.BlockSpec((tm, SB_W), row)
    return pl.pallas_call(
        body, name="tail_bwd", grid=(nb, 2),
        in_specs=[tile, pl.BlockSpec((tm, SB_W), lambda i, t: (i, (COL_GATE - QKV_W) // SB_W)),
                  tile, pl.BlockSpec((tm, SSD_W), lambda i, t: (i, (COL_Z - QKV_W) // SSD_W)),
                  tile, pl.BlockSpec(memory_space=_VMEM),
                  pl.BlockSpec((1, SB_W), one), pl.BlockSpec((1, SSD_W), one)],
        out_specs=[tile, tile, pl.BlockSpec((tm, SB_W), lambda i, t: (i, COL_GATE // SB_W + t)),
                   pl.BlockSpec((SB_W + SSD_W, D_MODEL), one), pl.BlockSpec((1, SB_W), one), pl.BlockSpec((1, SSD_W), one)],
        out_shape=[_SDS((lp, SB_W), F32), _SDS((lp, SSD_W), F32), _SDS((lp, W_ALL), _MXU),
                   _SDS((SB_W + SSD_W, D_MODEL), F32), _SDS((1, SB_W), F32), _SDS((1, SSD_W), F32)],
        compiler_params=_params(("arbitrary", "arbitrary"), 48),
    )(o_sb, proj, o_ssd, proj, d_h1, w_out, sb_w, ssd_w)


def _d_u(d_proj, w_t, send=(), dests=()):
    lp = d_proj.shape[0]
    tk = 512
    steps = N_MAIN // tk
    n = len(send)
    host_in, host_out, host_shapes, host_sems = _host_specs(send)

    def body(dp_ref, w_ref, dpdt_ref, wdt_ref, *rest):
        srcs, o_ref, dsts, sems = rest[:n], rest[n], rest[n + 1:2 * n + 1], rest[2 * n + 1:]
        j = pl.program_id(0)

        @pl.when(j == 0)
        def _():
            if n:
                _start_copies(_slab_copies(srcs, dsts, dests, *sems))
            o_ref[...] = jnp.dot(dpdt_ref[...], wdt_ref[...], preferred_element_type=F32)

        o_ref[...] += jnp.dot(dp_ref[...], w_ref[...], preferred_element_type=F32)

        if n:
            @pl.when(j == steps - 1)
            def _():
                _finish_copies(_slab_copies(srcs, dsts, dests, *sems))

    res = pl.pallas_call(
        body, name="d_u", grid=(steps,),
        in_specs=[pl.BlockSpec((lp, tk), lambda j: (0, j)),
                  pl.BlockSpec((tk, D_MODEL), lambda j: (j, 0)),
                  pl.BlockSpec((lp, 128), lambda j: (0, N_MAIN // 128)),
                  pl.BlockSpec((128, D_MODEL), lambda j: (N_MAIN // 128, 0))] + host_in,
        out_specs=[pl.BlockSpec((lp, D_MODEL), lambda j: (0, 0))] + host_out,
        out_shape=[_SDS((lp, D_MODEL), F32)] + host_shapes,
        scratch_shapes=host_sems,
        compiler_params=_params(("arbitrary",), 48),
    )(d_proj, w_t, d_proj, w_t, *send)
    return res[0], list(res[1:])


def _norm_bwd(du_all, h0, d_h1, norm_w):
    lp = h0.shape[0]
    nb = lp // TM
    seq = lp - OFF
    row = lambda i: (i, 0)
    one = lambda i: (0, 0)

    def body(du_ref, h0_ref, dh1_ref, nw_ref, gx_ref, gmeta_ref, gnw_ref):
        i = pl.program_id(0)

        @pl.when(i == 0)
        def _():
            gnw_ref[...] = jnp.zeros_like(gnw_ref)

        du = du_ref[...]
        h = h0_ref[...]
        rs = lax.rsqrt(jnp.mean(h * h, axis=-1, keepdims=True) + EPS)
        n0 = h * rs
        gnw_ref[...] += jnp.sum(du * n0, axis=0, keepdims=True)
        g = du * nw_ref[...]
        dh0 = dh1_ref[...] + rs * (g - n0 * jnp.mean(g * n0, axis=-1, keepdims=True))

        @pl.when(i == 0)
        def _():
            gmeta_ref[...] = dh0[PAD:PAD + N_META, :]

        @pl.when(i > 0)
        def _():
            gx_ref[...] = dh0

    tile = pl.BlockSpec((TM, D_MODEL), row)
    return pl.pallas_call(
        body, name="norm_bwd", grid=(nb,),
        in_specs=[tile, tile, tile, pl.BlockSpec((1, D_MODEL), one)],
        out_specs=[pl.BlockSpec((TM, D_MODEL), lambda i: (jnp.maximum(i - 1, 0), 0)),
                   pl.BlockSpec((N_META, D_MODEL), one), pl.BlockSpec((1, D_MODEL), one)],
        out_shape=[_SDS((seq, D_MODEL), F32), _SDS((N_META, D_MODEL), F32), _SDS((1, D_MODEL), F32)],
        compiler_params=_params(("arbitrary",)),
    )(du_all, h0, d_h1, norm_w)


def _grad_w_windows(u_t, d_proj, first, count, name):
    lp = d_proj.shape[0]
    hw = WIN_W // 2
    steps = 2 * count

    def body(ut_ref, dp_hbm, o_ref, buf, sems):
        s = pl.program_id(0)
        slot = s % 2

        def fetch(step, sl):
            start = pl.multiple_of((first + step // 2) * WIN_STEP + (step % 2) * hw, 128)
            return pltpu.make_async_copy(dp_hbm.at[:, pl.ds(start, hw)], buf.at[sl], sems.at[sl])

        @pl.when(s == 0)
        def _():
            fetch(0, 0).start()

        @pl.when(s + 1 < steps)
        def _():
            fetch(s + 1, 1 - slot).start()

        fetch(s, slot).wait()
        o_ref[0] = jnp.dot(ut_ref[...], buf[slot], preferred_element_type=F32).astype(o_ref.dtype)

    return pl.pallas_call(
        body, name=name, grid=(steps,),
        in_specs=[pl.BlockSpec((D_MODEL, lp), lambda s: (0, 0)), pl.BlockSpec(memory_space=pl.ANY)],
        out_specs=pl.BlockSpec((1, D_MODEL, hw), lambda s: (s // 2, 0, s % 2)),
        out_shape=_SDS((count, D_MODEL, WIN_W), _MXU),
        scratch_shapes=[pltpu.VMEM((2, lp, hw), _MXU), pltpu.SemaphoreType.DMA((2,))],
        compiler_params=_params(("arbitrary",), 40),
    )(u_t, d_proj)


def _device_grads(x2d, target2d, meta_full, norm_w, w_t, conv_w, conv_b, dt_bias, a_log, d_skip,
                  sb_w, ssd_w, w_out, fin_w, exchange=None):
    lp = x2d.shape[0] + OFF
    nc = lp // CHUNK
    h0, u, u_t = _prep(x2d, meta_full, norm_w)
    qkv, proj, dt_raw = _inproj(u, w_t)
    o_sb, o_lo = _sb_fwd(qkv)
    dt_bias128 = jnp.pad(dt_bias, ((0, 0), (0, 128 - N_HEADS)))
    xbc, dt128 = _conv_fwd(proj, dt_raw, conv_w, conv_b, dt_bias128)
    dt_c = dt128[:, :N_HEADS].reshape(nc, CHUNK, N_HEADS)
    dt_tc = jnp.swapaxes(dt_c, 1, 2)
    a = -jnp.exp(a_log)
    a_t = a.reshape(N_HEADS, 1)
    dskip_x = jnp.repeat(d_skip, HEAD, axis=1)
    o_ssd, states = _ssd_fwd(xbc, dt_c, dt_tc, a, a_t, dskip_x)
    d_h1, sq_err, g_fin = _tail_fwd(o_sb, o_ssd, proj, h0, target2d, w_out, sb_w, ssd_w, fin_w)

    d_osb, d_ossd, d_proj, g_wout, g_sb, g_ssd = _tail_bwd(o_sb, o_ssd, proj, d_h1, w_out, sb_w, ssd_w)
    d_xbc_act, ddt_a, ddt_b, ga1, ga2, gd = _ssd_bwd(xbc, dt_c, dt_tc, a, a_t, dskip_x, states, d_ossd)
    d_dt = (ddt_a + jnp.swapaxes(ddt_b, 1, 2)).reshape(lp, N_HEADS)
    d_dt128 = jnp.pad(d_dt, ((0, 0), (0, 128 - N_HEADS)))
    d_proj, g_convw, g_convb, g_dtb128 = _conv_bwd(proj, dt_raw, conv_w, conv_b, dt_bias128, d_xbc_act, d_dt128, d_proj)
    send_e, dests_e = ((), ()) if exchange is None else exchange["early"](g_wout)
    d_proj, arrived_e = _sb_bwd(qkv, o_sb, o_lo, d_osb, d_proj, send_e, dests_e)
    g_win = _grad_w_windows(u_t, d_proj, 0, N_CHIPS, "grad_w_in")
    send_l, dests_l = ((), ()) if exchange is None else exchange["late"](g_win)
    d_u, arrived_l = _d_u(d_proj, w_t, send_l, dests_l)
    send, arrived = tuple(send_e) + tuple(send_l), tuple(arrived_e) + tuple(arrived_l)
    g_x, g_meta, g_nw = _norm_bwd(d_u, h0, d_h1, norm_w)
    g_alog = (ga1 + ga2.reshape(1, N_HEADS)) * a
    g_dskip = gd.reshape(N_HEADS, HEAD).sum(axis=1).reshape(1, N_HEADS)
    grads = dict(meta_tokens=g_meta, norm_w=g_nw, w_in=g_win, conv_w=g_convw, conv_b=g_convb,
                 dt_bias=g_dtb128[:, :N_HEADS], a_log=g_alog, d_skip=g_dskip, sb_norm_w=g_sb, ssd_norm_w=g_ssd,
                 w_out=g_wout, final_norm_w=g_fin, sent=send, arrived=arrived)
    return sq_err, g_x, grads


_MESH = pl.DeviceIdType.MESH
_ANY = pl.BlockSpec(memory_space=pl.ANY)


def _place():
    return lax.axis_index("x"), lax.axis_index("y"), lax.axis_index("c")


def _other_chips(x, y):
    return ((1 - x, y), (x, 1 - y), (1 - x, 1 - y))


def _gather_shards(arrays, n_big):
    n = len(arrays)

    def body(*refs):
        srcs, dsts = refs[:n], refs[n:2 * n]
        send_sems, recv_sems, fwd_send, fwd_recv = refs[2 * n:]
        x, y, c = _place()
        mine = 2 * x + y
        chips = _other_chips(x, y)

        def window(a):
            half = arrays[a].shape[1] // 2
            return pl.ds(pl.multiple_of(c * half, 128), half)

        first = []
        for a in range(n):
            for k, (px, py) in enumerate(chips):
                if a < n_big:
                    src, dst = srcs[a].at[:, window(a)], dsts[a].at[mine, :, window(a)]
                else:
                    src, dst = srcs[a], dsts[a].at[mine]
                cp = pltpu.make_async_remote_copy(
                    src_ref=src, dst_ref=dst, send_sem=send_sems.at[a * 3 + k], recv_sem=recv_sems.at[a * 3 + k],
                    device_id=(px, py, c), device_id_type=_MESH)
                cp.start()
                first.append(cp)
        passed = []
        for a in range(n):
            for k, (px, py) in enumerate(chips):
                first[a * 3 + k].wait_recv()
                if a < n_big:
                    landed = dsts[a].at[2 * px + py, :, window(a)]
                    cp = pltpu.make_async_remote_copy(
                        src_ref=landed, dst_ref=landed, send_sem=fwd_send.at[a * 3 + k], recv_sem=fwd_recv.at[a * 3 + k],
                        device_id=(x, y, 1 - c), device_id_type=_MESH)
                    cp.start()
                    passed.append(cp)
        for cp in passed:
            cp.wait_recv()
        for cp in first + passed:
            cp.wait_send()

    got = pl.pallas_call(
        body, name="gather_shards",
        in_specs=[_ANY] * n, out_specs=[_ANY] * n,
        out_shape=[_SDS((N_CHIPS,) + a.shape, a.dtype) for a in arrays],
        scratch_shapes=[pltpu.SemaphoreType.DMA((3 * n,)), pltpu.SemaphoreType.DMA((3 * n,)),
                        pltpu.SemaphoreType.DMA((3 * n_big,)), pltpu.SemaphoreType.DMA((3 * n_big,))],
    )(*arrays)
    mine = 2 * lax.axis_index("x") + lax.axis_index("y")
    return [lax.dynamic_update_slice(g, a[None], (mine,) + (0,) * a.ndim) for g, a in zip(got, arrays)]


def _slab_copies(srcs, dsts, dests, send_sems, recv_sems):
    x, y, c = _place()
    mine = 2 * x + y
    copies = []
    for a in range(len(srcs)):
        lo, hi = dests[a]
        receives = jnp.logical_and(mine >= lo, mine < hi)
        for k, (px, py) in enumerate(_other_chips(x, y)):
            target = 2 * px + py
            cp = pltpu.make_async_remote_copy(
                src_ref=srcs[a].at[jnp.clip(target - lo, 0, hi - lo - 1)], dst_ref=dsts[a].at[mine],
                send_sem=send_sems.at[a * 3 + k], recv_sem=recv_sems.at[a * 3 + k],
                device_id=(px, py, c), device_id_type=_MESH)
            copies.append((cp, jnp.logical_and(target >= lo, target < hi), receives))
    return copies


def _start_copies(copies):
    for cp, sends, _ in copies:
        pl.when(sends)(cp.start)


def _finish_copies(copies):
    for cp, _, receives in copies:
        pl.when(receives)(cp.wait_recv)
    for cp, sends, _ in copies:
        pl.when(sends)(cp.wait_send)


def _host_specs(send):
    n = len(send)
    hbm = [pl.BlockSpec(memory_space=pl.ANY)] * n
    shapes = [_SDS((N_CHIPS,) + a.shape[1:], a.dtype) for a in send]
    sems = [pltpu.SemaphoreType.DMA((3 * n,)), pltpu.SemaphoreType.DMA((3 * n,))] if n else []
    return hbm, hbm, shapes, sems


def _swap_halves(arrays, name):
    n = len(arrays)

    def body(*refs):
        srcs, dsts = refs[:n], refs[n:2 * n]
        send_sems, recv_sems = refs[2 * n:]
        x, y, c = _place()
        copies = []
        for a in range(n):
            half = arrays[a].shape[1] // 2
            cp = pltpu.make_async_remote_copy(
                src_ref=srcs[a].at[:, pl.ds(pl.multiple_of((1 - c) * half, 16), half)], dst_ref=dsts[a],
                send_sem=send_sems.at[a], recv_sem=recv_sems.at[a],
                device_id=(x, y, 1 - c), device_id_type=_MESH)
            cp.start()
            copies.append(cp)
        for cp in copies:
            cp.wait_recv()
        for cp in copies:
            cp.wait_send()

    return pl.pallas_call(
        body, name=name,
        in_specs=[_ANY] * n, out_specs=[_ANY] * n,
        out_shape=[_SDS((a.shape[0], a.shape[1] // 2, a.shape[2]), a.dtype) for a in arrays],
        scratch_shapes=[pltpu.SemaphoreType.DMA((n,)), pltpu.SemaphoreType.DMA((n,))],
    )(*arrays)


def _join_halves(arrays, by_cols):
    n = len(arrays)

    def body(*refs):
        dsts = refs[n:2 * n]
        send_sems, recv_sems = refs[2 * n:]
        x, y, c = _place()
        copies = []
        for a in range(n):
            if by_cols[a]:
                half = arrays[a].shape[1] // 2
                mine = dsts[a].at[:, pl.ds(pl.multiple_of(c * half, 128), half)]
            else:
                half = arrays[a].shape[0] // 2
                mine = dsts[a].at[pl.ds(pl.multiple_of(c * half, 16), half)]
            cp = pltpu.make_async_remote_copy(
                src_ref=mine, dst_ref=mine, send_sem=send_sems.at[a], recv_sem=recv_sems.at[a],
                device_id=(x, y, 1 - c), device_id_type=_MESH)
            cp.start()
            copies.append(cp)
        for cp in copies:
            cp.wait_recv()
        for cp in copies:
            cp.wait_send()

    return pl.pallas_call(
        body, name="join_halves",
        in_specs=[_ANY] * n, out_specs=[_ANY] * n,
        out_shape=[_SDS(a.shape, a.dtype) for a in arrays],
        input_output_aliases={a: a for a in range(n)},
        scratch_shapes=[pltpu.SemaphoreType.DMA((n,)), pltpu.SemaphoreType.DMA((n,))],
    )(*arrays)


N_DEV = 8
SMALL_ROWS = 32
SMALL_COLS = XBC_W


def _gather_small(packed):
    def body(src_ref, dst_ref, send_sems, recv_sems, local_sem):
        x, y, c = _place()
        me = 4 * x + 2 * y + c
        own = pltpu.make_async_copy(src_ref, dst_ref.at[me], local_sem)
        own.start()
        copies = []
        for k in range(1, N_DEV):
            bx, by, bc = (k >> 2) & 1, (k >> 1) & 1, k & 1
            peer = (x + bx - 2 * x * bx, y + by - 2 * y * by, c + bc - 2 * c * bc)
            cp = pltpu.make_async_remote_copy(
                src_ref=src_ref, dst_ref=dst_ref.at[me], send_sem=send_sems.at[k - 1], recv_sem=recv_sems.at[k - 1],
                device_id=peer, device_id_type=_MESH)
            cp.start()
            copies.append(cp)
        for cp in copies:
            cp.wait_recv()
        for cp in copies:
            cp.wait_send()
        own.wait()

    return pl.pallas_call(
        body, name="gather_small",
        in_specs=[pl.BlockSpec(memory_space=_VMEM)], out_specs=pl.BlockSpec(memory_space=_VMEM),
        out_shape=_SDS((N_DEV, SMALL_ROWS, SMALL_COLS), F32),
        scratch_shapes=[pltpu.SemaphoreType.DMA((N_DEV - 1,)), pltpu.SemaphoreType.DMA((N_DEV - 1,)),
                        pltpu.SemaphoreType.DMA],
    )(packed)


def _adamw(w, g, m, v):
    m = ADAM_B1 * m + (1.0 - ADAM_B1) * g
    v = ADAM_B2 * v + (1.0 - ADAM_B2) * (g * g)
    m_hat = m / (1.0 - ADAM_B1 ** ADAM_STEP)
    v_hat = v / (1.0 - ADAM_B2 ** ADAM_STEP)
    delta = -ADAM_LR * (m_hat / (jnp.sqrt(v_hat) + ADAM_EPS) + ADAM_WD * w)
    return delta, m, v


def _sum_slabs(slabs, core, name, transposed=False):
    _, h, c = slabs.shape
    tr = 128
    nblk = h // tr

    def body(core_ref, s_ref, o_ref):
        tot = ((s_ref[0].astype(F32) + s_ref[1].astype(F32)) + s_ref[2].astype(F32)) + s_ref[3].astype(F32)
        o_ref[...] = tot.T if transposed else tot

    if transposed:
        out_spec = pl.BlockSpec((c, tr), lambda i, core_ref: (0, core_ref[0] * nblk + i))
        out_shape = _SDS((c, 2 * h), F32)
    else:
        out_spec = pl.BlockSpec((tr, c), lambda i, core_ref: (core_ref[0] * nblk + i, 0))
        out_shape = _SDS((2 * h, c), F32)
    grid_spec = pltpu.PrefetchScalarGridSpec(
        num_scalar_prefetch=1, grid=(nblk,),
        in_specs=[pl.BlockSpec((N_CHIPS, tr, c), lambda i, core_ref: (0, i, 0))],
        out_specs=out_spec)
    return pl.pallas_call(
        body, name=name, grid_spec=grid_spec, out_shape=out_shape,
        compiler_params=_params(("arbitrary",)),
    )(core, slabs)


def _add_halves(own, recv, core, name):
    ns, r, c = own.shape
    half = r // 2
    tr = 128
    nblk = half // tr

    def body(core_ref, a_ref, b_ref, o_ref):
        o_ref[...] = (a_ref[...].astype(F32) + b_ref[...].astype(F32)).astype(o_ref.dtype)

    grid_spec = pltpu.PrefetchScalarGridSpec(
        num_scalar_prefetch=1, grid=(nblk,),
        in_specs=[pl.BlockSpec((ns, tr, c), lambda i, core_ref: (0, core_ref[0] * nblk + i, 0)),
                  pl.BlockSpec((ns, tr, c), lambda i, core_ref: (0, i, 0))],
        out_specs=pl.BlockSpec((ns, tr, c), lambda i, core_ref: (0, i, 0)))
    return pl.pallas_call(
        body, name=name, grid_spec=grid_spec, out_shape=_SDS((ns, half, c), own.dtype),
        compiler_params=_params(("arbitrary",)),
    )(core, own, recv)


def _update_big(w, m, v, g, name):
    r, c = w.shape

    def body(w_ref, m_ref, v_ref, g_ref, d_ref, mo_ref, vo_ref):
        delta, m_new, v_new = _adamw(w_ref[...], g_ref[...], m_ref[...], v_ref[...])
        d_ref[...] = delta
        mo_ref[...] = m_new
        vo_ref[...] = v_new

    if r % 128 == 0:
        steps, spec = r // 128, pl.BlockSpec((128, c), lambda i: (i, 0))
    else:
        steps, spec = c // 128, pl.BlockSpec((r, 128), lambda i: (0, i))
    return pl.pallas_call(
        body, name=name, grid=(steps,),
        in_specs=[spec] * 4, out_specs=[spec] * 3,
        out_shape=[_SDS((r, c), F32)] * 3,
        compiler_params=_params(("arbitrary",)),
    )(w, m, v, g)


_ROW = dict(norm_w=0, sb_norm_w=1, ssd_norm_w=2, final_norm_w=3, conv_b=4, dt_bias=5, a_log=6, d_skip=7,
            conv_w=8, sq_err=12, meta_tokens=16)
_SMALL = ("meta_tokens", "norm_w", "conv_w", "conv_b", "dt_bias", "a_log", "d_skip", "sb_norm_w", "ssd_norm_w",
          "final_norm_w")


def _pack_small(sq_err, grads):
    def rowpad(a):
        return jnp.pad(a, ((0, 0), (0, SMALL_COLS - a.shape[1])))

    rows = [rowpad(grads[k]) for k in ("norm_w", "sb_norm_w", "ssd_norm_w", "final_norm_w", "conv_b", "dt_bias", "a_log", "d_skip")]
    rows.append(grads["conv_w"])
    rows.append(rowpad(sq_err))
    rows.append(jnp.zeros((3, SMALL_COLS), F32))
    rows.append(rowpad(grads["meta_tokens"]))
    return jnp.concatenate(rows, axis=0)


def _update_small(gathered, ws, ms, vs):
    names = _SMALL
    n = len(names)

    def body(*refs):
        g_ref = refs[0]
        w_refs, m_refs, v_refs = refs[1:1 + n], refs[1 + n:1 + 2 * n], refs[1 + 2 * n:1 + 3 * n]
        outs = refs[1 + 3 * n:]
        loss_ref = outs[0]
        go, do, mo, vo = outs[1:1 + n], outs[1 + n:1 + 2 * n], outs[1 + 2 * n:1 + 3 * n], outs[1 + 3 * n:1 + 4 * n]
        tot = g_ref[0]
        for d in range(1, N_DEV):
            tot = tot + g_ref[d]
        x, y, _ = _place()
        chip = 2 * x + y
        loss_ref[...] = jnp.broadcast_to(
            0.5 * jnp.sum(tot[_ROW["sq_err"]:_ROW["sq_err"] + 1, 0:D_MODEL], axis=1, keepdims=True) / D_MODEL, (1, 128))
        for idx, nm in enumerate(names):
            r0 = _ROW[nm]
            rows, cols = w_refs[idx].shape
            if nm in ("conv_w", "meta_tokens"):
                g = jnp.zeros((rows, cols), F32)
                for j in range(N_CHIPS):
                    g = g + jnp.where(chip == j, tot[r0:r0 + rows, j * cols:(j + 1) * cols], 0.0)
            else:
                g = tot[r0:r0 + rows, 0:cols]
            delta, m_new, v_new = _adamw(w_refs[idx][...], g, m_refs[idx][...], v_refs[idx][...])
            go[idx][...] = g
            do[idx][...] = delta
            mo[idx][...] = m_new
            vo[idx][...] = v_new

    shapes = [_SDS(ws[nm].shape, F32) for nm in names]
    vm = pl.BlockSpec(memory_space=_VMEM)
    res = pl.pallas_call(
        body, name="update_small",
        in_specs=[vm] * (1 + 3 * n), out_specs=[vm] * (1 + 4 * n),
        out_shape=[_SDS((1, 128), F32)] + shapes * 4,
    )(gathered, *[ws[nm] for nm in names], *[ms[nm] for nm in names], *[vs[nm] for nm in names])
    loss = res[0][0, 0]
    g = dict(zip(names, res[1:1 + n]))
    d = dict(zip(names, res[1 + n:1 + 2 * n]))
    m = dict(zip(names, res[1 + 2 * n:1 + 3 * n]))
    v = dict(zip(names, res[1 + 3 * n:1 + 4 * n]))
    return loss, g, d, m, v


_WEIGHTS = ("meta_tokens", "norm_w", "w_in", "conv_w", "conv_b", "dt_bias", "a_log", "d_skip", "sb_norm_w",
            "ssd_norm_w", "w_out", "final_norm_w")


def kernel(x, meta_tokens, norm_w, w_in, conv_w, conv_b, dt_bias, a_log, d_skip, sb_norm_w, ssd_norm_w, w_out, final_norm_w, loss_target, m_meta_tokens, m_norm_w, m_w_in, m_conv_w, m_conv_b, m_dt_bias, m_a_log, m_d_skip, m_sb_norm_w, m_ssd_norm_w, m_w_out, m_final_norm_w, v_meta_tokens, v_norm_w, v_w_in, v_conv_w, v_conv_b, v_dt_bias, v_a_log, v_d_skip, v_sb_norm_w, v_ssd_norm_w, v_w_out, v_final_norm_w):
    given = dict(meta_tokens=meta_tokens, norm_w=norm_w, w_in=w_in, conv_w=conv_w, conv_b=conv_b, dt_bias=dt_bias,
                 a_log=a_log, d_skip=d_skip, sb_norm_w=sb_norm_w, ssd_norm_w=ssd_norm_w, w_out=w_out,
                 final_norm_w=final_norm_w)
    mom = dict(meta_tokens=m_meta_tokens, norm_w=m_norm_w, w_in=m_w_in, conv_w=m_conv_w, conv_b=m_conv_b,
               dt_bias=m_dt_bias, a_log=m_a_log, d_skip=m_d_skip, sb_norm_w=m_sb_norm_w, ssd_norm_w=m_ssd_norm_w,
               w_out=m_w_out, final_norm_w=m_final_norm_w)
    var = dict(meta_tokens=v_meta_tokens, norm_w=v_norm_w, w_in=v_w_in, conv_w=v_conv_w, conv_b=v_conv_b,
               dt_bias=v_dt_bias, a_log=v_a_log, d_skip=v_d_skip, sb_norm_w=v_sb_norm_w, ssd_norm_w=v_ssd_norm_w,
               w_out=v_w_out, final_norm_w=v_final_norm_w)
    seq = x.shape[1]

    def two_d(a):
        return a.reshape((-1, a.shape[-1])) if a.ndim != 2 else a

    def rows_first(a):
        return jnp.transpose(a, (2, 0, 1)).reshape(W_IN_SHARD, D_MODEL)

    def rows_last(a):
        return jnp.transpose(a.reshape(W_IN_SHARD, 1, D_MODEL), (1, 2, 0))

    w_in_t, m_in_t, v_in_t = rows_first(w_in), rows_first(m_w_in), rows_first(v_w_in)

    g_win, g_wout, g_meta, g_cw = _gather_shards(
        [w_in_t.astype(_MXU), w_out[0].astype(_MXU), meta_tokens, conv_w[0]], 2)
    w_t = jnp.pad(g_win.reshape(D_IN, D_MODEL), ((0, W_ALL - D_IN), (0, 0)))
    w_out_full = g_wout.reshape(2 * D_MODEL, D_MODEL)
    meta_full = jnp.swapaxes(g_meta, 0, 1).reshape(N_META, D_MODEL)
    conv_w_full = jnp.swapaxes(g_cw, 0, 1).reshape(4, XBC_W)

    core = lax.axis_index("c").astype(jnp.int32).reshape(1)

    def early(g_wout):
        slab_out = g_wout.reshape(N_CHIPS, W_OUT_SHARD, D_MODEL).astype(_MXU)
        (sib_out,) = _swap_halves([slab_out], "swap_halves_w_out")
        return (_add_halves(slab_out, sib_out, core, "chip_sum_w_out"),), ((0, N_CHIPS),)

    def late(g_win):
        (sib_in,) = _swap_halves([g_win], "swap_halves_w_in")
        return (_add_halves(g_win, sib_in, core, "chip_sum_w_in"),), ((0, N_CHIPS),)

    sq_err, g_x, grads = _device_grads(
        x.reshape(seq, D_MODEL), loss_target.reshape(seq, D_MODEL), meta_full, norm_w, w_t, conv_w_full,
        conv_b, dt_bias, a_log, d_skip, sb_norm_w, ssd_norm_w, w_out_full, final_norm_w.reshape(1, D_MODEL),
        exchange=dict(early=early, late=late))
    chip_out, chip_in = grads["sent"]
    got_out, got_in = grads["arrived"]
    chip = 2 * lax.axis_index("x") + lax.axis_index("y")

    def with_own(got, sent):
        own = lax.dynamic_slice(sent, (chip, 0, 0), (1,) + sent.shape[1:])
        return lax.dynamic_update_slice(got, own, (chip, 0, 0))

    g_in, g_out = _join_halves([_sum_slabs(with_own(got_in, chip_in), core, "sum_w_in", transposed=True),
                                _sum_slabs(with_own(got_out, chip_out), core, "sum_w_out")], (True, False))
    g_in = lax.dynamic_slice(g_in, (4 * chip, 0), (W_IN_SHARD, D_MODEL))
    big = dict(w_in=tuple(rows_last(a) for a in (g_in,) + tuple(_update_big(w_in_t, m_in_t, v_in_t, g_in, "update_w_in"))),
               w_out=(g_out,) + tuple(_update_big(w_out[0], m_w_out[0], v_w_out[0], g_out, "update_w_out")))

    gathered = _gather_small(_pack_small(sq_err, grads))
    loss, sg, sd, sm, sv = _update_small(
        gathered, {k: two_d(given[k]) for k in _SMALL}, {k: two_d(mom[k]) for k in _SMALL},
        {k: two_d(var[k]) for k in _SMALL})

    out = {}
    for idx, group in enumerate((sg, sd, sm, sv)):
        for k in _SMALL:
            out[(idx, k)] = group[k].reshape(given[k].shape)
        for k in ("w_in", "w_out"):
            out[(idx, k)] = big[k][idx].reshape(given[k].shape)
    return (loss, g_x.reshape(x.shape), *[out[(idx, k)] for idx in range(4) for k in _WEIGHTS])
```

```python
import functools
import math

import jax
import jax.numpy as jnp
from jax import lax
from jax.experimental import pallas as pl
from jax.experimental.pallas import tpu as pltpu

F32 = jnp.float32
_MXU = jnp.bfloat16

D_MODEL = 1024
N_META = 16
PAD = 112
OFF = PAD + N_META
TM = 128
CHUNK = 64
SB_W = 1024
SSD_W = 1024
N_HEADS = 16
HEAD = 64
N_GROUPS = 2
N_STATE = 128
XBC_W = SSD_W + 2 * N_GROUPS * N_STATE
N_MAIN = 4 * SB_W + SSD_W + XBC_W
QKV_W = 3 * SB_W
REST_W = N_MAIN - QKV_W
COL_GATE = 3 * SB_W
COL_Z = 4 * SB_W
COL_XBC = 5 * SB_W
D_IN = N_MAIN + N_HEADS
W_ALL = N_MAIN + 128
WIN_STEP = 1664
WIN_W = 1792
EPS = 1e-5
N_CHIPS = 4
W_IN_SHARD = D_IN // N_CHIPS
W_OUT_SHARD = 2 * D_MODEL // N_CHIPS

ADAM_LR = 0.001
ADAM_B1 = 0.9
ADAM_B2 = 0.999
ADAM_EPS = 1e-08
ADAM_WD = 0.01
ADAM_STEP = 10

_SDS = jax.ShapeDtypeStruct
_NT = (((1,), (1,)), ((), ()))
_TN = (((0,), (0,)), ((), ()))
_VMEM = pltpu.VMEM


def _params(sem=None, vmem_mb=None):
    kw = {}
    if sem is not None:
        kw["dimension_semantics"] = sem
    if vmem_mb is not None:
        kw["vmem_limit_bytes"] = vmem_mb * 1024 * 1024
    return pltpu.CompilerParams(**kw)


def _mm(a, b):
    return jnp.dot(a.astype(_MXU), b.astype(_MXU), preferred_element_type=F32)


def _mm_nt(a, b):
    return lax.dot_general(a.astype(_MXU), b.astype(_MXU), _NT, preferred_element_type=F32)


def _mm_tn(a, b):
    return lax.dot_general(a.astype(_MXU), b.astype(_MXU), _TN, preferred_element_type=F32)


def _split(x, parts):
    out = []
    r = x
    for _ in range(parts):
        p = r.astype(_MXU)
        out.append(p)
        r = r - p.astype(F32)
    return out


def _sel_right(x, m01, parts=3):
    acc = None
    for p in _split(x, parts):
        t = jnp.dot(p, m01, preferred_element_type=F32)
        acc = t if acc is None else acc + t
    return acc


def _sel_left(m01, x, parts=3):
    acc = None
    for p in _split(x, parts):
        t = jnp.dot(m01, p, preferred_element_type=F32)
        acc = t if acc is None else acc + t
    return acc


def _iota(shape, axis):
    return lax.broadcasted_iota(jnp.int32, shape, axis)


def _sigmoid(x):
    return 1.0 / (1.0 + jnp.exp(-x))


def _prep(x2d, meta_full, norm_w):
    seq = x2d.shape[0]
    lp = seq + OFF
    nb = lp // TM

    def body(x_ref, meta_ref, w_ref, h0_ref, u_ref, ut_ref):
        i = pl.program_id(0)

        @pl.when(i == 0)
        def _():
            h0_ref[...] = jnp.concatenate([jnp.zeros((PAD, D_MODEL), F32), meta_ref[...]], axis=0)

        @pl.when(i > 0)
        def _():
            h0_ref[...] = x_ref[...]

        h = h0_ref[...]
        rs = lax.rsqrt(jnp.mean(h * h, axis=-1, keepdims=True) + EPS)
        u = (h * rs * w_ref[...]).astype(_MXU)
        u_ref[...] = u
        ut_ref[...] = u.T

    return pl.pallas_call(
        body, name="prep", grid=(nb,),
        in_specs=[pl.BlockSpec((TM, D_MODEL), lambda i: (jnp.maximum(i - 1, 0), 0)),
                  pl.BlockSpec((N_META, D_MODEL), lambda i: (0, 0)),
                  pl.BlockSpec((1, D_MODEL), lambda i: (0, 0))],
        out_specs=[pl.BlockSpec((TM, D_MODEL), lambda i: (i, 0)),
                   pl.BlockSpec((TM, D_MODEL), lambda i: (i, 0)),
                   pl.BlockSpec((D_MODEL, TM), lambda i: (0, i))],
        out_shape=[_SDS((lp, D_MODEL), F32), _SDS((lp, D_MODEL), _MXU), _SDS((D_MODEL, lp), _MXU)],
        compiler_params=_params(("arbitrary",)),
    )(x2d, meta_full, norm_w)


def _inproj(u, w_t):
    lp = u.shape[0]
    tn = 512

    nq = QKV_W // tn

    def body(u_ref, w_ref, wdt_ref, qkv_ref, rest_ref, odt_ref):
        j = pl.program_id(0)
        res = lax.dot_general(u_ref[...], w_ref[...], _NT, preferred_element_type=F32)

        @pl.when(j < nq)
        def _():
            qkv_ref[...] = res.astype(qkv_ref.dtype)

        @pl.when(j >= nq)
        def _():
            rest_ref[...] = res

        @pl.when(j == 0)
        def _():
            odt_ref[...] = lax.dot_general(u_ref[...], wdt_ref[...], _NT, preferred_element_type=F32)

    return pl.pallas_call(
        body, name="inproj", grid=(N_MAIN // tn,),
        in_specs=[pl.BlockSpec((lp, D_MODEL), lambda j: (0, 0)),
                  pl.BlockSpec((tn, D_MODEL), lambda j: (j, 0)),
                  pl.BlockSpec((128, D_MODEL), lambda j: (N_MAIN // 128, 0))],
        out_specs=[pl.BlockSpec((lp, tn), lambda j: (0, jnp.minimum(j, nq - 1))),
                   pl.BlockSpec((lp, tn), lambda j: (0, jnp.maximum(j - nq, 0))),
                   pl.BlockSpec((lp, 128), lambda j: (0, 0))],
        out_shape=[_SDS((lp, QKV_W), _MXU), _SDS((lp, REST_W), F32), _SDS((lp, 128), F32)],
        compiler_params=_params(("arbitrary",), 48),
    )(u, w_t, w_t)


SB_WINDOW = 3
SB_TOP = 16
SB_DEAD = -104.0


def _sb_logs(qh, kwin):
    z = lax.dot_general(qh, kwin, _NT, preferred_element_type=F32)
    e = jnp.exp(-jnp.abs(z))
    l1p = jnp.log(1.0 + e)
    lk_full = -(jnp.maximum(z, 0.0) + l1p)
    ls = jnp.minimum(z, 0.0) - l1p
    return z, e, ls, lk_full


def _blk(a, b):
    return a[:, b * TM:(b + 1) * TM]


def _stacked_sel(blocks, m01, parts):
    n = len(blocks)
    rows = blocks[0].shape[0]
    pieces = [_split(b, parts) for b in blocks]
    stacked = jnp.concatenate([p[k] for k in range(parts) for p in pieces], axis=0)
    res = jnp.dot(stacked, m01, preferred_element_type=F32)
    out = []
    for j in range(n):
        tot = res[j * rows:(j + 1) * rows]
        for k in range(1, parts):
            tot = tot + res[(k * n + j) * rows:(k * n + j + 1) * rows]
        out.append(tot)
    return out


def _sb_weights(ls, lk_full, run, last_mask, upper, n):
    lk = [_blk(lk_full, b) for b in range(n)]
    lk[n - 1] = jnp.where(last_mask, lk[n - 1], 0.0)
    aft = _stacked_sel(lk, upper, 1)
    w = [None] * n
    for b in range(n - 1, -1, -1):
        wb = jnp.exp(_blk(ls, b) + aft[b] + run)
        w[b] = jnp.where(last_mask, wb, 0.0) if b == n - 1 else wb
        run = run + jnp.sum(lk[b], axis=1, keepdims=True)
    return w, run


def _sb_alive(run_scr):
    top = jnp.max(run_scr[:, 0:SB_TOP, :]) > SB_DEAD
    rest = jnp.max(run_scr[:, SB_TOP:, :]) > SB_DEAD
    return top.astype(jnp.int32), rest.astype(jnp.int32)


def _sb_walk(i, key_set, strict, run_scr):
    @pl.when(i >= SB_WINDOW - 1)
    def _():
        key_set(i - (SB_WINDOW - 1), SB_WINDOW, strict, TM)

    start = jnp.where(i >= SB_WINDOW - 1, i - SB_WINDOW, i)

    def cond(c):
        return jnp.logical_and(c[0] >= 0, c[1] + c[2] > 0)

    def step(c):
        kb, _, rest = c
        mask = jnp.logical_or(strict, kb < i)

        @pl.when(rest > 0)
        def _():
            key_set(kb, 1, mask, TM)

        @pl.when(rest == 0)
        def _():
            key_set(kb, 1, mask, SB_TOP)

        return (kb - 1,) + _sb_alive(run_scr)

    lax.while_loop(cond, step, (start,) + _sb_alive(run_scr))


def _sb_fwd(qkv):
    lp = qkv.shape[0]
    nb = lp // TM

    def body(q_ref, k_ref, v_ref, o_ref, olo_ref, acc, run_scr):
        i = pl.program_id(1)
        lane = _iota((TM, TM), 1)
        row = _iota((TM, TM), 0)
        head0 = lane < HEAD
        upper = (row > lane).astype(_MXU)
        strict = lane < row
        q = q_ref[...] * (1.0 / math.sqrt(HEAD))
        qh = (jnp.where(head0, q, 0.0).astype(_MXU), jnp.where(head0, 0.0, q).astype(_MXU))

        def key_set(first, n, last_mask, nrows):
            off = pl.multiple_of(first * TM, TM)
            kwin = k_ref[pl.ds(off, n * TM), :].astype(_MXU)
            vwin = v_ref[pl.ds(off, n * TM), :].astype(_MXU)
            for hh in range(2):
                run = run_scr[hh, 0:nrows, 0:1]
                _, _, ls, lk_full = _sb_logs(qh[hh][0:nrows], kwin)
                w, run = _sb_weights(ls, lk_full, run, last_mask[0:nrows], upper, n)
                pieces = [_split(wb, 2) for wb in w]
                stacked = jnp.concatenate(
                    [jnp.concatenate([p[0] for p in pieces], axis=1), jnp.concatenate([p[1] for p in pieces], axis=1)], axis=0)
                res = jnp.dot(stacked, vwin, preferred_element_type=F32)
                acc[hh, 0:nrows] += res[0:nrows]
                acc[2 + hh, 0:nrows] += res[nrows:2 * nrows]
                run_scr[hh, 0:nrows] = jnp.broadcast_to(run, (nrows, TM))

        acc[...] = jnp.zeros_like(acc)
        run_scr[...] = jnp.zeros_like(run_scr)
        _sb_walk(i, key_set, strict, run_scr)
        o_ref[...] = jnp.where(head0, acc[0], acc[1])
        olo_ref[...] = jnp.where(head0, acc[2], acc[3])

    npair = SB_W // TM
    blk = pl.BlockSpec((TM, TM), lambda p, i: (i, p))
    return pl.pallas_call(
        body, name="sb_fwd", grid=(npair, nb),
        in_specs=[blk,
                  pl.BlockSpec((lp, TM), lambda p, i: (0, npair + p)),
                  pl.BlockSpec((lp, TM), lambda p, i: (0, 2 * npair + p))],
        out_specs=[blk, blk],
        out_shape=[_SDS((lp, SB_W), F32), _SDS((lp, SB_W), F32)],
        scratch_shapes=[pltpu.VMEM((4, TM, TM), F32), pltpu.VMEM((2, TM, TM), F32)],
        compiler_params=_params(("arbitrary", "arbitrary")),
    )(qkv, qkv, qkv)


def _sb_bwd(qkv, o_sb, o_lo, d_o, d_proj, send=(), dests=()):
    lp = qkv.shape[0]
    nb = lp // TM
    npair = SB_W // TM
    scale = 1.0 / math.sqrt(HEAD)
    n = len(send)
    host_in, host_out, host_shapes, host_sems = _host_specs(send)

    def body(q_ref, k_ref, v_ref, o_ref, olo_ref, do_ref, dproj_in, *rest):
        srcs, dproj_ref, dsts = rest[:n], rest[n], rest[n + 1:2 * n + 1]
        dq_all, dk_ref, dv_ref, stage, sems, dq_acc, run_scr, gsum_scr = rest[2 * n + 1:2 * n + 9]
        host_sem_refs = rest[2 * n + 9:]
        p = pl.program_id(0)
        i = pl.program_id(1)

        if n:
            @pl.when(jnp.logical_and(p == 0, i == 0))
            def _():
                _start_copies(_slab_copies(srcs, dsts, dests, *host_sem_refs))

        @pl.when(i == 0)
        def _():
            dk_ref[...] = jnp.zeros_like(dk_ref)
            dv_ref[...] = jnp.zeros_like(dv_ref)

        lane = _iota((TM, TM), 1)
        row = _iota((TM, TM), 0)
        head0 = lane < HEAD
        hmask = (head0, jnp.logical_not(head0))
        upper = (row > lane).astype(_MXU)
        lower_incl = (row >= lane).astype(_MXU)
        strict = lane < row
        q = q_ref[...] * scale
        do = do_ref[...]
        prod = do.astype(_MXU).astype(F32) * (o_ref[...] + olo_ref[...])
        qh = tuple(jnp.where(m, q, 0.0).astype(_MXU) for m in hmask)
        doh = tuple(jnp.where(m, do, 0.0).astype(_MXU) for m in hmask)
        gtot = tuple(jnp.sum(jnp.where(m, prod, 0.0), axis=1, keepdims=True) for m in hmask)

        def key_set(first, n, last_mask, nrows):
            off = pl.multiple_of(first * TM, TM)
            kf = k_ref[pl.ds(off, n * TM), :]
            kwin = kf.astype(_MXU)
            vwin = v_ref[pl.ds(off, n * TM), :].astype(_MXU)
            last_mask = last_mask[0:nrows]
            dk_win = None
            for hh in range(2):
                run = run_scr[hh, 0:nrows, 0:1]
                gsum = gsum_scr[hh, 0:nrows, 0:1]
                z, e, ls, lk_full = _sb_logs(qh[hh][0:nrows], kwin)
                w, run = _sb_weights(ls, lk_full, run, last_mask, upper, n)
                r = 1.0 / (1.0 + e)
                er = e * r
                pos = z >= 0.0
                beta = jnp.where(pos, r, er)
                one_m_beta = jnp.where(pos, er, r)
                dw = lax.dot_general(doh[hh][0:nrows], vwin, _NT, preferred_element_type=F32)
                g = [_blk(dw, b) * w[b] for b in range(n)]
                suffix = _stacked_sel(g, lower_incl, 2)
                dz = [None] * n
                for b in range(n - 1, -1, -1):
                    prefix = gtot[hh][0:nrows] - gsum - suffix[b]
                    d = g[b] * _blk(one_m_beta, b) - _blk(beta, b) * prefix
                    dz[b] = (jnp.where(last_mask, d, 0.0) if b == n - 1 else d).astype(_MXU)
                    gsum = gsum + jnp.sum(g[b], axis=1, keepdims=True)
                dzw = jnp.concatenate(dz, axis=1)
                ww = jnp.concatenate([wb.astype(_MXU) for wb in w], axis=1)
                kh = jnp.where(hmask[hh][0:1, :], kf, 0.0).astype(_MXU)
                dq_acc[0:nrows] += jnp.dot(dzw, kh, preferred_element_type=F32)
                dk_h = lax.dot_general(dzw, qh[hh][0:nrows], _TN, preferred_element_type=F32)
                dv_h = lax.dot_general(ww, doh[hh][0:nrows], _TN, preferred_element_type=F32)
                dk_win = (dk_h, dv_h) if dk_win is None else (dk_win[0] + dk_h, dk_win[1] + dv_h)
                run_scr[hh, 0:nrows] = jnp.broadcast_to(run, (nrows, TM))
                gsum_scr[hh, 0:nrows] = jnp.broadcast_to(gsum, (nrows, TM))
            dk_ref[pl.ds(off, n * TM), :] += dk_win[0]
            dv_ref[pl.ds(off, n * TM), :] += dk_win[1]

        dq_acc[...] = jnp.zeros_like(dq_acc)
        run_scr[...] = jnp.zeros_like(run_scr)
        gsum_scr[...] = jnp.zeros_like(gsum_scr)

        _sb_walk(i, key_set, strict, run_scr)
        dq_all[pl.ds(pl.multiple_of(i * TM, TM), TM), :] = dq_acc[...] * scale

        @pl.when(i == nb - 1)
        def _():
            copies = []
            for s, src in enumerate((dq_all, dk_ref, dv_ref)):
                stage[s] = src[...].astype(_MXU)
                col = pl.multiple_of((s * npair + p) * TM, TM)
                copies.append(pltpu.make_async_copy(stage.at[s], dproj_ref.at[:, pl.ds(col, TM)], sems.at[s]))
                copies[-1].start()
            for cp in copies:
                cp.wait()

        if n:
            @pl.when(jnp.logical_and(p == npair - 1, i == nb - 1))
            def _():
                _finish_copies(_slab_copies(srcs, dsts, dests, *host_sem_refs))

    blk = pl.BlockSpec((TM, TM), lambda p, i: (i, p))
    res = pl.pallas_call(
        body, name="sb_bwd", grid=(npair, nb),
        in_specs=[blk,
                  pl.BlockSpec((lp, TM), lambda p, i: (0, npair + p)),
                  pl.BlockSpec((lp, TM), lambda p, i: (0, 2 * npair + p)),
                  blk, blk, blk, pl.BlockSpec(memory_space=pl.ANY)] + host_in,
        out_specs=[pl.BlockSpec(memory_space=pl.ANY)] + host_out,
        out_shape=[_SDS(d_proj.shape, d_proj.dtype)] + host_shapes,
        input_output_aliases={6: 0},
        scratch_shapes=[pltpu.VMEM((lp, TM), F32), pltpu.VMEM((lp, TM), F32), pltpu.VMEM((lp, TM), F32),
                        pltpu.VMEM((3, lp, TM), _MXU), pltpu.SemaphoreType.DMA((3,)),
                        pltpu.VMEM((TM, TM), F32), pltpu.VMEM((2, TM, TM), F32), pltpu.VMEM((2, TM, TM), F32)] + host_sems,
        compiler_params=_params(("arbitrary", "arbitrary")),
    )(qkv, qkv, qkv, o_sb, o_lo, d_o, d_proj, *send)
    return res[0], list(res[1:])


def _conv_pre(x_ref, w_ref, b_ref, lp):
    n = lp - 8
    w = w_ref[...]
    pre = (x_ref[pl.ds(5, n), :] * w[0:1, :] + x_ref[pl.ds(6, n), :] * w[1:2, :]
           + x_ref[pl.ds(7, n), :] * w[2:3, :] + x_ref[pl.ds(8, n), :] * w[3:4, :]) + b_ref[...]
    live = (_iota((n, 128), 0) + 8) >= PAD
    return pre, live


def _conv_fwd(proj, dt_raw, conv_w, conv_b, dt_bias128):
    lp = proj.shape[0]
    nblk = XBC_W // 128
    c0 = (COL_XBC - QKV_W) // 128

    def body(x_ref, w_ref, b_ref, dtr_ref, dtb_ref, o_ref, dt_ref):
        pre, live = _conv_pre(x_ref, w_ref, b_ref, lp)
        act = pre * _sigmoid(pre)
        o_ref[pl.ds(0, 8), :] = jnp.zeros((8, 128), F32)
        o_ref[pl.ds(8, lp - 8), :] = jnp.where(live, act, 0.0)

        @pl.when(pl.program_id(0) == 0)
        def _():
            s = dtr_ref[...] + dtb_ref[...]
            sp = jnp.maximum(s, 0.0) + jnp.log(1.0 + jnp.exp(-jnp.abs(s)))
            dt_ref[...] = jnp.where(_iota((lp, 128), 0) >= PAD, sp, 0.0)

    return pl.pallas_call(
        body, name="conv_fwd", grid=(nblk,),
        in_specs=[pl.BlockSpec((lp, 128), lambda j: (0, c0 + j)),
                  pl.BlockSpec((4, 128), lambda j: (0, j)),
                  pl.BlockSpec((1, 128), lambda j: (0, j)),
                  pl.BlockSpec((lp, 128), lambda j: (0, 0)),
                  pl.BlockSpec((1, 128), lambda j: (0, 0))],
        out_specs=[pl.BlockSpec((lp, 128), lambda j: (0, j)),
                   pl.BlockSpec((lp, 128), lambda j: (0, 0))],
        out_shape=[_SDS((lp, XBC_W), F32), _SDS((lp, 128), F32)],
        compiler_params=_params(("arbitrary",)),
    )(proj, conv_w, conv_b, dt_raw, dt_bias128)


def _conv_bwd(proj, dt_raw, conv_w, conv_b, dt_bias128, d_xbc, d_dt128, d_proj):
    lp = proj.shape[0]
    nblk = XBC_W // 128
    c0 = COL_XBC // 128
    c0_in = (COL_XBC - QKV_W) // 128
    n = lp - 8
    last = nblk - 1

    def body(x_ref, w_ref, b_ref, dtr_ref, dtb_ref, dy_ref, ddt_ref, dproj_in,
             dx_ref, gw_ref, gb_ref, gdtb_ref, scr):
        j = pl.program_id(0)

        @pl.when(j < nblk)
        def _():
            pre, live = _conv_pre(x_ref, w_ref, b_ref, lp)
            sg = _sigmoid(pre)
            dpre = jnp.where(live, dy_ref[pl.ds(8, n), :] * (sg * (1.0 + pre * (1.0 - sg))), 0.0)
            gb_ref[...] = jnp.sum(dpre, axis=0, keepdims=True)
            gw_ref[...] = jnp.concatenate(
                [jnp.sum(dpre * x_ref[pl.ds(5 + k, n), :], axis=0, keepdims=True) for k in range(4)], axis=0)
            scr[pl.ds(0, 8), :] = jnp.zeros((8, 128), F32)
            scr[pl.ds(8, n), :] = dpre
            scr[pl.ds(lp, 8), :] = jnp.zeros((8, 128), F32)
            w = w_ref[...]
            dx_ref[...] = (scr[pl.ds(0, lp), :] * w[3:4, :] + scr[pl.ds(1, lp), :] * w[2:3, :]
                           + scr[pl.ds(2, lp), :] * w[1:2, :] + scr[pl.ds(3, lp), :] * w[0:1, :]).astype(dx_ref.dtype)

        @pl.when(j == nblk)
        def _():
            s = dtr_ref[...] + dtb_ref[...]
            d = jnp.where(_iota((lp, 128), 0) >= PAD, ddt_ref[...] * _sigmoid(s), 0.0)
            dx_ref[...] = d.astype(dx_ref.dtype)
            gdtb_ref[...] = jnp.sum(d, axis=0, keepdims=True)

    clamp = lambda j: (0, jnp.minimum(j, last))
    full128 = pl.BlockSpec((lp, 128), lambda j: (0, 0))
    return pl.pallas_call(
        body, name="conv_bwd", grid=(nblk + 1,),
        in_specs=[pl.BlockSpec((lp, 128), lambda j: (0, c0_in + jnp.minimum(j, last))),
                  pl.BlockSpec((4, 128), clamp),
                  pl.BlockSpec((1, 128), clamp),
                  full128, pl.BlockSpec((1, 128), lambda j: (0, 0)),
                  pl.BlockSpec((lp, 128), clamp), full128, pl.BlockSpec(memory_space=pl.ANY)],
        out_specs=[pl.BlockSpec((lp, 128), lambda j: (0, c0 + j)), pl.BlockSpec((4, 128), clamp),
                   pl.BlockSpec((1, 128), clamp), pl.BlockSpec((1, 128), lambda j: (0, 0))],
        out_shape=[_SDS(d_proj.shape, d_proj.dtype), _SDS((4, XBC_W), F32), _SDS((1, XBC_W), F32), _SDS((1, 128), F32)],
        input_output_aliases={7: 0},
        scratch_shapes=[pltpu.VMEM((lp + 8, 128), F32)],
        compiler_params=_params(("arbitrary",)),
    )(proj, conv_w, conv_b, dt_raw, dt_bias128, d_xbc, d_dt128, d_proj)


def _ssd_pieces(dt, dt_t, a, a_t):
    r64 = _iota((CHUNK, CHUNK), 0)
    c64 = _iota((CHUNK, CHUNK), 1)
    tril = c64 <= r64
    tril01 = tril.astype(_MXU)
    triu01 = (r64 <= c64).astype(_MXU)
    expand = (lax.shift_right_logical(_iota((N_HEADS, SSD_W), 1), 6) == _iota((N_HEADS, SSD_W), 0)).astype(_MXU)
    acum = _sel_left(tril01, dt * a)
    acum_t = _sel_right(dt_t * a_t, triu01)
    ax = _sel_right(acum, expand)
    dtx = _sel_right(dt, expand)
    return tril, expand, acum, acum_t, ax, dtx


def _seg_matrix():
    return (lax.shift_right_logical(_iota((SSD_W, N_HEADS), 0), 6) == _iota((SSD_W, N_HEADS), 1)).astype(_MXU)


def _head_decay(ax, acum_t, h, tril):
    col = ax[:, h * HEAD:(h + 1) * HEAD]
    rowv = acum_t[h:h + 1, :]
    return jnp.where(tril, jnp.exp(jnp.minimum(col - rowv, 0.0)), 0.0)


def _ssd_fwd(xbc, dt_c, dt_tc, a, a_t, dskip_x):
    lp = xbc.shape[0]
    nc = lp // CHUNK
    gw = SSD_W // N_GROUPS
    hpg = N_HEADS // N_GROUPS

    def body(x_ref, dt_ref, dtt_ref, a_ref, at_ref, d_ref, y_ref, st_ref, state):
        c = pl.program_id(0)

        @pl.when(c == 0)
        def _():
            state[...] = jnp.zeros_like(state)

        st_ref[0] = state[...]
        tril, _, _, acum_t, ax, dtx = _ssd_pieces(dt_ref[0], dtt_ref[0], a_ref[...], at_ref[...])
        x = x_ref[:, 0:SSD_W]
        xdt = x * dtx
        ea = jnp.exp(ax)
        aex = ax[CHUNK - 1:CHUNK, :]
        wd = jnp.exp(aex - ax)
        eae = jnp.exp(aex)
        xw = xdt * wd
        y_ref[...] = x * d_ref[...]
        for g in range(N_GROUPS):
            gs = slice(g * gw, (g + 1) * gw)
            rs = slice(g * N_STATE, (g + 1) * N_STATE)
            bg = x_ref[:, SSD_W + g * N_STATE:SSD_W + (g + 1) * N_STATE]
            cg = x_ref[:, SSD_W + N_GROUPS * N_STATE + g * N_STATE:SSD_W + N_GROUPS * N_STATE + (g + 1) * N_STATE]
            sg = state[rs, :]
            cb = _mm_nt(cg, bg)
            y_ref[:, gs] += _mm(cg, sg) * ea[:, gs]
            for r in range(hpg):
                h = g * hpg + r
                hs = slice(h * HEAD, (h + 1) * HEAD)
                m = cb * _head_decay(ax, acum_t, h, tril)
                y_ref[:, hs] += _mm(m, xdt[:, hs])
            state[rs, :] = sg * eae[:, gs] + _mm_tn(bg, xw[:, gs])

    return pl.pallas_call(
        body, name="ssd_fwd", grid=(nc,),
        in_specs=[pl.BlockSpec((CHUNK, XBC_W), lambda c: (c, 0)),
                  pl.BlockSpec((1, CHUNK, N_HEADS), lambda c: (c, 0, 0)),
                  pl.BlockSpec((1, N_HEADS, CHUNK), lambda c: (c, 0, 0)),
                  pl.BlockSpec((1, N_HEADS), lambda c: (0, 0)),
                  pl.BlockSpec((N_HEADS, 1), lambda c: (0, 0)),
                  pl.BlockSpec((1, SSD_W), lambda c: (0, 0))],
        out_specs=[pl.BlockSpec((CHUNK, SSD_W), lambda c: (c, 0)),
                   pl.BlockSpec((1, N_GROUPS * N_STATE, gw), lambda c: (c, 0, 0))],
        out_shape=[_SDS((lp, SSD_W), F32), _SDS((nc, N_GROUPS * N_STATE, gw), F32)],
        scratch_shapes=[pltpu.VMEM((N_GROUPS * N_STATE, gw), F32)],
        compiler_params=_params(("arbitrary",)),
    )(xbc, dt_c, dt_tc, a, a_t, dskip_x)


def _ssd_bwd(xbc, dt_c, dt_tc, a, a_t, dskip_x, states, d_y):
    lp = xbc.shape[0]
    nc = lp // CHUNK
    gw = SSD_W // N_GROUPS
    hpg = N_HEADS // N_GROUPS

    def body(x_ref, dt_ref, dtt_ref, a_ref, at_ref, d_ref, st_ref, dy_ref,
             dx_ref, ddta_ref, ddtb_ref, ga1_ref, ga2_ref, gd_ref, dstate, dxdt_scr, z_scr, yoff_scr, sds_scr):
        c = pl.program_id(0)

        @pl.when(c == 0)
        def _():
            dstate[...] = jnp.zeros_like(dstate)
            ga1_ref[...] = jnp.zeros_like(ga1_ref)
            ga2_ref[...] = jnp.zeros_like(ga2_ref)
            gd_ref[...] = jnp.zeros_like(gd_ref)

        dt = dt_ref[0]
        dt_t = dtt_ref[0]
        a = a_ref[...]
        a_t = at_ref[...]
        tril, _, acum, acum_t, ax, dtx = _ssd_pieces(dt, dt_t, a, a_t)
        seg = _seg_matrix()
        x = x_ref[:, 0:SSD_W]
        dy = dy_ref[...]
        xdt = x * dtx
        ea = jnp.exp(ax)
        aex = ax[CHUNK - 1:CHUNK, :]
        wd = jnp.exp(aex - ax)
        eae = jnp.exp(aex)
        xw = xdt * wd
        edy = ea * dy
        lane16 = _iota((CHUNK, N_HEADS), 1)
        row16 = _iota((N_HEADS, CHUNK), 0)
        da_col = jnp.zeros((CHUNK, N_HEADS), F32)
        da_row = jnp.zeros((N_HEADS, CHUNK), F32)
        for g in range(N_GROUPS):
            gs = slice(g * gw, (g + 1) * gw)
            rs = slice(g * N_STATE, (g + 1) * N_STATE)
            bcol = slice(SSD_W + g * N_STATE, SSD_W + (g + 1) * N_STATE)
            ccol = slice(SSD_W + N_GROUPS * N_STATE + g * N_STATE, SSD_W + N_GROUPS * N_STATE + (g + 1) * N_STATE)
            bg = x_ref[:, bcol]
            cg = x_ref[:, ccol]
            sg = st_ref[0, rs, :]
            dsn = dstate[rs, :]
            cb = _mm_nt(cg, bg)
            z_scr[:, gs] = _mm(bg, dsn)
            yoff_scr[:, gs] = _mm(cg, sg) * ea[:, gs]
            sds_scr[:, gs] = jnp.broadcast_to(jnp.sum(dsn * sg, axis=0, keepdims=True), (8, gw))
            dcb = jnp.zeros((CHUNK, CHUNK), F32)
            for r in range(hpg):
                h = g * hpg + r
                hs = slice(h * HEAD, (h + 1) * HEAD)
                dec = _head_decay(ax, acum_t, h, tril)
                m = cb * dec
                t1 = _mm_nt(dy[:, hs], xdt[:, hs])
                dcb = dcb + dec * t1
                tm = m * t1
                da_col = da_col + jnp.where(lane16 == h, jnp.sum(tm, axis=1, keepdims=True), 0.0)
                da_row = da_row - jnp.where(row16 == h, jnp.sum(tm, axis=0, keepdims=True), 0.0)
                dxdt_scr[:, hs] = _mm_tn(m, dy[:, hs])
            dx_ref[:, ccol] = _mm(dcb, bg) + _mm_nt(edy[:, gs], sg)
            dx_ref[:, bcol] = _mm_tn(dcb, cg) + _mm_nt(xw[:, gs], dsn)
            dstate[rs, :] = eae[:, gs] * dsn + _mm_tn(cg, edy[:, gs])
        zf = z_scr[...]
        dxdt = dxdt_scr[...] + wd * zf
        t3 = _sel_right(xw * zf, seg)
        da_col = da_col + _sel_right(dy * yoff_scr[...], seg) - t3
        aend = acum[CHUNK - 1:CHUNK, :]
        sd = _sel_right(sds_scr[...], seg)[0:1, :] * jnp.exp(aend)
        last = jnp.sum(t3, axis=0, keepdims=True) + sd
        da_col = da_col + jnp.where(_iota((CHUNK, N_HEADS), 0) == CHUNK - 1, last, 0.0)
        r64 = _iota((CHUNK, CHUNK), 0)
        c64 = _iota((CHUNK, CHUNK), 1)
        ddta1 = _sel_left((c64 >= r64).astype(_MXU), da_col)
        ddta2 = _sel_right(da_row, (r64 >= c64).astype(_MXU))
        ddta_ref[0] = a * ddta1 + _sel_right(dxdt * x, seg)
        ddtb_ref[0] = a_t * ddta2
        ga1_ref[...] += jnp.sum(dt * ddta1, axis=0, keepdims=True)
        ga2_ref[...] += jnp.sum(dt_t * ddta2, axis=1, keepdims=True)
        dx_ref[:, 0:SSD_W] = dxdt * dtx + d_ref[...] * dy
        gd_ref[...] += jnp.sum(dy * x, axis=0, keepdims=True)

    rev = lambda c: (nc - 1 - c, 0)
    rev3 = lambda c: (nc - 1 - c, 0, 0)
    return pl.pallas_call(
        body, name="ssd_bwd", grid=(nc,),
        in_specs=[pl.BlockSpec((CHUNK, XBC_W), rev),
                  pl.BlockSpec((1, CHUNK, N_HEADS), rev3),
                  pl.BlockSpec((1, N_HEADS, CHUNK), rev3),
                  pl.BlockSpec((1, N_HEADS), lambda c: (0, 0)),
                  pl.BlockSpec((N_HEADS, 1), lambda c: (0, 0)),
                  pl.BlockSpec((1, SSD_W), lambda c: (0, 0)),
                  pl.BlockSpec((1, N_GROUPS * N_STATE, gw), rev3),
                  pl.BlockSpec((CHUNK, SSD_W), rev)],
        out_specs=[pl.BlockSpec((CHUNK, XBC_W), rev),
                   pl.BlockSpec((1, CHUNK, N_HEADS), rev3),
                   pl.BlockSpec((1, N_HEADS, CHUNK), rev3),
                   pl.BlockSpec((1, N_HEADS), lambda c: (0, 0)),
                   pl.BlockSpec((N_HEADS, 1), lambda c: (0, 0)),
                   pl.BlockSpec((1, SSD_W), lambda c: (0, 0))],
        out_shape=[_SDS((lp, XBC_W), F32), _SDS((nc, CHUNK, N_HEADS), F32), _SDS((nc, N_HEADS, CHUNK), F32),
                   _SDS((1, N_HEADS), F32), _SDS((N_HEADS, 1), F32), _SDS((1, SSD_W), F32)],
        scratch_shapes=[pltpu.VMEM((N_GROUPS * N_STATE, gw), F32), pltpu.VMEM((CHUNK, SSD_W), F32),
                        pltpu.VMEM((CHUNK, SSD_W), F32), pltpu.VMEM((CHUNK, SSD_W), F32),
                        pltpu.VMEM((8, SSD_W), F32)],
        compiler_params=_params(("arbitrary",)),
    )(xbc, dt_c, dt_tc, a, a_t, dskip_x, states, d_y)


def _gated_norm(o, gate, w):
    sg = _sigmoid(gate)
    p = o * (gate * sg)
    rs = lax.rsqrt(jnp.mean(p * p, axis=-1, keepdims=True) + EPS)
    n = p * rs
    return sg, rs, n, n * w


def _tail_fwd(o_sb, o_ssd, proj, h0, target, w_out, sb_w, ssd_w, fin_w):
    lp = o_sb.shape[0]
    nb = lp // TM
    row = lambda i: (i, 0)
    one = lambda i: (0, 0)

    def body(osb_ref, gate_ref, ossd_ref, z_ref, h0_ref, tgt_ref, wo_ref, sbw_ref, ssdw_ref, fw_ref,
             dh1_ref, loss_ref, gfw_ref):
        i = pl.program_id(0)

        @pl.when(i == 0)
        def _():
            loss_ref[...] = jnp.zeros_like(loss_ref)
            gfw_ref[...] = jnp.zeros_like(gfw_ref)

        y1 = _gated_norm(osb_ref[...], gate_ref[...], sbw_ref[...])[3]
        y2 = _gated_norm(ossd_ref[...], z_ref[...], ssdw_ref[...])[3]
        h1 = (h0_ref[...] + _mm(y1, wo_ref[0:SB_W, :])) + _mm(y2, wo_ref[SB_W:SB_W + SSD_W, :])
        rs1 = lax.rsqrt(jnp.mean(h1 * h1, axis=-1, keepdims=True) + EPS)
        n1 = h1 * rs1
        fw = fw_ref[...]
        diff = jnp.where(i > 0, n1 * fw - tgt_ref[...], 0.0)
        loss_ref[...] += jnp.sum(diff * diff, axis=0, keepdims=True)
        d_out = diff * (1.0 / D_MODEL)
        gfw_ref[...] += jnp.sum(d_out * n1, axis=0, keepdims=True)
        g = d_out * fw
        dh1_ref[...] = rs1 * (g - n1 * jnp.mean(g * n1, axis=-1, keepdims=True))

    return pl.pallas_call(
        body, name="tail_fwd", grid=(nb,),
        in_specs=[pl.BlockSpec((TM, SB_W), row),
                  pl.BlockSpec((TM, SB_W), lambda i: (i, (COL_GATE - QKV_W) // SB_W)),
                  pl.BlockSpec((TM, SSD_W), row),
                  pl.BlockSpec((TM, SSD_W), lambda i: (i, (COL_Z - QKV_W) // SSD_W)),
                  pl.BlockSpec((TM, D_MODEL), row),
                  pl.BlockSpec((TM, D_MODEL), lambda i: (jnp.maximum(i - 1, 0), 0)),
                  pl.BlockSpec(memory_space=_VMEM),
                  pl.BlockSpec((1, SB_W), one), pl.BlockSpec((1, SSD_W), one), pl.BlockSpec((1, D_MODEL), one)],
        out_specs=[pl.BlockSpec((TM, D_MODEL), row), pl.BlockSpec((1, D_MODEL), one), pl.BlockSpec((1, D_MODEL), one)],
        out_shape=[_SDS((lp, D_MODEL), F32), _SDS((1, D_MODEL), F32), _SDS((1, D_MODEL), F32)],
        compiler_params=_params(("arbitrary",), 40),
    )(o_sb, proj, o_ssd, proj, h0, target, w_out, sb_w, ssd_w, fin_w)


def _gated_norm_bwd(o, gate, w, dy):
    sg, rs, n, _ = _gated_norm(o, gate, w)
    gw = jnp.sum(dy * n, axis=0, keepdims=True)
    dn = dy * w
    dp = rs * (dn - n * jnp.mean(dn * n, axis=-1, keepdims=True))
    d_o = dp * (gate * sg)
    d_gate = dp * o * (sg * (1.0 + gate * (1.0 - sg)))
    return d_o, d_gate, gw, n * w


def _tail_bwd(o_sb, o_ssd, proj, d_h1, w_out, sb_w, ssd_w):
    lp = o_sb.shape[0]
    tm = 272 if lp % 272 == 0 else TM
    nb = lp // tm
    row = lambda i, t: (i, 0)
    one = lambda i, t: (0, 0)

    def body(osb_ref, gate_ref, ossd_ref, z_ref, dh1_ref, wo_ref, sbw_ref, ssdw_ref,
             dosb_ref, dossd_ref, dproj_ref, gwo_ref, gsb_ref, gssd_ref):
        i = pl.program_id(0)
        t = pl.program_id(1)

        @pl.when(jnp.logical_and(i == 0, t == 0))
        def _():
            gwo_ref[...] = jnp.zeros_like(gwo_ref)
            gsb_ref[...] = jnp.zeros_like(gsb_ref)
            gssd_ref[...] = jnp.zeros_like(gssd_ref)

        dh1 = dh1_ref[...].astype(_MXU)

        def half(o_ref, g_ref, w_ref, do_ref, gn_ref, r0):
            dy = lax.dot_general(dh1, wo_ref[r0:r0 + SB_W, :], _NT, preferred_element_type=F32)
            d_o, d_g, gw, y = _gated_norm_bwd(o_ref[...], g_ref[...], w_ref[...], dy)
            do_ref[...] = d_o
            dproj_ref[...] = d_g.astype(_MXU)
            gn_ref[...] += gw
            gwo_ref[r0:r0 + SB_W, :] += lax.dot_general(y.astype(_MXU), dh1, _TN, preferred_element_type=F32)

        @pl.when(t == 0)
        def _():
            half(osb_ref, gate_ref, sbw_ref, dosb_ref, gsb_ref, 0)

        @pl.when(t == 1)
        def _():
            half(ossd_ref, z_ref, ssdw_ref, dossd_ref, gssd_ref, SB_W)

    tile = pl.BlockSpec((tm, SB_W), row)
    return pl.pallas_call(
        body, name="tail_bwd", grid=(nb, 2),
        in_specs=[tile, pl.BlockSpec((tm, SB_W), lambda i, t: (i, (COL_GATE - QKV_W) // SB_W)),
                  tile, pl.BlockSpec((tm, SSD_W), lambda i, t: (i, (COL_Z - QKV_W) // SSD_W)),
                  tile, pl.BlockSpec(memory_space=_VMEM),
                  pl.BlockSpec((1, SB_W), one), pl.BlockSpec((1, SSD_W), one)],
        out_specs=[tile, tile, pl.BlockSpec((tm, SB_W), lambda i, t: (i, COL_GATE // SB_W + t)),
                   pl.BlockSpec((SB_W + SSD_W, D_MODEL), one), pl.BlockSpec((1, SB_W), one), pl.BlockSpec((1, SSD_W), one)],
        out_shape=[_SDS((lp, SB_W), F32), _SDS((lp, SSD_W), F32), _SDS((lp, W_ALL), _MXU),
                   _SDS((SB_W + SSD_W, D_MODEL), F32), _SDS((1, SB_W), F32), _SDS((1, SSD_W), F32)],
        compiler_params=_params(("arbitrary", "arbitrary"), 48),
    )(o_sb, proj, o_ssd, proj, d_h1, w_out, sb_w, ssd_w)


def _d_u(d_proj, w_t, send=(), dests=()):
    lp = d_proj.shape[0]
    tk = 512
    steps = N_MAIN // tk
    n = len(send)
    host_in, host_out, host_shapes, host_sems = _host_specs(send)

    def body(dp_ref, w_ref, dpdt_ref, wdt_ref, *rest):
        srcs, o_ref, dsts, sems = rest[:n], rest[n], rest[n + 1:2 * n + 1], rest[2 * n + 1:]
        j = pl.program_id(0)

        @pl.when(j == 0)
        def _():
            if n:
                _start_copies(_slab_copies(srcs, dsts, dests, *sems))
            o_ref[...] = jnp.dot(dpdt_ref[...], wdt_ref[...], preferred_element_type=F32)

        o_ref[...] += jnp.dot(dp_ref[...], w_ref[...], preferred_element_type=F32)

        if n:
            @pl.when(j == steps - 1)
            def _():
                _finish_copies(_slab_copies(srcs, dsts, dests, *sems))

    res = pl.pallas_call(
        body, name="d_u", grid=(steps,),
        in_specs=[pl.BlockSpec((lp, tk), lambda j: (0, j)),
                  pl.BlockSpec((tk, D_MODEL), lambda j: (j, 0)),
                  pl.BlockSpec((lp, 128), lambda j: (0, N_MAIN // 128)),
                  pl.BlockSpec((128, D_MODEL), lambda j: (N_MAIN // 128, 0))] + host_in,
        out_specs=[pl.BlockSpec((lp, D_MODEL), lambda j: (0, 0))] + host_out,
        out_shape=[_SDS((lp, D_MODEL), F32)] + host_shapes,
        scratch_shapes=host_sems,
        compiler_params=_params(("arbitrary",), 48),
    )(d_proj, w_t, d_proj, w_t, *send)
    return res[0], list(res[1:])


def _norm_bwd(du_all, h0, d_h1, norm_w):
    lp = h0.shape[0]
    nb = lp // TM
    seq = lp - OFF
    row = lambda i: (i, 0)
    one = lambda i: (0, 0)

    def body(du_ref, h0_ref, dh1_ref, nw_ref, gx_ref, gmeta_ref, gnw_ref):
        i = pl.program_id(0)

        @pl.when(i == 0)
        def _():
            gnw_ref[...] = jnp.zeros_like(gnw_ref)

        du = du_ref[...]
        h = h0_ref[...]
        rs = lax.rsqrt(jnp.mean(h * h, axis=-1, keepdims=True) + EPS)
        n0 = h * rs
        gnw_ref[...] += jnp.sum(du * n0, axis=0, keepdims=True)
        g = du * nw_ref[...]
        dh0 = dh1_ref[...] + rs * (g - n0 * jnp.mean(g * n0, axis=-1, keepdims=True))

        @pl.when(i == 0)
        def _():
            gmeta_ref[...] = dh0[PAD:PAD + N_META, :]

        @pl.when(i > 0)
        def _():
            gx_ref[...] = dh0

    tile = pl.BlockSpec((TM, D_MODEL), row)
    return pl.pallas_call(
        body, name="norm_bwd", grid=(nb,),
        in_specs=[tile, tile, tile, pl.BlockSpec((1, D_MODEL), one)],
        out_specs=[pl.BlockSpec((TM, D_MODEL), lambda i: (jnp.maximum(i - 1, 0), 0)),
                   pl.BlockSpec((N_META, D_MODEL), one), pl.BlockSpec((1, D_MODEL), one)],
        out_shape=[_SDS((seq, D_MODEL), F32), _SDS((N_META, D_MODEL), F32), _SDS((1, D_MODEL), F32)],
        compiler_params=_params(("arbitrary",)),
    )(du_all, h0, d_h1, norm_w)


def _grad_w_windows(u_t, d_proj, first, count, name):
    lp = d_proj.shape[0]
    hw = WIN_W // 2
    steps = 2 * count

    def body(ut_ref, dp_hbm, o_ref, buf, sems):
        s = pl.program_id(0)
        slot = s % 2

        def fetch(step, sl):
            start = pl.multiple_of((first + step // 2) * WIN_STEP + (step % 2) * hw, 128)
            return pltpu.make_async_copy(dp_hbm.at[:, pl.ds(start, hw)], buf.at[sl], sems.at[sl])

        @pl.when(s == 0)
        def _():
            fetch(0, 0).start()

        @pl.when(s + 1 < steps)
        def _():
            fetch(s + 1, 1 - slot).start()

        fetch(s, slot).wait()
        o_ref[0] = jnp.dot(ut_ref[...], buf[slot], preferred_element_type=F32).astype(o_ref.dtype)

    return pl.pallas_call(
        body, name=name, grid=(steps,),
        in_specs=[pl.BlockSpec((D_MODEL, lp), lambda s: (0, 0)), pl.BlockSpec(memory_space=pl.ANY)],
        out_specs=pl.BlockSpec((1, D_MODEL, hw), lambda s: (s // 2, 0, s % 2)),
        out_shape=_SDS((count, D_MODEL, WIN_W), _MXU),
        scratch_shapes=[pltpu.VMEM((2, lp, hw), _MXU), pltpu.SemaphoreType.DMA((2,))],
        compiler_params=_params(("arbitrary",), 40),
    )(u_t, d_proj)


def _device_grads(x2d, target2d, meta_full, norm_w, w_t, conv_w, conv_b, dt_bias, a_log, d_skip,
                  sb_w, ssd_w, w_out, fin_w, exchange=None):
    lp = x2d.shape[0] + OFF
    nc = lp // CHUNK
    h0, u, u_t = _prep(x2d, meta_full, norm_w)
    qkv, proj, dt_raw = _inproj(u, w_t)
    o_sb, o_lo = _sb_fwd(qkv)
    dt_bias128 = jnp.pad(dt_bias, ((0, 0), (0, 128 - N_HEADS)))
    xbc, dt128 = _conv_fwd(proj, dt_raw, conv_w, conv_b, dt_bias128)
    dt_c = dt128[:, :N_HEADS].reshape(nc, CHUNK, N_HEADS)
    dt_tc = jnp.swapaxes(dt_c, 1, 2)
    a = -jnp.exp(a_log)
    a_t = a.reshape(N_HEADS, 1)
    dskip_x = jnp.repeat(d_skip, HEAD, axis=1)
    o_ssd, states = _ssd_fwd(xbc, dt_c, dt_tc, a, a_t, dskip_x)
    d_h1, sq_err, g_fin = _tail_fwd(o_sb, o_ssd, proj, h0, target2d, w_out, sb_w, ssd_w, fin_w)

    d_osb, d_ossd, d_proj, g_wout, g_sb, g_ssd = _tail_bwd(o_sb, o_ssd, proj, d_h1, w_out, sb_w, ssd_w)
    d_xbc_act, ddt_a, ddt_b, ga1, ga2, gd = _ssd_bwd(xbc, dt_c, dt_tc, a, a_t, dskip_x, states, d_ossd)
    d_dt = (ddt_a + jnp.swapaxes(ddt_b, 1, 2)).reshape(lp, N_HEADS)
    d_dt128 = jnp.pad(d_dt, ((0, 0), (0, 128 - N_HEADS)))
    d_proj, g_convw, g_convb, g_dtb128 = _conv_bwd(proj, dt_raw, conv_w, conv_b, dt_bias128, d_xbc_act, d_dt128, d_proj)
    send_e, dests_e = ((), ()) if exchange is None else exchange["early"](g_wout)
    d_proj, arrived_e = _sb_bwd(qkv, o_sb, o_lo, d_osb, d_proj, send_e, dests_e)
    g_win = _grad_w_windows(u_t, d_proj, 0, N_CHIPS, "grad_w_in")
    send_l, dests_l = ((), ()) if exchange is None else exchange["late"](g_win)
    d_u, arrived_l = _d_u(d_proj, w_t, send_l, dests_l)
    send, arrived = tuple(send_e) + tuple(send_l), tuple(arrived_e) + tuple(arrived_l)
    g_x, g_meta, g_nw = _norm_bwd(d_u, h0, d_h1, norm_w)
    g_alog = (ga1 + ga2.reshape(1, N_HEADS)) * a
    g_dskip = gd.reshape(N_HEADS, HEAD).sum(axis=1).reshape(1, N_HEADS)
    grads = dict(meta_tokens=g_meta, norm_w=g_nw, w_in=g_win, conv_w=g_convw, conv_b=g_convb,
                 dt_bias=g_dtb128[:, :N_HEADS], a_log=g_alog, d_skip=g_dskip, sb_norm_w=g_sb, ssd_norm_w=g_ssd,
                 w_out=g_wout, final_norm_w=g_fin, sent=send, arrived=arrived)
    return sq_err, g_x, grads


_MESH = pl.DeviceIdType.MESH
_ANY = pl.BlockSpec(memory_space=pl.ANY)


def _place():
    return lax.axis_index("x"), lax.axis_index("y"), lax.axis_index("c")


def _other_chips(x, y):
    return ((1 - x, y), (x, 1 - y), (1 - x, 1 - y))


def _gather_shards(arrays, n_big):
    n = len(arrays)

    def body(*refs):
        srcs, dsts = refs[:n], refs[n:2 * n]
        send_sems, recv_sems, fwd_send, fwd_recv = refs[2 * n:]
        x, y, c = _place()
        mine = 2 * x + y
        chips = _other_chips(x, y)

        def window(a):
            half = arrays[a].shape[1] // 2
            return pl.ds(pl.multiple_of(c * half, 128), half)

        first = []
        for a in range(n):
            for k, (px, py) in enumerate(chips):
                if a < n_big:
                    src, dst = srcs[a].at[:, window(a)], dsts[a].at[mine, :, window(a)]
                else:
                    src, dst = srcs[a], dsts[a].at[mine]
                cp = pltpu.make_async_remote_copy(
                    src_ref=src, dst_ref=dst, send_sem=send_sems.at[a * 3 + k], recv_sem=recv_sems.at[a * 3 + k],
                    device_id=(px, py, c), device_id_type=_MESH)
                cp.start()
                first.append(cp)
        passed = []
        for a in range(n):
            for k, (px, py) in enumerate(chips):
                first[a * 3 + k].wait_recv()
                if a < n_big:
                    landed = dsts[a].at[2 * px + py, :, window(a)]
                    cp = pltpu.make_async_remote_copy(
                        src_ref=landed, dst_ref=landed, send_sem=fwd_send.at[a * 3 + k], recv_sem=fwd_recv.at[a * 3 + k],
                        device_id=(x, y, 1 - c), device_id_type=_MESH)
                    cp.start()
                    passed.append(cp)
        for cp in passed:
            cp.wait_recv()
        for cp in first + passed:
            cp.wait_send()

    got = pl.pallas_call(
        body, name="gather_shards",
        in_specs=[_ANY] * n, out_specs=[_ANY] * n,
        out_shape=[_SDS((N_CHIPS,) + a.shape, a.dtype) for a in arrays],
        scratch_shapes=[pltpu.SemaphoreType.DMA((3 * n,)), pltpu.SemaphoreType.DMA((3 * n,)),
                        pltpu.SemaphoreType.DMA((3 * n_big,)), pltpu.SemaphoreType.DMA((3 * n_big,))],
    )(*arrays)
    mine = 2 * lax.axis_index("x") + lax.axis_index("y")
    return [lax.dynamic_update_slice(g, a[None], (mine,) + (0,) * a.ndim) for g, a in zip(got, arrays)]


def _slab_copies(srcs, dsts, dests, send_sems, recv_sems):
    x, y, c = _place()
    mine = 2 * x + y
    copies = []
    for a in range(len(srcs)):
        lo, hi = dests[a]
        receives = jnp.logical_and(mine >= lo, mine < hi)
        for k, (px, py) in enumerate(_other_chips(x, y)):
            target = 2 * px + py
            cp = pltpu.make_async_remote_copy(
                src_ref=srcs[a].at[jnp.clip(target - lo, 0, hi - lo - 1)], dst_ref=dsts[a].at[mine],
                send_sem=send_sems.at[a * 3 + k], recv_sem=recv_sems.at[a * 3 + k],
                device_id=(px, py, c), device_id_type=_MESH)
            copies.append((cp, jnp.logical_and(target >= lo, target < hi), receives))
    return copies


def _start_copies(copies):
    for cp, sends, _ in copies:
        pl.when(sends)(cp.start)


def _finish_copies(copies):
    for cp, _, receives in copies:
        pl.when(receives)(cp.wait_recv)
    for cp, sends, _ in copies:
        pl.when(sends)(cp.wait_send)


def _host_specs(send):
    n = len(send)
    hbm = [pl.BlockSpec(memory_space=pl.ANY)] * n
    shapes = [_SDS((N_CHIPS,) + a.shape[1:], a.dtype) for a in send]
    sems = [pltpu.SemaphoreType.DMA((3 * n,)), pltpu.SemaphoreType.DMA((3 * n,))] if n else []
    return hbm, hbm, shapes, sems


def _swap_halves(arrays, name):
    n = len(arrays)

    def body(*refs):
        srcs, dsts = refs[:n], refs[n:2 * n]
        send_sems, recv_sems = refs[2 * n:]
        x, y, c = _place()
        copies = []
        for a in range(n):
            half = arrays[a].shape[1] // 2
            cp = pltpu.make_async_remote_copy(
                src_ref=srcs[a].at[:, pl.ds(pl.multiple_of((1 - c) * half, 16), half)], dst_ref=dsts[a],
                send_sem=send_sems.at[a], recv_sem=recv_sems.at[a],
                device_id=(x, y, 1 - c), device_id_type=_MESH)
            cp.start()
            copies.append(cp)
        for cp in copies:
            cp.wait_recv()
        for cp in copies:
            cp.wait_send()

    return pl.pallas_call(
        body, name=name,
        in_specs=[_ANY] * n, out_specs=[_ANY] * n,
        out_shape=[_SDS((a.shape[0], a.shape[1] // 2, a.shape[2]), a.dtype) for a in arrays],
        scratch_shapes=[pltpu.SemaphoreType.DMA((n,)), pltpu.SemaphoreType.DMA((n,))],
    )(*arrays)


N_DEV = 8
SMALL_ROWS = 32
SMALL_COLS = XBC_W


def _final_exchange(arrays, by_cols, packed):
    n = len(arrays)

    def body(*refs):
        src_ref = refs[n]
        dsts = refs[n + 1:2 * n + 1]
        dst_ref = refs[2 * n + 1]
        send_sems, recv_sems, all_send, all_recv, local_sem = refs[2 * n + 2:]
        x, y, c = _place()
        me = 4 * x + 2 * y + c
        own = pltpu.make_async_copy(src_ref, dst_ref.at[me], local_sem)
        own.start()
        copies = []
        for k in range(1, N_DEV):
            bx, by, bc = (k >> 2) & 1, (k >> 1) & 1, k & 1
            peer = (x + bx - 2 * x * bx, y + by - 2 * y * by, c + bc - 2 * c * bc)
            cp = pltpu.make_async_remote_copy(
                src_ref=src_ref, dst_ref=dst_ref.at[me], send_sem=all_send.at[k - 1], recv_sem=all_recv.at[k - 1],
                device_id=peer, device_id_type=_MESH)
            cp.start()
            copies.append(cp)
        for a in range(n):
            if by_cols[a]:
                half = arrays[a].shape[1] // 2
                mine = dsts[a].at[:, pl.ds(pl.multiple_of(c * half, 128), half)]
            else:
                half = arrays[a].shape[0] // 2
                mine = dsts[a].at[pl.ds(pl.multiple_of(c * half, 16), half)]
            cp = pltpu.make_async_remote_copy(
                src_ref=mine, dst_ref=mine, send_sem=send_sems.at[a], recv_sem=recv_sems.at[a],
                device_id=(x, y, 1 - c), device_id_type=_MESH)
            cp.start()
            copies.append(cp)
        for cp in copies:
            cp.wait_recv()
        for cp in copies:
            cp.wait_send()
        own.wait()

    vm = pl.BlockSpec(memory_space=_VMEM)
    res = pl.pallas_call(
        body, name="final_exchange",
        in_specs=[_ANY] * n + [vm], out_specs=[_ANY] * n + [vm],
        out_shape=[_SDS(a.shape, a.dtype) for a in arrays] + [_SDS((N_DEV, SMALL_ROWS, SMALL_COLS), F32)],
        input_output_aliases={a: a for a in range(n)},
        scratch_shapes=[pltpu.SemaphoreType.DMA((n,)), pltpu.SemaphoreType.DMA((n,)),
                        pltpu.SemaphoreType.DMA((N_DEV - 1,)), pltpu.SemaphoreType.DMA((N_DEV - 1,)),
                        pltpu.SemaphoreType.DMA],
    )(*arrays, packed)
    return list(res[:n]), res[n]


def _adamw(w, g, m, v):
    m = ADAM_B1 * m + (1.0 - ADAM_B1) * g
    v = ADAM_B2 * v + (1.0 - ADAM_B2) * (g * g)
    m_hat = m / (1.0 - ADAM_B1 ** ADAM_STEP)
    v_hat = v / (1.0 - ADAM_B2 ** ADAM_STEP)
    delta = -ADAM_LR * (m_hat / (jnp.sqrt(v_hat) + ADAM_EPS) + ADAM_WD * w)
    return delta, m, v


def _sum_slabs(slabs, core, name, transposed=False):
    _, h, c = slabs.shape
    tr = 128
    nblk = h // tr

    def body(core_ref, s_ref, o_ref):
        tot = ((s_ref[0].astype(F32) + s_ref[1].astype(F32)) + s_ref[2].astype(F32)) + s_ref[3].astype(F32)
        o_ref[...] = tot.T if transposed else tot

    if transposed:
        out_spec = pl.BlockSpec((c, tr), lambda i, core_ref: (0, core_ref[0] * nblk + i))
        out_shape = _SDS((c, 2 * h), F32)
    else:
        out_spec = pl.BlockSpec((tr, c), lambda i, core_ref: (core_ref[0] * nblk + i, 0))
        out_shape = _SDS((2 * h, c), F32)
    grid_spec = pltpu.PrefetchScalarGridSpec(
        num_scalar_prefetch=1, grid=(nblk,),
        in_specs=[pl.BlockSpec((N_CHIPS, tr, c), lambda i, core_ref: (0, i, 0))],
        out_specs=out_spec)
    return pl.pallas_call(
        body, name=name, grid_spec=grid_spec, out_shape=out_shape,
        compiler_params=_params(("arbitrary",)),
    )(core, slabs)


def _add_halves(own, recv, core, name):
    ns, r, c = own.shape
    half = r // 2
    tr = 128
    nblk = half // tr

    def body(core_ref, a_ref, b_ref, o_ref):
        o_ref[...] = (a_ref[...].astype(F32) + b_ref[...].astype(F32)).astype(o_ref.dtype)

    grid_spec = pltpu.PrefetchScalarGridSpec(
        num_scalar_prefetch=1, grid=(nblk,),
        in_specs=[pl.BlockSpec((ns, tr, c), lambda i, core_ref: (0, core_ref[0] * nblk + i, 0)),
                  pl.BlockSpec((ns, tr, c), lambda i, core_ref: (0, i, 0))],
        out_specs=pl.BlockSpec((ns, tr, c), lambda i, core_ref: (0, i, 0)))
    return pl.pallas_call(
        body, name=name, grid_spec=grid_spec, out_shape=_SDS((ns, half, c), own.dtype),
        compiler_params=_params(("arbitrary",)),
    )(core, own, recv)


def _update_big(w, m, v, g, name):
    r, c = w.shape

    def body(w_ref, m_ref, v_ref, g_ref, d_ref, mo_ref, vo_ref):
        delta, m_new, v_new = _adamw(w_ref[...], g_ref[...], m_ref[...], v_ref[...])
        d_ref[...] = delta
        mo_ref[...] = m_new
        vo_ref[...] = v_new

    if r % 128 == 0:
        steps, spec = r // 128, pl.BlockSpec((128, c), lambda i: (i, 0))
    else:
        steps, spec = c // 128, pl.BlockSpec((r, 128), lambda i: (0, i))
    return pl.pallas_call(
        body, name=name, grid=(steps,),
        in_specs=[spec] * 4, out_specs=[spec] * 3,
        out_shape=[_SDS((r, c), F32)] * 3,
        compiler_params=_params(("arbitrary",)),
    )(w, m, v, g)


_ROW = dict(norm_w=0, sb_norm_w=1, ssd_norm_w=2, final_norm_w=3, conv_b=4, dt_bias=5, a_log=6, d_skip=7,
            conv_w=8, sq_err=12, meta_tokens=16)
_SMALL = ("meta_tokens", "norm_w", "conv_w", "conv_b", "dt_bias", "a_log", "d_skip", "sb_norm_w", "ssd_norm_w",
          "final_norm_w")


def _pack_small(sq_err, grads):
    def rowpad(a):
        return jnp.pad(a, ((0, 0), (0, SMALL_COLS - a.shape[1])))

    rows = [rowpad(grads[k]) for k in ("norm_w", "sb_norm_w", "ssd_norm_w", "final_norm_w", "conv_b", "dt_bias", "a_log", "d_skip")]
    rows.append(grads["conv_w"])
    rows.append(rowpad(sq_err))
    rows.append(jnp.zeros((3, SMALL_COLS), F32))
    rows.append(rowpad(grads["meta_tokens"]))
    return jnp.concatenate(rows, axis=0)


def _update_small(gathered, ws, ms, vs):
    names = _SMALL
    n = len(names)

    def body(*refs):
        g_ref = refs[0]
        w_refs, m_refs, v_refs = refs[1:1 + n], refs[1 + n:1 + 2 * n], refs[1 + 2 * n:1 + 3 * n]
        outs = refs[1 + 3 * n:]
        loss_ref = outs[0]
        go, do, mo, vo = outs[1:1 + n], outs[1 + n:1 + 2 * n], outs[1 + 2 * n:1 + 3 * n], outs[1 + 3 * n:1 + 4 * n]
        tot = g_ref[0]
        for d in range(1, N_DEV):
            tot = tot + g_ref[d]
        x, y, _ = _place()
        chip = 2 * x + y
        loss_ref[...] = jnp.broadcast_to(
            0.5 * jnp.sum(tot[_ROW["sq_err"]:_ROW["sq_err"] + 1, 0:D_MODEL], axis=1, keepdims=True) / D_MODEL, (1, 128))
        for idx, nm in enumerate(names):
            r0 = _ROW[nm]
            rows, cols = w_refs[idx].shape
            if nm in ("conv_w", "meta_tokens"):
                g = jnp.zeros((rows, cols), F32)
                for j in range(N_CHIPS):
                    g = g + jnp.where(chip == j, tot[r0:r0 + rows, j * cols:(j + 1) * cols], 0.0)
            else:
                g = tot[r0:r0 + rows, 0:cols]
            delta, m_new, v_new = _adamw(w_refs[idx][...], g, m_refs[idx][...], v_refs[idx][...])
            go[idx][...] = g
            do[idx][...] = delta
            mo[idx][...] = m_new
            vo[idx][...] = v_new

    shapes = [_SDS(ws[nm].shape, F32) for nm in names]
    vm = pl.BlockSpec(memory_space=_VMEM)
    res = pl.pallas_call(
        body, name="update_small",
        in_specs=[vm] * (1 + 3 * n), out_specs=[vm] * (1 + 4 * n),
        out_shape=[_SDS((1, 128), F32)] + shapes * 4,
    )(gathered, *[ws[nm] for nm in names], *[ms[nm] for nm in names], *[vs[nm] for nm in names])
    loss = res[0][0, 0]
    g = dict(zip(names, res[1:1 + n]))
    d = dict(zip(names, res[1 + n:1 + 2 * n]))
    m = dict(zip(names, res[1 + 2 * n:1 + 3 * n]))
    v = dict(zip(names, res[1 + 3 * n:1 + 4 * n]))
    return loss, g, d, m, v


_WEIGHTS = ("meta_tokens", "norm_w", "w_in", "conv_w", "conv_b", "dt_bias", "a_log", "d_skip", "sb_norm_w",
            "ssd_norm_w", "w_out", "final_norm_w")


def kernel(x, meta_tokens, norm_w, w_in, conv_w, conv_b, dt_bias, a_log, d_skip, sb_norm_w, ssd_norm_w, w_out, final_norm_w, loss_target, m_meta_tokens, m_norm_w, m_w_in, m_conv_w, m_conv_b, m_dt_bias, m_a_log, m_d_skip, m_sb_norm_w, m_ssd_norm_w, m_w_out, m_final_norm_w, v_meta_tokens, v_norm_w, v_w_in, v_conv_w, v_conv_b, v_dt_bias, v_a_log, v_d_skip, v_sb_norm_w, v_ssd_norm_w, v_w_out, v_final_norm_w):
    given = dict(meta_tokens=meta_tokens, norm_w=norm_w, w_in=w_in, conv_w=conv_w, conv_b=conv_b, dt_bias=dt_bias,
                 a_log=a_log, d_skip=d_skip, sb_norm_w=sb_norm_w, ssd_norm_w=ssd_norm_w, w_out=w_out,
                 final_norm_w=final_norm_w)
    mom = dict(meta_tokens=m_meta_tokens, norm_w=m_norm_w, w_in=m_w_in, conv_w=m_conv_w, conv_b=m_conv_b,
               dt_bias=m_dt_bias, a_log=m_a_log, d_skip=m_d_skip, sb_norm_w=m_sb_norm_w, ssd_norm_w=m_ssd_norm_w,
               w_out=m_w_out, final_norm_w=m_final_norm_w)
    var = dict(meta_tokens=v_meta_tokens, norm_w=v_norm_w, w_in=v_w_in, conv_w=v_conv_w, conv_b=v_conv_b,
               dt_bias=v_dt_bias, a_log=v_a_log, d_skip=v_d_skip, sb_norm_w=v_sb_norm_w, ssd_norm_w=v_ssd_norm_w,
               w_out=v_w_out, final_norm_w=v_final_norm_w)
    seq = x.shape[1]

    def two_d(a):
        return a.reshape((-1, a.shape[-1])) if a.ndim != 2 else a

    def rows_first(a):
        return jnp.transpose(a, (2, 0, 1)).reshape(W_IN_SHARD, D_MODEL)

    def rows_last(a):
        return jnp.transpose(a.reshape(W_IN_SHARD, 1, D_MODEL), (1, 2, 0))

    w_in_t, m_in_t, v_in_t = rows_first(w_in), rows_first(m_w_in), rows_first(v_w_in)

    g_win, g_wout, g_meta, g_cw = _gather_shards(
        [w_in_t.astype(_MXU), w_out[0].astype(_MXU), meta_tokens, conv_w[0]], 2)
    w_t = jnp.pad(g_win.reshape(D_IN, D_MODEL), ((0, W_ALL - D_IN), (0, 0)))
    w_out_full = g_wout.reshape(2 * D_MODEL, D_MODEL)
    meta_full = jnp.swapaxes(g_meta, 0, 1).reshape(N_META, D_MODEL)
    conv_w_full = jnp.swapaxes(g_cw, 0, 1).reshape(4, XBC_W)

    core = lax.axis_index("c").astype(jnp.int32).reshape(1)

    def early(g_wout):
        slab_out = g_wout.reshape(N_CHIPS, W_OUT_SHARD, D_MODEL).astype(_MXU)
        (sib_out,) = _swap_halves([slab_out], "swap_halves_w_out")
        return (_add_halves(slab_out, sib_out, core, "chip_sum_w_out"),), ((0, N_CHIPS),)

    def late(g_win):
        (sib_in,) = _swap_halves([g_win], "swap_halves_w_in")
        return (_add_halves(g_win, sib_in, core, "chip_sum_w_in"),), ((0, N_CHIPS),)

    sq_err, g_x, grads = _device_grads(
        x.reshape(seq, D_MODEL), loss_target.reshape(seq, D_MODEL), meta_full, norm_w, w_t, conv_w_full,
        conv_b, dt_bias, a_log, d_skip, sb_norm_w, ssd_norm_w, w_out_full, final_norm_w.reshape(1, D_MODEL),
        exchange=dict(early=early, late=late))
    chip_out, chip_in = grads["sent"]
    got_out, got_in = grads["arrived"]
    chip = 2 * lax.axis_index("x") + lax.axis_index("y")

    def with_own(got, sent):
        own = lax.dynamic_slice(sent, (chip, 0, 0), (1,) + sent.shape[1:])
        return lax.dynamic_update_slice(got, own, (chip, 0, 0))

    (g_in, g_out), gathered = _final_exchange(
        [_sum_slabs(with_own(got_in, chip_in), core, "sum_w_in", transposed=True),
         _sum_slabs(with_own(got_out, chip_out), core, "sum_w_out")], (True, False), _pack_small(sq_err, grads))
    g_in = lax.dynamic_slice(g_in, (4 * chip, 0), (W_IN_SHARD, D_MODEL))
    big = dict(w_in=tuple(rows_last(a) for a in (g_in,) + tuple(_update_big(w_in_t, m_in_t, v_in_t, g_in, "update_w_in"))),
               w_out=(g_out,) + tuple(_update_big(w_out[0], m_w_out[0], v_w_out[0], g_out, "update_w_out")))

    loss, sg, sd, sm, sv = _update_small(
        gathered, {k: two_d(given[k]) for k in _SMALL}, {k: two_d(mom[k]) for k in _SMALL},
        {k: two_d(var[k]) for k in _SMALL})

    out = {}
    for idx, group in enumerate((sg, sd, sm, sv)):
        for k in _SMALL:
            out[(idx, k)] = group[k].reshape(given[k].shape)
        for k in ("w_in", "w_out"):
            out[(idx, k)] = big[k][idx].reshape(given[k].shape)
    return (loss, g_x.reshape(x.shape), *[out[(idx, k)] for idx in range(4) for k in _WEIGHTS])
```

```python
import functools
import math

import jax
import jax.numpy as jnp
from jax import lax
from jax.experimental import pallas as pl
from jax.experimental.pallas import tpu as pltpu

F32 = jnp.float32
_MXU = jnp.bfloat16

D_MODEL = 1024
N_META = 16
PAD = 112
OFF = PAD + N_META
TM = 128
CHUNK = 64
SB_W = 1024
SSD_W = 1024
N_HEADS = 16
HEAD = 64
N_GROUPS = 2
N_STATE = 128
XBC_W = SSD_W + 2 * N_GROUPS * N_STATE
N_MAIN = 4 * SB_W + SSD_W + XBC_W
QKV_W = 3 * SB_W
REST_W = N_MAIN - QKV_W
COL_GATE = 3 * SB_W
COL_Z = 4 * SB_W
COL_XBC = 5 * SB_W
D_IN = N_MAIN + N_HEADS
W_ALL = N_MAIN + 128
WIN_STEP = 1664
WIN_W = 1792
EPS = 1e-5
N_CHIPS = 4
W_IN_SHARD = D_IN // N_CHIPS
W_OUT_SHARD = 2 * D_MODEL // N_CHIPS

ADAM_LR = 0.001
ADAM_B1 = 0.9
ADAM_B2 = 0.999
ADAM_EPS = 1e-08
ADAM_WD = 0.01
ADAM_STEP = 10

_SDS = jax.ShapeDtypeStruct
_NT = (((1,), (1,)), ((), ()))
_TN = (((0,), (0,)), ((), ()))
_VMEM = pltpu.VMEM


def _params(sem=None, vmem_mb=None):
    kw = {}
    if sem is not None:
        kw["dimension_semantics"] = sem
    if vmem_mb is not None:
        kw["vmem_limit_bytes"] = vmem_mb * 1024 * 1024
    return pltpu.CompilerParams(**kw)


def _mm(a, b):
    return jnp.dot(a.astype(_MXU), b.astype(_MXU), preferred_element_type=F32)


def _mm_nt(a, b):
    return lax.dot_general(a.astype(_MXU), b.astype(_MXU), _NT, preferred_element_type=F32)


def _mm_tn(a, b):
    return lax.dot_general(a.astype(_MXU), b.astype(_MXU), _TN, preferred_element_type=F32)


def _split(x, parts):
    out = []
    r = x
    for _ in range(parts):
        p = r.astype(_MXU)
        out.append(p)
        r = r - p.astype(F32)
    return out


def _sel_right(x, m01, parts=3):
    acc = None
    for p in _split(x, parts):
        t = jnp.dot(p, m01, preferred_element_type=F32)
        acc = t if acc is None else acc + t
    return acc


def _sel_left(m01, x, parts=3):
    acc = None
    for p in _split(x, parts):
        t = jnp.dot(m01, p, preferred_element_type=F32)
        acc = t if acc is None else acc + t
    return acc


def _iota(shape, axis):
    return lax.broadcasted_iota(jnp.int32, shape, axis)


def _sigmoid(x):
    return 1.0 / (1.0 + jnp.exp(-x))


def _prep(x2d, meta_full, norm_w):
    seq = x2d.shape[0]
    lp = seq + OFF
    nb = lp // TM

    def body(x_ref, meta_ref, w_ref, h0_ref, u_ref, ut_ref):
        i = pl.program_id(0)

        @pl.when(i == 0)
        def _():
            h0_ref[...] = jnp.concatenate([jnp.zeros((PAD, D_MODEL), F32), meta_ref[...]], axis=0)

        @pl.when(i > 0)
        def _():
            h0_ref[...] = x_ref[...]

        h = h0_ref[...]
        rs = lax.rsqrt(jnp.mean(h * h, axis=-1, keepdims=True) + EPS)
        u = (h * rs * w_ref[...]).astype(_MXU)
        u_ref[...] = u
        ut_ref[...] = u.T

    return pl.pallas_call(
        body, name="prep", grid=(nb,),
        in_specs=[pl.BlockSpec((TM, D_MODEL), lambda i: (jnp.maximum(i - 1, 0), 0)),
                  pl.BlockSpec((N_META, D_MODEL), lambda i: (0, 0)),
                  pl.BlockSpec((1, D_MODEL), lambda i: (0, 0))],
        out_specs=[pl.BlockSpec((TM, D_MODEL), lambda i: (i, 0)),
                   pl.BlockSpec((TM, D_MODEL), lambda i: (i, 0)),
                   pl.BlockSpec((D_MODEL, TM), lambda i: (0, i))],
        out_shape=[_SDS((lp, D_MODEL), F32), _SDS((lp, D_MODEL), _MXU), _SDS((D_MODEL, lp), _MXU)],
        compiler_params=_params(("arbitrary",)),
    )(x2d, meta_full, norm_w)


def _inproj(u, w_t):
    lp = u.shape[0]
    tn = 512

    nq = QKV_W // tn

    def body(u_ref, w_ref, wdt_ref, qkv_ref, rest_ref, odt_ref):
        j = pl.program_id(0)
        res = lax.dot_general(u_ref[...], w_ref[...], _NT, preferred_element_type=F32)

        @pl.when(j < nq)
        def _():
            qkv_ref[...] = res.astype(qkv_ref.dtype)

        @pl.when(j >= nq)
        def _():
            rest_ref[...] = res

        @pl.when(j == 0)
        def _():
            odt_ref[...] = lax.dot_general(u_ref[...], wdt_ref[...], _NT, preferred_element_type=F32)

    return pl.pallas_call(
        body, name="inproj", grid=(N_MAIN // tn,),
        in_specs=[pl.BlockSpec((lp, D_MODEL), lambda j: (0, 0)),
                  pl.BlockSpec((tn, D_MODEL), lambda j: (j, 0)),
                  pl.BlockSpec((128, D_MODEL), lambda j: (N_MAIN // 128, 0))],
        out_specs=[pl.BlockSpec((lp, tn), lambda j: (0, jnp.minimum(j, nq - 1))),
                   pl.BlockSpec((lp, tn), lambda j: (0, jnp.maximum(j - nq, 0))),
                   pl.BlockSpec((lp, 128), lambda j: (0, 0))],
        out_shape=[_SDS((lp, QKV_W), _MXU), _SDS((lp, REST_W), F32), _SDS((lp, 128), F32)],
        compiler_params=_params(("arbitrary",), 48),
    )(u, w_t, w_t)


SB_WINDOW = 3
SB_TOP = 16
SB_DEAD = -104.0


def _sb_logs(qh, kwin):
    z = lax.dot_general(qh, kwin, _NT, preferred_element_type=F32)
    e = jnp.exp(-jnp.abs(z))
    l1p = jnp.log(1.0 + e)
    lk_full = -(jnp.maximum(z, 0.0) + l1p)
    ls = jnp.minimum(z, 0.0) - l1p
    return z, e, ls, lk_full


def _blk(a, b):
    return a[:, b * TM:(b + 1) * TM]


def _stacked_sel(blocks, m01, parts):
    n = len(blocks)
    rows = blocks[0].shape[0]
    pieces = [_split(b, parts) for b in blocks]
    stacked = jnp.concatenate([p[k] for k in range(parts) for p in pieces], axis=0)
    res = jnp.dot(stacked, m01, preferred_element_type=F32)
    out = []
    for j in range(n):
        tot = res[j * rows:(j + 1) * rows]
        for k in range(1, parts):
            tot = tot + res[(k * n + j) * rows:(k * n + j + 1) * rows]
        out.append(tot)
    return out


def _sb_weights(ls, lk_full, run, last_mask, upper, n):
    lk = [_blk(lk_full, b) for b in range(n)]
    lk[n - 1] = jnp.where(last_mask, lk[n - 1], 0.0)
    aft = _stacked_sel(lk, upper, 1)
    w = [None] * n
    for b in range(n - 1, -1, -1):
        wb = jnp.exp(_blk(ls, b) + aft[b] + run)
        w[b] = jnp.where(last_mask, wb, 0.0) if b == n - 1 else wb
        run = run + jnp.sum(lk[b], axis=1, keepdims=True)
    return w, run


def _sb_alive(run_scr):
    top = jnp.max(run_scr[:, 0:SB_TOP, :]) > SB_DEAD
    rest = jnp.max(run_scr[:, SB_TOP:, :]) > SB_DEAD
    return top.astype(jnp.int32), rest.astype(jnp.int32)


def _sb_walk(i, key_set, strict, run_scr):
    @pl.when(i >= SB_WINDOW - 1)
    def _():
        key_set(i - (SB_WINDOW - 1), SB_WINDOW, strict, TM)

    start = jnp.where(i >= SB_WINDOW - 1, i - SB_WINDOW, i)

    def cond(c):
        return jnp.logical_and(c[0] >= 0, c[1] + c[2] > 0)

    def step(c):
        kb, _, rest = c
        mask = jnp.logical_or(strict, kb < i)

        @pl.when(rest > 0)
        def _():
            key_set(kb, 1, mask, TM)

        @pl.when(rest == 0)
        def _():
            key_set(kb, 1, mask, SB_TOP)

        return (kb - 1,) + _sb_alive(run_scr)

    lax.while_loop(cond, step, (start,) + _sb_alive(run_scr))


SB_LANES_FWD = 256
SB_LANES_BWD = 256


def _head_masks(lanes):
    lane = _iota((TM, lanes), 1)
    return tuple(jnp.logical_and(lane >= h * HEAD, lane < (h + 1) * HEAD) for h in range(lanes // HEAD))


def _by_head(hmask, parts):
    out = parts[-1]
    for h in range(len(parts) - 2, -1, -1):
        out = jnp.where(hmask[h], parts[h], out)
    return out


def _sb_fwd(qkv):
    lp = qkv.shape[0]
    nb = lp // TM
    lw = SB_LANES_FWD
    nh = lw // HEAD

    def body(q_ref, k_ref, v_ref, o_ref, olo_ref, acc, run_scr):
        i = pl.program_id(1)
        lane = _iota((TM, TM), 1)
        row = _iota((TM, TM), 0)
        hmask = _head_masks(lw)
        upper = (row > lane).astype(_MXU)
        strict = lane < row
        q = q_ref[...] * (1.0 / math.sqrt(HEAD))
        qh = tuple(jnp.where(m, q, 0.0).astype(_MXU) for m in hmask)

        def key_set(first, n, last_mask, nrows):
            off = pl.multiple_of(first * TM, TM)
            kwin = k_ref[pl.ds(off, n * TM), :].astype(_MXU)
            vwin = v_ref[pl.ds(off, n * TM), :].astype(_MXU)
            for hh in range(nh):
                run = run_scr[hh, 0:nrows, 0:1]
                _, _, ls, lk_full = _sb_logs(qh[hh][0:nrows], kwin)
                w, run = _sb_weights(ls, lk_full, run, last_mask[0:nrows], upper, n)
                pieces = [_split(wb, 2) for wb in w]
                stacked = jnp.concatenate(
                    [jnp.concatenate([p[0] for p in pieces], axis=1), jnp.concatenate([p[1] for p in pieces], axis=1)], axis=0)
                res = jnp.dot(stacked, vwin, preferred_element_type=F32)
                acc[hh, 0:nrows] += res[0:nrows]
                acc[nh + hh, 0:nrows] += res[nrows:2 * nrows]
                run_scr[hh, 0:nrows] = jnp.broadcast_to(run, (nrows, TM))

        acc[...] = jnp.zeros_like(acc)
        run_scr[...] = jnp.zeros_like(run_scr)
        _sb_walk(i, key_set, strict, run_scr)
        o_ref[...] = _by_head(hmask, [acc[h] for h in range(nh)])
        olo_ref[...] = _by_head(hmask, [acc[nh + h] for h in range(nh)])

    npair = SB_W // lw
    blk = pl.BlockSpec((TM, lw), lambda p, i: (i, p))
    return pl.pallas_call(
        body, name="sb_fwd", grid=(npair, nb),
        in_specs=[blk,
                  pl.BlockSpec((lp, lw), lambda p, i: (0, npair + p)),
                  pl.BlockSpec((lp, lw), lambda p, i: (0, 2 * npair + p))],
        out_specs=[blk, blk],
        out_shape=[_SDS((lp, SB_W), F32), _SDS((lp, SB_W), F32)],
        scratch_shapes=[pltpu.VMEM((2 * nh, TM, lw), F32), pltpu.VMEM((nh, TM, TM), F32)],
        compiler_params=_params(("arbitrary", "arbitrary")),
    )(qkv, qkv, qkv)


def _sb_bwd(qkv, o_sb, o_lo, d_o, d_proj, send=(), dests=()):
    lp = qkv.shape[0]
    nb = lp // TM
    lw = SB_LANES_BWD
    nh = lw // HEAD
    npair = SB_W // lw
    scale = 1.0 / math.sqrt(HEAD)
    n = len(send)
    host_in, host_out, host_shapes, host_sems = _host_specs(send)

    def body(q_ref, k_ref, v_ref, o_ref, olo_ref, do_ref, dproj_in, *rest):
        srcs, dproj_ref, dsts = rest[:n], rest[n], rest[n + 1:2 * n + 1]
        dq_all, dk_ref, dv_ref, stage, sems, dq_acc, run_scr, gsum_scr = rest[2 * n + 1:2 * n + 9]
        host_sem_refs = rest[2 * n + 9:]
        p = pl.program_id(0)
        i = pl.program_id(1)

        if n:
            @pl.when(jnp.logical_and(p == 0, i == 0))
            def _():
                _start_copies(_slab_copies(srcs, dsts, dests, *host_sem_refs))

        @pl.when(i == 0)
        def _():
            dk_ref[...] = jnp.zeros_like(dk_ref)
            dv_ref[...] = jnp.zeros_like(dv_ref)

        lane = _iota((TM, TM), 1)
        row = _iota((TM, TM), 0)
        hmask = _head_masks(lw)
        upper = (row > lane).astype(_MXU)
        lower_incl = (row >= lane).astype(_MXU)
        strict = lane < row
        q = q_ref[...] * scale
        do = do_ref[...]
        prod = do.astype(_MXU).astype(F32) * (o_ref[...] + olo_ref[...])
        qh = tuple(jnp.where(m, q, 0.0).astype(_MXU) for m in hmask)
        doh = tuple(jnp.where(m, do, 0.0).astype(_MXU) for m in hmask)
        gtot = tuple(jnp.sum(jnp.where(m, prod, 0.0), axis=1, keepdims=True) for m in hmask)

        def key_set(first, n, last_mask, nrows):
            off = pl.multiple_of(first * TM, TM)
            kf = k_ref[pl.ds(off, n * TM), :]
            kwin = kf.astype(_MXU)
            vwin = v_ref[pl.ds(off, n * TM), :].astype(_MXU)
            last_mask = last_mask[0:nrows]
            dk_win = None
            for hh in range(nh):
                run = run_scr[hh, 0:nrows, 0:1]
                gsum = gsum_scr[hh, 0:nrows, 0:1]
                z, e, ls, lk_full = _sb_logs(qh[hh][0:nrows], kwin)
                w, run = _sb_weights(ls, lk_full, run, last_mask, upper, n)
                r = 1.0 / (1.0 + e)
                er = e * r
                pos = z >= 0.0
                beta = jnp.where(pos, r, er)
                one_m_beta = jnp.where(pos, er, r)
                dw = lax.dot_general(doh[hh][0:nrows], vwin, _NT, preferred_element_type=F32)
                g = [_blk(dw, b) * w[b] for b in range(n)]
                suffix = _stacked_sel(g, lower_incl, 2)
                dz = [None] * n
                for b in range(n - 1, -1, -1):
                    prefix = gtot[hh][0:nrows] - gsum - suffix[b]
                    d = g[b] * _blk(one_m_beta, b) - _blk(beta, b) * prefix
                    dz[b] = (jnp.where(last_mask, d, 0.0) if b == n - 1 else d).astype(_MXU)
                    gsum = gsum + jnp.sum(g[b], axis=1, keepdims=True)
                dzw = jnp.concatenate(dz, axis=1)
                ww = jnp.concatenate([wb.astype(_MXU) for wb in w], axis=1)
                kh = jnp.where(hmask[hh][0:1, :], kf, 0.0).astype(_MXU)
                dq_acc[0:nrows] += jnp.dot(dzw, kh, preferred_element_type=F32)
                dk_h = lax.dot_general(dzw, qh[hh][0:nrows], _TN, preferred_element_type=F32)
                dv_h = lax.dot_general(ww, doh[hh][0:nrows], _TN, preferred_element_type=F32)
                dk_win = (dk_h, dv_h) if dk_win is None else (dk_win[0] + dk_h, dk_win[1] + dv_h)
                run_scr[hh, 0:nrows] = jnp.broadcast_to(run, (nrows, TM))
                gsum_scr[hh, 0:nrows] = jnp.broadcast_to(gsum, (nrows, TM))
            dk_ref[pl.ds(off, n * TM), :] += dk_win[0]
            dv_ref[pl.ds(off, n * TM), :] += dk_win[1]

        dq_acc[...] = jnp.zeros_like(dq_acc)
        run_scr[...] = jnp.zeros_like(run_scr)
        gsum_scr[...] = jnp.zeros_like(gsum_scr)

        _sb_walk(i, key_set, strict, run_scr)
        dq_all[pl.ds(pl.multiple_of(i * TM, TM), TM), :] = dq_acc[...] * scale

        @pl.when(i == nb - 1)
        def _():
            copies = []
            for s, src in enumerate((dq_all, dk_ref, dv_ref)):
                stage[s] = src[...].astype(_MXU)
                col = pl.multiple_of((s * npair + p) * lw, lw)
                copies.append(pltpu.make_async_copy(stage.at[s], dproj_ref.at[:, pl.ds(col, lw)], sems.at[s]))
                copies[-1].start()
            for cp in copies:
                cp.wait()

        if n:
            @pl.when(jnp.logical_and(p == npair - 1, i == nb - 1))
            def _():
                _finish_copies(_slab_copies(srcs, dsts, dests, *host_sem_refs))

    blk = pl.BlockSpec((TM, lw), lambda p, i: (i, p))
    res = pl.pallas_call(
        body, name="sb_bwd", grid=(npair, nb),
        in_specs=[blk,
                  pl.BlockSpec((lp, lw), lambda p, i: (0, npair + p)),
                  pl.BlockSpec((lp, lw), lambda p, i: (0, 2 * npair + p)),
                  blk, blk, blk, pl.BlockSpec(memory_space=pl.ANY)] + host_in,
        out_specs=[pl.BlockSpec(memory_space=pl.ANY)] + host_out,
        out_shape=[_SDS(d_proj.shape, d_proj.dtype)] + host_shapes,
        input_output_aliases={6: 0},
        scratch_shapes=[pltpu.VMEM((lp, lw), F32), pltpu.VMEM((lp, lw), F32), pltpu.VMEM((lp, lw), F32),
                        pltpu.VMEM((3, lp, lw), _MXU), pltpu.SemaphoreType.DMA((3,)),
                        pltpu.VMEM((TM, lw), F32), pltpu.VMEM((nh, TM, TM), F32),
                        pltpu.VMEM((nh, TM, TM), F32)] + host_sems,
        compiler_params=_params(("arbitrary", "arbitrary")),
    )(qkv, qkv, qkv, o_sb, o_lo, d_o, d_proj, *send)
    return res[0], list(res[1:])


def _conv_pre(x_ref, w_ref, b_ref, lp):
    n = lp - 8
    w = w_ref[...]
    pre = (x_ref[pl.ds(5, n), :] * w[0:1, :] + x_ref[pl.ds(6, n), :] * w[1:2, :]
           + x_ref[pl.ds(7, n), :] * w[2:3, :] + x_ref[pl.ds(8, n), :] * w[3:4, :]) + b_ref[...]
    live = (_iota((n, 128), 0) + 8) >= PAD
    return pre, live


def _conv_fwd(proj, dt_raw, conv_w, conv_b, dt_bias128):
    lp = proj.shape[0]
    nblk = XBC_W // 128
    c0 = (COL_XBC - QKV_W) // 128

    def body(x_ref, w_ref, b_ref, dtr_ref, dtb_ref, o_ref, dt_ref):
        pre, live = _conv_pre(x_ref, w_ref, b_ref, lp)
        act = pre * _sigmoid(pre)
        o_ref[pl.ds(0, 8), :] = jnp.zeros((8, 128), F32)
        o_ref[pl.ds(8, lp - 8), :] = jnp.where(live, act, 0.0)

        @pl.when(pl.program_id(0) == 0)
        def _():
            s = dtr_ref[...] + dtb_ref[...]
            sp = jnp.maximum(s, 0.0) + jnp.log(1.0 + jnp.exp(-jnp.abs(s)))
            dt_ref[...] = jnp.where(_iota((lp, 128), 0) >= PAD, sp, 0.0)

    return pl.pallas_call(
        body, name="conv_fwd", grid=(nblk,),
        in_specs=[pl.BlockSpec((lp, 128), lambda j: (0, c0 + j)),
                  pl.BlockSpec((4, 128), lambda j: (0, j)),
                  pl.BlockSpec((1, 128), lambda j: (0, j)),
                  pl.BlockSpec((lp, 128), lambda j: (0, 0)),
                  pl.BlockSpec((1, 128), lambda j: (0, 0))],
        out_specs=[pl.BlockSpec((lp, 128), lambda j: (0, j)),
                   pl.BlockSpec((lp, 128), lambda j: (0, 0))],
        out_shape=[_SDS((lp, XBC_W), F32), _SDS((lp, 128), F32)],
        compiler_params=_params(("arbitrary",)),
    )(proj, conv_w, conv_b, dt_raw, dt_bias128)


def _conv_bwd(proj, dt_raw, conv_w, conv_b, dt_bias128, d_xbc, d_dt128, d_proj):
    lp = proj.shape[0]
    nblk = XBC_W // 128
    c0 = COL_XBC // 128
    c0_in = (COL_XBC - QKV_W) // 128
    n = lp - 8
    last = nblk - 1

    def body(x_ref, w_ref, b_ref, dtr_ref, dtb_ref, dy_ref, ddt_ref, dproj_in,
             dx_ref, gw_ref, gb_ref, gdtb_ref, scr):
        j = pl.program_id(0)

        @pl.when(j < nblk)
        def _():
            pre, live = _conv_pre(x_ref, w_ref, b_ref, lp)
            sg = _sigmoid(pre)
            dpre = jnp.where(live, dy_ref[pl.ds(8, n), :] * (sg * (1.0 + pre * (1.0 - sg))), 0.0)
            gb_ref[...] = jnp.sum(dpre, axis=0, keepdims=True)
            gw_ref[...] = jnp.concatenate(
                [jnp.sum(dpre * x_ref[pl.ds(5 + k, n), :], axis=0, keepdims=True) for k in range(4)], axis=0)
            scr[pl.ds(0, 8), :] = jnp.zeros((8, 128), F32)
            scr[pl.ds(8, n), :] = dpre
            scr[pl.ds(lp, 8), :] = jnp.zeros((8, 128), F32)
            w = w_ref[...]
            dx_ref[...] = (scr[pl.ds(0, lp), :] * w[3:4, :] + scr[pl.ds(1, lp), :] * w[2:3, :]
                           + scr[pl.ds(2, lp), :] * w[1:2, :] + scr[pl.ds(3, lp), :] * w[0:1, :]).astype(dx_ref.dtype)

        @pl.when(j == nblk)
        def _():
            s = dtr_ref[...] + dtb_ref[...]
            d = jnp.where(_iota((lp, 128), 0) >= PAD, ddt_ref[...] * _sigmoid(s), 0.0)
            dx_ref[...] = d.astype(dx_ref.dtype)
            gdtb_ref[...] = jnp.sum(d, axis=0, keepdims=True)

    clamp = lambda j: (0, jnp.minimum(j, last))
    full128 = pl.BlockSpec((lp, 128), lambda j: (0, 0))
    return pl.pallas_call(
        body, name="conv_bwd", grid=(nblk + 1,),
        in_specs=[pl.BlockSpec((lp, 128), lambda j: (0, c0_in + jnp.minimum(j, last))),
                  pl.BlockSpec((4, 128), clamp),
                  pl.BlockSpec((1, 128), clamp),
                  full128, pl.BlockSpec((1, 128), lambda j: (0, 0)),
                  pl.BlockSpec((lp, 128), clamp), full128, pl.BlockSpec(memory_space=pl.ANY)],
        out_specs=[pl.BlockSpec((lp, 128), lambda j: (0, c0 + j)), pl.BlockSpec((4, 128), clamp),
                   pl.BlockSpec((1, 128), clamp), pl.BlockSpec((1, 128), lambda j: (0, 0))],
        out_shape=[_SDS(d_proj.shape, d_proj.dtype), _SDS((4, XBC_W), F32), _SDS((1, XBC_W), F32), _SDS((1, 128), F32)],
        input_output_aliases={7: 0},
        scratch_shapes=[pltpu.VMEM((lp + 8, 128), F32)],
        compiler_params=_params(("arbitrary",)),
    )(proj, conv_w, conv_b, dt_raw, dt_bias128, d_xbc, d_dt128, d_proj)


def _ssd_pieces(dt, dt_t, a, a_t):
    r64 = _iota((CHUNK, CHUNK), 0)
    c64 = _iota((CHUNK, CHUNK), 1)
    tril = c64 <= r64
    tril01 = tril.astype(_MXU)
    triu01 = (r64 <= c64).astype(_MXU)
    expand = (lax.shift_right_logical(_iota((N_HEADS, SSD_W), 1), 6) == _iota((N_HEADS, SSD_W), 0)).astype(_MXU)
    acum = _sel_left(tril01, dt * a)
    acum_t = _sel_right(dt_t * a_t, triu01)
    ax = _sel_right(acum, expand)
    dtx = _sel_right(dt, expand)
    return tril, expand, acum, acum_t, ax, dtx


def _seg_matrix():
    return (lax.shift_right_logical(_iota((SSD_W, N_HEADS), 0), 6) == _iota((SSD_W, N_HEADS), 1)).astype(_MXU)


def _head_decay(ax, acum_t, h, tril):
    col = ax[:, h * HEAD:(h + 1) * HEAD]
    rowv = acum_t[h:h + 1, :]
    return jnp.where(tril, jnp.exp(jnp.minimum(col - rowv, 0.0)), 0.0)


def _ssd_fwd(xbc, dt_c, dt_tc, a, a_t, dskip_x):
    lp = xbc.shape[0]
    nc = lp // CHUNK
    gw = SSD_W // N_GROUPS
    hpg = N_HEADS // N_GROUPS

    def body(x_ref, dt_ref, dtt_ref, a_ref, at_ref, d_ref, y_ref, st_ref, state):
        c = pl.program_id(0)

        @pl.when(c == 0)
        def _():
            state[...] = jnp.zeros_like(state)

        st_ref[0] = state[...]
        tril, _, _, acum_t, ax, dtx = _ssd_pieces(dt_ref[0], dtt_ref[0], a_ref[...], at_ref[...])
        x = x_ref[:, 0:SSD_W]
        xdt = x * dtx
        ea = jnp.exp(ax)
        aex = ax[CHUNK - 1:CHUNK, :]
        wd = jnp.exp(aex - ax)
        eae = jnp.exp(aex)
        xw = xdt * wd
        y_ref[...] = x * d_ref[...]
        for g in range(N_GROUPS):
            gs = slice(g * gw, (g + 1) * gw)
            rs = slice(g * N_STATE, (g + 1) * N_STATE)
            bg = x_ref[:, SSD_W + g * N_STATE:SSD_W + (g + 1) * N_STATE]
            cg = x_ref[:, SSD_W + N_GROUPS * N_STATE + g * N_STATE:SSD_W + N_GROUPS * N_STATE + (g + 1) * N_STATE]
            sg = state[rs, :]
            cb = _mm_nt(cg, bg)
            y_ref[:, gs] += _mm(cg, sg) * ea[:, gs]
            for r in range(hpg):
                h = g * hpg + r
                hs = slice(h * HEAD, (h + 1) * HEAD)
                m = cb * _head_decay(ax, acum_t, h, tril)
                y_ref[:, hs] += _mm(m, xdt[:, hs])
            state[rs, :] = sg * eae[:, gs] + _mm_tn(bg, xw[:, gs])

    return pl.pallas_call(
        body, name="ssd_fwd", grid=(nc,),
        in_specs=[pl.BlockSpec((CHUNK, XBC_W), lambda c: (c, 0)),
                  pl.BlockSpec((1, CHUNK, N_HEADS), lambda c: (c, 0, 0)),
                  pl.BlockSpec((1, N_HEADS, CHUNK), lambda c: (c, 0, 0)),
                  pl.BlockSpec((1, N_HEADS), lambda c: (0, 0)),
                  pl.BlockSpec((N_HEADS, 1), lambda c: (0, 0)),
                  pl.BlockSpec((1, SSD_W), lambda c: (0, 0))],
        out_specs=[pl.BlockSpec((CHUNK, SSD_W), lambda c: (c, 0)),
                   pl.BlockSpec((1, N_GROUPS * N_STATE, gw), lambda c: (c, 0, 0))],
        out_shape=[_SDS((lp, SSD_W), F32), _SDS((nc, N_GROUPS * N_STATE, gw), F32)],
        scratch_shapes=[pltpu.VMEM((N_GROUPS * N_STATE, gw), F32)],
        compiler_params=_params(("arbitrary",)),
    )(xbc, dt_c, dt_tc, a, a_t, dskip_x)


def _ssd_bwd(xbc, dt_c, dt_tc, a, a_t, dskip_x, states, d_y):
    lp = xbc.shape[0]
    nc = lp // CHUNK
    gw = SSD_W // N_GROUPS
    hpg = N_HEADS // N_GROUPS

    def body(x_ref, dt_ref, dtt_ref, a_ref, at_ref, d_ref, st_ref, dy_ref,
             dx_ref, ddta_ref, ddtb_ref, ga1_ref, ga2_ref, gd_ref, dstate, dxdt_scr, z_scr, yoff_scr, sds_scr):
        c = pl.program_id(0)

        @pl.when(c == 0)
        def _():
            dstate[...] = jnp.zeros_like(dstate)
            ga1_ref[...] = jnp.zeros_like(ga1_ref)
            ga2_ref[...] = jnp.zeros_like(ga2_ref)
            gd_ref[...] = jnp.zeros_like(gd_ref)

        dt = dt_ref[0]
        dt_t = dtt_ref[0]
        a = a_ref[...]
        a_t = at_ref[...]
        tril, _, acum, acum_t, ax, dtx = _ssd_pieces(dt, dt_t, a, a_t)
        seg = _seg_matrix()
        x = x_ref[:, 0:SSD_W]
        dy = dy_ref[...]
        xdt = x * dtx
        ea = jnp.exp(ax)
        aex = ax[CHUNK - 1:CHUNK, :]
        wd = jnp.exp(aex - ax)
        eae = jnp.exp(aex)
        xw = xdt * wd
        edy = ea * dy
        lane16 = _iota((CHUNK, N_HEADS), 1)
        row16 = _iota((N_HEADS, CHUNK), 0)
        da_col = jnp.zeros((CHUNK, N_HEADS), F32)
        da_row = jnp.zeros((N_HEADS, CHUNK), F32)
        for g in range(N_GROUPS):
            gs = slice(g * gw, (g + 1) * gw)
            rs = slice(g * N_STATE, (g + 1) * N_STATE)
            bcol = slice(SSD_W + g * N_STATE, SSD_W + (g + 1) * N_STATE)
            ccol = slice(SSD_W + N_GROUPS * N_STATE + g * N_STATE, SSD_W + N_GROUPS * N_STATE + (g + 1) * N_STATE)
            bg = x_ref[:, bcol]
            cg = x_ref[:, ccol]
            sg = st_ref[0, rs, :]
            dsn = dstate[rs, :]
            cb = _mm_nt(cg, bg)
            z_scr[:, gs] = _mm(bg, dsn)
            yoff_scr[:, gs] = _mm(cg, sg) * ea[:, gs]
            sds_scr[:, gs] = jnp.broadcast_to(jnp.sum(dsn * sg, axis=0, keepdims=True), (8, gw))
            dcb = jnp.zeros((CHUNK, CHUNK), F32)
            for r in range(hpg):
                h = g * hpg + r
                hs = slice(h * HEAD, (h + 1) * HEAD)
                dec = _head_decay(ax, acum_t, h, tril)
                m = cb * dec
                t1 = _mm_nt(dy[:, hs], xdt[:, hs])
                dcb = dcb + dec * t1
                tm = m * t1
                da_col = da_col + jnp.where(lane16 == h, jnp.sum(tm, axis=1, keepdims=True), 0.0)
                da_row = da_row - jnp.where(row16 == h, jnp.sum(tm, axis=0, keepdims=True), 0.0)
                dxdt_scr[:, hs] = _mm_tn(m, dy[:, hs])
            dx_ref[:, ccol] = _mm(dcb, bg) + _mm_nt(edy[:, gs], sg)
            dx_ref[:, bcol] = _mm_tn(dcb, cg) + _mm_nt(xw[:, gs], dsn)
            dstate[rs, :] = eae[:, gs] * dsn + _mm_tn(cg, edy[:, gs])
        zf = z_scr[...]
        dxdt = dxdt_scr[...] + wd * zf
        t3 = _sel_right(xw * zf, seg)
        da_col = da_col + _sel_right(dy * yoff_scr[...], seg) - t3
        aend = acum[CHUNK - 1:CHUNK, :]
        sd = _sel_right(sds_scr[...], seg)[0:1, :] * jnp.exp(aend)
        last = jnp.sum(t3, axis=0, keepdims=True) + sd
        da_col = da_col + jnp.where(_iota((CHUNK, N_HEADS), 0) == CHUNK - 1, last, 0.0)
        r64 = _iota((CHUNK, CHUNK), 0)
        c64 = _iota((CHUNK, CHUNK), 1)
        ddta1 = _sel_left((c64 >= r64).astype(_MXU), da_col)
        ddta2 = _sel_right(da_row, (r64 >= c64).astype(_MXU))
        ddta_ref[0] = a * ddta1 + _sel_right(dxdt * x, seg)
        ddtb_ref[0] = a_t * ddta2
        ga1_ref[...] += jnp.sum(dt * ddta1, axis=0, keepdims=True)
        ga2_ref[...] += jnp.sum(dt_t * ddta2, axis=1, keepdims=True)
        dx_ref[:, 0:SSD_W] = dxdt * dtx + d_ref[...] * dy
        gd_ref[...] += jnp.sum(dy * x, axis=0, keepdims=True)

    rev = lambda c: (nc - 1 - c, 0)
    rev3 = lambda c: (nc - 1 - c, 0, 0)
    return pl.pallas_call(
        body, name="ssd_bwd", grid=(nc,),
        in_specs=[pl.BlockSpec((CHUNK, XBC_W), rev),
                  pl.BlockSpec((1, CHUNK, N_HEADS), rev3),
                  pl.BlockSpec((1, N_HEADS, CHUNK), rev3),
                  pl.BlockSpec((1, N_HEADS), lambda c: (0, 0)),
                  pl.BlockSpec((N_HEADS, 1), lambda c: (0, 0)),
                  pl.BlockSpec((1, SSD_W), lambda c: (0, 0)),
                  pl.BlockSpec((1, N_GROUPS * N_STATE, gw), rev3),
                  pl.BlockSpec((CHUNK, SSD_W), rev)],
        out_specs=[pl.BlockSpec((CHUNK, XBC_W), rev),
                   pl.BlockSpec((1, CHUNK, N_HEADS), rev3),
                   pl.BlockSpec((1, N_HEADS, CHUNK), rev3),
                   pl.BlockSpec((1, N_HEADS), lambda c: (0, 0)),
                   pl.BlockSpec((N_HEADS, 1), lambda c: (0, 0)),
                   pl.BlockSpec((1, SSD_W), lambda c: (0, 0))],
        out_shape=[_SDS((lp, XBC_W), F32), _SDS((nc, CHUNK, N_HEADS), F32), _SDS((nc, N_HEADS, CHUNK), F32),
                   _SDS((1, N_HEADS), F32), _SDS((N_HEADS, 1), F32), _SDS((1, SSD_W), F32)],
        scratch_shapes=[pltpu.VMEM((N_GROUPS * N_STATE, gw), F32), pltpu.VMEM((CHUNK, SSD_W), F32),
                        pltpu.VMEM((CHUNK, SSD_W), F32), pltpu.VMEM((CHUNK, SSD_W), F32),
                        pltpu.VMEM((8, SSD_W), F32)],
        compiler_params=_params(("arbitrary",)),
    )(xbc, dt_c, dt_tc, a, a_t, dskip_x, states, d_y)


def _gated_norm(o, gate, w):
    sg = _sigmoid(gate)
    p = o * (gate * sg)
    rs = lax.rsqrt(jnp.mean(p * p, axis=-1, keepdims=True) + EPS)
    n = p * rs
    return sg, rs, n, n * w


def _tail_fwd(o_sb, o_ssd, proj, h0, target, w_out, sb_w, ssd_w, fin_w):
    lp = o_sb.shape[0]
    nb = lp // TM
    row = lambda i: (i, 0)
    one = lambda i: (0, 0)

    def body(osb_ref, gate_ref, ossd_ref, z_ref, h0_ref, tgt_ref, wo_ref, sbw_ref, ssdw_ref, fw_ref,
             dh1_ref, loss_ref, gfw_ref):
        i = pl.program_id(0)

        @pl.when(i == 0)
        def _():
            loss_ref[...] = jnp.zeros_like(loss_ref)
            gfw_ref[...] = jnp.zeros_like(gfw_ref)

        y1 = _gated_norm(osb_ref[...], gate_ref[...], sbw_ref[...])[3]
        y2 = _gated_norm(ossd_ref[...], z_ref[...], ssdw_ref[...])[3]
        h1 = (h0_ref[...] + _mm(y1, wo_ref[0:SB_W, :])) + _mm(y2, wo_ref[SB_W:SB_W + SSD_W, :])
        rs1 = lax.rsqrt(jnp.mean(h1 * h1, axis=-1, keepdims=True) + EPS)
        n1 = h1 * rs1
        fw = fw_ref[...]
        diff = jnp.where(i > 0, n1 * fw - tgt_ref[...], 0.0)
        loss_ref[...] += jnp.sum(diff * diff, axis=0, keepdims=True)
        d_out = diff * (1.0 / D_MODEL)
        gfw_ref[...] += jnp.sum(d_out * n1, axis=0, keepdims=True)
        g = d_out * fw
        dh1_ref[...] = rs1 * (g - n1 * jnp.mean(g * n1, axis=-1, keepdims=True))

    return pl.pallas_call(
        body, name="tail_fwd", grid=(nb,),
        in_specs=[pl.BlockSpec((TM, SB_W), row),
                  pl.BlockSpec((TM, SB_W), lambda i: (i, (COL_GATE - QKV_W) // SB_W)),
                  pl.BlockSpec((TM, SSD_W), row),
                  pl.BlockSpec((TM, SSD_W), lambda i: (i, (COL_Z - QKV_W) // SSD_W)),
                  pl.BlockSpec((TM, D_MODEL), row),
                  pl.BlockSpec((TM, D_MODEL), lambda i: (jnp.maximum(i - 1, 0), 0)),
                  pl.BlockSpec(memory_space=_VMEM),
                  pl.BlockSpec((1, SB_W), one), pl.BlockSpec((1, SSD_W), one), pl.BlockSpec((1, D_MODEL), one)],
        out_specs=[pl.BlockSpec((TM, D_MODEL), row), pl.BlockSpec((1, D_MODEL), one), pl.BlockSpec((1, D_MODEL), one)],
        out_shape=[_SDS((lp, D_MODEL), F32), _SDS((1, D_MODEL), F32), _SDS((1, D_MODEL), F32)],
        compiler_params=_params(("arbitrary",), 40),
    )(o_sb, proj, o_ssd, proj, h0, target, w_out, sb_w, ssd_w, fin_w)


def _gated_norm_bwd(o, gate, w, dy):
    sg, rs, n, _ = _gated_norm(o, gate, w)
    gw = jnp.sum(dy * n, axis=0, keepdims=True)
    dn = dy * w
    dp = rs * (dn - n * jnp.mean(dn * n, axis=-1, keepdims=True))
    d_o = dp * (gate * sg)
    d_gate = dp * o * (sg * (1.0 + gate * (1.0 - sg)))
    return d_o, d_gate, gw, n * w


def _tail_bwd(o_sb, o_ssd, proj, d_h1, w_out, sb_w, ssd_w):
    lp = o_sb.shape[0]
    tm = 272 if lp % 272 == 0 else TM
    nb = lp // tm
    row = lambda i, t: (i, 0)
    one = lambda i, t: (0, 0)

    def body(osb_ref, gate_ref, ossd_ref, z_ref, dh1_ref, wo_ref, sbw_ref, ssdw_ref,
             dosb_ref, dossd_ref, dproj_ref, gwo_ref, gsb_ref, gssd_ref):
        i = pl.program_id(0)
        t = pl.program_id(1)

        @pl.when(jnp.logical_and(i == 0, t == 0))
        def _():
            gwo_ref[...] = jnp.zeros_like(gwo_ref)
            gsb_ref[...] = jnp.zeros_like(gsb_ref)
            gssd_ref[...] = jnp.zeros_like(gssd_ref)

        dh1 = dh1_ref[...].astype(_MXU)

        def half(o_ref, g_ref, w_ref, do_ref, gn_ref, r0):
            dy = lax.dot_general(dh1, wo_ref[r0:r0 + SB_W, :], _NT, preferred_element_type=F32)
            d_o, d_g, gw, y = _gated_norm_bwd(o_ref[...], g_ref[...], w_ref[...], dy)
            do_ref[...] = d_o
            dproj_ref[...] = d_g.astype(_MXU)
            gn_ref[...] += gw
            gwo_ref[r0:r0 + SB_W, :] += lax.dot_general(y.astype(_MXU), dh1, _TN, preferred_element_type=F32)

        @pl.when(t == 0)
        def _():
            half(osb_ref, gate_ref, sbw_ref, dosb_ref, gsb_ref, 0)

        @pl.when(t == 1)
        def _():
            half(ossd_ref, z_ref, ssdw_ref, dossd_ref, gssd_ref, SB_W)

    tile = pl.BlockSpec((tm, SB_W), row)
    return pl.pallas_call(
        body, name="tail_bwd", grid=(nb, 2),
        in_specs=[tile, pl.BlockSpec((tm, SB_W), lambda i, t: (i, (COL_GATE - QKV_W) // SB_W)),
                  tile, pl.BlockSpec((tm, SSD_W), lambda i, t: (i, (COL_Z - QKV_W) // SSD_W)),
                  tile, pl.BlockSpec(memory_space=_VMEM),
                  pl.BlockSpec((1, SB_W), one), pl.BlockSpec((1, SSD_W), one)],
        out_specs=[tile, tile, pl.BlockSpec((tm, SB_W), lambda i, t: (i, COL_GATE // SB_W + t)),
                   pl.BlockSpec((SB_W + SSD_W, D_MODEL), one), pl.BlockSpec((1, SB_W), one), pl.BlockSpec((1, SSD_W), one)],
        out_shape=[_SDS((lp, SB_W), F32), _SDS((lp, SSD_W), F32), _SDS((lp, W_ALL), _MXU),
                   _SDS((SB_W + SSD_W, D_MODEL), F32), _SDS((1, SB_W), F32), _SDS((1, SSD_W), F32)],
        compiler_params=_params(("arbitrary", "arbitrary"), 48),
    )(o_sb, proj, o_ssd, proj, d_h1, w_out, sb_w, ssd_w)


def _d_u(d_proj, w_t, send=(), dests=()):
    lp = d_proj.shape[0]
    tk = 512
    steps = N_MAIN // tk
    n = len(send)
    host_in, host_out, host_shapes, host_sems = _host_specs(send)

    def body(dp_ref, w_ref, dpdt_ref, wdt_ref, *rest):
        srcs, o_ref, dsts, sems = rest[:n], rest[n], rest[n + 1:2 * n + 1], rest[2 * n + 1:]
        j = pl.program_id(0)

        @pl.when(j == 0)
        def _():
            if n:
                _start_copies(_slab_copies(srcs, dsts, dests, *sems))
            o_ref[...] = jnp.dot(dpdt_ref[...], wdt_ref[...], preferred_element_type=F32)

        o_ref[...] += jnp.dot(dp_ref[...], w_ref[...], preferred_element_type=F32)

        if n:
            @pl.when(j == steps - 1)
            def _():
                _finish_copies(_slab_copies(srcs, dsts, dests, *sems))

    res = pl.pallas_call(
        body, name="d_u", grid=(steps,),
        in_specs=[pl.BlockSpec((lp, tk), lambda j: (0, j)),
                  pl.BlockSpec((tk, D_MODEL), lambda j: (j, 0)),
                  pl.BlockSpec((lp, 128), lambda j: (0, N_MAIN // 128)),
                  pl.BlockSpec((128, D_MODEL), lambda j: (N_MAIN // 128, 0))] + host_in,
        out_specs=[pl.BlockSpec((lp, D_MODEL), lambda j: (0, 0))] + host_out,
        out_shape=[_SDS((lp, D_MODEL), F32)] + host_shapes,
        scratch_shapes=host_sems,
        compiler_params=_params(("arbitrary",), 48),
    )(d_proj, w_t, d_proj, w_t, *send)
    return res[0], list(res[1:])


def _norm_bwd(du_all, h0, d_h1, norm_w):
    lp = h0.shape[0]
    nb = lp // TM
    seq = lp - OFF
    row = lambda i: (i, 0)
    one = lambda i: (0, 0)

    def body(du_ref, h0_ref, dh1_ref, nw_ref, gx_ref, gmeta_ref, gnw_ref):
        i = pl.program_id(0)

        @pl.when(i == 0)
        def _():
            gnw_ref[...] = jnp.zeros_like(gnw_ref)

        du = du_ref[...]
        h = h0_ref[...]
        rs = lax.rsqrt(jnp.mean(h * h, axis=-1, keepdims=True) + EPS)
        n0 = h * rs
        gnw_ref[...] += jnp.sum(du * n0, axis=0, keepdims=True)
        g = du * nw_ref[...]
        dh0 = dh1_ref[...] + rs * (g - n0 * jnp.mean(g * n0, axis=-1, keepdims=True))

        @pl.when(i == 0)
        def _():
            gmeta_ref[...] = dh0[PAD:PAD + N_META, :]

        @pl.when(i > 0)
        def _():
            gx_ref[...] = dh0

    tile = pl.BlockSpec((TM, D_MODEL), row)
    return pl.pallas_call(
        body, name="norm_bwd", grid=(nb,),
        in_specs=[tile, tile, tile, pl.BlockSpec((1, D_MODEL), one)],
        out_specs=[pl.BlockSpec((TM, D_MODEL), lambda i: (jnp.maximum(i - 1, 0), 0)),
                   pl.BlockSpec((N_META, D_MODEL), one), pl.BlockSpec((1, D_MODEL), one)],
        out_shape=[_SDS((seq, D_MODEL), F32), _SDS((N_META, D_MODEL), F32), _SDS((1, D_MODEL), F32)],
        compiler_params=_params(("arbitrary",)),
    )(du_all, h0, d_h1, norm_w)


def _grad_w_windows(u_t, d_proj, first, count, name):
    lp = d_proj.shape[0]
    hw = WIN_W // 2
    steps = 2 * count

    def body(ut_ref, dp_hbm, o_ref, buf, sems):
        s = pl.program_id(0)
        slot = s % 2

        def fetch(step, sl):
            start = pl.multiple_of((first + step // 2) * WIN_STEP + (step % 2) * hw, 128)
            return pltpu.make_async_copy(dp_hbm.at[:, pl.ds(start, hw)], buf.at[sl], sems.at[sl])

        @pl.when(s == 0)
        def _():
            fetch(0, 0).start()

        @pl.when(s + 1 < steps)
        def _():
            fetch(s + 1, 1 - slot).start()

        fetch(s, slot).wait()
        o_ref[0] = jnp.dot(ut_ref[...], buf[slot], preferred_element_type=F32).astype(o_ref.dtype)

    return pl.pallas_call(
        body, name=name, grid=(steps,),
        in_specs=[pl.BlockSpec((D_MODEL, lp), lambda s: (0, 0)), pl.BlockSpec(memory_space=pl.ANY)],
        out_specs=pl.BlockSpec((1, D_MODEL, hw), lambda s: (s // 2, 0, s % 2)),
        out_shape=_SDS((count, D_MODEL, WIN_W), _MXU),
        scratch_shapes=[pltpu.VMEM((2, lp, hw), _MXU), pltpu.SemaphoreType.DMA((2,))],
        compiler_params=_params(("arbitrary",), 40),
    )(u_t, d_proj)


def _device_grads(x2d, target2d, meta_full, norm_w, w_t, conv_w, conv_b, dt_bias, a_log, d_skip,
                  sb_w, ssd_w, w_out, fin_w, exchange=None):
    lp = x2d.shape[0] + OFF
    nc = lp // CHUNK
    h0, u, u_t = _prep(x2d, meta_full, norm_w)
    qkv, proj, dt_raw = _inproj(u, w_t)
    o_sb, o_lo = _sb_fwd(qkv)
    dt_bias128 = jnp.pad(dt_bias, ((0, 0), (0, 128 - N_HEADS)))
    xbc, dt128 = _conv_fwd(proj, dt_raw, conv_w, conv_b, dt_bias128)
    dt_c = dt128[:, :N_HEADS].reshape(nc, CHUNK, N_HEADS)
    dt_tc = jnp.swapaxes(dt_c, 1, 2)
    a = -jnp.exp(a_log)
    a_t = a.reshape(N_HEADS, 1)
    dskip_x = jnp.repeat(d_skip, HEAD, axis=1)
    o_ssd, states = _ssd_fwd(xbc, dt_c, dt_tc, a, a_t, dskip_x)
    d_h1, sq_err, g_fin = _tail_fwd(o_sb, o_ssd, proj, h0, target2d, w_out, sb_w, ssd_w, fin_w)

    d_osb, d_ossd, d_proj, g_wout, g_sb, g_ssd = _tail_bwd(o_sb, o_ssd, proj, d_h1, w_out, sb_w, ssd_w)
    d_xbc_act, ddt_a, ddt_b, ga1, ga2, gd = _ssd_bwd(xbc, dt_c, dt_tc, a, a_t, dskip_x, states, d_ossd)
    d_dt = (ddt_a + jnp.swapaxes(ddt_b, 1, 2)).reshape(lp, N_HEADS)
    d_dt128 = jnp.pad(d_dt, ((0, 0), (0, 128 - N_HEADS)))
    d_proj, g_convw, g_convb, g_dtb128 = _conv_bwd(proj, dt_raw, conv_w, conv_b, dt_bias128, d_xbc_act, d_dt128, d_proj)
    send_e, dests_e = ((), ()) if exchange is None else exchange["early"](g_wout)
    d_proj, arrived_e = _sb_bwd(qkv, o_sb, o_lo, d_osb, d_proj, send_e, dests_e)
    g_win = _grad_w_windows(u_t, d_proj, 0, N_CHIPS, "grad_w_in")
    send_l, dests_l = ((), ()) if exchange is None else exchange["late"](g_win)
    d_u, arrived_l = _d_u(d_proj, w_t, send_l, dests_l)
    send, arrived = tuple(send_e) + tuple(send_l), tuple(arrived_e) + tuple(arrived_l)
    g_x, g_meta, g_nw = _norm_bwd(d_u, h0, d_h1, norm_w)
    g_alog = (ga1 + ga2.reshape(1, N_HEADS)) * a
    g_dskip = gd.reshape(N_HEADS, HEAD).sum(axis=1).reshape(1, N_HEADS)
    grads = dict(meta_tokens=g_meta, norm_w=g_nw, w_in=g_win, conv_w=g_convw, conv_b=g_convb,
                 dt_bias=g_dtb128[:, :N_HEADS], a_log=g_alog, d_skip=g_dskip, sb_norm_w=g_sb, ssd_norm_w=g_ssd,
                 w_out=g_wout, final_norm_w=g_fin, sent=send, arrived=arrived)
    return sq_err, g_x, grads


_MESH = pl.DeviceIdType.MESH
_ANY = pl.BlockSpec(memory_space=pl.ANY)


def _place():
    return lax.axis_index("x"), lax.axis_index("y"), lax.axis_index("c")


def _other_chips(x, y):
    return ((1 - x, y), (x, 1 - y), (1 - x, 1 - y))


def _gather_shards(arrays, n_big):
    n = len(arrays)

    def body(*refs):
        srcs, dsts = refs[:n], refs[n:2 * n]
        send_sems, recv_sems, fwd_send, fwd_recv = refs[2 * n:]
        x, y, c = _place()
        mine = 2 * x + y
        chips = _other_chips(x, y)

        def window(a):
            half = arrays[a].shape[1] // 2
            return pl.ds(pl.multiple_of(c * half, 128), half)

        first = []
        for a in range(n):
            for k, (px, py) in enumerate(chips):
                if a < n_big:
                    src, dst = srcs[a].at[:, window(a)], dsts[a].at[mine, :, window(a)]
                else:
                    src, dst = srcs[a], dsts[a].at[mine]
                cp = pltpu.make_async_remote_copy(
                    src_ref=src, dst_ref=dst, send_sem=send_sems.at[a * 3 + k], recv_sem=recv_sems.at[a * 3 + k],
                    device_id=(px, py, c), device_id_type=_MESH)
                cp.start()
                first.append(cp)
        passed = []
        for a in range(n):
            for k, (px, py) in enumerate(chips):
                first[a * 3 + k].wait_recv()
                if a < n_big:
                    landed = dsts[a].at[2 * px + py, :, window(a)]
                    cp = pltpu.make_async_remote_copy(
                        src_ref=landed, dst_ref=landed, send_sem=fwd_send.at[a * 3 + k], recv_sem=fwd_recv.at[a * 3 + k],
                        device_id=(x, y, 1 - c), device_id_type=_MESH)
                    cp.start()
                    passed.append(cp)
        for cp in passed:
            cp.wait_recv()
        for cp in first + passed:
            cp.wait_send()

    got = pl.pallas_call(
        body, name="gather_shards",
        in_specs=[_ANY] * n, out_specs=[_ANY] * n,
        out_shape=[_SDS((N_CHIPS,) + a.shape, a.dtype) for a in arrays],
        scratch_shapes=[pltpu.SemaphoreType.DMA((3 * n,)), pltpu.SemaphoreType.DMA((3 * n,)),
                        pltpu.SemaphoreType.DMA((3 * n_big,)), pltpu.SemaphoreType.DMA((3 * n_big,))],
    )(*arrays)
    mine = 2 * lax.axis_index("x") + lax.axis_index("y")
    return [lax.dynamic_update_slice(g, a[None], (mine,) + (0,) * a.ndim) for g, a in zip(got, arrays)]


def _slab_copies(srcs, dsts, dests, send_sems, recv_sems):
    x, y, c = _place()
    mine = 2 * x + y
    copies = []
    for a in range(len(srcs)):
        lo, hi = dests[a]
        receives = jnp.logical_and(mine >= lo, mine < hi)
        for k, (px, py) in enumerate(_other_chips(x, y)):
            target = 2 * px + py
            cp = pltpu.make_async_remote_copy(
                src_ref=srcs[a].at[jnp.clip(target - lo, 0, hi - lo - 1)], dst_ref=dsts[a].at[mine],
                send_sem=send_sems.at[a * 3 + k], recv_sem=recv_sems.at[a * 3 + k],
                device_id=(px, py, c), device_id_type=_MESH)
            copies.append((cp, jnp.logical_and(target >= lo, target < hi), receives))
    return copies


def _start_copies(copies):
    for cp, sends, _ in copies:
        pl.when(sends)(cp.start)


def _finish_copies(copies):
    for cp, _, receives in copies:
        pl.when(receives)(cp.wait_recv)
    for cp, sends, _ in copies:
        pl.when(sends)(cp.wait_send)


def _host_specs(send):
    n = len(send)
    hbm = [pl.BlockSpec(memory_space=pl.ANY)] * n
    shapes = [_SDS((N_CHIPS,) + a.shape[1:], a.dtype) for a in send]
    sems = [pltpu.SemaphoreType.DMA((3 * n,)), pltpu.SemaphoreType.DMA((3 * n,))] if n else []
    return hbm, hbm, shapes, sems


def _swap_halves(arrays, name):
    n = len(arrays)

    def body(*refs):
        srcs, dsts = refs[:n], refs[n:2 * n]
        send_sems, recv_sems = refs[2 * n:]
        x, y, c = _place()
        copies = []
        for a in range(n):
            half = arrays[a].shape[1] // 2
            cp = pltpu.make_async_remote_copy(
                src_ref=srcs[a].at[:, pl.ds(pl.multiple_of((1 - c) * half, 16), half)], dst_ref=dsts[a],
                send_sem=send_sems.at[a], recv_sem=recv_sems.at[a],
                device_id=(x, y, 1 - c), device_id_type=_MESH)
            cp.start()
            copies.append(cp)
        for cp in copies:
            cp.wait_recv()
        for cp in copies:
            cp.wait_send()

    return pl.pallas_call(
        body, name=name,
        in_specs=[_ANY] * n, out_specs=[_ANY] * n,
        out_shape=[_SDS((a.shape[0], a.shape[1] // 2, a.shape[2]), a.dtype) for a in arrays],
        scratch_shapes=[pltpu.SemaphoreType.DMA((n,)), pltpu.SemaphoreType.DMA((n,))],
    )(*arrays)


N_DEV = 8
SMALL_ROWS = 32
SMALL_COLS = XBC_W


def _final_exchange(arrays, by_cols, packed):
    n = len(arrays)

    def body(*refs):
        src_ref = refs[n]
        dsts = refs[n + 1:2 * n + 1]
        dst_ref = refs[2 * n + 1]
        send_sems, recv_sems, all_send, all_recv, local_sem = refs[2 * n + 2:]
        x, y, c = _place()
        me = 4 * x + 2 * y + c
        own = pltpu.make_async_copy(src_ref, dst_ref.at[me], local_sem)
        own.start()
        copies = []
        for k in range(1, N_DEV):
            bx, by, bc = (k >> 2) & 1, (k >> 1) & 1, k & 1
            peer = (x + bx - 2 * x * bx, y + by - 2 * y * by, c + bc - 2 * c * bc)
            cp = pltpu.make_async_remote_copy(
                src_ref=src_ref, dst_ref=dst_ref.at[me], send_sem=all_send.at[k - 1], recv_sem=all_recv.at[k - 1],
                device_id=peer, device_id_type=_MESH)
            cp.start()
            copies.append(cp)
        for a in range(n):
            if by_cols[a]:
                half = arrays[a].shape[1] // 2
                mine = dsts[a].at[:, pl.ds(pl.multiple_of(c * half, 128), half)]
            else:
                half = arrays[a].shape[0] // 2
                mine = dsts[a].at[pl.ds(pl.multiple_of(c * half, 16), half)]
            cp = pltpu.make_async_remote_copy(
                src_ref=mine, dst_ref=mine, send_sem=send_sems.at[a], recv_sem=recv_sems.at[a],
                device_id=(x, y, 1 - c), device_id_type=_MESH)
            cp.start()
            copies.append(cp)
        for cp in copies:
            cp.wait_recv()
        for cp in copies:
            cp.wait_send()
        own.wait()

    vm = pl.BlockSpec(memory_space=_VMEM)
    res = pl.pallas_call(
        body, name="final_exchange",
        in_specs=[_ANY] * n + [vm], out_specs=[_ANY] * n + [vm],
        out_shape=[_SDS(a.shape, a.dtype) for a in arrays] + [_SDS((N_DEV, SMALL_ROWS, SMALL_COLS), F32)],
        input_output_aliases={a: a for a in range(n)},
        scratch_shapes=[pltpu.SemaphoreType.DMA((n,)), pltpu.SemaphoreType.DMA((n,)),
                        pltpu.SemaphoreType.DMA((N_DEV - 1,)), pltpu.SemaphoreType.DMA((N_DEV - 1,)),
                        pltpu.SemaphoreType.DMA],
    )(*arrays, packed)
    return list(res[:n]), res[n]


def _adamw(w, g, m, v):
    m = ADAM_B1 * m + (1.0 - ADAM_B1) * g
    v = ADAM_B2 * v + (1.0 - ADAM_B2) * (g * g)
    m_hat = m / (1.0 - ADAM_B1 ** ADAM_STEP)
    v_hat = v / (1.0 - ADAM_B2 ** ADAM_STEP)
    delta = -ADAM_LR * (m_hat / (jnp.sqrt(v_hat) + ADAM_EPS) + ADAM_WD * w)
    return delta, m, v


def _sum_slabs(slabs, core, name, transposed=False):
    _, h, c = slabs.shape
    tr = 128
    nblk = h // tr

    def body(core_ref, s_ref, o_ref):
        tot = ((s_ref[0].astype(F32) + s_ref[1].astype(F32)) + s_ref[2].astype(F32)) + s_ref[3].astype(F32)
        o_ref[...] = tot.T if transposed else tot

    if transposed:
        out_spec = pl.BlockSpec((c, tr), lambda i, core_ref: (0, core_ref[0] * nblk + i))
        out_shape = _SDS((c, 2 * h), F32)
    else:
        out_spec = pl.BlockSpec((tr, c), lambda i, core_ref: (core_ref[0] * nblk + i, 0))
        out_shape = _SDS((2 * h, c), F32)
    grid_spec = pltpu.PrefetchScalarGridSpec(
        num_scalar_prefetch=1, grid=(nblk,),
        in_specs=[pl.BlockSpec((N_CHIPS, tr, c), lambda i, core_ref: (0, i, 0))],
        out_specs=out_spec)
    return pl.pallas_call(
        body, name=name, grid_spec=grid_spec, out_shape=out_shape,
        compiler_params=_params(("arbitrary",)),
    )(core, slabs)


def _add_halves(own, recv, core, name):
    ns, r, c = own.shape
    half = r // 2
    tr = 128
    nblk = half // tr

    def body(core_ref, a_ref, b_ref, o_ref):
        o_ref[...] = (a_ref[...].astype(F32) + b_ref[...].astype(F32)).astype(o_ref.dtype)

    grid_spec = pltpu.PrefetchScalarGridSpec(
        num_scalar_prefetch=1, grid=(nblk,),
        in_specs=[pl.BlockSpec((ns, tr, c), lambda i, core_ref: (0, core_ref[0] * nblk + i, 0)),
                  pl.BlockSpec((ns, tr, c), lambda i, core_ref: (0, i, 0))],
        out_specs=pl.BlockSpec((ns, tr, c), lambda i, core_ref: (0, i, 0)))
    return pl.pallas_call(
        body, name=name, grid_spec=grid_spec, out_shape=_SDS((ns, half, c), own.dtype),
        compiler_params=_params(("arbitrary",)),
    )(core, own, recv)


def _update_big(w, m, v, g, name):
    r, c = w.shape

    def body(w_ref, m_ref, v_ref, g_ref, d_ref, mo_ref, vo_ref):
        delta, m_new, v_new = _adamw(w_ref[...], g_ref[...], m_ref[...], v_ref[...])
        d_ref[...] = delta
        mo_ref[...] = m_new
        vo_ref[...] = v_new

    if r % 128 == 0:
        steps, spec = r // 128, pl.BlockSpec((128, c), lambda i: (i, 0))
    else:
        steps, spec = c // 128, pl.BlockSpec((r, 128), lambda i: (0, i))
    return pl.pallas_call(
        body, name=name, grid=(steps,),
        in_specs=[spec] * 4, out_specs=[spec] * 3,
        out_shape=[_SDS((r, c), F32)] * 3,
        compiler_params=_params(("arbitrary",)),
    )(w, m, v, g)


_ROW = dict(norm_w=0, sb_norm_w=1, ssd_norm_w=2, final_norm_w=3, conv_b=4, dt_bias=5, a_log=6, d_skip=7,
            conv_w=8, sq_err=12, meta_tokens=16)
_SMALL = ("meta_tokens", "norm_w", "conv_w", "conv_b", "dt_bias", "a_log", "d_skip", "sb_norm_w", "ssd_norm_w",
          "final_norm_w")


def _pack_small(sq_err, grads):
    def rowpad(a):
        return jnp.pad(a, ((0, 0), (0, SMALL_COLS - a.shape[1])))

    rows = [rowpad(grads[k]) for k in ("norm_w", "sb_norm_w", "ssd_norm_w", "final_norm_w", "conv_b", "dt_bias", "a_log", "d_skip")]
    rows.append(grads["conv_w"])
    rows.append(rowpad(sq_err))
    rows.append(jnp.zeros((3, SMALL_COLS), F32))
    rows.append(rowpad(grads["meta_tokens"]))
    return jnp.concatenate(rows, axis=0)


def _update_small(gathered, ws, ms, vs):
    names = _SMALL
    n = len(names)

    def body(*refs):
        g_ref = refs[0]
        w_refs, m_refs, v_refs = refs[1:1 + n], refs[1 + n:1 + 2 * n], refs[1 + 2 * n:1 + 3 * n]
        outs = refs[1 + 3 * n:]
        loss_ref = outs[0]
        go, do, mo, vo = outs[1:1 + n], outs[1 + n:1 + 2 * n], outs[1 + 2 * n:1 + 3 * n], outs[1 + 3 * n:1 + 4 * n]
        tot = g_ref[0]
        for d in range(1, N_DEV):
            tot = tot + g_ref[d]
        x, y, _ = _place()
        chip = 2 * x + y
        loss_ref[...] = jnp.broadcast_to(
            0.5 * jnp.sum(tot[_ROW["sq_err"]:_ROW["sq_err"] + 1, 0:D_MODEL], axis=1, keepdims=True) / D_MODEL, (1, 128))
        for idx, nm in enumerate(names):
            r0 = _ROW[nm]
            rows, cols = w_refs[idx].shape
            if nm in ("conv_w", "meta_tokens"):
                g = jnp.zeros((rows, cols), F32)
                for j in range(N_CHIPS):
                    g = g + jnp.where(chip == j, tot[r0:r0 + rows, j * cols:(j + 1) * cols], 0.0)
            else:
                g = tot[r0:r0 + rows, 0:cols]
            delta, m_new, v_new = _adamw(w_refs[idx][...], g, m_refs[idx][...], v_refs[idx][...])
            go[idx][...] = g
            do[idx][...] = delta
            mo[idx][...] = m_new
            vo[idx][...] = v_new

    shapes = [_SDS(ws[nm].shape, F32) for nm in names]
    vm = pl.BlockSpec(memory_space=_VMEM)
    res = pl.pallas_call(
        body, name="update_small",
        in_specs=[vm] * (1 + 3 * n), out_specs=[vm] * (1 + 4 * n),
        out_shape=[_SDS((1, 128), F32)] + shapes * 4,
    )(gathered, *[ws[nm] for nm in names], *[ms[nm] for nm in names], *[vs[nm] for nm in names])
    loss = res[0][0, 0]
    g = dict(zip(names, res[1:1 + n]))
    d = dict(zip(names, res[1 + n:1 + 2 * n]))
    m = dict(zip(names, res[1 + 2 * n:1 + 3 * n]))
    v = dict(zip(names, res[1 + 3 * n:1 + 4 * n]))
    return loss, g, d, m, v


_WEIGHTS = ("meta_tokens", "norm_w", "w_in", "conv_w", "conv_b", "dt_bias", "a_log", "d_skip", "sb_norm_w",
            "ssd_norm_w", "w_out", "final_norm_w")


def kernel(x, meta_tokens, norm_w, w_in, conv_w, conv_b, dt_bias, a_log, d_skip, sb_norm_w, ssd_norm_w, w_out, final_norm_w, loss_target, m_meta_tokens, m_norm_w, m_w_in, m_conv_w, m_conv_b, m_dt_bias, m_a_log, m_d_skip, m_sb_norm_w, m_ssd_norm_w, m_w_out, m_final_norm_w, v_meta_tokens, v_norm_w, v_w_in, v_conv_w, v_conv_b, v_dt_bias, v_a_log, v_d_skip, v_sb_norm_w, v_ssd_norm_w, v_w_out, v_final_norm_w):
    given = dict(meta_tokens=meta_tokens, norm_w=norm_w, w_in=w_in, conv_w=conv_w, conv_b=conv_b, dt_bias=dt_bias,
                 a_log=a_log, d_skip=d_skip, sb_norm_w=sb_norm_w, ssd_norm_w=ssd_norm_w, w_out=w_out,
                 final_norm_w=final_norm_w)
    mom = dict(meta_tokens=m_meta_tokens, norm_w=m_norm_w, w_in=m_w_in, conv_w=m_conv_w, conv_b=m_conv_b,
               dt_bias=m_dt_bias, a_log=m_a_log, d_skip=m_d_skip, sb_norm_w=m_sb_norm_w, ssd_norm_w=m_ssd_norm_w,
               w_out=m_w_out, final_norm_w=m_final_norm_w)
    var = dict(meta_tokens=v_meta_tokens, norm_w=v_norm_w, w_in=v_w_in, conv_w=v_conv_w, conv_b=v_conv_b,
               dt_bias=v_dt_bias, a_log=v_a_log, d_skip=v_d_skip, sb_norm_w=v_sb_norm_w, ssd_norm_w=v_ssd_norm_w,
               w_out=v_w_out, final_norm_w=v_final_norm_w)
    seq = x.shape[1]

    def two_d(a):
        return a.reshape((-1, a.shape[-1])) if a.ndim != 2 else a

    def rows_first(a):
        return jnp.transpose(a, (2, 0, 1)).reshape(W_IN_SHARD, D_MODEL)

    def rows_last(a):
        return jnp.transpose(a.reshape(W_IN_SHARD, 1, D_MODEL), (1, 2, 0))

    w_in_t, m_in_t, v_in_t = rows_first(w_in), rows_first(m_w_in), rows_first(v_w_in)

    g_win, g_wout, g_meta, g_cw = _gather_shards(
        [w_in_t.astype(_MXU), w_out[0].astype(_MXU), meta_tokens, conv_w[0]], 2)
    w_t = jnp.pad(g_win.reshape(D_IN, D_MODEL), ((0, W_ALL - D_IN), (0, 0)))
    w_out_full = g_wout.reshape(2 * D_MODEL, D_MODEL)
    meta_full = jnp.swapaxes(g_meta, 0, 1).reshape(N_META, D_MODEL)
    conv_w_full = jnp.swapaxes(g_cw, 0, 1).reshape(4, XBC_W)

    core = lax.axis_index("c").astype(jnp.int32).reshape(1)

    def early(g_wout):
        slab_out = g_wout.reshape(N_CHIPS, W_OUT_SHARD, D_MODEL).astype(_MXU)
        (sib_out,) = _swap_halves([slab_out], "swap_halves_w_out")
        return (_add_halves(slab_out, sib_out, core, "chip_sum_w_out"),), ((0, N_CHIPS),)

    def late(g_win):
        (sib_in,) = _swap_halves([g_win], "swap_halves_w_in")
        return (_add_halves(g_win, sib_in, core, "chip_sum_w_in"),), ((0, N_CHIPS),)

    sq_err, g_x, grads = _device_grads(
        x.reshape(seq, D_MODEL), loss_target.reshape(seq, D_MODEL), meta_full, norm_w, w_t, conv_w_full,
        conv_b, dt_bias, a_log, d_skip, sb_norm_w, ssd_norm_w, w_out_full, final_norm_w.reshape(1, D_MODEL),
        exchange=dict(early=early, late=late))
    chip_out, chip_in = grads["sent"]
    got_out, got_in = grads["arrived"]
    chip = 2 * lax.axis_index("x") + lax.axis_index("y")

    def with_own(got, sent):
        own = lax.dynamic_slice(sent, (chip, 0, 0), (1,) + sent.shape[1:])
        return lax.dynamic_update_slice(got, own, (chip, 0, 0))

    (g_in, g_out), gathered = _final_exchange(
        [_sum_slabs(with_own(got_in, chip_in), core, "sum_w_in", transposed=True),
         _sum_slabs(with_own(got_out, chip_out), core, "sum_w_out")], (True, False), _pack_small(sq_err, grads))
    g_in = lax.dynamic_slice(g_in, (4 * chip, 0), (W_IN_SHARD, D_MODEL))
    big = dict(w_in=tuple(rows_last(a) for a in (g_in,) + tuple(_update_big(w_in_t, m_in_t, v_in_t, g_in, "update_w_in"))),
               w_out=(g_out,) + tuple(_update_big(w_out[0], m_w_out[0], v_w_out[0], g_out, "update_w_out")))

    loss, sg, sd, sm, sv = _update_small(
        gathered, {k: two_d(given[k]) for k in _SMALL}, {k: two_d(mom[k]) for k in _SMALL},
        {k: two_d(var[k]) for k in _SMALL})

    out = {}
    for idx, group in enumerate((sg, sd, sm, sv)):
        for k in _SMALL:
            out[(idx, k)] = group[k].reshape(given[k].shape)
        for k in ("w_in", "w_out"):
            out[(idx, k)] = big[k][idx].reshape(given[k].shape)
    return (loss, g_x.reshape(x.shape), *[out[(idx, k)] for idx in range(4) for k in _WEIGHTS])
```

```python
import functools
import math

import jax
import jax.numpy as jnp
from jax import lax
from jax.experimental import pallas as pl
from jax.experimental.pallas import tpu as pltpu

F32 = jnp.float32
_MXU = jnp.bfloat16

D_MODEL = 1024
N_META = 16
PAD = 112
OFF = PAD + N_META
TM = 128
CHUNK = 64
SB_W = 1024
SSD_W = 1024
N_HEADS = 16
HEAD = 64
N_GROUPS = 2
N_STATE = 128
XBC_W = SSD_W + 2 * N_GROUPS * N_STATE
N_MAIN = 4 * SB_W + SSD_W + XBC_W
QKV_W = 3 * SB_W
REST_W = N_MAIN - QKV_W
COL_GATE = 3 * SB_W
COL_Z = 4 * SB_W
COL_XBC = 5 * SB_W
D_IN = N_MAIN + N_HEADS
W_ALL = N_MAIN + 128
WIN_STEP = 1664
WIN_W = 1792
EPS = 1e-5
N_CHIPS = 4
W_IN_SHARD = D_IN // N_CHIPS
W_OUT_SHARD = 2 * D_MODEL // N_CHIPS

ADAM_LR = 0.001
ADAM_B1 = 0.9
ADAM_B2 = 0.999
ADAM_EPS = 1e-08
ADAM_WD = 0.01
ADAM_STEP = 10

_SDS = jax.ShapeDtypeStruct
_NT = (((1,), (1,)), ((), ()))
_TN = (((0,), (0,)), ((), ()))
_VMEM = pltpu.VMEM


def _params(sem=None, vmem_mb=None):
    kw = {}
    if sem is not None:
        kw["dimension_semantics"] = sem
    if vmem_mb is not None:
        kw["vmem_limit_bytes"] = vmem_mb * 1024 * 1024
    return pltpu.CompilerParams(**kw)


def _mm(a, b):
    return jnp.dot(a.astype(_MXU), b.astype(_MXU), preferred_element_type=F32)


def _mm_nt(a, b):
    return lax.dot_general(a.astype(_MXU), b.astype(_MXU), _NT, preferred_element_type=F32)


def _mm_tn(a, b):
    return lax.dot_general(a.astype(_MXU), b.astype(_MXU), _TN, preferred_element_type=F32)


def _split(x, parts):
    out = []
    r = x
    for _ in range(parts):
        p = r.astype(_MXU)
        out.append(p)
        r = r - p.astype(F32)
    return out


def _sel_right(x, m01, parts=3):
    acc = None
    for p in _split(x, parts):
        t = jnp.dot(p, m01, preferred_element_type=F32)
        acc = t if acc is None else acc + t
    return acc


def _sel_left(m01, x, parts=3):
    acc = None
    for p in _split(x, parts):
        t = jnp.dot(m01, p, preferred_element_type=F32)
        acc = t if acc is None else acc + t
    return acc


def _iota(shape, axis):
    return lax.broadcasted_iota(jnp.int32, shape, axis)


def _sigmoid(x):
    return 1.0 / (1.0 + jnp.exp(-x))


def _prep(x2d, meta_full, norm_w):
    seq = x2d.shape[0]
    lp = seq + OFF
    nb = lp // TM

    def body(x_ref, meta_ref, w_ref, h0_ref, u_ref, ut_ref):
        i = pl.program_id(0)

        @pl.when(i == 0)
        def _():
            h0_ref[...] = jnp.concatenate([jnp.zeros((PAD, D_MODEL), F32), meta_ref[...]], axis=0)

        @pl.when(i > 0)
        def _():
            h0_ref[...] = x_ref[...]

        h = h0_ref[...]
        rs = lax.rsqrt(jnp.mean(h * h, axis=-1, keepdims=True) + EPS)
        u = (h * rs * w_ref[...]).astype(_MXU)
        u_ref[...] = u
        ut_ref[...] = u.T

    return pl.pallas_call(
        body, name="prep", grid=(nb,),
        in_specs=[pl.BlockSpec((TM, D_MODEL), lambda i: (jnp.maximum(i - 1, 0), 0)),
                  pl.BlockSpec((N_META, D_MODEL), lambda i: (0, 0)),
                  pl.BlockSpec((1, D_MODEL), lambda i: (0, 0))],
        out_specs=[pl.BlockSpec((TM, D_MODEL), lambda i: (i, 0)),
                   pl.BlockSpec((TM, D_MODEL), lambda i: (i, 0)),
                   pl.BlockSpec((D_MODEL, TM), lambda i: (0, i))],
        out_shape=[_SDS((lp, D_MODEL), F32), _SDS((lp, D_MODEL), _MXU), _SDS((D_MODEL, lp), _MXU)],
        compiler_params=_params(("arbitrary",)),
    )(x2d, meta_full, norm_w)


def _inproj(u, w_t):
    lp = u.shape[0]
    tn = 512

    nq = QKV_W // tn

    def body(u_ref, w_ref, wdt_ref, qkv_ref, rest_ref, odt_ref):
        j = pl.program_id(0)
        res = lax.dot_general(u_ref[...], w_ref[...], _NT, preferred_element_type=F32)

        @pl.when(j < nq)
        def _():
            qkv_ref[...] = res.astype(qkv_ref.dtype)

        @pl.when(j >= nq)
        def _():
            rest_ref[...] = res

        @pl.when(j == 0)
        def _():
            odt_ref[...] = lax.dot_general(u_ref[...], wdt_ref[...], _NT, preferred_element_type=F32)

    return pl.pallas_call(
        body, name="inproj", grid=(N_MAIN // tn,),
        in_specs=[pl.BlockSpec((lp, D_MODEL), lambda j: (0, 0)),
                  pl.BlockSpec((tn, D_MODEL), lambda j: (j, 0)),
                  pl.BlockSpec((128, D_MODEL), lambda j: (N_MAIN // 128, 0))],
        out_specs=[pl.BlockSpec((lp, tn), lambda j: (0, jnp.minimum(j, nq - 1))),
                   pl.BlockSpec((lp, tn), lambda j: (0, jnp.maximum(j - nq, 0))),
                   pl.BlockSpec((lp, 128), lambda j: (0, 0))],
        out_shape=[_SDS((lp, QKV_W), _MXU), _SDS((lp, REST_W), F32), _SDS((lp, 128), F32)],
        compiler_params=_params(("arbitrary",), 48),
    )(u, w_t, w_t)


SB_WINDOW = 3
SB_TOP = 16
SB_DEAD = -104.0


def _sb_logs(qh, kwin):
    z = lax.dot_general(qh, kwin, _NT, preferred_element_type=F32)
    e = jnp.exp(-jnp.abs(z))
    l1p = jnp.log(1.0 + e)
    lk_full = -(jnp.maximum(z, 0.0) + l1p)
    ls = jnp.minimum(z, 0.0) - l1p
    return z, e, ls, lk_full


def _blk(a, b):
    return a[:, b * TM:(b + 1) * TM]


def _stacked_sel(blocks, m01, parts):
    n = len(blocks)
    rows = blocks[0].shape[0]
    pieces = [_split(b, parts) for b in blocks]
    stacked = jnp.concatenate([p[k] for k in range(parts) for p in pieces], axis=0)
    res = jnp.dot(stacked, m01, preferred_element_type=F32)
    out = []
    for j in range(n):
        tot = res[j * rows:(j + 1) * rows]
        for k in range(1, parts):
            tot = tot + res[(k * n + j) * rows:(k * n + j + 1) * rows]
        out.append(tot)
    return out


def _sb_weights(ls, lk_full, run, last_mask, upper, n):
    lk = [_blk(lk_full, b) for b in range(n)]
    lk[n - 1] = jnp.where(last_mask, lk[n - 1], 0.0)
    aft = _stacked_sel(lk, upper, 1)
    w = [None] * n
    for b in range(n - 1, -1, -1):
        wb = jnp.exp(_blk(ls, b) + aft[b] + run)
        w[b] = jnp.where(last_mask, wb, 0.0) if b == n - 1 else wb
        run = run + jnp.sum(lk[b], axis=1, keepdims=True)
    return w, run


def _sb_alive(run_scr):
    top = jnp.max(run_scr[:, 0:SB_TOP, :]) > SB_DEAD
    rest = jnp.max(run_scr[:, SB_TOP:, :]) > SB_DEAD
    return top.astype(jnp.int32), rest.astype(jnp.int32)


def _sb_walk(i, key_set, strict, run_scr):
    @pl.when(i >= SB_WINDOW - 1)
    def _():
        key_set(i - (SB_WINDOW - 1), SB_WINDOW, strict, TM)

    start = jnp.where(i >= SB_WINDOW - 1, i - SB_WINDOW, i)

    def cond(c):
        return jnp.logical_and(c[0] >= 0, c[1] + c[2] > 0)

    def step(c):
        kb, _, rest = c
        mask = jnp.logical_or(strict, kb < i)

        @pl.when(rest > 0)
        def _():
            key_set(kb, 1, mask, TM)

        @pl.when(rest == 0)
        def _():
            key_set(kb, 1, mask, SB_TOP)

        return (kb - 1,) + _sb_alive(run_scr)

    lax.while_loop(cond, step, (start,) + _sb_alive(run_scr))


SB_LANES_FWD = 256
SB_LANES_BWD = 256


def _head_masks(lanes):
    lane = _iota((TM, lanes), 1)
    return tuple(jnp.logical_and(lane >= h * HEAD, lane < (h + 1) * HEAD) for h in range(lanes // HEAD))


def _by_head(hmask, parts):
    out = parts[-1]
    for h in range(len(parts) - 2, -1, -1):
        out = jnp.where(hmask[h], parts[h], out)
    return out


def _sb_fwd(qkv, fetch=()):
    lp = qkv.shape[0]
    nb = lp // TM
    lw = SB_LANES_FWD
    nh = lw // HEAD
    npair = SB_W // lw
    nf = len(fetch)
    host_in, host_out, host_shapes, host_sems = _gather_specs(fetch, nf)

    def body(q_ref, k_ref, v_ref, *rest):
        srcs, (o_ref, olo_ref), dsts = rest[:nf], rest[nf:nf + 2], rest[nf + 2:2 * nf + 2]
        acc, run_scr = rest[2 * nf + 2:2 * nf + 4]
        host_sem_refs = rest[2 * nf + 4:]
        p = pl.program_id(0)
        i = pl.program_id(1)

        if nf:
            @pl.when(jnp.logical_and(p == 0, i == 0))
            def _():
                for cp in _gather_copies(srcs, dsts, nf, *host_sem_refs)[0]:
                    cp.start()

        lane = _iota((TM, TM), 1)
        row = _iota((TM, TM), 0)
        hmask = _head_masks(lw)
        upper = (row > lane).astype(_MXU)
        strict = lane < row
        q = q_ref[...] * (1.0 / math.sqrt(HEAD))
        qh = tuple(jnp.where(m, q, 0.0).astype(_MXU) for m in hmask)

        def key_set(first, n, last_mask, nrows):
            off = pl.multiple_of(first * TM, TM)
            kwin = k_ref[pl.ds(off, n * TM), :].astype(_MXU)
            vwin = v_ref[pl.ds(off, n * TM), :].astype(_MXU)
            for hh in range(nh):
                run = run_scr[hh, 0:nrows, 0:1]
                _, _, ls, lk_full = _sb_logs(qh[hh][0:nrows], kwin)
                w, run = _sb_weights(ls, lk_full, run, last_mask[0:nrows], upper, n)
                pieces = [_split(wb, 2) for wb in w]
                stacked = jnp.concatenate(
                    [jnp.concatenate([p[0] for p in pieces], axis=1), jnp.concatenate([p[1] for p in pieces], axis=1)], axis=0)
                res = jnp.dot(stacked, vwin, preferred_element_type=F32)
                acc[hh, 0:nrows] += res[0:nrows]
                acc[nh + hh, 0:nrows] += res[nrows:2 * nrows]
                run_scr[hh, 0:nrows] = jnp.broadcast_to(run, (nrows, TM))

        acc[...] = jnp.zeros_like(acc)
        run_scr[...] = jnp.zeros_like(run_scr)
        _sb_walk(i, key_set, strict, run_scr)
        o_ref[...] = _by_head(hmask, [acc[h] for h in range(nh)])
        olo_ref[...] = _by_head(hmask, [acc[nh + h] for h in range(nh)])

        if nf:
            @pl.when(jnp.logical_and(p == npair - 1, i == nb - 1))
            def _():
                _gather_finish(*_gather_copies(srcs, dsts, nf, *host_sem_refs))

    blk = pl.BlockSpec((TM, lw), lambda p, i: (i, p))
    res = pl.pallas_call(
        body, name="sb_fwd", grid=(npair, nb),
        in_specs=[blk,
                  pl.BlockSpec((lp, lw), lambda p, i: (0, npair + p)),
                  pl.BlockSpec((lp, lw), lambda p, i: (0, 2 * npair + p))] + host_in,
        out_specs=[blk, blk] + host_out,
        out_shape=[_SDS((lp, SB_W), F32), _SDS((lp, SB_W), F32)] + host_shapes,
        scratch_shapes=[pltpu.VMEM((2 * nh, TM, lw), F32), pltpu.VMEM((nh, TM, TM), F32)] + host_sems,
        compiler_params=_params(("arbitrary", "arbitrary")),
    )(qkv, qkv, qkv, *fetch)
    return res[0], res[1], _own_slot(res[2:], fetch)


def _sb_bwd(qkv, o_sb, o_lo, d_o, d_proj, send=(), dests=()):
    lp = qkv.shape[0]
    nb = lp // TM
    lw = SB_LANES_BWD
    nh = lw // HEAD
    npair = SB_W // lw
    scale = 1.0 / math.sqrt(HEAD)
    n = len(send)
    host_in, host_out, host_shapes, host_sems = _host_specs(send)

    def body(q_ref, k_ref, v_ref, o_ref, olo_ref, do_ref, dproj_in, *rest):
        srcs, dproj_ref, dsts = rest[:n], rest[n], rest[n + 1:2 * n + 1]
        dq_all, dk_ref, dv_ref, stage, sems, dq_acc, run_scr, gsum_scr = rest[2 * n + 1:2 * n + 9]
        host_sem_refs = rest[2 * n + 9:]
        p = pl.program_id(0)
        i = pl.program_id(1)

        if n:
            @pl.when(jnp.logical_and(p == 0, i == 0))
            def _():
                _start_copies(_slab_copies(srcs, dsts, dests, *host_sem_refs))

        @pl.when(i == 0)
        def _():
            dk_ref[...] = jnp.zeros_like(dk_ref)
            dv_ref[...] = jnp.zeros_like(dv_ref)

        lane = _iota((TM, TM), 1)
        row = _iota((TM, TM), 0)
        hmask = _head_masks(lw)
        upper = (row > lane).astype(_MXU)
        lower_incl = (row >= lane).astype(_MXU)
        strict = lane < row
        q = q_ref[...] * scale
        do = do_ref[...]
        prod = do.astype(_MXU).astype(F32) * (o_ref[...] + olo_ref[...])
        qh = tuple(jnp.where(m, q, 0.0).astype(_MXU) for m in hmask)
        doh = tuple(jnp.where(m, do, 0.0).astype(_MXU) for m in hmask)
        gtot = tuple(jnp.sum(jnp.where(m, prod, 0.0), axis=1, keepdims=True) for m in hmask)

        def key_set(first, n, last_mask, nrows):
            off = pl.multiple_of(first * TM, TM)
            kf = k_ref[pl.ds(off, n * TM), :]
            kwin = kf.astype(_MXU)
            vwin = v_ref[pl.ds(off, n * TM), :].astype(_MXU)
            last_mask = last_mask[0:nrows]
            dk_win = None
            for hh in range(nh):
                run = run_scr[hh, 0:nrows, 0:1]
                gsum = gsum_scr[hh, 0:nrows, 0:1]
                z, e, ls, lk_full = _sb_logs(qh[hh][0:nrows], kwin)
                w, run = _sb_weights(ls, lk_full, run, last_mask, upper, n)
                r = 1.0 / (1.0 + e)
                er = e * r
                pos = z >= 0.0
                beta = jnp.where(pos, r, er)
                one_m_beta = jnp.where(pos, er, r)
                dw = lax.dot_general(doh[hh][0:nrows], vwin, _NT, preferred_element_type=F32)
                g = [_blk(dw, b) * w[b] for b in range(n)]
                suffix = _stacked_sel(g, lower_incl, 2)
                dz = [None] * n
                for b in range(n - 1, -1, -1):
                    prefix = gtot[hh][0:nrows] - gsum - suffix[b]
                    d = g[b] * _blk(one_m_beta, b) - _blk(beta, b) * prefix
                    dz[b] = (jnp.where(last_mask, d, 0.0) if b == n - 1 else d).astype(_MXU)
                    gsum = gsum + jnp.sum(g[b], axis=1, keepdims=True)
                dzw = jnp.concatenate(dz, axis=1)
                ww = jnp.concatenate([wb.astype(_MXU) for wb in w], axis=1)
                kh = jnp.where(hmask[hh][0:1, :], kf, 0.0).astype(_MXU)
                dq_acc[0:nrows] += jnp.dot(dzw, kh, preferred_element_type=F32)
                dk_h = lax.dot_general(dzw, qh[hh][0:nrows], _TN, preferred_element_type=F32)
                dv_h = lax.dot_general(ww, doh[hh][0:nrows], _TN, preferred_element_type=F32)
                dk_win = (dk_h, dv_h) if dk_win is None else (dk_win[0] + dk_h, dk_win[1] + dv_h)
                run_scr[hh, 0:nrows] = jnp.broadcast_to(run, (nrows, TM))
                gsum_scr[hh, 0:nrows] = jnp.broadcast_to(gsum, (nrows, TM))
            dk_ref[pl.ds(off, n * TM), :] += dk_win[0]
            dv_ref[pl.ds(off, n * TM), :] += dk_win[1]

        dq_acc[...] = jnp.zeros_like(dq_acc)
        run_scr[...] = jnp.zeros_like(run_scr)
        gsum_scr[...] = jnp.zeros_like(gsum_scr)

        _sb_walk(i, key_set, strict, run_scr)
        dq_all[pl.ds(pl.multiple_of(i * TM, TM), TM), :] = dq_acc[...] * scale

        @pl.when(i == nb - 1)
        def _():
            copies = []
            for s, src in enumerate((dq_all, dk_ref, dv_ref)):
                stage[s] = src[...].astype(_MXU)
                col = pl.multiple_of((s * npair + p) * lw, lw)
                copies.append(pltpu.make_async_copy(stage.at[s], dproj_ref.at[:, pl.ds(col, lw)], sems.at[s]))
                copies[-1].start()
            for cp in copies:
                cp.wait()

        if n:
            @pl.when(jnp.logical_and(p == npair - 1, i == nb - 1))
            def _():
                _finish_copies(_slab_copies(srcs, dsts, dests, *host_sem_refs))

    blk = pl.BlockSpec((TM, lw), lambda p, i: (i, p))
    res = pl.pallas_call(
        body, name="sb_bwd", grid=(npair, nb),
        in_specs=[blk,
                  pl.BlockSpec((lp, lw), lambda p, i: (0, npair + p)),
                  pl.BlockSpec((lp, lw), lambda p, i: (0, 2 * npair + p)),
                  blk, blk, blk, pl.BlockSpec(memory_space=pl.ANY)] + host_in,
        out_specs=[pl.BlockSpec(memory_space=pl.ANY)] + host_out,
        out_shape=[_SDS(d_proj.shape, d_proj.dtype)] + host_shapes,
        input_output_aliases={6: 0},
        scratch_shapes=[pltpu.VMEM((lp, lw), F32), pltpu.VMEM((lp, lw), F32), pltpu.VMEM((lp, lw), F32),
                        pltpu.VMEM((3, lp, lw), _MXU), pltpu.SemaphoreType.DMA((3,)),
                        pltpu.VMEM((TM, lw), F32), pltpu.VMEM((nh, TM, TM), F32),
                        pltpu.VMEM((nh, TM, TM), F32)] + host_sems,
        compiler_params=_params(("arbitrary", "arbitrary")),
    )(qkv, qkv, qkv, o_sb, o_lo, d_o, d_proj, *send)
    return res[0], list(res[1:])


def _conv_pre(x_ref, w_ref, b_ref, lp):
    n = lp - 8
    w = w_ref[...]
    pre = (x_ref[pl.ds(5, n), :] * w[0:1, :] + x_ref[pl.ds(6, n), :] * w[1:2, :]
           + x_ref[pl.ds(7, n), :] * w[2:3, :] + x_ref[pl.ds(8, n), :] * w[3:4, :]) + b_ref[...]
    live = (_iota((n, 128), 0) + 8) >= PAD
    return pre, live


def _conv_fwd(proj, dt_raw, conv_w, conv_b, dt_bias128):
    lp = proj.shape[0]
    nblk = XBC_W // 128
    c0 = (COL_XBC - QKV_W) // 128

    def body(x_ref, w_ref, b_ref, dtr_ref, dtb_ref, o_ref, dt_ref):
        pre, live = _conv_pre(x_ref, w_ref, b_ref, lp)
        act = pre * _sigmoid(pre)
        o_ref[pl.ds(0, 8), :] = jnp.zeros((8, 128), F32)
        o_ref[pl.ds(8, lp - 8), :] = jnp.where(live, act, 0.0)

        @pl.when(pl.program_id(0) == 0)
        def _():
            s = dtr_ref[...] + dtb_ref[...]
            sp = jnp.maximum(s, 0.0) + jnp.log(1.0 + jnp.exp(-jnp.abs(s)))
            dt_ref[...] = jnp.where(_iota((lp, 128), 0) >= PAD, sp, 0.0)

    return pl.pallas_call(
        body, name="conv_fwd", grid=(nblk,),
        in_specs=[pl.BlockSpec((lp, 128), lambda j: (0, c0 + j)),
                  pl.BlockSpec((4, 128), lambda j: (0, j)),
                  pl.BlockSpec((1, 128), lambda j: (0, j)),
                  pl.BlockSpec((lp, 128), lambda j: (0, 0)),
                  pl.BlockSpec((1, 128), lambda j: (0, 0))],
        out_specs=[pl.BlockSpec((lp, 128), lambda j: (0, j)),
                   pl.BlockSpec((lp, 128), lambda j: (0, 0))],
        out_shape=[_SDS((lp, XBC_W), F32), _SDS((lp, 128), F32)],
        compiler_params=_params(("arbitrary",)),
    )(proj, conv_w, conv_b, dt_raw, dt_bias128)


def _conv_bwd(proj, dt_raw, conv_w, conv_b, dt_bias128, d_xbc, d_dt128, d_proj):
    lp = proj.shape[0]
    nblk = XBC_W // 128
    c0 = COL_XBC // 128
    c0_in = (COL_XBC - QKV_W) // 128
    n = lp - 8
    last = nblk - 1

    def body(x_ref, w_ref, b_ref, dtr_ref, dtb_ref, dy_ref, ddt_ref, dproj_in,
             dx_ref, gw_ref, gb_ref, gdtb_ref, scr):
        j = pl.program_id(0)

        @pl.when(j < nblk)
        def _():
            pre, live = _conv_pre(x_ref, w_ref, b_ref, lp)
            sg = _sigmoid(pre)
            dpre = jnp.where(live, dy_ref[pl.ds(8, n), :] * (sg * (1.0 + pre * (1.0 - sg))), 0.0)
            gb_ref[...] = jnp.sum(dpre, axis=0, keepdims=True)
            gw_ref[...] = jnp.concatenate(
                [jnp.sum(dpre * x_ref[pl.ds(5 + k, n), :], axis=0, keepdims=True) for k in range(4)], axis=0)
            scr[pl.ds(0, 8), :] = jnp.zeros((8, 128), F32)
            scr[pl.ds(8, n), :] = dpre
            scr[pl.ds(lp, 8), :] = jnp.zeros((8, 128), F32)
            w = w_ref[...]
            dx_ref[...] = (scr[pl.ds(0, lp), :] * w[3:4, :] + scr[pl.ds(1, lp), :] * w[2:3, :]
                           + scr[pl.ds(2, lp), :] * w[1:2, :] + scr[pl.ds(3, lp), :] * w[0:1, :]).astype(dx_ref.dtype)

        @pl.when(j == nblk)
        def _():
            s = dtr_ref[...] + dtb_ref[...]
            d = jnp.where(_iota((lp, 128), 0) >= PAD, ddt_ref[...] * _sigmoid(s), 0.0)
            dx_ref[...] = d.astype(dx_ref.dtype)
            gdtb_ref[...] = jnp.sum(d, axis=0, keepdims=True)

    clamp = lambda j: (0, jnp.minimum(j, last))
    full128 = pl.BlockSpec((lp, 128), lambda j: (0, 0))
    return pl.pallas_call(
        body, name="conv_bwd", grid=(nblk + 1,),
        in_specs=[pl.BlockSpec((lp, 128), lambda j: (0, c0_in + jnp.minimum(j, last))),
                  pl.BlockSpec((4, 128), clamp),
                  pl.BlockSpec((1, 128), clamp),
                  full128, pl.BlockSpec((1, 128), lambda j: (0, 0)),
                  pl.BlockSpec((lp, 128), clamp), full128, pl.BlockSpec(memory_space=pl.ANY)],
        out_specs=[pl.BlockSpec((lp, 128), lambda j: (0, c0 + j)), pl.BlockSpec((4, 128), clamp),
                   pl.BlockSpec((1, 128), clamp), pl.BlockSpec((1, 128), lambda j: (0, 0))],
        out_shape=[_SDS(d_proj.shape, d_proj.dtype), _SDS((4, XBC_W), F32), _SDS((1, XBC_W), F32), _SDS((1, 128), F32)],
        input_output_aliases={7: 0},
        scratch_shapes=[pltpu.VMEM((lp + 8, 128), F32)],
        compiler_params=_params(("arbitrary",)),
    )(proj, conv_w, conv_b, dt_raw, dt_bias128, d_xbc, d_dt128, d_proj)


def _ssd_pieces(dt, dt_t, a, a_t):
    r64 = _iota((CHUNK, CHUNK), 0)
    c64 = _iota((CHUNK, CHUNK), 1)
    tril = c64 <= r64
    tril01 = tril.astype(_MXU)
    triu01 = (r64 <= c64).astype(_MXU)
    expand = (lax.shift_right_logical(_iota((N_HEADS, SSD_W), 1), 6) == _iota((N_HEADS, SSD_W), 0)).astype(_MXU)
    acum = _sel_left(tril01, dt * a)
    acum_t = _sel_right(dt_t * a_t, triu01)
    ax = _sel_right(acum, expand)
    dtx = _sel_right(dt, expand)
    return tril, expand, acum, acum_t, ax, dtx


def _seg_matrix():
    return (lax.shift_right_logical(_iota((SSD_W, N_HEADS), 0), 6) == _iota((SSD_W, N_HEADS), 1)).astype(_MXU)


def _head_decay(ax, acum_t, h, tril):
    col = ax[:, h * HEAD:(h + 1) * HEAD]
    rowv = acum_t[h:h + 1, :]
    return jnp.where(tril, jnp.exp(jnp.minimum(col - rowv, 0.0)), 0.0)


def _ssd_fwd(xbc, dt_c, dt_tc, a, a_t, dskip_x):
    lp = xbc.shape[0]
    nc = lp // CHUNK
    gw = SSD_W // N_GROUPS
    hpg = N_HEADS // N_GROUPS

    def body(x_ref, dt_ref, dtt_ref, a_ref, at_ref, d_ref, y_ref, st_ref, state):
        c = pl.program_id(0)

        @pl.when(c == 0)
        def _():
            state[...] = jnp.zeros_like(state)

        st_ref[0] = state[...]
        tril, _, _, acum_t, ax, dtx = _ssd_pieces(dt_ref[0], dtt_ref[0], a_ref[...], at_ref[...])
        x = x_ref[:, 0:SSD_W]
        xdt = x * dtx
        ea = jnp.exp(ax)
        aex = ax[CHUNK - 1:CHUNK, :]
        wd = jnp.exp(aex - ax)
        eae = jnp.exp(aex)
        xw = xdt * wd
        y_ref[...] = x * d_ref[...]
        for g in range(N_GROUPS):
            gs = slice(g * gw, (g + 1) * gw)
            rs = slice(g * N_STATE, (g + 1) * N_STATE)
            bg = x_ref[:, SSD_W + g * N_STATE:SSD_W + (g + 1) * N_STATE]
            cg = x_ref[:, SSD_W + N_GROUPS * N_STATE + g * N_STATE:SSD_W + N_GROUPS * N_STATE + (g + 1) * N_STATE]
            sg = state[rs, :]
            cb = _mm_nt(cg, bg)
            y_ref[:, gs] += _mm(cg, sg) * ea[:, gs]
            for r in range(hpg):
                h = g * hpg + r
                hs = slice(h * HEAD, (h + 1) * HEAD)
                m = cb * _head_decay(ax, acum_t, h, tril)
                y_ref[:, hs] += _mm(m, xdt[:, hs])
            state[rs, :] = sg * eae[:, gs] + _mm_tn(bg, xw[:, gs])

    return pl.pallas_call(
        body, name="ssd_fwd", grid=(nc,),
        in_specs=[pl.BlockSpec((CHUNK, XBC_W), lambda c: (c, 0)),
                  pl.BlockSpec((1, CHUNK, N_HEADS), lambda c: (c, 0, 0)),
                  pl.BlockSpec((1, N_HEADS, CHUNK), lambda c: (c, 0, 0)),
                  pl.BlockSpec((1, N_HEADS), lambda c: (0, 0)),
                  pl.BlockSpec((N_HEADS, 1), lambda c: (0, 0)),
                  pl.BlockSpec((1, SSD_W), lambda c: (0, 0))],
        out_specs=[pl.BlockSpec((CHUNK, SSD_W), lambda c: (c, 0)),
                   pl.BlockSpec((1, N_GROUPS * N_STATE, gw), lambda c: (c, 0, 0))],
        out_shape=[_SDS((lp, SSD_W), F32), _SDS((nc, N_GROUPS * N_STATE, gw), F32)],
        scratch_shapes=[pltpu.VMEM((N_GROUPS * N_STATE, gw), F32)],
        compiler_params=_params(("arbitrary",)),
    )(xbc, dt_c, dt_tc, a, a_t, dskip_x)


def _ssd_bwd(xbc, dt_c, dt_tc, a, a_t, dskip_x, states, d_y):
    lp = xbc.shape[0]
    nc = lp // CHUNK
    gw = SSD_W // N_GROUPS
    hpg = N_HEADS // N_GROUPS

    def body(x_ref, dt_ref, dtt_ref, a_ref, at_ref, d_ref, st_ref, dy_ref,
             dx_ref, ddta_ref, ddtb_ref, ga1_ref, ga2_ref, gd_ref, dstate, dxdt_scr, z_scr, yoff_scr, sds_scr):
        c = pl.program_id(0)

        @pl.when(c == 0)
        def _():
            dstate[...] = jnp.zeros_like(dstate)
            ga1_ref[...] = jnp.zeros_like(ga1_ref)
            ga2_ref[...] = jnp.zeros_like(ga2_ref)
            gd_ref[...] = jnp.zeros_like(gd_ref)

        dt = dt_ref[0]
        dt_t = dtt_ref[0]
        a = a_ref[...]
        a_t = at_ref[...]
        tril, _, acum, acum_t, ax, dtx = _ssd_pieces(dt, dt_t, a, a_t)
        seg = _seg_matrix()
        x = x_ref[:, 0:SSD_W]
        dy = dy_ref[...]
        xdt = x * dtx
        ea = jnp.exp(ax)
        aex = ax[CHUNK - 1:CHUNK, :]
        wd = jnp.exp(aex - ax)
        eae = jnp.exp(aex)
        xw = xdt * wd
        edy = ea * dy
        lane16 = _iota((CHUNK, N_HEADS), 1)
        row16 = _iota((N_HEADS, CHUNK), 0)
        da_col = jnp.zeros((CHUNK, N_HEADS), F32)
        da_row = jnp.zeros((N_HEADS, CHUNK), F32)
        for g in range(N_GROUPS):
            gs = slice(g * gw, (g + 1) * gw)
            rs = slice(g * N_STATE, (g + 1) * N_STATE)
            bcol = slice(SSD_W + g * N_STATE, SSD_W + (g + 1) * N_STATE)
            ccol = slice(SSD_W + N_GROUPS * N_STATE + g * N_STATE, SSD_W + N_GROUPS * N_STATE + (g + 1) * N_STATE)
            bg = x_ref[:, bcol]
            cg = x_ref[:, ccol]
            sg = st_ref[0, rs, :]
            dsn = dstate[rs, :]
            cb = _mm_nt(cg, bg)
            z_scr[:, gs] = _mm(bg, dsn)
            yoff_scr[:, gs] = _mm(cg, sg) * ea[:, gs]
            sds_scr[:, gs] = jnp.broadcast_to(jnp.sum(dsn * sg, axis=0, keepdims=True), (8, gw))
            dcb = jnp.zeros((CHUNK, CHUNK), F32)
            for r in range(hpg):
                h = g * hpg + r
                hs = slice(h * HEAD, (h + 1) * HEAD)
                dec = _head_decay(ax, acum_t, h, tril)
                m = cb * dec
                t1 = _mm_nt(dy[:, hs], xdt[:, hs])
                dcb = dcb + dec * t1
                tm = m * t1
                da_col = da_col + jnp.where(lane16 == h, jnp.sum(tm, axis=1, keepdims=True), 0.0)
                da_row = da_row - jnp.where(row16 == h, jnp.sum(tm, axis=0, keepdims=True), 0.0)
                dxdt_scr[:, hs] = _mm_tn(m, dy[:, hs])
            dx_ref[:, ccol] = _mm(dcb, bg) + _mm_nt(edy[:, gs], sg)
            dx_ref[:, bcol] = _mm_tn(dcb, cg) + _mm_nt(xw[:, gs], dsn)
            dstate[rs, :] = eae[:, gs] * dsn + _mm_tn(cg, edy[:, gs])
        zf = z_scr[...]
        dxdt = dxdt_scr[...] + wd * zf
        t3 = _sel_right(xw * zf, seg)
        da_col = da_col + _sel_right(dy * yoff_scr[...], seg) - t3
        aend = acum[CHUNK - 1:CHUNK, :]
        sd = _sel_right(sds_scr[...], seg)[0:1, :] * jnp.exp(aend)
        last = jnp.sum(t3, axis=0, keepdims=True) + sd
        da_col = da_col + jnp.where(_iota((CHUNK, N_HEADS), 0) == CHUNK - 1, last, 0.0)
        r64 = _iota((CHUNK, CHUNK), 0)
        c64 = _iota((CHUNK, CHUNK), 1)
        ddta1 = _sel_left((c64 >= r64).astype(_MXU), da_col)
        ddta2 = _sel_right(da_row, (r64 >= c64).astype(_MXU))
        ddta_ref[0] = a * ddta1 + _sel_right(dxdt * x, seg)
        ddtb_ref[0] = a_t * ddta2
        ga1_ref[...] += jnp.sum(dt * ddta1, axis=0, keepdims=True)
        ga2_ref[...] += jnp.sum(dt_t * ddta2, axis=1, keepdims=True)
        dx_ref[:, 0:SSD_W] = dxdt * dtx + d_ref[...] * dy
        gd_ref[...] += jnp.sum(dy * x, axis=0, keepdims=True)

    rev = lambda c: (nc - 1 - c, 0)
    rev3 = lambda c: (nc - 1 - c, 0, 0)
    return pl.pallas_call(
        body, name="ssd_bwd", grid=(nc,),
        in_specs=[pl.BlockSpec((CHUNK, XBC_W), rev),
                  pl.BlockSpec((1, CHUNK, N_HEADS), rev3),
                  pl.BlockSpec((1, N_HEADS, CHUNK), rev3),
                  pl.BlockSpec((1, N_HEADS), lambda c: (0, 0)),
                  pl.BlockSpec((N_HEADS, 1), lambda c: (0, 0)),
                  pl.BlockSpec((1, SSD_W), lambda c: (0, 0)),
                  pl.BlockSpec((1, N_GROUPS * N_STATE, gw), rev3),
                  pl.BlockSpec((CHUNK, SSD_W), rev)],
        out_specs=[pl.BlockSpec((CHUNK, XBC_W), rev),
                   pl.BlockSpec((1, CHUNK, N_HEADS), rev3),
                   pl.BlockSpec((1, N_HEADS, CHUNK), rev3),
                   pl.BlockSpec((1, N_HEADS), lambda c: (0, 0)),
                   pl.BlockSpec((N_HEADS, 1), lambda c: (0, 0)),
                   pl.BlockSpec((1, SSD_W), lambda c: (0, 0))],
        out_shape=[_SDS((lp, XBC_W), F32), _SDS((nc, CHUNK, N_HEADS), F32), _SDS((nc, N_HEADS, CHUNK), F32),
                   _SDS((1, N_HEADS), F32), _SDS((N_HEADS, 1), F32), _SDS((1, SSD_W), F32)],
        scratch_shapes=[pltpu.VMEM((N_GROUPS * N_STATE, gw), F32), pltpu.VMEM((CHUNK, SSD_W), F32),
                        pltpu.VMEM((CHUNK, SSD_W), F32), pltpu.VMEM((CHUNK, SSD_W), F32),
                        pltpu.VMEM((8, SSD_W), F32)],
        compiler_params=_params(("arbitrary",)),
    )(xbc, dt_c, dt_tc, a, a_t, dskip_x, states, d_y)


def _gated_norm(o, gate, w):
    sg = _sigmoid(gate)
    p = o * (gate * sg)
    rs = lax.rsqrt(jnp.mean(p * p, axis=-1, keepdims=True) + EPS)
    n = p * rs
    return sg, rs, n, n * w


def _tail_fwd(o_sb, o_ssd, proj, h0, target, w_out, sb_w, ssd_w, fin_w):
    lp = o_sb.shape[0]
    nb = lp // TM
    row = lambda i: (i, 0)
    one = lambda i: (0, 0)

    def body(osb_ref, gate_ref, ossd_ref, z_ref, h0_ref, tgt_ref, wo_ref, sbw_ref, ssdw_ref, fw_ref,
             dh1_ref, loss_ref, gfw_ref):
        i = pl.program_id(0)

        @pl.when(i == 0)
        def _():
            loss_ref[...] = jnp.zeros_like(loss_ref)
            gfw_ref[...] = jnp.zeros_like(gfw_ref)

        y1 = _gated_norm(osb_ref[...], gate_ref[...], sbw_ref[...])[3]
        y2 = _gated_norm(ossd_ref[...], z_ref[...], ssdw_ref[...])[3]
        h1 = (h0_ref[...] + _mm(y1, wo_ref[0:SB_W, :])) + _mm(y2, wo_ref[SB_W:SB_W + SSD_W, :])
        rs1 = lax.rsqrt(jnp.mean(h1 * h1, axis=-1, keepdims=True) + EPS)
        n1 = h1 * rs1
        fw = fw_ref[...]
        diff = jnp.where(i > 0, n1 * fw - tgt_ref[...], 0.0)
        loss_ref[...] += jnp.sum(diff * diff, axis=0, keepdims=True)
        d_out = diff * (1.0 / D_MODEL)
        gfw_ref[...] += jnp.sum(d_out * n1, axis=0, keepdims=True)
        g = d_out * fw
        dh1_ref[...] = rs1 * (g - n1 * jnp.mean(g * n1, axis=-1, keepdims=True))

    return pl.pallas_call(
        body, name="tail_fwd", grid=(nb,),
        in_specs=[pl.BlockSpec((TM, SB_W), row),
                  pl.BlockSpec((TM, SB_W), lambda i: (i, (COL_GATE - QKV_W) // SB_W)),
                  pl.BlockSpec((TM, SSD_W), row),
                  pl.BlockSpec((TM, SSD_W), lambda i: (i, (COL_Z - QKV_W) // SSD_W)),
                  pl.BlockSpec((TM, D_MODEL), row),
                  pl.BlockSpec((TM, D_MODEL), lambda i: (jnp.maximum(i - 1, 0), 0)),
                  pl.BlockSpec(memory_space=_VMEM),
                  pl.BlockSpec((1, SB_W), one), pl.BlockSpec((1, SSD_W), one), pl.BlockSpec((1, D_MODEL), one)],
        out_specs=[pl.BlockSpec((TM, D_MODEL), row), pl.BlockSpec((1, D_MODEL), one), pl.BlockSpec((1, D_MODEL), one)],
        out_shape=[_SDS((lp, D_MODEL), F32), _SDS((1, D_MODEL), F32), _SDS((1, D_MODEL), F32)],
        compiler_params=_params(("arbitrary",), 40),
    )(o_sb, proj, o_ssd, proj, h0, target, w_out, sb_w, ssd_w, fin_w)


def _gated_norm_bwd(o, gate, w, dy):
    sg, rs, n, _ = _gated_norm(o, gate, w)
    gw = jnp.sum(dy * n, axis=0, keepdims=True)
    dn = dy * w
    dp = rs * (dn - n * jnp.mean(dn * n, axis=-1, keepdims=True))
    d_o = dp * (gate * sg)
    d_gate = dp * o * (sg * (1.0 + gate * (1.0 - sg)))
    return d_o, d_gate, gw, n * w


def _tail_bwd(o_sb, o_ssd, proj, d_h1, w_out, sb_w, ssd_w):
    lp = o_sb.shape[0]
    tm = 272 if lp % 272 == 0 else TM
    nb = lp // tm
    row = lambda i, t: (i, 0)
    one = lambda i, t: (0, 0)

    def body(osb_ref, gate_ref, ossd_ref, z_ref, dh1_ref, wo_ref, sbw_ref, ssdw_ref,
             dosb_ref, dossd_ref, dproj_ref, gwo_ref, gsb_ref, gssd_ref):
        i = pl.program_id(0)
        t = pl.program_id(1)

        @pl.when(jnp.logical_and(i == 0, t == 0))
        def _():
            gwo_ref[...] = jnp.zeros_like(gwo_ref)
            gsb_ref[...] = jnp.zeros_like(gsb_ref)
            gssd_ref[...] = jnp.zeros_like(gssd_ref)

        dh1 = dh1_ref[...].astype(_MXU)

        def half(o_ref, g_ref, w_ref, do_ref, gn_ref, r0):
            dy = lax.dot_general(dh1, wo_ref[r0:r0 + SB_W, :], _NT, preferred_element_type=F32)
            d_o, d_g, gw, y = _gated_norm_bwd(o_ref[...], g_ref[...], w_ref[...], dy)
            do_ref[...] = d_o
            dproj_ref[...] = d_g.astype(_MXU)
            gn_ref[...] += gw
            gwo_ref[r0:r0 + SB_W, :] += lax.dot_general(y.astype(_MXU), dh1, _TN, preferred_element_type=F32)

        @pl.when(t == 0)
        def _():
            half(osb_ref, gate_ref, sbw_ref, dosb_ref, gsb_ref, 0)

        @pl.when(t == 1)
        def _():
            half(ossd_ref, z_ref, ssdw_ref, dossd_ref, gssd_ref, SB_W)

    tile = pl.BlockSpec((tm, SB_W), row)
    return pl.pallas_call(
        body, name="tail_bwd", grid=(nb, 2),
        in_specs=[tile, pl.BlockSpec((tm, SB_W), lambda i, t: (i, (COL_GATE - QKV_W) // SB_W)),
                  tile, pl.BlockSpec((tm, SSD_W), lambda i, t: (i, (COL_Z - QKV_W) // SSD_W)),
                  tile, pl.BlockSpec(memory_space=_VMEM),
                  pl.BlockSpec((1, SB_W), one), pl.BlockSpec((1, SSD_W), one)],
        out_specs=[tile, tile, pl.BlockSpec((tm, SB_W), lambda i, t: (i, COL_GATE // SB_W + t)),
                   pl.BlockSpec((SB_W + SSD_W, D_MODEL), one), pl.BlockSpec((1, SB_W), one), pl.BlockSpec((1, SSD_W), one)],
        out_shape=[_SDS((lp, SB_W), F32), _SDS((lp, SSD_W), F32), _SDS((lp, W_ALL), _MXU),
                   _SDS((SB_W + SSD_W, D_MODEL), F32), _SDS((1, SB_W), F32), _SDS((1, SSD_W), F32)],
        compiler_params=_params(("arbitrary", "arbitrary"), 48),
    )(o_sb, proj, o_ssd, proj, d_h1, w_out, sb_w, ssd_w)


def _d_u(d_proj, w_t, send=(), dests=()):
    lp = d_proj.shape[0]
    tk = 512
    steps = N_MAIN // tk
    n = len(send)
    host_in, host_out, host_shapes, host_sems = _host_specs(send)

    def body(dp_ref, w_ref, dpdt_ref, wdt_ref, *rest):
        srcs, o_ref, dsts, sems = rest[:n], rest[n], rest[n + 1:2 * n + 1], rest[2 * n + 1:]
        j = pl.program_id(0)

        @pl.when(j == 0)
        def _():
            if n:
                _start_copies(_slab_copies(srcs, dsts, dests, *sems))
            o_ref[...] = jnp.dot(dpdt_ref[...], wdt_ref[...], preferred_element_type=F32)

        o_ref[...] += jnp.dot(dp_ref[...], w_ref[...], preferred_element_type=F32)

        if n:
            @pl.when(j == steps - 1)
            def _():
                _finish_copies(_slab_copies(srcs, dsts, dests, *sems))

    res = pl.pallas_call(
        body, name="d_u", grid=(steps,),
        in_specs=[pl.BlockSpec((lp, tk), lambda j: (0, j)),
                  pl.BlockSpec((tk, D_MODEL), lambda j: (j, 0)),
                  pl.BlockSpec((lp, 128), lambda j: (0, N_MAIN // 128)),
                  pl.BlockSpec((128, D_MODEL), lambda j: (N_MAIN // 128, 0))] + host_in,
        out_specs=[pl.BlockSpec((lp, D_MODEL), lambda j: (0, 0))] + host_out,
        out_shape=[_SDS((lp, D_MODEL), F32)] + host_shapes,
        scratch_shapes=host_sems,
        compiler_params=_params(("arbitrary",), 48),
    )(d_proj, w_t, d_proj, w_t, *send)
    return res[0], list(res[1:])


def _norm_bwd(du_all, h0, d_h1, norm_w):
    lp = h0.shape[0]
    nb = lp // TM
    seq = lp - OFF
    row = lambda i: (i, 0)
    one = lambda i: (0, 0)

    def body(du_ref, h0_ref, dh1_ref, nw_ref, gx_ref, gmeta_ref, gnw_ref):
        i = pl.program_id(0)

        @pl.when(i == 0)
        def _():
            gnw_ref[...] = jnp.zeros_like(gnw_ref)

        du = du_ref[...]
        h = h0_ref[...]
        rs = lax.rsqrt(jnp.mean(h * h, axis=-1, keepdims=True) + EPS)
        n0 = h * rs
        gnw_ref[...] += jnp.sum(du * n0, axis=0, keepdims=True)
        g = du * nw_ref[...]
        dh0 = dh1_ref[...] + rs * (g - n0 * jnp.mean(g * n0, axis=-1, keepdims=True))

        @pl.when(i == 0)
        def _():
            gmeta_ref[...] = dh0[PAD:PAD + N_META, :]

        @pl.when(i > 0)
        def _():
            gx_ref[...] = dh0

    tile = pl.BlockSpec((TM, D_MODEL), row)
    return pl.pallas_call(
        body, name="norm_bwd", grid=(nb,),
        in_specs=[tile, tile, tile, pl.BlockSpec((1, D_MODEL), one)],
        out_specs=[pl.BlockSpec((TM, D_MODEL), lambda i: (jnp.maximum(i - 1, 0), 0)),
                   pl.BlockSpec((N_META, D_MODEL), one), pl.BlockSpec((1, D_MODEL), one)],
        out_shape=[_SDS((seq, D_MODEL), F32), _SDS((N_META, D_MODEL), F32), _SDS((1, D_MODEL), F32)],
        compiler_params=_params(("arbitrary",)),
    )(du_all, h0, d_h1, norm_w)


def _grad_w_windows(u_t, d_proj, first, count, name):
    lp = d_proj.shape[0]
    hw = WIN_W // 2
    steps = 2 * count

    def body(ut_ref, dp_hbm, o_ref, buf, sems):
        s = pl.program_id(0)
        slot = s % 2

        def fetch(step, sl):
            start = pl.multiple_of((first + step // 2) * WIN_STEP + (step % 2) * hw, 128)
            return pltpu.make_async_copy(dp_hbm.at[:, pl.ds(start, hw)], buf.at[sl], sems.at[sl])

        @pl.when(s == 0)
        def _():
            fetch(0, 0).start()

        @pl.when(s + 1 < steps)
        def _():
            fetch(s + 1, 1 - slot).start()

        fetch(s, slot).wait()
        o_ref[0] = jnp.dot(ut_ref[...], buf[slot], preferred_element_type=F32).astype(o_ref.dtype)

    return pl.pallas_call(
        body, name=name, grid=(steps,),
        in_specs=[pl.BlockSpec((D_MODEL, lp), lambda s: (0, 0)), pl.BlockSpec(memory_space=pl.ANY)],
        out_specs=pl.BlockSpec((1, D_MODEL, hw), lambda s: (s // 2, 0, s % 2)),
        out_shape=_SDS((count, D_MODEL, WIN_W), _MXU),
        scratch_shapes=[pltpu.VMEM((2, lp, hw), _MXU), pltpu.SemaphoreType.DMA((2,))],
        compiler_params=_params(("arbitrary",), 40),
    )(u_t, d_proj)


def _device_grads(x2d, target2d, meta_full, norm_w, w_t, conv_w, conv_b, dt_bias, a_log, d_skip,
                  sb_w, ssd_w, w_out, fin_w, exchange=None, w_out_shard=None):
    lp = x2d.shape[0] + OFF
    nc = lp // CHUNK
    h0, u, u_t = _prep(x2d, meta_full, norm_w)
    qkv, proj, dt_raw = _inproj(u, w_t)
    if w_out is None:
        o_sb, o_lo, (w_out_shards,) = _sb_fwd(qkv, (w_out_shard,))
        w_out = w_out_shards.reshape(2 * D_MODEL, D_MODEL)
    else:
        o_sb, o_lo, _ = _sb_fwd(qkv)
    dt_bias128 = jnp.pad(dt_bias, ((0, 0), (0, 128 - N_HEADS)))
    xbc, dt128 = _conv_fwd(proj, dt_raw, conv_w, conv_b, dt_bias128)
    dt_c = dt128[:, :N_HEADS].reshape(nc, CHUNK, N_HEADS)
    dt_tc = jnp.swapaxes(dt_c, 1, 2)
    a = -jnp.exp(a_log)
    a_t = a.reshape(N_HEADS, 1)
    dskip_x = jnp.repeat(d_skip, HEAD, axis=1)
    o_ssd, states = _ssd_fwd(xbc, dt_c, dt_tc, a, a_t, dskip_x)
    d_h1, sq_err, g_fin = _tail_fwd(o_sb, o_ssd, proj, h0, target2d, w_out, sb_w, ssd_w, fin_w)

    d_osb, d_ossd, d_proj, g_wout, g_sb, g_ssd = _tail_bwd(o_sb, o_ssd, proj, d_h1, w_out, sb_w, ssd_w)
    d_xbc_act, ddt_a, ddt_b, ga1, ga2, gd = _ssd_bwd(xbc, dt_c, dt_tc, a, a_t, dskip_x, states, d_ossd)
    d_dt = (ddt_a + jnp.swapaxes(ddt_b, 1, 2)).reshape(lp, N_HEADS)
    d_dt128 = jnp.pad(d_dt, ((0, 0), (0, 128 - N_HEADS)))
    d_proj, g_convw, g_convb, g_dtb128 = _conv_bwd(proj, dt_raw, conv_w, conv_b, dt_bias128, d_xbc_act, d_dt128, d_proj)
    send_e, dests_e = ((), ()) if exchange is None else exchange["early"](g_wout)
    d_proj, arrived_e = _sb_bwd(qkv, o_sb, o_lo, d_osb, d_proj, send_e, dests_e)
    g_win = _grad_w_windows(u_t, d_proj, 0, N_CHIPS, "grad_w_in")
    send_l, dests_l = ((), ()) if exchange is None else exchange["late"](g_win)
    d_u, arrived_l = _d_u(d_proj, w_t, send_l, dests_l)
    send, arrived = tuple(send_e) + tuple(send_l), tuple(arrived_e) + tuple(arrived_l)
    g_x, g_meta, g_nw = _norm_bwd(d_u, h0, d_h1, norm_w)
    g_alog = (ga1 + ga2.reshape(1, N_HEADS)) * a
    g_dskip = gd.reshape(N_HEADS, HEAD).sum(axis=1).reshape(1, N_HEADS)
    grads = dict(meta_tokens=g_meta, norm_w=g_nw, w_in=g_win, conv_w=g_convw, conv_b=g_convb,
                 dt_bias=g_dtb128[:, :N_HEADS], a_log=g_alog, d_skip=g_dskip, sb_norm_w=g_sb, ssd_norm_w=g_ssd,
                 w_out=g_wout, final_norm_w=g_fin, sent=send, arrived=arrived)
    return sq_err, g_x, grads


_MESH = pl.DeviceIdType.MESH
_ANY = pl.BlockSpec(memory_space=pl.ANY)


def _place():
    return lax.axis_index("x"), lax.axis_index("y"), lax.axis_index("c")


def _other_chips(x, y):
    return ((1 - x, y), (x, 1 - y), (1 - x, 1 - y))


def _gather_copies(srcs, dsts, n_big, send_sems, recv_sems, fwd_send, fwd_recv):
    x, y, c = _place()
    mine = 2 * x + y
    first, passed = [], []
    for a in range(len(srcs)):
        half = srcs[a].shape[1] // 2
        window = pl.ds(pl.multiple_of(c * half, 128), half)
        for k, (px, py) in enumerate(_other_chips(x, y)):
            if a < n_big:
                src, dst = srcs[a].at[:, window], dsts[a].at[mine, :, window]
                landed = dsts[a].at[2 * px + py, :, window]
                passed.append(pltpu.make_async_remote_copy(
                    src_ref=landed, dst_ref=landed, send_sem=fwd_send.at[a * 3 + k], recv_sem=fwd_recv.at[a * 3 + k],
                    device_id=(x, y, 1 - c), device_id_type=_MESH))
            else:
                src, dst = srcs[a], dsts[a].at[mine]
                passed.append(None)
            first.append(pltpu.make_async_remote_copy(
                src_ref=src, dst_ref=dst, send_sem=send_sems.at[a * 3 + k], recv_sem=recv_sems.at[a * 3 + k],
                device_id=(px, py, c), device_id_type=_MESH))
    return first, passed


def _gather_finish(first, passed):
    for cp, fwd in zip(first, passed):
        cp.wait_recv()
        if fwd is not None:
            fwd.start()
    for fwd in passed:
        if fwd is not None:
            fwd.wait_recv()
    for cp in first + [fwd for fwd in passed if fwd is not None]:
        cp.wait_send()


def _gather_specs(arrays, n_big):
    n = len(arrays)
    hbm = [pl.BlockSpec(memory_space=pl.ANY)] * n
    shapes = [_SDS((N_CHIPS,) + a.shape, a.dtype) for a in arrays]
    sems = [pltpu.SemaphoreType.DMA((3 * n,)), pltpu.SemaphoreType.DMA((3 * n,)),
            pltpu.SemaphoreType.DMA((3 * max(n_big, 1),)), pltpu.SemaphoreType.DMA((3 * max(n_big, 1),))] if n else []
    return hbm, hbm, shapes, sems


def _own_slot(got, arrays):
    if not arrays:
        return []
    mine = 2 * lax.axis_index("x") + lax.axis_index("y")
    return [lax.dynamic_update_slice(g, a[None], (mine,) + (0,) * a.ndim) for g, a in zip(got, arrays)]


def _gather_shards(arrays, n_big):
    n = len(arrays)

    def body(*refs):
        first, passed = _gather_copies(refs[:n], refs[n:2 * n], n_big, *refs[2 * n:])
        for cp in first:
            cp.start()
        _gather_finish(first, passed)

    hbm_in, hbm_out, shapes, sems = _gather_specs(arrays, n_big)
    got = pl.pallas_call(
        body, name="gather_shards", in_specs=hbm_in, out_specs=hbm_out, out_shape=shapes, scratch_shapes=sems,
    )(*arrays)
    return _own_slot(got, arrays)


def _slab_copies(srcs, dsts, dests, send_sems, recv_sems):
    x, y, c = _place()
    mine = 2 * x + y
    copies = []
    for a in range(len(srcs)):
        lo, hi = dests[a]
        receives = jnp.logical_and(mine >= lo, mine < hi)
        for k, (px, py) in enumerate(_other_chips(x, y)):
            target = 2 * px + py
            cp = pltpu.make_async_remote_copy(
                src_ref=srcs[a].at[jnp.clip(target - lo, 0, hi - lo - 1)], dst_ref=dsts[a].at[mine],
                send_sem=send_sems.at[a * 3 + k], recv_sem=recv_sems.at[a * 3 + k],
                device_id=(px, py, c), device_id_type=_MESH)
            copies.append((cp, jnp.logical_and(target >= lo, target < hi), receives))
    return copies


def _start_copies(copies):
    for cp, sends, _ in copies:
        pl.when(sends)(cp.start)


def _finish_copies(copies):
    for cp, _, receives in copies:
        pl.when(receives)(cp.wait_recv)
    for cp, sends, _ in copies:
        pl.when(sends)(cp.wait_send)


def _host_specs(send):
    n = len(send)
    hbm = [pl.BlockSpec(memory_space=pl.ANY)] * n
    shapes = [_SDS((N_CHIPS,) + a.shape[1:], a.dtype) for a in send]
    sems = [pltpu.SemaphoreType.DMA((3 * n,)), pltpu.SemaphoreType.DMA((3 * n,))] if n else []
    return hbm, hbm, shapes, sems


def _swap_halves(arrays, name):
    n = len(arrays)

    def body(*refs):
        srcs, dsts = refs[:n], refs[n:2 * n]
        send_sems, recv_sems = refs[2 * n:]
        x, y, c = _place()
        copies = []
        for a in range(n):
            half = arrays[a].shape[1] // 2
            cp = pltpu.make_async_remote_copy(
                src_ref=srcs[a].at[:, pl.ds(pl.multiple_of((1 - c) * half, 16), half)], dst_ref=dsts[a],
                send_sem=send_sems.at[a], recv_sem=recv_sems.at[a],
                device_id=(x, y, 1 - c), device_id_type=_MESH)
            cp.start()
            copies.append(cp)
        for cp in copies:
            cp.wait_recv()
        for cp in copies:
            cp.wait_send()

    return pl.pallas_call(
        body, name=name,
        in_specs=[_ANY] * n, out_specs=[_ANY] * n,
        out_shape=[_SDS((a.shape[0], a.shape[1] // 2, a.shape[2]), a.dtype) for a in arrays],
        scratch_shapes=[pltpu.SemaphoreType.DMA((n,)), pltpu.SemaphoreType.DMA((n,))],
    )(*arrays)


N_DEV = 8
SMALL_ROWS = 32
SMALL_COLS = XBC_W


def _final_exchange(arrays, by_cols, packed):
    n = len(arrays)

    def body(*refs):
        src_ref = refs[n]
        dsts = refs[n + 1:2 * n + 1]
        dst_ref = refs[2 * n + 1]
        send_sems, recv_sems, all_send, all_recv, local_sem = refs[2 * n + 2:]
        x, y, c = _place()
        me = 4 * x + 2 * y + c
        own = pltpu.make_async_copy(src_ref, dst_ref.at[me], local_sem)
        own.start()
        copies = []
        for k in range(1, N_DEV):
            bx, by, bc = (k >> 2) & 1, (k >> 1) & 1, k & 1
            peer = (x + bx - 2 * x * bx, y + by - 2 * y * by, c + bc - 2 * c * bc)
            cp = pltpu.make_async_remote_copy(
                src_ref=src_ref, dst_ref=dst_ref.at[me], send_sem=all_send.at[k - 1], recv_sem=all_recv.at[k - 1],
                device_id=peer, device_id_type=_MESH)
            cp.start()
            copies.append(cp)
        for a in range(n):
            if by_cols[a]:
                half = arrays[a].shape[1] // 2
                mine = dsts[a].at[:, pl.ds(pl.multiple_of(c * half, 128), half)]
            else:
                half = arrays[a].shape[0] // 2
                mine = dsts[a].at[pl.ds(pl.multiple_of(c * half, 16), half)]
            cp = pltpu.make_async_remote_copy(
                src_ref=mine, dst_ref=mine, send_sem=send_sems.at[a], recv_sem=recv_sems.at[a],
                device_id=(x, y, 1 - c), device_id_type=_MESH)
            cp.start()
            copies.append(cp)
        for cp in copies:
            cp.wait_recv()
        for cp in copies:
            cp.wait_send()
        own.wait()

    vm = pl.BlockSpec(memory_space=_VMEM)
    res = pl.pallas_call(
        body, name="final_exchange",
        in_specs=[_ANY] * n + [vm], out_specs=[_ANY] * n + [vm],
        out_shape=[_SDS(a.shape, a.dtype) for a in arrays] + [_SDS((N_DEV, SMALL_ROWS, SMALL_COLS), F32)],
        input_output_aliases={a: a for a in range(n)},
        scratch_shapes=[pltpu.SemaphoreType.DMA((n,)), pltpu.SemaphoreType.DMA((n,)),
                        pltpu.SemaphoreType.DMA((N_DEV - 1,)), pltpu.SemaphoreType.DMA((N_DEV - 1,)),
                        pltpu.SemaphoreType.DMA],
    )(*arrays, packed)
    return list(res[:n]), res[n]


def _adamw(w, g, m, v):
    m = ADAM_B1 * m + (1.0 - ADAM_B1) * g
    v = ADAM_B2 * v + (1.0 - ADAM_B2) * (g * g)
    m_hat = m / (1.0 - ADAM_B1 ** ADAM_STEP)
    v_hat = v / (1.0 - ADAM_B2 ** ADAM_STEP)
    delta = -ADAM_LR * (m_hat / (jnp.sqrt(v_hat) + ADAM_EPS) + ADAM_WD * w)
    return delta, m, v


def _sum_slabs(slabs, core, name, transposed=False):
    _, h, c = slabs.shape
    tr = 128
    nblk = h // tr

    def body(core_ref, s_ref, o_ref):
        tot = ((s_ref[0].astype(F32) + s_ref[1].astype(F32)) + s_ref[2].astype(F32)) + s_ref[3].astype(F32)
        o_ref[...] = tot.T if transposed else tot

    if transposed:
        out_spec = pl.BlockSpec((c, tr), lambda i, core_ref: (0, core_ref[0] * nblk + i))
        out_shape = _SDS((c, 2 * h), F32)
    else:
        out_spec = pl.BlockSpec((tr, c), lambda i, core_ref: (core_ref[0] * nblk + i, 0))
        out_shape = _SDS((2 * h, c), F32)
    grid_spec = pltpu.PrefetchScalarGridSpec(
        num_scalar_prefetch=1, grid=(nblk,),
        in_specs=[pl.BlockSpec((N_CHIPS, tr, c), lambda i, core_ref: (0, i, 0))],
        out_specs=out_spec)
    return pl.pallas_call(
        body, name=name, grid_spec=grid_spec, out_shape=out_shape,
        compiler_params=_params(("arbitrary",)),
    )(core, slabs)


def _add_halves(own, recv, core, name):
    ns, r, c = own.shape
    half = r // 2
    tr = 128
    nblk = half // tr

    def body(core_ref, a_ref, b_ref, o_ref):
        o_ref[...] = (a_ref[...].astype(F32) + b_ref[...].astype(F32)).astype(o_ref.dtype)

    grid_spec = pltpu.PrefetchScalarGridSpec(
        num_scalar_prefetch=1, grid=(nblk,),
        in_specs=[pl.BlockSpec((ns, tr, c), lambda i, core_ref: (0, core_ref[0] * nblk + i, 0)),
                  pl.BlockSpec((ns, tr, c), lambda i, core_ref: (0, i, 0))],
        out_specs=pl.BlockSpec((ns, tr, c), lambda i, core_ref: (0, i, 0)))
    return pl.pallas_call(
        body, name=name, grid_spec=grid_spec, out_shape=_SDS((ns, half, c), own.dtype),
        compiler_params=_params(("arbitrary",)),
    )(core, own, recv)


def _update_big(w, m, v, g, name):
    r, c = w.shape

    def body(w_ref, m_ref, v_ref, g_ref, d_ref, mo_ref, vo_ref):
        delta, m_new, v_new = _adamw(w_ref[...], g_ref[...], m_ref[...], v_ref[...])
        d_ref[...] = delta
        mo_ref[...] = m_new
        vo_ref[...] = v_new

    if r % 128 == 0:
        steps, spec = r // 128, pl.BlockSpec((128, c), lambda i: (i, 0))
    else:
        steps, spec = c // 128, pl.BlockSpec((r, 128), lambda i: (0, i))
    return pl.pallas_call(
        body, name=name, grid=(steps,),
        in_specs=[spec] * 4, out_specs=[spec] * 3,
        out_shape=[_SDS((r, c), F32)] * 3,
        compiler_params=_params(("arbitrary",)),
    )(w, m, v, g)


_ROW = dict(norm_w=0, sb_norm_w=1, ssd_norm_w=2, final_norm_w=3, conv_b=4, dt_bias=5, a_log=6, d_skip=7,
            conv_w=8, sq_err=12, meta_tokens=16)
_SMALL = ("meta_tokens", "norm_w", "conv_w", "conv_b", "dt_bias", "a_log", "d_skip", "sb_norm_w", "ssd_norm_w",
          "final_norm_w")


def _pack_small(sq_err, grads):
    def rowpad(a):
        return jnp.pad(a, ((0, 0), (0, SMALL_COLS - a.shape[1])))

    rows = [rowpad(grads[k]) for k in ("norm_w", "sb_norm_w", "ssd_norm_w", "final_norm_w", "conv_b", "dt_bias", "a_log", "d_skip")]
    rows.append(grads["conv_w"])
    rows.append(rowpad(sq_err))
    rows.append(jnp.zeros((3, SMALL_COLS), F32))
    rows.append(rowpad(grads["meta_tokens"]))
    return jnp.concatenate(rows, axis=0)


def _update_small(gathered, ws, ms, vs):
    names = _SMALL
    n = len(names)

    def body(*refs):
        g_ref = refs[0]
        w_refs, m_refs, v_refs = refs[1:1 + n], refs[1 + n:1 + 2 * n], refs[1 + 2 * n:1 + 3 * n]
        outs = refs[1 + 3 * n:]
        loss_ref = outs[0]
        go, do, mo, vo = outs[1:1 + n], outs[1 + n:1 + 2 * n], outs[1 + 2 * n:1 + 3 * n], outs[1 + 3 * n:1 + 4 * n]
        tot = g_ref[0]
        for d in range(1, N_DEV):
            tot = tot + g_ref[d]
        x, y, _ = _place()
        chip = 2 * x + y
        loss_ref[...] = jnp.broadcast_to(
            0.5 * jnp.sum(tot[_ROW["sq_err"]:_ROW["sq_err"] + 1, 0:D_MODEL], axis=1, keepdims=True) / D_MODEL, (1, 128))
        for idx, nm in enumerate(names):
            r0 = _ROW[nm]
            rows, cols = w_refs[idx].shape
            if nm in ("conv_w", "meta_tokens"):
                g = jnp.zeros((rows, cols), F32)
                for j in range(N_CHIPS):
                    g = g + jnp.where(chip == j, tot[r0:r0 + rows, j * cols:(j + 1) * cols], 0.0)
            else:
                g = tot[r0:r0 + rows, 0:cols]
            delta, m_new, v_new = _adamw(w_refs[idx][...], g, m_refs[idx][...], v_refs[idx][...])
            go[idx][...] = g
            do[idx][...] = delta
            mo[idx][...] = m_new
            vo[idx][...] = v_new

    shapes = [_SDS(ws[nm].shape, F32) for nm in names]
    vm = pl.BlockSpec(memory_space=_VMEM)
    res = pl.pallas_call(
        body, name="update_small",
        in_specs=[vm] * (1 + 3 * n), out_specs=[vm] * (1 + 4 * n),
        out_shape=[_SDS((1, 128), F32)] + shapes * 4,
    )(gathered, *[ws[nm] for nm in names], *[ms[nm] for nm in names], *[vs[nm] for nm in names])
    loss = res[0][0, 0]
    g = dict(zip(names, res[1:1 + n]))
    d = dict(zip(names, res[1 + n:1 + 2 * n]))
    m = dict(zip(names, res[1 + 2 * n:1 + 3 * n]))
    v = dict(zip(names, res[1 + 3 * n:1 + 4 * n]))
    return loss, g, d, m, v


_WEIGHTS = ("meta_tokens", "norm_w", "w_in", "conv_w", "conv_b", "dt_bias", "a_log", "d_skip", "sb_norm_w",
            "ssd_norm_w", "w_out", "final_norm_w")


def kernel(x, meta_tokens, norm_w, w_in, conv_w, conv_b, dt_bias, a_log, d_skip, sb_norm_w, ssd_norm_w, w_out, final_norm_w, loss_target, m_meta_tokens, m_norm_w, m_w_in, m_conv_w, m_conv_b, m_dt_bias, m_a_log, m_d_skip, m_sb_norm_w, m_ssd_norm_w, m_w_out, m_final_norm_w, v_meta_tokens, v_norm_w, v_w_in, v_conv_w, v_conv_b, v_dt_bias, v_a_log, v_d_skip, v_sb_norm_w, v_ssd_norm_w, v_w_out, v_final_norm_w):
    given = dict(meta_tokens=meta_tokens, norm_w=norm_w, w_in=w_in, conv_w=conv_w, conv_b=conv_b, dt_bias=dt_bias,
                 a_log=a_log, d_skip=d_skip, sb_norm_w=sb_norm_w, ssd_norm_w=ssd_norm_w, w_out=w_out,
                 final_norm_w=final_norm_w)
    mom = dict(meta_tokens=m_meta_tokens, norm_w=m_norm_w, w_in=m_w_in, conv_w=m_conv_w, conv_b=m_conv_b,
               dt_bias=m_dt_bias, a_log=m_a_log, d_skip=m_d_skip, sb_norm_w=m_sb_norm_w, ssd_norm_w=m_ssd_norm_w,
               w_out=m_w_out, final_norm_w=m_final_norm_w)
    var = dict(meta_tokens=v_meta_tokens, norm_w=v_norm_w, w_in=v_w_in, conv_w=v_conv_w, conv_b=v_conv_b,
               dt_bias=v_dt_bias, a_log=v_a_log, d_skip=v_d_skip, sb_norm_w=v_sb_norm_w, ssd_norm_w=v_ssd_norm_w,
               w_out=v_w_out, final_norm_w=v_final_norm_w)
    seq = x.shape[1]

    def two_d(a):
        return a.reshape((-1, a.shape[-1])) if a.ndim != 2 else a

    def rows_first(a):
        return jnp.transpose(a, (2, 0, 1)).reshape(W_IN_SHARD, D_MODEL)

    def rows_last(a):
        return jnp.transpose(a.reshape(W_IN_SHARD, 1, D_MODEL), (1, 2, 0))

    w_in_t, m_in_t, v_in_t = rows_first(w_in), rows_first(m_w_in), rows_first(v_w_in)

    g_win, g_meta, g_cw = _gather_shards([w_in_t.astype(_MXU), meta_tokens, conv_w[0]], 1)
    w_t = jnp.pad(g_win.reshape(D_IN, D_MODEL), ((0, W_ALL - D_IN), (0, 0)))
    meta_full = jnp.swapaxes(g_meta, 0, 1).reshape(N_META, D_MODEL)
    conv_w_full = jnp.swapaxes(g_cw, 0, 1).reshape(4, XBC_W)

    core = lax.axis_index("c").astype(jnp.int32).reshape(1)

    def early(g_wout):
        slab_out = g_wout.reshape(N_CHIPS, W_OUT_SHARD, D_MODEL).astype(_MXU)
        (sib_out,) = _swap_halves([slab_out], "swap_halves_w_out")
        return (_add_halves(slab_out, sib_out, core, "chip_sum_w_out"),), ((0, N_CHIPS),)

    def late(g_win):
        (sib_in,) = _swap_halves([g_win], "swap_halves_w_in")
        return (_add_halves(g_win, sib_in, core, "chip_sum_w_in"),), ((0, N_CHIPS),)

    sq_err, g_x, grads = _device_grads(
        x.reshape(seq, D_MODEL), loss_target.reshape(seq, D_MODEL), meta_full, norm_w, w_t, conv_w_full,
        conv_b, dt_bias, a_log, d_skip, sb_norm_w, ssd_norm_w, None, final_norm_w.reshape(1, D_MODEL),
        exchange=dict(early=early, late=late), w_out_shard=w_out[0].astype(_MXU))
    chip_out, chip_in = grads["sent"]
    got_out, got_in = grads["arrived"]
    chip = 2 * lax.axis_index("x") + lax.axis_index("y")

    def with_own(got, sent):
        own = lax.dynamic_slice(sent, (chip, 0, 0), (1,) + sent.shape[1:])
        return lax.dynamic_update_slice(got, own, (chip, 0, 0))

    (g_in, g_out), gathered = _final_exchange(
        [_sum_slabs(with_own(got_in, chip_in), core, "sum_w_in", transposed=True),
         _sum_slabs(with_own(got_out, chip_out), core, "sum_w_out")], (True, False), _pack_small(sq_err, grads))
    g_in = lax.dynamic_slice(g_in, (4 * chip, 0), (W_IN_SHARD, D_MODEL))
    big = dict(w_in=tuple(rows_last(a) for a in (g_in,) + tuple(_update_big(w_in_t, m_in_t, v_in_t, g_in, "update_w_in"))),
               w_out=(g_out,) + tuple(_update_big(w_out[0], m_w_out[0], v_w_out[0], g_out, "update_w_out")))

    loss, sg, sd, sm, sv = _update_small(
        gathered, {k: two_d(given[k]) for k in _SMALL}, {k: two_d(mom[k]) for k in _SMALL},
        {k: two_d(var[k]) for k in _SMALL})

    out = {}
    for idx, group in enumerate((sg, sd, sm, sv)):
        for k in _SMALL:
            out[(idx, k)] = group[k].reshape(given[k].shape)
        for k in ("w_in", "w_out"):
            out[(idx, k)] = big[k][idx].reshape(given[k].shape)
    return (loss, g_x.reshape(x.shape), *[out[(idx, k)] for idx in range(4) for k in _WEIGHTS])
```

```python
import functools
import math

import jax
import jax.numpy as jnp
from jax import lax
from jax.experimental import pallas as pl
from jax.experimental.pallas import tpu as pltpu

F32 = jnp.float32
_MXU = jnp.bfloat16

D_MODEL = 1024
N_META = 16
PAD = 112
OFF = PAD + N_META
TM = 128
CHUNK = 64
SB_W = 1024
SSD_W = 1024
N_HEADS = 16
HEAD = 64
N_GROUPS = 2
N_STATE = 128
XBC_W = SSD_W + 2 * N_GROUPS * N_STATE
N_MAIN = 4 * SB_W + SSD_W + XBC_W
QKV_W = 3 * SB_W
REST_W = N_MAIN - QKV_W
COL_GATE = 3 * SB_W
COL_Z = 4 * SB_W
COL_XBC = 5 * SB_W
D_IN = N_MAIN + N_HEADS
W_ALL = N_MAIN + 128
WIN_STEP = 1664
WIN_W = 1792
EPS = 1e-5
N_CHIPS = 4
W_IN_SHARD = D_IN // N_CHIPS
W_OUT_SHARD = 2 * D_MODEL // N_CHIPS

ADAM_LR = 0.001
ADAM_B1 = 0.9
ADAM_B2 = 0.999
ADAM_EPS = 1e-08
ADAM_WD = 0.01
ADAM_STEP = 10

_SDS = jax.ShapeDtypeStruct
_NT = (((1,), (1,)), ((), ()))
_TN = (((0,), (0,)), ((), ()))
_VMEM = pltpu.VMEM


def _params(sem=None, vmem_mb=None):
    kw = {}
    if sem is not None:
        kw["dimension_semantics"] = sem
    if vmem_mb is not None:
        kw["vmem_limit_bytes"] = vmem_mb * 1024 * 1024
    return pltpu.CompilerParams(**kw)


def _mm(a, b):
    return jnp.dot(a.astype(_MXU), b.astype(_MXU), preferred_element_type=F32)


def _mm_nt(a, b):
    return lax.dot_general(a.astype(_MXU), b.astype(_MXU), _NT, preferred_element_type=F32)


def _mm_tn(a, b):
    return lax.dot_general(a.astype(_MXU), b.astype(_MXU), _TN, preferred_element_type=F32)


def _split(x, parts):
    out = []
    r = x
    for _ in range(parts):
        p = r.astype(_MXU)
        out.append(p)
        r = r - p.astype(F32)
    return out


def _sel_right(x, m01, parts=2):
    acc = None
    for p in _split(x, parts):
        t = jnp.dot(p, m01, preferred_element_type=F32)
        acc = t if acc is None else acc + t
    return acc


def _sel_left(m01, x, parts=2):
    acc = None
    for p in _split(x, parts):
        t = jnp.dot(m01, p, preferred_element_type=F32)
        acc = t if acc is None else acc + t
    return acc


def _iota(shape, axis):
    return lax.broadcasted_iota(jnp.int32, shape, axis)


def _sigmoid(x):
    return 1.0 / (1.0 + jnp.exp(-x))


def _prep(x2d, meta_full, norm_w):
    seq = x2d.shape[0]
    lp = seq + OFF
    nb = lp // TM

    def body(x_ref, meta_ref, w_ref, h0_ref, u_ref, ut_ref):
        i = pl.program_id(0)

        @pl.when(i == 0)
        def _():
            h0_ref[...] = jnp.concatenate([jnp.zeros((PAD, D_MODEL), F32), meta_ref[...]], axis=0)

        @pl.when(i > 0)
        def _():
            h0_ref[...] = x_ref[...]

        h = h0_ref[...]
        rs = lax.rsqrt(jnp.mean(h * h, axis=-1, keepdims=True) + EPS)
        u = (h * rs * w_ref[...]).astype(_MXU)
        u_ref[...] = u
        ut_ref[...] = u.T

    return pl.pallas_call(
        body, name="prep", grid=(nb,),
        in_specs=[pl.BlockSpec((TM, D_MODEL), lambda i: (jnp.maximum(i - 1, 0), 0)),
                  pl.BlockSpec((N_META, D_MODEL), lambda i: (0, 0)),
                  pl.BlockSpec((1, D_MODEL), lambda i: (0, 0))],
        out_specs=[pl.BlockSpec((TM, D_MODEL), lambda i: (i, 0)),
                   pl.BlockSpec((TM, D_MODEL), lambda i: (i, 0)),
                   pl.BlockSpec((D_MODEL, TM), lambda i: (0, i))],
        out_shape=[_SDS((lp, D_MODEL), F32), _SDS((lp, D_MODEL), _MXU), _SDS((D_MODEL, lp), _MXU)],
        compiler_params=_params(("arbitrary",)),
    )(x2d, meta_full, norm_w)


def _inproj(u, w_t):
    lp = u.shape[0]
    tn = 512

    nq = QKV_W // tn

    def body(u_ref, w_ref, wdt_ref, qkv_ref, rest_ref, odt_ref):
        j = pl.program_id(0)
        res = lax.dot_general(u_ref[...], w_ref[...], _NT, preferred_element_type=F32)

        @pl.when(j < nq)
        def _():
            qkv_ref[...] = res.astype(qkv_ref.dtype)

        @pl.when(j >= nq)
        def _():
            rest_ref[...] = res

        @pl.when(j == 0)
        def _():
            odt_ref[...] = lax.dot_general(u_ref[...], wdt_ref[...], _NT, preferred_element_type=F32)

    return pl.pallas_call(
        body, name="inproj", grid=(N_MAIN // tn,),
        in_specs=[pl.BlockSpec((lp, D_MODEL), lambda j: (0, 0)),
                  pl.BlockSpec((tn, D_MODEL), lambda j: (j, 0)),
                  pl.BlockSpec((128, D_MODEL), lambda j: (N_MAIN // 128, 0))],
        out_specs=[pl.BlockSpec((lp, tn), lambda j: (0, jnp.minimum(j, nq - 1))),
                   pl.BlockSpec((lp, tn), lambda j: (0, jnp.maximum(j - nq, 0))),
                   pl.BlockSpec((lp, 128), lambda j: (0, 0))],
        out_shape=[_SDS((lp, QKV_W), _MXU), _SDS((lp, REST_W), F32), _SDS((lp, 128), F32)],
        compiler_params=_params(("arbitrary",), 48),
    )(u, w_t, w_t)


SB_WINDOW = 3
SB_TOP = 16
SB_DEAD = -104.0


def _sb_logs(qh, kwin):
    z = lax.dot_general(qh, kwin, _NT, preferred_element_type=F32)
    e = jnp.exp(-jnp.abs(z))
    l1p = jnp.log(1.0 + e)
    lk_full = -(jnp.maximum(z, 0.0) + l1p)
    ls = jnp.minimum(z, 0.0) - l1p
    return z, e, ls, lk_full


def _blk(a, b):
    return a[:, b * TM:(b + 1) * TM]


def _stacked_sel(blocks, m01, parts):
    n = len(blocks)
    rows = blocks[0].shape[0]
    pieces = [_split(b, parts) for b in blocks]
    stacked = jnp.concatenate([p[k] for k in range(parts) for p in pieces], axis=0)
    res = jnp.dot(stacked, m01, preferred_element_type=F32)
    out = []
    for j in range(n):
        tot = res[j * rows:(j + 1) * rows]
        for k in range(1, parts):
            tot = tot + res[(k * n + j) * rows:(k * n + j + 1) * rows]
        out.append(tot)
    return out


def _sb_weights(ls, lk_full, run, last_mask, upper, n):
    lk = [_blk(lk_full, b) for b in range(n)]
    lk[n - 1] = jnp.where(last_mask, lk[n - 1], 0.0)
    aft = _stacked_sel(lk, upper, 1)
    w = [None] * n
    for b in range(n - 1, -1, -1):
        wb = jnp.exp(_blk(ls, b) + aft[b] + run)
        w[b] = jnp.where(last_mask, wb, 0.0) if b == n - 1 else wb
        run = run + jnp.sum(lk[b], axis=1, keepdims=True)
    return w, run


def _sb_alive(run_scr):
    top = jnp.max(run_scr[:, 0:SB_TOP, :]) > SB_DEAD
    rest = jnp.max(run_scr[:, SB_TOP:, :]) > SB_DEAD
    return top.astype(jnp.int32), rest.astype(jnp.int32)


def _sb_walk(i, key_set, strict, run_scr):
    @pl.when(i >= SB_WINDOW - 1)
    def _():
        key_set(i - (SB_WINDOW - 1), SB_WINDOW, strict, TM)

    start = jnp.where(i >= SB_WINDOW - 1, i - SB_WINDOW, i)

    def cond(c):
        return jnp.logical_and(c[0] >= 0, c[1] + c[2] > 0)

    def step(c):
        kb, _, rest = c
        mask = jnp.logical_or(strict, kb < i)

        @pl.when(rest > 0)
        def _():
            key_set(kb, 1, mask, TM)

        @pl.when(rest == 0)
        def _():
            key_set(kb, 1, mask, SB_TOP)

        return (kb - 1,) + _sb_alive(run_scr)

    lax.while_loop(cond, step, (start,) + _sb_alive(run_scr))


SB_LANES_FWD = 256
SB_LANES_BWD = 256


def _head_masks(lanes):
    lane = _iota((TM, lanes), 1)
    return tuple(jnp.logical_and(lane >= h * HEAD, lane < (h + 1) * HEAD) for h in range(lanes // HEAD))


def _by_head(hmask, parts):
    out = parts[-1]
    for h in range(len(parts) - 2, -1, -1):
        out = jnp.where(hmask[h], parts[h], out)
    return out


def _sb_fwd(qkv, fetch=()):
    lp = qkv.shape[0]
    nb = lp // TM
    lw = SB_LANES_FWD
    nh = lw // HEAD
    npair = SB_W // lw
    nf = len(fetch)
    host_in, host_out, host_shapes, host_sems = _gather_specs(fetch, nf)

    def body(q_ref, k_ref, v_ref, *rest):
        srcs, (o_ref, olo_ref), dsts = rest[:nf], rest[nf:nf + 2], rest[nf + 2:2 * nf + 2]
        acc, run_scr = rest[2 * nf + 2:2 * nf + 4]
        host_sem_refs = rest[2 * nf + 4:]
        p = pl.program_id(0)
        i = pl.program_id(1)

        if nf:
            @pl.when(jnp.logical_and(p == 0, i == 0))
            def _():
                for cp in _gather_copies(srcs, dsts, nf, *host_sem_refs)[0]:
                    cp.start()

        lane = _iota((TM, TM), 1)
        row = _iota((TM, TM), 0)
        hmask = _head_masks(lw)
        upper = (row > lane).astype(_MXU)
        strict = lane < row
        q = q_ref[...] * (1.0 / math.sqrt(HEAD))
        qh = tuple(jnp.where(m, q, 0.0).astype(_MXU) for m in hmask)

        def key_set(first, n, last_mask, nrows):
            off = pl.multiple_of(first * TM, TM)
            kwin = k_ref[pl.ds(off, n * TM), :].astype(_MXU)
            vwin = v_ref[pl.ds(off, n * TM), :].astype(_MXU)
            for hh in range(nh):
                run = run_scr[hh, 0:nrows, 0:1]
                _, _, ls, lk_full = _sb_logs(qh[hh][0:nrows], kwin)
                w, run = _sb_weights(ls, lk_full, run, last_mask[0:nrows], upper, n)
                pieces = [_split(wb, 2) for wb in w]
                stacked = jnp.concatenate(
                    [jnp.concatenate([p[0] for p in pieces], axis=1), jnp.concatenate([p[1] for p in pieces], axis=1)], axis=0)
                res = jnp.dot(stacked, vwin, preferred_element_type=F32)
                acc[hh, 0:nrows] += res[0:nrows]
                acc[nh + hh, 0:nrows] += res[nrows:2 * nrows]
                run_scr[hh, 0:nrows] = jnp.broadcast_to(run, (nrows, TM))

        acc[...] = jnp.zeros_like(acc)
        run_scr[...] = jnp.zeros_like(run_scr)
        _sb_walk(i, key_set, strict, run_scr)
        o_ref[...] = _by_head(hmask, [acc[h] for h in range(nh)])
        olo_ref[...] = _by_head(hmask, [acc[nh + h] for h in range(nh)])

        if nf:
            @pl.when(jnp.logical_and(p == npair - 1, i == nb - 1))
            def _():
                _gather_finish(*_gather_copies(srcs, dsts, nf, *host_sem_refs))

    blk = pl.BlockSpec((TM, lw), lambda p, i: (i, p))
    res = pl.pallas_call(
        body, name="sb_fwd", grid=(npair, nb),
        in_specs=[blk,
                  pl.BlockSpec((lp, lw), lambda p, i: (0, npair + p)),
                  pl.BlockSpec((lp, lw), lambda p, i: (0, 2 * npair + p))] + host_in,
        out_specs=[blk, blk] + host_out,
        out_shape=[_SDS((lp, SB_W), F32), _SDS((lp, SB_W), F32)] + host_shapes,
        scratch_shapes=[pltpu.VMEM((2 * nh, TM, lw), F32), pltpu.VMEM((nh, TM, TM), F32)] + host_sems,
        compiler_params=_params(("arbitrary", "arbitrary")),
    )(qkv, qkv, qkv, *fetch)
    return res[0], res[1], _own_slot(res[2:], fetch)


def _sb_bwd(qkv, o_sb, o_lo, d_o, d_proj, send=(), dests=()):
    lp = qkv.shape[0]
    nb = lp // TM
    lw = SB_LANES_BWD
    nh = lw // HEAD
    npair = SB_W // lw
    scale = 1.0 / math.sqrt(HEAD)
    n = len(send)
    host_in, host_out, host_shapes, host_sems = _host_specs(send)

    def body(q_ref, k_ref, v_ref, o_ref, olo_ref, do_ref, dproj_in, *rest):
        srcs, dproj_ref, dsts = rest[:n], rest[n], rest[n + 1:2 * n + 1]
        dq_all, dk_ref, dv_ref, stage, sems, dq_acc, run_scr, gsum_scr = rest[2 * n + 1:2 * n + 9]
        host_sem_refs = rest[2 * n + 9:]
        p = pl.program_id(0)
        i = pl.program_id(1)

        if n:
            @pl.when(jnp.logical_and(p == 0, i == 0))
            def _():
                _start_copies(_slab_copies(srcs, dsts, dests, *host_sem_refs))

        @pl.when(i == 0)
        def _():
            dk_ref[...] = jnp.zeros_like(dk_ref)
            dv_ref[...] = jnp.zeros_like(dv_ref)

        lane = _iota((TM, TM), 1)
        row = _iota((TM, TM), 0)
        hmask = _head_masks(lw)
        upper = (row > lane).astype(_MXU)
        lower_incl = (row >= lane).astype(_MXU)
        strict = lane < row
        q = q_ref[...] * scale
        do = do_ref[...]
        prod = do.astype(_MXU).astype(F32) * (o_ref[...] + olo_ref[...])
        qh = tuple(jnp.where(m, q, 0.0).astype(_MXU) for m in hmask)
        doh = tuple(jnp.where(m, do, 0.0).astype(_MXU) for m in hmask)
        gtot = tuple(jnp.sum(jnp.where(m, prod, 0.0), axis=1, keepdims=True) for m in hmask)

        def key_set(first, n, last_mask, nrows):
            off = pl.multiple_of(first * TM, TM)
            kf = k_ref[pl.ds(off, n * TM), :]
            kwin = kf.astype(_MXU)
            vwin = v_ref[pl.ds(off, n * TM), :].astype(_MXU)
            last_mask = last_mask[0:nrows]
            dk_win = None
            for hh in range(nh):
                run = run_scr[hh, 0:nrows, 0:1]
                gsum = gsum_scr[hh, 0:nrows, 0:1]
                z, e, ls, lk_full = _sb_logs(qh[hh][0:nrows], kwin)
                w, run = _sb_weights(ls, lk_full, run, last_mask, upper, n)
                r = 1.0 / (1.0 + e)
                er = e * r
                pos = z >= 0.0
                beta = jnp.where(pos, r, er)
                one_m_beta = jnp.where(pos, er, r)
                dw = lax.dot_general(doh[hh][0:nrows], vwin, _NT, preferred_element_type=F32)
                g = [_blk(dw, b) * w[b] for b in range(n)]
                suffix = _stacked_sel(g, lower_incl, 2)
                dz = [None] * n
                for b in range(n - 1, -1, -1):
                    prefix = gtot[hh][0:nrows] - gsum - suffix[b]
                    d = g[b] * _blk(one_m_beta, b) - _blk(beta, b) * prefix
                    dz[b] = (jnp.where(last_mask, d, 0.0) if b == n - 1 else d).astype(_MXU)
                    gsum = gsum + jnp.sum(g[b], axis=1, keepdims=True)
                dzw = jnp.concatenate(dz, axis=1)
                ww = jnp.concatenate([wb.astype(_MXU) for wb in w], axis=1)
                kh = jnp.where(hmask[hh][0:1, :], kf, 0.0).astype(_MXU)
                dq_acc[0:nrows] += jnp.dot(dzw, kh, preferred_element_type=F32)
                dk_h = lax.dot_general(dzw, qh[hh][0:nrows], _TN, preferred_element_type=F32)
                dv_h = lax.dot_general(ww, doh[hh][0:nrows], _TN, preferred_element_type=F32)
                dk_win = (dk_h, dv_h) if dk_win is None else (dk_win[0] + dk_h, dk_win[1] + dv_h)
                run_scr[hh, 0:nrows] = jnp.broadcast_to(run, (nrows, TM))
                gsum_scr[hh, 0:nrows] = jnp.broadcast_to(gsum, (nrows, TM))
            dk_ref[pl.ds(off, n * TM), :] += dk_win[0]
            dv_ref[pl.ds(off, n * TM), :] += dk_win[1]

        dq_acc[...] = jnp.zeros_like(dq_acc)
        run_scr[...] = jnp.zeros_like(run_scr)
        gsum_scr[...] = jnp.zeros_like(gsum_scr)

        _sb_walk(i, key_set, strict, run_scr)
        dq_all[pl.ds(pl.multiple_of(i * TM, TM), TM), :] = dq_acc[...] * scale

        @pl.when(i == nb - 1)
        def _():
            copies = []
            for s, src in enumerate((dq_all, dk_ref, dv_ref)):
                stage[s] = src[...].astype(_MXU)
                col = pl.multiple_of((s * npair + p) * lw, lw)
                copies.append(pltpu.make_async_copy(stage.at[s], dproj_ref.at[:, pl.ds(col, lw)], sems.at[s]))
                copies[-1].start()
            for cp in copies:
                cp.wait()

        if n:
            @pl.when(jnp.logical_and(p == npair - 1, i == nb - 1))
            def _():
                _finish_copies(_slab_copies(srcs, dsts, dests, *host_sem_refs))

    blk = pl.BlockSpec((TM, lw), lambda p, i: (i, p))
    res = pl.pallas_call(
        body, name="sb_bwd", grid=(npair, nb),
        in_specs=[blk,
                  pl.BlockSpec((lp, lw), lambda p, i: (0, npair + p)),
                  pl.BlockSpec((lp, lw), lambda p, i: (0, 2 * npair + p)),
                  blk, blk, blk, pl.BlockSpec(memory_space=pl.ANY)] + host_in,
        out_specs=[pl.BlockSpec(memory_space=pl.ANY)] + host_out,
        out_shape=[_SDS(d_proj.shape, d_proj.dtype)] + host_shapes,
        input_output_aliases={6: 0},
        scratch_shapes=[pltpu.VMEM((lp, lw), F32), pltpu.VMEM((lp, lw), F32), pltpu.VMEM((lp, lw), F32),
                        pltpu.VMEM((3, lp, lw), _MXU), pltpu.SemaphoreType.DMA((3,)),
                        pltpu.VMEM((TM, lw), F32), pltpu.VMEM((nh, TM, TM), F32),
                        pltpu.VMEM((nh, TM, TM), F32)] + host_sems,
        compiler_params=_params(("arbitrary", "arbitrary")),
    )(qkv, qkv, qkv, o_sb, o_lo, d_o, d_proj, *send)
    return res[0], list(res[1:])


def _conv_pre(x_ref, w_ref, b_ref, lp):
    n = lp - 8
    w = w_ref[...]
    pre = (x_ref[pl.ds(5, n), :] * w[0:1, :] + x_ref[pl.ds(6, n), :] * w[1:2, :]
           + x_ref[pl.ds(7, n), :] * w[2:3, :] + x_ref[pl.ds(8, n), :] * w[3:4, :]) + b_ref[...]
    live = (_iota((n, 128), 0) + 8) >= PAD
    return pre, live


def _conv_fwd(proj, dt_raw, conv_w, conv_b, dt_bias128):
    lp = proj.shape[0]
    nblk = XBC_W // 128
    c0 = (COL_XBC - QKV_W) // 128

    def body(x_ref, w_ref, b_ref, dtr_ref, dtb_ref, o_ref, dt_ref):
        pre, live = _conv_pre(x_ref, w_ref, b_ref, lp)
        act = pre * _sigmoid(pre)
        o_ref[pl.ds(0, 8), :] = jnp.zeros((8, 128), F32)
        o_ref[pl.ds(8, lp - 8), :] = jnp.where(live, act, 0.0)

        @pl.when(pl.program_id(0) == 0)
        def _():
            s = dtr_ref[...] + dtb_ref[...]
            sp = jnp.maximum(s, 0.0) + jnp.log(1.0 + jnp.exp(-jnp.abs(s)))
            dt_ref[...] = jnp.where(_iota((lp, 128), 0) >= PAD, sp, 0.0)

    return pl.pallas_call(
        body, name="conv_fwd", grid=(nblk,),
        in_specs=[pl.BlockSpec((lp, 128), lambda j: (0, c0 + j)),
                  pl.BlockSpec((4, 128), lambda j: (0, j)),
                  pl.BlockSpec((1, 128), lambda j: (0, j)),
                  pl.BlockSpec((lp, 128), lambda j: (0, 0)),
                  pl.BlockSpec((1, 128), lambda j: (0, 0))],
        out_specs=[pl.BlockSpec((lp, 128), lambda j: (0, j)),
                   pl.BlockSpec((lp, 128), lambda j: (0, 0))],
        out_shape=[_SDS((lp, XBC_W), F32), _SDS((lp, 128), F32)],
        compiler_params=_params(("arbitrary",)),
    )(proj, conv_w, conv_b, dt_raw, dt_bias128)


def _conv_bwd(proj, dt_raw, conv_w, conv_b, dt_bias128, d_xbc, d_dt128, d_proj):
    lp = proj.shape[0]
    nblk = XBC_W // 128
    c0 = COL_XBC // 128
    c0_in = (COL_XBC - QKV_W) // 128
    n = lp - 8
    last = nblk - 1

    def body(x_ref, w_ref, b_ref, dtr_ref, dtb_ref, dy_ref, ddt_ref, dproj_in,
             dx_ref, gw_ref, gb_ref, gdtb_ref, scr):
        j = pl.program_id(0)

        @pl.when(j < nblk)
        def _():
            pre, live = _conv_pre(x_ref, w_ref, b_ref, lp)
            sg = _sigmoid(pre)
            dpre = jnp.where(live, dy_ref[pl.ds(8, n), :] * (sg * (1.0 + pre * (1.0 - sg))), 0.0)
            gb_ref[...] = jnp.sum(dpre, axis=0, keepdims=True)
            gw_ref[...] = jnp.concatenate(
                [jnp.sum(dpre * x_ref[pl.ds(5 + k, n), :], axis=0, keepdims=True) for k in range(4)], axis=0)
            scr[pl.ds(0, 8), :] = jnp.zeros((8, 128), F32)
            scr[pl.ds(8, n), :] = dpre
            scr[pl.ds(lp, 8), :] = jnp.zeros((8, 128), F32)
            w = w_ref[...]
            dx_ref[...] = (scr[pl.ds(0, lp), :] * w[3:4, :] + scr[pl.ds(1, lp), :] * w[2:3, :]
                           + scr[pl.ds(2, lp), :] * w[1:2, :] + scr[pl.ds(3, lp), :] * w[0:1, :]).astype(dx_ref.dtype)

        @pl.when(j == nblk)
        def _():
            s = dtr_ref[...] + dtb_ref[...]
            d = jnp.where(_iota((lp, 128), 0) >= PAD, ddt_ref[...] * _sigmoid(s), 0.0)
            dx_ref[...] = d.astype(dx_ref.dtype)
            gdtb_ref[...] = jnp.sum(d, axis=0, keepdims=True)

    clamp = lambda j: (0, jnp.minimum(j, last))
    full128 = pl.BlockSpec((lp, 128), lambda j: (0, 0))
    return pl.pallas_call(
        body, name="conv_bwd", grid=(nblk + 1,),
        in_specs=[pl.BlockSpec((lp, 128), lambda j: (0, c0_in + jnp.minimum(j, last))),
                  pl.BlockSpec((4, 128), clamp),
                  pl.BlockSpec((1, 128), clamp),
                  full128, pl.BlockSpec((1, 128), lambda j: (0, 0)),
                  pl.BlockSpec((lp, 128), clamp), full128, pl.BlockSpec(memory_space=pl.ANY)],
        out_specs=[pl.BlockSpec((lp, 128), lambda j: (0, c0 + j)), pl.BlockSpec((4, 128), clamp),
                   pl.BlockSpec((1, 128), clamp), pl.BlockSpec((1, 128), lambda j: (0, 0))],
        out_shape=[_SDS(d_proj.shape, d_proj.dtype), _SDS((4, XBC_W), F32), _SDS((1, XBC_W), F32), _SDS((1, 128), F32)],
        input_output_aliases={7: 0},
        scratch_shapes=[pltpu.VMEM((lp + 8, 128), F32)],
        compiler_params=_params(("arbitrary",)),
    )(proj, conv_w, conv_b, dt_raw, dt_bias128, d_xbc, d_dt128, d_proj)


def _ssd_pieces(dt, dt_t, a, a_t):
    r64 = _iota((CHUNK, CHUNK), 0)
    c64 = _iota((CHUNK, CHUNK), 1)
    tril = c64 <= r64
    tril01 = tril.astype(_MXU)
    triu01 = (r64 <= c64).astype(_MXU)
    expand = (lax.shift_right_logical(_iota((N_HEADS, SSD_W), 1), 6) == _iota((N_HEADS, SSD_W), 0)).astype(_MXU)
    acum = _sel_left(tril01, dt * a)
    acum_t = _sel_right(dt_t * a_t, triu01)
    ax = _sel_right(acum, expand)
    dtx = _sel_right(dt, expand)
    return tril, expand, acum, acum_t, ax, dtx


def _seg_matrix():
    return (lax.shift_right_logical(_iota((SSD_W, N_HEADS), 0), 6) == _iota((SSD_W, N_HEADS), 1)).astype(_MXU)


def _head_decay(ax, acum_t, h, tril):
    col = ax[:, h * HEAD:(h + 1) * HEAD]
    rowv = acum_t[h:h + 1, :]
    return jnp.where(tril, jnp.exp(jnp.minimum(col - rowv, 0.0)), 0.0)


def _ssd_fwd(xbc, dt_c, dt_tc, a, a_t, dskip_x):
    lp = xbc.shape[0]
    nc = lp // CHUNK
    gw = SSD_W // N_GROUPS
    hpg = N_HEADS // N_GROUPS

    def body(x_ref, dt_ref, dtt_ref, a_ref, at_ref, d_ref, y_ref, st_ref, state):
        c = pl.program_id(0)

        @pl.when(c == 0)
        def _():
            state[...] = jnp.zeros_like(state)

        st_ref[0] = state[...]
        tril, _, _, acum_t, ax, dtx = _ssd_pieces(dt_ref[0], dtt_ref[0], a_ref[...], at_ref[...])
        x = x_ref[:, 0:SSD_W]
        xdt = x * dtx
        ea = jnp.exp(ax)
        aex = ax[CHUNK - 1:CHUNK, :]
        wd = jnp.exp(aex - ax)
        eae = jnp.exp(aex)
        xw = xdt * wd
        y_ref[...] = x * d_ref[...]
        for g in range(N_GROUPS):
            gs = slice(g * gw, (g + 1) * gw)
            rs = slice(g * N_STATE, (g + 1) * N_STATE)
            bg = x_ref[:, SSD_W + g * N_STATE:SSD_W + (g + 1) * N_STATE]
            cg = x_ref[:, SSD_W + N_GROUPS * N_STATE + g * N_STATE:SSD_W + N_GROUPS * N_STATE + (g + 1) * N_STATE]
            sg = state[rs, :]
            cb = _mm_nt(cg, bg)
            y_ref[:, gs] += _mm(cg, sg) * ea[:, gs]
            for r in range(hpg):
                h = g * hpg + r
                hs = slice(h * HEAD, (h + 1) * HEAD)
                m = cb * _head_decay(ax, acum_t, h, tril)
                y_ref[:, hs] += _mm(m, xdt[:, hs])
            state[rs, :] = sg * eae[:, gs] + _mm_tn(bg, xw[:, gs])

    return pl.pallas_call(
        body, name="ssd_fwd", grid=(nc,),
        in_specs=[pl.BlockSpec((CHUNK, XBC_W), lambda c: (c, 0)),
                  pl.BlockSpec((1, CHUNK, N_HEADS), lambda c: (c, 0, 0)),
                  pl.BlockSpec((1, N_HEADS, CHUNK), lambda c: (c, 0, 0)),
                  pl.BlockSpec((1, N_HEADS), lambda c: (0, 0)),
                  pl.BlockSpec((N_HEADS, 1), lambda c: (0, 0)),
                  pl.BlockSpec((1, SSD_W), lambda c: (0, 0))],
        out_specs=[pl.BlockSpec((CHUNK, SSD_W), lambda c: (c, 0)),
                   pl.BlockSpec((1, N_GROUPS * N_STATE, gw), lambda c: (c, 0, 0))],
        out_shape=[_SDS((lp, SSD_W), F32), _SDS((nc, N_GROUPS * N_STATE, gw), F32)],
        scratch_shapes=[pltpu.VMEM((N_GROUPS * N_STATE, gw), F32)],
        compiler_params=_params(("arbitrary",)),
    )(xbc, dt_c, dt_tc, a, a_t, dskip_x)


def _ssd_bwd(xbc, dt_c, dt_tc, a, a_t, dskip_x, states, d_y):
    lp = xbc.shape[0]
    nc = lp // CHUNK
    gw = SSD_W // N_GROUPS
    hpg = N_HEADS // N_GROUPS

    def body(x_ref, dt_ref, dtt_ref, a_ref, at_ref, d_ref, st_ref, dy_ref,
             dx_ref, ddta_ref, ddtb_ref, ga1_ref, ga2_ref, gd_ref, dstate, dxdt_scr, z_scr, yoff_scr, sds_scr):
        c = pl.program_id(0)

        @pl.when(c == 0)
        def _():
            dstate[...] = jnp.zeros_like(dstate)
            ga1_ref[...] = jnp.zeros_like(ga1_ref)
            ga2_ref[...] = jnp.zeros_like(ga2_ref)
            gd_ref[...] = jnp.zeros_like(gd_ref)

        dt = dt_ref[0]
        dt_t = dtt_ref[0]
        a = a_ref[...]
        a_t = at_ref[...]
        tril, _, acum, acum_t, ax, dtx = _ssd_pieces(dt, dt_t, a, a_t)
        seg = _seg_matrix()
        x = x_ref[:, 0:SSD_W]
        dy = dy_ref[...]
        xdt = x * dtx
        ea = jnp.exp(ax)
        aex = ax[CHUNK - 1:CHUNK, :]
        wd = jnp.exp(aex - ax)
        eae = jnp.exp(aex)
        xw = xdt * wd
        edy = ea * dy
        lane16 = _iota((CHUNK, N_HEADS), 1)
        row16 = _iota((N_HEADS, CHUNK), 0)
        da_col = jnp.zeros((CHUNK, N_HEADS), F32)
        da_row = jnp.zeros((N_HEADS, CHUNK), F32)
        for g in range(N_GROUPS):
            gs = slice(g * gw, (g + 1) * gw)
            rs = slice(g * N_STATE, (g + 1) * N_STATE)
            bcol = slice(SSD_W + g * N_STATE, SSD_W + (g + 1) * N_STATE)
            ccol = slice(SSD_W + N_GROUPS * N_STATE + g * N_STATE, SSD_W + N_GROUPS * N_STATE + (g + 1) * N_STATE)
            bg = x_ref[:, bcol]
            cg = x_ref[:, ccol]
            sg = st_ref[0, rs, :]
            dsn = dstate[rs, :]
            cb = _mm_nt(cg, bg)
            z_scr[:, gs] = _mm(bg, dsn)
            yoff_scr[:, gs] = _mm(cg, sg) * ea[:, gs]
            sds_scr[:, gs] = jnp.broadcast_to(jnp.sum(dsn * sg, axis=0, keepdims=True), (8, gw))
            dcb = jnp.zeros((CHUNK, CHUNK), F32)
            for r in range(hpg):
                h = g * hpg + r
                hs = slice(h * HEAD, (h + 1) * HEAD)
                dec = _head_decay(ax, acum_t, h, tril)
                m = cb * dec
                t1 = _mm_nt(dy[:, hs], xdt[:, hs])
                dcb = dcb + dec * t1
                tm = m * t1
                da_col = da_col + jnp.where(lane16 == h, jnp.sum(tm, axis=1, keepdims=True), 0.0)
                da_row = da_row - jnp.where(row16 == h, jnp.sum(tm, axis=0, keepdims=True), 0.0)
                dxdt_scr[:, hs] = _mm_tn(m, dy[:, hs])
            dx_ref[:, ccol] = _mm(dcb, bg) + _mm_nt(edy[:, gs], sg)
            dx_ref[:, bcol] = _mm_tn(dcb, cg) + _mm_nt(xw[:, gs], dsn)
            dstate[rs, :] = eae[:, gs] * dsn + _mm_tn(cg, edy[:, gs])
        zf = z_scr[...]
        dxdt = dxdt_scr[...] + wd * zf
        t3 = _sel_right(xw * zf, seg)
        da_col = da_col + _sel_right(dy * yoff_scr[...], seg) - t3
        aend = acum[CHUNK - 1:CHUNK, :]
        sd = _sel_right(sds_scr[...], seg)[0:1, :] * jnp.exp(aend)
        last = jnp.sum(t3, axis=0, keepdims=True) + sd
        da_col = da_col + jnp.where(_iota((CHUNK, N_HEADS), 0) == CHUNK - 1, last, 0.0)
        r64 = _iota((CHUNK, CHUNK), 0)
        c64 = _iota((CHUNK, CHUNK), 1)
        ddta1 = _sel_left((c64 >= r64).astype(_MXU), da_col)
        ddta2 = _sel_right(da_row, (r64 >= c64).astype(_MXU))
        ddta_ref[0] = a * ddta1 + _sel_right(dxdt * x, seg)
        ddtb_ref[0] = a_t * ddta2
        ga1_ref[...] += jnp.sum(dt * ddta1, axis=0, keepdims=True)
        ga2_ref[...] += jnp.sum(dt_t * ddta2, axis=1, keepdims=True)
        dx_ref[:, 0:SSD_W] = dxdt * dtx + d_ref[...] * dy
        gd_ref[...] += jnp.sum(dy * x, axis=0, keepdims=True)

    rev = lambda c: (nc - 1 - c, 0)
    rev3 = lambda c: (nc - 1 - c, 0, 0)
    return pl.pallas_call(
        body, name="ssd_bwd", grid=(nc,),
        in_specs=[pl.BlockSpec((CHUNK, XBC_W), rev),
                  pl.BlockSpec((1, CHUNK, N_HEADS), rev3),
                  pl.BlockSpec((1, N_HEADS, CHUNK), rev3),
                  pl.BlockSpec((1, N_HEADS), lambda c: (0, 0)),
                  pl.BlockSpec((N_HEADS, 1), lambda c: (0, 0)),
                  pl.BlockSpec((1, SSD_W), lambda c: (0, 0)),
                  pl.BlockSpec((1, N_GROUPS * N_STATE, gw), rev3),
                  pl.BlockSpec((CHUNK, SSD_W), rev)],
        out_specs=[pl.BlockSpec((CHUNK, XBC_W), rev),
                   pl.BlockSpec((1, CHUNK, N_HEADS), rev3),
                   pl.BlockSpec((1, N_HEADS, CHUNK), rev3),
                   pl.BlockSpec((1, N_HEADS), lambda c: (0, 0)),
                   pl.BlockSpec((N_HEADS, 1), lambda c: (0, 0)),
                   pl.BlockSpec((1, SSD_W), lambda c: (0, 0))],
        out_shape=[_SDS((lp, XBC_W), F32), _SDS((nc, CHUNK, N_HEADS), F32), _SDS((nc, N_HEADS, CHUNK), F32),
                   _SDS((1, N_HEADS), F32), _SDS((N_HEADS, 1), F32), _SDS((1, SSD_W), F32)],
        scratch_shapes=[pltpu.VMEM((N_GROUPS * N_STATE, gw), F32), pltpu.VMEM((CHUNK, SSD_W), F32),
                        pltpu.VMEM((CHUNK, SSD_W), F32), pltpu.VMEM((CHUNK, SSD_W), F32),
                        pltpu.VMEM((8, SSD_W), F32)],
        compiler_params=_params(("arbitrary",)),
    )(xbc, dt_c, dt_tc, a, a_t, dskip_x, states, d_y)


def _gated_norm(o, gate, w):
    sg = _sigmoid(gate)
    p = o * (gate * sg)
    rs = lax.rsqrt(jnp.mean(p * p, axis=-1, keepdims=True) + EPS)
    n = p * rs
    return sg, rs, n, n * w


def _tail_fwd(o_sb, o_ssd, proj, h0, target, w_out, sb_w, ssd_w, fin_w):
    lp = o_sb.shape[0]
    nb = lp // TM
    row = lambda i: (i, 0)
    one = lambda i: (0, 0)

    def body(osb_ref, gate_ref, ossd_ref, z_ref, h0_ref, tgt_ref, wo_ref, sbw_ref, ssdw_ref, fw_ref,
             dh1_ref, loss_ref, gfw_ref):
        i = pl.program_id(0)

        @pl.when(i == 0)
        def _():
            loss_ref[...] = jnp.zeros_like(loss_ref)
            gfw_ref[...] = jnp.zeros_like(gfw_ref)

        y1 = _gated_norm(osb_ref[...], gate_ref[...], sbw_ref[...])[3]
        y2 = _gated_norm(ossd_ref[...], z_ref[...], ssdw_ref[...])[3]
        h1 = (h0_ref[...] + _mm(y1, wo_ref[0:SB_W, :])) + _mm(y2, wo_ref[SB_W:SB_W + SSD_W, :])
        rs1 = lax.rsqrt(jnp.mean(h1 * h1, axis=-1, keepdims=True) + EPS)
        n1 = h1 * rs1
        fw = fw_ref[...]
        diff = jnp.where(i > 0, n1 * fw - tgt_ref[...], 0.0)
        loss_ref[...] += jnp.sum(diff * diff, axis=0, keepdims=True)
        d_out = diff * (1.0 / D_MODEL)
        gfw_ref[...] += jnp.sum(d_out * n1, axis=0, keepdims=True)
        g = d_out * fw
        dh1_ref[...] = rs1 * (g - n1 * jnp.mean(g * n1, axis=-1, keepdims=True))

    return pl.pallas_call(
        body, name="tail_fwd", grid=(nb,),
        in_specs=[pl.BlockSpec((TM, SB_W), row),
                  pl.BlockSpec((TM, SB_W), lambda i: (i, (COL_GATE - QKV_W) // SB_W)),
                  pl.BlockSpec((TM, SSD_W), row),
                  pl.BlockSpec((TM, SSD_W), lambda i: (i, (COL_Z - QKV_W) // SSD_W)),
                  pl.BlockSpec((TM, D_MODEL), row),
                  pl.BlockSpec((TM, D_MODEL), lambda i: (jnp.maximum(i - 1, 0), 0)),
                  pl.BlockSpec(memory_space=_VMEM),
                  pl.BlockSpec((1, SB_W), one), pl.BlockSpec((1, SSD_W), one), pl.BlockSpec((1, D_MODEL), one)],
        out_specs=[pl.BlockSpec((TM, D_MODEL), row), pl.BlockSpec((1, D_MODEL), one), pl.BlockSpec((1, D_MODEL), one)],
        out_shape=[_SDS((lp, D_MODEL), F32), _SDS((1, D_MODEL), F32), _SDS((1, D_MODEL), F32)],
        compiler_params=_params(("arbitrary",), 40),
    )(o_sb, proj, o_ssd, proj, h0, target, w_out, sb_w, ssd_w, fin_w)


def _gated_norm_bwd(o, gate, w, dy):
    sg, rs, n, _ = _gated_norm(o, gate, w)
    gw = jnp.sum(dy * n, axis=0, keepdims=True)
    dn = dy * w
    dp = rs * (dn - n * jnp.mean(dn * n, axis=-1, keepdims=True))
    d_o = dp * (gate * sg)
    d_gate = dp * o * (sg * (1.0 + gate * (1.0 - sg)))
    return d_o, d_gate, gw, n * w


def _tail_bwd(o_sb, o_ssd, proj, d_h1, w_out, sb_w, ssd_w):
    lp = o_sb.shape[0]
    tm = 272 if lp % 272 == 0 else TM
    nb = lp // tm
    row = lambda i, t: (i, 0)
    one = lambda i, t: (0, 0)

    def body(osb_ref, gate_ref, ossd_ref, z_ref, dh1_ref, wo_ref, sbw_ref, ssdw_ref,
             dosb_ref, dossd_ref, dproj_ref, gwo_ref, gsb_ref, gssd_ref):
        i = pl.program_id(0)
        t = pl.program_id(1)

        @pl.when(jnp.logical_and(i == 0, t == 0))
        def _():
            gwo_ref[...] = jnp.zeros_like(gwo_ref)
            gsb_ref[...] = jnp.zeros_like(gsb_ref)
            gssd_ref[...] = jnp.zeros_like(gssd_ref)

        dh1 = dh1_ref[...].astype(_MXU)

        def half(o_ref, g_ref, w_ref, do_ref, gn_ref, r0):
            dy = lax.dot_general(dh1, wo_ref[r0:r0 + SB_W, :], _NT, preferred_element_type=F32)
            d_o, d_g, gw, y = _gated_norm_bwd(o_ref[...], g_ref[...], w_ref[...], dy)
            do_ref[...] = d_o
            dproj_ref[...] = d_g.astype(_MXU)
            gn_ref[...] += gw
            gwo_ref[r0:r0 + SB_W, :] += lax.dot_general(y.astype(_MXU), dh1, _TN, preferred_element_type=F32)

        @pl.when(t == 0)
        def _():
            half(osb_ref, gate_ref, sbw_ref, dosb_ref, gsb_ref, 0)

        @pl.when(t == 1)
        def _():
            half(ossd_ref, z_ref, ssdw_ref, dossd_ref, gssd_ref, SB_W)

    tile = pl.BlockSpec((tm, SB_W), row)
    return pl.pallas_call(
        body, name="tail_bwd", grid=(nb, 2),
        in_specs=[tile, pl.BlockSpec((tm, SB_W), lambda i, t: (i, (COL_GATE - QKV_W) // SB_W)),
                  tile, pl.BlockSpec((tm, SSD_W), lambda i, t: (i, (COL_Z - QKV_W) // SSD_W)),
                  tile, pl.BlockSpec(memory_space=_VMEM),
                  pl.BlockSpec((1, SB_W), one), pl.BlockSpec((1, SSD_W), one)],
        out_specs=[tile, tile, pl.BlockSpec((tm, SB_W), lambda i, t: (i, COL_GATE // SB_W + t)),
                   pl.BlockSpec((SB_W + SSD_W, D_MODEL), one), pl.BlockSpec((1, SB_W), one), pl.BlockSpec((1, SSD_W), one)],
        out_shape=[_SDS((lp, SB_W), F32), _SDS((lp, SSD_W), F32), _SDS((lp, W_ALL), _MXU),
                   _SDS((SB_W + SSD_W, D_MODEL), F32), _SDS((1, SB_W), F32), _SDS((1, SSD_W), F32)],
        compiler_params=_params(("arbitrary", "arbitrary"), 48),
    )(o_sb, proj, o_ssd, proj, d_h1, w_out, sb_w, ssd_w)


def _d_u(d_proj, w_t, send=(), dests=()):
    lp = d_proj.shape[0]
    tk = 512
    steps = N_MAIN // tk
    n = len(send)
    host_in, host_out, host_shapes, host_sems = _host_specs(send)

    def body(dp_ref, w_ref, dpdt_ref, wdt_ref, *rest):
        srcs, o_ref, dsts, sems = rest[:n], rest[n], rest[n + 1:2 * n + 1], rest[2 * n + 1:]
        j = pl.program_id(0)

        @pl.when(j == 0)
        def _():
            if n:
                _start_copies(_slab_copies(srcs, dsts, dests, *sems))
            o_ref[...] = jnp.dot(dpdt_ref[...], wdt_ref[...], preferred_element_type=F32)

        o_ref[...] += jnp.dot(dp_ref[...], w_ref[...], preferred_element_type=F32)

        if n:
            @pl.when(j == steps - 1)
            def _():
                _finish_copies(_slab_copies(srcs, dsts, dests, *sems))

    res = pl.pallas_call(
        body, name="d_u", grid=(steps,),
        in_specs=[pl.BlockSpec((lp, tk), lambda j: (0, j)),
                  pl.BlockSpec((tk, D_MODEL), lambda j: (j, 0)),
                  pl.BlockSpec((lp, 128), lambda j: (0, N_MAIN // 128)),
                  pl.BlockSpec((128, D_MODEL), lambda j: (N_MAIN // 128, 0))] + host_in,
        out_specs=[pl.BlockSpec((lp, D_MODEL), lambda j: (0, 0))] + host_out,
        out_shape=[_SDS((lp, D_MODEL), F32)] + host_shapes,
        scratch_shapes=host_sems,
        compiler_params=_params(("arbitrary",), 48),
    )(d_proj, w_t, d_proj, w_t, *send)
    return res[0], list(res[1:])


def _norm_bwd(du_all, h0, d_h1, norm_w):
    lp = h0.shape[0]
    nb = lp // TM
    seq = lp - OFF
    row = lambda i: (i, 0)
    one = lambda i: (0, 0)

    def body(du_ref, h0_ref, dh1_ref, nw_ref, gx_ref, gmeta_ref, gnw_ref):
        i = pl.program_id(0)

        @pl.when(i == 0)
        def _():
            gnw_ref[...] = jnp.zeros_like(gnw_ref)

        du = du_ref[...]
        h = h0_ref[...]
        rs = lax.rsqrt(jnp.mean(h * h, axis=-1, keepdims=True) + EPS)
        n0 = h * rs
        gnw_ref[...] += jnp.sum(du * n0, axis=0, keepdims=True)
        g = du * nw_ref[...]
        dh0 = dh1_ref[...] + rs * (g - n0 * jnp.mean(g * n0, axis=-1, keepdims=True))

        @pl.when(i == 0)
        def _():
            gmeta_ref[...] = dh0[PAD:PAD + N_META, :]

        @pl.when(i > 0)
        def _():
            gx_ref[...] = dh0

    tile = pl.BlockSpec((TM, D_MODEL), row)
    return pl.pallas_call(
        body, name="norm_bwd", grid=(nb,),
        in_specs=[tile, tile, tile, pl.BlockSpec((1, D_MODEL), one)],
        out_specs=[pl.BlockSpec((TM, D_MODEL), lambda i: (jnp.maximum(i - 1, 0), 0)),
                   pl.BlockSpec((N_META, D_MODEL), one), pl.BlockSpec((1, D_MODEL), one)],
        out_shape=[_SDS((seq, D_MODEL), F32), _SDS((N_META, D_MODEL), F32), _SDS((1, D_MODEL), F32)],
        compiler_params=_params(("arbitrary",)),
    )(du_all, h0, d_h1, norm_w)


def _grad_w_windows(u_t, d_proj, first, count, name):
    lp = d_proj.shape[0]
    hw = WIN_W // 2
    steps = 2 * count

    def body(ut_ref, dp_hbm, o_ref, buf, sems):
        s = pl.program_id(0)
        slot = s % 2

        def fetch(step, sl):
            start = pl.multiple_of((first + step // 2) * WIN_STEP + (step % 2) * hw, 128)
            return pltpu.make_async_copy(dp_hbm.at[:, pl.ds(start, hw)], buf.at[sl], sems.at[sl])

        @pl.when(s == 0)
        def _():
            fetch(0, 0).start()

        @pl.when(s + 1 < steps)
        def _():
            fetch(s + 1, 1 - slot).start()

        fetch(s, slot).wait()
        o_ref[0] = jnp.dot(ut_ref[...], buf[slot], preferred_element_type=F32).astype(o_ref.dtype)

    return pl.pallas_call(
        body, name=name, grid=(steps,),
        in_specs=[pl.BlockSpec((D_MODEL, lp), lambda s: (0, 0)), pl.BlockSpec(memory_space=pl.ANY)],
        out_specs=pl.BlockSpec((1, D_MODEL, hw), lambda s: (s // 2, 0, s % 2)),
        out_shape=_SDS((count, D_MODEL, WIN_W), _MXU),
        scratch_shapes=[pltpu.VMEM((2, lp, hw), _MXU), pltpu.SemaphoreType.DMA((2,))],
        compiler_params=_params(("arbitrary",), 40),
    )(u_t, d_proj)


def _device_grads(x2d, target2d, meta_full, norm_w, w_t, conv_w, conv_b, dt_bias, a_log, d_skip,
                  sb_w, ssd_w, w_out, fin_w, exchange=None, w_out_shard=None):
    lp = x2d.shape[0] + OFF
    nc = lp // CHUNK
    h0, u, u_t = _prep(x2d, meta_full, norm_w)
    qkv, proj, dt_raw = _inproj(u, w_t)
    if w_out is None:
        o_sb, o_lo, (w_out_shards,) = _sb_fwd(qkv, (w_out_shard,))
        w_out = w_out_shards.reshape(2 * D_MODEL, D_MODEL)
    else:
        o_sb, o_lo, _ = _sb_fwd(qkv)
    dt_bias128 = jnp.pad(dt_bias, ((0, 0), (0, 128 - N_HEADS)))
    xbc, dt128 = _conv_fwd(proj, dt_raw, conv_w, conv_b, dt_bias128)
    dt_c = dt128[:, :N_HEADS].reshape(nc, CHUNK, N_HEADS)
    dt_tc = jnp.swapaxes(dt_c, 1, 2)
    a = -jnp.exp(a_log)
    a_t = a.reshape(N_HEADS, 1)
    dskip_x = jnp.repeat(d_skip, HEAD, axis=1)
    o_ssd, states = _ssd_fwd(xbc, dt_c, dt_tc, a, a_t, dskip_x)
    d_h1, sq_err, g_fin = _tail_fwd(o_sb, o_ssd, proj, h0, target2d, w_out, sb_w, ssd_w, fin_w)

    d_osb, d_ossd, d_proj, g_wout, g_sb, g_ssd = _tail_bwd(o_sb, o_ssd, proj, d_h1, w_out, sb_w, ssd_w)
    d_xbc_act, ddt_a, ddt_b, ga1, ga2, gd = _ssd_bwd(xbc, dt_c, dt_tc, a, a_t, dskip_x, states, d_ossd)
    d_dt = (ddt_a + jnp.swapaxes(ddt_b, 1, 2)).reshape(lp, N_HEADS)
    d_dt128 = jnp.pad(d_dt, ((0, 0), (0, 128 - N_HEADS)))
    d_proj, g_convw, g_convb, g_dtb128 = _conv_bwd(proj, dt_raw, conv_w, conv_b, dt_bias128, d_xbc_act, d_dt128, d_proj)
    send_e, dests_e = ((), ()) if exchange is None else exchange["early"](g_wout)
    d_proj, arrived_e = _sb_bwd(qkv, o_sb, o_lo, d_osb, d_proj, send_e, dests_e)
    g_win = _grad_w_windows(u_t, d_proj, 0, N_CHIPS, "grad_w_in")
    send_l, dests_l = ((), ()) if exchange is None else exchange["late"](g_win)
    d_u, arrived_l = _d_u(d_proj, w_t, send_l, dests_l)
    send, arrived = tuple(send_e) + tuple(send_l), tuple(arrived_e) + tuple(arrived_l)
    g_x, g_meta, g_nw = _norm_bwd(d_u, h0, d_h1, norm_w)
    g_alog = (ga1 + ga2.reshape(1, N_HEADS)) * a
    g_dskip = gd.reshape(N_HEADS, HEAD).sum(axis=1).reshape(1, N_HEADS)
    grads = dict(meta_tokens=g_meta, norm_w=g_nw, w_in=g_win, conv_w=g_convw, conv_b=g_convb,
                 dt_bias=g_dtb128[:, :N_HEADS], a_log=g_alog, d_skip=g_dskip, sb_norm_w=g_sb, ssd_norm_w=g_ssd,
                 w_out=g_wout, final_norm_w=g_fin, sent=send, arrived=arrived)
    return sq_err, g_x, grads


_MESH = pl.DeviceIdType.MESH
_ANY = pl.BlockSpec(memory_space=pl.ANY)


def _place():
    return lax.axis_index("x"), lax.axis_index("y"), lax.axis_index("c")


def _other_chips(x, y):
    return ((1 - x, y), (x, 1 - y), (1 - x, 1 - y))


def _gather_copies(srcs, dsts, n_big, send_sems, recv_sems, fwd_send, fwd_recv):
    x, y, c = _place()
    mine = 2 * x + y
    first, passed = [], []
    for a in range(len(srcs)):
        half = srcs[a].shape[1] // 2
        window = pl.ds(pl.multiple_of(c * half, 128), half)
        for k, (px, py) in enumerate(_other_chips(x, y)):
            if a < n_big:
                src, dst = srcs[a].at[:, window], dsts[a].at[mine, :, window]
                landed = dsts[a].at[2 * px + py, :, window]
                passed.append(pltpu.make_async_remote_copy(
                    src_ref=landed, dst_ref=landed, send_sem=fwd_send.at[a * 3 + k], recv_sem=fwd_recv.at[a * 3 + k],
                    device_id=(x, y, 1 - c), device_id_type=_MESH))
            else:
                src, dst = srcs[a], dsts[a].at[mine]
                passed.append(None)
            first.append(pltpu.make_async_remote_copy(
                src_ref=src, dst_ref=dst, send_sem=send_sems.at[a * 3 + k], recv_sem=recv_sems.at[a * 3 + k],
                device_id=(px, py, c), device_id_type=_MESH))
    return first, passed


def _gather_finish(first, passed):
    for cp, fwd in zip(first, passed):
        cp.wait_recv()
        if fwd is not None:
            fwd.start()
    for fwd in passed:
        if fwd is not None:
            fwd.wait_recv()
    for cp in first + [fwd for fwd in passed if fwd is not None]:
        cp.wait_send()


def _gather_specs(arrays, n_big):
    n = len(arrays)
    hbm = [pl.BlockSpec(memory_space=pl.ANY)] * n
    shapes = [_SDS((N_CHIPS,) + a.shape, a.dtype) for a in arrays]
    sems = [pltpu.SemaphoreType.DMA((3 * n,)), pltpu.SemaphoreType.DMA((3 * n,)),
            pltpu.SemaphoreType.DMA((3 * max(n_big, 1),)), pltpu.SemaphoreType.DMA((3 * max(n_big, 1),))] if n else []
    return hbm, hbm, shapes, sems


def _own_slot(got, arrays):
    if not arrays:
        return []
    mine = 2 * lax.axis_index("x") + lax.axis_index("y")
    return [lax.dynamic_update_slice(g, a[None], (mine,) + (0,) * a.ndim) for g, a in zip(got, arrays)]


def _gather_shards(arrays, n_big):
    n = len(arrays)

    def body(*refs):
        first, passed = _gather_copies(refs[:n], refs[n:2 * n], n_big, *refs[2 * n:])
        for cp in first:
            cp.start()
        _gather_finish(first, passed)

    hbm_in, hbm_out, shapes, sems = _gather_specs(arrays, n_big)
    got = pl.pallas_call(
        body, name="gather_shards", in_specs=hbm_in, out_specs=hbm_out, out_shape=shapes, scratch_shapes=sems,
    )(*arrays)
    return _own_slot(got, arrays)


def _slab_copies(srcs, dsts, dests, send_sems, recv_sems):
    x, y, c = _place()
    mine = 2 * x + y
    copies = []
    for a in range(len(srcs)):
        lo, hi = dests[a]
        receives = jnp.logical_and(mine >= lo, mine < hi)
        for k, (px, py) in enumerate(_other_chips(x, y)):
            target = 2 * px + py
            cp = pltpu.make_async_remote_copy(
                src_ref=srcs[a].at[jnp.clip(target - lo, 0, hi - lo - 1)], dst_ref=dsts[a].at[mine],
                send_sem=send_sems.at[a * 3 + k], recv_sem=recv_sems.at[a * 3 + k],
                device_id=(px, py, c), device_id_type=_MESH)
            copies.append((cp, jnp.logical_and(target >= lo, target < hi), receives))
    return copies


def _start_copies(copies):
    for cp, sends, _ in copies:
        pl.when(sends)(cp.start)


def _finish_copies(copies):
    for cp, _, receives in copies:
        pl.when(receives)(cp.wait_recv)
    for cp, sends, _ in copies:
        pl.when(sends)(cp.wait_send)


def _host_specs(send):
    n = len(send)
    hbm = [pl.BlockSpec(memory_space=pl.ANY)] * n
    shapes = [_SDS((N_CHIPS,) + a.shape[1:], a.dtype) for a in send]
    sems = [pltpu.SemaphoreType.DMA((3 * n,)), pltpu.SemaphoreType.DMA((3 * n,))] if n else []
    return hbm, hbm, shapes, sems


def _swap_halves(arrays, name):
    n = len(arrays)

    def body(*refs):
        srcs, dsts = refs[:n], refs[n:2 * n]
        send_sems, recv_sems = refs[2 * n:]
        x, y, c = _place()
        copies = []
        for a in range(n):
            half = arrays[a].shape[1] // 2
            cp = pltpu.make_async_remote_copy(
                src_ref=srcs[a].at[:, pl.ds(pl.multiple_of((1 - c) * half, 16), half)], dst_ref=dsts[a],
                send_sem=send_sems.at[a], recv_sem=recv_sems.at[a],
                device_id=(x, y, 1 - c), device_id_type=_MESH)
            cp.start()
            copies.append(cp)
        for cp in copies:
            cp.wait_recv()
        for cp in copies:
            cp.wait_send()

    return pl.pallas_call(
        body, name=name,
        in_specs=[_ANY] * n, out_specs=[_ANY] * n,
        out_shape=[_SDS((a.shape[0], a.shape[1] // 2, a.shape[2]), a.dtype) for a in arrays],
        scratch_shapes=[pltpu.SemaphoreType.DMA((n,)), pltpu.SemaphoreType.DMA((n,))],
    )(*arrays)


N_DEV = 8
SMALL_ROWS = 32
SMALL_COLS = XBC_W


def _final_exchange(arrays, by_cols, packed):
    n = len(arrays)

    def body(*refs):
        src_ref = refs[n]
        dsts = refs[n + 1:2 * n + 1]
        dst_ref = refs[2 * n + 1]
        send_sems, recv_sems, all_send, all_recv, local_sem = refs[2 * n + 2:]
        x, y, c = _place()
        me = 4 * x + 2 * y + c
        own = pltpu.make_async_copy(src_ref, dst_ref.at[me], local_sem)
        own.start()
        copies = []
        for k in range(1, N_DEV):
            bx, by, bc = (k >> 2) & 1, (k >> 1) & 1, k & 1
            peer = (x + bx - 2 * x * bx, y + by - 2 * y * by, c + bc - 2 * c * bc)
            cp = pltpu.make_async_remote_copy(
                src_ref=src_ref, dst_ref=dst_ref.at[me], send_sem=all_send.at[k - 1], recv_sem=all_recv.at[k - 1],
                device_id=peer, device_id_type=_MESH)
            cp.start()
            copies.append(cp)
        for a in range(n):
            if by_cols[a]:
                half = arrays[a].shape[1] // 2
                mine = dsts[a].at[:, pl.ds(pl.multiple_of(c * half, 128), half)]
            else:
                half = arrays[a].shape[0] // 2
                mine = dsts[a].at[pl.ds(pl.multiple_of(c * half, 16), half)]
            cp = pltpu.make_async_remote_copy(
                src_ref=mine, dst_ref=mine, send_sem=send_sems.at[a], recv_sem=recv_sems.at[a],
                device_id=(x, y, 1 - c), device_id_type=_MESH)
            cp.start()
            copies.append(cp)
        for cp in copies:
            cp.wait_recv()
        for cp in copies:
            cp.wait_send()
        own.wait()

    vm = pl.BlockSpec(memory_space=_VMEM)
    res = pl.pallas_call(
        body, name="final_exchange",
        in_specs=[_ANY] * n + [vm], out_specs=[_ANY] * n + [vm],
        out_shape=[_SDS(a.shape, a.dtype) for a in arrays] + [_SDS((N_DEV, SMALL_ROWS, SMALL_COLS), F32)],
        input_output_aliases={a: a for a in range(n)},
        scratch_shapes=[pltpu.SemaphoreType.DMA((n,)), pltpu.SemaphoreType.DMA((n,)),
                        pltpu.SemaphoreType.DMA((N_DEV - 1,)), pltpu.SemaphoreType.DMA((N_DEV - 1,)),
                        pltpu.SemaphoreType.DMA],
    )(*arrays, packed)
    return list(res[:n]), res[n]


def _adamw(w, g, m, v):
    m = ADAM_B1 * m + (1.0 - ADAM_B1) * g
    v = ADAM_B2 * v + (1.0 - ADAM_B2) * (g * g)
    m_hat = m / (1.0 - ADAM_B1 ** ADAM_STEP)
    v_hat = v / (1.0 - ADAM_B2 ** ADAM_STEP)
    delta = -ADAM_LR * (m_hat / (jnp.sqrt(v_hat) + ADAM_EPS) + ADAM_WD * w)
    return delta, m, v


def _sum_slabs(slabs, core, name, transposed=False):
    _, h, c = slabs.shape
    tr = 128
    nblk = h // tr

    def body(core_ref, s_ref, o_ref):
        tot = ((s_ref[0].astype(F32) + s_ref[1].astype(F32)) + s_ref[2].astype(F32)) + s_ref[3].astype(F32)
        o_ref[...] = tot.T if transposed else tot

    if transposed:
        out_spec = pl.BlockSpec((c, tr), lambda i, core_ref: (0, core_ref[0] * nblk + i))
        out_shape = _SDS((c, 2 * h), F32)
    else:
        out_spec = pl.BlockSpec((tr, c), lambda i, core_ref: (core_ref[0] * nblk + i, 0))
        out_shape = _SDS((2 * h, c), F32)
    grid_spec = pltpu.PrefetchScalarGridSpec(
        num_scalar_prefetch=1, grid=(nblk,),
        in_specs=[pl.BlockSpec((N_CHIPS, tr, c), lambda i, core_ref: (0, i, 0))],
        out_specs=out_spec)
    return pl.pallas_call(
        body, name=name, grid_spec=grid_spec, out_shape=out_shape,
        compiler_params=_params(("arbitrary",)),
    )(core, slabs)


def _add_halves(own, recv, core, name):
    ns, r, c = own.shape
    half = r // 2
    tr = 128
    nblk = half // tr

    def body(core_ref, a_ref, b_ref, o_ref):
        o_ref[...] = (a_ref[...].astype(F32) + b_ref[...].astype(F32)).astype(o_ref.dtype)

    grid_spec = pltpu.PrefetchScalarGridSpec(
        num_scalar_prefetch=1, grid=(nblk,),
        in_specs=[pl.BlockSpec((ns, tr, c), lambda i, core_ref: (0, core_ref[0] * nblk + i, 0)),
                  pl.BlockSpec((ns, tr, c), lambda i, core_ref: (0, i, 0))],
        out_specs=pl.BlockSpec((ns, tr, c), lambda i, core_ref: (0, i, 0)))
    return pl.pallas_call(
        body, name=name, grid_spec=grid_spec, out_shape=_SDS((ns, half, c), own.dtype),
        compiler_params=_params(("arbitrary",)),
    )(core, own, recv)


def _update_big(w, m, v, g, name):
    r, c = w.shape

    def body(w_ref, m_ref, v_ref, g_ref, d_ref, mo_ref, vo_ref):
        delta, m_new, v_new = _adamw(w_ref[...], g_ref[...], m_ref[...], v_ref[...])
        d_ref[...] = delta
        mo_ref[...] = m_new
        vo_ref[...] = v_new

    if r % 128 == 0:
        steps, spec = r // 128, pl.BlockSpec((128, c), lambda i: (i, 0))
    else:
        steps, spec = c // 128, pl.BlockSpec((r, 128), lambda i: (0, i))
    return pl.pallas_call(
        body, name=name, grid=(steps,),
        in_specs=[spec] * 4, out_specs=[spec] * 3,
        out_shape=[_SDS((r, c), F32)] * 3,
        compiler_params=_params(("arbitrary",)),
    )(w, m, v, g)


_ROW = dict(norm_w=0, sb_norm_w=1, ssd_norm_w=2, final_norm_w=3, conv_b=4, dt_bias=5, a_log=6, d_skip=7,
            conv_w=8, sq_err=12, meta_tokens=16)
_SMALL = ("meta_tokens", "norm_w", "conv_w", "conv_b", "dt_bias", "a_log", "d_skip", "sb_norm_w", "ssd_norm_w",
          "final_norm_w")


def _pack_small(sq_err, grads):
    def rowpad(a):
        return jnp.pad(a, ((0, 0), (0, SMALL_COLS - a.shape[1])))

    rows = [rowpad(grads[k]) for k in ("norm_w", "sb_norm_w", "ssd_norm_w", "final_norm_w", "conv_b", "dt_bias", "a_log", "d_skip")]
    rows.append(grads["conv_w"])
    rows.append(rowpad(sq_err))
    rows.append(jnp.zeros((3, SMALL_COLS), F32))
    rows.append(rowpad(grads["meta_tokens"]))
    return jnp.concatenate(rows, axis=0)


def _update_small(gathered, ws, ms, vs):
    names = _SMALL
    n = len(names)

    def body(*refs):
        g_ref = refs[0]
        w_refs, m_refs, v_refs = refs[1:1 + n], refs[1 + n:1 + 2 * n], refs[1 + 2 * n:1 + 3 * n]
        outs = refs[1 + 3 * n:]
        loss_ref = outs[0]
        go, do, mo, vo = outs[1:1 + n], outs[1 + n:1 + 2 * n], outs[1 + 2 * n:1 + 3 * n], outs[1 + 3 * n:1 + 4 * n]
        tot = g_ref[0]
        for d in range(1, N_DEV):
            tot = tot + g_ref[d]
        x, y, _ = _place()
        chip = 2 * x + y
        loss_ref[...] = jnp.broadcast_to(
            0.5 * jnp.sum(tot[_ROW["sq_err"]:_ROW["sq_err"] + 1, 0:D_MODEL], axis=1, keepdims=True) / D_MODEL, (1, 128))
        for idx, nm in enumerate(names):
            r0 = _ROW[nm]
            rows, cols = w_refs[idx].shape
            if nm in ("conv_w", "meta_tokens"):
                g = jnp.zeros((rows, cols), F32)
                for j in range(N_CHIPS):
                    g = g + jnp.where(chip == j, tot[r0:r0 + rows, j * cols:(j + 1) * cols], 0.0)
            else:
                g = tot[r0:r0 + rows, 0:cols]
            delta, m_new, v_new = _adamw(w_refs[idx][...], g, m_refs[idx][...], v_refs[idx][...])
            go[idx][...] = g
            do[idx][...] = delta
            mo[idx][...] = m_new
            vo[idx][...] = v_new

    shapes = [_SDS(ws[nm].shape, F32) for nm in names]
    vm = pl.BlockSpec(memory_space=_VMEM)
    res = pl.pallas_call(
        body, name="update_small",
        in_specs=[vm] * (1 + 3 * n), out_specs=[vm] * (1 + 4 * n),
        out_shape=[_SDS((1, 128), F32)] + shapes * 4,
    )(gathered, *[ws[nm] for nm in names], *[ms[nm] for nm in names], *[vs[nm] for nm in names])
    loss = res[0][0, 0]
    g = dict(zip(names, res[1:1 + n]))
    d = dict(zip(names, res[1 + n:1 + 2 * n]))
    m = dict(zip(names, res[1 + 2 * n:1 + 3 * n]))
    v = dict(zip(names, res[1 + 3 * n:1 + 4 * n]))
    return loss, g, d, m, v


_WEIGHTS = ("meta_tokens", "norm_w", "w_in", "conv_w", "conv_b", "dt_bias", "a_log", "d_skip", "sb_norm_w",
            "ssd_norm_w", "w_out", "final_norm_w")


def kernel(x, meta_tokens, norm_w, w_in, conv_w, conv_b, dt_bias, a_log, d_skip, sb_norm_w, ssd_norm_w, w_out, final_norm_w, loss_target, m_meta_tokens, m_norm_w, m_w_in, m_conv_w, m_conv_b, m_dt_bias, m_a_log, m_d_skip, m_sb_norm_w, m_ssd_norm_w, m_w_out, m_final_norm_w, v_meta_tokens, v_norm_w, v_w_in, v_conv_w, v_conv_b, v_dt_bias, v_a_log, v_d_skip, v_sb_norm_w, v_ssd_norm_w, v_w_out, v_final_norm_w):
    given = dict(meta_tokens=meta_tokens, norm_w=norm_w, w_in=w_in, conv_w=conv_w, conv_b=conv_b, dt_bias=dt_bias,
                 a_log=a_log, d_skip=d_skip, sb_norm_w=sb_norm_w, ssd_norm_w=ssd_norm_w, w_out=w_out,
                 final_norm_w=final_norm_w)
    mom = dict(meta_tokens=m_meta_tokens, norm_w=m_norm_w, w_in=m_w_in, conv_w=m_conv_w, conv_b=m_conv_b,
               dt_bias=m_dt_bias, a_log=m_a_log, d_skip=m_d_skip, sb_norm_w=m_sb_norm_w, ssd_norm_w=m_ssd_norm_w,
               w_out=m_w_out, final_norm_w=m_final_norm_w)
    var = dict(meta_tokens=v_meta_tokens, norm_w=v_norm_w, w_in=v_w_in, conv_w=v_conv_w, conv_b=v_conv_b,
               dt_bias=v_dt_bias, a_log=v_a_log, d_skip=v_d_skip, sb_norm_w=v_sb_norm_w, ssd_norm_w=v_ssd_norm_w,
               w_out=v_w_out, final_norm_w=v_final_norm_w)
    seq = x.shape[1]

    def two_d(a):
        return a.reshape((-1, a.shape[-1])) if a.ndim != 2 else a

    def rows_first(a):
        return jnp.transpose(a, (2, 0, 1)).reshape(W_IN_SHARD, D_MODEL)

    def rows_last(a):
        return jnp.transpose(a.reshape(W_IN_SHARD, 1, D_MODEL), (1, 2, 0))

    w_in_t, m_in_t, v_in_t = rows_first(w_in), rows_first(m_w_in), rows_first(v_w_in)

    g_win, g_meta, g_cw = _gather_shards([w_in_t.astype(_MXU), meta_tokens, conv_w[0]], 1)
    w_t = jnp.pad(g_win.reshape(D_IN, D_MODEL), ((0, W_ALL - D_IN), (0, 0)))
    meta_full = jnp.swapaxes(g_meta, 0, 1).reshape(N_META, D_MODEL)
    conv_w_full = jnp.swapaxes(g_cw, 0, 1).reshape(4, XBC_W)

    core = lax.axis_index("c").astype(jnp.int32).reshape(1)

    def early(g_wout):
        slab_out = g_wout.reshape(N_CHIPS, W_OUT_SHARD, D_MODEL).astype(_MXU)
        (sib_out,) = _swap_halves([slab_out], "swap_halves_w_out")
        return (_add_halves(slab_out, sib_out, core, "chip_sum_w_out"),), ((0, N_CHIPS),)

    def late(g_win):
        (sib_in,) = _swap_halves([g_win], "swap_halves_w_in")
        return (_add_halves(g_win, sib_in, core, "chip_sum_w_in"),), ((0, N_CHIPS),)

    sq_err, g_x, grads = _device_grads(
        x.reshape(seq, D_MODEL), loss_target.reshape(seq, D_MODEL), meta_full, norm_w, w_t, conv_w_full,
        conv_b, dt_bias, a_log, d_skip, sb_norm_w, ssd_norm_w, None, final_norm_w.reshape(1, D_MODEL),
        exchange=dict(early=early, late=late), w_out_shard=w_out[0].astype(_MXU))
    chip_out, chip_in = grads["sent"]
    got_out, got_in = grads["arrived"]
    chip = 2 * lax.axis_index("x") + lax.axis_index("y")

    def with_own(got, sent):
        own = lax.dynamic_slice(sent, (chip, 0, 0), (1,) + sent.shape[1:])
        return lax.dynamic_update_slice(got, own, (chip, 0, 0))

    (g_in, g_out), gathered = _final_exchange(
        [_sum_slabs(with_own(got_in, chip_in), core, "sum_w_in", transposed=True),
         _sum_slabs(with_own(got_out, chip_out), core, "sum_w_out")], (True, False), _pack_small(sq_err, grads))
    g_in = lax.dynamic_slice(g_in, (4 * chip, 0), (W_IN_SHARD, D_MODEL))
    big = dict(w_in=tuple(rows_last(a) for a in (g_in,) + tuple(_update_big(w_in_t, m_in_t, v_in_t, g_in, "update_w_in"))),
               w_out=(g_out,) + tuple(_update_big(w_out[0], m_w_out[0], v_w_out[0], g_out, "update_w_out")))

    loss, sg, sd, sm, sv = _update_small(
        gathered, {k: two_d(given[k]) for k in _SMALL}, {k: two_d(mom[k]) for k in _SMALL},
        {k: two_d(var[k]) for k in _SMALL})

    out = {}
    for idx, group in enumerate((sg, sd, sm, sv)):
        for k in _SMALL:
            out[(idx, k)] = group[k].reshape(given[k].shape)
        for k in ("w_in", "w_out"):
            out[(idx, k)] = big[k][idx].reshape(given[k].shape)
    return (loss, g_x.reshape(x.shape), *[out[(idx, k)] for idx in range(4) for k in _WEIGHTS])
```

```python
import functools
import math

import jax
import jax.numpy as jnp
from jax import lax
from jax.experimental import pallas as pl
from jax.experimental.pallas import tpu as pltpu

F32 = jnp.float32
_MXU = jnp.bfloat16

D_MODEL = 1024
N_META = 16
PAD = 112
OFF = PAD + N_META
TM = 128
CHUNK = 64
SB_W = 1024
SSD_W = 1024
N_HEADS = 16
HEAD = 64
N_GROUPS = 2
N_STATE = 128
XBC_W = SSD_W + 2 * N_GROUPS * N_STATE
N_MAIN = 4 * SB_W + SSD_W + XBC_W
QKV_W = 3 * SB_W
REST_W = N_MAIN - QKV_W
COL_GATE = 3 * SB_W
COL_Z = 4 * SB_W
COL_XBC = 5 * SB_W
D_IN = N_MAIN + N_HEADS
W_ALL = N_MAIN + 128
WIN_STEP = 1664
WIN_W = 1792
EPS = 1e-5
N_CHIPS = 4
W_IN_SHARD = D_IN // N_CHIPS
W_OUT_SHARD = 2 * D_MODEL // N_CHIPS

ADAM_LR = 0.001
ADAM_B1 = 0.9
ADAM_B2 = 0.999
ADAM_EPS = 1e-08
ADAM_WD = 0.01
ADAM_STEP = 10

_SDS = jax.ShapeDtypeStruct
_NT = (((1,), (1,)), ((), ()))
_TN = (((0,), (0,)), ((), ()))
_VMEM = pltpu.VMEM


def _params(sem=None, vmem_mb=None):
    kw = {}
    if sem is not None:
        kw["dimension_semantics"] = sem
    if vmem_mb is not None:
        kw["vmem_limit_bytes"] = vmem_mb * 1024 * 1024
    return pltpu.CompilerParams(**kw)


def _mm(a, b):
    return jnp.dot(a.astype(_MXU), b.astype(_MXU), preferred_element_type=F32)


def _mm_nt(a, b):
    return lax.dot_general(a.astype(_MXU), b.astype(_MXU), _NT, preferred_element_type=F32)


def _mm_tn(a, b):
    return lax.dot_general(a.astype(_MXU), b.astype(_MXU), _TN, preferred_element_type=F32)


def _split(x, parts):
    out = []
    r = x
    for _ in range(parts):
        p = r.astype(_MXU)
        out.append(p)
        r = r - p.astype(F32)
    return out


def _sel_right(x, m01, parts=2):
    acc = None
    for p in _split(x, parts):
        t = jnp.dot(p, m01, preferred_element_type=F32)
        acc = t if acc is None else acc + t
    return acc


def _sel_left(m01, x, parts=2):
    acc = None
    for p in _split(x, parts):
        t = jnp.dot(m01, p, preferred_element_type=F32)
        acc = t if acc is None else acc + t
    return acc


def _iota(shape, axis):
    return lax.broadcasted_iota(jnp.int32, shape, axis)


def _sigmoid(x):
    return 1.0 / (1.0 + jnp.exp(-x))


def _prep(x2d, meta_full, norm_w):
    seq = x2d.shape[0]
    lp = seq + OFF
    nb = lp // TM

    def body(x_ref, meta_ref, w_ref, h0_ref, u_ref, ut_ref):
        i = pl.program_id(0)

        @pl.when(i == 0)
        def _():
            h0_ref[...] = jnp.concatenate([jnp.zeros((PAD, D_MODEL), F32), meta_ref[...]], axis=0)

        @pl.when(i > 0)
        def _():
            h0_ref[...] = x_ref[...]

        h = h0_ref[...]
        rs = lax.rsqrt(jnp.mean(h * h, axis=-1, keepdims=True) + EPS)
        u = (h * rs * w_ref[...]).astype(_MXU)
        u_ref[...] = u
        ut_ref[...] = u.T

    return pl.pallas_call(
        body, name="prep", grid=(nb,),
        in_specs=[pl.BlockSpec((TM, D_MODEL), lambda i: (jnp.maximum(i - 1, 0), 0)),
                  pl.BlockSpec((N_META, D_MODEL), lambda i: (0, 0)),
                  pl.BlockSpec((1, D_MODEL), lambda i: (0, 0))],
        out_specs=[pl.BlockSpec((TM, D_MODEL), lambda i: (i, 0)),
                   pl.BlockSpec((TM, D_MODEL), lambda i: (i, 0)),
                   pl.BlockSpec((D_MODEL, TM), lambda i: (0, i))],
        out_shape=[_SDS((lp, D_MODEL), F32), _SDS((lp, D_MODEL), _MXU), _SDS((D_MODEL, lp), _MXU)],
        compiler_params=_params(("arbitrary",)),
    )(x2d, meta_full, norm_w)


def _inproj(u, w_t):
    lp = u.shape[0]
    tn = 512

    nq = QKV_W // tn

    def body(u_ref, w_ref, wdt_ref, qkv_ref, rest_ref, odt_ref):
        j = pl.program_id(0)
        res = lax.dot_general(u_ref[...], w_ref[...], _NT, preferred_element_type=F32)

        @pl.when(j < nq)
        def _():
            qkv_ref[...] = res.astype(qkv_ref.dtype)

        @pl.when(j >= nq)
        def _():
            rest_ref[...] = res

        @pl.when(j == 0)
        def _():
            odt_ref[...] = lax.dot_general(u_ref[...], wdt_ref[...], _NT, preferred_element_type=F32)

    return pl.pallas_call(
        body, name="inproj", grid=(N_MAIN // tn,),
        in_specs=[pl.BlockSpec((lp, D_MODEL), lambda j: (0, 0)),
                  pl.BlockSpec((tn, D_MODEL), lambda j: (j, 0)),
                  pl.BlockSpec((128, D_MODEL), lambda j: (N_MAIN // 128, 0))],
        out_specs=[pl.BlockSpec((lp, tn), lambda j: (0, jnp.minimum(j, nq - 1))),
                   pl.BlockSpec((lp, tn), lambda j: (0, jnp.maximum(j - nq, 0))),
                   pl.BlockSpec((lp, 128), lambda j: (0, 0))],
        out_shape=[_SDS((lp, QKV_W), _MXU), _SDS((lp, REST_W), F32), _SDS((lp, 128), F32)],
        compiler_params=_params(("arbitrary",), 48),
    )(u, w_t, w_t)


SB_WINDOW = 3
SB_TOP = 16
SB_DEAD = -104.0


def _sb_logs(qh, kwin):
    z = lax.dot_general(qh, kwin, _NT, preferred_element_type=F32)
    e = jnp.exp(-jnp.abs(z))
    l1p = jnp.log(1.0 + e)
    lk_full = -(jnp.maximum(z, 0.0) + l1p)
    ls = jnp.minimum(z, 0.0) - l1p
    return z, e, ls, lk_full


def _blk(a, b):
    return a[:, b * TM:(b + 1) * TM]


def _stacked_sel(blocks, m01, parts):
    n = len(blocks)
    rows = blocks[0].shape[0]
    pieces = [_split(b, parts) for b in blocks]
    stacked = jnp.concatenate([p[k] for k in range(parts) for p in pieces], axis=0)
    res = jnp.dot(stacked, m01, preferred_element_type=F32)
    out = []
    for j in range(n):
        tot = res[j * rows:(j + 1) * rows]
        for k in range(1, parts):
            tot = tot + res[(k * n + j) * rows:(k * n + j + 1) * rows]
        out.append(tot)
    return out


def _sb_weights(ls, lk_full, run, last_mask, upper, n):
    lk = [_blk(lk_full, b) for b in range(n)]
    lk[n - 1] = jnp.where(last_mask, lk[n - 1], 0.0)
    aft = _stacked_sel(lk, upper, 1)
    w = [None] * n
    for b in range(n - 1, -1, -1):
        wb = jnp.exp(_blk(ls, b) + aft[b] + run)
        w[b] = jnp.where(last_mask, wb, 0.0) if b == n - 1 else wb
        run = run + jnp.sum(lk[b], axis=1, keepdims=True)
    return w, run


def _sb_alive(run_scr):
    top = jnp.max(run_scr[:, 0:SB_TOP, :]) > SB_DEAD
    rest = jnp.max(run_scr[:, SB_TOP:, :]) > SB_DEAD
    return top.astype(jnp.int32), rest.astype(jnp.int32)


def _sb_walk(i, key_set, strict, run_scr):
    @pl.when(i >= SB_WINDOW - 1)
    def _():
        key_set(i - (SB_WINDOW - 1), SB_WINDOW, strict, TM)

    start = jnp.where(i >= SB_WINDOW - 1, i - SB_WINDOW, i)

    def cond(c):
        return jnp.logical_and(c[0] >= 0, c[1] + c[2] > 0)

    def step(c):
        kb, _, rest = c
        mask = jnp.logical_or(strict, kb < i)

        @pl.when(rest > 0)
        def _():
            key_set(kb, 1, mask, TM)

        @pl.when(rest == 0)
        def _():
            key_set(kb, 1, mask, SB_TOP)

        return (kb - 1,) + _sb_alive(run_scr)

    lax.while_loop(cond, step, (start,) + _sb_alive(run_scr))


SB_LANES_FWD = 256
SB_LANES_BWD = 256


def _head_masks(lanes):
    lane = _iota((TM, lanes), 1)
    return tuple(jnp.logical_and(lane >= h * HEAD, lane < (h + 1) * HEAD) for h in range(lanes // HEAD))


def _by_head(hmask, parts):
    out = parts[-1]
    for h in range(len(parts) - 2, -1, -1):
        out = jnp.where(hmask[h], parts[h], out)
    return out


def _sb_fwd(qkv, fetch=()):
    lp = qkv.shape[0]
    nb = lp // TM
    lw = SB_LANES_FWD
    nh = lw // HEAD
    npair = SB_W // lw
    nf = len(fetch)
    host_in, host_out, host_shapes, host_sems = _gather_specs(fetch, nf)

    def body(q_ref, k_ref, v_ref, *rest):
        srcs, (o_ref, olo_ref), dsts = rest[:nf], rest[nf:nf + 2], rest[nf + 2:2 * nf + 2]
        acc, run_scr = rest[2 * nf + 2:2 * nf + 4]
        host_sem_refs = rest[2 * nf + 4:]
        p = pl.program_id(0)
        i = pl.program_id(1)

        if nf:
            @pl.when(jnp.logical_and(p == 0, i == 0))
            def _():
                for cp in _gather_copies(srcs, dsts, nf, *host_sem_refs)[0]:
                    cp.start()

        lane = _iota((TM, TM), 1)
        row = _iota((TM, TM), 0)
        hmask = _head_masks(lw)
        upper = (row > lane).astype(_MXU)
        strict = lane < row
        q = q_ref[...] * (1.0 / math.sqrt(HEAD))
        qh = tuple(jnp.where(m, q, 0.0).astype(_MXU) for m in hmask)

        def key_set(first, n, last_mask, nrows):
            off = pl.multiple_of(first * TM, TM)
            kwin = k_ref[pl.ds(off, n * TM), :].astype(_MXU)
            vwin = v_ref[pl.ds(off, n * TM), :].astype(_MXU)
            for hh in range(nh):
                run = run_scr[hh, 0:nrows, 0:1]
                _, _, ls, lk_full = _sb_logs(qh[hh][0:nrows], kwin)
                w, run = _sb_weights(ls, lk_full, run, last_mask[0:nrows], upper, n)
                pieces = [_split(wb, 2) for wb in w]
                stacked = jnp.concatenate(
                    [jnp.concatenate([p[0] for p in pieces], axis=1), jnp.concatenate([p[1] for p in pieces], axis=1)], axis=0)
                res = jnp.dot(stacked, vwin, preferred_element_type=F32)
                acc[hh, 0:nrows] += res[0:nrows]
                acc[nh + hh, 0:nrows] += res[nrows:2 * nrows]
                run_scr[hh, 0:nrows] = jnp.broadcast_to(run, (nrows, TM))

        acc[...] = jnp.zeros_like(acc)
        run_scr[...] = jnp.zeros_like(run_scr)
        _sb_walk(i, key_set, strict, run_scr)
        o_ref[...] = _by_head(hmask, [acc[h] for h in range(nh)])
        olo_ref[...] = _by_head(hmask, [acc[nh + h] for h in range(nh)])

        if nf:
            @pl.when(jnp.logical_and(p == npair - 1, i == nb - 1))
            def _():
                _gather_finish(*_gather_copies(srcs, dsts, nf, *host_sem_refs))

    blk = pl.BlockSpec((TM, lw), lambda p, i: (i, p))
    res = pl.pallas_call(
        body, name="sb_fwd", grid=(npair, nb),
        in_specs=[blk,
                  pl.BlockSpec((lp, lw), lambda p, i: (0, npair + p)),
                  pl.BlockSpec((lp, lw), lambda p, i: (0, 2 * npair + p))] + host_in,
        out_specs=[blk, blk] + host_out,
        out_shape=[_SDS((lp, SB_W), F32), _SDS((lp, SB_W), F32)] + host_shapes,
        scratch_shapes=[pltpu.VMEM((2 * nh, TM, lw), F32), pltpu.VMEM((nh, TM, TM), F32)] + host_sems,
        compiler_params=_params(("arbitrary", "arbitrary")),
    )(qkv, qkv, qkv, *fetch)
    return res[0], res[1], _own_slot(res[2:], fetch)


def _sb_bwd(qkv, o_sb, o_lo, d_o, d_proj, send=(), dests=()):
    lp = qkv.shape[0]
    nb = lp // TM
    lw = SB_LANES_BWD
    nh = lw // HEAD
    npair = SB_W // lw
    scale = 1.0 / math.sqrt(HEAD)
    n = len(send)
    host_in, host_out, host_shapes, host_sems = _host_specs(send)

    def body(q_ref, k_ref, v_ref, o_ref, olo_ref, do_ref, dproj_in, *rest):
        srcs, dproj_ref, dsts = rest[:n], rest[n], rest[n + 1:2 * n + 1]
        dq_all, dk_ref, dv_ref, stage, sems, dq_acc, run_scr, gsum_scr = rest[2 * n + 1:2 * n + 9]
        host_sem_refs = rest[2 * n + 9:]
        p = pl.program_id(0)
        i = pl.program_id(1)

        if n:
            @pl.when(jnp.logical_and(p == 0, i == 0))
            def _():
                _start_copies(_slab_copies(srcs, dsts, dests, *host_sem_refs))

        @pl.when(i == 0)
        def _():
            dk_ref[...] = jnp.zeros_like(dk_ref)
            dv_ref[...] = jnp.zeros_like(dv_ref)

        lane = _iota((TM, TM), 1)
        row = _iota((TM, TM), 0)
        hmask = _head_masks(lw)
        upper = (row > lane).astype(_MXU)
        lower_incl = (row >= lane).astype(_MXU)
        strict = lane < row
        q = q_ref[...] * scale
        do = do_ref[...]
        prod = do.astype(_MXU).astype(F32) * (o_ref[...] + olo_ref[...])
        qh = tuple(jnp.where(m, q, 0.0).astype(_MXU) for m in hmask)
        doh = tuple(jnp.where(m, do, 0.0).astype(_MXU) for m in hmask)
        gtot = tuple(jnp.sum(jnp.where(m, prod, 0.0), axis=1, keepdims=True) for m in hmask)

        def key_set(first, n, last_mask, nrows):
            off = pl.multiple_of(first * TM, TM)
            kf = k_ref[pl.ds(off, n * TM), :]
            kwin = kf.astype(_MXU)
            vwin = v_ref[pl.ds(off, n * TM), :].astype(_MXU)
            last_mask = last_mask[0:nrows]
            dk_win = None
            for hh in range(nh):
                run = run_scr[hh, 0:nrows, 0:1]
                gsum = gsum_scr[hh, 0:nrows, 0:1]
                z, e, ls, lk_full = _sb_logs(qh[hh][0:nrows], kwin)
                w, run = _sb_weights(ls, lk_full, run, last_mask, upper, n)
                r = 1.0 / (1.0 + e)
                er = e * r
                pos = z >= 0.0
                beta = jnp.where(pos, r, er)
                one_m_beta = jnp.where(pos, er, r)
                dw = lax.dot_general(doh[hh][0:nrows], vwin, _NT, preferred_element_type=F32)
                g = [_blk(dw, b) * w[b] for b in range(n)]
                suffix = _stacked_sel(g, lower_incl, 2)
                dz = [None] * n
                for b in range(n - 1, -1, -1):
                    prefix = gtot[hh][0:nrows] - gsum - suffix[b]
                    d = g[b] * _blk(one_m_beta, b) - _blk(beta, b) * prefix
                    dz[b] = (jnp.where(last_mask, d, 0.0) if b == n - 1 else d).astype(_MXU)
                    gsum = gsum + jnp.sum(g[b], axis=1, keepdims=True)
                dzw = jnp.concatenate(dz, axis=1)
                ww = jnp.concatenate([wb.astype(_MXU) for wb in w], axis=1)
                kh = jnp.where(hmask[hh][0:1, :], kf, 0.0).astype(_MXU)
                dq_acc[0:nrows] += jnp.dot(dzw, kh, preferred_element_type=F32)
                dk_h = lax.dot_general(dzw, qh[hh][0:nrows], _TN, preferred_element_type=F32)
                dv_h = lax.dot_general(ww, doh[hh][0:nrows], _TN, preferred_element_type=F32)
                dk_win = (dk_h, dv_h) if dk_win is None else (dk_win[0] + dk_h, dk_win[1] + dv_h)
                run_scr[hh, 0:nrows] = jnp.broadcast_to(run, (nrows, TM))
                gsum_scr[hh, 0:nrows] = jnp.broadcast_to(gsum, (nrows, TM))
            dk_ref[pl.ds(off, n * TM), :] += dk_win[0]
            dv_ref[pl.ds(off, n * TM), :] += dk_win[1]

        dq_acc[...] = jnp.zeros_like(dq_acc)
        run_scr[...] = jnp.zeros_like(run_scr)
        gsum_scr[...] = jnp.zeros_like(gsum_scr)

        _sb_walk(i, key_set, strict, run_scr)
        dq_all[pl.ds(pl.multiple_of(i * TM, TM), TM), :] = dq_acc[...] * scale

        @pl.when(i == nb - 1)
        def _():
            copies = []
            for s, src in enumerate((dq_all, dk_ref, dv_ref)):
                stage[s] = src[...].astype(_MXU)
                col = pl.multiple_of((s * npair + p) * lw, lw)
                copies.append(pltpu.make_async_copy(stage.at[s], dproj_ref.at[:, pl.ds(col, lw)], sems.at[s]))
                copies[-1].start()
            for cp in copies:
                cp.wait()

        if n:
            @pl.when(jnp.logical_and(p == npair - 1, i == nb - 1))
            def _():
                _finish_copies(_slab_copies(srcs, dsts, dests, *host_sem_refs))

    blk = pl.BlockSpec((TM, lw), lambda p, i: (i, p))
    res = pl.pallas_call(
        body, name="sb_bwd", grid=(npair, nb),
        in_specs=[blk,
                  pl.BlockSpec((lp, lw), lambda p, i: (0, npair + p)),
                  pl.BlockSpec((lp, lw), lambda p, i: (0, 2 * npair + p)),
                  blk, blk, blk, pl.BlockSpec(memory_space=pl.ANY)] + host_in,
        out_specs=[pl.BlockSpec(memory_space=pl.ANY)] + host_out,
        out_shape=[_SDS(d_proj.shape, d_proj.dtype)] + host_shapes,
        input_output_aliases={6: 0},
        scratch_shapes=[pltpu.VMEM((lp, lw), F32), pltpu.VMEM((lp, lw), F32), pltpu.VMEM((lp, lw), F32),
                        pltpu.VMEM((3, lp, lw), _MXU), pltpu.SemaphoreType.DMA((3,)),
                        pltpu.VMEM((TM, lw), F32), pltpu.VMEM((nh, TM, TM), F32),
                        pltpu.VMEM((nh, TM, TM), F32)] + host_sems,
        compiler_params=_params(("arbitrary", "arbitrary")),
    )(qkv, qkv, qkv, o_sb, o_lo, d_o, d_proj, *send)
    return res[0], list(res[1:])


def _conv_pre(x_ref, w_ref, b_ref, lp):
    n = lp - 8
    w = w_ref[...]
    pre = (x_ref[pl.ds(5, n), :] * w[0:1, :] + x_ref[pl.ds(6, n), :] * w[1:2, :]
           + x_ref[pl.ds(7, n), :] * w[2:3, :] + x_ref[pl.ds(8, n), :] * w[3:4, :]) + b_ref[...]
    live = (_iota((n, 128), 0) + 8) >= PAD
    return pre, live


def _conv_fwd(proj, dt_raw, conv_w, conv_b, dt_bias128):
    lp = proj.shape[0]
    nblk = XBC_W // 128
    c0 = (COL_XBC - QKV_W) // 128

    def body(x_ref, w_ref, b_ref, dtr_ref, dtb_ref, o_ref, dt_ref):
        pre, live = _conv_pre(x_ref, w_ref, b_ref, lp)
        act = pre * _sigmoid(pre)
        o_ref[pl.ds(0, 8), :] = jnp.zeros((8, 128), F32)
        o_ref[pl.ds(8, lp - 8), :] = jnp.where(live, act, 0.0)

        @pl.when(pl.program_id(0) == 0)
        def _():
            s = dtr_ref[...] + dtb_ref[...]
            sp = jnp.maximum(s, 0.0) + jnp.log(1.0 + jnp.exp(-jnp.abs(s)))
            dt_ref[...] = jnp.where(_iota((lp, 128), 0) >= PAD, sp, 0.0)

    return pl.pallas_call(
        body, name="conv_fwd", grid=(nblk,),
        in_specs=[pl.BlockSpec((lp, 128), lambda j: (0, c0 + j)),
                  pl.BlockSpec((4, 128), lambda j: (0, j)),
                  pl.BlockSpec((1, 128), lambda j: (0, j)),
                  pl.BlockSpec((lp, 128), lambda j: (0, 0)),
                  pl.BlockSpec((1, 128), lambda j: (0, 0))],
        out_specs=[pl.BlockSpec((lp, 128), lambda j: (0, j)),
                   pl.BlockSpec((lp, 128), lambda j: (0, 0))],
        out_shape=[_SDS((lp, XBC_W), F32), _SDS((lp, 128), F32)],
        compiler_params=_params(("arbitrary",)),
    )(proj, conv_w, conv_b, dt_raw, dt_bias128)


def _conv_bwd(proj, dt_raw, conv_w, conv_b, dt_bias128, d_xbc, d_dt128, d_proj):
    lp = proj.shape[0]
    nblk = XBC_W // 128
    c0 = COL_XBC // 128
    c0_in = (COL_XBC - QKV_W) // 128
    n = lp - 8
    last = nblk - 1

    def body(x_ref, w_ref, b_ref, dtr_ref, dtb_ref, dy_ref, ddt_ref, dproj_in,
             dx_ref, gw_ref, gb_ref, gdtb_ref, scr):
        j = pl.program_id(0)

        @pl.when(j < nblk)
        def _():
            pre, live = _conv_pre(x_ref, w_ref, b_ref, lp)
            sg = _sigmoid(pre)
            dpre = jnp.where(live, dy_ref[pl.ds(8, n), :] * (sg * (1.0 + pre * (1.0 - sg))), 0.0)
            gb_ref[...] = jnp.sum(dpre, axis=0, keepdims=True)
            gw_ref[...] = jnp.concatenate(
                [jnp.sum(dpre * x_ref[pl.ds(5 + k, n), :], axis=0, keepdims=True) for k in range(4)], axis=0)
            scr[pl.ds(0, 8), :] = jnp.zeros((8, 128), F32)
            scr[pl.ds(8, n), :] = dpre
            scr[pl.ds(lp, 8), :] = jnp.zeros((8, 128), F32)
            w = w_ref[...]
            dx_ref[...] = (scr[pl.ds(0, lp), :] * w[3:4, :] + scr[pl.ds(1, lp), :] * w[2:3, :]
                           + scr[pl.ds(2, lp), :] * w[1:2, :] + scr[pl.ds(3, lp), :] * w[0:1, :]).astype(dx_ref.dtype)

        @pl.when(j == nblk)
        def _():
            s = dtr_ref[...] + dtb_ref[...]
            d = jnp.where(_iota((lp, 128), 0) >= PAD, ddt_ref[...] * _sigmoid(s), 0.0)
            dx_ref[...] = d.astype(dx_ref.dtype)
            gdtb_ref[...] = jnp.sum(d, axis=0, keepdims=True)

    clamp = lambda j: (0, jnp.minimum(j, last))
    full128 = pl.BlockSpec((lp, 128), lambda j: (0, 0))
    return pl.pallas_call(
        body, name="conv_bwd", grid=(nblk + 1,),
        in_specs=[pl.BlockSpec((lp, 128), lambda j: (0, c0_in + jnp.minimum(j, last))),
                  pl.BlockSpec((4, 128), clamp),
                  pl.BlockSpec((1, 128), clamp),
                  full128, pl.BlockSpec((1, 128), lambda j: (0, 0)),
                  pl.BlockSpec((lp, 128), clamp), full128, pl.BlockSpec(memory_space=pl.ANY)],
        out_specs=[pl.BlockSpec((lp, 128), lambda j: (0, c0 + j)), pl.BlockSpec((4, 128), clamp),
                   pl.BlockSpec((1, 128), clamp), pl.BlockSpec((1, 128), lambda j: (0, 0))],
        out_shape=[_SDS(d_proj.shape, d_proj.dtype), _SDS((4, XBC_W), F32), _SDS((1, XBC_W), F32), _SDS((1, 128), F32)],
        input_output_aliases={7: 0},
        scratch_shapes=[pltpu.VMEM((lp + 8, 128), F32)],
        compiler_params=_params(("arbitrary",)),
    )(proj, conv_w, conv_b, dt_raw, dt_bias128, d_xbc, d_dt128, d_proj)


def _ssd_pieces(dt, dt_t, a, a_t):
    r64 = _iota((CHUNK, CHUNK), 0)
    c64 = _iota((CHUNK, CHUNK), 1)
    tril = c64 <= r64
    tril01 = tril.astype(_MXU)
    triu01 = (r64 <= c64).astype(_MXU)
    expand = (lax.shift_right_logical(_iota((N_HEADS, SSD_W), 1), 6) == _iota((N_HEADS, SSD_W), 0)).astype(_MXU)
    acum = _sel_left(tril01, dt * a)
    acum_t = _sel_right(dt_t * a_t, triu01)
    ax = _sel_right(acum, expand)
    dtx = _sel_right(dt, expand)
    return tril, expand, acum, acum_t, ax, dtx


def _seg_matrix():
    return (lax.shift_right_logical(_iota((SSD_W, N_HEADS), 0), 6) == _iota((SSD_W, N_HEADS), 1)).astype(_MXU)


def _head_decay(ax, acum_t, h, tril):
    col = ax[:, h * HEAD:(h + 1) * HEAD]
    rowv = acum_t[h:h + 1, :]
    return jnp.where(tril, jnp.exp(jnp.minimum(col - rowv, 0.0)), 0.0)


def _ssd_fwd(xbc, dt_c, dt_tc, a, a_t, dskip_x):
    lp = xbc.shape[0]
    nc = lp // CHUNK
    gw = SSD_W // N_GROUPS
    hpg = N_HEADS // N_GROUPS

    def body(x_ref, dt_ref, dtt_ref, a_ref, at_ref, d_ref, y_ref, st_ref, state):
        c = pl.program_id(0)

        @pl.when(c == 0)
        def _():
            state[...] = jnp.zeros_like(state)

        st_ref[0] = state[...]
        tril, _, _, acum_t, ax, dtx = _ssd_pieces(dt_ref[0], dtt_ref[0], a_ref[...], at_ref[...])
        x = x_ref[:, 0:SSD_W]
        xdt = x * dtx
        ea = jnp.exp(ax)
        aex = ax[CHUNK - 1:CHUNK, :]
        wd = jnp.exp(aex - ax)
        eae = jnp.exp(aex)
        xw = xdt * wd
        y_ref[...] = x * d_ref[...]
        for g in range(N_GROUPS):
            gs = slice(g * gw, (g + 1) * gw)
            rs = slice(g * N_STATE, (g + 1) * N_STATE)
            bg = x_ref[:, SSD_W + g * N_STATE:SSD_W + (g + 1) * N_STATE]
            cg = x_ref[:, SSD_W + N_GROUPS * N_STATE + g * N_STATE:SSD_W + N_GROUPS * N_STATE + (g + 1) * N_STATE]
            sg = state[rs, :]
            cb = _mm_nt(cg, bg)
            y_ref[:, gs] += _mm(cg, sg) * ea[:, gs]
            for r in range(hpg):
                h = g * hpg + r
                hs = slice(h * HEAD, (h + 1) * HEAD)
                m = cb * _head_decay(ax, acum_t, h, tril)
                y_ref[:, hs] += _mm(m, xdt[:, hs])
            state[rs, :] = sg * eae[:, gs] + _mm_tn(bg, xw[:, gs])

    return pl.pallas_call(
        body, name="ssd_fwd", grid=(nc,),
        in_specs=[pl.BlockSpec((CHUNK, XBC_W), lambda c: (c, 0)),
                  pl.BlockSpec((1, CHUNK, N_HEADS), lambda c: (c, 0, 0)),
                  pl.BlockSpec((1, N_HEADS, CHUNK), lambda c: (c, 0, 0)),
                  pl.BlockSpec((1, N_HEADS), lambda c: (0, 0)),
                  pl.BlockSpec((N_HEADS, 1), lambda c: (0, 0)),
                  pl.BlockSpec((1, SSD_W), lambda c: (0, 0))],
        out_specs=[pl.BlockSpec((CHUNK, SSD_W), lambda c: (c, 0)),
                   pl.BlockSpec((1, N_GROUPS * N_STATE, gw), lambda c: (c, 0, 0))],
        out_shape=[_SDS((lp, SSD_W), F32), _SDS((nc, N_GROUPS * N_STATE, gw), F32)],
        scratch_shapes=[pltpu.VMEM((N_GROUPS * N_STATE, gw), F32)],
        compiler_params=_params(("arbitrary",)),
    )(xbc, dt_c, dt_tc, a, a_t, dskip_x)


def _ssd_bwd(xbc, dt_c, dt_tc, a, a_t, dskip_x, states, d_y):
    lp = xbc.shape[0]
    nc = lp // CHUNK
    gw = SSD_W // N_GROUPS
    hpg = N_HEADS // N_GROUPS

    def body(x_ref, dt_ref, dtt_ref, a_ref, at_ref, d_ref, st_ref, dy_ref,
             dx_ref, ddta_ref, ddtb_ref, ga1_ref, ga2_ref, gd_ref, dstate, dxdt_scr, z_scr, yoff_scr, sds_scr):
        c = pl.program_id(0)

        @pl.when(c == 0)
        def _():
            dstate[...] = jnp.zeros_like(dstate)
            ga1_ref[...] = jnp.zeros_like(ga1_ref)
            ga2_ref[...] = jnp.zeros_like(ga2_ref)
            gd_ref[...] = jnp.zeros_like(gd_ref)

        dt = dt_ref[0]
        dt_t = dtt_ref[0]
        a = a_ref[...]
        a_t = at_ref[...]
        tril, _, acum, acum_t, ax, dtx = _ssd_pieces(dt, dt_t, a, a_t)
        seg = _seg_matrix()
        x = x_ref[:, 0:SSD_W]
        dy = dy_ref[...]
        xdt = x * dtx
        ea = jnp.exp(ax)
        aex = ax[CHUNK - 1:CHUNK, :]
        wd = jnp.exp(aex - ax)
        eae = jnp.exp(aex)
        xw = xdt * wd
        edy = ea * dy
        lane16 = _iota((CHUNK, N_HEADS), 1)
        row16 = _iota((N_HEADS, CHUNK), 0)
        da_col = jnp.zeros((CHUNK, N_HEADS), F32)
        da_row = jnp.zeros((N_HEADS, CHUNK), F32)
        for g in range(N_GROUPS):
            gs = slice(g * gw, (g + 1) * gw)
            rs = slice(g * N_STATE, (g + 1) * N_STATE)
            bcol = slice(SSD_W + g * N_STATE, SSD_W + (g + 1) * N_STATE)
            ccol = slice(SSD_W + N_GROUPS * N_STATE + g * N_STATE, SSD_W + N_GROUPS * N_STATE + (g + 1) * N_STATE)
            bg = x_ref[:, bcol]
            cg = x_ref[:, ccol]
            sg = st_ref[0, rs, :]
            dsn = dstate[rs, :]
            cb = _mm_nt(cg, bg)
            z_scr[:, gs] = _mm(bg, dsn)
            yoff_scr[:, gs] = _mm(cg, sg) * ea[:, gs]
            sds_scr[:, gs] = jnp.broadcast_to(jnp.sum(dsn * sg, axis=0, keepdims=True), (8, gw))
            dcb = jnp.zeros((CHUNK, CHUNK), F32)
            for r in range(hpg):
                h = g * hpg + r
                hs = slice(h * HEAD, (h + 1) * HEAD)
                dec = _head_decay(ax, acum_t, h, tril)
                m = cb * dec
                t1 = _mm_nt(dy[:, hs], xdt[:, hs])
                dcb = dcb + dec * t1
                tm = m * t1
                da_col = da_col + jnp.where(lane16 == h, jnp.sum(tm, axis=1, keepdims=True), 0.0)
                da_row = da_row - jnp.where(row16 == h, jnp.sum(tm, axis=0, keepdims=True), 0.0)
                dxdt_scr[:, hs] = _mm_tn(m, dy[:, hs])
            dx_ref[:, ccol] = _mm(dcb, bg) + _mm_nt(edy[:, gs], sg)
            dx_ref[:, bcol] = _mm_tn(dcb, cg) + _mm_nt(xw[:, gs], dsn)
            dstate[rs, :] = eae[:, gs] * dsn + _mm_tn(cg, edy[:, gs])
        zf = z_scr[...]
        dxdt = dxdt_scr[...] + wd * zf
        t3 = _sel_right(xw * zf, seg)
        da_col = da_col + _sel_right(dy * yoff_scr[...], seg) - t3
        aend = acum[CHUNK - 1:CHUNK, :]
        sd = _sel_right(sds_scr[...], seg)[0:1, :] * jnp.exp(aend)
        last = jnp.sum(t3, axis=0, keepdims=True) + sd
        da_col = da_col + jnp.where(_iota((CHUNK, N_HEADS), 0) == CHUNK - 1, last, 0.0)
        r64 = _iota((CHUNK, CHUNK), 0)
        c64 = _iota((CHUNK, CHUNK), 1)
        ddta1 = _sel_left((c64 >= r64).astype(_MXU), da_col)
        ddta2 = _sel_right(da_row, (r64 >= c64).astype(_MXU))
        ddta_ref[0] = a * ddta1 + _sel_right(dxdt * x, seg)
        ddtb_ref[0] = a_t * ddta2
        ga1_ref[...] += jnp.sum(dt * ddta1, axis=0, keepdims=True)
        ga2_ref[...] += jnp.sum(dt_t * ddta2, axis=1, keepdims=True)
        dx_ref[:, 0:SSD_W] = dxdt * dtx + d_ref[...] * dy
        gd_ref[...] += jnp.sum(dy * x, axis=0, keepdims=True)

    rev = lambda c: (nc - 1 - c, 0)
    rev3 = lambda c: (nc - 1 - c, 0, 0)
    return pl.pallas_call(
        body, name="ssd_bwd", grid=(nc,),
        in_specs=[pl.BlockSpec((CHUNK, XBC_W), rev),
                  pl.BlockSpec((1, CHUNK, N_HEADS), rev3),
                  pl.BlockSpec((1, N_HEADS, CHUNK), rev3),
                  pl.BlockSpec((1, N_HEADS), lambda c: (0, 0)),
                  pl.BlockSpec((N_HEADS, 1), lambda c: (0, 0)),
                  pl.BlockSpec((1, SSD_W), lambda c: (0, 0)),
                  pl.BlockSpec((1, N_GROUPS * N_STATE, gw), rev3),
                  pl.BlockSpec((CHUNK, SSD_W), rev)],
        out_specs=[pl.BlockSpec((CHUNK, XBC_W), rev),
                   pl.BlockSpec((1, CHUNK, N_HEADS), rev3),
                   pl.BlockSpec((1, N_HEADS, CHUNK), rev3),
                   pl.BlockSpec((1, N_HEADS), lambda c: (0, 0)),
                   pl.BlockSpec((N_HEADS, 1), lambda c: (0, 0)),
                   pl.BlockSpec((1, SSD_W), lambda c: (0, 0))],
        out_shape=[_SDS((lp, XBC_W), F32), _SDS((nc, CHUNK, N_HEADS), F32), _SDS((nc, N_HEADS, CHUNK), F32),
                   _SDS((1, N_HEADS), F32), _SDS((N_HEADS, 1), F32), _SDS((1, SSD_W), F32)],
        scratch_shapes=[pltpu.VMEM((N_GROUPS * N_STATE, gw), F32), pltpu.VMEM((CHUNK, SSD_W), F32),
                        pltpu.VMEM((CHUNK, SSD_W), F32), pltpu.VMEM((CHUNK, SSD_W), F32),
                        pltpu.VMEM((8, SSD_W), F32)],
        compiler_params=_params(("arbitrary",)),
    )(xbc, dt_c, dt_tc, a, a_t, dskip_x, states, d_y)


def _gated_norm(o, gate, w):
    sg = _sigmoid(gate)
    p = o * (gate * sg)
    rs = lax.rsqrt(jnp.mean(p * p, axis=-1, keepdims=True) + EPS)
    n = p * rs
    return sg, rs, n, n * w


def _tail_fwd(o_sb, o_ssd, proj, h0, target, w_out, sb_w, ssd_w, fin_w):
    lp = o_sb.shape[0]
    nb = lp // TM
    row = lambda i: (i, 0)
    one = lambda i: (0, 0)

    def body(osb_ref, gate_ref, ossd_ref, z_ref, h0_ref, tgt_ref, wo_ref, sbw_ref, ssdw_ref, fw_ref,
             dh1_ref, loss_ref, gfw_ref):
        i = pl.program_id(0)

        @pl.when(i == 0)
        def _():
            loss_ref[...] = jnp.zeros_like(loss_ref)
            gfw_ref[...] = jnp.zeros_like(gfw_ref)

        y1 = _gated_norm(osb_ref[...], gate_ref[...], sbw_ref[...])[3]
        y2 = _gated_norm(ossd_ref[...], z_ref[...], ssdw_ref[...])[3]
        h1 = (h0_ref[...] + _mm(y1, wo_ref[0:SB_W, :])) + _mm(y2, wo_ref[SB_W:SB_W + SSD_W, :])
        rs1 = lax.rsqrt(jnp.mean(h1 * h1, axis=-1, keepdims=True) + EPS)
        n1 = h1 * rs1
        fw = fw_ref[...]
        diff = jnp.where(i > 0, n1 * fw - tgt_ref[...], 0.0)
        loss_ref[...] += jnp.sum(diff * diff, axis=0, keepdims=True)
        d_out = diff * (1.0 / D_MODEL)
        gfw_ref[...] += jnp.sum(d_out * n1, axis=0, keepdims=True)
        g = d_out * fw
        dh1_ref[...] = rs1 * (g - n1 * jnp.mean(g * n1, axis=-1, keepdims=True))

    return pl.pallas_call(
        body, name="tail_fwd", grid=(nb,),
        in_specs=[pl.BlockSpec((TM, SB_W), row),
                  pl.BlockSpec((TM, SB_W), lambda i: (i, (COL_GATE - QKV_W) // SB_W)),
                  pl.BlockSpec((TM, SSD_W), row),
                  pl.BlockSpec((TM, SSD_W), lambda i: (i, (COL_Z - QKV_W) // SSD_W)),
                  pl.BlockSpec((TM, D_MODEL), row),
                  pl.BlockSpec((TM, D_MODEL), lambda i: (jnp.maximum(i - 1, 0), 0)),
                  pl.BlockSpec(memory_space=_VMEM),
                  pl.BlockSpec((1, SB_W), one), pl.BlockSpec((1, SSD_W), one), pl.BlockSpec((1, D_MODEL), one)],
        out_specs=[pl.BlockSpec((TM, D_MODEL), row), pl.BlockSpec((1, D_MODEL), one), pl.BlockSpec((1, D_MODEL), one)],
        out_shape=[_SDS((lp, D_MODEL), F32), _SDS((1, D_MODEL), F32), _SDS((1, D_MODEL), F32)],
        compiler_params=_params(("arbitrary",), 40),
    )(o_sb, proj, o_ssd, proj, h0, target, w_out, sb_w, ssd_w, fin_w)


def _gated_norm_bwd(o, gate, w, dy):
    sg, rs, n, _ = _gated_norm(o, gate, w)
    gw = jnp.sum(dy * n, axis=0, keepdims=True)
    dn = dy * w
    dp = rs * (dn - n * jnp.mean(dn * n, axis=-1, keepdims=True))
    d_o = dp * (gate * sg)
    d_gate = dp * o * (sg * (1.0 + gate * (1.0 - sg)))
    return d_o, d_gate, gw, n * w


def _tail_bwd(o_sb, o_ssd, proj, d_h1, w_out, sb_w, ssd_w):
    lp = o_sb.shape[0]
    tm = 272 if lp % 272 == 0 else TM
    nb = lp // tm
    row = lambda i, t: (i, 0)
    one = lambda i, t: (0, 0)

    def body(osb_ref, gate_ref, ossd_ref, z_ref, dh1_ref, wo_ref, sbw_ref, ssdw_ref,
             dosb_ref, dossd_ref, dproj_ref, gwo_ref, gsb_ref, gssd_ref):
        i = pl.program_id(0)
        t = pl.program_id(1)

        @pl.when(jnp.logical_and(i == 0, t == 0))
        def _():
            gwo_ref[...] = jnp.zeros_like(gwo_ref)
            gsb_ref[...] = jnp.zeros_like(gsb_ref)
            gssd_ref[...] = jnp.zeros_like(gssd_ref)

        dh1 = dh1_ref[...].astype(_MXU)

        def half(o_ref, g_ref, w_ref, do_ref, gn_ref, r0):
            dy = lax.dot_general(dh1, wo_ref[r0:r0 + SB_W, :], _NT, preferred_element_type=F32)
            d_o, d_g, gw, y = _gated_norm_bwd(o_ref[...], g_ref[...], w_ref[...], dy)
            do_ref[...] = d_o
            dproj_ref[...] = d_g.astype(_MXU)
            gn_ref[...] += gw
            gwo_ref[r0:r0 + SB_W, :] += lax.dot_general(y.astype(_MXU), dh1, _TN, preferred_element_type=F32)

        @pl.when(t == 0)
        def _():
            half(osb_ref, gate_ref, sbw_ref, dosb_ref, gsb_ref, 0)

        @pl.when(t == 1)
        def _():
            half(ossd_ref, z_ref, ssdw_ref, dossd_ref, gssd_ref, SB_W)

    tile = pl.BlockSpec((tm, SB_W), row)
    return pl.pallas_call(
        body, name="tail_bwd", grid=(nb, 2),
        in_specs=[tile, pl.BlockSpec((tm, SB_W), lambda i, t: (i, (COL_GATE - QKV_W) // SB_W)),
                  tile, pl.BlockSpec((tm, SSD_W), lambda i, t: (i, (COL_Z - QKV_W) // SSD_W)),
                  tile, pl.BlockSpec(memory_space=_VMEM),
                  pl.BlockSpec((1, SB_W), one), pl.BlockSpec((1, SSD_W), one)],
        out_specs=[tile, tile, pl.BlockSpec((tm, SB_W), lambda i, t: (i, COL_GATE // SB_W + t)),
                   pl.BlockSpec((SB_W + SSD_W, D_MODEL), one), pl.BlockSpec((1, SB_W), one), pl.BlockSpec((1, SSD_W), one)],
        out_shape=[_SDS((lp, SB_W), F32), _SDS((lp, SSD_W), F32), _SDS((lp, W_ALL), _MXU),
                   _SDS((SB_W + SSD_W, D_MODEL), F32), _SDS((1, SB_W), F32), _SDS((1, SSD_W), F32)],
        compiler_params=_params(("arbitrary", "arbitrary"), 48),
    )(o_sb, proj, o_ssd, proj, d_h1, w_out, sb_w, ssd_w)


def _d_u_norm_bwd(d_proj, w_t, h0, d_h1, norm_w, send=(), dests=()):
    lp = d_proj.shape[0]
    nb = lp // TM
    seq = lp - OFF
    tk = 512
    ksteps = N_MAIN // tk
    n = len(send)
    host_in, host_out, host_shapes, host_sems = _host_specs(send)

    def body(dp_ref, w_ref, dpdt_ref, wdt_ref, h0_ref, dh1_ref, nw_ref, *rest):
        srcs, (gx_ref, gmeta_ref, gnw_ref), dsts = rest[:n], rest[n:n + 3], rest[n + 3:2 * n + 3]
        acc, sems = rest[2 * n + 3], rest[2 * n + 4:]
        j = pl.program_id(0)
        i = j - ksteps

        @pl.when(j == 0)
        def _():
            if n:
                _start_copies(_slab_copies(srcs, dsts, dests, *sems))
            acc[...] = jnp.dot(dpdt_ref[...], wdt_ref[...], preferred_element_type=F32)
            gnw_ref[...] = jnp.zeros_like(gnw_ref)

        @pl.when(j < ksteps)
        def _():
            acc[...] += jnp.dot(dp_ref[...], w_ref[...], preferred_element_type=F32)

        @pl.when(i >= 0)
        def _():
            du = acc[pl.ds(pl.multiple_of(i * TM, TM), TM), :]
            h = h0_ref[...]
            rs = lax.rsqrt(jnp.mean(h * h, axis=-1, keepdims=True) + EPS)
            n0 = h * rs
            gnw_ref[...] += jnp.sum(du * n0, axis=0, keepdims=True)
            g = du * nw_ref[...]
            dh0 = dh1_ref[...] + rs * (g - n0 * jnp.mean(g * n0, axis=-1, keepdims=True))

            @pl.when(i == 0)
            def _():
                gmeta_ref[...] = dh0[PAD:PAD + N_META, :]

            @pl.when(i > 0)
            def _():
                gx_ref[...] = dh0

        if n:
            @pl.when(j == ksteps + nb - 1)
            def _():
                _finish_copies(_slab_copies(srcs, dsts, dests, *sems))

    last_k = ksteps - 1
    tile = pl.BlockSpec((TM, D_MODEL), lambda j: (jnp.maximum(j - ksteps, 0), 0))
    one = lambda j: (0, 0)
    res = pl.pallas_call(
        body, name="d_u_norm_bwd", grid=(ksteps + nb,),
        in_specs=[pl.BlockSpec((lp, tk), lambda j: (0, jnp.minimum(j, last_k))),
                  pl.BlockSpec((tk, D_MODEL), lambda j: (jnp.minimum(j, last_k), 0)),
                  pl.BlockSpec((lp, 128), lambda j: (0, N_MAIN // 128)),
                  pl.BlockSpec((128, D_MODEL), lambda j: (N_MAIN // 128, 0)),
                  tile, tile, pl.BlockSpec((1, D_MODEL), one)] + host_in,
        out_specs=[pl.BlockSpec((TM, D_MODEL), lambda j: (jnp.maximum(j - ksteps - 1, 0), 0)),
                   pl.BlockSpec((N_META, D_MODEL), one), pl.BlockSpec((1, D_MODEL), one)] + host_out,
        out_shape=[_SDS((seq, D_MODEL), F32), _SDS((N_META, D_MODEL), F32), _SDS((1, D_MODEL), F32)] + host_shapes,
        scratch_shapes=[pltpu.VMEM((lp, D_MODEL), F32)] + host_sems,
        compiler_params=_params(("arbitrary",), 48),
    )(d_proj, w_t, d_proj, w_t, h0, d_h1, norm_w, *send)
    return res[0], res[1], res[2], list(res[3:])


def _grad_w_windows(u_t, d_proj, first, count, name):
    lp = d_proj.shape[0]
    hw = WIN_W // 2
    steps = 2 * count

    def body(ut_ref, dp_hbm, o_ref, buf, sems):
        s = pl.program_id(0)
        slot = s % 2

        def fetch(step, sl):
            start = pl.multiple_of((first + step // 2) * WIN_STEP + (step % 2) * hw, 128)
            return pltpu.make_async_copy(dp_hbm.at[:, pl.ds(start, hw)], buf.at[sl], sems.at[sl])

        @pl.when(s == 0)
        def _():
            fetch(0, 0).start()

        @pl.when(s + 1 < steps)
        def _():
            fetch(s + 1, 1 - slot).start()

        fetch(s, slot).wait()
        o_ref[0] = jnp.dot(ut_ref[...], buf[slot], preferred_element_type=F32).astype(o_ref.dtype)

    return pl.pallas_call(
        body, name=name, grid=(steps,),
        in_specs=[pl.BlockSpec((D_MODEL, lp), lambda s: (0, 0)), pl.BlockSpec(memory_space=pl.ANY)],
        out_specs=pl.BlockSpec((1, D_MODEL, hw), lambda s: (s // 2, 0, s % 2)),
        out_shape=_SDS((count, D_MODEL, WIN_W), _MXU),
        scratch_shapes=[pltpu.VMEM((2, lp, hw), _MXU), pltpu.SemaphoreType.DMA((2,))],
        compiler_params=_params(("arbitrary",), 40),
    )(u_t, d_proj)


def _device_grads(x2d, target2d, meta_full, norm_w, w_t, conv_w, conv_b, dt_bias, a_log, d_skip,
                  sb_w, ssd_w, w_out, fin_w, exchange=None, w_out_shard=None):
    lp = x2d.shape[0] + OFF
    nc = lp // CHUNK
    h0, u, u_t = _prep(x2d, meta_full, norm_w)
    qkv, proj, dt_raw = _inproj(u, w_t)
    if w_out is None:
        o_sb, o_lo, (w_out_shards,) = _sb_fwd(qkv, (w_out_shard,))
        w_out = w_out_shards.reshape(2 * D_MODEL, D_MODEL)
    else:
        o_sb, o_lo, _ = _sb_fwd(qkv)
    dt_bias128 = jnp.pad(dt_bias, ((0, 0), (0, 128 - N_HEADS)))
    xbc, dt128 = _conv_fwd(proj, dt_raw, conv_w, conv_b, dt_bias128)
    dt_c = dt128[:, :N_HEADS].reshape(nc, CHUNK, N_HEADS)
    dt_tc = jnp.swapaxes(dt_c, 1, 2)
    a = -jnp.exp(a_log)
    a_t = a.reshape(N_HEADS, 1)
    dskip_x = jnp.repeat(d_skip, HEAD, axis=1)
    o_ssd, states = _ssd_fwd(xbc, dt_c, dt_tc, a, a_t, dskip_x)
    d_h1, sq_err, g_fin = _tail_fwd(o_sb, o_ssd, proj, h0, target2d, w_out, sb_w, ssd_w, fin_w)

    d_osb, d_ossd, d_proj, g_wout, g_sb, g_ssd = _tail_bwd(o_sb, o_ssd, proj, d_h1, w_out, sb_w, ssd_w)
    d_xbc_act, ddt_a, ddt_b, ga1, ga2, gd = _ssd_bwd(xbc, dt_c, dt_tc, a, a_t, dskip_x, states, d_ossd)
    d_dt = (ddt_a + jnp.swapaxes(ddt_b, 1, 2)).reshape(lp, N_HEADS)
    d_dt128 = jnp.pad(d_dt, ((0, 0), (0, 128 - N_HEADS)))
    d_proj, g_convw, g_convb, g_dtb128 = _conv_bwd(proj, dt_raw, conv_w, conv_b, dt_bias128, d_xbc_act, d_dt128, d_proj)
    send_e, dests_e = ((), ()) if exchange is None else exchange["early"](g_wout)
    d_proj, arrived_e = _sb_bwd(qkv, o_sb, o_lo, d_osb, d_proj, send_e, dests_e)
    g_win = _grad_w_windows(u_t, d_proj, 0, N_CHIPS, "grad_w_in")
    send_l, dests_l = ((), ()) if exchange is None else exchange["late"](g_win)
    g_x, g_meta, g_nw, arrived_l = _d_u_norm_bwd(d_proj, w_t, h0, d_h1, norm_w, send_l, dests_l)
    send, arrived = tuple(send_e) + tuple(send_l), tuple(arrived_e) + tuple(arrived_l)
    g_alog = (ga1 + ga2.reshape(1, N_HEADS)) * a
    g_dskip = gd.reshape(N_HEADS, HEAD).sum(axis=1).reshape(1, N_HEADS)
    grads = dict(meta_tokens=g_meta, norm_w=g_nw, w_in=g_win, conv_w=g_convw, conv_b=g_convb,
                 dt_bias=g_dtb128[:, :N_HEADS], a_log=g_alog, d_skip=g_dskip, sb_norm_w=g_sb, ssd_norm_w=g_ssd,
                 w_out=g_wout, final_norm_w=g_fin, sent=send, arrived=arrived)
    return sq_err, g_x, grads


_MESH = pl.DeviceIdType.MESH
_ANY = pl.BlockSpec(memory_space=pl.ANY)


def _place():
    return lax.axis_index("x"), lax.axis_index("y"), lax.axis_index("c")


def _other_chips(x, y):
    return ((1 - x, y), (x, 1 - y), (1 - x, 1 - y))


def _gather_copies(srcs, dsts, n_big, send_sems, recv_sems, fwd_send, fwd_recv):
    x, y, c = _place()
    mine = 2 * x + y
    first, passed = [], []
    for a in range(len(srcs)):
        half = srcs[a].shape[1] // 2
        window = pl.ds(pl.multiple_of(c * half, 128), half)
        for k, (px, py) in enumerate(_other_chips(x, y)):
            if a < n_big:
                src, dst = srcs[a].at[:, window], dsts[a].at[mine, :, window]
                landed = dsts[a].at[2 * px + py, :, window]
                passed.append(pltpu.make_async_remote_copy(
                    src_ref=landed, dst_ref=landed, send_sem=fwd_send.at[a * 3 + k], recv_sem=fwd_recv.at[a * 3 + k],
                    device_id=(x, y, 1 - c), device_id_type=_MESH))
            else:
                src, dst = srcs[a], dsts[a].at[mine]
                passed.append(None)
            first.append(pltpu.make_async_remote_copy(
                src_ref=src, dst_ref=dst, send_sem=send_sems.at[a * 3 + k], recv_sem=recv_sems.at[a * 3 + k],
                device_id=(px, py, c), device_id_type=_MESH))
    return first, passed


def _gather_finish(first, passed):
    for cp, fwd in zip(first, passed):
        cp.wait_recv()
        if fwd is not None:
            fwd.start()
    for fwd in passed:
        if fwd is not None:
            fwd.wait_recv()
    for cp in first + [fwd for fwd in passed if fwd is not None]:
        cp.wait_send()


def _gather_specs(arrays, n_big):
    n = len(arrays)
    hbm = [pl.BlockSpec(memory_space=pl.ANY)] * n
    shapes = [_SDS((N_CHIPS,) + a.shape, a.dtype) for a in arrays]
    sems = [pltpu.SemaphoreType.DMA((3 * n,)), pltpu.SemaphoreType.DMA((3 * n,)),
            pltpu.SemaphoreType.DMA((3 * max(n_big, 1),)), pltpu.SemaphoreType.DMA((3 * max(n_big, 1),))] if n else []
    return hbm, hbm, shapes, sems


def _own_slot(got, arrays):
    if not arrays:
        return []
    mine = 2 * lax.axis_index("x") + lax.axis_index("y")
    return [lax.dynamic_update_slice(g, a[None], (mine,) + (0,) * a.ndim) for g, a in zip(got, arrays)]


def _gather_shards(arrays, n_big):
    n = len(arrays)

    def body(*refs):
        first, passed = _gather_copies(refs[:n], refs[n:2 * n], n_big, *refs[2 * n:])
        for cp in first:
            cp.start()
        _gather_finish(first, passed)

    hbm_in, hbm_out, shapes, sems = _gather_specs(arrays, n_big)
    got = pl.pallas_call(
        body, name="gather_shards", in_specs=hbm_in, out_specs=hbm_out, out_shape=shapes, scratch_shapes=sems,
    )(*arrays)
    return _own_slot(got, arrays)


def _slab_copies(srcs, dsts, dests, send_sems, recv_sems):
    x, y, c = _place()
    mine = 2 * x + y
    copies = []
    for a in range(len(srcs)):
        lo, hi = dests[a]
        receives = jnp.logical_and(mine >= lo, mine < hi)
        for k, (px, py) in enumerate(_other_chips(x, y)):
            target = 2 * px + py
            cp = pltpu.make_async_remote_copy(
                src_ref=srcs[a].at[jnp.clip(target - lo, 0, hi - lo - 1)], dst_ref=dsts[a].at[mine],
                send_sem=send_sems.at[a * 3 + k], recv_sem=recv_sems.at[a * 3 + k],
                device_id=(px, py, c), device_id_type=_MESH)
            copies.append((cp, jnp.logical_and(target >= lo, target < hi), receives))
    return copies


def _start_copies(copies):
    for cp, sends, _ in copies:
        pl.when(sends)(cp.start)


def _finish_copies(copies):
    for cp, _, receives in copies:
        pl.when(receives)(cp.wait_recv)
    for cp, sends, _ in copies:
        pl.when(sends)(cp.wait_send)


def _host_specs(send):
    n = len(send)
    hbm = [pl.BlockSpec(memory_space=pl.ANY)] * n
    shapes = [_SDS((N_CHIPS,) + a.shape[1:], a.dtype) for a in send]
    sems = [pltpu.SemaphoreType.DMA((3 * n,)), pltpu.SemaphoreType.DMA((3 * n,))] if n else []
    return hbm, hbm, shapes, sems


def _swap_halves(arrays, name):
    n = len(arrays)

    def body(*refs):
        srcs, dsts = refs[:n], refs[n:2 * n]
        send_sems, recv_sems = refs[2 * n:]
        x, y, c = _place()
        copies = []
        for a in range(n):
            half = arrays[a].shape[1] // 2
            cp = pltpu.make_async_remote_copy(
                src_ref=srcs[a].at[:, pl.ds(pl.multiple_of((1 - c) * half, 16), half)], dst_ref=dsts[a],
                send_sem=send_sems.at[a], recv_sem=recv_sems.at[a],
                device_id=(x, y, 1 - c), device_id_type=_MESH)
            cp.start()
            copies.append(cp)
        for cp in copies:
            cp.wait_recv()
        for cp in copies:
            cp.wait_send()

    return pl.pallas_call(
        body, name=name,
        in_specs=[_ANY] * n, out_specs=[_ANY] * n,
        out_shape=[_SDS((a.shape[0], a.shape[1] // 2, a.shape[2]), a.dtype) for a in arrays],
        scratch_shapes=[pltpu.SemaphoreType.DMA((n,)), pltpu.SemaphoreType.DMA((n,))],
    )(*arrays)


N_DEV = 8
SMALL_ROWS = 32
SMALL_COLS = XBC_W


def _final_exchange(arrays, by_cols, packed):
    n = len(arrays)

    def body(*refs):
        src_ref = refs[n]
        dsts = refs[n + 1:2 * n + 1]
        dst_ref = refs[2 * n + 1]
        send_sems, recv_sems, all_send, all_recv, local_sem = refs[2 * n + 2:]
        x, y, c = _place()
        me = 4 * x + 2 * y + c
        own = pltpu.make_async_copy(src_ref, dst_ref.at[me], local_sem)
        own.start()
        copies = []
        for k in range(1, N_DEV):
            bx, by, bc = (k >> 2) & 1, (k >> 1) & 1, k & 1
            peer = (x + bx - 2 * x * bx, y + by - 2 * y * by, c + bc - 2 * c * bc)
            cp = pltpu.make_async_remote_copy(
                src_ref=src_ref, dst_ref=dst_ref.at[me], send_sem=all_send.at[k - 1], recv_sem=all_recv.at[k - 1],
                device_id=peer, device_id_type=_MESH)
            cp.start()
            copies.append(cp)
        for a in range(n):
            if by_cols[a]:
                half = arrays[a].shape[1] // 2
                mine = dsts[a].at[:, pl.ds(pl.multiple_of(c * half, 128), half)]
            else:
                half = arrays[a].shape[0] // 2
                mine = dsts[a].at[pl.ds(pl.multiple_of(c * half, 16), half)]
            cp = pltpu.make_async_remote_copy(
                src_ref=mine, dst_ref=mine, send_sem=send_sems.at[a], recv_sem=recv_sems.at[a],
                device_id=(x, y, 1 - c), device_id_type=_MESH)
            cp.start()
            copies.append(cp)
        for cp in copies:
            cp.wait_recv()
        for cp in copies:
            cp.wait_send()
        own.wait()

    vm = pl.BlockSpec(memory_space=_VMEM)
    res = pl.pallas_call(
        body, name="final_exchange",
        in_specs=[_ANY] * n + [vm], out_specs=[_ANY] * n + [vm],
        out_shape=[_SDS(a.shape, a.dtype) for a in arrays] + [_SDS((N_DEV, SMALL_ROWS, SMALL_COLS), F32)],
        input_output_aliases={a: a for a in range(n)},
        scratch_shapes=[pltpu.SemaphoreType.DMA((n,)), pltpu.SemaphoreType.DMA((n,)),
                        pltpu.SemaphoreType.DMA((N_DEV - 1,)), pltpu.SemaphoreType.DMA((N_DEV - 1,)),
                        pltpu.SemaphoreType.DMA],
    )(*arrays, packed)
    return list(res[:n]), res[n]


def _adamw(w, g, m, v):
    m = ADAM_B1 * m + (1.0 - ADAM_B1) * g
    v = ADAM_B2 * v + (1.0 - ADAM_B2) * (g * g)
    m_hat = m / (1.0 - ADAM_B1 ** ADAM_STEP)
    v_hat = v / (1.0 - ADAM_B2 ** ADAM_STEP)
    delta = -ADAM_LR * (m_hat / (jnp.sqrt(v_hat) + ADAM_EPS) + ADAM_WD * w)
    return delta, m, v


def _sum_slabs(slabs, core, name, transposed=False):
    _, h, c = slabs.shape
    tr = 128
    nblk = h // tr

    def body(core_ref, s_ref, o_ref):
        tot = ((s_ref[0].astype(F32) + s_ref[1].astype(F32)) + s_ref[2].astype(F32)) + s_ref[3].astype(F32)
        o_ref[...] = tot.T if transposed else tot

    if transposed:
        out_spec = pl.BlockSpec((c, tr), lambda i, core_ref: (0, core_ref[0] * nblk + i))
        out_shape = _SDS((c, 2 * h), F32)
    else:
        out_spec = pl.BlockSpec((tr, c), lambda i, core_ref: (core_ref[0] * nblk + i, 0))
        out_shape = _SDS((2 * h, c), F32)
    grid_spec = pltpu.PrefetchScalarGridSpec(
        num_scalar_prefetch=1, grid=(nblk,),
        in_specs=[pl.BlockSpec((N_CHIPS, tr, c), lambda i, core_ref: (0, i, 0))],
        out_specs=out_spec)
    return pl.pallas_call(
        body, name=name, grid_spec=grid_spec, out_shape=out_shape,
        compiler_params=_params(("arbitrary",)),
    )(core, slabs)


def _add_halves(own, recv, core, name):
    ns, r, c = own.shape
    half = r // 2
    tr = 128
    nblk = half // tr

    def body(core_ref, a_ref, b_ref, o_ref):
        o_ref[...] = (a_ref[...].astype(F32) + b_ref[...].astype(F32)).astype(o_ref.dtype)

    grid_spec = pltpu.PrefetchScalarGridSpec(
        num_scalar_prefetch=1, grid=(nblk,),
        in_specs=[pl.BlockSpec((ns, tr, c), lambda i, core_ref: (0, core_ref[0] * nblk + i, 0)),
                  pl.BlockSpec((ns, tr, c), lambda i, core_ref: (0, i, 0))],
        out_specs=pl.BlockSpec((ns, tr, c), lambda i, core_ref: (0, i, 0)))
    return pl.pallas_call(
        body, name=name, grid_spec=grid_spec, out_shape=_SDS((ns, half, c), own.dtype),
        compiler_params=_params(("arbitrary",)),
    )(core, own, recv)


def _update_big(w, m, v, g, name):
    r, c = w.shape

    def body(w_ref, m_ref, v_ref, g_ref, d_ref, mo_ref, vo_ref):
        delta, m_new, v_new = _adamw(w_ref[...], g_ref[...], m_ref[...], v_ref[...])
        d_ref[...] = delta
        mo_ref[...] = m_new
        vo_ref[...] = v_new

    if r % 128 == 0:
        steps, spec = r // 128, pl.BlockSpec((128, c), lambda i: (i, 0))
    else:
        steps, spec = c // 128, pl.BlockSpec((r, 128), lambda i: (0, i))
    return pl.pallas_call(
        body, name=name, grid=(steps,),
        in_specs=[spec] * 4, out_specs=[spec] * 3,
        out_shape=[_SDS((r, c), F32)] * 3,
        compiler_params=_params(("arbitrary",)),
    )(w, m, v, g)


_ROW = dict(norm_w=0, sb_norm_w=1, ssd_norm_w=2, final_norm_w=3, conv_b=4, dt_bias=5, a_log=6, d_skip=7,
            conv_w=8, sq_err=12, meta_tokens=16)
_SMALL = ("meta_tokens", "norm_w", "conv_w", "conv_b", "dt_bias", "a_log", "d_skip", "sb_norm_w", "ssd_norm_w",
          "final_norm_w")


def _pack_small(sq_err, grads):
    def rowpad(a):
        return jnp.pad(a, ((0, 0), (0, SMALL_COLS - a.shape[1])))

    rows = [rowpad(grads[k]) for k in ("norm_w", "sb_norm_w", "ssd_norm_w", "final_norm_w", "conv_b", "dt_bias", "a_log", "d_skip")]
    rows.append(grads["conv_w"])
    rows.append(rowpad(sq_err))
    rows.append(jnp.zeros((3, SMALL_COLS), F32))
    rows.append(rowpad(grads["meta_tokens"]))
    return jnp.concatenate(rows, axis=0)


def _update_small(gathered, ws, ms, vs):
    names = _SMALL
    n = len(names)

    def body(*refs):
        g_ref = refs[0]
        w_refs, m_refs, v_refs = refs[1:1 + n], refs[1 + n:1 + 2 * n], refs[1 + 2 * n:1 + 3 * n]
        outs = refs[1 + 3 * n:]
        loss_ref = outs[0]
        go, do, mo, vo = outs[1:1 + n], outs[1 + n:1 + 2 * n], outs[1 + 2 * n:1 + 3 * n], outs[1 + 3 * n:1 + 4 * n]
        tot = g_ref[0]
        for d in range(1, N_DEV):
            tot = tot + g_ref[d]
        x, y, _ = _place()
        chip = 2 * x + y
        loss_ref[...] = jnp.broadcast_to(
            0.5 * jnp.sum(tot[_ROW["sq_err"]:_ROW["sq_err"] + 1, 0:D_MODEL], axis=1, keepdims=True) / D_MODEL, (1, 128))
        for idx, nm in enumerate(names):
            r0 = _ROW[nm]
            rows, cols = w_refs[idx].shape
            if nm in ("conv_w", "meta_tokens"):
                g = jnp.zeros((rows, cols), F32)
                for j in range(N_CHIPS):
                    g = g + jnp.where(chip == j, tot[r0:r0 + rows, j * cols:(j + 1) * cols], 0.0)
            else:
                g = tot[r0:r0 + rows, 0:cols]
            delta, m_new, v_new = _adamw(w_refs[idx][...], g, m_refs[idx][...], v_refs[idx][...])
            go[idx][...] = g
            do[idx][...] = delta
            mo[idx][...] = m_new
            vo[idx][...] = v_new

    shapes = [_SDS(ws[nm].shape, F32) for nm in names]
    vm = pl.BlockSpec(memory_space=_VMEM)
    res = pl.pallas_call(
        body, name="update_small",
        in_specs=[vm] * (1 + 3 * n), out_specs=[vm] * (1 + 4 * n),
        out_shape=[_SDS((1, 128), F32)] + shapes * 4,
    )(gathered, *[ws[nm] for nm in names], *[ms[nm] for nm in names], *[vs[nm] for nm in names])
    loss = res[0][0, 0]
    g = dict(zip(names, res[1:1 + n]))
    d = dict(zip(names, res[1 + n:1 + 2 * n]))
    m = dict(zip(names, res[1 + 2 * n:1 + 3 * n]))
    v = dict(zip(names, res[1 + 3 * n:1 + 4 * n]))
    return loss, g, d, m, v


_WEIGHTS = ("meta_tokens", "norm_w", "w_in", "conv_w", "conv_b", "dt_bias", "a_log", "d_skip", "sb_norm_w",
            "ssd_norm_w", "w_out", "final_norm_w")


def kernel(x, meta_tokens, norm_w, w_in, conv_w, conv_b, dt_bias, a_log, d_skip, sb_norm_w, ssd_norm_w, w_out, final_norm_w, loss_target, m_meta_tokens, m_norm_w, m_w_in, m_conv_w, m_conv_b, m_dt_bias, m_a_log, m_d_skip, m_sb_norm_w, m_ssd_norm_w, m_w_out, m_final_norm_w, v_meta_tokens, v_norm_w, v_w_in, v_conv_w, v_conv_b, v_dt_bias, v_a_log, v_d_skip, v_sb_norm_w, v_ssd_norm_w, v_w_out, v_final_norm_w):
    given = dict(meta_tokens=meta_tokens, norm_w=norm_w, w_in=w_in, conv_w=conv_w, conv_b=conv_b, dt_bias=dt_bias,
                 a_log=a_log, d_skip=d_skip, sb_norm_w=sb_norm_w, ssd_norm_w=ssd_norm_w, w_out=w_out,
                 final_norm_w=final_norm_w)
    mom = dict(meta_tokens=m_meta_tokens, norm_w=m_norm_w, w_in=m_w_in, conv_w=m_conv_w, conv_b=m_conv_b,
               dt_bias=m_dt_bias, a_log=m_a_log, d_skip=m_d_skip, sb_norm_w=m_sb_norm_w, ssd_norm_w=m_ssd_norm_w,
               w_out=m_w_out, final_norm_w=m_final_norm_w)
    var = dict(meta_tokens=v_meta_tokens, norm_w=v_norm_w, w_in=v_w_in, conv_w=v_conv_w, conv_b=v_conv_b,
               dt_bias=v_dt_bias, a_log=v_a_log, d_skip=v_d_skip, sb_norm_w=v_sb_norm_w, ssd_norm_w=v_ssd_norm_w,
               w_out=v_w_out, final_norm_w=v_final_norm_w)
    seq = x.shape[1]

    def two_d(a):
        return a.reshape((-1, a.shape[-1])) if a.ndim != 2 else a

    def rows_first(a):
        return jnp.transpose(a, (2, 0, 1)).reshape(W_IN_SHARD, D_MODEL)

    def rows_last(a):
        return jnp.transpose(a.reshape(W_IN_SHARD, 1, D_MODEL), (1, 2, 0))

    w_in_t, m_in_t, v_in_t = rows_first(w_in), rows_first(m_w_in), rows_first(v_w_in)

    g_win, g_meta, g_cw = _gather_shards([w_in_t.astype(_MXU), meta_tokens, conv_w[0]], 1)
    w_t = jnp.pad(g_win.reshape(D_IN, D_MODEL), ((0, W_ALL - D_IN), (0, 0)))
    meta_full = jnp.swapaxes(g_meta, 0, 1).reshape(N_META, D_MODEL)
    conv_w_full = jnp.swapaxes(g_cw, 0, 1).reshape(4, XBC_W)

    core = lax.axis_index("c").astype(jnp.int32).reshape(1)

    def early(g_wout):
        slab_out = g_wout.reshape(N_CHIPS, W_OUT_SHARD, D_MODEL).astype(_MXU)
        (sib_out,) = _swap_halves([slab_out], "swap_halves_w_out")
        return (_add_halves(slab_out, sib_out, core, "chip_sum_w_out"),), ((0, N_CHIPS),)

    def late(g_win):
        (sib_in,) = _swap_halves([g_win], "swap_halves_w_in")
        return (_add_halves(g_win, sib_in, core, "chip_sum_w_in"),), ((0, N_CHIPS),)

    sq_err, g_x, grads = _device_grads(
        x.reshape(seq, D_MODEL), loss_target.reshape(seq, D_MODEL), meta_full, norm_w, w_t, conv_w_full,
        conv_b, dt_bias, a_log, d_skip, sb_norm_w, ssd_norm_w, None, final_norm_w.reshape(1, D_MODEL),
        exchange=dict(early=early, late=late), w_out_shard=w_out[0].astype(_MXU))
    chip_out, chip_in = grads["sent"]
    got_out, got_in = grads["arrived"]
    chip = 2 * lax.axis_index("x") + lax.axis_index("y")

    def with_own(got, sent):
        own = lax.dynamic_slice(sent, (chip, 0, 0), (1,) + sent.shape[1:])
        return lax.dynamic_update_slice(got, own, (chip, 0, 0))

    (g_in, g_out), gathered = _final_exchange(
        [_sum_slabs(with_own(got_in, chip_in), core, "sum_w_in", transposed=True),
         _sum_slabs(with_own(got_out, chip_out), core, "sum_w_out")], (True, False), _pack_small(sq_err, grads))
    g_in = lax.dynamic_slice(g_in, (4 * chip, 0), (W_IN_SHARD, D_MODEL))
    big = dict(w_in=tuple(rows_last(a) for a in (g_in,) + tuple(_update_big(w_in_t, m_in_t, v_in_t, g_in, "update_w_in"))),
               w_out=(g_out,) + tuple(_update_big(w_out[0], m_w_out[0], v_w_out[0], g_out, "update_w_out")))

    loss, sg, sd, sm, sv = _update_small(
        gathered, {k: two_d(given[k]) for k in _SMALL}, {k: two_d(mom[k]) for k in _SMALL},
        {k: two_d(var[k]) for k in _SMALL})

    out = {}
    for idx, group in enumerate((sg, sd, sm, sv)):
        for k in _SMALL:
            out[(idx, k)] = group[k].reshape(given[k].shape)
        for k in ("w_in", "w_out"):
            out[(idx, k)] = big[k][idx].reshape(given[k].shape)
    return (loss, g_x.reshape(x.shape), *[out[(idx, k)] for idx in range(4) for k in _WEIGHTS])
```

```python
import functools
import math

import jax
import jax.numpy as jnp
from jax import lax
from jax.experimental import pallas as pl
from jax.experimental.pallas import tpu as pltpu

F32 = jnp.float32
_MXU = jnp.bfloat16

D_MODEL = 1024
N_META = 16
PAD = 112
OFF = PAD + N_META
TM = 128
CHUNK = 64
SB_W = 1024
SSD_W = 1024
N_HEADS = 16
HEAD = 64
N_GROUPS = 2
N_STATE = 128
XBC_W = SSD_W + 2 * N_GROUPS * N_STATE
N_MAIN = 4 * SB_W + SSD_W + XBC_W
QKV_W = 3 * SB_W
REST_W = N_MAIN - QKV_W
COL_GATE = 3 * SB_W
COL_Z = 4 * SB_W
COL_XBC = 5 * SB_W
D_IN = N_MAIN + N_HEADS
W_ALL = N_MAIN + 128
WIN_STEP = 1664
WIN_W = 1792
EPS = 1e-5
N_CHIPS = 4
W_IN_SHARD = D_IN // N_CHIPS
W_OUT_SHARD = 2 * D_MODEL // N_CHIPS

ADAM_LR = 0.001
ADAM_B1 = 0.9
ADAM_B2 = 0.999
ADAM_EPS = 1e-08
ADAM_WD = 0.01
ADAM_STEP = 10

_SDS = jax.ShapeDtypeStruct
_NT = (((1,), (1,)), ((), ()))
_TN = (((0,), (0,)), ((), ()))
_VMEM = pltpu.VMEM


def _params(sem=None, vmem_mb=None):
    kw = {}
    if sem is not None:
        kw["dimension_semantics"] = sem
    if vmem_mb is not None:
        kw["vmem_limit_bytes"] = vmem_mb * 1024 * 1024
    return pltpu.CompilerParams(**kw)


def _mm(a, b):
    return jnp.dot(a.astype(_MXU), b.astype(_MXU), preferred_element_type=F32)


def _mm_nt(a, b):
    return lax.dot_general(a.astype(_MXU), b.astype(_MXU), _NT, preferred_element_type=F32)


def _mm_tn(a, b):
    return lax.dot_general(a.astype(_MXU), b.astype(_MXU), _TN, preferred_element_type=F32)


def _split(x, parts):
    out = []
    r = x
    for _ in range(parts):
        p = r.astype(_MXU)
        out.append(p)
        r = r - p.astype(F32)
    return out


def _sel_right(x, m01, parts=2):
    acc = None
    for p in _split(x, parts):
        t = jnp.dot(p, m01, preferred_element_type=F32)
        acc = t if acc is None else acc + t
    return acc


def _sel_left(m01, x, parts=2):
    acc = None
    for p in _split(x, parts):
        t = jnp.dot(m01, p, preferred_element_type=F32)
        acc = t if acc is None else acc + t
    return acc


def _iota(shape, axis):
    return lax.broadcasted_iota(jnp.int32, shape, axis)


def _sigmoid(x):
    return 1.0 / (1.0 + jnp.exp(-x))


def _prep(x2d, meta_full, norm_w):
    seq = x2d.shape[0]
    lp = seq + OFF
    nb = lp // TM

    def body(x_ref, meta_ref, w_ref, h0_ref, u_ref, ut_ref):
        i = pl.program_id(0)

        @pl.when(i == 0)
        def _():
            h0_ref[...] = jnp.concatenate([jnp.zeros((PAD, D_MODEL), F32), meta_ref[...]], axis=0)

        @pl.when(i > 0)
        def _():
            h0_ref[...] = x_ref[...]

        h = h0_ref[...]
        rs = lax.rsqrt(jnp.mean(h * h, axis=-1, keepdims=True) + EPS)
        u = (h * rs * w_ref[...]).astype(_MXU)
        u_ref[...] = u
        ut_ref[...] = u.T

    return pl.pallas_call(
        body, name="prep", grid=(nb,),
        in_specs=[pl.BlockSpec((TM, D_MODEL), lambda i: (jnp.maximum(i - 1, 0), 0)),
                  pl.BlockSpec((N_META, D_MODEL), lambda i: (0, 0)),
                  pl.BlockSpec((1, D_MODEL), lambda i: (0, 0))],
        out_specs=[pl.BlockSpec((TM, D_MODEL), lambda i: (i, 0)),
                   pl.BlockSpec((TM, D_MODEL), lambda i: (i, 0)),
                   pl.BlockSpec((D_MODEL, TM), lambda i: (0, i))],
        out_shape=[_SDS((lp, D_MODEL), F32), _SDS((lp, D_MODEL), _MXU), _SDS((D_MODEL, lp), _MXU)],
        compiler_params=_params(("arbitrary",)),
    )(x2d, meta_full, norm_w)


def _inproj(u, w_t):
    lp = u.shape[0]
    tn = 512

    nq = QKV_W // tn

    def body(u_ref, w_ref, wdt_ref, qkv_ref, rest_ref, odt_ref):
        j = pl.program_id(0)
        res = lax.dot_general(u_ref[...], w_ref[...], _NT, preferred_element_type=F32)

        @pl.when(j < nq)
        def _():
            qkv_ref[...] = res.astype(qkv_ref.dtype)

        @pl.when(j >= nq)
        def _():
            rest_ref[...] = res

        @pl.when(j == 0)
        def _():
            odt_ref[...] = lax.dot_general(u_ref[...], wdt_ref[...], _NT, preferred_element_type=F32)

    return pl.pallas_call(
        body, name="inproj", grid=(N_MAIN // tn,),
        in_specs=[pl.BlockSpec((lp, D_MODEL), lambda j: (0, 0)),
                  pl.BlockSpec((tn, D_MODEL), lambda j: (j, 0)),
                  pl.BlockSpec((128, D_MODEL), lambda j: (N_MAIN // 128, 0))],
        out_specs=[pl.BlockSpec((lp, tn), lambda j: (0, jnp.minimum(j, nq - 1))),
                   pl.BlockSpec((lp, tn), lambda j: (0, jnp.maximum(j - nq, 0))),
                   pl.BlockSpec((lp, 128), lambda j: (0, 0))],
        out_shape=[_SDS((lp, QKV_W), _MXU), _SDS((lp, REST_W), F32), _SDS((lp, 128), F32)],
        compiler_params=_params(("arbitrary",), 48),
    )(u, w_t, w_t)


SB_WINDOW = 3
SB_TOP = 16
SB_DEAD = -104.0


def _sb_logs(qh, kwin):
    z = lax.dot_general(qh, kwin, _NT, preferred_element_type=F32)
    e = jnp.exp(-jnp.abs(z))
    l1p = jnp.log(1.0 + e)
    lk_full = -(jnp.maximum(z, 0.0) + l1p)
    ls = jnp.minimum(z, 0.0) - l1p
    return z, e, ls, lk_full


def _blk(a, b):
    return a[:, b * TM:(b + 1) * TM]


def _stacked_sel(blocks, m01, parts):
    n = len(blocks)
    rows = blocks[0].shape[0]
    pieces = [_split(b, parts) for b in blocks]
    stacked = jnp.concatenate([p[k] for k in range(parts) for p in pieces], axis=0)
    res = jnp.dot(stacked, m01, preferred_element_type=F32)
    out = []
    for j in range(n):
        tot = res[j * rows:(j + 1) * rows]
        for k in range(1, parts):
            tot = tot + res[(k * n + j) * rows:(k * n + j + 1) * rows]
        out.append(tot)
    return out


def _sb_weights(ls, lk_full, run, last_mask, upper, n):
    lk = [_blk(lk_full, b) for b in range(n)]
    lk[n - 1] = jnp.where(last_mask, lk[n - 1], 0.0)
    aft = _stacked_sel(lk, upper, 1)
    w = [None] * n
    for b in range(n - 1, -1, -1):
        wb = jnp.exp(_blk(ls, b) + aft[b] + run)
        w[b] = jnp.where(last_mask, wb, 0.0) if b == n - 1 else wb
        run = run + jnp.sum(lk[b], axis=1, keepdims=True)
    return w, run


def _sb_alive(run_scr):
    top = jnp.max(run_scr[:, 0:SB_TOP, :]) > SB_DEAD
    rest = jnp.max(run_scr[:, SB_TOP:, :]) > SB_DEAD
    return top.astype(jnp.int32), rest.astype(jnp.int32)


def _sb_walk(i, key_set, strict, run_scr):
    @pl.when(i >= SB_WINDOW - 1)
    def _():
        key_set(i - (SB_WINDOW - 1), SB_WINDOW, strict, TM)

    start = jnp.where(i >= SB_WINDOW - 1, i - SB_WINDOW, i)

    def cond(c):
        return jnp.logical_and(c[0] >= 0, c[1] + c[2] > 0)

    def step(c):
        kb, _, rest = c
        mask = jnp.logical_or(strict, kb < i)

        @pl.when(rest > 0)
        def _():
            key_set(kb, 1, mask, TM)

        @pl.when(rest == 0)
        def _():
            key_set(kb, 1, mask, SB_TOP)

        return (kb - 1,) + _sb_alive(run_scr)

    lax.while_loop(cond, step, (start,) + _sb_alive(run_scr))


SB_LANES_FWD = 256
SB_LANES_BWD = 256


def _head_masks(lanes):
    lane = _iota((TM, lanes), 1)
    return tuple(jnp.logical_and(lane >= h * HEAD, lane < (h + 1) * HEAD) for h in range(lanes // HEAD))


def _by_head(hmask, parts):
    out = parts[-1]
    for h in range(len(parts) - 2, -1, -1):
        out = jnp.where(hmask[h], parts[h], out)
    return out


def _sb_fwd(qkv, fetch=()):
    lp = qkv.shape[0]
    nb = lp // TM
    lw = SB_LANES_FWD
    nh = lw // HEAD
    npair = SB_W // lw
    nf = len(fetch)
    host_in, host_out, host_shapes, host_sems = _gather_specs(fetch, nf)

    def body(q_ref, k_ref, v_ref, *rest):
        srcs, (o_ref, olo_ref), dsts = rest[:nf], rest[nf:nf + 2], rest[nf + 2:2 * nf + 2]
        acc, run_scr = rest[2 * nf + 2:2 * nf + 4]
        host_sem_refs = rest[2 * nf + 4:]
        p = pl.program_id(0)
        i = pl.program_id(1)

        if nf:
            @pl.when(jnp.logical_and(p == 0, i == 0))
            def _():
                for cp in _gather_copies(srcs, dsts, nf, *host_sem_refs)[0]:
                    cp.start()

        lane = _iota((TM, TM), 1)
        row = _iota((TM, TM), 0)
        hmask = _head_masks(lw)
        upper = (row > lane).astype(_MXU)
        strict = lane < row
        q = q_ref[...] * (1.0 / math.sqrt(HEAD))
        qh = tuple(jnp.where(m, q, 0.0).astype(_MXU) for m in hmask)

        def key_set(first, n, last_mask, nrows):
            off = pl.multiple_of(first * TM, TM)
            kwin = k_ref[pl.ds(off, n * TM), :].astype(_MXU)
            vwin = v_ref[pl.ds(off, n * TM), :].astype(_MXU)
            for hh in range(nh):
                run = run_scr[hh, 0:nrows, 0:1]
                _, _, ls, lk_full = _sb_logs(qh[hh][0:nrows], kwin)
                w, run = _sb_weights(ls, lk_full, run, last_mask[0:nrows], upper, n)
                pieces = [_split(wb, 2) for wb in w]
                stacked = jnp.concatenate(
                    [jnp.concatenate([p[0] for p in pieces], axis=1), jnp.concatenate([p[1] for p in pieces], axis=1)], axis=0)
                res = jnp.dot(stacked, vwin, preferred_element_type=F32)
                acc[hh, 0:nrows] += res[0:nrows]
                acc[nh + hh, 0:nrows] += res[nrows:2 * nrows]
                run_scr[hh, 0:nrows] = jnp.broadcast_to(run, (nrows, TM))

        acc[...] = jnp.zeros_like(acc)
        run_scr[...] = jnp.zeros_like(run_scr)
        _sb_walk(i, key_set, strict, run_scr)
        o_ref[...] = _by_head(hmask, [acc[h] for h in range(nh)])
        olo_ref[...] = _by_head(hmask, [acc[nh + h] for h in range(nh)])

        if nf:
            @pl.when(jnp.logical_and(p == npair - 1, i == nb - 1))
            def _():
                _gather_finish(*_gather_copies(srcs, dsts, nf, *host_sem_refs))

    blk = pl.BlockSpec((TM, lw), lambda p, i: (i, p))
    res = pl.pallas_call(
        body, name="sb_fwd", grid=(npair, nb),
        in_specs=[blk,
                  pl.BlockSpec((lp, lw), lambda p, i: (0, npair + p)),
                  pl.BlockSpec((lp, lw), lambda p, i: (0, 2 * npair + p))] + host_in,
        out_specs=[blk, blk] + host_out,
        out_shape=[_SDS((lp, SB_W), F32), _SDS((lp, SB_W), F32)] + host_shapes,
        scratch_shapes=[pltpu.VMEM((2 * nh, TM, lw), F32), pltpu.VMEM((nh, TM, TM), F32)] + host_sems,
        compiler_params=_params(("arbitrary", "arbitrary")),
    )(qkv, qkv, qkv, *fetch)
    return res[0], res[1], _own_slot(res[2:], fetch)


def _sb_bwd(qkv, o_sb, o_lo, d_o, d_proj, send=(), dests=()):
    lp = qkv.shape[0]
    nb = lp // TM
    lw = SB_LANES_BWD
    nh = lw // HEAD
    npair = SB_W // lw
    scale = 1.0 / math.sqrt(HEAD)
    n = len(send)
    host_in, host_out, host_shapes, host_sems = _host_specs(send)

    def body(q_ref, k_ref, v_ref, o_ref, olo_ref, do_ref, dproj_in, *rest):
        srcs, dproj_ref, dsts = rest[:n], rest[n], rest[n + 1:2 * n + 1]
        dq_all, dk_ref, dv_ref, stage, sems, dq_acc, run_scr, gsum_scr = rest[2 * n + 1:2 * n + 9]
        host_sem_refs = rest[2 * n + 9:]
        p = pl.program_id(0)
        i = pl.program_id(1)

        if n:
            @pl.when(jnp.logical_and(p == 0, i == 0))
            def _():
                _start_copies(_slab_copies(srcs, dsts, dests, *host_sem_refs))

        @pl.when(i == 0)
        def _():
            dk_ref[...] = jnp.zeros_like(dk_ref)
            dv_ref[...] = jnp.zeros_like(dv_ref)

        lane = _iota((TM, TM), 1)
        row = _iota((TM, TM), 0)
        hmask = _head_masks(lw)
        upper = (row > lane).astype(_MXU)
        lower_incl = (row >= lane).astype(_MXU)
        strict = lane < row
        q = q_ref[...] * scale
        do = do_ref[...]
        prod = do.astype(_MXU).astype(F32) * (o_ref[...] + olo_ref[...])
        qh = tuple(jnp.where(m, q, 0.0).astype(_MXU) for m in hmask)
        doh = tuple(jnp.where(m, do, 0.0).astype(_MXU) for m in hmask)
        gtot = tuple(jnp.sum(jnp.where(m, prod, 0.0), axis=1, keepdims=True) for m in hmask)

        def key_set(first, n, last_mask, nrows):
            off = pl.multiple_of(first * TM, TM)
            kf = k_ref[pl.ds(off, n * TM), :]
            kwin = kf.astype(_MXU)
            vwin = v_ref[pl.ds(off, n * TM), :].astype(_MXU)
            last_mask = last_mask[0:nrows]
            dk_win = None
            for hh in range(nh):
                run = run_scr[hh, 0:nrows, 0:1]
                gsum = gsum_scr[hh, 0:nrows, 0:1]
                z, e, ls, lk_full = _sb_logs(qh[hh][0:nrows], kwin)
                w, run = _sb_weights(ls, lk_full, run, last_mask, upper, n)
                r = 1.0 / (1.0 + e)
                er = e * r
                pos = z >= 0.0
                beta = jnp.where(pos, r, er)
                one_m_beta = jnp.where(pos, er, r)
                dw = lax.dot_general(doh[hh][0:nrows], vwin, _NT, preferred_element_type=F32)
                g = [_blk(dw, b) * w[b] for b in range(n)]
                suffix = _stacked_sel(g, lower_incl, 2)
                dz = [None] * n
                for b in range(n - 1, -1, -1):
                    prefix = gtot[hh][0:nrows] - gsum - suffix[b]
                    d = g[b] * _blk(one_m_beta, b) - _blk(beta, b) * prefix
                    dz[b] = (jnp.where(last_mask, d, 0.0) if b == n - 1 else d).astype(_MXU)
                    gsum = gsum + jnp.sum(g[b], axis=1, keepdims=True)
                dzw = jnp.concatenate(dz, axis=1)
                ww = jnp.concatenate([wb.astype(_MXU) for wb in w], axis=1)
                kh = jnp.where(hmask[hh][0:1, :], kf, 0.0).astype(_MXU)
                dq_acc[0:nrows] += jnp.dot(dzw, kh, preferred_element_type=F32)
                dk_h = lax.dot_general(dzw, qh[hh][0:nrows], _TN, preferred_element_type=F32)
                dv_h = lax.dot_general(ww, doh[hh][0:nrows], _TN, preferred_element_type=F32)
                dk_win = (dk_h, dv_h) if dk_win is None else (dk_win[0] + dk_h, dk_win[1] + dv_h)
                run_scr[hh, 0:nrows] = jnp.broadcast_to(run, (nrows, TM))
                gsum_scr[hh, 0:nrows] = jnp.broadcast_to(gsum, (nrows, TM))
            dk_ref[pl.ds(off, n * TM), :] += dk_win[0]
            dv_ref[pl.ds(off, n * TM), :] += dk_win[1]

        dq_acc[...] = jnp.zeros_like(dq_acc)
        run_scr[...] = jnp.zeros_like(run_scr)
        gsum_scr[...] = jnp.zeros_like(gsum_scr)

        _sb_walk(i, key_set, strict, run_scr)
        dq_all[pl.ds(pl.multiple_of(i * TM, TM), TM), :] = dq_acc[...] * scale

        @pl.when(i == nb - 1)
        def _():
            copies = []
            for s, src in enumerate((dq_all, dk_ref, dv_ref)):
                stage[s] = src[...].astype(_MXU)
                col = pl.multiple_of((s * npair + p) * lw, lw)
                copies.append(pltpu.make_async_copy(stage.at[s], dproj_ref.at[:, pl.ds(col, lw)], sems.at[s]))
                copies[-1].start()
            for cp in copies:
                cp.wait()

        if n:
            @pl.when(jnp.logical_and(p == npair - 1, i == nb - 1))
            def _():
                _finish_copies(_slab_copies(srcs, dsts, dests, *host_sem_refs))

    blk = pl.BlockSpec((TM, lw), lambda p, i: (i, p))
    res = pl.pallas_call(
        body, name="sb_bwd", grid=(npair, nb),
        in_specs=[blk,
                  pl.BlockSpec((lp, lw), lambda p, i: (0, npair + p)),
                  pl.BlockSpec((lp, lw), lambda p, i: (0, 2 * npair + p)),
                  blk, blk, blk, pl.BlockSpec(memory_space=pl.ANY)] + host_in,
        out_specs=[pl.BlockSpec(memory_space=pl.ANY)] + host_out,
        out_shape=[_SDS(d_proj.shape, d_proj.dtype)] + host_shapes,
        input_output_aliases={6: 0},
        scratch_shapes=[pltpu.VMEM((lp, lw), F32), pltpu.VMEM((lp, lw), F32), pltpu.VMEM((lp, lw), F32),
                        pltpu.VMEM((3, lp, lw), _MXU), pltpu.SemaphoreType.DMA((3,)),
                        pltpu.VMEM((TM, lw), F32), pltpu.VMEM((nh, TM, TM), F32),
                        pltpu.VMEM((nh, TM, TM), F32)] + host_sems,
        compiler_params=_params(("arbitrary", "arbitrary")),
    )(qkv, qkv, qkv, o_sb, o_lo, d_o, d_proj, *send)
    return res[0], list(res[1:])


def _conv_pre(x_ref, w_ref, b_ref, lp):
    n = lp - 8
    w = w_ref[...]
    pre = (x_ref[pl.ds(5, n), :] * w[0:1, :] + x_ref[pl.ds(6, n), :] * w[1:2, :]
           + x_ref[pl.ds(7, n), :] * w[2:3, :] + x_ref[pl.ds(8, n), :] * w[3:4, :]) + b_ref[...]
    live = (_iota((n, 128), 0) + 8) >= PAD
    return pre, live


def _conv_fwd(proj, dt_raw, conv_w, conv_b, dt_bias128):
    lp = proj.shape[0]
    nblk = XBC_W // 128
    c0 = (COL_XBC - QKV_W) // 128

    def body(x_ref, w_ref, b_ref, dtr_ref, dtb_ref, o_ref, dt_ref):
        pre, live = _conv_pre(x_ref, w_ref, b_ref, lp)
        act = pre * _sigmoid(pre)
        o_ref[pl.ds(0, 8), :] = jnp.zeros((8, 128), F32)
        o_ref[pl.ds(8, lp - 8), :] = jnp.where(live, act, 0.0)

        @pl.when(pl.program_id(0) == 0)
        def _():
            s = dtr_ref[...] + dtb_ref[...]
            sp = jnp.maximum(s, 0.0) + jnp.log(1.0 + jnp.exp(-jnp.abs(s)))
            dt_ref[...] = jnp.where(_iota((lp, 128), 0) >= PAD, sp, 0.0)

    return pl.pallas_call(
        body, name="conv_fwd", grid=(nblk,),
        in_specs=[pl.BlockSpec((lp, 128), lambda j: (0, c0 + j)),
                  pl.BlockSpec((4, 128), lambda j: (0, j)),
                  pl.BlockSpec((1, 128), lambda j: (0, j)),
                  pl.BlockSpec((lp, 128), lambda j: (0, 0)),
                  pl.BlockSpec((1, 128), lambda j: (0, 0))],
        out_specs=[pl.BlockSpec((lp, 128), lambda j: (0, j)),
                   pl.BlockSpec((lp, 128), lambda j: (0, 0))],
        out_shape=[_SDS((lp, XBC_W), F32), _SDS((lp, 128), F32)],
        compiler_params=_params(("arbitrary",)),
    )(proj, conv_w, conv_b, dt_raw, dt_bias128)


def _conv_bwd(proj, dt_raw, conv_w, conv_b, dt_bias128, d_xbc, d_dt128, d_proj):
    lp = proj.shape[0]
    nblk = XBC_W // 128
    c0 = COL_XBC // 128
    c0_in = (COL_XBC - QKV_W) // 128
    n = lp - 8
    last = nblk - 1

    def body(x_ref, w_ref, b_ref, dtr_ref, dtb_ref, dy_ref, ddt_ref, dproj_in,
             dx_ref, gw_ref, gb_ref, gdtb_ref, scr):
        j = pl.program_id(0)

        @pl.when(j < nblk)
        def _():
            pre, live = _conv_pre(x_ref, w_ref, b_ref, lp)
            sg = _sigmoid(pre)
            dpre = jnp.where(live, dy_ref[pl.ds(8, n), :] * (sg * (1.0 + pre * (1.0 - sg))), 0.0)
            gb_ref[...] = jnp.sum(dpre, axis=0, keepdims=True)
            gw_ref[...] = jnp.concatenate(
                [jnp.sum(dpre * x_ref[pl.ds(5 + k, n), :], axis=0, keepdims=True) for k in range(4)], axis=0)
            scr[pl.ds(0, 8), :] = jnp.zeros((8, 128), F32)
            scr[pl.ds(8, n), :] = dpre
            scr[pl.ds(lp, 8), :] = jnp.zeros((8, 128), F32)
            w = w_ref[...]
            dx_ref[...] = (scr[pl.ds(0, lp), :] * w[3:4, :] + scr[pl.ds(1, lp), :] * w[2:3, :]
                           + scr[pl.ds(2, lp), :] * w[1:2, :] + scr[pl.ds(3, lp), :] * w[0:1, :]).astype(dx_ref.dtype)

        @pl.when(j == nblk)
        def _():
            s = dtr_ref[...] + dtb_ref[...]
            d = jnp.where(_iota((lp, 128), 0) >= PAD, ddt_ref[...] * _sigmoid(s), 0.0)
            dx_ref[...] = d.astype(dx_ref.dtype)
            gdtb_ref[...] = jnp.sum(d, axis=0, keepdims=True)

    clamp = lambda j: (0, jnp.minimum(j, last))
    full128 = pl.BlockSpec((lp, 128), lambda j: (0, 0))
    return pl.pallas_call(
        body, name="conv_bwd", grid=(nblk + 1,),
        in_specs=[pl.BlockSpec((lp, 128), lambda j: (0, c0_in + jnp.minimum(j, last))),
                  pl.BlockSpec((4, 128), clamp),
                  pl.BlockSpec((1, 128), clamp),
                  full128, pl.BlockSpec((1, 128), lambda j: (0, 0)),
                  pl.BlockSpec((lp, 128), clamp), full128, pl.BlockSpec(memory_space=pl.ANY)],
        out_specs=[pl.BlockSpec((lp, 128), lambda j: (0, c0 + j)), pl.BlockSpec((4, 128), clamp),
                   pl.BlockSpec((1, 128), clamp), pl.BlockSpec((1, 128), lambda j: (0, 0))],
        out_shape=[_SDS(d_proj.shape, d_proj.dtype), _SDS((4, XBC_W), F32), _SDS((1, XBC_W), F32), _SDS((1, 128), F32)],
        input_output_aliases={7: 0},
        scratch_shapes=[pltpu.VMEM((lp + 8, 128), F32)],
        compiler_params=_params(("arbitrary",)),
    )(proj, conv_w, conv_b, dt_raw, dt_bias128, d_xbc, d_dt128, d_proj)


def _ssd_pieces(dt, dt_t, a, a_t):
    r64 = _iota((CHUNK, CHUNK), 0)
    c64 = _iota((CHUNK, CHUNK), 1)
    tril = c64 <= r64
    tril01 = tril.astype(_MXU)
    triu01 = (r64 <= c64).astype(_MXU)
    expand = (lax.shift_right_logical(_iota((N_HEADS, SSD_W), 1), 6) == _iota((N_HEADS, SSD_W), 0)).astype(_MXU)
    acum = _sel_left(tril01, dt * a)
    acum_t = _sel_right(dt_t * a_t, triu01)
    ax = _sel_right(acum, expand)
    dtx = _sel_right(dt, expand)
    return tril, expand, acum, acum_t, ax, dtx


def _seg_matrix():
    return (lax.shift_right_logical(_iota((SSD_W, N_HEADS), 0), 6) == _iota((SSD_W, N_HEADS), 1)).astype(_MXU)


def _head_decay(ax, acum_t, h, tril):
    col = ax[:, h * HEAD:(h + 1) * HEAD]
    rowv = acum_t[h:h + 1, :]
    return jnp.where(tril, jnp.exp(jnp.minimum(col - rowv, 0.0)), 0.0)


def _ssd_fwd(xbc, dt_c, dt_tc, a, a_t, dskip_x):
    lp = xbc.shape[0]
    nc = lp // CHUNK
    gw = SSD_W // N_GROUPS
    hpg = N_HEADS // N_GROUPS

    def body(x_ref, dt_ref, dtt_ref, a_ref, at_ref, d_ref, y_ref, st_ref, state):
        c = pl.program_id(0)

        @pl.when(c == 0)
        def _():
            state[...] = jnp.zeros_like(state)

        st_ref[0] = state[...]
        tril, _, _, acum_t, ax, dtx = _ssd_pieces(dt_ref[0], dtt_ref[0], a_ref[...], at_ref[...])
        x = x_ref[:, 0:SSD_W]
        xdt = x * dtx
        ea = jnp.exp(ax)
        aex = ax[CHUNK - 1:CHUNK, :]
        wd = jnp.exp(aex - ax)
        eae = jnp.exp(aex)
        xw = xdt * wd
        y_ref[...] = x * d_ref[...]
        for g in range(N_GROUPS):
            gs = slice(g * gw, (g + 1) * gw)
            rs = slice(g * N_STATE, (g + 1) * N_STATE)
            bg = x_ref[:, SSD_W + g * N_STATE:SSD_W + (g + 1) * N_STATE]
            cg = x_ref[:, SSD_W + N_GROUPS * N_STATE + g * N_STATE:SSD_W + N_GROUPS * N_STATE + (g + 1) * N_STATE]
            sg = state[rs, :]
            cb = _mm_nt(cg, bg)
            y_ref[:, gs] += _mm(cg, sg) * ea[:, gs]
            for r in range(hpg):
                h = g * hpg + r
                hs = slice(h * HEAD, (h + 1) * HEAD)
                m = cb * _head_decay(ax, acum_t, h, tril)
                y_ref[:, hs] += _mm(m, xdt[:, hs])
            state[rs, :] = sg * eae[:, gs] + _mm_tn(bg, xw[:, gs])

    return pl.pallas_call(
        body, name="ssd_fwd", grid=(nc,),
        in_specs=[pl.BlockSpec((CHUNK, XBC_W), lambda c: (c, 0)),
                  pl.BlockSpec((1, CHUNK, N_HEADS), lambda c: (c, 0, 0)),
                  pl.BlockSpec((1, N_HEADS, CHUNK), lambda c: (c, 0, 0)),
                  pl.BlockSpec((1, N_HEADS), lambda c: (0, 0)),
                  pl.BlockSpec((N_HEADS, 1), lambda c: (0, 0)),
                  pl.BlockSpec((1, SSD_W), lambda c: (0, 0))],
        out_specs=[pl.BlockSpec((CHUNK, SSD_W), lambda c: (c, 0)),
                   pl.BlockSpec((1, N_GROUPS * N_STATE, gw), lambda c: (c, 0, 0))],
        out_shape=[_SDS((lp, SSD_W), F32), _SDS((nc, N_GROUPS * N_STATE, gw), F32)],
        scratch_shapes=[pltpu.VMEM((N_GROUPS * N_STATE, gw), F32)],
        compiler_params=_params(("arbitrary",)),
    )(xbc, dt_c, dt_tc, a, a_t, dskip_x)


def _ssd_bwd(xbc, dt_c, dt_tc, a, a_t, dskip_x, states, d_y):
    lp = xbc.shape[0]
    nc = lp // CHUNK
    gw = SSD_W // N_GROUPS
    hpg = N_HEADS // N_GROUPS

    def body(x_ref, dt_ref, dtt_ref, a_ref, at_ref, d_ref, st_ref, dy_ref,
             dx_ref, ddta_ref, ddtb_ref, ga1_ref, ga2_ref, gd_ref, dstate, dxdt_scr, z_scr, yoff_scr, sds_scr):
        c = pl.program_id(0)

        @pl.when(c == 0)
        def _():
            dstate[...] = jnp.zeros_like(dstate)
            ga1_ref[...] = jnp.zeros_like(ga1_ref)
            ga2_ref[...] = jnp.zeros_like(ga2_ref)
            gd_ref[...] = jnp.zeros_like(gd_ref)

        dt = dt_ref[0]
        dt_t = dtt_ref[0]
        a = a_ref[...]
        a_t = at_ref[...]
        tril, _, acum, acum_t, ax, dtx = _ssd_pieces(dt, dt_t, a, a_t)
        seg = _seg_matrix()
        x = x_ref[:, 0:SSD_W]
        dy = dy_ref[...]
        xdt = x * dtx
        ea = jnp.exp(ax)
        aex = ax[CHUNK - 1:CHUNK, :]
        wd = jnp.exp(aex - ax)
        eae = jnp.exp(aex)
        xw = xdt * wd
        edy = ea * dy
        lane16 = _iota((CHUNK, N_HEADS), 1)
        row16 = _iota((N_HEADS, CHUNK), 0)
        da_col = jnp.zeros((CHUNK, N_HEADS), F32)
        da_row = jnp.zeros((N_HEADS, CHUNK), F32)
        for g in range(N_GROUPS):
            gs = slice(g * gw, (g + 1) * gw)
            rs = slice(g * N_STATE, (g + 1) * N_STATE)
            bcol = slice(SSD_W + g * N_STATE, SSD_W + (g + 1) * N_STATE)
            ccol = slice(SSD_W + N_GROUPS * N_STATE + g * N_STATE, SSD_W + N_GROUPS * N_STATE + (g + 1) * N_STATE)
            bg = x_ref[:, bcol]
            cg = x_ref[:, ccol]
            sg = st_ref[0, rs, :]
            dsn = dstate[rs, :]
            cb = _mm_nt(cg, bg)
            z_scr[:, gs] = _mm(bg, dsn)
            yoff_scr[:, gs] = _mm(cg, sg) * ea[:, gs]
            sds_scr[:, gs] = jnp.broadcast_to(jnp.sum(dsn * sg, axis=0, keepdims=True), (8, gw))
            dcb = jnp.zeros((CHUNK, CHUNK), F32)
            for r in range(hpg):
                h = g * hpg + r
                hs = slice(h * HEAD, (h + 1) * HEAD)
                dec = _head_decay(ax, acum_t, h, tril)
                m = cb * dec
                t1 = _mm_nt(dy[:, hs], xdt[:, hs])
                dcb = dcb + dec * t1
                tm = m * t1
                da_col = da_col + jnp.where(lane16 == h, jnp.sum(tm, axis=1, keepdims=True), 0.0)
                da_row = da_row - jnp.where(row16 == h, jnp.sum(tm, axis=0, keepdims=True), 0.0)
                dxdt_scr[:, hs] = _mm_tn(m, dy[:, hs])
            dx_ref[:, ccol] = _mm(dcb, bg) + _mm_nt(edy[:, gs], sg)
            dx_ref[:, bcol] = _mm_tn(dcb, cg) + _mm_nt(xw[:, gs], dsn)
            dstate[rs, :] = eae[:, gs] * dsn + _mm_tn(cg, edy[:, gs])
        zf = z_scr[...]
        dxdt = dxdt_scr[...] + wd * zf
        t3 = _sel_right(xw * zf, seg)
        da_col = da_col + _sel_right(dy * yoff_scr[...], seg) - t3
        aend = acum[CHUNK - 1:CHUNK, :]
        sd = _sel_right(sds_scr[...], seg)[0:1, :] * jnp.exp(aend)
        last = jnp.sum(t3, axis=0, keepdims=True) + sd
        da_col = da_col + jnp.where(_iota((CHUNK, N_HEADS), 0) == CHUNK - 1, last, 0.0)
        r64 = _iota((CHUNK, CHUNK), 0)
        c64 = _iota((CHUNK, CHUNK), 1)
        ddta1 = _sel_left((c64 >= r64).astype(_MXU), da_col)
        ddta2 = _sel_right(da_row, (r64 >= c64).astype(_MXU))
        ddta_ref[0] = a * ddta1 + _sel_right(dxdt * x, seg)
        ddtb_ref[0] = a_t * ddta2
        ga1_ref[...] += jnp.sum(dt * ddta1, axis=0, keepdims=True)
        ga2_ref[...] += jnp.sum(dt_t * ddta2, axis=1, keepdims=True)
        dx_ref[:, 0:SSD_W] = dxdt * dtx + d_ref[...] * dy
        gd_ref[...] += jnp.sum(dy * x, axis=0, keepdims=True)

    rev = lambda c: (nc - 1 - c, 0)
    rev3 = lambda c: (nc - 1 - c, 0, 0)
    return pl.pallas_call(
        body, name="ssd_bwd", grid=(nc,),
        in_specs=[pl.BlockSpec((CHUNK, XBC_W), rev),
                  pl.BlockSpec((1, CHUNK, N_HEADS), rev3),
                  pl.BlockSpec((1, N_HEADS, CHUNK), rev3),
                  pl.BlockSpec((1, N_HEADS), lambda c: (0, 0)),
                  pl.BlockSpec((N_HEADS, 1), lambda c: (0, 0)),
                  pl.BlockSpec((1, SSD_W), lambda c: (0, 0)),
                  pl.BlockSpec((1, N_GROUPS * N_STATE, gw), rev3),
                  pl.BlockSpec((CHUNK, SSD_W), rev)],
        out_specs=[pl.BlockSpec((CHUNK, XBC_W), rev),
                   pl.BlockSpec((1, CHUNK, N_HEADS), rev3),
                   pl.BlockSpec((1, N_HEADS, CHUNK), rev3),
                   pl.BlockSpec((1, N_HEADS), lambda c: (0, 0)),
                   pl.BlockSpec((N_HEADS, 1), lambda c: (0, 0)),
                   pl.BlockSpec((1, SSD_W), lambda c: (0, 0))],
        out_shape=[_SDS((lp, XBC_W), F32), _SDS((nc, CHUNK, N_HEADS), F32), _SDS((nc, N_HEADS, CHUNK), F32),
                   _SDS((1, N_HEADS), F32), _SDS((N_HEADS, 1), F32), _SDS((1, SSD_W), F32)],
        scratch_shapes=[pltpu.VMEM((N_GROUPS * N_STATE, gw), F32), pltpu.VMEM((CHUNK, SSD_W), F32),
                        pltpu.VMEM((CHUNK, SSD_W), F32), pltpu.VMEM((CHUNK, SSD_W), F32),
                        pltpu.VMEM((8, SSD_W), F32)],
        compiler_params=_params(("arbitrary",)),
    )(xbc, dt_c, dt_tc, a, a_t, dskip_x, states, d_y)


def _gated_norm(o, gate, w):
    sg = _sigmoid(gate)
    p = o * (gate * sg)
    rs = lax.rsqrt(jnp.mean(p * p, axis=-1, keepdims=True) + EPS)
    n = p * rs
    return sg, rs, n, n * w


def _tail_fwd(o_sb, o_ssd, proj, h0, target, w_out, sb_w, ssd_w, fin_w):
    lp = o_sb.shape[0]
    nb = lp // TM
    row = lambda i: (i, 0)
    one = lambda i: (0, 0)

    def body(osb_ref, gate_ref, ossd_ref, z_ref, h0_ref, tgt_ref, wo_ref, sbw_ref, ssdw_ref, fw_ref,
             dh1_ref, loss_ref, gfw_ref):
        i = pl.program_id(0)

        @pl.when(i == 0)
        def _():
            loss_ref[...] = jnp.zeros_like(loss_ref)
            gfw_ref[...] = jnp.zeros_like(gfw_ref)

        y1 = _gated_norm(osb_ref[...], gate_ref[...], sbw_ref[...])[3]
        y2 = _gated_norm(ossd_ref[...], z_ref[...], ssdw_ref[...])[3]
        h1 = (h0_ref[...] + _mm(y1, wo_ref[0:SB_W, :])) + _mm(y2, wo_ref[SB_W:SB_W + SSD_W, :])
        rs1 = lax.rsqrt(jnp.mean(h1 * h1, axis=-1, keepdims=True) + EPS)
        n1 = h1 * rs1
        fw = fw_ref[...]
        diff = jnp.where(i > 0, n1 * fw - tgt_ref[...], 0.0)
        loss_ref[...] += jnp.sum(diff * diff, axis=0, keepdims=True)
        d_out = diff * (1.0 / D_MODEL)
        gfw_ref[...] += jnp.sum(d_out * n1, axis=0, keepdims=True)
        g = d_out * fw
        dh1_ref[...] = rs1 * (g - n1 * jnp.mean(g * n1, axis=-1, keepdims=True))

    return pl.pallas_call(
        body, name="tail_fwd", grid=(nb,),
        in_specs=[pl.BlockSpec((TM, SB_W), row),
                  pl.BlockSpec((TM, SB_W), lambda i: (i, (COL_GATE - QKV_W) // SB_W)),
                  pl.BlockSpec((TM, SSD_W), row),
                  pl.BlockSpec((TM, SSD_W), lambda i: (i, (COL_Z - QKV_W) // SSD_W)),
                  pl.BlockSpec((TM, D_MODEL), row),
                  pl.BlockSpec((TM, D_MODEL), lambda i: (jnp.maximum(i - 1, 0), 0)),
                  pl.BlockSpec(memory_space=_VMEM),
                  pl.BlockSpec((1, SB_W), one), pl.BlockSpec((1, SSD_W), one), pl.BlockSpec((1, D_MODEL), one)],
        out_specs=[pl.BlockSpec((TM, D_MODEL), row), pl.BlockSpec((1, D_MODEL), one), pl.BlockSpec((1, D_MODEL), one)],
        out_shape=[_SDS((lp, D_MODEL), F32), _SDS((1, D_MODEL), F32), _SDS((1, D_MODEL), F32)],
        compiler_params=_params(("arbitrary",), 40),
    )(o_sb, proj, o_ssd, proj, h0, target, w_out, sb_w, ssd_w, fin_w)


def _gated_norm_bwd(o, gate, w, dy):
    sg, rs, n, _ = _gated_norm(o, gate, w)
    gw = jnp.sum(dy * n, axis=0, keepdims=True)
    dn = dy * w
    dp = rs * (dn - n * jnp.mean(dn * n, axis=-1, keepdims=True))
    d_o = dp * (gate * sg)
    d_gate = dp * o * (sg * (1.0 + gate * (1.0 - sg)))
    return d_o, d_gate, gw, n * w


def _tail_bwd(o_sb, o_ssd, proj, d_h1, w_out, sb_w, ssd_w):
    lp = o_sb.shape[0]
    tm = 272 if lp % 272 == 0 else TM
    nb = lp // tm
    row = lambda i, t: (i, 0)
    one = lambda i, t: (0, 0)

    def body(osb_ref, gate_ref, ossd_ref, z_ref, dh1_ref, wo_ref, sbw_ref, ssdw_ref,
             dosb_ref, dossd_ref, dproj_ref, gwo_ref, gsb_ref, gssd_ref):
        i = pl.program_id(0)
        t = pl.program_id(1)

        @pl.when(jnp.logical_and(i == 0, t == 0))
        def _():
            gwo_ref[...] = jnp.zeros_like(gwo_ref)
            gsb_ref[...] = jnp.zeros_like(gsb_ref)
            gssd_ref[...] = jnp.zeros_like(gssd_ref)

        dh1 = dh1_ref[...].astype(_MXU)

        def half(o_ref, g_ref, w_ref, do_ref, gn_ref, r0):
            dy = lax.dot_general(dh1, wo_ref[r0:r0 + SB_W, :], _NT, preferred_element_type=F32)
            d_o, d_g, gw, y = _gated_norm_bwd(o_ref[...], g_ref[...], w_ref[...], dy)
            do_ref[...] = d_o
            dproj_ref[...] = d_g.astype(_MXU)
            gn_ref[...] += gw
            gwo_ref[r0:r0 + SB_W, :] += lax.dot_general(y.astype(_MXU), dh1, _TN, preferred_element_type=F32)

        @pl.when(t == 0)
        def _():
            half(osb_ref, gate_ref, sbw_ref, dosb_ref, gsb_ref, 0)

        @pl.when(t == 1)
        def _():
            half(ossd_ref, z_ref, ssdw_ref, dossd_ref, gssd_ref, SB_W)

    tile = pl.BlockSpec((tm, SB_W), row)
    return pl.pallas_call(
        body, name="tail_bwd", grid=(nb, 2),
        in_specs=[tile, pl.BlockSpec((tm, SB_W), lambda i, t: (i, (COL_GATE - QKV_W) // SB_W)),
                  tile, pl.BlockSpec((tm, SSD_W), lambda i, t: (i, (COL_Z - QKV_W) // SSD_W)),
                  tile, pl.BlockSpec(memory_space=_VMEM),
                  pl.BlockSpec((1, SB_W), one), pl.BlockSpec((1, SSD_W), one)],
        out_specs=[tile, tile, pl.BlockSpec((tm, SB_W), lambda i, t: (i, COL_GATE // SB_W + t)),
                   pl.BlockSpec((SB_W + SSD_W, D_MODEL), one), pl.BlockSpec((1, SB_W), one), pl.BlockSpec((1, SSD_W), one)],
        out_shape=[_SDS((lp, SB_W), F32), _SDS((lp, SSD_W), F32), _SDS((lp, W_ALL), _MXU),
                   _SDS((SB_W + SSD_W, D_MODEL), F32), _SDS((1, SB_W), F32), _SDS((1, SSD_W), F32)],
        compiler_params=_params(("arbitrary", "arbitrary"), 48),
    )(o_sb, proj, o_ssd, proj, d_h1, w_out, sb_w, ssd_w)


def _d_u_norm_bwd(d_proj, w_t, h0, d_h1, norm_w, send=(), dests=()):
    lp = d_proj.shape[0]
    nb = lp // TM
    seq = lp - OFF
    tk = 512
    ksteps = N_MAIN // tk
    n = len(send)
    host_in, host_out, host_shapes, host_sems = _host_specs(send)

    def body(dp_ref, w_ref, dpdt_ref, wdt_ref, h0_ref, dh1_ref, nw_ref, *rest):
        srcs, (gx_ref, gmeta_ref, gnw_ref), dsts = rest[:n], rest[n:n + 3], rest[n + 3:2 * n + 3]
        acc, sems = rest[2 * n + 3], rest[2 * n + 4:]
        j = pl.program_id(0)
        i = j - ksteps

        @pl.when(j == 0)
        def _():
            if n:
                _start_copies(_slab_copies(srcs, dsts, dests, *sems))
            acc[...] = jnp.dot(dpdt_ref[...], wdt_ref[...], preferred_element_type=F32)
            gnw_ref[...] = jnp.zeros_like(gnw_ref)

        @pl.when(j < ksteps)
        def _():
            acc[...] += jnp.dot(dp_ref[...], w_ref[...], preferred_element_type=F32)

        @pl.when(i >= 0)
        def _():
            du = acc[pl.ds(pl.multiple_of(i * TM, TM), TM), :]
            h = h0_ref[...]
            rs = lax.rsqrt(jnp.mean(h * h, axis=-1, keepdims=True) + EPS)
            n0 = h * rs
            gnw_ref[...] += jnp.sum(du * n0, axis=0, keepdims=True)
            g = du * nw_ref[...]
            dh0 = dh1_ref[...] + rs * (g - n0 * jnp.mean(g * n0, axis=-1, keepdims=True))

            @pl.when(i == 0)
            def _():
                gmeta_ref[...] = dh0[PAD:PAD + N_META, :]

            @pl.when(i > 0)
            def _():
                gx_ref[...] = dh0

        if n:
            @pl.when(j == ksteps + nb - 1)
            def _():
                _finish_copies(_slab_copies(srcs, dsts, dests, *sems))

    last_k = ksteps - 1
    tile = pl.BlockSpec((TM, D_MODEL), lambda j: (jnp.maximum(j - ksteps, 0), 0))
    one = lambda j: (0, 0)
    res = pl.pallas_call(
        body, name="d_u_norm_bwd", grid=(ksteps + nb,),
        in_specs=[pl.BlockSpec((lp, tk), lambda j: (0, jnp.minimum(j, last_k))),
                  pl.BlockSpec((tk, D_MODEL), lambda j: (jnp.minimum(j, last_k), 0)),
                  pl.BlockSpec((lp, 128), lambda j: (0, N_MAIN // 128)),
                  pl.BlockSpec((128, D_MODEL), lambda j: (N_MAIN // 128, 0)),
                  tile, tile, pl.BlockSpec((1, D_MODEL), one)] + host_in,
        out_specs=[pl.BlockSpec((TM, D_MODEL), lambda j: (jnp.maximum(j - ksteps - 1, 0), 0)),
                   pl.BlockSpec((N_META, D_MODEL), one), pl.BlockSpec((1, D_MODEL), one)] + host_out,
        out_shape=[_SDS((seq, D_MODEL), F32), _SDS((N_META, D_MODEL), F32), _SDS((1, D_MODEL), F32)] + host_shapes,
        scratch_shapes=[pltpu.VMEM((lp, D_MODEL), F32)] + host_sems,
        compiler_params=_params(("arbitrary",), 48),
    )(d_proj, w_t, d_proj, w_t, h0, d_h1, norm_w, *send)
    return res[0], res[1], res[2], list(res[3:])


def _grad_w_windows(u_t, d_proj, first, count, name):
    lp = d_proj.shape[0]
    hw = WIN_W // 2
    steps = 2 * count

    def body(ut_ref, dp_hbm, o_ref, buf, sems):
        s = pl.program_id(0)
        slot = s % 2

        def fetch(step, sl):
            start = pl.multiple_of((first + step // 2) * WIN_STEP + (step % 2) * hw, 128)
            return pltpu.make_async_copy(dp_hbm.at[:, pl.ds(start, hw)], buf.at[sl], sems.at[sl])

        @pl.when(s == 0)
        def _():
            fetch(0, 0).start()

        @pl.when(s + 1 < steps)
        def _():
            fetch(s + 1, 1 - slot).start()

        fetch(s, slot).wait()
        o_ref[0] = jnp.dot(ut_ref[...], buf[slot], preferred_element_type=F32).astype(o_ref.dtype)

    return pl.pallas_call(
        body, name=name, grid=(steps,),
        in_specs=[pl.BlockSpec((D_MODEL, lp), lambda s: (0, 0)), pl.BlockSpec(memory_space=pl.ANY)],
        out_specs=pl.BlockSpec((1, D_MODEL, hw), lambda s: (s // 2, 0, s % 2)),
        out_shape=_SDS((count, D_MODEL, WIN_W), _MXU),
        scratch_shapes=[pltpu.VMEM((2, lp, hw), _MXU), pltpu.SemaphoreType.DMA((2,))],
        compiler_params=_params(("arbitrary",), 40),
    )(u_t, d_proj)


def _device_grads(x2d, target2d, meta_full, norm_w, w_t, conv_w, conv_b, dt_bias, a_log, d_skip,
                  sb_w, ssd_w, w_out, fin_w, exchange=None, w_out_shard=None):
    lp = x2d.shape[0] + OFF
    nc = lp // CHUNK
    h0, u, u_t = _prep(x2d, meta_full, norm_w)
    qkv, proj, dt_raw = _inproj(u, w_t)
    if w_out is None:
        o_sb, o_lo, (w_out_shards,) = _sb_fwd(qkv, (w_out_shard,))
        w_out = w_out_shards.reshape(2 * D_MODEL, D_MODEL)
    else:
        o_sb, o_lo, _ = _sb_fwd(qkv)
    dt_bias128 = jnp.pad(dt_bias, ((0, 0), (0, 128 - N_HEADS)))
    xbc, dt128 = _conv_fwd(proj, dt_raw, conv_w, conv_b, dt_bias128)
    dt_c = dt128[:, :N_HEADS].reshape(nc, CHUNK, N_HEADS)
    dt_tc = jnp.swapaxes(dt_c, 1, 2)
    a = -jnp.exp(a_log)
    a_t = a.reshape(N_HEADS, 1)
    dskip_x = jnp.repeat(d_skip, HEAD, axis=1)
    o_ssd, states = _ssd_fwd(xbc, dt_c, dt_tc, a, a_t, dskip_x)
    d_h1, sq_err, g_fin = _tail_fwd(o_sb, o_ssd, proj, h0, target2d, w_out, sb_w, ssd_w, fin_w)

    d_osb, d_ossd, d_proj, g_wout, g_sb, g_ssd = _tail_bwd(o_sb, o_ssd, proj, d_h1, w_out, sb_w, ssd_w)
    d_xbc_act, ddt_a, ddt_b, ga1, ga2, gd = _ssd_bwd(xbc, dt_c, dt_tc, a, a_t, dskip_x, states, d_ossd)
    d_dt = (ddt_a + jnp.swapaxes(ddt_b, 1, 2)).reshape(lp, N_HEADS)
    d_dt128 = jnp.pad(d_dt, ((0, 0), (0, 128 - N_HEADS)))
    d_proj, g_convw, g_convb, g_dtb128 = _conv_bwd(proj, dt_raw, conv_w, conv_b, dt_bias128, d_xbc_act, d_dt128, d_proj)
    send_e, dests_e = ((), ()) if exchange is None else exchange["early"](g_wout)
    d_proj, arrived_e = _sb_bwd(qkv, o_sb, o_lo, d_osb, d_proj, send_e, dests_e)
    g_win = _grad_w_windows(u_t, d_proj, 0, N_CHIPS, "grad_w_in")
    send_l, dests_l = ((), ()) if exchange is None else exchange["late"](g_win)
    g_x, g_meta, g_nw, arrived_l = _d_u_norm_bwd(d_proj, w_t, h0, d_h1, norm_w, send_l, dests_l)
    send, arrived = tuple(send_e) + tuple(send_l), tuple(arrived_e) + tuple(arrived_l)
    g_alog = (ga1 + ga2.reshape(1, N_HEADS)) * a
    g_dskip = gd.reshape(N_HEADS, HEAD).sum(axis=1).reshape(1, N_HEADS)
    grads = dict(meta_tokens=g_meta, norm_w=g_nw, w_in=g_win, conv_w=g_convw, conv_b=g_convb,
                 dt_bias=g_dtb128[:, :N_HEADS], a_log=g_alog, d_skip=g_dskip, sb_norm_w=g_sb, ssd_norm_w=g_ssd,
                 w_out=g_wout, final_norm_w=g_fin, sent=send, arrived=arrived)
    return sq_err, g_x, grads


_MESH = pl.DeviceIdType.MESH
_ANY = pl.BlockSpec(memory_space=pl.ANY)


def _place():
    return lax.axis_index("x"), lax.axis_index("y"), lax.axis_index("c")


def _other_chips(x, y):
    return ((1 - x, y), (x, 1 - y), (1 - x, 1 - y))


def _gather_copies(srcs, dsts, n_big, send_sems, recv_sems, fwd_send, fwd_recv):
    x, y, c = _place()
    mine = 2 * x + y
    first, passed = [], []
    for a in range(len(srcs)):
        half = srcs[a].shape[1] // 2
        window = pl.ds(pl.multiple_of(c * half, 128), half)
        for k, (px, py) in enumerate(_other_chips(x, y)):
            if a < n_big:
                src, dst = srcs[a].at[:, window], dsts[a].at[mine, :, window]
                landed = dsts[a].at[2 * px + py, :, window]
                passed.append(pltpu.make_async_remote_copy(
                    src_ref=landed, dst_ref=landed, send_sem=fwd_send.at[a * 3 + k], recv_sem=fwd_recv.at[a * 3 + k],
                    device_id=(x, y, 1 - c), device_id_type=_MESH))
            else:
                src, dst = srcs[a], dsts[a].at[mine]
                passed.append(None)
            first.append(pltpu.make_async_remote_copy(
                src_ref=src, dst_ref=dst, send_sem=send_sems.at[a * 3 + k], recv_sem=recv_sems.at[a * 3 + k],
                device_id=(px, py, c), device_id_type=_MESH))
    return first, passed


def _gather_finish(first, passed):
    for cp, fwd in zip(first, passed):
        cp.wait_recv()
        if fwd is not None:
            fwd.start()
    for fwd in passed:
        if fwd is not None:
            fwd.wait_recv()
    for cp in first + [fwd for fwd in passed if fwd is not None]:
        cp.wait_send()


def _gather_specs(arrays, n_big):
    n = len(arrays)
    hbm = [pl.BlockSpec(memory_space=pl.ANY)] * n
    shapes = [_SDS((N_CHIPS,) + a.shape, a.dtype) for a in arrays]
    sems = [pltpu.SemaphoreType.DMA((3 * n,)), pltpu.SemaphoreType.DMA((3 * n,)),
            pltpu.SemaphoreType.DMA((3 * max(n_big, 1),)), pltpu.SemaphoreType.DMA((3 * max(n_big, 1),))] if n else []
    return hbm, hbm, shapes, sems


def _own_slot(got, arrays):
    if not arrays:
        return []
    mine = 2 * lax.axis_index("x") + lax.axis_index("y")
    return [lax.dynamic_update_slice(g, a[None], (mine,) + (0,) * a.ndim) for g, a in zip(got, arrays)]


W_SLAB = WIN_STEP + 16
W_SLAB_END = 48


def _assemble_w(slabs):
    tail = W_IN_SHARD - (W_SLAB - W_SLAB_END)

    def body(s_ref, t_ref, o_ref):
        j = pl.program_id(0)

        @pl.when(j == 0)
        def _():
            o_ref[...] = s_ref[0, 0:WIN_STEP, :]

        for k in range(1, N_CHIPS):
            @pl.when(j == k)
            def _(k=k):
                o_ref[0:4 * k, :] = t_ref[0, tail - 4 * k:tail, :]
                o_ref[4 * k:WIN_STEP, :] = s_ref[0, 0:WIN_STEP - 4 * k, :]

        @pl.when(j == N_CHIPS)
        def _():
            left = D_IN - N_CHIPS * WIN_STEP
            o_ref[...] = jnp.zeros_like(o_ref)
            o_ref[0:left, :] = t_ref[0, tail - left:tail, :]

    return pl.pallas_call(
        body, name="assemble_w", grid=(N_CHIPS + 1,),
        in_specs=[pl.BlockSpec((1, W_SLAB, D_MODEL), lambda j: (jnp.minimum(j, N_CHIPS - 1), 0, 0)),
                  pl.BlockSpec((1, W_SLAB_END, D_MODEL),
                               lambda j: (jnp.maximum(j - 1, 0), W_SLAB // W_SLAB_END - 1, 0))],
        out_specs=pl.BlockSpec((WIN_STEP, D_MODEL), lambda j: (j, 0)),
        out_shape=_SDS((W_ALL, D_MODEL), slabs.dtype),
        compiler_params=_params(("arbitrary",), 40),
    )(slabs, slabs)


def _gather_shards(arrays, n_big):
    n = len(arrays)

    def body(*refs):
        first, passed = _gather_copies(refs[:n], refs[n:2 * n], n_big, *refs[2 * n:])
        for cp in first:
            cp.start()
        _gather_finish(first, passed)

    hbm_in, hbm_out, shapes, sems = _gather_specs(arrays, n_big)
    got = pl.pallas_call(
        body, name="gather_shards", in_specs=hbm_in, out_specs=hbm_out, out_shape=shapes, scratch_shapes=sems,
    )(*arrays)
    return _own_slot(got, arrays)


def _slab_copies(srcs, dsts, dests, send_sems, recv_sems):
    x, y, c = _place()
    mine = 2 * x + y
    copies = []
    for a in range(len(srcs)):
        lo, hi = dests[a]
        receives = jnp.logical_and(mine >= lo, mine < hi)
        for k, (px, py) in enumerate(_other_chips(x, y)):
            target = 2 * px + py
            cp = pltpu.make_async_remote_copy(
                src_ref=srcs[a].at[jnp.clip(target - lo, 0, hi - lo - 1)], dst_ref=dsts[a].at[mine],
                send_sem=send_sems.at[a * 3 + k], recv_sem=recv_sems.at[a * 3 + k],
                device_id=(px, py, c), device_id_type=_MESH)
            copies.append((cp, jnp.logical_and(target >= lo, target < hi), receives))
    return copies


def _start_copies(copies):
    for cp, sends, _ in copies:
        pl.when(sends)(cp.start)


def _finish_copies(copies):
    for cp, _, receives in copies:
        pl.when(receives)(cp.wait_recv)
    for cp, sends, _ in copies:
        pl.when(sends)(cp.wait_send)


def _host_specs(send):
    n = len(send)
    hbm = [pl.BlockSpec(memory_space=pl.ANY)] * n
    shapes = [_SDS((N_CHIPS,) + a.shape[1:], a.dtype) for a in send]
    sems = [pltpu.SemaphoreType.DMA((3 * n,)), pltpu.SemaphoreType.DMA((3 * n,))] if n else []
    return hbm, hbm, shapes, sems


def _swap_halves(arrays, name):
    n = len(arrays)

    def body(*refs):
        srcs, dsts = refs[:n], refs[n:2 * n]
        send_sems, recv_sems = refs[2 * n:]
        x, y, c = _place()
        copies = []
        for a in range(n):
            half = arrays[a].shape[1] // 2
            cp = pltpu.make_async_remote_copy(
                src_ref=srcs[a].at[:, pl.ds(pl.multiple_of((1 - c) * half, 16), half)], dst_ref=dsts[a],
                send_sem=send_sems.at[a], recv_sem=recv_sems.at[a],
                device_id=(x, y, 1 - c), device_id_type=_MESH)
            cp.start()
            copies.append(cp)
        for cp in copies:
            cp.wait_recv()
        for cp in copies:
            cp.wait_send()

    return pl.pallas_call(
        body, name=name,
        in_specs=[_ANY] * n, out_specs=[_ANY] * n,
        out_shape=[_SDS((a.shape[0], a.shape[1] // 2, a.shape[2]), a.dtype) for a in arrays],
        scratch_shapes=[pltpu.SemaphoreType.DMA((n,)), pltpu.SemaphoreType.DMA((n,))],
    )(*arrays)


N_DEV = 8
SMALL_ROWS = 32
SMALL_COLS = XBC_W


def _final_exchange(arrays, by_cols, packed):
    n = len(arrays)

    def body(*refs):
        src_ref = refs[n]
        dsts = refs[n + 1:2 * n + 1]
        dst_ref = refs[2 * n + 1]
        send_sems, recv_sems, all_send, all_recv, local_sem = refs[2 * n + 2:]
        x, y, c = _place()
        me = 4 * x + 2 * y + c
        own = pltpu.make_async_copy(src_ref, dst_ref.at[me], local_sem)
        own.start()
        copies = []
        for k in range(1, N_DEV):
            bx, by, bc = (k >> 2) & 1, (k >> 1) & 1, k & 1
            peer = (x + bx - 2 * x * bx, y + by - 2 * y * by, c + bc - 2 * c * bc)
            cp = pltpu.make_async_remote_copy(
                src_ref=src_ref, dst_ref=dst_ref.at[me], send_sem=all_send.at[k - 1], recv_sem=all_recv.at[k - 1],
                device_id=peer, device_id_type=_MESH)
            cp.start()
            copies.append(cp)
        for a in range(n):
            if by_cols[a]:
                half = arrays[a].shape[1] // 2
                mine = dsts[a].at[:, pl.ds(pl.multiple_of(c * half, 128), half)]
            else:
                half = arrays[a].shape[0] // 2
                mine = dsts[a].at[pl.ds(pl.multiple_of(c * half, 16), half)]
            cp = pltpu.make_async_remote_copy(
                src_ref=mine, dst_ref=mine, send_sem=send_sems.at[a], recv_sem=recv_sems.at[a],
                device_id=(x, y, 1 - c), device_id_type=_MESH)
            cp.start()
            copies.append(cp)
        for cp in copies:
            cp.wait_recv()
        for cp in copies:
            cp.wait_send()
        own.wait()

    vm = pl.BlockSpec(memory_space=_VMEM)
    res = pl.pallas_call(
        body, name="final_exchange",
        in_specs=[_ANY] * n + [vm], out_specs=[_ANY] * n + [vm],
        out_shape=[_SDS(a.shape, a.dtype) for a in arrays] + [_SDS((N_DEV, SMALL_ROWS, SMALL_COLS), F32)],
        input_output_aliases={a: a for a in range(n)},
        scratch_shapes=[pltpu.SemaphoreType.DMA((n,)), pltpu.SemaphoreType.DMA((n,)),
                        pltpu.SemaphoreType.DMA((N_DEV - 1,)), pltpu.SemaphoreType.DMA((N_DEV - 1,)),
                        pltpu.SemaphoreType.DMA],
    )(*arrays, packed)
    return list(res[:n]), res[n]


def _adamw(w, g, m, v):
    m = ADAM_B1 * m + (1.0 - ADAM_B1) * g
    v = ADAM_B2 * v + (1.0 - ADAM_B2) * (g * g)
    m_hat = m / (1.0 - ADAM_B1 ** ADAM_STEP)
    v_hat = v / (1.0 - ADAM_B2 ** ADAM_STEP)
    delta = -ADAM_LR * (m_hat / (jnp.sqrt(v_hat) + ADAM_EPS) + ADAM_WD * w)
    return delta, m, v


def _sum_slabs(slabs, core, name, transposed=False):
    _, h, c = slabs.shape
    tr = 128
    nblk = h // tr

    def body(core_ref, s_ref, o_ref):
        tot = ((s_ref[0].astype(F32) + s_ref[1].astype(F32)) + s_ref[2].astype(F32)) + s_ref[3].astype(F32)
        o_ref[...] = tot.T if transposed else tot

    if transposed:
        out_spec = pl.BlockSpec((c, tr), lambda i, core_ref: (0, core_ref[0] * nblk + i))
        out_shape = _SDS((c, 2 * h), F32)
    else:
        out_spec = pl.BlockSpec((tr, c), lambda i, core_ref: (core_ref[0] * nblk + i, 0))
        out_shape = _SDS((2 * h, c), F32)
    grid_spec = pltpu.PrefetchScalarGridSpec(
        num_scalar_prefetch=1, grid=(nblk,),
        in_specs=[pl.BlockSpec((N_CHIPS, tr, c), lambda i, core_ref: (0, i, 0))],
        out_specs=out_spec)
    return pl.pallas_call(
        body, name=name, grid_spec=grid_spec, out_shape=out_shape,
        compiler_params=_params(("arbitrary",)),
    )(core, slabs)


def _add_halves(own, recv, core, name):
    ns, r, c = own.shape
    half = r // 2
    tr = 128
    nblk = half // tr

    def body(core_ref, a_ref, b_ref, o_ref):
        o_ref[...] = (a_ref[...].astype(F32) + b_ref[...].astype(F32)).astype(o_ref.dtype)

    grid_spec = pltpu.PrefetchScalarGridSpec(
        num_scalar_prefetch=1, grid=(nblk,),
        in_specs=[pl.BlockSpec((ns, tr, c), lambda i, core_ref: (0, core_ref[0] * nblk + i, 0)),
                  pl.BlockSpec((ns, tr, c), lambda i, core_ref: (0, i, 0))],
        out_specs=pl.BlockSpec((ns, tr, c), lambda i, core_ref: (0, i, 0)))
    return pl.pallas_call(
        body, name=name, grid_spec=grid_spec, out_shape=_SDS((ns, half, c), own.dtype),
        compiler_params=_params(("arbitrary",)),
    )(core, own, recv)


def _update_big(w, m, v, g, name):
    r, c = w.shape

    def body(w_ref, m_ref, v_ref, g_ref, d_ref, mo_ref, vo_ref):
        delta, m_new, v_new = _adamw(w_ref[...], g_ref[...], m_ref[...], v_ref[...])
        d_ref[...] = delta
        mo_ref[...] = m_new
        vo_ref[...] = v_new

    if r % 128 == 0:
        steps, spec = r // 128, pl.BlockSpec((128, c), lambda i: (i, 0))
    else:
        steps, spec = c // 128, pl.BlockSpec((r, 128), lambda i: (0, i))
    return pl.pallas_call(
        body, name=name, grid=(steps,),
        in_specs=[spec] * 4, out_specs=[spec] * 3,
        out_shape=[_SDS((r, c), F32)] * 3,
        compiler_params=_params(("arbitrary",)),
    )(w, m, v, g)


_ROW = dict(norm_w=0, sb_norm_w=1, ssd_norm_w=2, final_norm_w=3, conv_b=4, dt_bias=5, a_log=6, d_skip=7,
            conv_w=8, sq_err=12, meta_tokens=16)
_SMALL = ("meta_tokens", "norm_w", "conv_w", "conv_b", "dt_bias", "a_log", "d_skip", "sb_norm_w", "ssd_norm_w",
          "final_norm_w")


def _pack_small(sq_err, grads):
    def rowpad(a):
        return jnp.pad(a, ((0, 0), (0, SMALL_COLS - a.shape[1])))

    rows = [rowpad(grads[k]) for k in ("norm_w", "sb_norm_w", "ssd_norm_w", "final_norm_w", "conv_b", "dt_bias", "a_log", "d_skip")]
    rows.append(grads["conv_w"])
    rows.append(rowpad(sq_err))
    rows.append(jnp.zeros((3, SMALL_COLS), F32))
    rows.append(rowpad(grads["meta_tokens"]))
    return jnp.concatenate(rows, axis=0)


def _update_small(gathered, ws, ms, vs):
    names = _SMALL
    n = len(names)

    def body(*refs):
        g_ref = refs[0]
        w_refs, m_refs, v_refs = refs[1:1 + n], refs[1 + n:1 + 2 * n], refs[1 + 2 * n:1 + 3 * n]
        outs = refs[1 + 3 * n:]
        loss_ref = outs[0]
        go, do, mo, vo = outs[1:1 + n], outs[1 + n:1 + 2 * n], outs[1 + 2 * n:1 + 3 * n], outs[1 + 3 * n:1 + 4 * n]
        tot = g_ref[0]
        for d in range(1, N_DEV):
            tot = tot + g_ref[d]
        x, y, _ = _place()
        chip = 2 * x + y
        loss_ref[...] = jnp.broadcast_to(
            0.5 * jnp.sum(tot[_ROW["sq_err"]:_ROW["sq_err"] + 1, 0:D_MODEL], axis=1, keepdims=True) / D_MODEL, (1, 128))
        for idx, nm in enumerate(names):
            r0 = _ROW[nm]
            rows, cols = w_refs[idx].shape
            if nm in ("conv_w", "meta_tokens"):
                g = jnp.zeros((rows, cols), F32)
                for j in range(N_CHIPS):
                    g = g + jnp.where(chip == j, tot[r0:r0 + rows, j * cols:(j + 1) * cols], 0.0)
            else:
                g = tot[r0:r0 + rows, 0:cols]
            delta, m_new, v_new = _adamw(w_refs[idx][...], g, m_refs[idx][...], v_refs[idx][...])
            go[idx][...] = g
            do[idx][...] = delta
            mo[idx][...] = m_new
            vo[idx][...] = v_new

    shapes = [_SDS(ws[nm].shape, F32) for nm in names]
    vm = pl.BlockSpec(memory_space=_VMEM)
    res = pl.pallas_call(
        body, name="update_small",
        in_specs=[vm] * (1 + 3 * n), out_specs=[vm] * (1 + 4 * n),
        out_shape=[_SDS((1, 128), F32)] + shapes * 4,
    )(gathered, *[ws[nm] for nm in names], *[ms[nm] for nm in names], *[vs[nm] for nm in names])
    loss = res[0][0, 0]
    g = dict(zip(names, res[1:1 + n]))
    d = dict(zip(names, res[1 + n:1 + 2 * n]))
    m = dict(zip(names, res[1 + 2 * n:1 + 3 * n]))
    v = dict(zip(names, res[1 + 3 * n:1 + 4 * n]))
    return loss, g, d, m, v


_WEIGHTS = ("meta_tokens", "norm_w", "w_in", "conv_w", "conv_b", "dt_bias", "a_log", "d_skip", "sb_norm_w",
            "ssd_norm_w", "w_out", "final_norm_w")


def kernel(x, meta_tokens, norm_w, w_in, conv_w, conv_b, dt_bias, a_log, d_skip, sb_norm_w, ssd_norm_w, w_out, final_norm_w, loss_target, m_meta_tokens, m_norm_w, m_w_in, m_conv_w, m_conv_b, m_dt_bias, m_a_log, m_d_skip, m_sb_norm_w, m_ssd_norm_w, m_w_out, m_final_norm_w, v_meta_tokens, v_norm_w, v_w_in, v_conv_w, v_conv_b, v_dt_bias, v_a_log, v_d_skip, v_sb_norm_w, v_ssd_norm_w, v_w_out, v_final_norm_w):
    given = dict(meta_tokens=meta_tokens, norm_w=norm_w, w_in=w_in, conv_w=conv_w, conv_b=conv_b, dt_bias=dt_bias,
                 a_log=a_log, d_skip=d_skip, sb_norm_w=sb_norm_w, ssd_norm_w=ssd_norm_w, w_out=w_out,
                 final_norm_w=final_norm_w)
    mom = dict(meta_tokens=m_meta_tokens, norm_w=m_norm_w, w_in=m_w_in, conv_w=m_conv_w, conv_b=m_conv_b,
               dt_bias=m_dt_bias, a_log=m_a_log, d_skip=m_d_skip, sb_norm_w=m_sb_norm_w, ssd_norm_w=m_ssd_norm_w,
               w_out=m_w_out, final_norm_w=m_final_norm_w)
    var = dict(meta_tokens=v_meta_tokens, norm_w=v_norm_w, w_in=v_w_in, conv_w=v_conv_w, conv_b=v_conv_b,
               dt_bias=v_dt_bias, a_log=v_a_log, d_skip=v_d_skip, sb_norm_w=v_sb_norm_w, ssd_norm_w=v_ssd_norm_w,
               w_out=v_w_out, final_norm_w=v_final_norm_w)
    seq = x.shape[1]

    def two_d(a):
        return a.reshape((-1, a.shape[-1])) if a.ndim != 2 else a

    def rows_first(a):
        return jnp.transpose(a, (2, 0, 1)).reshape(W_IN_SHARD, D_MODEL)

    def rows_last(a):
        return jnp.transpose(a.reshape(W_IN_SHARD, 1, D_MODEL), (1, 2, 0))

    w_in_t, m_in_t, v_in_t = rows_first(w_in), rows_first(m_w_in), rows_first(v_w_in)

    chip = 2 * lax.axis_index("x") + lax.axis_index("y")
    slab = jnp.pad(w_in_t.astype(_MXU), ((0, W_SLAB - W_IN_SHARD), (0, 0)))
    g_win, g_meta, g_cw = _gather_shards([slab, meta_tokens, conv_w[0]], 1)
    w_t = _assemble_w(g_win)
    meta_full = jnp.swapaxes(g_meta, 0, 1).reshape(N_META, D_MODEL)
    conv_w_full = jnp.swapaxes(g_cw, 0, 1).reshape(4, XBC_W)

    core = lax.axis_index("c").astype(jnp.int32).reshape(1)

    def early(g_wout):
        slab_out = g_wout.reshape(N_CHIPS, W_OUT_SHARD, D_MODEL).astype(_MXU)
        (sib_out,) = _swap_halves([slab_out], "swap_halves_w_out")
        return (_add_halves(slab_out, sib_out, core, "chip_sum_w_out"),), ((0, N_CHIPS),)

    def late(g_win):
        (sib_in,) = _swap_halves([g_win], "swap_halves_w_in")
        return (_add_halves(g_win, sib_in, core, "chip_sum_w_in"),), ((0, N_CHIPS),)

    sq_err, g_x, grads = _device_grads(
        x.reshape(seq, D_MODEL), loss_target.reshape(seq, D_MODEL), meta_full, norm_w, w_t, conv_w_full,
        conv_b, dt_bias, a_log, d_skip, sb_norm_w, ssd_norm_w, None, final_norm_w.reshape(1, D_MODEL),
        exchange=dict(early=early, late=late), w_out_shard=w_out[0].astype(_MXU))
    chip_out, chip_in = grads["sent"]
    got_out, got_in = grads["arrived"]

    def with_own(got, sent):
        own = lax.dynamic_slice(sent, (chip, 0, 0), (1,) + sent.shape[1:])
        return lax.dynamic_update_slice(got, own, (chip, 0, 0))

    (g_in, g_out), gathered = _final_exchange(
        [_sum_slabs(with_own(got_in, chip_in), core, "sum_w_in", transposed=True),
         _sum_slabs(with_own(got_out, chip_out), core, "sum_w_out")], (True, False), _pack_small(sq_err, grads))
    g_in = lax.dynamic_slice(g_in, (4 * chip, 0), (W_IN_SHARD, D_MODEL))
    big = dict(w_in=tuple(rows_last(a) for a in (g_in,) + tuple(_update_big(w_in_t, m_in_t, v_in_t, g_in, "update_w_in"))),
               w_out=(g_out,) + tuple(_update_big(w_out[0], m_w_out[0], v_w_out[0], g_out, "update_w_out")))

    loss, sg, sd, sm, sv = _update_small(
        gathered, {k: two_d(given[k]) for k in _SMALL}, {k: two_d(mom[k]) for k in _SMALL},
        {k: two_d(var[k]) for k in _SMALL})

    out = {}
    for idx, group in enumerate((sg, sd, sm, sv)):
        for k in _SMALL:
            out[(idx, k)] = group[k].reshape(given[k].shape)
        for k in ("w_in", "w_out"):
            out[(idx, k)] = big[k][idx].reshape(given[k].shape)
    return (loss, g_x.reshape(x.shape), *[out[(idx, k)] for idx in range(4) for k in _WEIGHTS])
```
